```python
import math
import jax, jax.numpy as jnp
from jax import lax
import numpy as np

D_MODEL = 2048
BATCH = 8
SEQ = 2048
DEPTH = 4

N_MIXERS = 2
N_SSD_LAYERS = (DEPTH + 1) // 2
N_SB_LAYERS = DEPTH // 2
PLE_DIM = 256
SSD_EXPAND = 2
SSD_D_INNER = SSD_EXPAND * D_MODEL
SSD_HEAD_DIM = 64
SSD_N_HEADS = SSD_D_INNER // SSD_HEAD_DIM
SSD_N_GROUPS = 8
SSD_HEADS_PER_GROUP = SSD_N_HEADS // SSD_N_GROUPS
SSD_D_STATE = 128
SSD_D_CONV = 4
SSD_CHUNK = 128
SSD_CONV_DIM = SSD_D_INNER + 2 * SSD_N_GROUPS * SSD_D_STATE
SSD_IN_DIM = SSD_D_INNER + SSD_CONV_DIM + SSD_N_HEADS
SB_HEAD_DIM = 128
SB_N_HEADS = D_MODEL // SB_HEAD_DIM
SB_WIDTH = SB_N_HEADS * SB_HEAD_DIM
SB_QBLOCK = 128
NORM_EPS = 1e-6
GATED_NORM_EPS = 1e-5

kernel_name = "ssd_stickbreaking_interleaved_ple"


def rms_norm(x, w, eps=NORM_EPS):
    xf = x.astype(jnp.float32)
    y = xf * lax.rsqrt(jnp.mean(xf * xf, axis=-1, keepdims=True) + eps)
    return (y * w.astype(jnp.float32)).astype(x.dtype)


def causal_depthwise_conv(x, w, b):
    y = lax.conv_general_dilated(
        x, w[:, None, :], window_strides=(1,), padding=[(SSD_D_CONV - 1, 0)],
        dimension_numbers=("NWC", "WIO", "NWC"), feature_group_count=x.shape[-1])
    return y + b


def ssd_chunked_scan(xh, dt, a, bm, cm):
    b, s = xh.shape[0], xh.shape[1]
    nc = s // SSD_CHUNK
    L = SSD_CHUNK
    xdt = (xh * dt[..., None]).reshape(b, nc, L, SSD_N_GROUPS, SSD_HEADS_PER_GROUP, SSD_HEAD_DIM)
    adt = (dt * a).reshape(b, nc, L, SSD_N_GROUPS, SSD_HEADS_PER_GROUP)
    bm = bm.reshape(b, nc, L, SSD_N_GROUPS, SSD_D_STATE)
    cm = cm.reshape(b, nc, L, SSD_N_GROUPS, SSD_D_STATE)
    acum = jnp.cumsum(adt, axis=2)
    seg = acum[:, :, :, None] - acum[:, :, None, :]
    causal = jnp.tril(jnp.ones((L, L), dtype=bool))[None, None, :, :, None, None]
    decay = jnp.exp(jnp.where(causal, seg, -jnp.inf))
    scores = jnp.einsum("bclgn,bcsgn->bclsg", cm, bm)
    y_diag = jnp.einsum("bclsg,bclsgr,bcsgrp->bclgrp", scores, decay, xdt)
    decay_to_end = jnp.exp(acum[:, :, -1:] - acum)
    chunk_states = jnp.einsum("bclgn,bclgr,bclgrp->bcgrpn", bm, decay_to_end, xdt)
    chunk_decay = jnp.exp(acum[:, :, -1])

    def step(state, inp):
        cs, cd = inp
        return state * cd[..., None, None] + cs, state

    init = jnp.zeros((b, SSD_N_GROUPS, SSD_HEADS_PER_GROUP, SSD_HEAD_DIM, SSD_D_STATE), jnp.float32)
    _, prev_states = lax.scan(step, init, (jnp.moveaxis(chunk_states, 1, 0), jnp.moveaxis(chunk_decay, 1, 0)))
    prev_states = jnp.moveaxis(prev_states, 0, 1)
    y_off = jnp.einsum("bclgn,bcgrpn,bclgr->bclgrp", cm, prev_states, jnp.exp(acum))
    return (y_diag + y_off).reshape(b, s, SSD_N_GROUPS, SSD_HEADS_PER_GROUP, SSD_HEAD_DIM)


def ssd_branch(u, in_w, conv_w, conv_b, dt_bias, a_log, d_skip, gnorm_w, out_w):
    b, s, _ = u.shape
    proj = u @ in_w
    z = proj[..., :SSD_D_INNER]
    xbc = proj[..., SSD_D_INNER:SSD_D_INNER + SSD_CONV_DIM]
    dt_raw = proj[..., SSD_D_INNER + SSD_CONV_DIM:]
    xbc = jax.nn.silu(causal_depthwise_conv(xbc, conv_w, conv_b))
    nbc = SSD_N_GROUPS * SSD_D_STATE
    xs = xbc[..., :SSD_D_INNER].astype(jnp.float32).reshape(b, s, SSD_N_GROUPS, SSD_HEADS_PER_GROUP, SSD_HEAD_DIM)
    bm = xbc[..., SSD_D_INNER:SSD_D_INNER + nbc].astype(jnp.float32).reshape(b, s, SSD_N_GROUPS, SSD_D_STATE)
    cm = xbc[..., SSD_D_INNER + nbc:].astype(jnp.float32).reshape(b, s, SSD_N_GROUPS, SSD_D_STATE)
    dt = jax.nn.softplus(dt_raw.astype(jnp.float32) + dt_bias.astype(jnp.float32))
    dt = dt.reshape(b, s, SSD_N_GROUPS, SSD_HEADS_PER_GROUP)
    a = (-jnp.exp(a_log.astype(jnp.float32))).reshape(SSD_N_GROUPS, SSD_HEADS_PER_GROUP)
    y = ssd_chunked_scan(xs, dt, a, bm, cm)
    y = y + d_skip.astype(jnp.float32).reshape(SSD_N_GROUPS, SSD_HEADS_PER_GROUP)[..., None] * xs
    y = y.reshape(b, s, SSD_D_INNER) * jax.nn.silu(z.astype(jnp.float32))
    yg = y.reshape(b, s, SSD_N_GROUPS, SSD_D_INNER // SSD_N_GROUPS)
    yg = yg * lax.rsqrt(jnp.mean(yg * yg, axis=-1, keepdims=True) + GATED_NORM_EPS)
    y = (yg.reshape(b, s, SSD_D_INNER) * gnorm_w.astype(jnp.float32)).astype(u.dtype)
    return y @ out_w


def stick_breaking_attention(q, k, v):
    s = q.shape[2]
    scale = 1.0 / math.sqrt(SB_HEAD_DIM)
    outs = []
    for blk in range(s // SB_QBLOCK):
        t0 = blk * SB_QBLOCK
        kend = t0 + SB_QBLOCK
        z = jnp.einsum("bhtd,bhsd->bhts", q[:, :, t0:kend], k[:, :, :kend]) * scale
        t_idx = t0 + jnp.arange(SB_QBLOCK)[:, None]
        s_idx = jnp.arange(kend)[None, :]
        strict = s_idx < t_idx
        log_beta = jax.nn.log_sigmoid(z)
        log_1m_beta = jnp.where(strict, jax.nn.log_sigmoid(-z), 0.0)
        rest = lax.cumsum(log_1m_beta, axis=3, reverse=True) - log_1m_beta
        att = jnp.where(strict, jnp.exp(log_beta + rest), 0.0)
        outs.append(jnp.einsum("bhts,bhsd->bhtd", att, v[:, :, :kend]))
    return jnp.concatenate(outs, axis=2)


def sb_branch(u, in_w, qn_w, kn_w, out_w):
    b, s, _ = u.shape
    proj = u @ in_w
    q, k, v, g = jnp.split(proj, 4, axis=-1)
    def heads(t):
        return t.reshape(b, s, SB_N_HEADS, SB_HEAD_DIM)
    q = rms_norm(heads(q), qn_w).astype(jnp.float32).transpose(0, 2, 1, 3)
    k = rms_norm(heads(k), kn_w).astype(jnp.float32).transpose(0, 2, 1, 3)
    v = heads(v).astype(jnp.float32).transpose(0, 2, 1, 3)
    o = stick_breaking_attention(q, k, v).transpose(0, 2, 1, 3).reshape(b, s, SB_WIDTH)
    o = (o * jax.nn.silu(g.astype(jnp.float32))).astype(u.dtype)
    return o @ out_w


def _fwd_setup_inputs(seed: int = 0) -> dict:
    key = jax.random.key(seed)
    ks = jax.random.split(key, 24)
    f32 = jnp.float32
    nrm = lambda k, shape, sc: jax.random.normal(k, shape, f32) * sc
    dt0 = jnp.exp(jax.random.uniform(ks[7], (N_SSD_LAYERS, SSD_N_HEADS), f32)
                  * (math.log(0.1) - math.log(0.001)) + math.log(0.001))
    return {
        "x": nrm(ks[0], (BATCH, SEQ, D_MODEL), 1.0),
        "p": nrm(ks[1], (DEPTH, BATCH, SEQ, PLE_DIM), 1.0),
        "norm_w": 1.0 + nrm(ks[2], (DEPTH, D_MODEL), 0.02),
        "ssd_in_w": nrm(ks[3], (N_SSD_LAYERS, D_MODEL, SSD_IN_DIM), D_MODEL ** -0.5),
        "ssd_conv_w": nrm(ks[4], (N_SSD_LAYERS, SSD_D_CONV, SSD_CONV_DIM), SSD_D_CONV ** -0.5),
        "ssd_conv_b": nrm(ks[5], (N_SSD_LAYERS, SSD_CONV_DIM), 0.02),
        "ssd_dt_bias": dt0 + jnp.log(-jnp.expm1(-dt0)),
        "ssd_a_log": jnp.log(jax.random.uniform(ks[8], (N_SSD_LAYERS, SSD_N_HEADS), f32, 1.0, 16.0)),
        "ssd_d": 1.0 + nrm(ks[9], (N_SSD_LAYERS, SSD_N_HEADS), 0.02),
        "ssd_gnorm_w": 1.0 + nrm(ks[10], (N_SSD_LAYERS, SSD_D_INNER), 0.02),
        "ssd_out_w": nrm(ks[11], (N_SSD_LAYERS, SSD_D_INNER, D_MODEL), SSD_D_INNER ** -0.5),
        "sb_in_w": nrm(ks[12], (N_SB_LAYERS, D_MODEL, 4 * SB_WIDTH), D_MODEL ** -0.5),
        "sb_qn_w": 1.0 + nrm(ks[13], (N_SB_LAYERS, SB_HEAD_DIM), 0.02),
        "sb_kn_w": 1.0 + nrm(ks[14], (N_SB_LAYERS, SB_HEAD_DIM), 0.02),
        "sb_out_w": nrm(ks[15], (N_SB_LAYERS, SB_WIDTH, D_MODEL), SB_WIDTH ** -0.5),
        "ple_norm_w": 1.0 + nrm(ks[16], (DEPTH, D_MODEL), 0.02),
        "ple_gate_w": nrm(ks[17], (DEPTH, D_MODEL, D_MODEL), D_MODEL ** -0.5),
        "ple_proj_w": nrm(ks[18], (DEPTH, PLE_DIM, D_MODEL), 0.5 * PLE_DIM ** -0.5),
    }


def _fwd_reference(x, p, norm_w, ssd_in_w, ssd_conv_w, ssd_conv_b, ssd_dt_bias, ssd_a_log, ssd_d,
              ssd_gnorm_w, ssd_out_w, sb_in_w, sb_qn_w, sb_kn_w, sb_out_w,
              ple_norm_w, ple_gate_w, ple_proj_w):
    h = x
    for i in range(DEPTH):
        u = rms_norm(h, norm_w[i])
        j = i // N_MIXERS
        if i % N_MIXERS == 0:
            mix = ssd_branch(u, ssd_in_w[j], ssd_conv_w[j], ssd_conv_b[j], ssd_dt_bias[j],
                             ssd_a_log[j], ssd_d[j], ssd_gnorm_w[j], ssd_out_w[j])
        else:
            mix = sb_branch(u, sb_in_w[j], sb_qn_w[j], sb_kn_w[j], sb_out_w[j])
        h = h + mix
        gate = jax.nn.sigmoid((rms_norm(h, ple_norm_w[i]) @ ple_gate_w[i]).astype(jnp.float32))
        h = h + ((p[i] @ ple_proj_w[i]).astype(jnp.float32) * gate).astype(h.dtype)
    return h


import jax as _jax
import jax.numpy as _jnp

TWIN_FORMAT = 'train_step'
FWD_PARAMS = ['x', 'p', 'norm_w', 'ssd_in_w', 'ssd_conv_w', 'ssd_conv_b', 'ssd_dt_bias', 'ssd_a_log', 'ssd_d', 'ssd_gnorm_w', 'ssd_out_w', 'sb_in_w', 'sb_qn_w', 'sb_kn_w', 'sb_out_w', 'ple_norm_w', 'ple_gate_w', 'ple_proj_w']
TWIN_WEIGHTS = ['norm_w', 'ssd_in_w', 'ssd_conv_w', 'ssd_conv_b', 'ssd_dt_bias', 'ssd_a_log', 'ssd_d', 'ssd_gnorm_w', 'ssd_out_w', 'sb_in_w', 'sb_qn_w', 'sb_kn_w', 'sb_out_w', 'ple_norm_w', 'ple_gate_w', 'ple_proj_w']
TWIN_DIFF_INPUT = 'x'
TWIN_INPUTS = ['x', 'p', 'norm_w', 'ssd_in_w', 'ssd_conv_w', 'ssd_conv_b', 'ssd_dt_bias', 'ssd_a_log', 'ssd_d', 'ssd_gnorm_w', 'ssd_out_w', 'sb_in_w', 'sb_qn_w', 'sb_kn_w', 'sb_out_w', 'ple_norm_w', 'ple_gate_w', 'ple_proj_w', 'loss_target', 'm_norm_w', 'm_ssd_in_w', 'm_ssd_conv_w', 'm_ssd_conv_b', 'm_ssd_dt_bias', 'm_ssd_a_log', 'm_ssd_d', 'm_ssd_gnorm_w', 'm_ssd_out_w', 'm_sb_in_w', 'm_sb_qn_w', 'm_sb_kn_w', 'm_sb_out_w', 'm_ple_norm_w', 'm_ple_gate_w', 'm_ple_proj_w', 'v_norm_w', 'v_ssd_in_w', 'v_ssd_conv_w', 'v_ssd_conv_b', 'v_ssd_dt_bias', 'v_ssd_a_log', 'v_ssd_d', 'v_ssd_gnorm_w', 'v_ssd_out_w', 'v_sb_in_w', 'v_sb_qn_w', 'v_sb_kn_w', 'v_sb_out_w', 'v_ple_norm_w', 'v_ple_gate_w', 'v_ple_proj_w']
TWIN_OUTPUTS = ['loss', 'grad_x', 'grad_norm_w', 'grad_ssd_in_w', 'grad_ssd_conv_w', 'grad_ssd_conv_b', 'grad_ssd_dt_bias', 'grad_ssd_a_log', 'grad_ssd_d', 'grad_ssd_gnorm_w', 'grad_ssd_out_w', 'grad_sb_in_w', 'grad_sb_qn_w', 'grad_sb_kn_w', 'grad_sb_out_w', 'grad_ple_norm_w', 'grad_ple_gate_w', 'grad_ple_proj_w', 'delta_norm_w', 'delta_ssd_in_w', 'delta_ssd_conv_w', 'delta_ssd_conv_b', 'delta_ssd_dt_bias', 'delta_ssd_a_log', 'delta_ssd_d', 'delta_ssd_gnorm_w', 'delta_ssd_out_w', 'delta_sb_in_w', 'delta_sb_qn_w', 'delta_sb_kn_w', 'delta_sb_out_w', 'delta_ple_norm_w', 'delta_ple_gate_w', 'delta_ple_proj_w', 'new_m_norm_w', 'new_m_ssd_in_w', 'new_m_ssd_conv_w', 'new_m_ssd_conv_b', 'new_m_ssd_dt_bias', 'new_m_ssd_a_log', 'new_m_ssd_d', 'new_m_ssd_gnorm_w', 'new_m_ssd_out_w', 'new_m_sb_in_w', 'new_m_sb_qn_w', 'new_m_sb_kn_w', 'new_m_sb_out_w', 'new_m_ple_norm_w', 'new_m_ple_gate_w', 'new_m_ple_proj_w', 'new_v_norm_w', 'new_v_ssd_in_w', 'new_v_ssd_conv_w', 'new_v_ssd_conv_b', 'new_v_ssd_dt_bias', 'new_v_ssd_a_log', 'new_v_ssd_d', 'new_v_ssd_gnorm_w', 'new_v_ssd_out_w', 'new_v_sb_in_w', 'new_v_sb_qn_w', 'new_v_sb_kn_w', 'new_v_sb_out_w', 'new_v_ple_norm_w', 'new_v_ple_gate_w', 'new_v_ple_proj_w']
TWIN_LEAF_KINDS = {'loss': 'loss', 'grad_x': 'grad_x', 'grad_norm_w': 'grad_w', 'grad_ssd_in_w': 'grad_w', 'grad_ssd_conv_w': 'grad_w', 'grad_ssd_conv_b': 'grad_w', 'grad_ssd_dt_bias': 'grad_w', 'grad_ssd_a_log': 'grad_w', 'grad_ssd_d': 'grad_w', 'grad_ssd_gnorm_w': 'grad_w', 'grad_ssd_out_w': 'grad_w', 'grad_sb_in_w': 'grad_w', 'grad_sb_qn_w': 'grad_w', 'grad_sb_kn_w': 'grad_w', 'grad_sb_out_w': 'grad_w', 'grad_ple_norm_w': 'grad_w', 'grad_ple_gate_w': 'grad_w', 'grad_ple_proj_w': 'grad_w', 'delta_norm_w': 'delta_w', 'delta_ssd_in_w': 'delta_w', 'delta_ssd_conv_w': 'delta_w', 'delta_ssd_conv_b': 'delta_w', 'delta_ssd_dt_bias': 'delta_w', 'delta_ssd_a_log': 'delta_w', 'delta_ssd_d': 'delta_w', 'delta_ssd_gnorm_w': 'delta_w', 'delta_ssd_out_w': 'delta_w', 'delta_sb_in_w': 'delta_w', 'delta_sb_qn_w': 'delta_w', 'delta_sb_kn_w': 'delta_w', 'delta_sb_out_w': 'delta_w', 'delta_ple_norm_w': 'delta_w', 'delta_ple_gate_w': 'delta_w', 'delta_ple_proj_w': 'delta_w', 'new_m_norm_w': 'new_m', 'new_m_ssd_in_w': 'new_m', 'new_m_ssd_conv_w': 'new_m', 'new_m_ssd_conv_b': 'new_m', 'new_m_ssd_dt_bias': 'new_m', 'new_m_ssd_a_log': 'new_m', 'new_m_ssd_d': 'new_m', 'new_m_ssd_gnorm_w': 'new_m', 'new_m_ssd_out_w': 'new_m', 'new_m_sb_in_w': 'new_m', 'new_m_sb_qn_w': 'new_m', 'new_m_sb_kn_w': 'new_m', 'new_m_sb_out_w': 'new_m', 'new_m_ple_norm_w': 'new_m', 'new_m_ple_gate_w': 'new_m', 'new_m_ple_proj_w': 'new_m', 'new_v_norm_w': 'new_v', 'new_v_ssd_in_w': 'new_v', 'new_v_ssd_conv_w': 'new_v', 'new_v_ssd_conv_b': 'new_v', 'new_v_ssd_dt_bias': 'new_v', 'new_v_ssd_a_log': 'new_v', 'new_v_ssd_d': 'new_v', 'new_v_ssd_gnorm_w': 'new_v', 'new_v_ssd_out_w': 'new_v', 'new_v_sb_in_w': 'new_v', 'new_v_sb_qn_w': 'new_v', 'new_v_sb_kn_w': 'new_v', 'new_v_sb_out_w': 'new_v', 'new_v_ple_norm_w': 'new_v', 'new_v_ple_gate_w': 'new_v', 'new_v_ple_proj_w': 'new_v'}


def _forward(args):
    return _fwd_reference(*[args[k] for k in FWD_PARAMS])


def _output_shape():
    out = _jax.eval_shape(lambda: _forward(_fwd_setup_inputs(0)))
    return out.shape, out.dtype

N_MICROBATCH = 1
ADAM_LR = 0.001
ADAM_B1 = 0.9
ADAM_B2 = 0.999
ADAM_EPS = 1e-08
ADAM_WD = 0.01
ADAM_STEP = 10
PER_EXAMPLE_BATCH_AXIS = {'x': 0, 'p': 1, 'loss_target': 0}
SHARED_INPUTS = []
_WEIGHT_DTYPES = {'norm_w': _jnp.float32, 'ssd_in_w': _jnp.float32, 'ssd_conv_w': _jnp.float32, 'ssd_conv_b': _jnp.float32, 'ssd_dt_bias': _jnp.float32, 'ssd_a_log': _jnp.float32, 'ssd_d': _jnp.float32, 'ssd_gnorm_w': _jnp.float32, 'ssd_out_w': _jnp.float32, 'sb_in_w': _jnp.float32, 'sb_qn_w': _jnp.float32, 'sb_kn_w': _jnp.float32, 'sb_out_w': _jnp.float32, 'ple_norm_w': _jnp.float32, 'ple_gate_w': _jnp.float32, 'ple_proj_w': _jnp.float32}
MOMENT_SCALE = {'norm_w': 1.827880e+00, 'ssd_in_w': 1.407808e-01, 'ssd_conv_w': 2.310879e-01, 'ssd_conv_b': 7.333069e-01, 'ssd_dt_bias': 4.320980e-01, 'ssd_a_log': 1.766143e+00, 'ssd_d': 1.354068e+00, 'ssd_gnorm_w': 5.699953e+00, 'ssd_out_w': 5.547429e-01, 'sb_in_w': 9.658584e-02, 'sb_qn_w': 2.704609e+00, 'sb_kn_w': 2.703018e+00, 'sb_out_w': 9.425447e-02, 'ple_norm_w': 6.146442e-02, 'ple_gate_w': 2.890214e-02, 'ple_proj_w': 1.105456e-01}


def _to_microbatches(a, axis):
    t = _jnp.moveaxis(a, axis, 0)
    t = t.reshape((N_MICROBATCH, t.shape[0] // N_MICROBATCH) + t.shape[1:])
    return _jnp.moveaxis(t, 1, axis + 1)


def setup_inputs(seed: int = 0) -> dict:
    inp = _fwd_setup_inputs(seed)
    key = _jax.random.fold_in(_jax.random.key(seed), 7919)
    shape, _ = _output_shape()
    out = dict(inp)
    out["loss_target"] = _jax.random.normal(_jax.random.fold_in(key, 0), shape, _jnp.float32)
    for i, name in enumerate(TWIN_WEIGHTS):
        w = inp[name].astype(_jnp.float32)
        if MOMENT_SCALE is None:
            s = _jnp.sqrt(_jnp.mean(_jnp.square(w)) + 1e-30)
        else:
            s = MOMENT_SCALE[name]
        km, kv = _jax.random.split(_jax.random.fold_in(key, i + 1))
        out[name] = w
        out["m_" + name] = s * _jax.random.normal(km, w.shape, _jnp.float32)
        out["v_" + name] = (s * s) * _jax.random.uniform(kv, w.shape, _jnp.float32, 0.5, 1.5)
    if N_MICROBATCH > 1:
        for name, axis in PER_EXAMPLE_BATCH_AXIS.items():
            out[name] = _to_microbatches(out[name], axis)
    return {'x': out['x'], 'p': out['p'], 'norm_w': out['norm_w'], 'ssd_in_w': out['ssd_in_w'], 'ssd_conv_w': out['ssd_conv_w'], 'ssd_conv_b': out['ssd_conv_b'], 'ssd_dt_bias': out['ssd_dt_bias'], 'ssd_a_log': out['ssd_a_log'], 'ssd_d': out['ssd_d'], 'ssd_gnorm_w': out['ssd_gnorm_w'], 'ssd_out_w': out['ssd_out_w'], 'sb_in_w': out['sb_in_w'], 'sb_qn_w': out['sb_qn_w'], 'sb_kn_w': out['sb_kn_w'], 'sb_out_w': out['sb_out_w'], 'ple_norm_w': out['ple_norm_w'], 'ple_gate_w': out['ple_gate_w'], 'ple_proj_w': out['ple_proj_w'], 'loss_target': out['loss_target'], 'm_norm_w': out['m_norm_w'], 'm_ssd_in_w': out['m_ssd_in_w'], 'm_ssd_conv_w': out['m_ssd_conv_w'], 'm_ssd_conv_b': out['m_ssd_conv_b'], 'm_ssd_dt_bias': out['m_ssd_dt_bias'], 'm_ssd_a_log': out['m_ssd_a_log'], 'm_ssd_d': out['m_ssd_d'], 'm_ssd_gnorm_w': out['m_ssd_gnorm_w'], 'm_ssd_out_w': out['m_ssd_out_w'], 'm_sb_in_w': out['m_sb_in_w'], 'm_sb_qn_w': out['m_sb_qn_w'], 'm_sb_kn_w': out['m_sb_kn_w'], 'm_sb_out_w': out['m_sb_out_w'], 'm_ple_norm_w': out['m_ple_norm_w'], 'm_ple_gate_w': out['m_ple_gate_w'], 'm_ple_proj_w': out['m_ple_proj_w'], 'v_norm_w': out['v_norm_w'], 'v_ssd_in_w': out['v_ssd_in_w'], 'v_ssd_conv_w': out['v_ssd_conv_w'], 'v_ssd_conv_b': out['v_ssd_conv_b'], 'v_ssd_dt_bias': out['v_ssd_dt_bias'], 'v_ssd_a_log': out['v_ssd_a_log'], 'v_ssd_d': out['v_ssd_d'], 'v_ssd_gnorm_w': out['v_ssd_gnorm_w'], 'v_ssd_out_w': out['v_ssd_out_w'], 'v_sb_in_w': out['v_sb_in_w'], 'v_sb_qn_w': out['v_sb_qn_w'], 'v_sb_kn_w': out['v_sb_kn_w'], 'v_sb_out_w': out['v_sb_out_w'], 'v_ple_norm_w': out['v_ple_norm_w'], 'v_ple_gate_w': out['v_ple_gate_w'], 'v_ple_proj_w': out['v_ple_proj_w']}


def _loss(weights, diff, rest, loss_target):
    with _jax.named_scope("forward"):
        args = {**rest, TWIN_DIFF_INPUT: diff, **{k: w.astype(_WEIGHT_DTYPES[k]) for k, w in weights.items()}}
        y = _forward(args)
    with _jax.named_scope("loss_head"):
        err = _jnp.square(y.astype(_jnp.float32) - loss_target)
        return 0.5 * _jnp.sum(_jnp.mean(err, axis=-1)) if err.ndim else 0.5 * err


def _adamw(w, g, m, v):
    m = ADAM_B1 * m + (1.0 - ADAM_B1) * g
    v = ADAM_B2 * v + (1.0 - ADAM_B2) * _jnp.square(g)
    m_hat = m / (1.0 - ADAM_B1 ** ADAM_STEP)
    v_hat = v / (1.0 - ADAM_B2 ** ADAM_STEP)
    delta = -ADAM_LR * (m_hat / (_jnp.sqrt(v_hat) + ADAM_EPS) + ADAM_WD * w)
    return delta, m, v


def reference(x, p, norm_w, ssd_in_w, ssd_conv_w, ssd_conv_b, ssd_dt_bias, ssd_a_log, ssd_d, ssd_gnorm_w, ssd_out_w, sb_in_w, sb_qn_w, sb_kn_w, sb_out_w, ple_norm_w, ple_gate_w, ple_proj_w, loss_target, m_norm_w, m_ssd_in_w, m_ssd_conv_w, m_ssd_conv_b, m_ssd_dt_bias, m_ssd_a_log, m_ssd_d, m_ssd_gnorm_w, m_ssd_out_w, m_sb_in_w, m_sb_qn_w, m_sb_kn_w, m_sb_out_w, m_ple_norm_w, m_ple_gate_w, m_ple_proj_w, v_norm_w, v_ssd_in_w, v_ssd_conv_w, v_ssd_conv_b, v_ssd_dt_bias, v_ssd_a_log, v_ssd_d, v_ssd_gnorm_w, v_ssd_out_w, v_sb_in_w, v_sb_qn_w, v_sb_kn_w, v_sb_out_w, v_ple_norm_w, v_ple_gate_w, v_ple_proj_w):
    given = dict(x=x, p=p, norm_w=norm_w, ssd_in_w=ssd_in_w, ssd_conv_w=ssd_conv_w, ssd_conv_b=ssd_conv_b, ssd_dt_bias=ssd_dt_bias, ssd_a_log=ssd_a_log, ssd_d=ssd_d, ssd_gnorm_w=ssd_gnorm_w, ssd_out_w=ssd_out_w, sb_in_w=sb_in_w, sb_qn_w=sb_qn_w, sb_kn_w=sb_kn_w, sb_out_w=sb_out_w, ple_norm_w=ple_norm_w, ple_gate_w=ple_gate_w, ple_proj_w=ple_proj_w, loss_target=loss_target, m_norm_w=m_norm_w, m_ssd_in_w=m_ssd_in_w, m_ssd_conv_w=m_ssd_conv_w, m_ssd_conv_b=m_ssd_conv_b, m_ssd_dt_bias=m_ssd_dt_bias, m_ssd_a_log=m_ssd_a_log, m_ssd_d=m_ssd_d, m_ssd_gnorm_w=m_ssd_gnorm_w, m_ssd_out_w=m_ssd_out_w, m_sb_in_w=m_sb_in_w, m_sb_qn_w=m_sb_qn_w, m_sb_kn_w=m_sb_kn_w, m_sb_out_w=m_sb_out_w, m_ple_norm_w=m_ple_norm_w, m_ple_gate_w=m_ple_gate_w, m_ple_proj_w=m_ple_proj_w, v_norm_w=v_norm_w, v_ssd_in_w=v_ssd_in_w, v_ssd_conv_w=v_ssd_conv_w, v_ssd_conv_b=v_ssd_conv_b, v_ssd_dt_bias=v_ssd_dt_bias, v_ssd_a_log=v_ssd_a_log, v_ssd_d=v_ssd_d, v_ssd_gnorm_w=v_ssd_gnorm_w, v_ssd_out_w=v_ssd_out_w, v_sb_in_w=v_sb_in_w, v_sb_qn_w=v_sb_qn_w, v_sb_kn_w=v_sb_kn_w, v_sb_out_w=v_sb_out_w, v_ple_norm_w=v_ple_norm_w, v_ple_gate_w=v_ple_gate_w, v_ple_proj_w=v_ple_proj_w)
    weights = {n: given[n] for n in TWIN_WEIGHTS}
    shared = {n: given[n] for n in SHARED_INPUTS}
    per_example = {n: given[n] for n in ['x', 'p']}
    grad_fn = _jax.value_and_grad(_loss, argnums=(0, 1))

    def one_microbatch(ex, loss_target):
        ex = dict(ex)
        diff = ex.pop(TWIN_DIFF_INPUT)
        return grad_fn(weights, diff, {**shared, **ex}, loss_target)

    if N_MICROBATCH == 1:
        loss, (grad_w, grad_x) = one_microbatch(per_example, given["loss_target"])
    else:
        def body(carry, xs):
            loss_sum, grad_sum = carry
            l_k, (gw_k, gx_k) = one_microbatch(xs[0], xs[1])
            with _jax.named_scope("update"):
                return (loss_sum + l_k, _jax.tree.map(_jnp.add, grad_sum, gw_k)), gx_k

        init = (_jnp.zeros((), _jnp.float32), _jax.tree.map(_jnp.zeros_like, weights))
        (loss, grad_w), grad_x = _jax.lax.scan(body, init, (per_example, given["loss_target"]))
    with _jax.named_scope("update"):
        delta_w, new_m, new_v = {}, {}, {}
        for n in TWIN_WEIGHTS:
            delta_w[n], new_m[n], new_v[n] = _adamw(weights[n], grad_w[n], given["m_" + n], given["v_" + n])
    return (loss, grad_x, *[grad_w[n] for n in TWIN_WEIGHTS], *[delta_w[n] for n in TWIN_WEIGHTS],
            *[new_m[n] for n in TWIN_WEIGHTS], *[new_v[n] for n in TWIN_WEIGHTS])
```

```python
import functools
import math

import jax
import jax.numpy as jnp
from jax import lax
from jax.experimental import pallas as pl
from jax.experimental.pallas import tpu as pltpu

F32 = jnp.float32
BF16 = jnp.bfloat16
MESH = pl.DeviceIdType.MESH

D_MODEL = 2048
DEPTH = 4
SSD_D_INNER = 4096
SSD_N_GROUPS = 8
SSD_GROUP_W = SSD_D_INNER // SSD_N_GROUPS
SSD_D_STATE = 128
SSD_CHUNK = 128
SSD_CONV_DIM = 6144
SSD_D_CONV = 4
SSD_N_HEADS = 64
SB_HEAD_DIM = 128
SB_N_HEADS = 16
SB_WIDTH = 2048
NORM_EPS = 1e-6
GATED_NORM_EPS = 1e-5
ADAM_LR = 0.001
ADAM_B1 = 0.9
ADAM_B2 = 0.999
ADAM_EPS = 1e-08
ADAM_WD = 0.01
ADAM_STEP = 10

LANES = 128
FLAT_W = 1024
FLAT_ROWS = 29696
FLAT_BLK = 512

_BIG = (
    ("ssd_in_w", (2, 2048, 2576)),
    ("ssd_out_w", (2, 1024, 2048)),
    ("sb_in_w", (2, 2048, 2048)),
    ("sb_out_w", (2, 512, 2048)),
    ("ple_gate_w", (4, 512, 2048)),
    ("ple_proj_w", (4, 256, 512)),
)
_SMALL = (
    ("norm_w", (4, 2048)),
    ("ssd_conv_b", (2, 6144)),
    ("ssd_dt_bias", (2, 64)),
    ("ssd_a_log", (2, 64)),
    ("ssd_d", (2, 64)),
    ("ssd_gnorm_w", (2, 4096)),
    ("sb_qn_w", (2, 128)),
    ("sb_kn_w", (2, 128)),
    ("ple_norm_w", (4, 2048)),
)

_DN = {
    "nn": (((1,), (0,)), ((), ())),
    "nt": (((1,), (1,)), ((), ())),
    "tn": (((0,), (0,)), ((), ())),
}


def _dot(a, b, dn="nn"):
    return lax.dot_general(a.astype(BF16), b.astype(BF16), _DN[dn], preferred_element_type=F32)


@functools.partial(jax.custom_vjp, nondiff_argnums=(2,))
def _gdot(a, b, dn):
    return _dot(a, b, dn)


def _gdot_fwd(a, b, dn):
    return _dot(a, b, dn), (a, b)


def _gdot_bwd(dn, res, g):
    a, b = res
    if dn == "nn":
        return _dot(g, b, "nt"), _dot(a, g, "tn")
    if dn == "nt":
        return _dot(g, b, "nn"), _dot(g, a, "tn")
    return _dot(b, g, "nt"), _dot(a, g, "nn")


_gdot.defvjp(_gdot_fwd, _gdot_bwd)


def _split_dot(x, t, parts, x_left):
    acc = None
    r = x
    for i in range(parts):
        p = r.astype(BF16)
        d = lax.dot_general(p, t, _DN["nn"], preferred_element_type=F32) if x_left else lax.dot_general(
            t, p, _DN["nn"], preferred_element_type=F32)
        acc = d if acc is None else acc + d
        if i + 1 < parts:
            r = r - p.astype(F32)
    return acc


def _tri(n, lower):
    r = lax.broadcasted_iota(jnp.int32, (n, n), 0)
    c = lax.broadcasted_iota(jnp.int32, (n, n), 1)
    return jnp.where(r >= c if lower else r <= c, 1.0, 0.0).astype(BF16)


def _cumsum_rows_raw(x):
    return _split_dot(x, _tri(x.shape[0], True), 3, False)


@jax.custom_vjp
def _cumsum_rows(x):
    return _cumsum_rows_raw(x)


def _cumsum_rows_fwd(x):
    return _cumsum_rows_raw(x), None


def _cumsum_rows_bwd(_, g):
    return (_split_dot(g, _tri(g.shape[0], False), 3, False),)


_cumsum_rows.defvjp(_cumsum_rows_fwd, _cumsum_rows_bwd)


def _sigmoid(x):
    return 1.0 / (1.0 + jnp.exp(-x))


def _softplus(x):
    return jnp.maximum(x, 0.0) + jnp.log(1.0 + jnp.exp(-jnp.abs(x)))


def _rms(x, w, eps):
    return x * lax.rsqrt(jnp.mean(x * x, axis=-1, keepdims=True) + eps) * w


def _params(*sem):
    return pltpu.CompilerParams(dimension_semantics=sem)


def _pick(dim, pref):
    t = pref
    while t >= LANES:
        if dim % t == 0:
            return t
        t //= 2
    return dim


def _matmul(a, b, *, dn="nn", res=None, out_dtype=F32, name):
    if dn == "tn":
        k_dim, m_dim = a.shape
    else:
        m_dim, k_dim = a.shape
    n_dim = b.shape[0] if dn == "nt" else b.shape[1]
    tm, tn, tk = _pick(m_dim, 1024), _pick(n_dim, 1024), _pick(k_dim, 512)
    nk = k_dim // tk
    a_spec = pl.BlockSpec((tk, tm), lambda i, j, k: (k, i)) if dn == "tn" else pl.BlockSpec((tm, tk), lambda i, j, k: (i, k))
    b_spec = pl.BlockSpec((tn, tk), lambda i, j, k: (j, k)) if dn == "nt" else pl.BlockSpec((tk, tn), lambda i, j, k: (k, j))
    o_spec = pl.BlockSpec((tm, tn), lambda i, j, k: (i, j))
    has_res = res is not None

    def body(*refs):
        if has_res:
            a_ref, b_ref, r_ref, o_ref, acc_ref = refs
        else:
            a_ref, b_ref, o_ref, acc_ref = refs
        k = pl.program_id(2)

        @pl.when(k == 0)
        def _():
            acc_ref[...] = jnp.zeros_like(acc_ref)

        acc_ref[...] += _dot(a_ref[...], b_ref[...], dn)

        @pl.when(k == nk - 1)
        def _():
            v = acc_ref[...]
            if has_res:
                v = v + r_ref[...]
            o_ref[...] = v.astype(o_ref.dtype)

    return pl.pallas_call(
        body,
        grid=(m_dim // tm, n_dim // tn, nk),
        in_specs=[a_spec, b_spec] + ([o_spec] if has_res else []),
        out_specs=o_spec,
        out_shape=jax.ShapeDtypeStruct((m_dim, n_dim), out_dtype),
        scratch_shapes=[pltpu.VMEM((tm, tn), F32)],
        compiler_params=_params("parallel", "parallel", "arbitrary"),
        name=name,
    )(*((a, b, res) if has_res else (a, b)))


def _rowcall(fn, rows, consts, outs, accs, *, name, tm=256):
    args, in_specs = [], []
    for r in rows:
        if isinstance(r, tuple):
            arr, w, cb = r
            in_specs.append(pl.BlockSpec((tm, w), lambda i, cb=cb: (i, cb)))
        else:
            arr = r
            in_specs.append(pl.BlockSpec((tm, arr.shape[1]), lambda i: (i, 0)))
        args.append(arr)
    s_dim = args[0].shape[0]
    for c in consts:
        in_specs.append(pl.BlockSpec(c.shape, lambda i: (0, 0)))
        args.append(c)
    n_in, n_out = len(args), len(outs)
    out_shape = [jax.ShapeDtypeStruct((s_dim, w), dt) for w, dt in outs] + [jax.ShapeDtypeStruct(s, F32) for s in accs]
    out_specs = [pl.BlockSpec((tm, w), lambda i: (i, 0)) for w, _ in outs] + [pl.BlockSpec(s, lambda i: (0, 0)) for s in accs]

    def body(*refs):
        vals = fn(*[r[...] for r in refs[:n_in]])
        o_refs = refs[n_in:n_in + n_out]
        a_refs = refs[n_in + n_out:]
        for o, v in zip(o_refs, vals[:n_out]):
            o[...] = v.astype(o.dtype)
        if a_refs:
            @pl.when(pl.program_id(0) == 0)
            def _():
                for a_ref in a_refs:
                    a_ref[...] = jnp.zeros_like(a_ref)

            for a_ref, v in zip(a_refs, vals[n_out:]):
                a_ref[...] += v

    return pl.pallas_call(
        body, grid=(s_dim // tm,), in_specs=in_specs, out_specs=out_specs, out_shape=out_shape,
        compiler_params=_params("arbitrary"), name=name,
    )(*args)


def _rms_fwd(h, w, name):
    return _rowcall(lambda x, w_: (_rms(x, w_, NORM_EPS),), [h], [w], [(h.shape[1], BF16)], [], name=name)[0]


def _rms_bwd(h, w, dy, dres, name):
    def fn(x, dy_, dres_, w_):
        _, vjp = jax.vjp(lambda a, b: _rms(a, b, NORM_EPS), x, w_)
        dx, dw = vjp(dy_)
        return dx + dres_, dw

    return _rowcall(fn, [h, dy, dres], [w], [(h.shape[1], F32)], [w.shape], name=name)


def _ple_fwd(h1, pp, gl, name):
    return _rowcall(lambda a, b, c: (a + b * _sigmoid(c),), [h1, pp, gl], [], [(h1.shape[1], F32)], [], name=name)[0]


def _ple_bwd(dh2, pp, gl, name):
    def fn(d, b, c):
        gate = _sigmoid(c)
        return d * gate, d * b * gate * (1.0 - gate)

    return _rowcall(fn, [dh2, pp, gl], [], [(dh2.shape[1], BF16), (dh2.shape[1], BF16)], [], name=name)


def _loss_bwd(y, target, name):
    width = y.shape[1]

    def fn(a, t):
        d = a - t
        col = jnp.sum(d * d, axis=0, keepdims=True)
        part = col[:, 0:LANES]
        for j in range(1, width // LANES):
            part = part + col[:, j * LANES:(j + 1) * LANES]
        return d * (1.0 / width), part

    return _rowcall(fn, [y, target], [], [(width, F32)], [(1, LANES)], name=name)


CONV_TC = 256


def _shift_down(x, j):
    if j == 0:
        return x
    row = lax.broadcasted_iota(jnp.int32, x.shape, 0)
    return jnp.where(row >= j, pltpu.roll(x, j, 0), 0.0)


def _shift_up(x, j):
    if j == 0:
        return x
    n = x.shape[0]
    row = lax.broadcasted_iota(jnp.int32, x.shape, 0)
    return jnp.where(row < n - j, pltpu.roll(x, n - j, 0), 0.0)


def _conv_fwd(pzx, cw, cb, name):
    s_dim = pzx.shape[0]
    off = SSD_D_INNER // CONV_TC

    def body(x_ref, w_ref, b_ref, o_ref):
        x = x_ref[...]
        w = w_ref[...]
        y = b_ref[...] + w[3:4, :] * x
        for k in range(SSD_D_CONV - 1):
            y = y + w[k:k + 1, :] * _shift_down(x, SSD_D_CONV - 1 - k)
        o_ref[...] = y * _sigmoid(y)

    return pl.pallas_call(
        body, grid=(SSD_CONV_DIM // CONV_TC,),
        in_specs=[pl.BlockSpec((s_dim, CONV_TC), lambda j: (0, off + j)), pl.BlockSpec((SSD_D_CONV, CONV_TC), lambda j: (0, j)),
                  pl.BlockSpec((1, CONV_TC), lambda j: (0, j))],
        out_specs=pl.BlockSpec((s_dim, CONV_TC), lambda j: (0, j)),
        out_shape=jax.ShapeDtypeStruct((s_dim, SSD_CONV_DIM), F32),
        compiler_params=_params("parallel"), name=name,
    )(pzx, cw, cb)


def _conv_bwd(pzx, cw, cb, dact, name):
    s_dim = pzx.shape[0]
    off = SSD_D_INNER // CONV_TC

    def body(x_ref, w_ref, b_ref, d_ref, dx_ref, dw_ref, db_ref):
        x = x_ref[...]
        w = w_ref[...]
        xs = [_shift_down(x, SSD_D_CONV - 1 - k) for k in range(SSD_D_CONV)]
        y = b_ref[...]
        for k in range(SSD_D_CONV):
            y = y + w[k:k + 1, :] * xs[k]
        sg = _sigmoid(y)
        dy = d_ref[...] * (sg * (1.0 + y * (1.0 - sg)))
        dx = w[3:4, :] * dy
        for k in range(SSD_D_CONV - 1):
            dx = dx + w[k:k + 1, :] * _shift_up(dy, SSD_D_CONV - 1 - k)
        dx_ref[...] = dx.astype(dx_ref.dtype)
        for k in range(SSD_D_CONV):
            dw_ref[k:k + 1, :] = jnp.sum(dy * xs[k], axis=0, keepdims=True)
        db_ref[...] = jnp.sum(dy, axis=0, keepdims=True)

    col = pl.BlockSpec((s_dim, CONV_TC), lambda j: (0, j))
    return pl.pallas_call(
        body, grid=(SSD_CONV_DIM // CONV_TC,),
        in_specs=[pl.BlockSpec((s_dim, CONV_TC), lambda j: (0, off + j)), pl.BlockSpec((SSD_D_CONV, CONV_TC), lambda j: (0, j)),
                  pl.BlockSpec((1, CONV_TC), lambda j: (0, j)), col],
        out_specs=[col, pl.BlockSpec((SSD_D_CONV, CONV_TC), lambda j: (0, j)), pl.BlockSpec((1, CONV_TC), lambda j: (0, j))],
        out_shape=[jax.ShapeDtypeStruct((s_dim, SSD_CONV_DIM), BF16), jax.ShapeDtypeStruct((SSD_D_CONV, SSD_CONV_DIM), F32),
                   jax.ShapeDtypeStruct((1, SSD_CONV_DIM), F32)],
        compiler_params=_params("parallel"), name=name,
    )(pzx, cw, cb, dact)


def _ssd_step(xs, bm, cm, dtraw, bias, alog, dskip, st_in, z, gw, dot, cumsum):
    n = xs.shape[0]
    lane = lax.broadcasted_iota(jnp.int32, (1, LANES), 1)
    sub = lax.broadcasted_iota(jnp.int32, (LANES, 1), 0)
    left = (lane < 64).astype(F32)
    right = 1.0 - left
    top = (sub < 64).astype(F32)
    bot = 1.0 - top
    row = lax.broadcasted_iota(jnp.int32, (n, n), 0)
    colm = lax.broadcasted_iota(jnp.int32, (n, n), 1)
    causal = row >= colm

    dt = _softplus(dtraw + bias)
    adt = dt * (-jnp.exp(alog))
    acum = cumsum(adt)
    acum_t = acum.T
    last = jnp.sum(adt, axis=0, keepdims=True)
    scores = dot(cm, bm, "nt")

    def lane_of(v, h):
        return jnp.sum(v * (lane == h).astype(F32), axis=1, keepdims=True)

    ys, sts = [], []
    for pr in range(4):
        heads = (2 * pr, 2 * pr + 1)
        ac = [lane_of(acum, h) for h in heads]
        ar = [jnp.sum(acum_t * (sub == h).astype(F32), axis=0, keepdims=True) for h in heads]
        dth = [lane_of(dt, h) for h in heads]
        la = [lane_of(last, h) for h in heads]
        dk = [lane_of(dskip, h) for h in heads]
        x2 = xs[:, pr * LANES:(pr + 1) * LANES]
        xdt = x2 * (dth[0] * left + dth[1] * right)
        yd = None
        for i, side in enumerate((left, right)):
            decay = jnp.where(causal, jnp.exp(jnp.minimum(ac[i] - ar[i], 0.0)), 0.0)
            t = dot(scores * decay, xdt * side, "nn")
            yd = t if yd is None else yd + t
        st2 = st_in[pr * LANES:(pr + 1) * LANES, :]
        yo = dot(cm, st2, "nt") * (jnp.exp(ac[0]) * left + jnp.exp(ac[1]) * right)
        dte = jnp.exp(la[0] - ac[0]) * left + jnp.exp(la[1] - ac[1]) * right
        cs = dot(xdt * dte, bm, "tn")
        sts.append(st2 * (jnp.exp(la[0]) * top + jnp.exp(la[1]) * bot) + cs)
        ys.append(yd + yo + (dk[0] * left + dk[1] * right) * x2)
    y = jnp.concatenate(ys, axis=1)
    yg = y * (z * _sigmoid(z))
    yn = yg * lax.rsqrt(jnp.mean(yg * yg, axis=-1, keepdims=True) + GATED_NORM_EPS) * gw
    return yn, jnp.concatenate(sts, axis=0)


def _ssd_specs(n_chunks, rev):
    ci = (lambda c: n_chunks - 1 - c) if rev else (lambda c: c)
    n_x = SSD_D_INNER // LANES
    return dict(
        xs=pl.BlockSpec((SSD_CHUNK, SSD_GROUP_W), lambda g, c: (ci(c), g)),
        bm=pl.BlockSpec((SSD_CHUNK, LANES), lambda g, c: (ci(c), n_x + g)),
        cm=pl.BlockSpec((SSD_CHUNK, LANES), lambda g, c: (ci(c), n_x + SSD_N_GROUPS + g)),
        dt=pl.BlockSpec((None, SSD_CHUNK, LANES), lambda g, c: (g, ci(c), 0)),
        vec=pl.BlockSpec((None, 1, LANES), lambda g, c: (g, 0, 0)),
        z=pl.BlockSpec((SSD_CHUNK, SSD_GROUP_W), lambda g, c: (ci(c), g)),
        gw=pl.BlockSpec((1, SSD_GROUP_W), lambda g, c: (0, g)),
        st=pl.BlockSpec((None, None, SSD_GROUP_W, SSD_D_STATE), lambda g, c: (g, ci(c), 0, 0)),
    )


def _ssd_fwd(act, dtg, bias, alog, dskip, pzx, gw, name):
    s_dim = act.shape[0]
    n_chunks = s_dim // SSD_CHUNK
    sp = _ssd_specs(n_chunks, False)

    def body(xs, bm, cm, dt, b_ref, a_ref, d_ref, z, gw_ref, yn_ref, st_ref, state):
        @pl.when(pl.program_id(1) == 0)
        def _():
            state[...] = jnp.zeros_like(state)

        st_in = state[...]
        st_ref[...] = st_in
        yn, st_out = _ssd_step(xs[...], bm[...], cm[...], dt[...], b_ref[...], a_ref[...], d_ref[...], st_in, z[...], gw_ref[...],
                               _dot, _cumsum_rows_raw)
        yn_ref[...] = yn.astype(yn_ref.dtype)
        state[...] = st_out

    return pl.pallas_call(
        body, grid=(SSD_N_GROUPS, n_chunks),
        in_specs=[sp["xs"], sp["bm"], sp["cm"], sp["dt"], sp["vec"], sp["vec"], sp["vec"], sp["z"], sp["gw"]],
        out_specs=[sp["xs"], sp["st"]],
        out_shape=[jax.ShapeDtypeStruct((s_dim, SSD_D_INNER), BF16),
                   jax.ShapeDtypeStruct((SSD_N_GROUPS, n_chunks, SSD_GROUP_W, SSD_D_STATE), F32)],
        scratch_shapes=[pltpu.VMEM((SSD_GROUP_W, SSD_D_STATE), F32)],
        compiler_params=_params("parallel", "arbitrary"), name=name,
    )(act, act, act, dtg, bias, alog, dskip, pzx, gw)


def _ssd_bwd(act, dtg, bias, alog, dskip, pzx, gw, states, dyn, name):
    s_dim = act.shape[0]
    n_chunks = s_dim // SSD_CHUNK
    sp = _ssd_specs(n_chunks, True)
    rc = lambda c: n_chunks - 1 - c

    def body(xs, bm, cm, dt, b_ref, a_ref, d_ref, z, gw_ref, st_ref, dyn_ref,
             dxs_ref, dbm_ref, dcm_ref, ddt_ref, db_ref, da_ref, dd_ref, dz_ref, dgw_ref, dstate):
        first = pl.program_id(1) == 0

        @pl.when(first)
        def _():
            dstate[...] = jnp.zeros_like(dstate)
            db_ref[...] = jnp.zeros_like(db_ref)
            da_ref[...] = jnp.zeros_like(da_ref)
            dd_ref[...] = jnp.zeros_like(dd_ref)
            dgw_ref[...] = jnp.zeros_like(dgw_ref)

        fn = functools.partial(_ssd_step, dot=_gdot, cumsum=_cumsum_rows)
        _, vjp = jax.vjp(fn, xs[...], bm[...], cm[...], dt[...], b_ref[...], a_ref[...], d_ref[...], st_ref[...], z[...], gw_ref[...])
        dxs, dbm, dcm, ddt, db, da, dd, dst, dz, dgw = vjp((dyn_ref[...], dstate[...]))
        dxs_ref[...] = dxs
        dbm_ref[...] = dbm
        dcm_ref[...] = dcm
        ddt_ref[...] = ddt
        dz_ref[...] = dz.astype(dz_ref.dtype)
        db_ref[...] += db
        da_ref[...] += da
        dd_ref[...] += dd
        dgw_ref[...] += dgw
        dstate[...] = dst

    bc = pl.BlockSpec((SSD_CHUNK, LANES), lambda g, c: (rc(c), g))
    return pl.pallas_call(
        body, grid=(SSD_N_GROUPS, n_chunks),
        in_specs=[sp["xs"], sp["bm"], sp["cm"], sp["dt"], sp["vec"], sp["vec"], sp["vec"], sp["z"], sp["gw"], sp["st"], sp["xs"]],
        out_specs=[sp["xs"], bc, bc, sp["dt"], sp["vec"], sp["vec"], sp["vec"], sp["xs"], sp["gw"]],
        out_shape=[jax.ShapeDtypeStruct((s_dim, SSD_D_INNER), F32),
                   jax.ShapeDtypeStruct((s_dim, SSD_N_GROUPS * SSD_D_STATE), F32),
                   jax.ShapeDtypeStruct((s_dim, SSD_N_GROUPS * SSD_D_STATE), F32),
                   jax.ShapeDtypeStruct((SSD_N_GROUPS, s_dim, LANES), F32),
                   jax.ShapeDtypeStruct((SSD_N_GROUPS, 1, LANES), F32),
                   jax.ShapeDtypeStruct((SSD_N_GROUPS, 1, LANES), F32),
                   jax.ShapeDtypeStruct((SSD_N_GROUPS, 1, LANES), F32),
                   jax.ShapeDtypeStruct((s_dim, SSD_D_INNER), BF16),
                   jax.ShapeDtypeStruct((1, SSD_D_INNER), F32)],
        scratch_shapes=[pltpu.VMEM((SSD_GROUP_W, SSD_D_STATE), F32)],
        compiler_params=_params("arbitrary", "arbitrary"), name=name,
    )(act, act, act, dtg, bias, alog, dskip, pzx, gw, states, dyn)


SB_T = 128
SB_SCALE = 1.0 / math.sqrt(SB_HEAD_DIM)


def _qknorm_fwd(proj, qw, kw, name, tm=512):
    s_dim = proj.shape[0]

    def body(q_ref, k_ref, v_ref, qw_ref, kw_ref, qo, ko, vo):
        qo[...] = _rms(q_ref[...], qw_ref[...], NORM_EPS).astype(BF16)
        ko[...] = _rms(k_ref[...], kw_ref[...], NORM_EPS).astype(BF16)
        vo[...] = v_ref[...].astype(BF16)

    blk = lambda o: pl.BlockSpec((tm, SB_HEAD_DIM), lambda i, h: (i, o + h))
    vec = pl.BlockSpec((1, SB_HEAD_DIM), lambda i, h: (0, 0))
    return pl.pallas_call(
        body, grid=(s_dim // tm, SB_N_HEADS),
        in_specs=[blk(0), blk(SB_N_HEADS), blk(2 * SB_N_HEADS), vec, vec],
        out_specs=[blk(0)] * 3,
        out_shape=[jax.ShapeDtypeStruct((s_dim, SB_WIDTH), BF16)] * 3,
        compiler_params=_params("parallel", "parallel"), name=name,
    )(proj, proj, proj, qw, kw)


def _qknorm_bwd(proj, qw, kw, dqn, dkn, name, tm=512):
    s_dim = proj.shape[0]

    def body(q_ref, k_ref, dq_ref, dk_ref, qw_ref, kw_ref, dqo, dko, dqw, dkw):
        @pl.when((pl.program_id(0) == 0) & (pl.program_id(1) == 0))
        def _():
            dqw[...] = jnp.zeros_like(dqw)
            dkw[...] = jnp.zeros_like(dkw)

        fn = lambda a, b: _rms(a, b, NORM_EPS)
        _, vq = jax.vjp(fn, q_ref[...], qw_ref[...])
        dq, dw = vq(dq_ref[...])
        dqo[...] = dq.astype(BF16)
        dqw[...] += dw
        _, vk = jax.vjp(fn, k_ref[...], kw_ref[...])
        dk, dw = vk(dk_ref[...])
        dko[...] = dk.astype(BF16)
        dkw[...] += dw

    blk = lambda o: pl.BlockSpec((tm, SB_HEAD_DIM), lambda i, h: (i, o + h))
    vec = pl.BlockSpec((1, SB_HEAD_DIM), lambda i, h: (0, 0))
    return pl.pallas_call(
        body, grid=(s_dim // tm, SB_N_HEADS),
        in_specs=[blk(0), blk(SB_N_HEADS), blk(0), blk(0), vec, vec],
        out_specs=[blk(0), blk(0), vec, vec],
        out_shape=[jax.ShapeDtypeStruct((s_dim, SB_WIDTH), BF16)] * 2 + [jax.ShapeDtypeStruct((1, SB_HEAD_DIM), F32)] * 2,
        compiler_params=_params("arbitrary", "arbitrary"), name=name,
    )(proj, proj, dqn, dkn, qw, kw)


def _sb_tile_logits(q, k, diag):
    z = _dot(q, k, "nt") * SB_SCALE
    lb = jnp.minimum(z, 0.0) - jnp.log(1.0 + jnp.exp(-jnp.abs(z)))
    lm = lb - z
    strict = None
    if diag:
        r = lax.broadcasted_iota(jnp.int32, z.shape, 0)
        c = lax.broadcasted_iota(jnp.int32, z.shape, 1)
        strict = c < r
        lm = jnp.where(strict, lm, 0.0)
    return lb, lm, strict


def _sb_fwd(qn, kn, vb, proj, name):
    s_dim = qn.shape[0]
    nq = s_dim // SB_T

    def body(q_ref, k_ref, v_ref, g_ref, og_ref, o_ref, t_ref):
        qi = pl.program_id(1)
        q = q_ref[...]
        upper = _tri(SB_T, False)

        def tile(kb, carry, diag):
            o_acc, cr = carry
            start = pl.multiple_of(kb * SB_T, SB_T)
            k = k_ref[pl.ds(start, SB_T), :]
            v = v_ref[pl.ds(start, SB_T), :]
            lb, lm, strict = _sb_tile_logits(q, k, diag)
            pin = _split_dot(lm, upper, 3, True)
            tot = jnp.sum(lm, axis=1, keepdims=True)
            a = jnp.exp(lb + cr + (tot - pin))
            if diag:
                a = jnp.where(strict, a, 0.0)
            return o_acc + _dot(a, v), cr + tot

        carry = tile(qi, (jnp.zeros((SB_T, SB_HEAD_DIM), F32), jnp.zeros((SB_T, 1), F32)), True)
        o, tot = lax.fori_loop(0, qi, lambda i, c: tile(qi - 1 - i, c, False), carry)
        g = g_ref[...]
        o_ref[...] = o
        og_ref[...] = (o * (g * _sigmoid(g))).astype(og_ref.dtype)
        t_ref[...] = jnp.broadcast_to(tot, (SB_T, LANES))

    qb = pl.BlockSpec((SB_T, SB_HEAD_DIM), lambda h, i: (i, h))
    kv = pl.BlockSpec((s_dim, SB_HEAD_DIM), lambda h, i: (0, h))
    return pl.pallas_call(
        body, grid=(SB_N_HEADS, nq),
        in_specs=[qb, kv, kv, pl.BlockSpec((SB_T, SB_HEAD_DIM), lambda h, i: (i, 3 * SB_N_HEADS + h))],
        out_specs=[qb, qb, pl.BlockSpec((None, SB_T, LANES), lambda h, i: (h, i, 0))],
        out_shape=[jax.ShapeDtypeStruct((s_dim, SB_WIDTH), BF16), jax.ShapeDtypeStruct((s_dim, SB_WIDTH), F32),
                   jax.ShapeDtypeStruct((SB_N_HEADS, s_dim, LANES), F32)],
        compiler_params=_params("parallel", "arbitrary"), name=name,
    )(qn, kn, vb, proj)


def _sb_bwd(qn, kn, vb, proj, o, tot, dog, name):
    s_dim = qn.shape[0]
    nq = s_dim // SB_T

    def body(q_ref, k_ref, v_ref, g_ref, o_ref, t_ref, dog_ref, dq_ref, dk_ref, dv_ref, dvb_ref, dg_ref):
        qi = pl.program_id(1)

        @pl.when(qi == 0)
        def _():
            dk_ref[...] = jnp.zeros_like(dk_ref)
            dv_ref[...] = jnp.zeros_like(dv_ref)

        g = g_ref[...]
        sg = _sigmoid(g)
        dog_v = dog_ref[...]
        dg_ref[...] = (dog_v * o_ref[...] * (sg * (1.0 + g * (1.0 - sg)))).astype(dg_ref.dtype)
        do = (dog_v * (g * sg)).astype(BF16)
        q = q_ref[...]
        total = t_ref[:, 0:1]
        upper = _tri(SB_T, False)

        def tile(kb, carry, diag):
            dq_acc, cp, ce = carry
            start = pl.multiple_of(kb * SB_T, SB_T)
            k = k_ref[pl.ds(start, SB_T), :]
            v = v_ref[pl.ds(start, SB_T), :]
            lb, lm, strict = _sb_tile_logits(q, k, diag)
            pin = cp + _split_dot(lm, upper, 3, True)
            a = jnp.exp(lb + (total - pin))
            if diag:
                a = jnp.where(strict, a, 0.0)
            e = a * _dot(do, v, "nt")
            eex = ce + _split_dot(e, upper, 2, True) - e
            if diag:
                eex = jnp.where(strict, eex, 0.0)
            sig = jnp.exp(lb)
            dz = (e * (1.0 - sig) - eex * sig) * SB_SCALE
            dv_ref[pl.ds(start, SB_T), :] += _dot(a, do, "tn")
            dk_ref[pl.ds(start, SB_T), :] += _dot(dz, q, "tn")
            return (dq_acc + _dot(dz, k), cp + jnp.sum(lm, axis=1, keepdims=True), ce + jnp.sum(e, axis=1, keepdims=True))

        zero = jnp.zeros((SB_T, 1), F32)
        carry = lax.fori_loop(0, qi, lambda i, c: tile(i, c, False), (jnp.zeros((SB_T, SB_HEAD_DIM), F32), zero, zero))
        dq, _, _ = tile(qi, carry, True)
        dq_ref[...] = dq

        @pl.when(qi == nq - 1)
        def _():
            dvb_ref[...] = dv_ref[...].astype(BF16)

    qb = pl.BlockSpec((SB_T, SB_HEAD_DIM), lambda h, i: (i, h))
    kv = pl.BlockSpec((s_dim, SB_HEAD_DIM), lambda h, i: (0, h))
    return pl.pallas_call(
        body, grid=(SB_N_HEADS, nq),
        in_specs=[qb, kv, kv, pl.BlockSpec((SB_T, SB_HEAD_DIM), lambda h, i: (i, 3 * SB_N_HEADS + h)), qb,
                  pl.BlockSpec((None, SB_T, LANES), lambda h, i: (h, i, 0)), qb],
        out_specs=[qb, kv, kv, kv, qb],
        out_shape=[jax.ShapeDtypeStruct((s_dim, SB_WIDTH), F32), jax.ShapeDtypeStruct((s_dim, SB_WIDTH), F32),
                   jax.ShapeDtypeStruct((s_dim, SB_WIDTH), F32), jax.ShapeDtypeStruct((s_dim, SB_WIDTH), BF16),
                   jax.ShapeDtypeStruct((s_dim, SB_WIDTH), BF16)],
        compiler_params=_params("parallel", "arbitrary"), name=name,
    )(qn, kn, vb, proj, o, tot, dog)


def _adamw_math(w, g, m, v):
    m = ADAM_B1 * m + (1.0 - ADAM_B1) * g
    v = ADAM_B2 * v + (1.0 - ADAM_B2) * (g * g)
    m_hat = m / (1.0 - ADAM_B1 ** ADAM_STEP)
    v_hat = v / (1.0 - ADAM_B2 ** ADAM_STEP)
    delta = -ADAM_LR * (m_hat / (jnp.sqrt(v_hat) + ADAM_EPS) + ADAM_WD * w)
    return delta, m, v


def _adamw(w, g, m, v, name):
    n, rows, cols = w.shape
    tr = rows
    while tr * cols * 4 > (1 << 20) and tr % 16 == 0:
        tr //= 2

    def body(w_ref, g_ref, m_ref, v_ref, d_out, m_out, v_out):
        d, m_new, v_new = _adamw_math(w_ref[...], g_ref[...], m_ref[...], v_ref[...])
        d_out[...] = d
        m_out[...] = m_new
        v_out[...] = v_new

    blk = pl.BlockSpec((None, tr, cols), lambda i, j: (i, j, 0))
    return pl.pallas_call(
        body, grid=(n, rows // tr), in_specs=[blk] * 4, out_specs=[blk] * 3,
        out_shape=[jax.ShapeDtypeStruct(w.shape, F32)] * 3,
        compiler_params=_params("parallel", "parallel"), name=name,
    )(w, g, m, v)


_FLIPS = ((1, 0), (0, 1), (1, 1))
_ANY = pl.BlockSpec(memory_space=pl.ANY)


def _place():
    return lax.axis_index("x"), lax.axis_index("y"), lax.axis_index("c")


def _flip(v, f):
    return 1 - v if f else v


def _allgather_big(w):
    rows, cols = w.shape
    half = rows // 2

    def body(w_ref, o_ref, send_sems, recv_sems, local_sem):
        x, y, c = _place()
        me = 2 * x + y
        sibling = (x, y, 1 - c)

        def rows_of(chip, hc):
            return o_ref.at[chip, pl.ds(pl.multiple_of(hc * half, 16), half), :]

        def copy(k, src, dst, to):
            return pltpu.make_async_remote_copy(src_ref=src, dst_ref=dst, send_sem=send_sems.at[k], recv_sem=recv_sems.at[k],
                                                device_id=to, device_id_type=MESH)

        mine = pltpu.make_async_copy(w_ref, o_ref.at[me], local_sem)
        mine.start()
        my_half = w_ref.at[pl.ds(pl.multiple_of(c * half, 16), half), :]
        started = []
        for j, (fx, fy) in enumerate(_FLIPS):
            cp = copy(j, my_half, rows_of(me, c), (_flip(x, fx), _flip(y, fy), c))
            cp.start()
            started.append(cp)
        for j, (fx, fy) in enumerate(_FLIPS):
            chip = 2 * _flip(x, fx) + _flip(y, fy)
            copy(j, my_half, rows_of(chip, c), sibling).wait_recv()
            cp = copy(3 + j, rows_of(chip, c), rows_of(chip, c), sibling)
            cp.start()
            started.append(cp)
        for j, (fx, fy) in enumerate(_FLIPS):
            chip = 2 * _flip(x, fx) + _flip(y, fy)
            copy(3 + j, my_half, rows_of(chip, 1 - c), sibling).wait_recv()
        for cp in started:
            cp.wait_send()
        mine.wait()

    return pl.pallas_call(
        body, in_specs=[_ANY], out_specs=_ANY, out_shape=jax.ShapeDtypeStruct((4, rows, cols), w.dtype),
        scratch_shapes=[pltpu.SemaphoreType.DMA((6,)), pltpu.SemaphoreType.DMA((6,)), pltpu.SemaphoreType.DMA],
        name="allgather_big",
    )(w)


def _pair_exchange(g):
    _, _, half, cols = g.shape

    def body(g_ref, o_ref, send_sem, recv_sem):
        x, y, c = _place()
        cp = pltpu.make_async_remote_copy(src_ref=g_ref.at[:, 1 - c], dst_ref=o_ref, send_sem=send_sem, recv_sem=recv_sem,
                                          device_id=(x, y, 1 - c), device_id_type=MESH)
        cp.start()
        cp.wait()

    return pl.pallas_call(
        body, in_specs=[_ANY], out_specs=_ANY, out_shape=jax.ShapeDtypeStruct((4, half, cols), g.dtype),
        scratch_shapes=[pltpu.SemaphoreType.DMA, pltpu.SemaphoreType.DMA], name="rs_pair_exchange",
    )(g)


def _pair_sum(g, got, c_arr):
    _, _, half, cols = g.shape

    def body(c_ref, g_ref, r_ref, o_ref):
        o_ref[...] = (g_ref[...].astype(F32) + r_ref[...].astype(F32)).astype(o_ref.dtype)

    blk = pl.BlockSpec((None, FLAT_BLK, cols), lambda k, i, c_ref: (k, i, 0))
    return pl.pallas_call(
        body,
        grid_spec=pltpu.PrefetchScalarGridSpec(
            num_scalar_prefetch=1, grid=(4, half // FLAT_BLK),
            in_specs=[pl.BlockSpec((None, None, FLAT_BLK, cols), lambda k, i, c_ref: (k, c_ref[0], i, 0)), blk],
            out_specs=blk),
        out_shape=jax.ShapeDtypeStruct((4, half, cols), BF16),
        compiler_params=_params("parallel", "parallel"), name="rs_pair_sum",
    )(c_arr, g, got)


def _chip_scatter(p):
    _, half, cols = p.shape

    def body(p_ref, o_ref, send_sems, recv_sems):
        x, y, c = _place()
        cps = []
        for j, (fx, fy) in enumerate(_FLIPS):
            tx, ty = _flip(x, fx), _flip(y, fy)
            cp = pltpu.make_async_remote_copy(src_ref=p_ref.at[2 * tx + ty], dst_ref=o_ref.at[j], send_sem=send_sems.at[j],
                                              recv_sem=recv_sems.at[j], device_id=(tx, ty, c), device_id_type=MESH)
            cp.start()
            cps.append(cp)
        for cp in cps:
            cp.wait()

    return pl.pallas_call(
        body, in_specs=[_ANY], out_specs=_ANY, out_shape=jax.ShapeDtypeStruct((3, half, cols), p.dtype),
        scratch_shapes=[pltpu.SemaphoreType.DMA((3,)), pltpu.SemaphoreType.DMA((3,))], name="rs_chip_scatter",
    )(p)


def _chip_sum(p, got, me_arr):
    _, half, cols = p.shape

    def body(me_ref, p_ref, r_ref, o_ref):
        acc = p_ref[...].astype(F32)
        for j in range(3):
            acc = acc + r_ref[j].astype(F32)
        o_ref[...] = acc

    return pl.pallas_call(
        body,
        grid_spec=pltpu.PrefetchScalarGridSpec(
            num_scalar_prefetch=1, grid=(half // FLAT_BLK,),
            in_specs=[pl.BlockSpec((None, FLAT_BLK, cols), lambda i, me_ref: (me_ref[0], i, 0)),
                      pl.BlockSpec((3, FLAT_BLK, cols), lambda i, me_ref: (0, i, 0))],
            out_specs=pl.BlockSpec((FLAT_BLK, cols), lambda i, me_ref: (i, 0))),
        out_shape=jax.ShapeDtypeStruct((half, cols), F32),
        compiler_params=_params("parallel"), name="rs_chip_sum",
    )(me_arr, p, got)


def _pair_gather(t):
    half, cols = t.shape

    def body(t_ref, o_ref, send_sem, recv_sem, local_sem):
        x, y, c = _place()
        mine = pltpu.make_async_copy(t_ref, o_ref.at[c], local_sem)
        mine.start()
        cp = pltpu.make_async_remote_copy(src_ref=t_ref, dst_ref=o_ref.at[c], send_sem=send_sem, recv_sem=recv_sem,
                                          device_id=(x, y, 1 - c), device_id_type=MESH)
        cp.start()
        pltpu.make_async_remote_copy(src_ref=t_ref, dst_ref=o_ref.at[1 - c], send_sem=send_sem, recv_sem=recv_sem,
                                     device_id=(x, y, 1 - c), device_id_type=MESH).wait_recv()
        cp.wait_send()
        mine.wait()

    return pl.pallas_call(
        body, in_specs=[_ANY], out_specs=_ANY, out_shape=jax.ShapeDtypeStruct((2, half, cols), t.dtype),
        scratch_shapes=[pltpu.SemaphoreType.DMA, pltpu.SemaphoreType.DMA, pltpu.SemaphoreType.DMA], name="rs_pair_gather",
    )(t)


def _allreduce_small(v, name):
    rows, cols = v.shape

    def body(v_ref, o_ref, buf, send_sems, recv_sems):
        x, y, c = _place()
        me = 4 * x + 2 * y + c
        buf[0] = v_ref[...]
        cps = []
        for k in range(1, 8):
            kx, ky, kc = (k >> 2) & 1, (k >> 1) & 1, k & 1
            cp = pltpu.make_async_remote_copy(src_ref=v_ref, dst_ref=buf.at[k], send_sem=send_sems.at[k - 1], recv_sem=recv_sems.at[k - 1],
                                              device_id=(_flip(x, kx), _flip(y, ky), _flip(c, kc)), device_id_type=MESH)
            cp.start()
            cps.append(cp)
        for cp in cps:
            cp.wait()
        acc = buf[me]
        for d in range(1, 8):
            acc = acc + buf[jnp.bitwise_xor(d, me)]
        o_ref[...] = acc

    vm = pl.BlockSpec(memory_space=pltpu.VMEM)
    return pl.pallas_call(
        body, in_specs=[vm], out_specs=vm, out_shape=jax.ShapeDtypeStruct((rows, cols), F32),
        scratch_shapes=[pltpu.VMEM((8, rows, cols), F32), pltpu.SemaphoreType.DMA((7,)), pltpu.SemaphoreType.DMA((7,))],
        name=name,
    )(v)


def _pack_flat(parts, dtype):
    flat = jnp.concatenate([p.astype(dtype).reshape(-1) for p in parts])
    pad = FLAT_ROWS * FLAT_W - flat.shape[0]
    return jnp.concatenate([flat, jnp.zeros((pad,), dtype)]).reshape(FLAT_ROWS, FLAT_W)


def _unpack_flat(flat, lead=()):
    flat = flat.reshape(lead + (FLAT_ROWS * FLAT_W,))
    out, off = {}, 0
    for name, shape in _BIG:
        n = math.prod(shape)
        out[name] = lax.slice_in_dim(flat, off, off + n, axis=len(lead)).reshape(lead + shape)
        off += n
    return out


def _full_from_shards(name, g):
    if name in ("ssd_in_w", "sb_in_w", "ple_proj_w"):
        n, r, cs = g.shape[1:]
        return g.transpose(1, 2, 0, 3).reshape(n, r, 4 * cs)
    n, rs, cols = g.shape[1:]
    return g.transpose(1, 0, 2, 3).reshape(n, 4 * rs, cols)


def _shards_from_full(name, w):
    if name in ("ssd_in_w", "sb_in_w", "ple_proj_w"):
        n, r, cols = w.shape
        return w.reshape(n, r, 4, cols // 4).transpose(2, 0, 1, 3)
    n, rows, cols = w.shape
    return w.reshape(n, 4, rows // 4, cols).transpose(1, 0, 2, 3)


def _pad_lanes(a):
    return jnp.pad(a, ((0, 0), (0, LANES - a.shape[1])))


def _group_lanes(v):
    return jnp.pad(v.reshape(SSD_N_GROUPS, 1, 8), ((0, 0), (0, 0), (0, LANES - 8)))


def kernel(x, p, norm_w, ssd_in_w, ssd_conv_w, ssd_conv_b, ssd_dt_bias, ssd_a_log, ssd_d, ssd_gnorm_w, ssd_out_w, sb_in_w, sb_qn_w, sb_kn_w, sb_out_w, ple_norm_w, ple_gate_w, ple_proj_w, loss_target, m_norm_w, m_ssd_in_w, m_ssd_conv_w, m_ssd_conv_b, m_ssd_dt_bias, m_ssd_a_log, m_ssd_d, m_ssd_gnorm_w, m_ssd_out_w, m_sb_in_w, m_sb_qn_w, m_sb_kn_w, m_sb_out_w, m_ple_norm_w, m_ple_gate_w, m_ple_proj_w, v_norm_w, v_ssd_in_w, v_ssd_conv_w, v_ssd_conv_b, v_ssd_dt_bias, v_ssd_a_log, v_ssd_d, v_ssd_gnorm_w, v_ssd_out_w, v_sb_in_w, v_sb_qn_w, v_sb_kn_w, v_sb_out_w, v_ple_norm_w, v_ple_gate_w, v_ple_proj_w):
    w_in = dict(norm_w=norm_w, ssd_in_w=ssd_in_w, ssd_conv_w=ssd_conv_w, ssd_conv_b=ssd_conv_b, ssd_dt_bias=ssd_dt_bias,
                ssd_a_log=ssd_a_log, ssd_d=ssd_d, ssd_gnorm_w=ssd_gnorm_w, ssd_out_w=ssd_out_w, sb_in_w=sb_in_w, sb_qn_w=sb_qn_w,
                sb_kn_w=sb_kn_w, sb_out_w=sb_out_w, ple_norm_w=ple_norm_w, ple_gate_w=ple_gate_w, ple_proj_w=ple_proj_w)
    m_in = dict(norm_w=m_norm_w, ssd_in_w=m_ssd_in_w, ssd_conv_w=m_ssd_conv_w, ssd_conv_b=m_ssd_conv_b, ssd_dt_bias=m_ssd_dt_bias,
                ssd_a_log=m_ssd_a_log, ssd_d=m_ssd_d, ssd_gnorm_w=m_ssd_gnorm_w, ssd_out_w=m_ssd_out_w, sb_in_w=m_sb_in_w,
                sb_qn_w=m_sb_qn_w, sb_kn_w=m_sb_kn_w, sb_out_w=m_sb_out_w, ple_norm_w=m_ple_norm_w, ple_gate_w=m_ple_gate_w,
                ple_proj_w=m_ple_proj_w)
    v_in = dict(norm_w=v_norm_w, ssd_in_w=v_ssd_in_w, ssd_conv_w=v_ssd_conv_w, ssd_conv_b=v_ssd_conv_b, ssd_dt_bias=v_ssd_dt_bias,
                ssd_a_log=v_ssd_a_log, ssd_d=v_ssd_d, ssd_gnorm_w=v_ssd_gnorm_w, ssd_out_w=v_ssd_out_w, sb_in_w=v_sb_in_w,
                sb_qn_w=v_sb_qn_w, sb_kn_w=v_sb_kn_w, sb_out_w=v_sb_out_w, ple_norm_w=v_ple_norm_w, ple_gate_w=v_ple_gate_w,
                ple_proj_w=v_ple_proj_w)
    ix, iy, ic = lax.axis_index("x"), lax.axis_index("y"), lax.axis_index("c")
    chip = 2 * ix + iy
    c_arr = jnp.reshape(ic, (1,)).astype(jnp.int32)
    chip_arr = jnp.reshape(chip, (1,)).astype(jnp.int32)

    gathered = _allgather_big(_pack_flat([w_in[n] for n, _ in _BIG], BF16))
    full = {n: _full_from_shards(n, g) for n, g in _unpack_flat(gathered, (4,)).items()}
    onehot = (jnp.arange(4) == chip).astype(F32) * (ic == 0).astype(F32)
    cw_mine = onehot[:, None, None, None] * ssd_conv_w[None]
    cw_full = _allreduce_small(cw_mine.transpose(1, 2, 0, 3).reshape(-1, LANES), "gather_conv_w").reshape(2, SSD_D_CONV, SSD_CONV_DIM)

    h = x[0]
    target = loss_target[0]
    saved = []
    for i in range(DEPTH):
        j = i // 2
        nw = norm_w[i:i + 1]
        pw = ple_norm_w[i:i + 1]
        s = dict(h=h)
        u = _rms_fwd(h, nw, f"rms_{i}")
        s["u"] = u
        if i % 2 == 0:
            w_all = full["ssd_in_w"][j]
            w_zx = w_all[:, :SSD_D_INNER + SSD_CONV_DIM]
            w_dt = _pad_lanes(w_all[:, SSD_D_INNER + SSD_CONV_DIM:])
            pzx = _matmul(u, w_zx, name=f"ssd_in_{i}")
            pdt = _matmul(u, w_dt, name=f"ssd_indt_{i}")
            act = _conv_fwd(pzx, cw_full[j], ssd_conv_b[j:j + 1], f"conv_{i}")
            dtg = jnp.pad(pdt[:, :SSD_N_HEADS].reshape(-1, SSD_N_GROUPS, 8).transpose(1, 0, 2), ((0, 0), (0, 0), (0, LANES - 8)))
            vecs = (_group_lanes(ssd_dt_bias[j]), _group_lanes(ssd_a_log[j]), _group_lanes(ssd_d[j]))
            yn, states = _ssd_fwd(act, dtg, *vecs, pzx, ssd_gnorm_w[j:j + 1], f"ssd_{i}")
            s.update(w_zx=w_zx, w_dt=w_dt, pzx=pzx, act=act, dtg=dtg, vecs=vecs, yn=yn, states=states)
            h1 = _matmul(yn, full["ssd_out_w"][j], res=h, name=f"ssd_out_{i}")
        else:
            proj = _matmul(u, full["sb_in_w"][j], name=f"sb_in_{i}")
            qn, kn, vb = _qknorm_fwd(proj, sb_qn_w[j:j + 1], sb_kn_w[j:j + 1], f"qknorm_{i}")
            og, o, tot = _sb_fwd(qn, kn, vb, proj, f"sb_{i}")
            s.update(proj=proj, qn=qn, kn=kn, vb=vb, og=og, o=o, tot=tot)
            h1 = _matmul(og, full["sb_out_w"][j], res=h, name=f"sb_out_{i}")
        n2 = _rms_fwd(h1, pw, f"ple_rms_{i}")
        gl = _matmul(n2, full["ple_gate_w"][i], name=f"ple_gate_{i}")
        pp = _matmul(p[i, 0], full["ple_proj_w"][i], name=f"ple_proj_{i}")
        h = _ple_fwd(h1, pp, gl, f"ple_{i}")
        s.update(h1=h1, n2=n2, gl=gl, pp=pp)
        saved.append(s)

    dh, loss_lanes = _loss_bwd(h, target, "loss")

    gbig = {n: [None] * s[0] for n, s in _BIG}
    gsmall = {n: [None] * s[0] for n, s in _SMALL}
    g_conv_w = [None, None]
    for i in reversed(range(DEPTH)):
        j = i // 2
        s = saved[i]
        nw = norm_w[i:i + 1]
        pw = ple_norm_w[i:i + 1]
        dpp, dgl = _ple_bwd(dh, s["pp"], s["gl"], f"ple_bwd_{i}")
        gbig["ple_proj_w"][i] = _matmul(p[i, 0], dpp, dn="tn", out_dtype=BF16, name=f"d_ple_proj_{i}")
        gbig["ple_gate_w"][i] = _matmul(s["n2"], dgl, dn="tn", out_dtype=BF16, name=f"d_ple_gate_{i}")
        dn2 = _matmul(dgl, full["ple_gate_w"][i], dn="nt", name=f"ple_gate_bwd_{i}")
        dh1, dpw = _rms_bwd(s["h1"], pw, dn2, dh, f"ple_rms_bwd_{i}")
        gsmall["ple_norm_w"][i] = dpw
        if i % 2 == 0:
            gbig["ssd_out_w"][j] = _matmul(s["yn"], dh1, dn="tn", out_dtype=BF16, name=f"d_ssd_out_{i}")
            dyn = _matmul(dh1, full["ssd_out_w"][j], dn="nt", name=f"ssd_out_bwd_{i}")
            dxs, dbm, dcm, ddtg, dbias, dalog, ddsk, dz, dgw = _ssd_bwd(
                s["act"], s["dtg"], *s["vecs"], s["pzx"], ssd_gnorm_w[j:j + 1], s["states"], dyn, f"ssd_bwd_{i}")
            dact = jnp.concatenate([dxs, dbm, dcm], axis=1)
            dxbc, dcw, dcb = _conv_bwd(s["pzx"], cw_full[j], ssd_conv_b[j:j + 1], dact, f"conv_bwd_{i}")
            dzx = jnp.concatenate([dz, dxbc], axis=1)
            ddt = _pad_lanes(ddtg[:, :, :8].transpose(1, 0, 2).reshape(-1, SSD_N_HEADS)).astype(BF16)
            du = _matmul(dzx, s["w_zx"], dn="nt", name=f"ssd_in_bwd_{i}")
            du = _matmul(ddt, s["w_dt"], dn="nt", res=du, name=f"ssd_indt_bwd_{i}")
            dw_zx = _matmul(s["u"], dzx, dn="tn", out_dtype=BF16, name=f"d_ssd_in_{i}")
            dw_dt = _matmul(s["u"], ddt, dn="tn", out_dtype=BF16, name=f"d_ssd_indt_{i}")
            gbig["ssd_in_w"][j] = jnp.concatenate([dw_zx, dw_dt[:, :SSD_N_HEADS]], axis=1)
            g_conv_w[j] = dcw
            gsmall["ssd_conv_b"][j] = dcb
            gsmall["ssd_dt_bias"][j] = dbias[:, 0, :8].reshape(1, SSD_N_HEADS)
            gsmall["ssd_a_log"][j] = dalog[:, 0, :8].reshape(1, SSD_N_HEADS)
            gsmall["ssd_d"][j] = ddsk[:, 0, :8].reshape(1, SSD_N_HEADS)
            gsmall["ssd_gnorm_w"][j] = dgw
        else:
            gbig["sb_out_w"][j] = _matmul(s["og"], dh1, dn="tn", out_dtype=BF16, name=f"d_sb_out_{i}")
            dog = _matmul(dh1, full["sb_out_w"][j], dn="nt", name=f"sb_out_bwd_{i}")
            dqn, dkn, _, dvb, dg = _sb_bwd(s["qn"], s["kn"], s["vb"], s["proj"], s["o"], s["tot"], dog, f"sb_bwd_{i}")
            dq, dk, dqw, dkw = _qknorm_bwd(s["proj"], sb_qn_w[j:j + 1], sb_kn_w[j:j + 1], dqn, dkn, f"qknorm_bwd_{i}")
            dproj = jnp.concatenate([dq, dk, dvb, dg], axis=1)
            du = _matmul(dproj, full["sb_in_w"][j], dn="nt", name=f"sb_in_bwd_{i}")
            gbig["sb_in_w"][j] = _matmul(s["u"], dproj, dn="tn", out_dtype=BF16, name=f"d_sb_in_{i}")
            gsmall["sb_qn_w"][j] = dqw
            gsmall["sb_kn_w"][j] = dkw
        dh, dnw = _rms_bwd(s["h"], nw, du, dh1, f"rms_bwd_{i}")
        gsmall["norm_w"][i] = dnw
    grad_x = dh[None]

    g_shards = [_shards_from_full(n, jnp.stack(gbig[n])) for n, _ in _BIG]
    g_flat = jnp.concatenate([g.reshape(4, -1) for g in g_shards], axis=1)
    g_flat = jnp.pad(g_flat, ((0, 0), (0, FLAT_ROWS * FLAT_W - g_flat.shape[1]))).reshape(4, 2, FLAT_ROWS // 2, FLAT_W)
    pair = _pair_sum(g_flat, _pair_exchange(g_flat), c_arr)
    reduced = _pair_gather(_chip_sum(pair, _chip_scatter(pair), chip_arr))
    g_big = _unpack_flat(reduced.reshape(FLAT_ROWS, FLAT_W))

    small_parts = [jnp.concatenate(gsmall[n], axis=0).reshape(-1) for n, _ in _SMALL]
    small_parts.append(jnp.stack(g_conv_w).reshape(-1))
    small_parts.append(loss_lanes.reshape(-1))
    small_sum = _allreduce_small(jnp.concatenate(small_parts).reshape(-1, LANES), "allreduce_small").reshape(-1)
    g_small, off = {}, 0
    for n, shape in _SMALL:
        size = math.prod(shape)
        g_small[n] = small_sum[off:off + size].reshape(shape)
        off += size
    cw_size = 2 * SSD_D_CONV * SSD_CONV_DIM
    g_cw_full = small_sum[off:off + cw_size].reshape(2, SSD_D_CONV, 4, SSD_CONV_DIM // 4)
    g_small["ssd_conv_w"] = jnp.sum(g_cw_full * (jnp.arange(4) == chip).astype(F32)[None, None, :, None], axis=2)
    loss = 0.5 * jnp.sum(small_sum[off + cw_size:]) / D_MODEL

    grads, delta, new_m, new_v = {}, {}, {}, {}
    for n, _ in _BIG:
        grads[n] = g_big[n]
        delta[n], new_m[n], new_v[n] = _adamw(w_in[n], g_big[n], m_in[n], v_in[n], f"adamw_{n}")
    small_names = [n for n, _ in _SMALL] + ["ssd_conv_w"]
    pack = lambda d: jnp.concatenate([d[n].reshape(-1) for n in small_names]).reshape(1, -1, LANES)
    ds, ms, vs = _adamw(pack(w_in), pack(g_small), pack(m_in), pack(v_in), "adamw_small")
    off = 0
    for n in small_names:
        shape = w_in[n].shape
        size = math.prod(shape)
        grads[n] = g_small[n]
        delta[n] = ds.reshape(-1)[off:off + size].reshape(shape)
        new_m[n] = ms.reshape(-1)[off:off + size].reshape(shape)
        new_v[n] = vs.reshape(-1)[off:off + size].reshape(shape)
        off += size

    order = ["norm_w", "ssd_in_w", "ssd_conv_w", "ssd_conv_b", "ssd_dt_bias", "ssd_a_log", "ssd_d", "ssd_gnorm_w", "ssd_out_w",
             "sb_in_w", "sb_qn_w", "sb_kn_w", "sb_out_w", "ple_norm_w", "ple_gate_w", "ple_proj_w"]
    return (loss, grad_x, *[grads[n] for n in order], *[delta[n] for n in order], *[new_m[n] for n in order],
            *[new_v[n] for n in order])
```

```python
import functools
import math

import jax
import jax.numpy as jnp
from jax import lax
from jax.experimental import pallas as pl
from jax.experimental.pallas import tpu as pltpu

F32 = jnp.float32
BF16 = jnp.bfloat16
MESH = pl.DeviceIdType.MESH

D_MODEL = 2048
DEPTH = 4
SSD_D_INNER = 4096
SSD_N_GROUPS = 8
SSD_GROUP_W = SSD_D_INNER // SSD_N_GROUPS
SSD_D_STATE = 128
SSD_CHUNK = 128
SSD_CONV_DIM = 6144
SSD_D_CONV = 4
SSD_N_HEADS = 64
SB_HEAD_DIM = 128
SB_N_HEADS = 16
SB_WIDTH = 2048
NORM_EPS = 1e-6
GATED_NORM_EPS = 1e-5
ADAM_LR = 0.001
ADAM_B1 = 0.9
ADAM_B2 = 0.999
ADAM_EPS = 1e-08
ADAM_WD = 0.01
ADAM_STEP = 10

LANES = 128
BF16_ROWS = 16

_BIG = (
    ("ssd_in_w", (2, 2048, 2576), "col"),
    ("ssd_out_w", (2, 1024, 2048), "row"),
    ("sb_in_w", (2, 2048, 2048), "col"),
    ("sb_out_w", (2, 512, 2048), "row"),
    ("ple_gate_w", (4, 512, 2048), "row"),
    ("ple_proj_w", (4, 256, 512), "col"),
)
_SMALL = (
    ("norm_w", (4, 2048)),
    ("ssd_conv_b", (2, 6144)),
    ("ssd_dt_bias", (2, 64)),
    ("ssd_a_log", (2, 64)),
    ("ssd_d", (2, 64)),
    ("ssd_gnorm_w", (2, 4096)),
    ("sb_qn_w", (2, 128)),
    ("sb_kn_w", (2, 128)),
    ("ple_norm_w", (4, 2048)),
)

_DN = {
    "nn": (((1,), (0,)), ((), ())),
    "nt": (((1,), (1,)), ((), ())),
    "tn": (((0,), (0,)), ((), ())),
}


def _dot(a, b, dn="nn"):
    return lax.dot_general(a.astype(BF16), b.astype(BF16), _DN[dn], preferred_element_type=F32)


@functools.partial(jax.custom_vjp, nondiff_argnums=(2,))
def _gdot(a, b, dn):
    return _dot(a, b, dn)


def _gdot_fwd(a, b, dn):
    return _dot(a, b, dn), (a, b)


def _gdot_bwd(dn, res, g):
    a, b = res
    if dn == "nn":
        return _dot(g, b, "nt"), _dot(a, g, "tn")
    if dn == "nt":
        return _dot(g, b, "nn"), _dot(g, a, "tn")
    return _dot(b, g, "nt"), _dot(a, g, "nn")


_gdot.defvjp(_gdot_fwd, _gdot_bwd)


def _split_dot(x, t, parts, x_left):
    acc = None
    r = x
    for i in range(parts):
        p = r.astype(BF16)
        d = lax.dot_general(p, t, _DN["nn"], preferred_element_type=F32) if x_left else lax.dot_general(
            t, p, _DN["nn"], preferred_element_type=F32)
        acc = d if acc is None else acc + d
        if i + 1 < parts:
            r = r - p.astype(F32)
    return acc


def _tri(n, lower):
    r = lax.broadcasted_iota(jnp.int32, (n, n), 0)
    c = lax.broadcasted_iota(jnp.int32, (n, n), 1)
    return jnp.where(r >= c if lower else r <= c, 1.0, 0.0).astype(BF16)


def _cumsum_rows_raw(x):
    return _split_dot(x, _tri(x.shape[0], True), 3, False)


@jax.custom_vjp
def _cumsum_rows(x):
    return _cumsum_rows_raw(x)


def _cumsum_rows_fwd(x):
    return _cumsum_rows_raw(x), None


def _cumsum_rows_bwd(_, g):
    return (_split_dot(g, _tri(g.shape[0], False), 3, False),)


_cumsum_rows.defvjp(_cumsum_rows_fwd, _cumsum_rows_bwd)


def _sigmoid(x):
    return 1.0 / (1.0 + jnp.exp(-x))


def _softplus(x):
    return jnp.maximum(x, 0.0) + jnp.log(1.0 + jnp.exp(-jnp.abs(x)))


def _rms(x, w, eps):
    return x * lax.rsqrt(jnp.mean(x * x, axis=-1, keepdims=True) + eps) * w


def _params(*sem):
    return pltpu.CompilerParams(dimension_semantics=sem)


def _pick(dim, pref, unit=None):
    t = pref
    while t >= LANES:
        if dim % t == 0 and (unit is None or unit % t == 0):
            return t
        t //= 2
    return dim


def _matmul(a, b, *, dn="nn", res=None, out_dtype=F32, name, b_lay=None, o_lay=None, o_buf=None):
    if dn == "tn":
        k_dim, m_dim = a.shape
    else:
        m_dim, k_dim = a.shape
    unit_m = unit_n = unit_k = None
    if b_lay is None:
        n_dim = b.shape[0] if dn == "nt" else b.shape[1]
    else:
        cut, layer = b_lay
        r, c = b.shape[2:]
        rows, cols = (4 * r, c) if cut == "row" else (r, 4 * c)
        n_dim = cols if dn == "nn" else rows
        assert k_dim == (rows if dn == "nn" else cols) and dn != "tn"
        if (cut == "row") == (dn == "nn"):
            unit_k = r if cut == "row" else c
        else:
            unit_n = r if cut == "row" else c
    if o_lay is not None:
        o_cut, o_layer, o_layers = o_lay
        if o_cut == "row":
            unit_m = m_dim // 4
        else:
            unit_n = n_dim // 4
    tm, tn, tk = _pick(m_dim, 1024, unit_m), _pick(n_dim, 1024, unit_n), _pick(k_dim, 512, unit_k)
    nk = k_dim // tk
    a_spec = pl.BlockSpec((tk, tm), lambda i, j, k: (k, i)) if dn == "tn" else pl.BlockSpec((tm, tk), lambda i, j, k: (i, k))
    if b_lay is None:
        b_spec = pl.BlockSpec((tn, tk), lambda i, j, k: (j, k)) if dn == "nt" else pl.BlockSpec((tk, tn), lambda i, j, k: (k, j))
    elif dn == "nn" and cut == "row":
        per = r // tk
        b_spec = pl.BlockSpec((None, None, tk, tn), lambda i, j, k: (k // per, layer, k % per, j))
    elif dn == "nn":
        per = c // tn
        b_spec = pl.BlockSpec((None, None, tk, tn), lambda i, j, k: (j // per, layer, k, j % per))
    elif cut == "row":
        per = r // tn
        b_spec = pl.BlockSpec((None, None, tn, tk), lambda i, j, k: (j // per, layer, j % per, k))
    else:
        per = c // tk
        b_spec = pl.BlockSpec((None, None, tn, tk), lambda i, j, k: (k // per, layer, j, k % per))
    r_spec = pl.BlockSpec((tm, tn), lambda i, j, k: (i, j))
    if o_lay is None:
        o_spec = r_spec
        out_shape = jax.ShapeDtypeStruct((m_dim, n_dim), out_dtype)
    elif o_cut == "row":
        per_o = unit_m // tm
        o_spec = pl.BlockSpec((None, None, tm, tn), lambda i, j, k: (i // per_o, o_layer, i % per_o, j))
        out_shape = jax.ShapeDtypeStruct((4, o_layers, unit_m, n_dim), out_dtype)
    else:
        per_o = unit_n // tn
        o_spec = pl.BlockSpec((None, None, tm, tn), lambda i, j, k: (j // per_o, o_layer, i, j % per_o))
        out_shape = jax.ShapeDtypeStruct((4, o_layers, m_dim, unit_n), out_dtype)
    has_res = res is not None
    has_buf = o_buf is not None

    def body(*refs):
        a_ref, b_ref = refs[:2]
        r_ref = refs[2] if has_res else None
        o_ref, acc_ref = refs[-2:]
        k = pl.program_id(2)

        @pl.when(k == 0)
        def _():
            acc_ref[...] = jnp.zeros_like(acc_ref)

        acc_ref[...] += _dot(a_ref[...], b_ref[...], dn)

        @pl.when(k == nk - 1)
        def _():
            v = acc_ref[...]
            if has_res:
                v = v + r_ref[...]
            o_ref[...] = v.astype(o_ref.dtype)

    args = [a, b] + ([res] if has_res else []) + ([o_buf] if has_buf else [])
    return pl.pallas_call(
        body,
        grid=(m_dim // tm, n_dim // tn, nk),
        in_specs=[a_spec, b_spec] + ([r_spec] if has_res else []) + ([pl.BlockSpec(memory_space=pl.ANY)] if has_buf else []),
        out_specs=o_spec,
        out_shape=out_shape,
        scratch_shapes=[pltpu.VMEM((tm, tn), F32)],
        input_output_aliases={len(args) - 1: 0} if has_buf else {},
        compiler_params=_params("parallel", "parallel", "arbitrary"),
        name=name,
    )(*args)


def _rowcall(fn, rows, consts, outs, accs, *, name, tm=256):
    args = list(rows) + list(consts)
    in_specs = [pl.BlockSpec((tm, r.shape[1]), lambda i: (i, 0)) for r in rows]
    in_specs += [pl.BlockSpec(c.shape, lambda i: (0, 0)) for c in consts]
    s_dim = args[0].shape[0]
    n_in, n_out = len(args), len(outs)
    out_shape = [jax.ShapeDtypeStruct((s_dim, w), dt) for w, dt in outs] + [jax.ShapeDtypeStruct(s, F32) for s in accs]
    out_specs = [pl.BlockSpec((tm, w), lambda i: (i, 0)) for w, _ in outs] + [pl.BlockSpec(s, lambda i: (0, 0)) for s in accs]

    def body(*refs):
        vals = fn(*[r[...] for r in refs[:n_in]])
        o_refs = refs[n_in:n_in + n_out]
        a_refs = refs[n_in + n_out:]
        for o, v in zip(o_refs, vals[:n_out]):
            o[...] = v.astype(o.dtype)
        if a_refs:
            @pl.when(pl.program_id(0) == 0)
            def _():
                for a_ref in a_refs:
                    a_ref[...] = jnp.zeros_like(a_ref)

            for a_ref, v in zip(a_refs, vals[n_out:]):
                a_ref[...] += v

    return pl.pallas_call(
        body, grid=(s_dim // tm,), in_specs=in_specs, out_specs=out_specs, out_shape=out_shape,
        compiler_params=_params("arbitrary"), name=name,
    )(*args)


def _rms_fwd(h, w, name):
    return _rowcall(lambda x, w_: (_rms(x, w_, NORM_EPS),), [h], [w], [(h.shape[1], BF16)], [], name=name)[0]


def _rms_bwd(h, w, dy, dres, name):
    def fn(x, dy_, dres_, w_):
        _, vjp = jax.vjp(lambda a, b: _rms(a, b, NORM_EPS), x, w_)
        dx, dw = vjp(dy_)
        return dx + dres_, dw

    return _rowcall(fn, [h, dy, dres], [w], [(h.shape[1], F32)], [w.shape], name=name)


def _ple_fwd(h1, pp, gl, name):
    return _rowcall(lambda a, b, c: (a + b * _sigmoid(c),), [h1, pp, gl], [], [(h1.shape[1], F32)], [], name=name)[0]


def _ple_bwd(dh2, pp, gl, name):
    def fn(d, b, c):
        gate = _sigmoid(c)
        return d * gate, d * b * gate * (1.0 - gate)

    return _rowcall(fn, [dh2, pp, gl], [], [(dh2.shape[1], BF16), (dh2.shape[1], BF16)], [], name=name)


def _loss_bwd(y, target, name):
    width = y.shape[1]

    def fn(a, t):
        d = a - t
        col = jnp.sum(d * d, axis=0, keepdims=True)
        part = col[:, 0:LANES]
        for j in range(1, width // LANES):
            part = part + col[:, j * LANES:(j + 1) * LANES]
        return d * (1.0 / width), part

    return _rowcall(fn, [y, target], [], [(width, F32)], [(1, LANES)], name=name)


CONV_TC = 256


def _shift_down(x, j):
    if j == 0:
        return x
    row = lax.broadcasted_iota(jnp.int32, x.shape, 0)
    return jnp.where(row >= j, pltpu.roll(x, j, 0), 0.0)


def _shift_up(x, j):
    if j == 0:
        return x
    n = x.shape[0]
    row = lax.broadcasted_iota(jnp.int32, x.shape, 0)
    return jnp.where(row < n - j, pltpu.roll(x, n - j, 0), 0.0)


def _conv_fwd(pzx, cw, cb, name):
    s_dim = pzx.shape[0]
    off = SSD_D_INNER // CONV_TC

    def body(x_ref, w_ref, b_ref, o_ref):
        x = x_ref[...]
        w = w_ref[...]
        y = b_ref[...] + w[3:4, :] * x
        for k in range(SSD_D_CONV - 1):
            y = y + w[k:k + 1, :] * _shift_down(x, SSD_D_CONV - 1 - k)
        o_ref[...] = y * _sigmoid(y)

    return pl.pallas_call(
        body, grid=(SSD_CONV_DIM // CONV_TC,),
        in_specs=[pl.BlockSpec((s_dim, CONV_TC), lambda j: (0, off + j)), pl.BlockSpec((SSD_D_CONV, CONV_TC), lambda j: (0, j)),
                  pl.BlockSpec((1, CONV_TC), lambda j: (0, j))],
        out_specs=pl.BlockSpec((s_dim, CONV_TC), lambda j: (0, j)),
        out_shape=jax.ShapeDtypeStruct((s_dim, SSD_CONV_DIM), F32),
        compiler_params=_params("parallel"), name=name,
    )(pzx, cw, cb)


def _conv_bwd(pzx, cw, cb, dact, name):
    s_dim = pzx.shape[0]
    off = SSD_D_INNER // CONV_TC

    def body(x_ref, w_ref, b_ref, d_ref, dx_ref, dw_ref, db_ref):
        x = x_ref[...]
        w = w_ref[...]
        xs = [_shift_down(x, SSD_D_CONV - 1 - k) for k in range(SSD_D_CONV)]
        y = b_ref[...]
        for k in range(SSD_D_CONV):
            y = y + w[k:k + 1, :] * xs[k]
        sg = _sigmoid(y)
        dy = d_ref[...] * (sg * (1.0 + y * (1.0 - sg)))
        dx = w[3:4, :] * dy
        for k in range(SSD_D_CONV - 1):
            dx = dx + w[k:k + 1, :] * _shift_up(dy, SSD_D_CONV - 1 - k)
        dx_ref[...] = dx.astype(dx_ref.dtype)
        for k in range(SSD_D_CONV):
            dw_ref[k:k + 1, :] = jnp.sum(dy * xs[k], axis=0, keepdims=True)
        db_ref[...] = jnp.sum(dy, axis=0, keepdims=True)

    col = pl.BlockSpec((s_dim, CONV_TC), lambda j: (0, j))
    return pl.pallas_call(
        body, grid=(SSD_CONV_DIM // CONV_TC,),
        in_specs=[pl.BlockSpec((s_dim, CONV_TC), lambda j: (0, off + j)), pl.BlockSpec((SSD_D_CONV, CONV_TC), lambda j: (0, j)),
                  pl.BlockSpec((1, CONV_TC), lambda j: (0, j)), col],
        out_specs=[col, pl.BlockSpec((SSD_D_CONV, CONV_TC), lambda j: (0, j)), pl.BlockSpec((1, CONV_TC), lambda j: (0, j))],
        out_shape=[jax.ShapeDtypeStruct((s_dim, SSD_CONV_DIM), BF16), jax.ShapeDtypeStruct((SSD_D_CONV, SSD_CONV_DIM), F32),
                   jax.ShapeDtypeStruct((1, SSD_CONV_DIM), F32)],
        compiler_params=_params("parallel"), name=name,
    )(pzx, cw, cb, dact)


def _ssd_step(xs, bm, cm, dtraw, bias, alog, dskip, st_in, z, gw, dot, cumsum):
    n = xs.shape[0]
    lane = lax.broadcasted_iota(jnp.int32, (1, LANES), 1)
    sub = lax.broadcasted_iota(jnp.int32, (LANES, 1), 0)
    left = (lane < 64).astype(F32)
    right = 1.0 - left
    top = (sub < 64).astype(F32)
    bot = 1.0 - top
    row = lax.broadcasted_iota(jnp.int32, (n, n), 0)
    colm = lax.broadcasted_iota(jnp.int32, (n, n), 1)
    causal = row >= colm

    dt = _softplus(dtraw + bias)
    adt = dt * (-jnp.exp(alog))
    acum = cumsum(adt)
    acum_t = acum.T
    last = jnp.sum(adt, axis=0, keepdims=True)
    scores = dot(cm, bm, "nt")

    def lane_of(v, h):
        return jnp.sum(v * (lane == h).astype(F32), axis=1, keepdims=True)

    ys, sts = [], []
    for pr in range(4):
        heads = (2 * pr, 2 * pr + 1)
        ac = [lane_of(acum, h) for h in heads]
        ar = [jnp.sum(acum_t * (sub == h).astype(F32), axis=0, keepdims=True) for h in heads]
        dth = [lane_of(dt, h) for h in heads]
        la = [lane_of(last, h) for h in heads]
        dk = [lane_of(dskip, h) for h in heads]
        x2 = xs[:, pr * LANES:(pr + 1) * LANES]
        xdt = x2 * (dth[0] * left + dth[1] * right)
        yd = None
        for i, side in enumerate((left, right)):
            decay = jnp.where(causal, jnp.exp(jnp.minimum(ac[i] - ar[i], 0.0)), 0.0)
            t = dot(scores * decay, xdt * side, "nn")
            yd = t if yd is None else yd + t
        st2 = st_in[pr * LANES:(pr + 1) * LANES, :]
        yo = dot(cm, st2, "nt") * (jnp.exp(ac[0]) * left + jnp.exp(ac[1]) * right)
        dte = jnp.exp(la[0] - ac[0]) * left + jnp.exp(la[1] - ac[1]) * right
        cs = dot(xdt * dte, bm, "tn")
        sts.append(st2 * (jnp.exp(la[0]) * top + jnp.exp(la[1]) * bot) + cs)
        ys.append(yd + yo + (dk[0] * left + dk[1] * right) * x2)
    y = jnp.concatenate(ys, axis=1)
    yg = y * (z * _sigmoid(z))
    yn = yg * lax.rsqrt(jnp.mean(yg * yg, axis=-1, keepdims=True) + GATED_NORM_EPS) * gw
    return yn, jnp.concatenate(sts, axis=0)


def _ssd_specs(n_chunks, rev):
    ci = (lambda c: n_chunks - 1 - c) if rev else (lambda c: c)
    n_x = SSD_D_INNER // LANES
    return dict(
        xs=pl.BlockSpec((SSD_CHUNK, SSD_GROUP_W), lambda g, c: (ci(c), g)),
        bm=pl.BlockSpec((SSD_CHUNK, LANES), lambda g, c: (ci(c), n_x + g)),
        cm=pl.BlockSpec((SSD_CHUNK, LANES), lambda g, c: (ci(c), n_x + SSD_N_GROUPS + g)),
        dt=pl.BlockSpec((None, SSD_CHUNK, LANES), lambda g, c: (g, ci(c), 0)),
        vec=pl.BlockSpec((None, 1, LANES), lambda g, c: (g, 0, 0)),
        z=pl.BlockSpec((SSD_CHUNK, SSD_GROUP_W), lambda g, c: (ci(c), g)),
        gw=pl.BlockSpec((1, SSD_GROUP_W), lambda g, c: (0, g)),
        st=pl.BlockSpec((None, None, SSD_GROUP_W, SSD_D_STATE), lambda g, c: (g, ci(c), 0, 0)),
    )


def _ssd_fwd(act, dtg, bias, alog, dskip, pzx, gw, name):
    s_dim = act.shape[0]
    n_chunks = s_dim // SSD_CHUNK
    sp = _ssd_specs(n_chunks, False)

    def body(xs, bm, cm, dt, b_ref, a_ref, d_ref, z, gw_ref, yn_ref, st_ref, state):
        @pl.when(pl.program_id(1) == 0)
        def _():
            state[...] = jnp.zeros_like(state)

        st_in = state[...]
        st_ref[...] = st_in
        yn, st_out = _ssd_step(xs[...], bm[...], cm[...], dt[...], b_ref[...], a_ref[...], d_ref[...], st_in, z[...], gw_ref[...],
                               _dot, _cumsum_rows_raw)
        yn_ref[...] = yn.astype(yn_ref.dtype)
        state[...] = st_out

    return pl.pallas_call(
        body, grid=(SSD_N_GROUPS, n_chunks),
        in_specs=[sp["xs"], sp["bm"], sp["cm"], sp["dt"], sp["vec"], sp["vec"], sp["vec"], sp["z"], sp["gw"]],
        out_specs=[sp["xs"], sp["st"]],
        out_shape=[jax.ShapeDtypeStruct((s_dim, SSD_D_INNER), BF16),
                   jax.ShapeDtypeStruct((SSD_N_GROUPS, n_chunks, SSD_GROUP_W, SSD_D_STATE), F32)],
        scratch_shapes=[pltpu.VMEM((SSD_GROUP_W, SSD_D_STATE), F32)],
        compiler_params=_params("parallel", "arbitrary"), name=name,
    )(act, act, act, dtg, bias, alog, dskip, pzx, gw)


def _ssd_bwd(act, dtg, bias, alog, dskip, pzx, gw, states, dyn, name):
    s_dim = act.shape[0]
    n_chunks = s_dim // SSD_CHUNK
    sp = _ssd_specs(n_chunks, True)
    rc = lambda c: n_chunks - 1 - c

    def body(xs, bm, cm, dt, b_ref, a_ref, d_ref, z, gw_ref, st_ref, dyn_ref,
             dxs_ref, dbm_ref, dcm_ref, ddt_ref, db_ref, da_ref, dd_ref, dz_ref, dgw_ref, dstate):
        first = pl.program_id(1) == 0

        @pl.when(first)
        def _():
            dstate[...] = jnp.zeros_like(dstate)
            db_ref[...] = jnp.zeros_like(db_ref)
            da_ref[...] = jnp.zeros_like(da_ref)
            dd_ref[...] = jnp.zeros_like(dd_ref)
            dgw_ref[...] = jnp.zeros_like(dgw_ref)

        fn = functools.partial(_ssd_step, dot=_gdot, cumsum=_cumsum_rows)
        _, vjp = jax.vjp(fn, xs[...], bm[...], cm[...], dt[...], b_ref[...], a_ref[...], d_ref[...], st_ref[...], z[...], gw_ref[...])
        dxs, dbm, dcm, ddt, db, da, dd, dst, dz, dgw = vjp((dyn_ref[...], dstate[...]))
        dxs_ref[...] = dxs
        dbm_ref[...] = dbm
        dcm_ref[...] = dcm
        ddt_ref[...] = ddt
        dz_ref[...] = dz.astype(dz_ref.dtype)
        db_ref[...] += db
        da_ref[...] += da
        dd_ref[...] += dd
        dgw_ref[...] += dgw
        dstate[...] = dst

    bc = pl.BlockSpec((SSD_CHUNK, LANES), lambda g, c: (rc(c), g))
    return pl.pallas_call(
        body, grid=(SSD_N_GROUPS, n_chunks),
        in_specs=[sp["xs"], sp["bm"], sp["cm"], sp["dt"], sp["vec"], sp["vec"], sp["vec"], sp["z"], sp["gw"], sp["st"], sp["xs"]],
        out_specs=[sp["xs"], bc, bc, sp["dt"], sp["vec"], sp["vec"], sp["vec"], sp["xs"], sp["gw"]],
        out_shape=[jax.ShapeDtypeStruct((s_dim, SSD_D_INNER), F32),
                   jax.ShapeDtypeStruct((s_dim, SSD_N_GROUPS * SSD_D_STATE), F32),
                   jax.ShapeDtypeStruct((s_dim, SSD_N_GROUPS * SSD_D_STATE), F32),
                   jax.ShapeDtypeStruct((SSD_N_GROUPS, s_dim, LANES), F32),
                   jax.ShapeDtypeStruct((SSD_N_GROUPS, 1, LANES), F32),
                   jax.ShapeDtypeStruct((SSD_N_GROUPS, 1, LANES), F32),
                   jax.ShapeDtypeStruct((SSD_N_GROUPS, 1, LANES), F32),
                   jax.ShapeDtypeStruct((s_dim, SSD_D_INNER), BF16),
                   jax.ShapeDtypeStruct((1, SSD_D_INNER), F32)],
        scratch_shapes=[pltpu.VMEM((SSD_GROUP_W, SSD_D_STATE), F32)],
        compiler_params=_params("arbitrary", "arbitrary"), name=name,
    )(act, act, act, dtg, bias, alog, dskip, pzx, gw, states, dyn)


SB_T = 128
SB_GROUP = 4
SB_SCALE = 1.0 / math.sqrt(SB_HEAD_DIM)


def _qknorm_fwd(proj, qw, kw, name, tm=512):
    s_dim = proj.shape[0]

    def body(q_ref, k_ref, v_ref, qw_ref, kw_ref, qo, ko, vo):
        qo[...] = _rms(q_ref[...], qw_ref[...], NORM_EPS).astype(BF16)
        ko[...] = _rms(k_ref[...], kw_ref[...], NORM_EPS).astype(BF16)
        vo[...] = v_ref[...].astype(BF16)

    blk = lambda o: pl.BlockSpec((tm, SB_HEAD_DIM), lambda i, h: (i, o + h))
    vec = pl.BlockSpec((1, SB_HEAD_DIM), lambda i, h: (0, 0))
    return pl.pallas_call(
        body, grid=(s_dim // tm, SB_N_HEADS),
        in_specs=[blk(0), blk(SB_N_HEADS), blk(2 * SB_N_HEADS), vec, vec],
        out_specs=[blk(0)] * 3,
        out_shape=[jax.ShapeDtypeStruct((s_dim, SB_WIDTH), BF16)] * 3,
        compiler_params=_params("parallel", "parallel"), name=name,
    )(proj, proj, proj, qw, kw)


def _qknorm_bwd(proj, qw, kw, dqn, dkn, name, tm=512):
    s_dim = proj.shape[0]

    def body(q_ref, k_ref, dq_ref, dk_ref, qw_ref, kw_ref, dqo, dko, dqw, dkw):
        @pl.when((pl.program_id(0) == 0) & (pl.program_id(1) == 0))
        def _():
            dqw[...] = jnp.zeros_like(dqw)
            dkw[...] = jnp.zeros_like(dkw)

        fn = lambda a, b: _rms(a, b, NORM_EPS)
        _, vq = jax.vjp(fn, q_ref[...], qw_ref[...])
        dq, dw = vq(dq_ref[...])
        dqo[...] = dq.astype(BF16)
        dqw[...] += dw
        _, vk = jax.vjp(fn, k_ref[...], kw_ref[...])
        dk, dw = vk(dk_ref[...])
        dko[...] = dk.astype(BF16)
        dkw[...] += dw

    blk = lambda o: pl.BlockSpec((tm, SB_HEAD_DIM), lambda i, h: (i, o + h))
    vec = pl.BlockSpec((1, SB_HEAD_DIM), lambda i, h: (0, 0))
    return pl.pallas_call(
        body, grid=(s_dim // tm, SB_N_HEADS),
        in_specs=[blk(0), blk(SB_N_HEADS), blk(0), blk(0), vec, vec],
        out_specs=[blk(0), blk(0), vec, vec],
        out_shape=[jax.ShapeDtypeStruct((s_dim, SB_WIDTH), BF16)] * 2 + [jax.ShapeDtypeStruct((1, SB_HEAD_DIM), F32)] * 2,
        compiler_params=_params("arbitrary", "arbitrary"), name=name,
    )(proj, proj, dqn, dkn, qw, kw)


def _sb_logits(q, k, diag):
    z = _dot(q, k, "nt") * SB_SCALE
    lb = jnp.minimum(z, 0.0) - jnp.log(1.0 + jnp.exp(-jnp.abs(z)))
    lm = lb - z
    strict = None
    if diag:
        r = lax.broadcasted_iota(jnp.int32, z.shape, 0)
        c = lax.broadcasted_iota(jnp.int32, z.shape, 1)
        strict = c < r
        lm = jnp.where(strict, lm, 0.0)
    return lb, lm, strict


def _lane_cat(parts):
    return parts[0] if len(parts) == 1 else jnp.concatenate(parts, axis=1)


def _sb_fwd(qn, kn, vb, proj, name):
    s_dim = qn.shape[0]
    nq = s_dim // SB_T

    def body(q_ref, k_ref, v_ref, g_ref, og_ref, o_ref, t_ref):
        qi = pl.program_id(1)
        q = q_ref[...]
        upper = _tri(SB_T, False)

        def step(first, width, diag, carry):
            o_acc, cr = carry
            start = pl.multiple_of(first * SB_T, SB_T)
            k = k_ref[pl.ds(start, width * SB_T), :]
            v = v_ref[pl.ds(start, width * SB_T), :]
            lb, lm, strict = _sb_logits(q, k, diag)
            rest = [None] * width
            for g in reversed(range(width)):
                lm_g = lm[:, g * SB_T:(g + 1) * SB_T]
                tot = jnp.sum(lm_g, axis=1, keepdims=True)
                rest[g] = cr + (tot - _split_dot(lm_g, upper, 3, True))
                cr = cr + tot
            a = jnp.exp(lb + _lane_cat(rest))
            if diag:
                a = jnp.where(strict, a, 0.0)
            return o_acc + _dot(a, v), cr

        rem, ngrp = qi % SB_GROUP, qi // SB_GROUP
        carry = step(qi, 1, True, (jnp.zeros((SB_T, SB_HEAD_DIM), F32), jnp.zeros((SB_T, 1), F32)))
        carry = lax.fori_loop(0, rem, lambda i, c: step(qi - 1 - i, 1, False, c), carry)
        o, tot = lax.fori_loop(0, ngrp, lambda i, c: step(SB_GROUP * (ngrp - 1 - i), SB_GROUP, False, c), carry)
        g = g_ref[...]
        o_ref[...] = o
        og_ref[...] = (o * (g * _sigmoid(g))).astype(og_ref.dtype)
        t_ref[...] = jnp.broadcast_to(tot, (SB_T, LANES))

    qb = pl.BlockSpec((SB_T, SB_HEAD_DIM), lambda h, i: (i, h))
    kv = pl.BlockSpec((s_dim, SB_HEAD_DIM), lambda h, i: (0, h))
    return pl.pallas_call(
        body, grid=(SB_N_HEADS, nq),
        in_specs=[qb, kv, kv, pl.BlockSpec((SB_T, SB_HEAD_DIM), lambda h, i: (i, 3 * SB_N_HEADS + h))],
        out_specs=[qb, qb, pl.BlockSpec((None, SB_T, LANES), lambda h, i: (h, i, 0))],
        out_shape=[jax.ShapeDtypeStruct((s_dim, SB_WIDTH), BF16), jax.ShapeDtypeStruct((s_dim, SB_WIDTH), F32),
                   jax.ShapeDtypeStruct((SB_N_HEADS, s_dim, LANES), F32)],
        compiler_params=_params("parallel", "arbitrary"), name=name,
    )(qn, kn, vb, proj)


def _sb_bwd(qn, kn, vb, proj, o, tot, dog, name):
    s_dim = qn.shape[0]
    nq = s_dim // SB_T

    def body(q_ref, k_ref, v_ref, g_ref, o_ref, t_ref, dog_ref, dq_ref, dk_ref, dv_ref, dvb_ref, dg_ref):
        qi = pl.program_id(1)

        @pl.when(qi == 0)
        def _():
            dk_ref[...] = jnp.zeros_like(dk_ref)
            dv_ref[...] = jnp.zeros_like(dv_ref)

        g = g_ref[...]
        sg = _sigmoid(g)
        dog_v = dog_ref[...]
        dg_ref[...] = (dog_v * o_ref[...] * (sg * (1.0 + g * (1.0 - sg)))).astype(dg_ref.dtype)
        do = (dog_v * (g * sg)).astype(BF16)
        q = q_ref[...]
        total = t_ref[:, 0:1]
        upper = _tri(SB_T, False)

        def step(first, width, diag, carry):
            dq_acc, cp, ce = carry
            start = pl.multiple_of(first * SB_T, SB_T)
            k = k_ref[pl.ds(start, width * SB_T), :]
            v = v_ref[pl.ds(start, width * SB_T), :]
            lb, lm, strict = _sb_logits(q, k, diag)
            pref = []
            for t in range(width):
                lm_t = lm[:, t * SB_T:(t + 1) * SB_T]
                pref.append(cp + _split_dot(lm_t, upper, 3, True))
                cp = cp + jnp.sum(lm_t, axis=1, keepdims=True)
            a = jnp.exp(lb + (total - _lane_cat(pref)))
            if diag:
                a = jnp.where(strict, a, 0.0)
            e = a * _dot(do, v, "nt")
            excl = []
            for t in range(width):
                e_t = e[:, t * SB_T:(t + 1) * SB_T]
                excl.append(ce + _split_dot(e_t, upper, 2, True) - e_t)
                ce = ce + jnp.sum(e_t, axis=1, keepdims=True)
            eex = _lane_cat(excl)
            if diag:
                eex = jnp.where(strict, eex, 0.0)
            sig = jnp.exp(lb)
            dz = (e * (1.0 - sig) - eex * sig) * SB_SCALE
            dv_ref[pl.ds(start, width * SB_T), :] += _dot(a, do, "tn")
            dk_ref[pl.ds(start, width * SB_T), :] += _dot(dz, q, "tn")
            return dq_acc + _dot(dz, k), cp, ce

        rem, ngrp = qi % SB_GROUP, qi // SB_GROUP
        zero = jnp.zeros((SB_T, 1), F32)
        carry = lax.fori_loop(0, ngrp, lambda i, c: step(SB_GROUP * i, SB_GROUP, False, c),
                              (jnp.zeros((SB_T, SB_HEAD_DIM), F32), zero, zero))
        carry = lax.fori_loop(0, rem, lambda i, c: step(SB_GROUP * ngrp + i, 1, False, c), carry)
        dq, _, _ = step(qi, 1, True, carry)
        dq_ref[...] = dq

        @pl.when(qi == nq - 1)
        def _():
            dvb_ref[...] = dv_ref[...].astype(BF16)

    qb = pl.BlockSpec((SB_T, SB_HEAD_DIM), lambda h, i: (i, h))
    kv = pl.BlockSpec((s_dim, SB_HEAD_DIM), lambda h, i: (0, h))
    return pl.pallas_call(
        body, grid=(SB_N_HEADS, nq),
        in_specs=[qb, kv, kv, pl.BlockSpec((SB_T, SB_HEAD_DIM), lambda h, i: (i, 3 * SB_N_HEADS + h)), qb,
                  pl.BlockSpec((None, SB_T, LANES), lambda h, i: (h, i, 0)), qb],
        out_specs=[qb, kv, kv, kv, qb],
        out_shape=[jax.ShapeDtypeStruct((s_dim, SB_WIDTH), F32), jax.ShapeDtypeStruct((s_dim, SB_WIDTH), F32),
                   jax.ShapeDtypeStruct((s_dim, SB_WIDTH), F32), jax.ShapeDtypeStruct((s_dim, SB_WIDTH), BF16),
                   jax.ShapeDtypeStruct((s_dim, SB_WIDTH), BF16)],
        compiler_params=_params("parallel", "arbitrary"), name=name,
    )(qn, kn, vb, proj, o, tot, dog)


def _adamw_math(w, g, m, v):
    m = ADAM_B1 * m + (1.0 - ADAM_B1) * g
    v = ADAM_B2 * v + (1.0 - ADAM_B2) * (g * g)
    m_hat = m / (1.0 - ADAM_B1 ** ADAM_STEP)
    v_hat = v / (1.0 - ADAM_B2 ** ADAM_STEP)
    delta = -ADAM_LR * (m_hat / (jnp.sqrt(v_hat) + ADAM_EPS) + ADAM_WD * w)
    return delta, m, v


def _row_block(rows, cols, itemsize=4, limit=1 << 20):
    tr = rows
    while tr * cols * itemsize > limit and tr % (2 * BF16_ROWS) == 0:
        tr //= 2
    return tr


def _adamw(w, g, m, v, name):
    n, rows, cols = w.shape
    tr = _row_block(rows, cols)

    def body(w_ref, g_ref, m_ref, v_ref, d_out, m_out, v_out):
        d, m_new, v_new = _adamw_math(w_ref[...], g_ref[...], m_ref[...], v_ref[...])
        d_out[...] = d
        m_out[...] = m_new
        v_out[...] = v_new

    blk = pl.BlockSpec((None, tr, cols), lambda i, j: (i, j, 0))
    return pl.pallas_call(
        body, grid=(n, rows // tr), in_specs=[blk] * 4, out_specs=[blk] * 3,
        out_shape=[jax.ShapeDtypeStruct(w.shape, F32)] * 3,
        compiler_params=_params("parallel", "parallel"), name=name,
    )(w, g, m, v)


_FLIPS = ((1, 0), (0, 1), (1, 1))
_ANY = pl.BlockSpec(memory_space=pl.ANY)


def _place():
    return lax.axis_index("x"), lax.axis_index("y"), lax.axis_index("c")


def _flip(v, f):
    return 1 - v if f else v


def _half_rows(ref, lead, hc, hr):
    return ref.at[(*lead, pl.ds(pl.multiple_of(hc * hr, BF16_ROWS), hr), slice(None))]


def _allgather_big(shards):
    n_arr = len(shards)

    def body(*refs):
        ins, outs = refs[:n_arr], refs[n_arr:2 * n_arr]
        send_sems, recv_sems = refs[2 * n_arr:]
        x, y, c = _place()
        me = 2 * x + y
        sibling = (x, y, 1 - c)
        all_l = slice(None)

        def copy(a, k, src, dst, to):
            return pltpu.make_async_remote_copy(src_ref=src, dst_ref=dst, send_sem=send_sems.at[a, k], recv_sem=recv_sems.at[a, k],
                                                device_id=to, device_id_type=MESH)

        hrs = [s.shape[1] // 2 for s in shards]
        mine = [_half_rows(ins[a], (all_l,), c, hrs[a]) for a in range(n_arr)]
        started = []
        for a in range(n_arr):
            for j, (fx, fy) in enumerate(_FLIPS):
                cp = copy(a, j, mine[a], _half_rows(outs[a], (me, all_l), c, hrs[a]), (_flip(x, fx), _flip(y, fy), c))
                cp.start()
                started.append(cp)
        for j, (fx, fy) in enumerate(_FLIPS):
            chip = 2 * _flip(x, fx) + _flip(y, fy)
            for a in range(n_arr):
                landed = _half_rows(outs[a], (chip, all_l), c, hrs[a])
                copy(a, j, mine[a], landed, sibling).wait_recv()
                cp = copy(a, 3 + j, landed, landed, sibling)
                cp.start()
                started.append(cp)
        for j, (fx, fy) in enumerate(_FLIPS):
            chip = 2 * _flip(x, fx) + _flip(y, fy)
            for a in range(n_arr):
                copy(a, 3 + j, mine[a], _half_rows(outs[a], (chip, all_l), 1 - c, hrs[a]), sibling).wait_recv()
        for cp in started:
            cp.wait_send()

    return pl.pallas_call(
        body, in_specs=[_ANY] * n_arr, out_specs=[_ANY] * n_arr,
        out_shape=[jax.ShapeDtypeStruct((4,) + s.shape, s.dtype) for s in shards],
        scratch_shapes=[pltpu.SemaphoreType.DMA((n_arr, 6)), pltpu.SemaphoreType.DMA((n_arr, 6))],
        name="allgather_big",
    )(*shards)


def _pair_exchange(grads):
    n_arr = len(grads)

    def body(*refs):
        ins, outs = refs[:n_arr], refs[n_arr:2 * n_arr]
        send_sems, recv_sems = refs[2 * n_arr:]
        x, y, c = _place()
        cps = []
        for a in range(n_arr):
            hr = grads[a].shape[2] // 2
            cp = pltpu.make_async_remote_copy(
                src_ref=_half_rows(ins[a], (slice(None), slice(None)), 1 - c, hr), dst_ref=outs[a], send_sem=send_sems.at[a],
                recv_sem=recv_sems.at[a], device_id=(x, y, 1 - c), device_id_type=MESH)
            cp.start()
            cps.append(cp)
        for cp in cps:
            cp.wait()

    return pl.pallas_call(
        body, in_specs=[_ANY] * n_arr, out_specs=[_ANY] * n_arr,
        out_shape=[jax.ShapeDtypeStruct(g.shape[:2] + (g.shape[2] // 2, g.shape[3]), g.dtype) for g in grads],
        scratch_shapes=[pltpu.SemaphoreType.DMA((n_arr,)), pltpu.SemaphoreType.DMA((n_arr,))], name="rs_pair_exchange",
    )(*grads)


def _pair_sum(g, got, place, name):
    _, layers, hr, cols = got.shape
    tr = _row_block(hr, cols)
    per = hr // tr

    def body(place_ref, g_ref, r_ref, o_ref):
        o_ref[...] = (g_ref[...].astype(F32) + r_ref[...].astype(F32)).astype(o_ref.dtype)

    blk = pl.BlockSpec((None, None, tr, cols), lambda k, l, i, pr: (k, l, i, 0))
    return pl.pallas_call(
        body,
        grid_spec=pltpu.PrefetchScalarGridSpec(
            num_scalar_prefetch=1, grid=(4, layers, per),
            in_specs=[pl.BlockSpec((None, None, tr, cols), lambda k, l, i, pr: (k, l, pr[1] * per + i, 0)), blk],
            out_specs=blk),
        out_shape=jax.ShapeDtypeStruct(got.shape, BF16),
        compiler_params=_params("parallel", "parallel", "parallel"), name=name,
    )(place, g, got)


def _chip_scatter(pairs):
    n_arr = len(pairs)

    def body(*refs):
        ins, outs = refs[:n_arr], refs[n_arr:2 * n_arr]
        send_sems, recv_sems = refs[2 * n_arr:]
        x, y, c = _place()
        cps = []
        for a in range(n_arr):
            for j, (fx, fy) in enumerate(_FLIPS):
                tx, ty = _flip(x, fx), _flip(y, fy)
                cp = pltpu.make_async_remote_copy(src_ref=ins[a].at[2 * tx + ty], dst_ref=outs[a].at[j], send_sem=send_sems.at[a, j],
                                                  recv_sem=recv_sems.at[a, j], device_id=(tx, ty, c), device_id_type=MESH)
                cp.start()
                cps.append(cp)
        for cp in cps:
            cp.wait()

    return pl.pallas_call(
        body, in_specs=[_ANY] * n_arr, out_specs=[_ANY] * n_arr,
        out_shape=[jax.ShapeDtypeStruct((3,) + p.shape[1:], p.dtype) for p in pairs],
        scratch_shapes=[pltpu.SemaphoreType.DMA((n_arr, 3)), pltpu.SemaphoreType.DMA((n_arr, 3))], name="rs_chip_scatter",
    )(*pairs)


def _chip_sum(p, got, place, name):
    _, layers, hr, cols = p.shape
    tr = _row_block(hr, cols)
    per = hr // tr

    def body(place_ref, p_ref, r_ref, o_ref):
        acc = p_ref[...].astype(F32)
        for j in range(3):
            acc = acc + r_ref[j].astype(F32)
        o_ref[...] = acc

    return pl.pallas_call(
        body,
        grid_spec=pltpu.PrefetchScalarGridSpec(
            num_scalar_prefetch=1, grid=(layers, per),
            in_specs=[pl.BlockSpec((None, None, tr, cols), lambda l, i, pr: (pr[0], l, i, 0)),
                      pl.BlockSpec((3, None, tr, cols), lambda l, i, pr: (0, l, i, 0))],
            out_specs=pl.BlockSpec((None, tr, cols), lambda l, i, pr: (l, pr[1] * per + i, 0))),
        out_shape=jax.ShapeDtypeStruct((layers, 2 * hr, cols), F32),
        compiler_params=_params("parallel", "parallel"), name=name,
    )(place, p, got)


def _pair_gather(halves):
    n_arr = len(halves)

    def body(*refs):
        bufs = refs[n_arr:2 * n_arr]
        send_sems, recv_sems = refs[2 * n_arr:]
        x, y, c = _place()
        cps = []
        for a in range(n_arr):
            hr = halves[a].shape[1] // 2
            mine = _half_rows(bufs[a], (slice(None),), c, hr)
            cp = pltpu.make_async_remote_copy(src_ref=mine, dst_ref=mine, send_sem=send_sems.at[a], recv_sem=recv_sems.at[a],
                                              device_id=(x, y, 1 - c), device_id_type=MESH)
            cp.start()
            cps.append(cp)
        for a in range(n_arr):
            hr = halves[a].shape[1] // 2
            theirs = _half_rows(bufs[a], (slice(None),), 1 - c, hr)
            pltpu.make_async_remote_copy(src_ref=theirs, dst_ref=theirs, send_sem=send_sems.at[a], recv_sem=recv_sems.at[a],
                                         device_id=(x, y, 1 - c), device_id_type=MESH).wait_recv()
        for cp in cps:
            cp.wait_send()

    return pl.pallas_call(
        body, in_specs=[_ANY] * n_arr, out_specs=[_ANY] * n_arr,
        out_shape=[jax.ShapeDtypeStruct(h.shape, h.dtype) for h in halves],
        input_output_aliases={a: a for a in range(n_arr)},
        scratch_shapes=[pltpu.SemaphoreType.DMA((n_arr,)), pltpu.SemaphoreType.DMA((n_arr,))], name="rs_pair_gather",
    )(*halves)


def _allreduce_small(v, name):
    rows, cols = v.shape

    def body(v_ref, o_ref, buf, send_sems, recv_sems):
        x, y, c = _place()
        me = 4 * x + 2 * y + c
        buf[0] = v_ref[...]
        cps = []
        for k in range(1, 8):
            kx, ky, kc = (k >> 2) & 1, (k >> 1) & 1, k & 1
            cp = pltpu.make_async_remote_copy(src_ref=v_ref, dst_ref=buf.at[k], send_sem=send_sems.at[k - 1], recv_sem=recv_sems.at[k - 1],
                                              device_id=(_flip(x, kx), _flip(y, ky), _flip(c, kc)), device_id_type=MESH)
            cp.start()
            cps.append(cp)
        for cp in cps:
            cp.wait()
        acc = buf[me]
        for d in range(1, 8):
            acc = acc + buf[jnp.bitwise_xor(d, me)]
        o_ref[...] = acc

    vm = pl.BlockSpec(memory_space=pltpu.VMEM)
    return pl.pallas_call(
        body, in_specs=[vm], out_specs=vm, out_shape=jax.ShapeDtypeStruct((rows, cols), F32),
        scratch_shapes=[pltpu.VMEM((8, rows, cols), F32), pltpu.SemaphoreType.DMA((7,)), pltpu.SemaphoreType.DMA((7,))],
        name=name,
    )(v)


def _cols_from_shards(g):
    _, n, r, cs = g.shape
    return g.transpose(1, 2, 0, 3).reshape(n, r, 4 * cs)


def _shards_from_cols(w):
    n, r, cols = w.shape
    return w.reshape(n, r, 4, cols // 4).transpose(2, 0, 1, 3)


def _pad_lanes(a):
    return jnp.pad(a, ((0, 0), (0, LANES - a.shape[1])))


def _group_lanes(v):
    return jnp.pad(v.reshape(SSD_N_GROUPS, 1, 8), ((0, 0), (0, 0), (0, LANES - 8)))


def kernel(x, p, norm_w, ssd_in_w, ssd_conv_w, ssd_conv_b, ssd_dt_bias, ssd_a_log, ssd_d, ssd_gnorm_w, ssd_out_w, sb_in_w, sb_qn_w, sb_kn_w, sb_out_w, ple_norm_w, ple_gate_w, ple_proj_w, loss_target, m_norm_w, m_ssd_in_w, m_ssd_conv_w, m_ssd_conv_b, m_ssd_dt_bias, m_ssd_a_log, m_ssd_d, m_ssd_gnorm_w, m_ssd_out_w, m_sb_in_w, m_sb_qn_w, m_sb_kn_w, m_sb_out_w, m_ple_norm_w, m_ple_gate_w, m_ple_proj_w, v_norm_w, v_ssd_in_w, v_ssd_conv_w, v_ssd_conv_b, v_ssd_dt_bias, v_ssd_a_log, v_ssd_d, v_ssd_gnorm_w, v_ssd_out_w, v_sb_in_w, v_sb_qn_w, v_sb_kn_w, v_sb_out_w, v_ple_norm_w, v_ple_gate_w, v_ple_proj_w):
    w_in = dict(norm_w=norm_w, ssd_in_w=ssd_in_w, ssd_conv_w=ssd_conv_w, ssd_conv_b=ssd_conv_b, ssd_dt_bias=ssd_dt_bias,
                ssd_a_log=ssd_a_log, ssd_d=ssd_d, ssd_gnorm_w=ssd_gnorm_w, ssd_out_w=ssd_out_w, sb_in_w=sb_in_w, sb_qn_w=sb_qn_w,
                sb_kn_w=sb_kn_w, sb_out_w=sb_out_w, ple_norm_w=ple_norm_w, ple_gate_w=ple_gate_w, ple_proj_w=ple_proj_w)
    m_in = dict(norm_w=m_norm_w, ssd_in_w=m_ssd_in_w, ssd_conv_w=m_ssd_conv_w, ssd_conv_b=m_ssd_conv_b, ssd_dt_bias=m_ssd_dt_bias,
                ssd_a_log=m_ssd_a_log, ssd_d=m_ssd_d, ssd_gnorm_w=m_ssd_gnorm_w, ssd_out_w=m_ssd_out_w, sb_in_w=m_sb_in_w,
                sb_qn_w=m_sb_qn_w, sb_kn_w=m_sb_kn_w, sb_out_w=m_sb_out_w, ple_norm_w=m_ple_norm_w, ple_gate_w=m_ple_gate_w,
                ple_proj_w=m_ple_proj_w)
    v_in = dict(norm_w=v_norm_w, ssd_in_w=v_ssd_in_w, ssd_conv_w=v_ssd_conv_w, ssd_conv_b=v_ssd_conv_b, ssd_dt_bias=v_ssd_dt_bias,
                ssd_a_log=v_ssd_a_log, ssd_d=v_ssd_d, ssd_gnorm_w=v_ssd_gnorm_w, ssd_out_w=v_ssd_out_w, sb_in_w=v_sb_in_w,
                sb_qn_w=v_sb_qn_w, sb_kn_w=v_sb_kn_w, sb_out_w=v_sb_out_w, ple_norm_w=v_ple_norm_w, ple_gate_w=v_ple_gate_w,
                ple_proj_w=v_ple_proj_w)
    ix, iy, ic = lax.axis_index("x"), lax.axis_index("y"), lax.axis_index("c")
    chip = (2 * ix + iy).astype(jnp.int32)
    place = jnp.stack([chip, ic.astype(jnp.int32)])
    zero = jnp.zeros((), jnp.int32)
    big_names = [n for n, _, _ in _BIG]
    layers_of = {n: s[0] for n, s, _ in _BIG}
    cut_of = {n: cut for n, _, cut in _BIG}

    mine = [w_in[n].astype(BF16) for n in big_names]
    gw = {n: lax.dynamic_update_slice(g, s[None], (chip, zero, zero, zero)) for n, g, s in zip(big_names, _allgather_big(mine), mine)}
    ssd_in_full = _cols_from_shards(gw["ssd_in_w"])
    onehot = (jnp.arange(4) == chip).astype(F32) * (ic == 0).astype(F32)
    cw_mine = onehot[:, None, None, None] * ssd_conv_w[None]
    cw_full = _allreduce_small(cw_mine.transpose(1, 2, 0, 3).reshape(-1, LANES), "gather_conv_w").reshape(2, SSD_D_CONV, SSD_CONV_DIM)

    def wmm(a, name, layer, *, dn="nn", res=None, call):
        return _matmul(a, gw[name], dn=dn, res=res, b_lay=(cut_of[name], layer), name=call)

    h = x[0]
    target = loss_target[0]
    saved = []
    for i in range(DEPTH):
        j = i // 2
        nw = norm_w[i:i + 1]
        pw = ple_norm_w[i:i + 1]
        s = dict(h=h)
        u = _rms_fwd(h, nw, f"rms_{i}")
        s["u"] = u
        if i % 2 == 0:
            w_all = ssd_in_full[j]
            w_zx = w_all[:, :SSD_D_INNER + SSD_CONV_DIM]
            w_dt = _pad_lanes(w_all[:, SSD_D_INNER + SSD_CONV_DIM:])
            pzx = _matmul(u, w_zx, name=f"ssd_in_{i}")
            pdt = _matmul(u, w_dt, name=f"ssd_indt_{i}")
            act = _conv_fwd(pzx, cw_full[j], ssd_conv_b[j:j + 1], f"conv_{i}")
            dtg = jnp.pad(pdt[:, :SSD_N_HEADS].reshape(-1, SSD_N_GROUPS, 8).transpose(1, 0, 2), ((0, 0), (0, 0), (0, LANES - 8)))
            vecs = (_group_lanes(ssd_dt_bias[j]), _group_lanes(ssd_a_log[j]), _group_lanes(ssd_d[j]))
            yn, states = _ssd_fwd(act, dtg, *vecs, pzx, ssd_gnorm_w[j:j + 1], f"ssd_{i}")
            s.update(w_zx=w_zx, w_dt=w_dt, pzx=pzx, act=act, dtg=dtg, vecs=vecs, yn=yn, states=states)
            h1 = wmm(yn, "ssd_out_w", j, res=h, call=f"ssd_out_{i}")
        else:
            proj = wmm(u, "sb_in_w", j, call=f"sb_in_{i}")
            qn, kn, vb = _qknorm_fwd(proj, sb_qn_w[j:j + 1], sb_kn_w[j:j + 1], f"qknorm_{i}")
            og, o, tot = _sb_fwd(qn, kn, vb, proj, f"sb_{i}")
            s.update(proj=proj, qn=qn, kn=kn, vb=vb, og=og, o=o, tot=tot)
            h1 = wmm(og, "sb_out_w", j, res=h, call=f"sb_out_{i}")
        n2 = _rms_fwd(h1, pw, f"ple_rms_{i}")
        gl = wmm(n2, "ple_gate_w", i, call=f"ple_gate_{i}")
        pp = wmm(p[i, 0], "ple_proj_w", i, call=f"ple_proj_{i}")
        h = _ple_fwd(h1, pp, gl, f"ple_{i}")
        s.update(h1=h1, n2=n2, gl=gl, pp=pp)
        saved.append(s)

    dh, loss_lanes = _loss_bwd(h, target, "loss")

    gb = {n: None for n in big_names}
    d_ssd_in = [None, None]
    gsmall = {n: [None] * s[0] for n, s in _SMALL}
    g_conv_w = [None, None]

    def wgrad(a, b, name, layer, call):
        gb[name] = _matmul(a, b, dn="tn", out_dtype=BF16, o_lay=(cut_of[name], layer, layers_of[name]), o_buf=gb[name], name=call)

    for i in reversed(range(DEPTH)):
        j = i // 2
        s = saved[i]
        nw = norm_w[i:i + 1]
        pw = ple_norm_w[i:i + 1]
        dpp, dgl = _ple_bwd(dh, s["pp"], s["gl"], f"ple_bwd_{i}")
        wgrad(p[i, 0], dpp, "ple_proj_w", i, f"d_ple_proj_{i}")
        wgrad(s["n2"], dgl, "ple_gate_w", i, f"d_ple_gate_{i}")
        dn2 = wmm(dgl, "ple_gate_w", i, dn="nt", call=f"ple_gate_bwd_{i}")
        dh1, dpw = _rms_bwd(s["h1"], pw, dn2, dh, f"ple_rms_bwd_{i}")
        gsmall["ple_norm_w"][i] = dpw
        if i % 2 == 0:
            wgrad(s["yn"], dh1, "ssd_out_w", j, f"d_ssd_out_{i}")
            dyn = wmm(dh1, "ssd_out_w", j, dn="nt", call=f"ssd_out_bwd_{i}")
            dxs, dbm, dcm, ddtg, dbias, dalog, ddsk, dz, dgw = _ssd_bwd(
                s["act"], s["dtg"], *s["vecs"], s["pzx"], ssd_gnorm_w[j:j + 1], s["states"], dyn, f"ssd_bwd_{i}")
            dact = jnp.concatenate([dxs, dbm, dcm], axis=1)
            dxbc, dcw, dcb = _conv_bwd(s["pzx"], cw_full[j], ssd_conv_b[j:j + 1], dact, f"conv_bwd_{i}")
            dzx = jnp.concatenate([dz, dxbc], axis=1)
            ddt = _pad_lanes(ddtg[:, :, :8].transpose(1, 0, 2).reshape(-1, SSD_N_HEADS)).astype(BF16)
            du = _matmul(dzx, s["w_zx"], dn="nt", name=f"ssd_in_bwd_{i}")
            du = _matmul(ddt, s["w_dt"], dn="nt", res=du, name=f"ssd_indt_bwd_{i}")
            dw_zx = _matmul(s["u"], dzx, dn="tn", out_dtype=BF16, name=f"d_ssd_in_{i}")
            dw_dt = _matmul(s["u"], ddt, dn="tn", out_dtype=BF16, name=f"d_ssd_indt_{i}")
            d_ssd_in[j] = jnp.concatenate([dw_zx, dw_dt[:, :SSD_N_HEADS]], axis=1)
            g_conv_w[j] = dcw
            gsmall["ssd_conv_b"][j] = dcb
            gsmall["ssd_dt_bias"][j] = dbias[:, 0, :8].reshape(1, SSD_N_HEADS)
            gsmall["ssd_a_log"][j] = dalog[:, 0, :8].reshape(1, SSD_N_HEADS)
            gsmall["ssd_d"][j] = ddsk[:, 0, :8].reshape(1, SSD_N_HEADS)
            gsmall["ssd_gnorm_w"][j] = dgw
        else:
            wgrad(s["og"], dh1, "sb_out_w", j, f"d_sb_out_{i}")
            dog = wmm(dh1, "sb_out_w", j, dn="nt", call=f"sb_out_bwd_{i}")
            dqn, dkn, _, dvb, dg = _sb_bwd(s["qn"], s["kn"], s["vb"], s["proj"], s["o"], s["tot"], dog, f"sb_bwd_{i}")
            dq, dk, dqw, dkw = _qknorm_bwd(s["proj"], sb_qn_w[j:j + 1], sb_kn_w[j:j + 1], dqn, dkn, f"qknorm_bwd_{i}")
            dproj = jnp.concatenate([dq, dk, dvb, dg], axis=1)
            du = wmm(dproj, "sb_in_w", j, dn="nt", call=f"sb_in_bwd_{i}")
            wgrad(s["u"], dproj, "sb_in_w", j, f"d_sb_in_{i}")
            gsmall["sb_qn_w"][j] = dqw
            gsmall["sb_kn_w"][j] = dkw
        dh, dnw = _rms_bwd(s["h"], nw, du, dh1, f"rms_bwd_{i}")
        gsmall["norm_w"][i] = dnw
    grad_x = dh[None]
    gb["ssd_in_w"] = _shards_from_cols(jnp.stack(d_ssd_in))

    g_list = [gb[n] for n in big_names]
    pairs = [_pair_sum(g, r, place, f"rs_pair_sum_{n}") for n, g, r in zip(big_names, g_list, _pair_exchange(g_list))]
    halves = [_chip_sum(q, r, place, f"rs_chip_sum_{n}") for n, q, r in zip(big_names, pairs, _chip_scatter(pairs))]
    g_big = dict(zip(big_names, _pair_gather(halves)))

    small_parts = [jnp.concatenate(gsmall[n], axis=0).reshape(-1) for n, _ in _SMALL]
    small_parts.append(jnp.stack(g_conv_w).reshape(-1))
    small_parts.append(loss_lanes.reshape(-1))
    small_sum = _allreduce_small(jnp.concatenate(small_parts).reshape(-1, LANES), "allreduce_small").reshape(-1)
    g_small, off = {}, 0
    for n, shape in _SMALL:
        size = math.prod(shape)
        g_small[n] = small_sum[off:off + size].reshape(shape)
        off += size
    cw_size = 2 * SSD_D_CONV * SSD_CONV_DIM
    g_cw_full = small_sum[off:off + cw_size].reshape(2, SSD_D_CONV, 4, SSD_CONV_DIM // 4)
    g_small["ssd_conv_w"] = jnp.sum(g_cw_full * (jnp.arange(4) == chip).astype(F32)[None, None, :, None], axis=2)
    loss = 0.5 * jnp.sum(small_sum[off + cw_size:]) / D_MODEL

    grads, delta, new_m, new_v = {}, {}, {}, {}
    for n in big_names:
        grads[n] = g_big[n]
        delta[n], new_m[n], new_v[n] = _adamw(w_in[n], g_big[n], m_in[n], v_in[n], f"adamw_{n}")
    small_names = [n for n, _ in _SMALL] + ["ssd_conv_w"]
    pack = lambda d: jnp.concatenate([d[n].reshape(-1) for n in small_names]).reshape(1, -1, LANES)
    ds, ms, vs = _adamw(pack(w_in), pack(g_small), pack(m_in), pack(v_in), "adamw_small")
    off = 0
    for n in small_names:
        shape = w_in[n].shape
        size = math.prod(shape)
        grads[n] = g_small[n]
        delta[n] = ds.reshape(-1)[off:off + size].reshape(shape)
        new_m[n] = ms.reshape(-1)[off:off + size].reshape(shape)
        new_v[n] = vs.reshape(-1)[off:off + size].reshape(shape)
        off += size

    order = ["norm_w", "ssd_in_w", "ssd_conv_w", "ssd_conv_b", "ssd_dt_bias", "ssd_a_log", "ssd_d", "ssd_gnorm_w", "ssd_out_w",
             "sb_in_w", "sb_qn_w", "sb_kn_w", "sb_out_w", "ple_norm_w", "ple_gate_w", "ple_proj_w"]
    return (loss, grad_x, *[grads[n] for n in order], *[delta[n] for n in order], *[new_m[n] for n in order],
            *[new_v[n] for n in order])
```

```python
import functools
import math

import jax
import jax.numpy as jnp
from jax import lax
from jax.experimental import pallas as pl
from jax.experimental.pallas import tpu as pltpu

F32 = jnp.float32
BF16 = jnp.bfloat16
MESH = pl.DeviceIdType.MESH

D_MODEL = 2048
DEPTH = 4
SSD_D_INNER = 4096
SSD_N_GROUPS = 8
SSD_GROUP_W = SSD_D_INNER // SSD_N_GROUPS
SSD_D_STATE = 128
SSD_CHUNK = 128
SSD_CONV_DIM = 6144
SSD_D_CONV = 4
SSD_N_HEADS = 64
SB_HEAD_DIM = 128
SB_N_HEADS = 16
SB_WIDTH = 2048
NORM_EPS = 1e-6
GATED_NORM_EPS = 1e-5
ADAM_LR = 0.001
ADAM_B1 = 0.9
ADAM_B2 = 0.999
ADAM_EPS = 1e-08
ADAM_WD = 0.01
ADAM_STEP = 10

LANES = 128
BF16_ROWS = 16

_BIG = (
    ("ssd_in_w", (2, 2048, 2576), "col"),
    ("ssd_out_w", (2, 1024, 2048), "row"),
    ("sb_in_w", (2, 2048, 2048), "col"),
    ("sb_out_w", (2, 512, 2048), "row"),
    ("ple_gate_w", (4, 512, 2048), "row"),
    ("ple_proj_w", (4, 256, 512), "col"),
)
_SMALL = (
    ("norm_w", (4, 2048)),
    ("ssd_conv_b", (2, 6144)),
    ("ssd_dt_bias", (2, 64)),
    ("ssd_a_log", (2, 64)),
    ("ssd_d", (2, 64)),
    ("ssd_gnorm_w", (2, 4096)),
    ("sb_qn_w", (2, 128)),
    ("sb_kn_w", (2, 128)),
    ("ple_norm_w", (4, 2048)),
)

_DN = {
    "nn": (((1,), (0,)), ((), ())),
    "nt": (((1,), (1,)), ((), ())),
    "tn": (((0,), (0,)), ((), ())),
}


def _dot(a, b, dn="nn"):
    return lax.dot_general(a.astype(BF16), b.astype(BF16), _DN[dn], preferred_element_type=F32)


@functools.partial(jax.custom_vjp, nondiff_argnums=(2,))
def _gdot(a, b, dn):
    return _dot(a, b, dn)


def _gdot_fwd(a, b, dn):
    return _dot(a, b, dn), (a, b)


def _gdot_bwd(dn, res, g):
    a, b = res
    if dn == "nn":
        return _dot(g, b, "nt"), _dot(a, g, "tn")
    if dn == "nt":
        return _dot(g, b, "nn"), _dot(g, a, "tn")
    return _dot(b, g, "nt"), _dot(a, g, "nn")


_gdot.defvjp(_gdot_fwd, _gdot_bwd)


def _split_dot(x, t, parts, x_left):
    acc = None
    r = x
    for i in range(parts):
        p = r.astype(BF16)
        d = lax.dot_general(p, t, _DN["nn"], preferred_element_type=F32) if x_left else lax.dot_general(
            t, p, _DN["nn"], preferred_element_type=F32)
        acc = d if acc is None else acc + d
        if i + 1 < parts:
            r = r - p.astype(F32)
    return acc


def _tri(n, lower, strict=False):
    r = lax.broadcasted_iota(jnp.int32, (n, n), 0)
    c = lax.broadcasted_iota(jnp.int32, (n, n), 1)
    keep = (r > c if strict else r >= c) if lower else (r < c if strict else r <= c)
    return jnp.where(keep, 1.0, 0.0).astype(BF16)


def _cumsum_rows_raw(x):
    return _split_dot(x, _tri(x.shape[0], True), 3, False)


@jax.custom_vjp
def _cumsum_rows(x):
    return _cumsum_rows_raw(x)


def _cumsum_rows_fwd(x):
    return _cumsum_rows_raw(x), None


def _cumsum_rows_bwd(_, g):
    return (_split_dot(g, _tri(g.shape[0], False), 3, False),)


_cumsum_rows.defvjp(_cumsum_rows_fwd, _cumsum_rows_bwd)


def _sigmoid(x):
    return 1.0 / (1.0 + jnp.exp(-x))


def _softplus(x):
    return jnp.maximum(x, 0.0) + jnp.log(1.0 + jnp.exp(-jnp.abs(x)))


def _rms(x, w, eps):
    return x * lax.rsqrt(jnp.mean(x * x, axis=-1, keepdims=True) + eps) * w


def _params(*sem):
    return pltpu.CompilerParams(dimension_semantics=sem)


MM_TK = 2048


def _pick(dim, pref, unit=None):
    t = pref
    while t >= LANES:
        if dim % t == 0 and (unit is None or unit % t == 0):
            return t
        t //= 2
    return dim


def _matmul(a, b, *, dn="nn", res=None, out_dtype=F32, name, b_lay=None, o_lay=None, o_buf=None):
    if dn == "tn":
        k_dim, m_dim = a.shape
    else:
        m_dim, k_dim = a.shape
    unit_m = unit_n = unit_k = None
    if b_lay is None:
        n_dim = b.shape[0] if dn == "nt" else b.shape[1]
    else:
        cut, layer = b_lay
        r, c = b.shape[2:]
        rows, cols = (4 * r, c) if cut == "row" else (r, 4 * c)
        n_dim = cols if dn == "nn" else rows
        assert k_dim == (rows if dn == "nn" else cols) and dn != "tn"
        if (cut == "row") == (dn == "nn"):
            unit_k = r if cut == "row" else c
        else:
            unit_n = r if cut == "row" else c
    if o_lay is not None:
        o_cut, o_layer, o_layers = o_lay
        if o_cut == "row":
            unit_m = m_dim // 4
        else:
            unit_n = n_dim // 4
    tm, tn, tk = _pick(m_dim, 1024, unit_m), _pick(n_dim, 1024, unit_n), _pick(k_dim, MM_TK, unit_k)
    nk = k_dim // tk
    a_spec = pl.BlockSpec((tk, tm), lambda i, j, k: (k, i)) if dn == "tn" else pl.BlockSpec((tm, tk), lambda i, j, k: (i, k))
    if b_lay is None:
        b_spec = pl.BlockSpec((tn, tk), lambda i, j, k: (j, k)) if dn == "nt" else pl.BlockSpec((tk, tn), lambda i, j, k: (k, j))
    elif dn == "nn" and cut == "row":
        per = r // tk
        b_spec = pl.BlockSpec((None, None, tk, tn), lambda i, j, k: (k // per, layer, k % per, j))
    elif dn == "nn":
        per = c // tn
        b_spec = pl.BlockSpec((None, None, tk, tn), lambda i, j, k: (j // per, layer, k, j % per))
    elif cut == "row":
        per = r // tn
        b_spec = pl.BlockSpec((None, None, tn, tk), lambda i, j, k: (j // per, layer, j % per, k))
    else:
        per = c // tk
        b_spec = pl.BlockSpec((None, None, tn, tk), lambda i, j, k: (k // per, layer, j, k % per))
    r_spec = pl.BlockSpec((tm, tn), lambda i, j, k: (i, j))
    if o_lay is None:
        o_spec = r_spec
        out_shape = jax.ShapeDtypeStruct((m_dim, n_dim), out_dtype)
    elif o_cut == "row":
        per_o = unit_m // tm
        o_spec = pl.BlockSpec((None, None, tm, tn), lambda i, j, k: (i // per_o, o_layer, i % per_o, j))
        out_shape = jax.ShapeDtypeStruct((4, o_layers, unit_m, n_dim), out_dtype)
    else:
        per_o = unit_n // tn
        o_spec = pl.BlockSpec((None, None, tm, tn), lambda i, j, k: (j // per_o, o_layer, i, j % per_o))
        out_shape = jax.ShapeDtypeStruct((4, o_layers, m_dim, unit_n), out_dtype)
    has_res = res is not None
    has_buf = o_buf is not None

    def body(*refs):
        a_ref, b_ref = refs[:2]
        r_ref = refs[2] if has_res else None
        o_ref = refs[-1] if nk == 1 else refs[-2]

        def finish(v):
            if has_res:
                v = v + r_ref[...]
            o_ref[...] = v.astype(o_ref.dtype)

        if nk == 1:
            finish(_dot(a_ref[...], b_ref[...], dn))
            return
        acc_ref = refs[-1]
        k = pl.program_id(2)

        @pl.when(k == 0)
        def _():
            acc_ref[...] = jnp.zeros_like(acc_ref)

        acc_ref[...] += _dot(a_ref[...], b_ref[...], dn)

        @pl.when(k == nk - 1)
        def _():
            finish(acc_ref[...])

    args = [a, b] + ([res] if has_res else []) + ([o_buf] if has_buf else [])
    return pl.pallas_call(
        body,
        grid=(m_dim // tm, n_dim // tn, nk),
        in_specs=[a_spec, b_spec] + ([r_spec] if has_res else []) + ([pl.BlockSpec(memory_space=pl.ANY)] if has_buf else []),
        out_specs=o_spec,
        out_shape=out_shape,
        scratch_shapes=[] if nk == 1 else [pltpu.VMEM((tm, tn), F32)],
        input_output_aliases={len(args) - 1: 0} if has_buf else {},
        compiler_params=_params("parallel", "parallel", "arbitrary"),
        name=name,
    )(*args)


def _rowcall(fn, rows, consts, outs, accs, *, name, tm=256):
    args = list(rows) + list(consts)
    in_specs = [pl.BlockSpec((tm, r.shape[1]), lambda i: (i, 0)) for r in rows]
    in_specs += [pl.BlockSpec(c.shape, lambda i: (0, 0)) for c in consts]
    s_dim = args[0].shape[0]
    n_in, n_out = len(args), len(outs)
    out_shape = [jax.ShapeDtypeStruct((s_dim, w), dt) for w, dt in outs] + [jax.ShapeDtypeStruct(s, F32) for s in accs]
    out_specs = [pl.BlockSpec((tm, w), lambda i: (i, 0)) for w, _ in outs] + [pl.BlockSpec(s, lambda i: (0, 0)) for s in accs]

    def body(*refs):
        vals = fn(*[r[...] for r in refs[:n_in]])
        o_refs = refs[n_in:n_in + n_out]
        a_refs = refs[n_in + n_out:]
        for o, v in zip(o_refs, vals[:n_out]):
            o[...] = v.astype(o.dtype)
        if a_refs:
            @pl.when(pl.program_id(0) == 0)
            def _():
                for a_ref in a_refs:
                    a_ref[...] = jnp.zeros_like(a_ref)

            for a_ref, v in zip(a_refs, vals[n_out:]):
                a_ref[...] += v

    return pl.pallas_call(
        body, grid=(s_dim // tm,), in_specs=in_specs, out_specs=out_specs, out_shape=out_shape,
        compiler_params=_params("arbitrary"), name=name,
    )(*args)


def _rms_fwd(h, w, name):
    return _rowcall(lambda x, w_: (_rms(x, w_, NORM_EPS),), [h], [w], [(h.shape[1], BF16)], [], name=name)[0]


def _rms_bwd(h, w, dy, dres, name):
    def fn(x, dy_, dres_, w_):
        _, vjp = jax.vjp(lambda a, b: _rms(a, b, NORM_EPS), x, w_)
        dx, dw = vjp(dy_)
        return dx + dres_, dw

    return _rowcall(fn, [h, dy, dres], [w], [(h.shape[1], F32)], [w.shape], name=name)


def _ple_fwd(h1, pp, gl, name):
    return _rowcall(lambda a, b, c: (a + b * _sigmoid(c),), [h1, pp, gl], [], [(h1.shape[1], F32)], [], name=name)[0]


def _ple_bwd(dh2, pp, gl, name):
    def fn(d, b, c):
        gate = _sigmoid(c)
        return d * gate, d * b * gate * (1.0 - gate)

    return _rowcall(fn, [dh2, pp, gl], [], [(dh2.shape[1], BF16), (dh2.shape[1], BF16)], [], name=name)


def _loss_bwd(y, target, name):
    width = y.shape[1]

    def fn(a, t):
        d = a - t
        col = jnp.sum(d * d, axis=0, keepdims=True)
        part = col[:, 0:LANES]
        for j in range(1, width // LANES):
            part = part + col[:, j * LANES:(j + 1) * LANES]
        return d * (1.0 / width), part

    return _rowcall(fn, [y, target], [], [(width, F32)], [(1, LANES)], name=name)


CONV_TC = 256


def _shift_down(x, j):
    if j == 0:
        return x
    row = lax.broadcasted_iota(jnp.int32, x.shape, 0)
    return jnp.where(row >= j, pltpu.roll(x, j, 0), 0.0)


def _shift_up(x, j):
    if j == 0:
        return x
    n = x.shape[0]
    row = lax.broadcasted_iota(jnp.int32, x.shape, 0)
    return jnp.where(row < n - j, pltpu.roll(x, n - j, 0), 0.0)


def _conv_fwd(pzx, cw, cb, name):
    s_dim = pzx.shape[0]
    off = SSD_D_INNER // CONV_TC

    def body(x_ref, w_ref, b_ref, o_ref):
        x = x_ref[...]
        w = w_ref[...]
        y = b_ref[...] + w[3:4, :] * x
        for k in range(SSD_D_CONV - 1):
            y = y + w[k:k + 1, :] * _shift_down(x, SSD_D_CONV - 1 - k)
        o_ref[...] = y * _sigmoid(y)

    return pl.pallas_call(
        body, grid=(SSD_CONV_DIM // CONV_TC,),
        in_specs=[pl.BlockSpec((s_dim, CONV_TC), lambda j: (0, off + j)), pl.BlockSpec((SSD_D_CONV, CONV_TC), lambda j: (0, j)),
                  pl.BlockSpec((1, CONV_TC), lambda j: (0, j))],
        out_specs=pl.BlockSpec((s_dim, CONV_TC), lambda j: (0, j)),
        out_shape=jax.ShapeDtypeStruct((s_dim, SSD_CONV_DIM), F32),
        compiler_params=_params("parallel"), name=name,
    )(pzx, cw, cb)


def _conv_bwd(pzx, cw, cb, dact, name):
    s_dim = pzx.shape[0]
    off = SSD_D_INNER // CONV_TC

    def body(x_ref, w_ref, b_ref, d_ref, dx_ref, dw_ref, db_ref):
        x = x_ref[...]
        w = w_ref[...]
        xs = [_shift_down(x, SSD_D_CONV - 1 - k) for k in range(SSD_D_CONV)]
        y = b_ref[...]
        for k in range(SSD_D_CONV):
            y = y + w[k:k + 1, :] * xs[k]
        sg = _sigmoid(y)
        dy = d_ref[...] * (sg * (1.0 + y * (1.0 - sg)))
        dx = w[3:4, :] * dy
        for k in range(SSD_D_CONV - 1):
            dx = dx + w[k:k + 1, :] * _shift_up(dy, SSD_D_CONV - 1 - k)
        dx_ref[...] = dx.astype(dx_ref.dtype)
        for k in range(SSD_D_CONV):
            dw_ref[k:k + 1, :] = jnp.sum(dy * xs[k], axis=0, keepdims=True)
        db_ref[...] = jnp.sum(dy, axis=0, keepdims=True)

    col = pl.BlockSpec((s_dim, CONV_TC), lambda j: (0, j))
    return pl.pallas_call(
        body, grid=(SSD_CONV_DIM // CONV_TC,),
        in_specs=[pl.BlockSpec((s_dim, CONV_TC), lambda j: (0, off + j)), pl.BlockSpec((SSD_D_CONV, CONV_TC), lambda j: (0, j)),
                  pl.BlockSpec((1, CONV_TC), lambda j: (0, j)), col],
        out_specs=[col, pl.BlockSpec((SSD_D_CONV, CONV_TC), lambda j: (0, j)), pl.BlockSpec((1, CONV_TC), lambda j: (0, j))],
        out_shape=[jax.ShapeDtypeStruct((s_dim, SSD_CONV_DIM), BF16), jax.ShapeDtypeStruct((SSD_D_CONV, SSD_CONV_DIM), F32),
                   jax.ShapeDtypeStruct((1, SSD_CONV_DIM), F32)],
        compiler_params=_params("parallel"), name=name,
    )(pzx, cw, cb, dact)


def _ssd_step(xs, bm, cm, dtraw, bias, alog, dskip, st_in, z, gw, dot, cumsum):
    n = xs.shape[0]
    lane = lax.broadcasted_iota(jnp.int32, (1, LANES), 1)
    sub = lax.broadcasted_iota(jnp.int32, (LANES, 1), 0)
    left = (lane < 64).astype(F32)
    right = 1.0 - left
    top = (sub < 64).astype(F32)
    bot = 1.0 - top
    row = lax.broadcasted_iota(jnp.int32, (n, n), 0)
    colm = lax.broadcasted_iota(jnp.int32, (n, n), 1)
    causal = row >= colm

    dt = _softplus(dtraw + bias)
    adt = dt * (-jnp.exp(alog))
    acum = cumsum(adt)
    acum_t = acum.T
    last = jnp.sum(adt, axis=0, keepdims=True)
    scores = dot(cm, bm, "nt")

    def lane_of(v, h):
        return jnp.sum(v * (lane == h).astype(F32), axis=1, keepdims=True)

    ys, sts = [], []
    for pr in range(4):
        heads = (2 * pr, 2 * pr + 1)
        ac = [lane_of(acum, h) for h in heads]
        ar = [jnp.sum(acum_t * (sub == h).astype(F32), axis=0, keepdims=True) for h in heads]
        dth = [lane_of(dt, h) for h in heads]
        la = [lane_of(last, h) for h in heads]
        dk = [lane_of(dskip, h) for h in heads]
        x2 = xs[:, pr * LANES:(pr + 1) * LANES]
        xdt = x2 * (dth[0] * left + dth[1] * right)
        yd = None
        for i, side in enumerate((left, right)):
            decay = jnp.where(causal, jnp.exp(jnp.minimum(ac[i] - ar[i], 0.0)), 0.0)
            t = dot(scores * decay, xdt * side, "nn")
            yd = t if yd is None else yd + t
        st2 = st_in[pr * LANES:(pr + 1) * LANES, :]
        yo = dot(cm, st2, "nt") * (jnp.exp(ac[0]) * left + jnp.exp(ac[1]) * right)
        dte = jnp.exp(la[0] - ac[0]) * left + jnp.exp(la[1] - ac[1]) * right
        cs = dot(xdt * dte, bm, "tn")
        sts.append(st2 * (jnp.exp(la[0]) * top + jnp.exp(la[1]) * bot) + cs)
        ys.append(yd + yo + (dk[0] * left + dk[1] * right) * x2)
    y = jnp.concatenate(ys, axis=1)
    yg = y * (z * _sigmoid(z))
    yn = yg * lax.rsqrt(jnp.mean(yg * yg, axis=-1, keepdims=True) + GATED_NORM_EPS) * gw
    return yn, jnp.concatenate(sts, axis=0)


def _ssd_specs(n_chunks, rev):
    ci = (lambda c: n_chunks - 1 - c) if rev else (lambda c: c)
    n_x = SSD_D_INNER // LANES
    return dict(
        xs=pl.BlockSpec((SSD_CHUNK, SSD_GROUP_W), lambda g, c: (ci(c), g)),
        bm=pl.BlockSpec((SSD_CHUNK, LANES), lambda g, c: (ci(c), n_x + g)),
        cm=pl.BlockSpec((SSD_CHUNK, LANES), lambda g, c: (ci(c), n_x + SSD_N_GROUPS + g)),
        dt=pl.BlockSpec((None, SSD_CHUNK, LANES), lambda g, c: (g, ci(c), 0)),
        vec=pl.BlockSpec((None, 1, LANES), lambda g, c: (g, 0, 0)),
        z=pl.BlockSpec((SSD_CHUNK, SSD_GROUP_W), lambda g, c: (ci(c), g)),
        gw=pl.BlockSpec((1, SSD_GROUP_W), lambda g, c: (0, g)),
        st=pl.BlockSpec((None, None, SSD_GROUP_W, SSD_D_STATE), lambda g, c: (g, ci(c), 0, 0)),
    )


def _ssd_fwd(act, dtg, bias, alog, dskip, pzx, gw, name):
    s_dim = act.shape[0]
    n_chunks = s_dim // SSD_CHUNK
    sp = _ssd_specs(n_chunks, False)

    def body(xs, bm, cm, dt, b_ref, a_ref, d_ref, z, gw_ref, yn_ref, st_ref, state):
        @pl.when(pl.program_id(1) == 0)
        def _():
            state[...] = jnp.zeros_like(state)

        st_in = state[...]
        st_ref[...] = st_in
        yn, st_out = _ssd_step(xs[...], bm[...], cm[...], dt[...], b_ref[...], a_ref[...], d_ref[...], st_in, z[...], gw_ref[...],
                               _dot, _cumsum_rows_raw)
        yn_ref[...] = yn.astype(yn_ref.dtype)
        state[...] = st_out

    return pl.pallas_call(
        body, grid=(SSD_N_GROUPS, n_chunks),
        in_specs=[sp["xs"], sp["bm"], sp["cm"], sp["dt"], sp["vec"], sp["vec"], sp["vec"], sp["z"], sp["gw"]],
        out_specs=[sp["xs"], sp["st"]],
        out_shape=[jax.ShapeDtypeStruct((s_dim, SSD_D_INNER), BF16),
                   jax.ShapeDtypeStruct((SSD_N_GROUPS, n_chunks, SSD_GROUP_W, SSD_D_STATE), F32)],
        scratch_shapes=[pltpu.VMEM((SSD_GROUP_W, SSD_D_STATE), F32)],
        compiler_params=_params("parallel", "arbitrary"), name=name,
    )(act, act, act, dtg, bias, alog, dskip, pzx, gw)


def _ssd_bwd(act, dtg, bias, alog, dskip, pzx, gw, states, dyn, name):
    s_dim = act.shape[0]
    n_chunks = s_dim // SSD_CHUNK
    sp = _ssd_specs(n_chunks, True)
    rc = lambda c: n_chunks - 1 - c

    def body(xs, bm, cm, dt, b_ref, a_ref, d_ref, z, gw_ref, st_ref, dyn_ref,
             dxs_ref, dbm_ref, dcm_ref, ddt_ref, db_ref, da_ref, dd_ref, dz_ref, dgw_ref, dstate):
        first = pl.program_id(1) == 0

        @pl.when(first)
        def _():
            dstate[...] = jnp.zeros_like(dstate)
            db_ref[...] = jnp.zeros_like(db_ref)
            da_ref[...] = jnp.zeros_like(da_ref)
            dd_ref[...] = jnp.zeros_like(dd_ref)
            dgw_ref[...] = jnp.zeros_like(dgw_ref)

        fn = functools.partial(_ssd_step, dot=_gdot, cumsum=_cumsum_rows)
        _, vjp = jax.vjp(fn, xs[...], bm[...], cm[...], dt[...], b_ref[...], a_ref[...], d_ref[...], st_ref[...], z[...], gw_ref[...])
        dxs, dbm, dcm, ddt, db, da, dd, dst, dz, dgw = vjp((dyn_ref[...], dstate[...]))
        dxs_ref[...] = dxs
        dbm_ref[...] = dbm
        dcm_ref[...] = dcm
        ddt_ref[...] = ddt
        dz_ref[...] = dz.astype(dz_ref.dtype)
        db_ref[...] += db
        da_ref[...] += da
        dd_ref[...] += dd
        dgw_ref[...] += dgw
        dstate[...] = dst

    bc = pl.BlockSpec((SSD_CHUNK, LANES), lambda g, c: (rc(c), g))
    return pl.pallas_call(
        body, grid=(SSD_N_GROUPS, n_chunks),
        in_specs=[sp["xs"], sp["bm"], sp["cm"], sp["dt"], sp["vec"], sp["vec"], sp["vec"], sp["z"], sp["gw"], sp["st"], sp["xs"]],
        out_specs=[sp["xs"], bc, bc, sp["dt"], sp["vec"], sp["vec"], sp["vec"], sp["xs"], sp["gw"]],
        out_shape=[jax.ShapeDtypeStruct((s_dim, SSD_D_INNER), F32),
                   jax.ShapeDtypeStruct((s_dim, SSD_N_GROUPS * SSD_D_STATE), F32),
                   jax.ShapeDtypeStruct((s_dim, SSD_N_GROUPS * SSD_D_STATE), F32),
                   jax.ShapeDtypeStruct((SSD_N_GROUPS, s_dim, LANES), F32),
                   jax.ShapeDtypeStruct((SSD_N_GROUPS, 1, LANES), F32),
                   jax.ShapeDtypeStruct((SSD_N_GROUPS, 1, LANES), F32),
                   jax.ShapeDtypeStruct((SSD_N_GROUPS, 1, LANES), F32),
                   jax.ShapeDtypeStruct((s_dim, SSD_D_INNER), BF16),
                   jax.ShapeDtypeStruct((1, SSD_D_INNER), F32)],
        scratch_shapes=[pltpu.VMEM((SSD_GROUP_W, SSD_D_STATE), F32)],
        compiler_params=_params("arbitrary", "arbitrary"), name=name,
    )(act, act, act, dtg, bias, alog, dskip, pzx, gw, states, dyn)


SB_T = 128
SB_GROUP = 8
SB_WIDE = SB_GROUP * SB_T
SB_HB = 2
SB_SCALE = 1.0 / math.sqrt(SB_HEAD_DIM)


def _qknorm_fwd(proj, qw, kw, name, tm=512):
    s_dim = proj.shape[0]

    def body(q_ref, k_ref, v_ref, qw_ref, kw_ref, qo, ko, vo):
        qo[...] = _rms(q_ref[...], qw_ref[...], NORM_EPS).astype(BF16)
        ko[...] = _rms(k_ref[...], kw_ref[...], NORM_EPS).astype(BF16)
        vo[...] = v_ref[...].astype(BF16)

    blk = lambda o: pl.BlockSpec((tm, SB_HEAD_DIM), lambda i, h: (i, o + h))
    vec = pl.BlockSpec((1, SB_HEAD_DIM), lambda i, h: (0, 0))
    return pl.pallas_call(
        body, grid=(s_dim // tm, SB_N_HEADS),
        in_specs=[blk(0), blk(SB_N_HEADS), blk(2 * SB_N_HEADS), vec, vec],
        out_specs=[blk(0)] * 3,
        out_shape=[jax.ShapeDtypeStruct((s_dim, SB_WIDTH), BF16)] * 3,
        compiler_params=_params("parallel", "parallel"), name=name,
    )(proj, proj, proj, qw, kw)


def _qknorm_bwd(proj, qw, kw, dqn, dkn, name, tm=512):
    s_dim = proj.shape[0]

    def body(q_ref, k_ref, dq_ref, dk_ref, qw_ref, kw_ref, dqo, dko, dqw, dkw):
        @pl.when((pl.program_id(0) == 0) & (pl.program_id(1) == 0))
        def _():
            dqw[...] = jnp.zeros_like(dqw)
            dkw[...] = jnp.zeros_like(dkw)

        fn = lambda a, b: _rms(a, b, NORM_EPS)
        _, vq = jax.vjp(fn, q_ref[...], qw_ref[...])
        dq, dw = vq(dq_ref[...])
        dqo[...] = dq.astype(BF16)
        dqw[...] += dw
        _, vk = jax.vjp(fn, k_ref[...], kw_ref[...])
        dk, dw = vk(dk_ref[...])
        dko[...] = dk.astype(BF16)
        dkw[...] += dw

    blk = lambda o: pl.BlockSpec((tm, SB_HEAD_DIM), lambda i, h: (i, o + h))
    vec = pl.BlockSpec((1, SB_HEAD_DIM), lambda i, h: (0, 0))
    return pl.pallas_call(
        body, grid=(s_dim // tm, SB_N_HEADS),
        in_specs=[blk(0), blk(SB_N_HEADS), blk(0), blk(0), vec, vec],
        out_specs=[blk(0), blk(0), vec, vec],
        out_shape=[jax.ShapeDtypeStruct((s_dim, SB_WIDTH), BF16)] * 2 + [jax.ShapeDtypeStruct((1, SB_HEAD_DIM), F32)] * 2,
        compiler_params=_params("arbitrary", "arbitrary"), name=name,
    )(proj, proj, dqn, dkn, qw, kw)


def _sb_logits(q, k, strict):
    z = _dot(q, k, "nt") * SB_SCALE
    lb = jnp.minimum(z, 0.0) - jnp.log(1.0 + jnp.exp(-jnp.abs(z)))
    lm = lb - z
    if strict is not None:
        lm = jnp.where(strict, lm, 0.0)
    return lb, lm


def _sb_strict(qi, grp):
    r = lax.broadcasted_iota(jnp.int32, (SB_T, SB_WIDE), 0) + qi * SB_T
    c = lax.broadcasted_iota(jnp.int32, (SB_T, SB_WIDE), 1) + grp * SB_WIDE
    return c < r


def _head_lanes(hh):
    return slice(hh * SB_HEAD_DIM, (hh + 1) * SB_HEAD_DIM)


def _sb_fwd(qn, kn, vb, proj, name):
    s_dim = qn.shape[0]
    nq = s_dim // SB_T
    assert nq % SB_GROUP == 0

    def body(q_ref, k_ref, v_ref, g_ref, og_ref, o_ref, t_ref):
        qi = pl.program_id(1)
        top = qi // SB_GROUP
        after = _tri(SB_T, True, strict=True)
        qs = [q_ref[:, _head_lanes(hh)] for hh in range(SB_HB)]

        def step(grp, masked, carries):
            start = pl.multiple_of(grp * SB_WIDE, SB_WIDE)
            strict = _sb_strict(qi, grp) if masked else None
            out = []
            for hh in range(SB_HB):
                o_acc, cr = carries[hh]
                k = k_ref[pl.ds(start, SB_WIDE), _head_lanes(hh)]
                v = v_ref[pl.ds(start, SB_WIDE), _head_lanes(hh)]
                lb, lm = _sb_logits(qs[hh], k, strict)
                rest = [None] * SB_GROUP
                for t in reversed(range(SB_GROUP)):
                    lm_t = lm[:, t * SB_T:(t + 1) * SB_T]
                    rest[t] = cr + _split_dot(lm_t, after, 2, True)
                    cr = cr + jnp.sum(lm_t, axis=1, keepdims=True)
                a = jnp.exp(lb + jnp.concatenate(rest, axis=1))
                if masked:
                    a = jnp.where(strict, a, 0.0)
                out.append((o_acc + _dot(a, v), cr))
            return tuple(out)

        init = tuple((jnp.zeros((SB_T, SB_HEAD_DIM), F32), jnp.zeros((SB_T, 1), F32)) for _ in range(SB_HB))
        carries = step(top, True, init)
        carries = lax.fori_loop(0, top, lambda i, c: step(top - 1 - i, False, c), carries)
        for hh in range(SB_HB):
            o, tot = carries[hh]
            g = g_ref[:, _head_lanes(hh)]
            o_ref[:, _head_lanes(hh)] = o
            og_ref[:, _head_lanes(hh)] = (o * (g * _sigmoid(g))).astype(og_ref.dtype)
            t_ref[hh] = jnp.broadcast_to(tot, (SB_T, LANES))

    wide = SB_HB * SB_HEAD_DIM
    qb = pl.BlockSpec((SB_T, wide), lambda h, i: (i, h))
    kv = pl.BlockSpec((s_dim, wide), lambda h, i: (0, h))
    return pl.pallas_call(
        body, grid=(SB_N_HEADS // SB_HB, nq),
        in_specs=[qb, kv, kv, pl.BlockSpec((SB_T, wide), lambda h, i: (i, 3 * SB_N_HEADS // SB_HB + h))],
        out_specs=[qb, qb, pl.BlockSpec((SB_HB, SB_T, LANES), lambda h, i: (h, i, 0))],
        out_shape=[jax.ShapeDtypeStruct((s_dim, SB_WIDTH), BF16), jax.ShapeDtypeStruct((s_dim, SB_WIDTH), F32),
                   jax.ShapeDtypeStruct((SB_N_HEADS, s_dim, LANES), F32)],
        compiler_params=_params("parallel", "arbitrary"), name=name,
    )(qn, kn, vb, proj)


def _sb_bwd(qn, kn, vb, proj, o, tot, dog, name):
    s_dim = qn.shape[0]
    nq = s_dim // SB_T
    assert nq % SB_GROUP == 0

    def body(q_ref, k_ref, v_ref, g_ref, o_ref, t_ref, dog_ref, dq_ref, dk_ref, dv_ref, dvb_ref, dg_ref):
        qi = pl.program_id(1)
        top = qi // SB_GROUP

        @pl.when(qi == 0)
        def _():
            dk_ref[...] = jnp.zeros_like(dk_ref)
            dv_ref[...] = jnp.zeros_like(dv_ref)

        after = _tri(SB_T, True, strict=True)
        before = _tri(SB_T, False, strict=True)
        qs, dos, totals = [], [], []
        for hh in range(SB_HB):
            g = g_ref[:, _head_lanes(hh)]
            sg = _sigmoid(g)
            dog_v = dog_ref[:, _head_lanes(hh)]
            dg_ref[:, _head_lanes(hh)] = (dog_v * o_ref[:, _head_lanes(hh)] * (sg * (1.0 + g * (1.0 - sg)))).astype(dg_ref.dtype)
            dos.append((dog_v * (g * sg)).astype(BF16))
            qs.append(q_ref[:, _head_lanes(hh)])
            totals.append(t_ref[hh][:, 0:1])

        def step(grp, masked, carries):
            start = pl.multiple_of(grp * SB_WIDE, SB_WIDE)
            strict = _sb_strict(qi, grp) if masked else None
            out = []
            for hh in range(SB_HB):
                dq_acc, cp, ce = carries[hh]
                q, do = qs[hh], dos[hh]
                k = k_ref[pl.ds(start, SB_WIDE), _head_lanes(hh)]
                v = v_ref[pl.ds(start, SB_WIDE), _head_lanes(hh)]
                lb, lm = _sb_logits(q, k, strict)
                rest = []
                for t in range(SB_GROUP):
                    lm_t = lm[:, t * SB_T:(t + 1) * SB_T]
                    cp = cp + jnp.sum(lm_t, axis=1, keepdims=True)
                    rest.append((totals[hh] - cp) + _split_dot(lm_t, after, 2, True))
                a = jnp.exp(lb + jnp.concatenate(rest, axis=1))
                if masked:
                    a = jnp.where(strict, a, 0.0)
                e = a * _dot(do, v, "nt")
                excl = []
                for t in range(SB_GROUP):
                    e_t = e[:, t * SB_T:(t + 1) * SB_T]
                    excl.append(ce + _split_dot(e_t, before, 2, True))
                    ce = ce + jnp.sum(e_t, axis=1, keepdims=True)
                eex = jnp.concatenate(excl, axis=1)
                if masked:
                    eex = jnp.where(strict, eex, 0.0)
                sig = jnp.exp(lb)
                dz = (e * (1.0 - sig) - eex * sig) * SB_SCALE
                dv_ref[pl.ds(start, SB_WIDE), _head_lanes(hh)] += _dot(a, do, "tn")
                dk_ref[pl.ds(start, SB_WIDE), _head_lanes(hh)] += _dot(dz, q, "tn")
                out.append((dq_acc + _dot(dz, k), cp, ce))
            return tuple(out)

        zero = jnp.zeros((SB_T, 1), F32)
        init = tuple((jnp.zeros((SB_T, SB_HEAD_DIM), F32), zero, zero) for _ in range(SB_HB))
        carries = lax.fori_loop(0, top, lambda i, c: step(i, False, c), init)
        carries = step(top, True, carries)
        for hh in range(SB_HB):
            dq_ref[:, _head_lanes(hh)] = carries[hh][0]

        @pl.when(qi == nq - 1)
        def _():
            dvb_ref[...] = dv_ref[...].astype(BF16)

    wide = SB_HB * SB_HEAD_DIM
    qb = pl.BlockSpec((SB_T, wide), lambda h, i: (i, h))
    kv = pl.BlockSpec((s_dim, wide), lambda h, i: (0, h))
    return pl.pallas_call(
        body, grid=(SB_N_HEADS // SB_HB, nq),
        in_specs=[qb, kv, kv, pl.BlockSpec((SB_T, wide), lambda h, i: (i, 3 * SB_N_HEADS // SB_HB + h)), qb,
                  pl.BlockSpec((SB_HB, SB_T, LANES), lambda h, i: (h, i, 0)), qb],
        out_specs=[qb, kv, kv, kv, qb],
        out_shape=[jax.ShapeDtypeStruct((s_dim, SB_WIDTH), F32), jax.ShapeDtypeStruct((s_dim, SB_WIDTH), F32),
                   jax.ShapeDtypeStruct((s_dim, SB_WIDTH), F32), jax.ShapeDtypeStruct((s_dim, SB_WIDTH), BF16),
                   jax.ShapeDtypeStruct((s_dim, SB_WIDTH), BF16)],
        compiler_params=_params("parallel", "arbitrary"), name=name,
    )(qn, kn, vb, proj, o, tot, dog)


def _adamw_math(w, g, m, v):
    m = ADAM_B1 * m + (1.0 - ADAM_B1) * g
    v = ADAM_B2 * v + (1.0 - ADAM_B2) * (g * g)
    m_hat = m / (1.0 - ADAM_B1 ** ADAM_STEP)
    v_hat = v / (1.0 - ADAM_B2 ** ADAM_STEP)
    delta = -ADAM_LR * (m_hat / (jnp.sqrt(v_hat) + ADAM_EPS) + ADAM_WD * w)
    return delta, m, v


def _row_block(rows, cols, itemsize=4, limit=1 << 20):
    tr = rows
    while tr * cols * itemsize > limit and tr % (2 * BF16_ROWS) == 0:
        tr //= 2
    return tr


def _adamw(w, g, m, v, name):
    n, rows, cols = w.shape
    tr = _row_block(rows, cols)

    def body(w_ref, g_ref, m_ref, v_ref, d_out, m_out, v_out):
        d, m_new, v_new = _adamw_math(w_ref[...], g_ref[...], m_ref[...], v_ref[...])
        d_out[...] = d
        m_out[...] = m_new
        v_out[...] = v_new

    blk = pl.BlockSpec((None, tr, cols), lambda i, j: (i, j, 0))
    return pl.pallas_call(
        body, grid=(n, rows // tr), in_specs=[blk] * 4, out_specs=[blk] * 3,
        out_shape=[jax.ShapeDtypeStruct(w.shape, F32)] * 3,
        compiler_params=_params("parallel", "parallel"), name=name,
    )(w, g, m, v)


_FLIPS = ((1, 0), (0, 1), (1, 1))
_ANY = pl.BlockSpec(memory_space=pl.ANY)


def _place():
    return lax.axis_index("x"), lax.axis_index("y"), lax.axis_index("c")


def _flip(v, f):
    return 1 - v if f else v


def _half_rows(ref, lead, hc, hr):
    return ref.at[(*lead, pl.ds(pl.multiple_of(hc * hr, BF16_ROWS), hr), slice(None))]


def _allgather_big(shards):
    n_arr = len(shards)

    def body(*refs):
        ins, outs = refs[:n_arr], refs[n_arr:2 * n_arr]
        send_sems, recv_sems = refs[2 * n_arr:]
        x, y, c = _place()
        me = 2 * x + y
        sibling = (x, y, 1 - c)
        all_l = slice(None)

        def copy(a, k, src, dst, to):
            return pltpu.make_async_remote_copy(src_ref=src, dst_ref=dst, send_sem=send_sems.at[a, k], recv_sem=recv_sems.at[a, k],
                                                device_id=to, device_id_type=MESH)

        hrs = [s.shape[1] // 2 for s in shards]
        mine = [_half_rows(ins[a], (all_l,), c, hrs[a]) for a in range(n_arr)]
        started = []
        for a in range(n_arr):
            for j, (fx, fy) in enumerate(_FLIPS):
                cp = copy(a, j, mine[a], _half_rows(outs[a], (me, all_l), c, hrs[a]), (_flip(x, fx), _flip(y, fy), c))
                cp.start()
                started.append(cp)
        for j, (fx, fy) in enumerate(_FLIPS):
            chip = 2 * _flip(x, fx) + _flip(y, fy)
            for a in range(n_arr):
                landed = _half_rows(outs[a], (chip, all_l), c, hrs[a])
                copy(a, j, mine[a], landed, sibling).wait_recv()
                cp = copy(a, 3 + j, landed, landed, sibling)
                cp.start()
                started.append(cp)
        for j, (fx, fy) in enumerate(_FLIPS):
            chip = 2 * _flip(x, fx) + _flip(y, fy)
            for a in range(n_arr):
                copy(a, 3 + j, mine[a], _half_rows(outs[a], (chip, all_l), 1 - c, hrs[a]), sibling).wait_recv()
        for cp in started:
            cp.wait_send()

    return pl.pallas_call(
        body, in_specs=[_ANY] * n_arr, out_specs=[_ANY] * n_arr,
        out_shape=[jax.ShapeDtypeStruct((4,) + s.shape, s.dtype) for s in shards],
        scratch_shapes=[pltpu.SemaphoreType.DMA((n_arr, 6)), pltpu.SemaphoreType.DMA((n_arr, 6))],
        name="allgather_big",
    )(*shards)


def _pair_exchange(grads):
    n_arr = len(grads)

    def body(*refs):
        ins, outs = refs[:n_arr], refs[n_arr:2 * n_arr]
        send_sems, recv_sems = refs[2 * n_arr:]
        x, y, c = _place()
        cps = []
        for a in range(n_arr):
            hr = grads[a].shape[2] // 2
            cp = pltpu.make_async_remote_copy(
                src_ref=_half_rows(ins[a], (slice(None), slice(None)), 1 - c, hr), dst_ref=outs[a], send_sem=send_sems.at[a],
                recv_sem=recv_sems.at[a], device_id=(x, y, 1 - c), device_id_type=MESH)
            cp.start()
            cps.append(cp)
        for cp in cps:
            cp.wait()

    return pl.pallas_call(
        body, in_specs=[_ANY] * n_arr, out_specs=[_ANY] * n_arr,
        out_shape=[jax.ShapeDtypeStruct(g.shape[:2] + (g.shape[2] // 2, g.shape[3]), g.dtype) for g in grads],
        scratch_shapes=[pltpu.SemaphoreType.DMA((n_arr,)), pltpu.SemaphoreType.DMA((n_arr,))], name="rs_pair_exchange",
    )(*grads)


def _pair_sum(g, got, place, name):
    _, layers, hr, cols = got.shape
    tr = _row_block(hr, cols)
    per = hr // tr

    def body(place_ref, g_ref, r_ref, o_ref):
        o_ref[...] = (g_ref[...].astype(F32) + r_ref[...].astype(F32)).astype(o_ref.dtype)

    blk = pl.BlockSpec((None, None, tr, cols), lambda k, l, i, pr: (k, l, i, 0))
    return pl.pallas_call(
        body,
        grid_spec=pltpu.PrefetchScalarGridSpec(
            num_scalar_prefetch=1, grid=(4, layers, per),
            in_specs=[pl.BlockSpec((None, None, tr, cols), lambda k, l, i, pr: (k, l, pr[1] * per + i, 0)), blk],
            out_specs=blk),
        out_shape=jax.ShapeDtypeStruct(got.shape, BF16),
        compiler_params=_params("parallel", "parallel", "parallel"), name=name,
    )(place, g, got)


def _chip_scatter(pairs):
    n_arr = len(pairs)

    def body(*refs):
        ins, outs = refs[:n_arr], refs[n_arr:2 * n_arr]
        send_sems, recv_sems = refs[2 * n_arr:]
        x, y, c = _place()
        cps = []
        for a in range(n_arr):
            for j, (fx, fy) in enumerate(_FLIPS):
                tx, ty = _flip(x, fx), _flip(y, fy)
                cp = pltpu.make_async_remote_copy(src_ref=ins[a].at[2 * tx + ty], dst_ref=outs[a].at[j], send_sem=send_sems.at[a, j],
                                                  recv_sem=recv_sems.at[a, j], device_id=(tx, ty, c), device_id_type=MESH)
                cp.start()
                cps.append(cp)
        for cp in cps:
            cp.wait()

    return pl.pallas_call(
        body, in_specs=[_ANY] * n_arr, out_specs=[_ANY] * n_arr,
        out_shape=[jax.ShapeDtypeStruct((3,) + p.shape[1:], p.dtype) for p in pairs],
        scratch_shapes=[pltpu.SemaphoreType.DMA((n_arr, 3)), pltpu.SemaphoreType.DMA((n_arr, 3))], name="rs_chip_scatter",
    )(*pairs)


def _chip_sum(p, got, place, name):
    _, layers, hr, cols = p.shape
    tr = _row_block(hr, cols)
    per = hr // tr

    def body(place_ref, p_ref, r_ref, o_ref):
        acc = p_ref[...].astype(F32)
        for j in range(3):
            acc = acc + r_ref[j].astype(F32)
        o_ref[...] = acc

    return pl.pallas_call(
        body,
        grid_spec=pltpu.PrefetchScalarGridSpec(
            num_scalar_prefetch=1, grid=(layers, per),
            in_specs=[pl.BlockSpec((None, None, tr, cols), lambda l, i, pr: (pr[0], l, i, 0)),
                      pl.BlockSpec((3, None, tr, cols), lambda l, i, pr: (0, l, i, 0))],
            out_specs=pl.BlockSpec((None, tr, cols), lambda l, i, pr: (l, pr[1] * per + i, 0))),
        out_shape=jax.ShapeDtypeStruct((layers, 2 * hr, cols), F32),
        compiler_params=_params("parallel", "parallel"), name=name,
    )(place, p, got)


def _pair_gather(halves):
    n_arr = len(halves)

    def body(*refs):
        bufs = refs[n_arr:2 * n_arr]
        send_sems, recv_sems = refs[2 * n_arr:]
        x, y, c = _place()
        cps = []
        for a in range(n_arr):
            hr = halves[a].shape[1] // 2
            mine = _half_rows(bufs[a], (slice(None),), c, hr)
            cp = pltpu.make_async_remote_copy(src_ref=mine, dst_ref=mine, send_sem=send_sems.at[a], recv_sem=recv_sems.at[a],
                                              device_id=(x, y, 1 - c), device_id_type=MESH)
            cp.start()
            cps.append(cp)
        for a in range(n_arr):
            hr = halves[a].shape[1] // 2
            theirs = _half_rows(bufs[a], (slice(None),), 1 - c, hr)
            pltpu.make_async_remote_copy(src_ref=theirs, dst_ref=theirs, send_sem=send_sems.at[a], recv_sem=recv_sems.at[a],
                                         device_id=(x, y, 1 - c), device_id_type=MESH).wait_recv()
        for cp in cps:
            cp.wait_send()

    return pl.pallas_call(
        body, in_specs=[_ANY] * n_arr, out_specs=[_ANY] * n_arr,
        out_shape=[jax.ShapeDtypeStruct(h.shape, h.dtype) for h in halves],
        input_output_aliases={a: a for a in range(n_arr)},
        scratch_shapes=[pltpu.SemaphoreType.DMA((n_arr,)), pltpu.SemaphoreType.DMA((n_arr,))], name="rs_pair_gather",
    )(*halves)


def _allreduce_small(v, name):
    rows, cols = v.shape

    def body(v_ref, o_ref, buf, send_sems, recv_sems):
        x, y, c = _place()
        me = 4 * x + 2 * y + c
        buf[0] = v_ref[...]
        cps = []
        for k in range(1, 8):
            kx, ky, kc = (k >> 2) & 1, (k >> 1) & 1, k & 1
            cp = pltpu.make_async_remote_copy(src_ref=v_ref, dst_ref=buf.at[k], send_sem=send_sems.at[k - 1], recv_sem=recv_sems.at[k - 1],
                                              device_id=(_flip(x, kx), _flip(y, ky), _flip(c, kc)), device_id_type=MESH)
            cp.start()
            cps.append(cp)
        for cp in cps:
            cp.wait()
        acc = buf[me]
        for d in range(1, 8):
            acc = acc + buf[jnp.bitwise_xor(d, me)]
        o_ref[...] = acc

    vm = pl.BlockSpec(memory_space=pltpu.VMEM)
    return pl.pallas_call(
        body, in_specs=[vm], out_specs=vm, out_shape=jax.ShapeDtypeStruct((rows, cols), F32),
        scratch_shapes=[pltpu.VMEM((8, rows, cols), F32), pltpu.SemaphoreType.DMA((7,)), pltpu.SemaphoreType.DMA((7,))],
        name=name,
    )(v)


def _cols_from_shards(g):
    _, n, r, cs = g.shape
    return g.transpose(1, 2, 0, 3).reshape(n, r, 4 * cs)


def _shards_from_cols(w):
    n, r, cols = w.shape
    return w.reshape(n, r, 4, cols // 4).transpose(2, 0, 1, 3)


def _pad_lanes(a):
    return jnp.pad(a, ((0, 0), (0, LANES - a.shape[1])))


def _group_lanes(v):
    return jnp.pad(v.reshape(SSD_N_GROUPS, 1, 8), ((0, 0), (0, 0), (0, LANES - 8)))


def kernel(x, p, norm_w, ssd_in_w, ssd_conv_w, ssd_conv_b, ssd_dt_bias, ssd_a_log, ssd_d, ssd_gnorm_w, ssd_out_w, sb_in_w, sb_qn_w, sb_kn_w, sb_out_w, ple_norm_w, ple_gate_w, ple_proj_w, loss_target, m_norm_w, m_ssd_in_w, m_ssd_conv_w, m_ssd_conv_b, m_ssd_dt_bias, m_ssd_a_log, m_ssd_d, m_ssd_gnorm_w, m_ssd_out_w, m_sb_in_w, m_sb_qn_w, m_sb_kn_w, m_sb_out_w, m_ple_norm_w, m_ple_gate_w, m_ple_proj_w, v_norm_w, v_ssd_in_w, v_ssd_conv_w, v_ssd_conv_b, v_ssd_dt_bias, v_ssd_a_log, v_ssd_d, v_ssd_gnorm_w, v_ssd_out_w, v_sb_in_w, v_sb_qn_w, v_sb_kn_w, v_sb_out_w, v_ple_norm_w, v_ple_gate_w, v_ple_proj_w):
    w_in = dict(norm_w=norm_w, ssd_in_w=ssd_in_w, ssd_conv_w=ssd_conv_w, ssd_conv_b=ssd_conv_b, ssd_dt_bias=ssd_dt_bias,
                ssd_a_log=ssd_a_log, ssd_d=ssd_d, ssd_gnorm_w=ssd_gnorm_w, ssd_out_w=ssd_out_w, sb_in_w=sb_in_w, sb_qn_w=sb_qn_w,
                sb_kn_w=sb_kn_w, sb_out_w=sb_out_w, ple_norm_w=ple_norm_w, ple_gate_w=ple_gate_w, ple_proj_w=ple_proj_w)
    m_in = dict(norm_w=m_norm_w, ssd_in_w=m_ssd_in_w, ssd_conv_w=m_ssd_conv_w, ssd_conv_b=m_ssd_conv_b, ssd_dt_bias=m_ssd_dt_bias,
                ssd_a_log=m_ssd_a_log, ssd_d=m_ssd_d, ssd_gnorm_w=m_ssd_gnorm_w, ssd_out_w=m_ssd_out_w, sb_in_w=m_sb_in_w,
                sb_qn_w=m_sb_qn_w, sb_kn_w=m_sb_kn_w, sb_out_w=m_sb_out_w, ple_norm_w=m_ple_norm_w, ple_gate_w=m_ple_gate_w,
                ple_proj_w=m_ple_proj_w)
    v_in = dict(norm_w=v_norm_w, ssd_in_w=v_ssd_in_w, ssd_conv_w=v_ssd_conv_w, ssd_conv_b=v_ssd_conv_b, ssd_dt_bias=v_ssd_dt_bias,
                ssd_a_log=v_ssd_a_log, ssd_d=v_ssd_d, ssd_gnorm_w=v_ssd_gnorm_w, ssd_out_w=v_ssd_out_w, sb_in_w=v_sb_in_w,
                sb_qn_w=v_sb_qn_w, sb_kn_w=v_sb_kn_w, sb_out_w=v_sb_out_w, ple_norm_w=v_ple_norm_w, ple_gate_w=v_ple_gate_w,
                ple_proj_w=v_ple_proj_w)
    ix, iy, ic = lax.axis_index("x"), lax.axis_index("y"), lax.axis_index("c")
    chip = (2 * ix + iy).astype(jnp.int32)
    place = jnp.stack([chip, ic.astype(jnp.int32)])
    zero = jnp.zeros((), jnp.int32)
    big_names = [n for n, _, _ in _BIG]
    layers_of = {n: s[0] for n, s, _ in _BIG}
    cut_of = {n: cut for n, _, cut in _BIG}

    mine = [w_in[n].astype(BF16) for n in big_names]
    gw = {n: lax.dynamic_update_slice(g, s[None], (chip, zero, zero, zero)) for n, g, s in zip(big_names, _allgather_big(mine), mine)}
    ssd_in_full = _cols_from_shards(gw["ssd_in_w"])
    onehot = (jnp.arange(4) == chip).astype(F32) * (ic == 0).astype(F32)
    cw_mine = onehot[:, None, None, None] * ssd_conv_w[None]
    cw_full = _allreduce_small(cw_mine.transpose(1, 2, 0, 3).reshape(-1, LANES), "gather_conv_w").reshape(2, SSD_D_CONV, SSD_CONV_DIM)

    def wmm(a, name, layer, *, dn="nn", res=None, call):
        return _matmul(a, gw[name], dn=dn, res=res, b_lay=(cut_of[name], layer), name=call)

    h = x[0]
    target = loss_target[0]
    saved = []
    for i in range(DEPTH):
        j = i // 2
        nw = norm_w[i:i + 1]
        pw = ple_norm_w[i:i + 1]
        s = dict(h=h)
        u = _rms_fwd(h, nw, f"rms_{i}")
        s["u"] = u
        if i % 2 == 0:
            w_all = ssd_in_full[j]
            w_zx = w_all[:, :SSD_D_INNER + SSD_CONV_DIM]
            w_dt = _pad_lanes(w_all[:, SSD_D_INNER + SSD_CONV_DIM:])
            pzx = _matmul(u, w_zx, name=f"ssd_in_{i}")
            pdt = _matmul(u, w_dt, name=f"ssd_indt_{i}")
            act = _conv_fwd(pzx, cw_full[j], ssd_conv_b[j:j + 1], f"conv_{i}")
            dtg = jnp.pad(pdt[:, :SSD_N_HEADS].reshape(-1, SSD_N_GROUPS, 8).transpose(1, 0, 2), ((0, 0), (0, 0), (0, LANES - 8)))
            vecs = (_group_lanes(ssd_dt_bias[j]), _group_lanes(ssd_a_log[j]), _group_lanes(ssd_d[j]))
            yn, states = _ssd_fwd(act, dtg, *vecs, pzx, ssd_gnorm_w[j:j + 1], f"ssd_{i}")
            s.update(w_zx=w_zx, w_dt=w_dt, pzx=pzx, act=act, dtg=dtg, vecs=vecs, yn=yn, states=states)
            h1 = wmm(yn, "ssd_out_w", j, res=h, call=f"ssd_out_{i}")
        else:
            proj = wmm(u, "sb_in_w", j, call=f"sb_in_{i}")
            qn, kn, vb = _qknorm_fwd(proj, sb_qn_w[j:j + 1], sb_kn_w[j:j + 1], f"qknorm_{i}")
            og, o, tot = _sb_fwd(qn, kn, vb, proj, f"sb_{i}")
            s.update(proj=proj, qn=qn, kn=kn, vb=vb, og=og, o=o, tot=tot)
            h1 = wmm(og, "sb_out_w", j, res=h, call=f"sb_out_{i}")
        n2 = _rms_fwd(h1, pw, f"ple_rms_{i}")
        gl = wmm(n2, "ple_gate_w", i, call=f"ple_gate_{i}")
        pp = wmm(p[i, 0], "ple_proj_w", i, call=f"ple_proj_{i}")
        h = _ple_fwd(h1, pp, gl, f"ple_{i}")
        s.update(h1=h1, n2=n2, gl=gl, pp=pp)
        saved.append(s)

    dh, loss_lanes = _loss_bwd(h, target, "loss")

    gb = {n: None for n in big_names}
    d_ssd_in = [None, None]
    gsmall = {n: [None] * s[0] for n, s in _SMALL}
    g_conv_w = [None, None]

    def wgrad(a, b, name, layer, call):
        gb[name] = _matmul(a, b, dn="tn", out_dtype=BF16, o_lay=(cut_of[name], layer, layers_of[name]), o_buf=gb[name], name=call)

    for i in reversed(range(DEPTH)):
        j = i // 2
        s = saved[i]
        nw = norm_w[i:i + 1]
        pw = ple_norm_w[i:i + 1]
        dpp, dgl = _ple_bwd(dh, s["pp"], s["gl"], f"ple_bwd_{i}")
        wgrad(p[i, 0], dpp, "ple_proj_w", i, f"d_ple_proj_{i}")
        wgrad(s["n2"], dgl, "ple_gate_w", i, f"d_ple_gate_{i}")
        dn2 = wmm(dgl, "ple_gate_w", i, dn="nt", call=f"ple_gate_bwd_{i}")
        dh1, dpw = _rms_bwd(s["h1"], pw, dn2, dh, f"ple_rms_bwd_{i}")
        gsmall["ple_norm_w"][i] = dpw
        if i % 2 == 0:
            wgrad(s["yn"], dh1, "ssd_out_w", j, f"d_ssd_out_{i}")
            dyn = wmm(dh1, "ssd_out_w", j, dn="nt", call=f"ssd_out_bwd_{i}")
            dxs, dbm, dcm, ddtg, dbias, dalog, ddsk, dz, dgw = _ssd_bwd(
                s["act"], s["dtg"], *s["vecs"], s["pzx"], ssd_gnorm_w[j:j + 1], s["states"], dyn, f"ssd_bwd_{i}")
            dact = jnp.concatenate([dxs, dbm, dcm], axis=1)
            dxbc, dcw, dcb = _conv_bwd(s["pzx"], cw_full[j], ssd_conv_b[j:j + 1], dact, f"conv_bwd_{i}")
            dzx = jnp.concatenate([dz, dxbc], axis=1)
            ddt = _pad_lanes(ddtg[:, :, :8].transpose(1, 0, 2).reshape(-1, SSD_N_HEADS)).astype(BF16)
            du = _matmul(dzx, s["w_zx"], dn="nt", name=f"ssd_in_bwd_{i}")
            du = _matmul(ddt, s["w_dt"], dn="nt", res=du, name=f"ssd_indt_bwd_{i}")
            dw_zx = _matmul(s["u"], dzx, dn="tn", out_dtype=BF16, name=f"d_ssd_in_{i}")
            dw_dt = _matmul(s["u"], ddt, dn="tn", out_dtype=BF16, name=f"d_ssd_indt_{i}")
            d_ssd_in[j] = jnp.concatenate([dw_zx, dw_dt[:, :SSD_N_HEADS]], axis=1)
            g_conv_w[j] = dcw
            gsmall["ssd_conv_b"][j] = dcb
            gsmall["ssd_dt_bias"][j] = dbias[:, 0, :8].reshape(1, SSD_N_HEADS)
            gsmall["ssd_a_log"][j] = dalog[:, 0, :8].reshape(1, SSD_N_HEADS)
            gsmall["ssd_d"][j] = ddsk[:, 0, :8].reshape(1, SSD_N_HEADS)
            gsmall["ssd_gnorm_w"][j] = dgw
        else:
            wgrad(s["og"], dh1, "sb_out_w", j, f"d_sb_out_{i}")
            dog = wmm(dh1, "sb_out_w", j, dn="nt", call=f"sb_out_bwd_{i}")
            dqn, dkn, _, dvb, dg = _sb_bwd(s["qn"], s["kn"], s["vb"], s["proj"], s["o"], s["tot"], dog, f"sb_bwd_{i}")
            dq, dk, dqw, dkw = _qknorm_bwd(s["proj"], sb_qn_w[j:j + 1], sb_kn_w[j:j + 1], dqn, dkn, f"qknorm_bwd_{i}")
            dproj = jnp.concatenate([dq, dk, dvb, dg], axis=1)
            du = wmm(dproj, "sb_in_w", j, dn="nt", call=f"sb_in_bwd_{i}")
            wgrad(s["u"], dproj, "sb_in_w", j, f"d_sb_in_{i}")
            gsmall["sb_qn_w"][j] = dqw
            gsmall["sb_kn_w"][j] = dkw
        dh, dnw = _rms_bwd(s["h"], nw, du, dh1, f"rms_bwd_{i}")
        gsmall["norm_w"][i] = dnw
    grad_x = dh[None]
    gb["ssd_in_w"] = _shards_from_cols(jnp.stack(d_ssd_in))

    g_list = [gb[n] for n in big_names]
    pairs = [_pair_sum(g, r, place, f"rs_pair_sum_{n}") for n, g, r in zip(big_names, g_list, _pair_exchange(g_list))]
    halves = [_chip_sum(q, r, place, f"rs_chip_sum_{n}") for n, q, r in zip(big_names, pairs, _chip_scatter(pairs))]
    g_big = dict(zip(big_names, _pair_gather(halves)))

    small_parts = [jnp.concatenate(gsmall[n], axis=0).reshape(-1) for n, _ in _SMALL]
    small_parts.append(jnp.stack(g_conv_w).reshape(-1))
    small_parts.append(loss_lanes.reshape(-1))
    small_sum = _allreduce_small(jnp.concatenate(small_parts).reshape(-1, LANES), "allreduce_small").reshape(-1)
    g_small, off = {}, 0
    for n, shape in _SMALL:
        size = math.prod(shape)
        g_small[n] = small_sum[off:off + size].reshape(shape)
        off += size
    cw_size = 2 * SSD_D_CONV * SSD_CONV_DIM
    g_cw_full = small_sum[off:off + cw_size].reshape(2, SSD_D_CONV, 4, SSD_CONV_DIM // 4)
    g_small["ssd_conv_w"] = jnp.sum(g_cw_full * (jnp.arange(4) == chip).astype(F32)[None, None, :, None], axis=2)
    loss = 0.5 * jnp.sum(small_sum[off + cw_size:]) / D_MODEL

    grads, delta, new_m, new_v = {}, {}, {}, {}
    for n in big_names:
        grads[n] = g_big[n]
        delta[n], new_m[n], new_v[n] = _adamw(w_in[n], g_big[n], m_in[n], v_in[n], f"adamw_{n}")
    small_names = [n for n, _ in _SMALL] + ["ssd_conv_w"]
    pack = lambda d: jnp.concatenate([d[n].reshape(-1) for n in small_names]).reshape(1, -1, LANES)
    ds, ms, vs = _adamw(pack(w_in), pack(g_small), pack(m_in), pack(v_in), "adamw_small")
    off = 0
    for n in small_names:
        shape = w_in[n].shape
        size = math.prod(shape)
        grads[n] = g_small[n]
        delta[n] = ds.reshape(-1)[off:off + size].reshape(shape)
        new_m[n] = ms.reshape(-1)[off:off + size].reshape(shape)
        new_v[n] = vs.reshape(-1)[off:off + size].reshape(shape)
        off += size

    order = ["norm_w", "ssd_in_w", "ssd_conv_w", "ssd_conv_b", "ssd_dt_bias", "ssd_a_log", "ssd_d", "ssd_gnorm_w", "ssd_out_w",
             "sb_in_w", "sb_qn_w", "sb_kn_w", "sb_out_w", "ple_norm_w", "ple_gate_w", "ple_proj_w"]
    return (loss, grad_x, *[grads[n] for n in order], *[delta[n] for n in order], *[new_m[n] for n in order],
            *[new_v[n] for n in order])
```

```python
import functools
import math

import jax
import jax.numpy as jnp
from jax import lax
from jax.experimental import pallas as pl
from jax.experimental.pallas import tpu as pltpu

F32 = jnp.float32
BF16 = jnp.bfloat16
MESH = pl.DeviceIdType.MESH

D_MODEL = 2048
DEPTH = 4
SSD_D_INNER = 4096
SSD_N_GROUPS = 8
SSD_GROUP_W = SSD_D_INNER // SSD_N_GROUPS
SSD_D_STATE = 128
SSD_CHUNK = 128
SSD_CONV_DIM = 6144
SSD_D_CONV = 4
SSD_N_HEADS = 64
SB_HEAD_DIM = 128
SB_N_HEADS = 16
SB_WIDTH = 2048
NORM_EPS = 1e-6
GATED_NORM_EPS = 1e-5
ADAM_LR = 0.001
ADAM_B1 = 0.9
ADAM_B2 = 0.999
ADAM_EPS = 1e-08
ADAM_WD = 0.01
ADAM_STEP = 10

LANES = 128
BF16_ROWS = 16

_BIG = (
    ("ssd_in_w", (2, 2048, 2576), "col"),
    ("ssd_out_w", (2, 1024, 2048), "row"),
    ("sb_in_w", (2, 2048, 2048), "col"),
    ("sb_out_w", (2, 512, 2048), "row"),
    ("ple_gate_w", (4, 512, 2048), "row"),
    ("ple_proj_w", (4, 256, 512), "col"),
)
_SMALL = (
    ("norm_w", (4, 2048)),
    ("ssd_conv_b", (2, 6144)),
    ("ssd_dt_bias", (2, 64)),
    ("ssd_a_log", (2, 64)),
    ("ssd_d", (2, 64)),
    ("ssd_gnorm_w", (2, 4096)),
    ("sb_qn_w", (2, 128)),
    ("sb_kn_w", (2, 128)),
    ("ple_norm_w", (4, 2048)),
)

_DN = {
    "nn": (((1,), (0,)), ((), ())),
    "nt": (((1,), (1,)), ((), ())),
    "tn": (((0,), (0,)), ((), ())),
}


def _dot(a, b, dn="nn"):
    return lax.dot_general(a.astype(BF16), b.astype(BF16), _DN[dn], preferred_element_type=F32)


@functools.partial(jax.custom_vjp, nondiff_argnums=(2,))
def _gdot(a, b, dn):
    return _dot(a, b, dn)


def _gdot_fwd(a, b, dn):
    return _dot(a, b, dn), (a, b)


def _gdot_bwd(dn, res, g):
    a, b = res
    if dn == "nn":
        return _dot(g, b, "nt"), _dot(a, g, "tn")
    if dn == "nt":
        return _dot(g, b, "nn"), _dot(g, a, "tn")
    return _dot(b, g, "nt"), _dot(a, g, "nn")


_gdot.defvjp(_gdot_fwd, _gdot_bwd)


def _split_dot(x, t, parts, x_left):
    acc = None
    r = x
    for i in range(parts):
        p = r.astype(BF16)
        d = lax.dot_general(p, t, _DN["nn"], preferred_element_type=F32) if x_left else lax.dot_general(
            t, p, _DN["nn"], preferred_element_type=F32)
        acc = d if acc is None else acc + d
        if i + 1 < parts:
            r = r - p.astype(F32)
    return acc


def _tri(n, lower, strict=False):
    r = lax.broadcasted_iota(jnp.int32, (n, n), 0)
    c = lax.broadcasted_iota(jnp.int32, (n, n), 1)
    keep = (r > c if strict else r >= c) if lower else (r < c if strict else r <= c)
    return jnp.where(keep, 1.0, 0.0).astype(BF16)


def _cumsum_rows_raw(x):
    return _split_dot(x, _tri(x.shape[0], True), 3, False)


@jax.custom_vjp
def _cumsum_rows(x):
    return _cumsum_rows_raw(x)


def _cumsum_rows_fwd(x):
    return _cumsum_rows_raw(x), None


def _cumsum_rows_bwd(_, g):
    return (_split_dot(g, _tri(g.shape[0], False), 3, False),)


_cumsum_rows.defvjp(_cumsum_rows_fwd, _cumsum_rows_bwd)


def _sigmoid(x):
    return 1.0 / (1.0 + jnp.exp(-x))


def _softplus(x):
    return jnp.maximum(x, 0.0) + jnp.log(1.0 + jnp.exp(-jnp.abs(x)))


def _rms(x, w, eps):
    return x * lax.rsqrt(jnp.mean(x * x, axis=-1, keepdims=True) + eps) * w


_ANY = pl.BlockSpec(memory_space=pl.ANY)


def _params(*sem):
    return pltpu.CompilerParams(dimension_semantics=sem)


class _Rider:
    def __init__(self, reads, writes, n_sems, issue):
        self.reads, self.writes, self.n_sems, self.issue = list(reads), list(writes), n_sems, issue


def _pcall(body, *, grid, in_specs, out_specs, out_shape, args, sem, name, scratch_shapes=(), aliases=None, rider=None):
    aliases = dict(aliases or {})
    if rider is None:
        outs = pl.pallas_call(body, grid=grid, in_specs=in_specs, out_specs=out_specs, out_shape=out_shape,
                              scratch_shapes=list(scratch_shapes), input_output_aliases=aliases,
                              compiler_params=_params(*sem), name=name)(*args)
        return list(outs), []
    n_in, n_out, n_scr, n_rd, n_wr = len(args), len(out_shape), len(scratch_shapes), len(rider.reads), len(rider.writes)
    passed = [k for k, w in enumerate(rider.writes) if not isinstance(w, jax.ShapeDtypeStruct)]
    for pos, k in enumerate(passed):
        aliases[n_in + n_rd + pos] = n_out + k

    def wrapped(*refs):
        ins = refs[:n_in]
        reads = refs[n_in:n_in + n_rd]
        base = n_in + n_rd + len(passed)
        outs = refs[base:base + n_out]
        writes = refs[base + n_out:base + n_out + n_wr]
        scr = refs[base + n_out + n_wr:base + n_out + n_wr + n_scr]
        send, recv = refs[-2:]
        first = last = None
        for d, n in enumerate(grid):
            i = pl.program_id(d)
            first = (i == 0) if first is None else first & (i == 0)
            last = (i == n - 1) if last is None else last & (i == n - 1)

        @pl.when(first)
        def _():
            for cp in rider.issue(reads, writes, send, recv)[0]:
                cp.start()

        body(*ins, *outs, *scr)

        @pl.when(last)
        def _():
            sends, arrivals = rider.issue(reads, writes, send, recv)
            for cp in arrivals:
                cp.wait_recv()
            for cp in sends:
                cp.wait_send()

    outs = pl.pallas_call(
        wrapped, grid=grid,
        in_specs=list(in_specs) + [_ANY] * (n_rd + len(passed)),
        out_specs=list(out_specs) + [_ANY] * n_wr,
        out_shape=list(out_shape) + [jax.ShapeDtypeStruct(w.shape, w.dtype) for w in rider.writes],
        scratch_shapes=list(scratch_shapes) + [pltpu.SemaphoreType.DMA((rider.n_sems,)), pltpu.SemaphoreType.DMA((rider.n_sems,))],
        input_output_aliases=aliases, compiler_params=_params(*(["arbitrary"] * len(grid))), name=name,
    )(*args, *rider.reads, *[rider.writes[k] for k in passed])
    return list(outs[:n_out]), list(outs[n_out:])


MM_TK = 2048


def _pick(dim, pref, unit=None):
    t = pref
    while t >= LANES:
        if dim % t == 0 and (unit is None or unit % t == 0):
            return t
        t //= 2
    return dim


def _matmul(a, b, *, dn="nn", res=None, out_dtype=F32, name, b_lay=None, o_lay=None, o_buf=None, rider=None):
    if dn == "tn":
        k_dim, m_dim = a.shape
    else:
        m_dim, k_dim = a.shape
    unit_m = unit_n = unit_k = None
    if b_lay is None:
        n_dim = b.shape[0] if dn == "nt" else b.shape[1]
    else:
        cut, layer = b_lay
        r, c = b.shape[2:]
        rows, cols = (4 * r, c) if cut == "row" else (r, 4 * c)
        n_dim = cols if dn == "nn" else rows
        assert k_dim == (rows if dn == "nn" else cols) and dn != "tn"
        if (cut == "row") == (dn == "nn"):
            unit_k = r if cut == "row" else c
        else:
            unit_n = r if cut == "row" else c
    if o_lay is not None:
        o_cut, o_layer, o_layers = o_lay
        if o_cut == "row":
            unit_m = m_dim // 4
        else:
            unit_n = n_dim // 4
    tm, tn, tk = _pick(m_dim, 1024, unit_m), _pick(n_dim, 1024, unit_n), _pick(k_dim, MM_TK, unit_k)
    nk = k_dim // tk
    a_spec = pl.BlockSpec((tk, tm), lambda i, j, k: (k, i)) if dn == "tn" else pl.BlockSpec((tm, tk), lambda i, j, k: (i, k))
    if b_lay is None:
        b_spec = pl.BlockSpec((tn, tk), lambda i, j, k: (j, k)) if dn == "nt" else pl.BlockSpec((tk, tn), lambda i, j, k: (k, j))
    elif dn == "nn" and cut == "row":
        per = r // tk
        b_spec = pl.BlockSpec((None, None, tk, tn), lambda i, j, k: (k // per, layer, k % per, j))
    elif dn == "nn":
        per = c // tn
        b_spec = pl.BlockSpec((None, None, tk, tn), lambda i, j, k: (j // per, layer, k, j % per))
    elif cut == "row":
        per = r // tn
        b_spec = pl.BlockSpec((None, None, tn, tk), lambda i, j, k: (j // per, layer, j % per, k))
    else:
        per = c // tk
        b_spec = pl.BlockSpec((None, None, tn, tk), lambda i, j, k: (k // per, layer, j, k % per))
    r_spec = pl.BlockSpec((tm, tn), lambda i, j, k: (i, j))
    if o_lay is None:
        o_spec = r_spec
        out_shape = jax.ShapeDtypeStruct((m_dim, n_dim), out_dtype)
    elif o_cut == "row":
        per_o = unit_m // tm
        o_spec = pl.BlockSpec((None, None, tm, tn), lambda i, j, k: (i // per_o, o_layer, i % per_o, j))
        out_shape = jax.ShapeDtypeStruct((4, o_layers, unit_m, n_dim), out_dtype)
    else:
        per_o = unit_n // tn
        o_spec = pl.BlockSpec((None, None, tm, tn), lambda i, j, k: (j // per_o, o_layer, i, j % per_o))
        out_shape = jax.ShapeDtypeStruct((4, o_layers, m_dim, unit_n), out_dtype)
    has_res = res is not None
    has_buf = o_buf is not None

    def body(*refs):
        a_ref, b_ref = refs[:2]
        r_ref = refs[2] if has_res else None
        o_ref = refs[-1] if nk == 1 else refs[-2]

        def finish(v):
            if has_res:
                v = v + r_ref[...]
            o_ref[...] = v.astype(o_ref.dtype)

        if nk == 1:
            finish(_dot(a_ref[...], b_ref[...], dn))
            return
        acc_ref = refs[-1]
        k = pl.program_id(2)

        @pl.when(k == 0)
        def _():
            acc_ref[...] = jnp.zeros_like(acc_ref)

        acc_ref[...] += _dot(a_ref[...], b_ref[...], dn)

        @pl.when(k == nk - 1)
        def _():
            finish(acc_ref[...])

    args = [a, b] + ([res] if has_res else []) + ([o_buf] if has_buf else [])
    outs, rode = _pcall(
        body, grid=(m_dim // tm, n_dim // tn, nk),
        in_specs=[a_spec, b_spec] + ([r_spec] if has_res else []) + ([_ANY] if has_buf else []),
        out_specs=[o_spec], out_shape=[out_shape],
        scratch_shapes=[] if nk == 1 else [pltpu.VMEM((tm, tn), F32)],
        aliases={len(args) - 1: 0} if has_buf else {},
        args=args, sem=("parallel", "parallel", "arbitrary"), name=name, rider=rider)
    return (outs[0], rode) if rider is not None else outs[0]


def _rowcall(fn, rows, consts, outs, accs, *, name, tm=256):
    args = list(rows) + list(consts)
    in_specs = [pl.BlockSpec((tm, r.shape[1]), lambda i: (i, 0)) for r in rows]
    in_specs += [pl.BlockSpec(c.shape, lambda i: (0, 0)) for c in consts]
    s_dim = args[0].shape[0]
    n_in, n_out = len(args), len(outs)
    out_shape = [jax.ShapeDtypeStruct((s_dim, w), dt) for w, dt in outs] + [jax.ShapeDtypeStruct(s, F32) for s in accs]
    out_specs = [pl.BlockSpec((tm, w), lambda i: (i, 0)) for w, _ in outs] + [pl.BlockSpec(s, lambda i: (0, 0)) for s in accs]

    def body(*refs):
        vals = fn(*[r[...] for r in refs[:n_in]])
        o_refs = refs[n_in:n_in + n_out]
        a_refs = refs[n_in + n_out:]
        for o, v in zip(o_refs, vals[:n_out]):
            o[...] = v.astype(o.dtype)
        if a_refs:
            @pl.when(pl.program_id(0) == 0)
            def _():
                for a_ref in a_refs:
                    a_ref[...] = jnp.zeros_like(a_ref)

            for a_ref, v in zip(a_refs, vals[n_out:]):
                a_ref[...] += v

    return pl.pallas_call(
        body, grid=(s_dim // tm,), in_specs=in_specs, out_specs=out_specs, out_shape=out_shape,
        compiler_params=_params("arbitrary"), name=name,
    )(*args)


def _rms_fwd(h, w, name):
    return _rowcall(lambda x, w_: (_rms(x, w_, NORM_EPS),), [h], [w], [(h.shape[1], BF16)], [], name=name)[0]


def _rms_bwd(h, w, dy, dres, name):
    def fn(x, dy_, dres_, w_):
        _, vjp = jax.vjp(lambda a, b: _rms(a, b, NORM_EPS), x, w_)
        dx, dw = vjp(dy_)
        return dx + dres_, dw

    return _rowcall(fn, [h, dy, dres], [w], [(h.shape[1], F32)], [w.shape], name=name)


def _ple_fwd(h1, pp, gl, name):
    return _rowcall(lambda a, b, c: (a + b * _sigmoid(c),), [h1, pp, gl], [], [(h1.shape[1], F32)], [], name=name)[0]


def _ple_bwd(dh2, pp, gl, name):
    def fn(d, b, c):
        gate = _sigmoid(c)
        return d * gate, d * b * gate * (1.0 - gate)

    return _rowcall(fn, [dh2, pp, gl], [], [(dh2.shape[1], BF16), (dh2.shape[1], BF16)], [], name=name)


def _loss_bwd(y, target, name):
    width = y.shape[1]

    def fn(a, t):
        d = a - t
        col = jnp.sum(d * d, axis=0, keepdims=True)
        part = col[:, 0:LANES]
        for j in range(1, width // LANES):
            part = part + col[:, j * LANES:(j + 1) * LANES]
        return d * (1.0 / width), part

    return _rowcall(fn, [y, target], [], [(width, F32)], [(1, LANES)], name=name)


CONV_TC = 256


def _shift_down(x, j):
    if j == 0:
        return x
    row = lax.broadcasted_iota(jnp.int32, x.shape, 0)
    return jnp.where(row >= j, pltpu.roll(x, j, 0), 0.0)


def _shift_up(x, j):
    if j == 0:
        return x
    n = x.shape[0]
    row = lax.broadcasted_iota(jnp.int32, x.shape, 0)
    return jnp.where(row < n - j, pltpu.roll(x, n - j, 0), 0.0)


def _conv_fwd(pzx, cw, cb, name):
    s_dim = pzx.shape[0]
    off = SSD_D_INNER // CONV_TC

    def body(x_ref, w_ref, b_ref, o_ref):
        x = x_ref[...]
        w = w_ref[...]
        y = b_ref[...] + w[3:4, :] * x
        for k in range(SSD_D_CONV - 1):
            y = y + w[k:k + 1, :] * _shift_down(x, SSD_D_CONV - 1 - k)
        o_ref[...] = y * _sigmoid(y)

    return pl.pallas_call(
        body, grid=(SSD_CONV_DIM // CONV_TC,),
        in_specs=[pl.BlockSpec((s_dim, CONV_TC), lambda j: (0, off + j)), pl.BlockSpec((SSD_D_CONV, CONV_TC), lambda j: (0, j)),
                  pl.BlockSpec((1, CONV_TC), lambda j: (0, j))],
        out_specs=pl.BlockSpec((s_dim, CONV_TC), lambda j: (0, j)),
        out_shape=jax.ShapeDtypeStruct((s_dim, SSD_CONV_DIM), F32),
        compiler_params=_params("parallel"), name=name,
    )(pzx, cw, cb)


def _conv_bwd(pzx, cw, cb, dact, name):
    s_dim = pzx.shape[0]
    off = SSD_D_INNER // CONV_TC

    def body(x_ref, w_ref, b_ref, d_ref, dx_ref, dw_ref, db_ref):
        x = x_ref[...]
        w = w_ref[...]
        xs = [_shift_down(x, SSD_D_CONV - 1 - k) for k in range(SSD_D_CONV)]
        y = b_ref[...]
        for k in range(SSD_D_CONV):
            y = y + w[k:k + 1, :] * xs[k]
        sg = _sigmoid(y)
        dy = d_ref[...] * (sg * (1.0 + y * (1.0 - sg)))
        dx = w[3:4, :] * dy
        for k in range(SSD_D_CONV - 1):
            dx = dx + w[k:k + 1, :] * _shift_up(dy, SSD_D_CONV - 1 - k)
        dx_ref[...] = dx.astype(dx_ref.dtype)
        for k in range(SSD_D_CONV):
            dw_ref[k:k + 1, :] = jnp.sum(dy * xs[k], axis=0, keepdims=True)
        db_ref[...] = jnp.sum(dy, axis=0, keepdims=True)

    col = pl.BlockSpec((s_dim, CONV_TC), lambda j: (0, j))
    return pl.pallas_call(
        body, grid=(SSD_CONV_DIM // CONV_TC,),
        in_specs=[pl.BlockSpec((s_dim, CONV_TC), lambda j: (0, off + j)), pl.BlockSpec((SSD_D_CONV, CONV_TC), lambda j: (0, j)),
                  pl.BlockSpec((1, CONV_TC), lambda j: (0, j)), col],
        out_specs=[col, pl.BlockSpec((SSD_D_CONV, CONV_TC), lambda j: (0, j)), pl.BlockSpec((1, CONV_TC), lambda j: (0, j))],
        out_shape=[jax.ShapeDtypeStruct((s_dim, SSD_CONV_DIM), BF16), jax.ShapeDtypeStruct((SSD_D_CONV, SSD_CONV_DIM), F32),
                   jax.ShapeDtypeStruct((1, SSD_CONV_DIM), F32)],
        compiler_params=_params("parallel"), name=name,
    )(pzx, cw, cb, dact)


def _ssd_step(xs, bm, cm, dtraw, bias, alog, dskip, st_in, z, gw, dot, cumsum):
    n = xs.shape[0]
    lane = lax.broadcasted_iota(jnp.int32, (1, LANES), 1)
    sub = lax.broadcasted_iota(jnp.int32, (LANES, 1), 0)
    left = (lane < 64).astype(F32)
    right = 1.0 - left
    top = (sub < 64).astype(F32)
    bot = 1.0 - top
    row = lax.broadcasted_iota(jnp.int32, (n, n), 0)
    colm = lax.broadcasted_iota(jnp.int32, (n, n), 1)
    causal = row >= colm

    dt = _softplus(dtraw + bias)
    adt = dt * (-jnp.exp(alog))
    acum = cumsum(adt)
    acum_t = acum.T
    last = jnp.sum(adt, axis=0, keepdims=True)
    scores = dot(cm, bm, "nt")

    def lane_of(v, h):
        return jnp.sum(v * (lane == h).astype(F32), axis=1, keepdims=True)

    ys, sts = [], []
    for pr in range(4):
        heads = (2 * pr, 2 * pr + 1)
        ac = [lane_of(acum, h) for h in heads]
        ar = [jnp.sum(acum_t * (sub == h).astype(F32), axis=0, keepdims=True) for h in heads]
        dth = [lane_of(dt, h) for h in heads]
        la = [lane_of(last, h) for h in heads]
        dk = [lane_of(dskip, h) for h in heads]
        x2 = xs[:, pr * LANES:(pr + 1) * LANES]
        xdt = x2 * (dth[0] * left + dth[1] * right)
        yd = None
        for i, side in enumerate((left, right)):
            decay = jnp.where(causal, jnp.exp(jnp.minimum(ac[i] - ar[i], 0.0)), 0.0)
            t = dot(scores * decay, xdt * side, "nn")
            yd = t if yd is None else yd + t
        st2 = st_in[pr * LANES:(pr + 1) * LANES, :]
        yo = dot(cm, st2, "nt") * (jnp.exp(ac[0]) * left + jnp.exp(ac[1]) * right)
        dte = jnp.exp(la[0] - ac[0]) * left + jnp.exp(la[1] - ac[1]) * right
        cs = dot(xdt * dte, bm, "tn")
        sts.append(st2 * (jnp.exp(la[0]) * top + jnp.exp(la[1]) * bot) + cs)
        ys.append(yd + yo + (dk[0] * left + dk[1] * right) * x2)
    y = jnp.concatenate(ys, axis=1)
    yg = y * (z * _sigmoid(z))
    yn = yg * lax.rsqrt(jnp.mean(yg * yg, axis=-1, keepdims=True) + GATED_NORM_EPS) * gw
    return yn, jnp.concatenate(sts, axis=0)


def _ssd_specs(n_chunks, rev):
    ci = (lambda c: n_chunks - 1 - c) if rev else (lambda c: c)
    n_x = SSD_D_INNER // LANES
    return dict(
        xs=pl.BlockSpec((SSD_CHUNK, SSD_GROUP_W), lambda g, c: (ci(c), g)),
        bm=pl.BlockSpec((SSD_CHUNK, LANES), lambda g, c: (ci(c), n_x + g)),
        cm=pl.BlockSpec((SSD_CHUNK, LANES), lambda g, c: (ci(c), n_x + SSD_N_GROUPS + g)),
        dt=pl.BlockSpec((None, SSD_CHUNK, LANES), lambda g, c: (g, ci(c), 0)),
        vec=pl.BlockSpec((None, 1, LANES), lambda g, c: (g, 0, 0)),
        z=pl.BlockSpec((SSD_CHUNK, SSD_GROUP_W), lambda g, c: (ci(c), g)),
        gw=pl.BlockSpec((1, SSD_GROUP_W), lambda g, c: (0, g)),
        st=pl.BlockSpec((None, None, SSD_GROUP_W, SSD_D_STATE), lambda g, c: (g, ci(c), 0, 0)),
    )


def _ssd_fwd(act, dtg, bias, alog, dskip, pzx, gw, name, rider=None):
    s_dim = act.shape[0]
    n_chunks = s_dim // SSD_CHUNK
    sp = _ssd_specs(n_chunks, False)

    def body(xs, bm, cm, dt, b_ref, a_ref, d_ref, z, gw_ref, yn_ref, st_ref, state):
        @pl.when(pl.program_id(1) == 0)
        def _():
            state[...] = jnp.zeros_like(state)

        st_in = state[...]
        st_ref[...] = st_in
        yn, st_out = _ssd_step(xs[...], bm[...], cm[...], dt[...], b_ref[...], a_ref[...], d_ref[...], st_in, z[...], gw_ref[...],
                               _dot, _cumsum_rows_raw)
        yn_ref[...] = yn.astype(yn_ref.dtype)
        state[...] = st_out

    outs, rode = _pcall(
        body, grid=(SSD_N_GROUPS, n_chunks),
        in_specs=[sp["xs"], sp["bm"], sp["cm"], sp["dt"], sp["vec"], sp["vec"], sp["vec"], sp["z"], sp["gw"]],
        out_specs=[sp["xs"], sp["st"]],
        out_shape=[jax.ShapeDtypeStruct((s_dim, SSD_D_INNER), BF16),
                   jax.ShapeDtypeStruct((SSD_N_GROUPS, n_chunks, SSD_GROUP_W, SSD_D_STATE), F32)],
        scratch_shapes=[pltpu.VMEM((SSD_GROUP_W, SSD_D_STATE), F32)],
        args=[act, act, act, dtg, bias, alog, dskip, pzx, gw], sem=("parallel", "arbitrary"), name=name, rider=rider)
    return (outs, rode) if rider is not None else outs


def _ssd_bwd(act, dtg, bias, alog, dskip, pzx, gw, states, dyn, name, rider=None):
    s_dim = act.shape[0]
    n_chunks = s_dim // SSD_CHUNK
    sp = _ssd_specs(n_chunks, True)
    rc = lambda c: n_chunks - 1 - c

    def body(xs, bm, cm, dt, b_ref, a_ref, d_ref, z, gw_ref, st_ref, dyn_ref,
             dxs_ref, dbm_ref, dcm_ref, ddt_ref, db_ref, da_ref, dd_ref, dz_ref, dgw_ref, dstate):
        first = pl.program_id(1) == 0

        @pl.when(first)
        def _():
            dstate[...] = jnp.zeros_like(dstate)
            db_ref[...] = jnp.zeros_like(db_ref)
            da_ref[...] = jnp.zeros_like(da_ref)
            dd_ref[...] = jnp.zeros_like(dd_ref)
            dgw_ref[...] = jnp.zeros_like(dgw_ref)

        fn = functools.partial(_ssd_step, dot=_gdot, cumsum=_cumsum_rows)
        _, vjp = jax.vjp(fn, xs[...], bm[...], cm[...], dt[...], b_ref[...], a_ref[...], d_ref[...], st_ref[...], z[...], gw_ref[...])
        dxs, dbm, dcm, ddt, db, da, dd, dst, dz, dgw = vjp((dyn_ref[...], dstate[...]))
        dxs_ref[...] = dxs
        dbm_ref[...] = dbm
        dcm_ref[...] = dcm
        ddt_ref[...] = ddt
        dz_ref[...] = dz.astype(dz_ref.dtype)
        db_ref[...] += db
        da_ref[...] += da
        dd_ref[...] += dd
        dgw_ref[...] += dgw
        dstate[...] = dst

    bc = pl.BlockSpec((SSD_CHUNK, LANES), lambda g, c: (rc(c), g))
    outs, rode = _pcall(
        body, grid=(SSD_N_GROUPS, n_chunks),
        in_specs=[sp["xs"], sp["bm"], sp["cm"], sp["dt"], sp["vec"], sp["vec"], sp["vec"], sp["z"], sp["gw"], sp["st"], sp["xs"]],
        out_specs=[sp["xs"], bc, bc, sp["dt"], sp["vec"], sp["vec"], sp["vec"], sp["xs"], sp["gw"]],
        out_shape=[jax.ShapeDtypeStruct((s_dim, SSD_D_INNER), F32),
                   jax.ShapeDtypeStruct((s_dim, SSD_N_GROUPS * SSD_D_STATE), F32),
                   jax.ShapeDtypeStruct((s_dim, SSD_N_GROUPS * SSD_D_STATE), F32),
                   jax.ShapeDtypeStruct((SSD_N_GROUPS, s_dim, LANES), F32),
                   jax.ShapeDtypeStruct((SSD_N_GROUPS, 1, LANES), F32),
                   jax.ShapeDtypeStruct((SSD_N_GROUPS, 1, LANES), F32),
                   jax.ShapeDtypeStruct((SSD_N_GROUPS, 1, LANES), F32),
                   jax.ShapeDtypeStruct((s_dim, SSD_D_INNER), BF16),
                   jax.ShapeDtypeStruct((1, SSD_D_INNER), F32)],
        scratch_shapes=[pltpu.VMEM((SSD_GROUP_W, SSD_D_STATE), F32)],
        args=[act, act, act, dtg, bias, alog, dskip, pzx, gw, states, dyn], sem=("arbitrary", "arbitrary"), name=name, rider=rider)
    return (outs, rode) if rider is not None else outs


SB_T = 128
SB_GROUP = 8
SB_WIDE = SB_GROUP * SB_T
SB_HB = 2
SB_SCALE = 1.0 / math.sqrt(SB_HEAD_DIM)


def _qknorm_fwd(proj, qw, kw, name, tm=512):
    s_dim = proj.shape[0]

    def body(q_ref, k_ref, v_ref, qw_ref, kw_ref, qo, ko, vo):
        qo[...] = _rms(q_ref[...], qw_ref[...], NORM_EPS).astype(BF16)
        ko[...] = _rms(k_ref[...], kw_ref[...], NORM_EPS).astype(BF16)
        vo[...] = v_ref[...].astype(BF16)

    blk = lambda o: pl.BlockSpec((tm, SB_HEAD_DIM), lambda i, h: (i, o + h))
    vec = pl.BlockSpec((1, SB_HEAD_DIM), lambda i, h: (0, 0))
    return pl.pallas_call(
        body, grid=(s_dim // tm, SB_N_HEADS),
        in_specs=[blk(0), blk(SB_N_HEADS), blk(2 * SB_N_HEADS), vec, vec],
        out_specs=[blk(0)] * 3,
        out_shape=[jax.ShapeDtypeStruct((s_dim, SB_WIDTH), BF16)] * 3,
        compiler_params=_params("parallel", "parallel"), name=name,
    )(proj, proj, proj, qw, kw)


def _qknorm_bwd(proj, qw, kw, dqn, dkn, name, tm=512):
    s_dim = proj.shape[0]

    def body(q_ref, k_ref, dq_ref, dk_ref, qw_ref, kw_ref, dqo, dko, dqw, dkw):
        @pl.when((pl.program_id(0) == 0) & (pl.program_id(1) == 0))
        def _():
            dqw[...] = jnp.zeros_like(dqw)
            dkw[...] = jnp.zeros_like(dkw)

        fn = lambda a, b: _rms(a, b, NORM_EPS)
        _, vq = jax.vjp(fn, q_ref[...], qw_ref[...])
        dq, dw = vq(dq_ref[...])
        dqo[...] = dq.astype(BF16)
        dqw[...] += dw
        _, vk = jax.vjp(fn, k_ref[...], kw_ref[...])
        dk, dw = vk(dk_ref[...])
        dko[...] = dk.astype(BF16)
        dkw[...] += dw

    blk = lambda o: pl.BlockSpec((tm, SB_HEAD_DIM), lambda i, h: (i, o + h))
    vec = pl.BlockSpec((1, SB_HEAD_DIM), lambda i, h: (0, 0))
    return pl.pallas_call(
        body, grid=(s_dim // tm, SB_N_HEADS),
        in_specs=[blk(0), blk(SB_N_HEADS), blk(0), blk(0), vec, vec],
        out_specs=[blk(0), blk(0), vec, vec],
        out_shape=[jax.ShapeDtypeStruct((s_dim, SB_WIDTH), BF16)] * 2 + [jax.ShapeDtypeStruct((1, SB_HEAD_DIM), F32)] * 2,
        compiler_params=_params("arbitrary", "arbitrary"), name=name,
    )(proj, proj, dqn, dkn, qw, kw)


def _sb_logits(q, k, strict):
    z = _dot(q, k, "nt") * SB_SCALE
    lb = jnp.minimum(z, 0.0) - jnp.log(1.0 + jnp.exp(-jnp.abs(z)))
    lm = lb - z
    if strict is not None:
        lm = jnp.where(strict, lm, 0.0)
    return lb, lm


def _sb_strict(qi, grp):
    r = lax.broadcasted_iota(jnp.int32, (SB_T, SB_WIDE), 0) + qi * SB_T
    c = lax.broadcasted_iota(jnp.int32, (SB_T, SB_WIDE), 1) + grp * SB_WIDE
    return c < r


def _head_lanes(hh):
    return slice(hh * SB_HEAD_DIM, (hh + 1) * SB_HEAD_DIM)


def _sb_fwd(qn, kn, vb, proj, name, rider=None):
    s_dim = qn.shape[0]
    nq = s_dim // SB_T
    assert nq % SB_GROUP == 0

    def body(q_ref, k_ref, v_ref, g_ref, og_ref, o_ref, t_ref):
        qi = pl.program_id(1)
        top = qi // SB_GROUP
        after = _tri(SB_T, True, strict=True)
        qs = [q_ref[:, _head_lanes(hh)] for hh in range(SB_HB)]

        def step(grp, masked, carries):
            start = pl.multiple_of(grp * SB_WIDE, SB_WIDE)
            strict = _sb_strict(qi, grp) if masked else None
            out = []
            for hh in range(SB_HB):
                o_acc, cr = carries[hh]
                k = k_ref[pl.ds(start, SB_WIDE), _head_lanes(hh)]
                v = v_ref[pl.ds(start, SB_WIDE), _head_lanes(hh)]
                lb, lm = _sb_logits(qs[hh], k, strict)
                rest = [None] * SB_GROUP
                for t in reversed(range(SB_GROUP)):
                    lm_t = lm[:, t * SB_T:(t + 1) * SB_T]
                    rest[t] = cr + _split_dot(lm_t, after, 2, True)
                    cr = cr + jnp.sum(lm_t, axis=1, keepdims=True)
                a = jnp.exp(lb + jnp.concatenate(rest, axis=1))
                if masked:
                    a = jnp.where(strict, a, 0.0)
                out.append((o_acc + _dot(a, v), cr))
            return tuple(out)

        init = tuple((jnp.zeros((SB_T, SB_HEAD_DIM), F32), jnp.zeros((SB_T, 1), F32)) for _ in range(SB_HB))
        carries = step(top, True, init)
        carries = lax.fori_loop(0, top, lambda i, c: step(top - 1 - i, False, c), carries)
        for hh in range(SB_HB):
            o, tot = carries[hh]
            g = g_ref[:, _head_lanes(hh)]
            o_ref[:, _head_lanes(hh)] = o
            og_ref[:, _head_lanes(hh)] = (o * (g * _sigmoid(g))).astype(og_ref.dtype)
            t_ref[hh] = jnp.broadcast_to(tot, (SB_T, LANES))

    wide = SB_HB * SB_HEAD_DIM
    qb = pl.BlockSpec((SB_T, wide), lambda h, i: (i, h))
    kv = pl.BlockSpec((s_dim, wide), lambda h, i: (0, h))
    outs, rode = _pcall(
        body, grid=(SB_N_HEADS // SB_HB, nq),
        in_specs=[qb, kv, kv, pl.BlockSpec((SB_T, wide), lambda h, i: (i, 3 * SB_N_HEADS // SB_HB + h))],
        out_specs=[qb, qb, pl.BlockSpec((SB_HB, SB_T, LANES), lambda h, i: (h, i, 0))],
        out_shape=[jax.ShapeDtypeStruct((s_dim, SB_WIDTH), BF16), jax.ShapeDtypeStruct((s_dim, SB_WIDTH), F32),
                   jax.ShapeDtypeStruct((SB_N_HEADS, s_dim, LANES), F32)],
        args=[qn, kn, vb, proj], sem=("parallel", "arbitrary"), name=name, rider=rider)
    return (outs, rode) if rider is not None else outs


def _sb_bwd(qn, kn, vb, proj, o, tot, dog, name, rider=None):
    s_dim = qn.shape[0]
    nq = s_dim // SB_T
    assert nq % SB_GROUP == 0

    def body(q_ref, k_ref, v_ref, g_ref, o_ref, t_ref, dog_ref, dq_ref, dk_ref, dv_ref, dvb_ref, dg_ref):
        qi = pl.program_id(1)
        top = qi // SB_GROUP

        @pl.when(qi == 0)
        def _():
            dk_ref[...] = jnp.zeros_like(dk_ref)
            dv_ref[...] = jnp.zeros_like(dv_ref)

        after = _tri(SB_T, True, strict=True)
        before = _tri(SB_T, False, strict=True)
        qs, dos, totals = [], [], []
        for hh in range(SB_HB):
            g = g_ref[:, _head_lanes(hh)]
            sg = _sigmoid(g)
            dog_v = dog_ref[:, _head_lanes(hh)]
            dg_ref[:, _head_lanes(hh)] = (dog_v * o_ref[:, _head_lanes(hh)] * (sg * (1.0 + g * (1.0 - sg)))).astype(dg_ref.dtype)
            dos.append((dog_v * (g * sg)).astype(BF16))
            qs.append(q_ref[:, _head_lanes(hh)])
            totals.append(t_ref[hh][:, 0:1])

        def step(grp, masked, carries):
            start = pl.multiple_of(grp * SB_WIDE, SB_WIDE)
            strict = _sb_strict(qi, grp) if masked else None
            out = []
            for hh in range(SB_HB):
                dq_acc, cp, ce = carries[hh]
                q, do = qs[hh], dos[hh]
                k = k_ref[pl.ds(start, SB_WIDE), _head_lanes(hh)]
                v = v_ref[pl.ds(start, SB_WIDE), _head_lanes(hh)]
                lb, lm = _sb_logits(q, k, strict)
                rest = []
                for t in range(SB_GROUP):
                    lm_t = lm[:, t * SB_T:(t + 1) * SB_T]
                    cp = cp + jnp.sum(lm_t, axis=1, keepdims=True)
                    rest.append((totals[hh] - cp) + _split_dot(lm_t, after, 2, True))
                a = jnp.exp(lb + jnp.concatenate(rest, axis=1))
                if masked:
                    a = jnp.where(strict, a, 0.0)
                e = a * _dot(do, v, "nt")
                excl = []
                for t in range(SB_GROUP):
                    e_t = e[:, t * SB_T:(t + 1) * SB_T]
                    excl.append(ce + _split_dot(e_t, before, 2, True))
                    ce = ce + jnp.sum(e_t, axis=1, keepdims=True)
                eex = jnp.concatenate(excl, axis=1)
                if masked:
                    eex = jnp.where(strict, eex, 0.0)
                sig = jnp.exp(lb)
                dz = (e * (1.0 - sig) - eex * sig) * SB_SCALE
                dv_ref[pl.ds(start, SB_WIDE), _head_lanes(hh)] += _dot(a, do, "tn")
                dk_ref[pl.ds(start, SB_WIDE), _head_lanes(hh)] += _dot(dz, q, "tn")
                out.append((dq_acc + _dot(dz, k), cp, ce))
            return tuple(out)

        zero = jnp.zeros((SB_T, 1), F32)
        init = tuple((jnp.zeros((SB_T, SB_HEAD_DIM), F32), zero, zero) for _ in range(SB_HB))
        carries = lax.fori_loop(0, top, lambda i, c: step(i, False, c), init)
        carries = step(top, True, carries)
        for hh in range(SB_HB):
            dq_ref[:, _head_lanes(hh)] = carries[hh][0]

        @pl.when(qi == nq - 1)
        def _():
            dvb_ref[...] = dv_ref[...].astype(BF16)

    wide = SB_HB * SB_HEAD_DIM
    qb = pl.BlockSpec((SB_T, wide), lambda h, i: (i, h))
    kv = pl.BlockSpec((s_dim, wide), lambda h, i: (0, h))
    outs, rode = _pcall(
        body, grid=(SB_N_HEADS // SB_HB, nq),
        in_specs=[qb, kv, kv, pl.BlockSpec((SB_T, wide), lambda h, i: (i, 3 * SB_N_HEADS // SB_HB + h)), qb,
                  pl.BlockSpec((SB_HB, SB_T, LANES), lambda h, i: (h, i, 0)), qb],
        out_specs=[qb, kv, kv, kv, qb],
        out_shape=[jax.ShapeDtypeStruct((s_dim, SB_WIDTH), F32), jax.ShapeDtypeStruct((s_dim, SB_WIDTH), F32),
                   jax.ShapeDtypeStruct((s_dim, SB_WIDTH), F32), jax.ShapeDtypeStruct((s_dim, SB_WIDTH), BF16),
                   jax.ShapeDtypeStruct((s_dim, SB_WIDTH), BF16)],
        args=[qn, kn, vb, proj, o, tot, dog], sem=("parallel", "arbitrary"), name=name, rider=rider)
    return (outs, rode) if rider is not None else outs


def _adamw_math(w, g, m, v):
    m = ADAM_B1 * m + (1.0 - ADAM_B1) * g
    v = ADAM_B2 * v + (1.0 - ADAM_B2) * (g * g)
    m_hat = m / (1.0 - ADAM_B1 ** ADAM_STEP)
    v_hat = v / (1.0 - ADAM_B2 ** ADAM_STEP)
    delta = -ADAM_LR * (m_hat / (jnp.sqrt(v_hat) + ADAM_EPS) + ADAM_WD * w)
    return delta, m, v


def _row_block(rows, cols, itemsize=4, limit=1 << 20):
    tr = rows
    while tr * cols * itemsize > limit and tr % (2 * BF16_ROWS) == 0:
        tr //= 2
    return tr


def _adamw(w, g, m, v, name):
    n, rows, cols = w.shape
    tr = _row_block(rows, cols)

    def body(w_ref, g_ref, m_ref, v_ref, d_out, m_out, v_out):
        d, m_new, v_new = _adamw_math(w_ref[...], g_ref[...], m_ref[...], v_ref[...])
        d_out[...] = d
        m_out[...] = m_new
        v_out[...] = v_new

    blk = pl.BlockSpec((None, tr, cols), lambda i, j: (i, j, 0))
    return pl.pallas_call(
        body, grid=(n, rows // tr), in_specs=[blk] * 4, out_specs=[blk] * 3,
        out_shape=[jax.ShapeDtypeStruct(w.shape, F32)] * 3,
        compiler_params=_params("parallel", "parallel"), name=name,
    )(w, g, m, v)


_FLIPS = ((1, 0), (0, 1), (1, 1))


def _place():
    return lax.axis_index("x"), lax.axis_index("y"), lax.axis_index("c")


def _flip(v, f):
    return 1 - v if f else v


def _half_rows(ref, lead, hc, hr):
    return ref.at[(*lead, pl.ds(pl.multiple_of(hc * hr, BF16_ROWS), hr), slice(None))]


def _remote(src, dst, send, recv, k, to):
    return pltpu.make_async_remote_copy(src_ref=src, dst_ref=dst, send_sem=send.at[k], recv_sem=recv.at[k], device_id=to,
                                        device_id_type=MESH)


def _comm_call(reads, writes, n_sems, phases, name):
    passed = [k for k, w in enumerate(writes) if not isinstance(w, jax.ShapeDtypeStruct)]
    n_rd = len(reads)

    def body(*refs):
        rd = refs[:n_rd]
        wr = refs[n_rd + len(passed):n_rd + len(passed) + len(writes)]
        send, recv = refs[-2:]
        for phase in phases:
            sends, arrivals = phase(rd, wr, send, recv)
            for cp in sends:
                cp.start()
            for cp in arrivals:
                cp.wait_recv()
            for cp in sends:
                cp.wait_send()

    return pl.pallas_call(
        body, in_specs=[_ANY] * (n_rd + len(passed)), out_specs=[_ANY] * len(writes),
        out_shape=[jax.ShapeDtypeStruct(w.shape, w.dtype) for w in writes],
        input_output_aliases={n_rd + pos: k for pos, k in enumerate(passed)},
        scratch_shapes=[pltpu.SemaphoreType.DMA((n_sems,)), pltpu.SemaphoreType.DMA((n_sems,))], name=name,
    )(*reads, *[writes[k] for k in passed])


def _ag_ici(pieces, names, base=0):
    def phase(shards, gathered, send, recv):
        x, y, c = _place()
        me = 2 * x + y
        sends, arrivals = [], []
        for k, (n, l) in enumerate(pieces):
            a = names.index(n)
            hr = shards[a].shape[1] // 2
            src = _half_rows(shards[a], (l,), c, hr)
            for j, (fx, fy) in enumerate(_FLIPS):
                tx, ty = _flip(x, fx), _flip(y, fy)
                sends.append(_remote(src, _half_rows(gathered[a], (me, l), c, hr), send, recv, base + 3 * k + j, (tx, ty, c)))
                arrivals.append(_remote(src, _half_rows(gathered[a], (2 * tx + ty, l), c, hr), send, recv, base + 3 * k + j, (tx, ty, c)))
        return sends, arrivals

    return phase


def _ag_pass_on(pieces, names, base=0):
    def phase(_, gathered, send, recv):
        x, y, c = _place()
        sibling = (x, y, 1 - c)
        sends, arrivals = [], []
        for k, (n, l) in enumerate(pieces):
            a = names.index(n)
            hr = gathered[a].shape[2] // 2
            for j, (fx, fy) in enumerate(_FLIPS):
                chip = 2 * _flip(x, fx) + _flip(y, fy)
                landed = _half_rows(gathered[a], (chip, l), c, hr)
                sends.append(_remote(landed, landed, send, recv, base + 3 * k + j, sibling))
                arrivals.append(_remote(landed, _half_rows(gathered[a], (chip, l), 1 - c, hr), send, recv, base + 3 * k + j, sibling))
        return sends, arrivals

    return phase


def _pair_exchange(grads, name):
    def phase(ins, outs, send, recv):
        x, y, c = _place()
        cps = [_remote(_half_rows(ins[a], (slice(None), slice(None)), 1 - c, g.shape[2] // 2), outs[a], send, recv, a, (x, y, 1 - c))
               for a, g in enumerate(grads)]
        return cps, cps

    outs = [jax.ShapeDtypeStruct(g.shape[:2] + (g.shape[2] // 2, g.shape[3]), g.dtype) for g in grads]
    return _comm_call(grads, outs, len(grads), [phase], name)


def _pair_sum(g, got, place, name):
    _, layers, hr, cols = got.shape
    tr = _row_block(hr, cols)
    per = hr // tr

    def body(place_ref, g_ref, r_ref, o_ref):
        o_ref[...] = (g_ref[...].astype(F32) + r_ref[...].astype(F32)).astype(o_ref.dtype)

    blk = pl.BlockSpec((None, None, tr, cols), lambda k, l, i, pr: (k, l, i, 0))
    return pl.pallas_call(
        body,
        grid_spec=pltpu.PrefetchScalarGridSpec(
            num_scalar_prefetch=1, grid=(4, layers, per),
            in_specs=[pl.BlockSpec((None, None, tr, cols), lambda k, l, i, pr: (k, l, pr[1] * per + i, 0)), blk],
            out_specs=blk),
        out_shape=jax.ShapeDtypeStruct(got.shape, BF16),
        compiler_params=_params("parallel", "parallel", "parallel"), name=name,
    )(place, g, got)


def _scatter_phase(n_arr):
    def phase(ins, outs, send, recv):
        x, y, c = _place()
        cps = []
        for a in range(n_arr):
            for j, (fx, fy) in enumerate(_FLIPS):
                tx, ty = _flip(x, fx), _flip(y, fy)
                cps.append(_remote(ins[a].at[2 * tx + ty], outs[a].at[j], send, recv, 3 * a + j, (tx, ty, c)))
        return cps, cps

    return phase


def _scatter_outs(pairs):
    return [jax.ShapeDtypeStruct((3,) + p.shape[1:], p.dtype) for p in pairs]


def _chip_scatter(pairs, name):
    return _comm_call(pairs, _scatter_outs(pairs), 3 * len(pairs), [_scatter_phase(len(pairs))], name)


def _scatter_rider(pairs):
    return _Rider(pairs, _scatter_outs(pairs), 3 * len(pairs), _scatter_phase(len(pairs)))


def _chip_sum(p, got, place, layer, layers, o_buf, name):
    _, _, hr, cols = p.shape
    tr = _row_block(hr, cols)
    per = hr // tr

    def body(place_ref, p_ref, r_ref, *rest):
        o_ref = rest[-1]
        acc = p_ref[...].astype(F32)
        for j in range(3):
            acc = acc + r_ref[j].astype(F32)
        o_ref[...] = acc

    has_buf = o_buf is not None
    return pl.pallas_call(
        body,
        grid_spec=pltpu.PrefetchScalarGridSpec(
            num_scalar_prefetch=1, grid=(per,),
            in_specs=[pl.BlockSpec((None, None, tr, cols), lambda i, pr: (pr[0], 0, i, 0)),
                      pl.BlockSpec((3, None, tr, cols), lambda i, pr: (0, 0, i, 0))] + ([_ANY] if has_buf else []),
            out_specs=pl.BlockSpec((None, tr, cols), lambda i, pr: (layer, pr[1] * per + i, 0))),
        out_shape=jax.ShapeDtypeStruct((layers, 2 * hr, cols), F32),
        input_output_aliases={3: 0} if has_buf else {},
        compiler_params=_params("parallel"), name=name,
    )(*((place, p, got) + ((o_buf,) if has_buf else ())))


def _pair_gather(halves):
    def phase(_, bufs, send, recv):
        x, y, c = _place()
        sends, arrivals = [], []
        for a, h in enumerate(halves):
            hr = h.shape[1] // 2
            mine = _half_rows(bufs[a], (slice(None),), c, hr)
            sends.append(_remote(mine, mine, send, recv, a, (x, y, 1 - c)))
            arrivals.append(_remote(mine, _half_rows(bufs[a], (slice(None),), 1 - c, hr), send, recv, a, (x, y, 1 - c)))
        return sends, arrivals

    return _comm_call([], halves, len(halves), [phase], "rs_pair_gather")


def _allreduce_small(v, name):
    rows, cols = v.shape

    def body(v_ref, o_ref, buf, send_sems, recv_sems):
        x, y, c = _place()
        me = 4 * x + 2 * y + c
        buf[0] = v_ref[...]
        cps = []
        for k in range(1, 8):
            kx, ky, kc = (k >> 2) & 1, (k >> 1) & 1, k & 1
            cp = pltpu.make_async_remote_copy(src_ref=v_ref, dst_ref=buf.at[k], send_sem=send_sems.at[k - 1], recv_sem=recv_sems.at[k - 1],
                                              device_id=(_flip(x, kx), _flip(y, ky), _flip(c, kc)), device_id_type=MESH)
            cp.start()
            cps.append(cp)
        for cp in cps:
            cp.wait()
        acc = buf[me]
        for d in range(1, 8):
            acc = acc + buf[jnp.bitwise_xor(d, me)]
        o_ref[...] = acc

    vm = pl.BlockSpec(memory_space=pltpu.VMEM)
    return pl.pallas_call(
        body, in_specs=[vm], out_specs=vm, out_shape=jax.ShapeDtypeStruct((rows, cols), F32),
        scratch_shapes=[pltpu.VMEM((8, rows, cols), F32), pltpu.SemaphoreType.DMA((7,)), pltpu.SemaphoreType.DMA((7,))],
        name=name,
    )(v)


def _cols_from_shards(g):
    _, n, r, cs = g.shape
    return g.transpose(1, 2, 0, 3).reshape(n, r, 4 * cs)


def _shards_from_cols(w):
    n, r, cols = w.shape
    return w.reshape(n, r, 4, cols // 4).transpose(2, 0, 1, 3)


def _pad_lanes(a):
    return jnp.pad(a, ((0, 0), (0, LANES - a.shape[1])))


def _group_lanes(v):
    return jnp.pad(v.reshape(SSD_N_GROUPS, 1, 8), ((0, 0), (0, 0), (0, LANES - 8)))


def kernel(x, p, norm_w, ssd_in_w, ssd_conv_w, ssd_conv_b, ssd_dt_bias, ssd_a_log, ssd_d, ssd_gnorm_w, ssd_out_w, sb_in_w, sb_qn_w, sb_kn_w, sb_out_w, ple_norm_w, ple_gate_w, ple_proj_w, loss_target, m_norm_w, m_ssd_in_w, m_ssd_conv_w, m_ssd_conv_b, m_ssd_dt_bias, m_ssd_a_log, m_ssd_d, m_ssd_gnorm_w, m_ssd_out_w, m_sb_in_w, m_sb_qn_w, m_sb_kn_w, m_sb_out_w, m_ple_norm_w, m_ple_gate_w, m_ple_proj_w, v_norm_w, v_ssd_in_w, v_ssd_conv_w, v_ssd_conv_b, v_ssd_dt_bias, v_ssd_a_log, v_ssd_d, v_ssd_gnorm_w, v_ssd_out_w, v_sb_in_w, v_sb_qn_w, v_sb_kn_w, v_sb_out_w, v_ple_norm_w, v_ple_gate_w, v_ple_proj_w):
    w_in = dict(norm_w=norm_w, ssd_in_w=ssd_in_w, ssd_conv_w=ssd_conv_w, ssd_conv_b=ssd_conv_b, ssd_dt_bias=ssd_dt_bias,
                ssd_a_log=ssd_a_log, ssd_d=ssd_d, ssd_gnorm_w=ssd_gnorm_w, ssd_out_w=ssd_out_w, sb_in_w=sb_in_w, sb_qn_w=sb_qn_w,
                sb_kn_w=sb_kn_w, sb_out_w=sb_out_w, ple_norm_w=ple_norm_w, ple_gate_w=ple_gate_w, ple_proj_w=ple_proj_w)
    m_in = dict(norm_w=m_norm_w, ssd_in_w=m_ssd_in_w, ssd_conv_w=m_ssd_conv_w, ssd_conv_b=m_ssd_conv_b, ssd_dt_bias=m_ssd_dt_bias,
                ssd_a_log=m_ssd_a_log, ssd_d=m_ssd_d, ssd_gnorm_w=m_ssd_gnorm_w, ssd_out_w=m_ssd_out_w, sb_in_w=m_sb_in_w,
                sb_qn_w=m_sb_qn_w, sb_kn_w=m_sb_kn_w, sb_out_w=m_sb_out_w, ple_norm_w=m_ple_norm_w, ple_gate_w=m_ple_gate_w,
                ple_proj_w=m_ple_proj_w)
    v_in = dict(norm_w=v_norm_w, ssd_in_w=v_ssd_in_w, ssd_conv_w=v_ssd_conv_w, ssd_conv_b=v_ssd_conv_b, ssd_dt_bias=v_ssd_dt_bias,
                ssd_a_log=v_ssd_a_log, ssd_d=v_ssd_d, ssd_gnorm_w=v_ssd_gnorm_w, ssd_out_w=v_ssd_out_w, sb_in_w=v_sb_in_w,
                sb_qn_w=v_sb_qn_w, sb_kn_w=v_sb_kn_w, sb_out_w=v_sb_out_w, ple_norm_w=v_ple_norm_w, ple_gate_w=v_ple_gate_w,
                ple_proj_w=v_ple_proj_w)
    ix, iy, ic = lax.axis_index("x"), lax.axis_index("y"), lax.axis_index("c")
    chip = (2 * ix + iy).astype(jnp.int32)
    place = jnp.stack([chip, ic.astype(jnp.int32)])
    zero = jnp.zeros((), jnp.int32)
    big_names = [n for n, _, _ in _BIG]
    layers_of = {n: s[0] for n, s, _ in _BIG}
    cut_of = {n: cut for n, _, cut in _BIG}

    def layer_pieces(i):
        mixer = ("ssd_in_w", "ssd_out_w") if i % 2 == 0 else ("sb_in_w", "sb_out_w")
        return [(mixer[0], i // 2), (mixer[1], i // 2), ("ple_gate_w", i), ("ple_proj_w", i)]

    def names_of(pieces):
        return [n for n in big_names if any(n == q for q, _ in pieces)]

    mine = {n: w_in[n].astype(BF16) for n in big_names}
    first = layer_pieces(0)
    gathered = _comm_call([mine[n] for n in big_names], [jax.ShapeDtypeStruct((4,) + mine[n].shape, BF16) for n in big_names],
                          6 * len(first), [_ag_ici(first, big_names), _ag_pass_on(first, big_names, base=3 * len(first))],
                          "allgather_layer0")
    gw = {n: lax.dynamic_update_slice(g, mine[n][None], (chip, zero, zero, zero)) for n, g in zip(big_names, gathered)}

    def gather_rider(pieces):
        names = names_of(pieces)
        return names, _Rider([mine[n] for n in names], [gw[n] for n in names], 3 * len(pieces), _ag_ici(pieces, names))

    def landed(names, bufs):
        for n, g in zip(names, bufs):
            gw[n] = g

    def pass_on(pieces, call):
        names = names_of(pieces)
        landed(names, _comm_call([], [gw[n] for n in names], 3 * len(pieces), [_ag_pass_on(pieces, names)], call))

    onehot = (jnp.arange(4) == chip).astype(F32) * (ic == 0).astype(F32)
    cw_mine = onehot[:, None, None, None] * ssd_conv_w[None]
    cw_full = _allreduce_small(cw_mine.transpose(1, 2, 0, 3).reshape(-1, LANES), "gather_conv_w").reshape(2, SSD_D_CONV, SSD_CONV_DIM)

    def wmm(a, name, layer, *, dn="nn", res=None, call, rider=None):
        return _matmul(a, gw[name], dn=dn, res=res, b_lay=(cut_of[name], layer), name=call, rider=rider)

    h = x[0]
    target = loss_target[0]
    saved = []
    for i in range(DEPTH):
        j = i // 2
        nw = norm_w[i:i + 1]
        pw = ple_norm_w[i:i + 1]
        nxt = layer_pieces(i + 1) if i + 1 < DEPTH else None
        s = dict(h=h)
        u = _rms_fwd(h, nw, f"rms_{i}")
        s["u"] = u
        if i % 2 == 0:
            w_all = _cols_from_shards(gw["ssd_in_w"][:, j:j + 1])[0]
            w_zx = w_all[:, :SSD_D_INNER + SSD_CONV_DIM]
            w_dt = _pad_lanes(w_all[:, SSD_D_INNER + SSD_CONV_DIM:])
            if nxt:
                names, rider = gather_rider(nxt[1:])
                pzx, rode = _matmul(u, w_zx, name=f"ssd_in_{i}", rider=rider)
                landed(names, rode)
            else:
                pzx = _matmul(u, w_zx, name=f"ssd_in_{i}")
            pdt = _matmul(u, w_dt, name=f"ssd_indt_{i}")
            act = _conv_fwd(pzx, cw_full[j], ssd_conv_b[j:j + 1], f"conv_{i}")
            dtg = jnp.pad(pdt[:, :SSD_N_HEADS].reshape(-1, SSD_N_GROUPS, 8).transpose(1, 0, 2), ((0, 0), (0, 0), (0, LANES - 8)))
            vecs = (_group_lanes(ssd_dt_bias[j]), _group_lanes(ssd_a_log[j]), _group_lanes(ssd_d[j]))
            if nxt:
                names, rider = gather_rider(nxt[:1])
                (yn, states), rode = _ssd_fwd(act, dtg, *vecs, pzx, ssd_gnorm_w[j:j + 1], f"ssd_{i}", rider=rider)
                landed(names, rode)
                pass_on(nxt, f"allgather_pass_{i + 1}")
            else:
                yn, states = _ssd_fwd(act, dtg, *vecs, pzx, ssd_gnorm_w[j:j + 1], f"ssd_{i}")
            s.update(w_zx=w_zx, w_dt=w_dt, pzx=pzx, act=act, dtg=dtg, vecs=vecs, yn=yn, states=states)
            h1 = wmm(yn, "ssd_out_w", j, res=h, call=f"ssd_out_{i}")
        else:
            if nxt:
                names, rider = gather_rider(nxt[2:])
                proj, rode = wmm(u, "sb_in_w", j, call=f"sb_in_{i}", rider=rider)
                landed(names, rode)
            else:
                proj = wmm(u, "sb_in_w", j, call=f"sb_in_{i}")
            qn, kn, vb = _qknorm_fwd(proj, sb_qn_w[j:j + 1], sb_kn_w[j:j + 1], f"qknorm_{i}")
            if nxt:
                names, rider = gather_rider(nxt[:2])
                (og, o, tot), rode = _sb_fwd(qn, kn, vb, proj, f"sb_{i}", rider=rider)
                landed(names, rode)
                pass_on(nxt, f"allgather_pass_{i + 1}")
            else:
                og, o, tot = _sb_fwd(qn, kn, vb, proj, f"sb_{i}")
            s.update(proj=proj, qn=qn, kn=kn, vb=vb, og=og, o=o, tot=tot)
            h1 = wmm(og, "sb_out_w", j, res=h, call=f"sb_out_{i}")
        n2 = _rms_fwd(h1, pw, f"ple_rms_{i}")
        gl = wmm(n2, "ple_gate_w", i, call=f"ple_gate_{i}")
        pp = wmm(p[i, 0], "ple_proj_w", i, call=f"ple_proj_{i}")
        h = _ple_fwd(h1, pp, gl, f"ple_{i}")
        s.update(h1=h1, n2=n2, gl=gl, pp=pp)
        saved.append(s)

    dh, loss_lanes = _loss_bwd(h, target, "loss")

    wg = {}
    gsmall = {n: [None] * s[0] for n, s in _SMALL}
    g_conv_w = [None, None]
    scattered = {}
    pending = None

    def wgrad(a, b, name, layer, call):
        wg[(name, layer)] = _matmul(a, b, dn="tn", out_dtype=BF16, o_lay=(cut_of[name], 0, 1), name=call)

    for i in reversed(range(DEPTH)):
        j = i // 2
        s = saved[i]
        nw = norm_w[i:i + 1]
        pw = ple_norm_w[i:i + 1]
        rider = _scatter_rider(pending[1]) if pending else None
        dpp, dgl = _ple_bwd(dh, s["pp"], s["gl"], f"ple_bwd_{i}")
        wgrad(p[i, 0], dpp, "ple_proj_w", i, f"d_ple_proj_{i}")
        wgrad(s["n2"], dgl, "ple_gate_w", i, f"d_ple_gate_{i}")
        dn2 = wmm(dgl, "ple_gate_w", i, dn="nt", call=f"ple_gate_bwd_{i}")
        dh1, dpw = _rms_bwd(s["h1"], pw, dn2, dh, f"ple_rms_bwd_{i}")
        gsmall["ple_norm_w"][i] = dpw
        if i % 2 == 0:
            wgrad(s["yn"], dh1, "ssd_out_w", j, f"d_ssd_out_{i}")
            dyn = wmm(dh1, "ssd_out_w", j, dn="nt", call=f"ssd_out_bwd_{i}")
            outs = _ssd_bwd(s["act"], s["dtg"], *s["vecs"], s["pzx"], ssd_gnorm_w[j:j + 1], s["states"], dyn, f"ssd_bwd_{i}", rider=rider)
            if rider:
                outs, got = outs
                scattered[pending[0]] = (pending[1], got)
            dxs, dbm, dcm, ddtg, dbias, dalog, ddsk, dz, dgw = outs
            dact = jnp.concatenate([dxs, dbm, dcm], axis=1)
            dxbc, dcw, dcb = _conv_bwd(s["pzx"], cw_full[j], ssd_conv_b[j:j + 1], dact, f"conv_bwd_{i}")
            dzx = jnp.concatenate([dz, dxbc], axis=1)
            ddt = _pad_lanes(ddtg[:, :, :8].transpose(1, 0, 2).reshape(-1, SSD_N_HEADS)).astype(BF16)
            du = _matmul(dzx, s["w_zx"], dn="nt", name=f"ssd_in_bwd_{i}")
            du = _matmul(ddt, s["w_dt"], dn="nt", res=du, name=f"ssd_indt_bwd_{i}")
            dw_zx = _matmul(s["u"], dzx, dn="tn", out_dtype=BF16, name=f"d_ssd_in_{i}")
            dw_dt = _matmul(s["u"], ddt, dn="tn", out_dtype=BF16, name=f"d_ssd_indt_{i}")
            wg[("ssd_in_w", j)] = _shards_from_cols(jnp.concatenate([dw_zx, dw_dt[:, :SSD_N_HEADS]], axis=1)[None])
            g_conv_w[j] = dcw
            gsmall["ssd_conv_b"][j] = dcb
            gsmall["ssd_dt_bias"][j] = dbias[:, 0, :8].reshape(1, SSD_N_HEADS)
            gsmall["ssd_a_log"][j] = dalog[:, 0, :8].reshape(1, SSD_N_HEADS)
            gsmall["ssd_d"][j] = ddsk[:, 0, :8].reshape(1, SSD_N_HEADS)
            gsmall["ssd_gnorm_w"][j] = dgw
        else:
            wgrad(s["og"], dh1, "sb_out_w", j, f"d_sb_out_{i}")
            dog = wmm(dh1, "sb_out_w", j, dn="nt", call=f"sb_out_bwd_{i}")
            outs = _sb_bwd(s["qn"], s["kn"], s["vb"], s["proj"], s["o"], s["tot"], dog, f"sb_bwd_{i}", rider=rider)
            if rider:
                outs, got = outs
                scattered[pending[0]] = (pending[1], got)
            dqn, dkn, _, dvb, dg = outs
            dq, dk, dqw, dkw = _qknorm_bwd(s["proj"], sb_qn_w[j:j + 1], sb_kn_w[j:j + 1], dqn, dkn, f"qknorm_bwd_{i}")
            dproj = jnp.concatenate([dq, dk, dvb, dg], axis=1)
            du = wmm(dproj, "sb_in_w", j, dn="nt", call=f"sb_in_bwd_{i}")
            wgrad(s["u"], dproj, "sb_in_w", j, f"d_sb_in_{i}")
            gsmall["sb_qn_w"][j] = dqw
            gsmall["sb_kn_w"][j] = dkw
        dh, dnw = _rms_bwd(s["h"], nw, du, dh1, f"rms_bwd_{i}")
        gsmall["norm_w"][i] = dnw
        g_list = [wg[q] for q in layer_pieces(i)]
        pending = (i, [_pair_sum(g, r, place, f"rs_pair_sum_{i}_{k}")
                       for k, (g, r) in enumerate(zip(g_list, _pair_exchange(g_list, f"rs_pair_exchange_{i}")))])
    grad_x = dh[None]
    scattered[0] = (pending[1], _chip_scatter(pending[1], "rs_chip_scatter_0"))

    halves = []
    for n in big_names:
        buf = None
        for l in range(layers_of[n]):
            i = l if n.startswith("ple") else 2 * l + (0 if n.startswith("ssd") else 1)
            k = layer_pieces(i).index((n, l))
            buf = _chip_sum(scattered[i][0][k], scattered[i][1][k], place, l, layers_of[n], buf, f"rs_chip_sum_{n}_{l}")
        halves.append(buf)
    g_big = dict(zip(big_names, _pair_gather(halves)))

    small_parts = [jnp.concatenate(gsmall[n], axis=0).reshape(-1) for n, _ in _SMALL]
    small_parts.append(jnp.stack(g_conv_w).reshape(-1))
    small_parts.append(loss_lanes.reshape(-1))
    small_sum = _allreduce_small(jnp.concatenate(small_parts).reshape(-1, LANES), "allreduce_small").reshape(-1)
    g_small, off = {}, 0
    for n, shape in _SMALL:
        size = math.prod(shape)
        g_small[n] = small_sum[off:off + size].reshape(shape)
        off += size
    cw_size = 2 * SSD_D_CONV * SSD_CONV_DIM
    g_cw_full = small_sum[off:off + cw_size].reshape(2, SSD_D_CONV, 4, SSD_CONV_DIM // 4)
    g_small["ssd_conv_w"] = jnp.sum(g_cw_full * (jnp.arange(4) == chip).astype(F32)[None, None, :, None], axis=2)
    loss = 0.5 * jnp.sum(small_sum[off + cw_size:]) / D_MODEL

    grads, delta, new_m, new_v = {}, {}, {}, {}
    for n in big_names:
        grads[n] = g_big[n]
        delta[n], new_m[n], new_v[n] = _adamw(w_in[n], g_big[n], m_in[n], v_in[n], f"adamw_{n}")
    small_names = [n for n, _ in _SMALL] + ["ssd_conv_w"]
    pack = lambda d: jnp.concatenate([d[n].reshape(-1) for n in small_names]).reshape(1, -1, LANES)
    ds, ms, vs = _adamw(pack(w_in), pack(g_small), pack(m_in), pack(v_in), "adamw_small")
    off = 0
    for n in small_names:
        shape = w_in[n].shape
        size = math.prod(shape)
        grads[n] = g_small[n]
        delta[n] = ds.reshape(-1)[off:off + size].reshape(shape)
        new_m[n] = ms.reshape(-1)[off:off + size].reshape(shape)
        new_v[n] = vs.reshape(-1)[off:off + size].reshape(shape)
        off += size

    order = ["norm_w", "ssd_in_w", "ssd_conv_w", "ssd_conv_b", "ssd_dt_bias", "ssd_a_log", "ssd_d", "ssd_gnorm_w", "ssd_out_w",
             "sb_in_w", "sb_qn_w", "sb_kn_w", "sb_out_w", "ple_norm_w", "ple_gate_w", "ple_proj_w"]
    return (loss, grad_x, *[grads[n] for n in order], *[delta[n] for n in order], *[new_m[n] for n in order],
            *[new_v[n] for n in order])
```

```python
import functools
import math

import jax
import jax.numpy as jnp
from jax import lax
from jax.experimental import pallas as pl
from jax.experimental.pallas import tpu as pltpu

F32 = jnp.float32
BF16 = jnp.bfloat16
MESH = pl.DeviceIdType.MESH

D_MODEL = 2048
DEPTH = 4
SSD_D_INNER = 4096
SSD_N_GROUPS = 8
SSD_GROUP_W = SSD_D_INNER // SSD_N_GROUPS
SSD_D_STATE = 128
SSD_CHUNK = 128
SSD_CONV_DIM = 6144
SSD_D_CONV = 4
SSD_N_HEADS = 64
SB_HEAD_DIM = 128
SB_N_HEADS = 16
SB_WIDTH = 2048
NORM_EPS = 1e-6
GATED_NORM_EPS = 1e-5
ADAM_LR = 0.001
ADAM_B1 = 0.9
ADAM_B2 = 0.999
ADAM_EPS = 1e-08
ADAM_WD = 0.01
ADAM_STEP = 10

SSD_ZX = SSD_D_INNER + SSD_CONV_DIM
SSD_IN_DIM = SSD_ZX + SSD_N_HEADS
LANES = 128
BF16_ROWS = 16

_BIG = (
    ("ssd_in_w", (2, 2576, 2048), "stack"),
    ("ssd_out_w", (2, 1024, 2048), "row"),
    ("sb_in_w", (2, 2048, 2048), "col"),
    ("sb_out_w", (2, 512, 2048), "row"),
    ("ple_gate_w", (4, 512, 2048), "row"),
    ("ple_proj_w", (4, 256, 512), "col"),
)
_SMALL = (
    ("norm_w", (4, 2048)),
    ("ssd_conv_b", (2, 6144)),
    ("ssd_dt_bias", (2, 64)),
    ("ssd_a_log", (2, 64)),
    ("ssd_d", (2, 64)),
    ("ssd_gnorm_w", (2, 4096)),
    ("sb_qn_w", (2, 128)),
    ("sb_kn_w", (2, 128)),
    ("ple_norm_w", (4, 2048)),
)

_DN = {
    "nn": (((1,), (0,)), ((), ())),
    "nt": (((1,), (1,)), ((), ())),
    "tn": (((0,), (0,)), ((), ())),
}


def _dot(a, b, dn="nn"):
    return lax.dot_general(a.astype(BF16), b.astype(BF16), _DN[dn], preferred_element_type=F32)


@functools.partial(jax.custom_vjp, nondiff_argnums=(2,))
def _gdot(a, b, dn):
    return _dot(a, b, dn)


def _gdot_fwd(a, b, dn):
    return _dot(a, b, dn), (a, b)


def _gdot_bwd(dn, res, g):
    a, b = res
    if dn == "nn":
        return _dot(g, b, "nt"), _dot(a, g, "tn")
    if dn == "nt":
        return _dot(g, b, "nn"), _dot(g, a, "tn")
    return _dot(b, g, "nt"), _dot(a, g, "nn")


_gdot.defvjp(_gdot_fwd, _gdot_bwd)


def _split_dot(x, t, parts, x_left):
    acc = None
    r = x
    for i in range(parts):
        p = r.astype(BF16)
        d = lax.dot_general(p, t, _DN["nn"], preferred_element_type=F32) if x_left else lax.dot_general(
            t, p, _DN["nn"], preferred_element_type=F32)
        acc = d if acc is None else acc + d
        if i + 1 < parts:
            r = r - p.astype(F32)
    return acc


def _tri(n, lower, strict=False):
    r = lax.broadcasted_iota(jnp.int32, (n, n), 0)
    c = lax.broadcasted_iota(jnp.int32, (n, n), 1)
    keep = (r > c if strict else r >= c) if lower else (r < c if strict else r <= c)
    return jnp.where(keep, 1.0, 0.0).astype(BF16)


def _cumsum_rows_raw(x):
    return _split_dot(x, _tri(x.shape[0], True), 3, False)


@jax.custom_vjp
def _cumsum_rows(x):
    return _cumsum_rows_raw(x)


def _cumsum_rows_fwd(x):
    return _cumsum_rows_raw(x), None


def _cumsum_rows_bwd(_, g):
    return (_split_dot(g, _tri(g.shape[0], False), 3, False),)


_cumsum_rows.defvjp(_cumsum_rows_fwd, _cumsum_rows_bwd)


def _sigmoid(x):
    return 1.0 / (1.0 + jnp.exp(-x))


def _softplus(x):
    return jnp.maximum(x, 0.0) + jnp.log(1.0 + jnp.exp(-jnp.abs(x)))


def _rms(x, w, eps):
    return x * lax.rsqrt(jnp.mean(x * x, axis=-1, keepdims=True) + eps) * w


_ANY = pl.BlockSpec(memory_space=pl.ANY)


def _params(*sem):
    return pltpu.CompilerParams(dimension_semantics=sem)


class _Rider:
    def __init__(self, reads, writes, n_sems, issue):
        self.reads, self.writes, self.n_sems, self.issue = list(reads), list(writes), n_sems, issue


def _pcall(body, *, grid, in_specs, out_specs, out_shape, args, sem, name, scratch_shapes=(), aliases=None, rider=None):
    aliases = dict(aliases or {})
    if rider is None:
        outs = pl.pallas_call(body, grid=grid, in_specs=in_specs, out_specs=out_specs, out_shape=out_shape,
                              scratch_shapes=list(scratch_shapes), input_output_aliases=aliases,
                              compiler_params=_params(*sem), name=name)(*args)
        return list(outs), []
    n_in, n_out, n_scr, n_rd, n_wr = len(args), len(out_shape), len(scratch_shapes), len(rider.reads), len(rider.writes)
    passed = [k for k, w in enumerate(rider.writes) if not isinstance(w, jax.ShapeDtypeStruct)]
    for pos, k in enumerate(passed):
        aliases[n_in + n_rd + pos] = n_out + k

    def wrapped(*refs):
        ins = refs[:n_in]
        reads = refs[n_in:n_in + n_rd]
        base = n_in + n_rd + len(passed)
        outs = refs[base:base + n_out]
        writes = refs[base + n_out:base + n_out + n_wr]
        scr = refs[base + n_out + n_wr:base + n_out + n_wr + n_scr]
        send, recv = refs[-2:]
        first = last = None
        for d, n in enumerate(grid):
            i = pl.program_id(d)
            first = (i == 0) if first is None else first & (i == 0)
            last = (i == n - 1) if last is None else last & (i == n - 1)

        @pl.when(first)
        def _():
            for cp in rider.issue(reads, writes, send, recv)[0]:
                cp.start()

        body(*ins, *outs, *scr)

        @pl.when(last)
        def _():
            sends, arrivals = rider.issue(reads, writes, send, recv)
            for cp in arrivals:
                cp.wait_recv()
            for cp in sends:
                cp.wait_send()

    outs = pl.pallas_call(
        wrapped, grid=grid,
        in_specs=list(in_specs) + [_ANY] * (n_rd + len(passed)),
        out_specs=list(out_specs) + [_ANY] * n_wr,
        out_shape=list(out_shape) + [jax.ShapeDtypeStruct(w.shape, w.dtype) for w in rider.writes],
        scratch_shapes=list(scratch_shapes) + [pltpu.SemaphoreType.DMA((rider.n_sems,)), pltpu.SemaphoreType.DMA((rider.n_sems,))],
        input_output_aliases=aliases, compiler_params=_params(*(["arbitrary"] * len(grid))), name=name,
    )(*args, *rider.reads, *[rider.writes[k] for k in passed])
    return list(outs[:n_out]), list(outs[n_out:])


MM_TK = 2048


def _pick(dim, pref, unit=None):
    t = pref
    while t >= LANES:
        if dim % t == 0 and (unit is None or unit % t == 0):
            return t
        t //= 2
    return dim


def _matmul(a, b, *, dn="nn", res=None, out_dtype=F32, name, b_lay=None, o_lay=None, o_buf=None, out_rows=None, rider=None):
    if dn == "tn":
        k_dim, m_dim = a.shape
    else:
        m_dim, k_dim = a.shape
    unit_m = unit_n = unit_k = None
    if b_lay is None:
        n_dim = b.shape[0] if dn == "nt" else b.shape[1]
    elif b_lay[0] == "stack":
        cut, layer, rows = b_lay
        cols = b.shape[2]
        n_dim = cols if dn == "nn" else rows
        assert k_dim == (rows if dn == "nn" else cols) and dn != "tn"
    else:
        cut, layer = b_lay
        r, c = b.shape[2:]
        rows, cols = (4 * r, c) if cut == "row" else (r, 4 * c)
        n_dim = cols if dn == "nn" else rows
        assert k_dim == (rows if dn == "nn" else cols) and dn != "tn"
        if (cut == "row") == (dn == "nn"):
            unit_k = r if cut == "row" else c
        else:
            unit_n = r if cut == "row" else c
    if o_lay is not None:
        o_cut, o_layer, o_layers = o_lay
        if o_cut == "row":
            unit_m = m_dim // 4
        else:
            unit_n = n_dim // 4
    tm, tn, tk = _pick(m_dim, 1024, unit_m), _pick(n_dim, 1024, unit_n), _pick(k_dim, MM_TK, unit_k)
    nk = k_dim // tk
    a_spec = pl.BlockSpec((tk, tm), lambda i, j, k: (k, i)) if dn == "tn" else pl.BlockSpec((tm, tk), lambda i, j, k: (i, k))
    if b_lay is None:
        b_spec = pl.BlockSpec((tn, tk), lambda i, j, k: (j, k)) if dn == "nt" else pl.BlockSpec((tk, tn), lambda i, j, k: (k, j))
    elif cut == "stack":
        b_spec = (pl.BlockSpec((None, tk, tn), lambda i, j, k: (layer, k, j)) if dn == "nn" else
                  pl.BlockSpec((None, tn, tk), lambda i, j, k: (layer, j, k)))
    elif dn == "nn" and cut == "row":
        per = r // tk
        b_spec = pl.BlockSpec((None, None, tk, tn), lambda i, j, k: (k // per, layer, k % per, j))
    elif dn == "nn":
        per = c // tn
        b_spec = pl.BlockSpec((None, None, tk, tn), lambda i, j, k: (j // per, layer, k, j % per))
    elif cut == "row":
        per = r // tn
        b_spec = pl.BlockSpec((None, None, tn, tk), lambda i, j, k: (j // per, layer, j % per, k))
    else:
        per = c // tk
        b_spec = pl.BlockSpec((None, None, tn, tk), lambda i, j, k: (k // per, layer, j, k % per))
    r_spec = pl.BlockSpec((tm, tn), lambda i, j, k: (i, j))
    if o_lay is None:
        o_spec = r_spec
        out_shape = jax.ShapeDtypeStruct((out_rows or m_dim, n_dim), out_dtype)
    elif o_cut == "row":
        per_o = unit_m // tm
        o_spec = pl.BlockSpec((None, None, tm, tn), lambda i, j, k: (i // per_o, o_layer, i % per_o, j))
        out_shape = jax.ShapeDtypeStruct((4, o_layers, unit_m, n_dim), out_dtype)
    else:
        per_o = unit_n // tn
        o_spec = pl.BlockSpec((None, None, tm, tn), lambda i, j, k: (j // per_o, o_layer, i, j % per_o))
        out_shape = jax.ShapeDtypeStruct((4, o_layers, m_dim, unit_n), out_dtype)
    has_res = res is not None
    has_buf = o_buf is not None

    def body(*refs):
        a_ref, b_ref = refs[:2]
        r_ref = refs[2] if has_res else None
        o_ref = refs[-1] if nk == 1 else refs[-2]

        def finish(v):
            if has_res:
                v = v + r_ref[...]
            o_ref[...] = v.astype(o_ref.dtype)

        if nk == 1:
            finish(_dot(a_ref[...], b_ref[...], dn))
            return
        acc_ref = refs[-1]
        k = pl.program_id(2)

        @pl.when(k == 0)
        def _():
            acc_ref[...] = jnp.zeros_like(acc_ref)

        acc_ref[...] += _dot(a_ref[...], b_ref[...], dn)

        @pl.when(k == nk - 1)
        def _():
            finish(acc_ref[...])

    args = [a, b] + ([res] if has_res else []) + ([o_buf] if has_buf else [])
    outs, rode = _pcall(
        body, grid=(m_dim // tm, n_dim // tn, nk),
        in_specs=[a_spec, b_spec] + ([r_spec] if has_res else []) + ([_ANY] if has_buf else []),
        out_specs=[o_spec], out_shape=[out_shape],
        scratch_shapes=[] if nk == 1 else [pltpu.VMEM((tm, tn), F32)],
        aliases={len(args) - 1: 0} if has_buf else {},
        args=args, sem=("parallel", "parallel", "arbitrary"), name=name, rider=rider)
    return (outs[0], rode) if rider is not None else outs[0]


def _rowcall(fn, rows, consts, outs, accs, *, name, tm=256):
    args = list(rows) + list(consts)
    in_specs = [pl.BlockSpec((tm, r.shape[1]), lambda i: (i, 0)) for r in rows]
    in_specs += [pl.BlockSpec(c.shape, lambda i: (0, 0)) for c in consts]
    s_dim = args[0].shape[0]
    n_in, n_out = len(args), len(outs)
    out_shape = [jax.ShapeDtypeStruct((s_dim, w), dt) for w, dt in outs] + [jax.ShapeDtypeStruct(s, F32) for s in accs]
    out_specs = [pl.BlockSpec((tm, w), lambda i: (i, 0)) for w, _ in outs] + [pl.BlockSpec(s, lambda i: (0, 0)) for s in accs]

    def body(*refs):
        vals = fn(*[r[...] for r in refs[:n_in]])
        o_refs = refs[n_in:n_in + n_out]
        a_refs = refs[n_in + n_out:]
        for o, v in zip(o_refs, vals[:n_out]):
            o[...] = v.astype(o.dtype)
        if a_refs:
            @pl.when(pl.program_id(0) == 0)
            def _():
                for a_ref in a_refs:
                    a_ref[...] = jnp.zeros_like(a_ref)

            for a_ref, v in zip(a_refs, vals[n_out:]):
                a_ref[...] += v

    return pl.pallas_call(
        body, grid=(s_dim // tm,), in_specs=in_specs, out_specs=out_specs, out_shape=out_shape,
        compiler_params=_params("arbitrary"), name=name,
    )(*args)


def _rms_fwd(h, w, name):
    return _rowcall(lambda x, w_: (_rms(x, w_, NORM_EPS),), [h], [w], [(h.shape[1], BF16)], [], name=name)[0]


def _rms_bwd(h, w, dy, dres, name):
    def fn(x, dy_, dres_, w_):
        _, vjp = jax.vjp(lambda a, b: _rms(a, b, NORM_EPS), x, w_)
        dx, dw = vjp(dy_)
        return dx + dres_, dw

    return _rowcall(fn, [h, dy, dres], [w], [(h.shape[1], F32)], [w.shape], name=name)


def _ple_fwd(h1, pp, gl, name):
    return _rowcall(lambda a, b, c: (a + b * _sigmoid(c),), [h1, pp, gl], [], [(h1.shape[1], F32)], [], name=name)[0]


def _ple_bwd(dh2, pp, gl, name):
    def fn(d, b, c):
        gate = _sigmoid(c)
        return d * gate, d * b * gate * (1.0 - gate)

    return _rowcall(fn, [dh2, pp, gl], [], [(dh2.shape[1], BF16), (dh2.shape[1], BF16)], [], name=name)


def _loss_bwd(y, target, name):
    width = y.shape[1]

    def fn(a, t):
        d = a - t
        col = jnp.sum(d * d, axis=0, keepdims=True)
        part = col[:, 0:LANES]
        for j in range(1, width // LANES):
            part = part + col[:, j * LANES:(j + 1) * LANES]
        return d * (1.0 / width), part

    return _rowcall(fn, [y, target], [], [(width, F32)], [(1, LANES)], name=name)


CONV_TC = 256


def _shift_down(x, j):
    if j == 0:
        return x
    row = lax.broadcasted_iota(jnp.int32, x.shape, 0)
    return jnp.where(row >= j, pltpu.roll(x, j, 0), 0.0)


def _shift_up(x, j):
    if j == 0:
        return x
    n = x.shape[0]
    row = lax.broadcasted_iota(jnp.int32, x.shape, 0)
    return jnp.where(row < n - j, pltpu.roll(x, n - j, 0), 0.0)


def _conv_fwd(pzx, cw, cb, name):
    s_dim = pzx.shape[0]
    off = SSD_D_INNER // CONV_TC

    def body(x_ref, w_ref, b_ref, o_ref):
        x = x_ref[...]
        w = w_ref[...]
        y = b_ref[...] + w[3:4, :] * x
        for k in range(SSD_D_CONV - 1):
            y = y + w[k:k + 1, :] * _shift_down(x, SSD_D_CONV - 1 - k)
        o_ref[...] = y * _sigmoid(y)

    return pl.pallas_call(
        body, grid=(SSD_CONV_DIM // CONV_TC,),
        in_specs=[pl.BlockSpec((s_dim, CONV_TC), lambda j: (0, off + j)), pl.BlockSpec((SSD_D_CONV, CONV_TC), lambda j: (0, j)),
                  pl.BlockSpec((1, CONV_TC), lambda j: (0, j))],
        out_specs=pl.BlockSpec((s_dim, CONV_TC), lambda j: (0, j)),
        out_shape=jax.ShapeDtypeStruct((s_dim, SSD_CONV_DIM), F32),
        compiler_params=_params("parallel"), name=name,
    )(pzx, cw, cb)


def _conv_bwd(pzx, cw, cb, dact, dzx, name):
    s_dim = pzx.shape[0]
    off = SSD_D_INNER // CONV_TC

    def body(x_ref, w_ref, b_ref, d_ref, _, dx_ref, dw_ref, db_ref):
        x = x_ref[...]
        w = w_ref[...]
        xs = [_shift_down(x, SSD_D_CONV - 1 - k) for k in range(SSD_D_CONV)]
        y = b_ref[...]
        for k in range(SSD_D_CONV):
            y = y + w[k:k + 1, :] * xs[k]
        sg = _sigmoid(y)
        dy = d_ref[...] * (sg * (1.0 + y * (1.0 - sg)))
        dx = w[3:4, :] * dy
        for k in range(SSD_D_CONV - 1):
            dx = dx + w[k:k + 1, :] * _shift_up(dy, SSD_D_CONV - 1 - k)
        dx_ref[...] = dx.astype(dx_ref.dtype)
        for k in range(SSD_D_CONV):
            dw_ref[k:k + 1, :] = jnp.sum(dy * xs[k], axis=0, keepdims=True)
        db_ref[...] = jnp.sum(dy, axis=0, keepdims=True)

    col = pl.BlockSpec((s_dim, CONV_TC), lambda j: (0, j))
    return pl.pallas_call(
        body, grid=(SSD_CONV_DIM // CONV_TC,),
        in_specs=[pl.BlockSpec((s_dim, CONV_TC), lambda j: (0, off + j)), pl.BlockSpec((SSD_D_CONV, CONV_TC), lambda j: (0, j)),
                  pl.BlockSpec((1, CONV_TC), lambda j: (0, j)), col, _ANY],
        out_specs=[pl.BlockSpec((s_dim, CONV_TC), lambda j: (0, off + j)), pl.BlockSpec((SSD_D_CONV, CONV_TC), lambda j: (0, j)),
                   pl.BlockSpec((1, CONV_TC), lambda j: (0, j))],
        out_shape=[jax.ShapeDtypeStruct(dzx.shape, dzx.dtype), jax.ShapeDtypeStruct((SSD_D_CONV, SSD_CONV_DIM), F32),
                   jax.ShapeDtypeStruct((1, SSD_CONV_DIM), F32)],
        input_output_aliases={4: 0}, compiler_params=_params("parallel"), name=name,
    )(pzx, cw, cb, dact, dzx)


def _ssd_step(xs, bm, cm, dtraw, bias, alog, dskip, st_in, z, gw, dot, cumsum):
    n = xs.shape[0]
    lane = lax.broadcasted_iota(jnp.int32, (1, LANES), 1)
    sub = lax.broadcasted_iota(jnp.int32, (LANES, 1), 0)
    left = (lane < 64).astype(F32)
    right = 1.0 - left
    top = (sub < 64).astype(F32)
    bot = 1.0 - top
    row = lax.broadcasted_iota(jnp.int32, (n, n), 0)
    colm = lax.broadcasted_iota(jnp.int32, (n, n), 1)
    causal = row >= colm

    dt = _softplus(dtraw + bias)
    adt = dt * (-jnp.exp(alog))
    acum = cumsum(adt)
    acum_t = acum.T
    last = jnp.sum(adt, axis=0, keepdims=True)
    scores = dot(cm, bm, "nt")

    def lane_of(v, h):
        return jnp.sum(v * (lane == h).astype(F32), axis=1, keepdims=True)

    ys, sts = [], []
    for pr in range(4):
        heads = (2 * pr, 2 * pr + 1)
        ac = [lane_of(acum, h) for h in heads]
        ar = [jnp.sum(acum_t * (sub == h).astype(F32), axis=0, keepdims=True) for h in heads]
        dth = [lane_of(dt, h) for h in heads]
        la = [lane_of(last, h) for h in heads]
        dk = [lane_of(dskip, h) for h in heads]
        x2 = xs[:, pr * LANES:(pr + 1) * LANES]
        xdt = x2 * (dth[0] * left + dth[1] * right)
        yd = None
        for i, side in enumerate((left, right)):
            decay = jnp.where(causal, jnp.exp(jnp.minimum(ac[i] - ar[i], 0.0)), 0.0)
            t = dot(scores * decay, xdt * side, "nn")
            yd = t if yd is None else yd + t
        st2 = st_in[pr * LANES:(pr + 1) * LANES, :]
        yo = dot(cm, st2, "nt") * (jnp.exp(ac[0]) * left + jnp.exp(ac[1]) * right)
        dte = jnp.exp(la[0] - ac[0]) * left + jnp.exp(la[1] - ac[1]) * right
        cs = dot(xdt * dte, bm, "tn")
        sts.append(st2 * (jnp.exp(la[0]) * top + jnp.exp(la[1]) * bot) + cs)
        ys.append(yd + yo + (dk[0] * left + dk[1] * right) * x2)
    y = jnp.concatenate(ys, axis=1)
    yg = y * (z * _sigmoid(z))
    yn = yg * lax.rsqrt(jnp.mean(yg * yg, axis=-1, keepdims=True) + GATED_NORM_EPS) * gw
    return yn, jnp.concatenate(sts, axis=0)


def _ssd_specs(n_chunks, rev):
    ci = (lambda c: n_chunks - 1 - c) if rev else (lambda c: c)
    n_x = SSD_D_INNER // LANES
    return dict(
        xs=pl.BlockSpec((SSD_CHUNK, SSD_GROUP_W), lambda g, c: (ci(c), g)),
        bm=pl.BlockSpec((SSD_CHUNK, LANES), lambda g, c: (ci(c), n_x + g)),
        cm=pl.BlockSpec((SSD_CHUNK, LANES), lambda g, c: (ci(c), n_x + SSD_N_GROUPS + g)),
        dt=pl.BlockSpec((None, SSD_CHUNK, LANES), lambda g, c: (g, ci(c), 0)),
        vec=pl.BlockSpec((None, 1, LANES), lambda g, c: (g, 0, 0)),
        z=pl.BlockSpec((SSD_CHUNK, SSD_GROUP_W), lambda g, c: (ci(c), g)),
        gw=pl.BlockSpec((1, SSD_GROUP_W), lambda g, c: (0, g)),
        st=pl.BlockSpec((None, None, SSD_GROUP_W, SSD_D_STATE), lambda g, c: (g, ci(c), 0, 0)),
    )


def _ssd_fwd(act, dtg, bias, alog, dskip, pzx, gw, name, rider=None):
    s_dim = act.shape[0]
    n_chunks = s_dim // SSD_CHUNK
    sp = _ssd_specs(n_chunks, False)

    def body(xs, bm, cm, dt, b_ref, a_ref, d_ref, z, gw_ref, yn_ref, st_ref, state):
        @pl.when(pl.program_id(1) == 0)
        def _():
            state[...] = jnp.zeros_like(state)

        st_in = state[...]
        st_ref[...] = st_in
        yn, st_out = _ssd_step(xs[...], bm[...], cm[...], dt[...], b_ref[...], a_ref[...], d_ref[...], st_in, z[...], gw_ref[...],
                               _dot, _cumsum_rows_raw)
        yn_ref[...] = yn.astype(yn_ref.dtype)
        state[...] = st_out

    outs, rode = _pcall(
        body, grid=(SSD_N_GROUPS, n_chunks),
        in_specs=[sp["xs"], sp["bm"], sp["cm"], sp["dt"], sp["vec"], sp["vec"], sp["vec"], sp["z"], sp["gw"]],
        out_specs=[sp["xs"], sp["st"]],
        out_shape=[jax.ShapeDtypeStruct((s_dim, SSD_D_INNER), BF16),
                   jax.ShapeDtypeStruct((SSD_N_GROUPS, n_chunks, SSD_GROUP_W, SSD_D_STATE), F32)],
        scratch_shapes=[pltpu.VMEM((SSD_GROUP_W, SSD_D_STATE), F32)],
        args=[act, act, act, dtg, bias, alog, dskip, pzx, gw], sem=("parallel", "arbitrary"), name=name, rider=rider)
    return (outs, rode) if rider is not None else outs


def _ssd_bwd(act, dtg, bias, alog, dskip, pzx, gw, states, dyn, name, rider=None):
    s_dim = act.shape[0]
    n_chunks = s_dim // SSD_CHUNK
    sp = _ssd_specs(n_chunks, True)
    rc = lambda c: n_chunks - 1 - c

    def body(xs, bm, cm, dt, b_ref, a_ref, d_ref, z, gw_ref, st_ref, dyn_ref,
             dxs_ref, dbm_ref, dcm_ref, ddt_ref, db_ref, da_ref, dd_ref, dz_ref, dgw_ref, dstate):
        first = pl.program_id(1) == 0

        @pl.when(first)
        def _():
            dstate[...] = jnp.zeros_like(dstate)
            db_ref[...] = jnp.zeros_like(db_ref)
            da_ref[...] = jnp.zeros_like(da_ref)
            dd_ref[...] = jnp.zeros_like(dd_ref)
            dgw_ref[...] = jnp.zeros_like(dgw_ref)

        fn = functools.partial(_ssd_step, dot=_gdot, cumsum=_cumsum_rows)
        _, vjp = jax.vjp(fn, xs[...], bm[...], cm[...], dt[...], b_ref[...], a_ref[...], d_ref[...], st_ref[...], z[...], gw_ref[...])
        dxs, dbm, dcm, ddt, db, da, dd, dst, dz, dgw = vjp((dyn_ref[...], dstate[...]))
        dxs_ref[...] = dxs
        dbm_ref[...] = dbm
        dcm_ref[...] = dcm
        ddt_ref[...] = ddt
        dz_ref[...] = dz.astype(dz_ref.dtype)
        db_ref[...] += db
        da_ref[...] += da
        dd_ref[...] += dd
        dgw_ref[...] += dgw
        dstate[...] = dst

    bc = pl.BlockSpec((SSD_CHUNK, LANES), lambda g, c: (rc(c), g))
    outs, rode = _pcall(
        body, grid=(SSD_N_GROUPS, n_chunks),
        in_specs=[sp["xs"], sp["bm"], sp["cm"], sp["dt"], sp["vec"], sp["vec"], sp["vec"], sp["z"], sp["gw"], sp["st"], sp["xs"]],
        out_specs=[sp["xs"], bc, bc, sp["dt"], sp["vec"], sp["vec"], sp["vec"], sp["xs"], sp["gw"]],
        out_shape=[jax.ShapeDtypeStruct((s_dim, SSD_D_INNER), F32),
                   jax.ShapeDtypeStruct((s_dim, SSD_N_GROUPS * SSD_D_STATE), F32),
                   jax.ShapeDtypeStruct((s_dim, SSD_N_GROUPS * SSD_D_STATE), F32),
                   jax.ShapeDtypeStruct((SSD_N_GROUPS, s_dim, LANES), F32),
                   jax.ShapeDtypeStruct((SSD_N_GROUPS, 1, LANES), F32),
                   jax.ShapeDtypeStruct((SSD_N_GROUPS, 1, LANES), F32),
                   jax.ShapeDtypeStruct((SSD_N_GROUPS, 1, LANES), F32),
                   jax.ShapeDtypeStruct((s_dim, SSD_ZX), BF16),
                   jax.ShapeDtypeStruct((1, SSD_D_INNER), F32)],
        scratch_shapes=[pltpu.VMEM((SSD_GROUP_W, SSD_D_STATE), F32)],
        args=[act, act, act, dtg, bias, alog, dskip, pzx, gw, states, dyn], sem=("arbitrary", "arbitrary"), name=name, rider=rider)
    return (outs, rode) if rider is not None else outs


SB_T = 128
SB_GROUP = 8
SB_WIDE = SB_GROUP * SB_T
SB_HB = 2
SB_SCALE = 1.0 / math.sqrt(SB_HEAD_DIM)


def _qknorm_fwd(proj, qw, kw, name, tm=512):
    s_dim = proj.shape[0]

    def body(q_ref, k_ref, v_ref, qw_ref, kw_ref, qo, ko, vo):
        qo[...] = _rms(q_ref[...], qw_ref[...], NORM_EPS).astype(BF16)
        ko[...] = _rms(k_ref[...], kw_ref[...], NORM_EPS).astype(BF16)
        vo[...] = v_ref[...].astype(BF16)

    blk = lambda o: pl.BlockSpec((tm, SB_HEAD_DIM), lambda i, h: (i, o + h))
    vec = pl.BlockSpec((1, SB_HEAD_DIM), lambda i, h: (0, 0))
    return pl.pallas_call(
        body, grid=(s_dim // tm, SB_N_HEADS),
        in_specs=[blk(0), blk(SB_N_HEADS), blk(2 * SB_N_HEADS), vec, vec],
        out_specs=[blk(0)] * 3,
        out_shape=[jax.ShapeDtypeStruct((s_dim, SB_WIDTH), BF16)] * 3,
        compiler_params=_params("parallel", "parallel"), name=name,
    )(proj, proj, proj, qw, kw)


def _qknorm_bwd(proj, qw, kw, dqn, dkn, name, tm=512):
    s_dim = proj.shape[0]

    def body(q_ref, k_ref, dq_ref, dk_ref, qw_ref, kw_ref, dqo, dko, dqw, dkw):
        @pl.when((pl.program_id(0) == 0) & (pl.program_id(1) == 0))
        def _():
            dqw[...] = jnp.zeros_like(dqw)
            dkw[...] = jnp.zeros_like(dkw)

        fn = lambda a, b: _rms(a, b, NORM_EPS)
        _, vq = jax.vjp(fn, q_ref[...], qw_ref[...])
        dq, dw = vq(dq_ref[...])
        dqo[...] = dq.astype(BF16)
        dqw[...] += dw
        _, vk = jax.vjp(fn, k_ref[...], kw_ref[...])
        dk, dw = vk(dk_ref[...])
        dko[...] = dk.astype(BF16)
        dkw[...] += dw

    blk = lambda o: pl.BlockSpec((tm, SB_HEAD_DIM), lambda i, h: (i, o + h))
    vec = pl.BlockSpec((1, SB_HEAD_DIM), lambda i, h: (0, 0))
    return pl.pallas_call(
        body, grid=(s_dim // tm, SB_N_HEADS),
        in_specs=[blk(0), blk(SB_N_HEADS), blk(0), blk(0), vec, vec],
        out_specs=[blk(0), blk(0), vec, vec],
        out_shape=[jax.ShapeDtypeStruct((s_dim, SB_WIDTH), BF16)] * 2 + [jax.ShapeDtypeStruct((1, SB_HEAD_DIM), F32)] * 2,
        compiler_params=_params("arbitrary", "arbitrary"), name=name,
    )(proj, proj, dqn, dkn, qw, kw)


def _sb_logits(q, k, strict):
    z = _dot(q, k, "nt") * SB_SCALE
    lb = jnp.minimum(z, 0.0) - jnp.log(1.0 + jnp.exp(-jnp.abs(z)))
    lm = lb - z
    if strict is not None:
        lm = jnp.where(strict, lm, 0.0)
    return lb, lm


def _sb_strict(qi, grp):
    r = lax.broadcasted_iota(jnp.int32, (SB_T, SB_WIDE), 0) + qi * SB_T
    c = lax.broadcasted_iota(jnp.int32, (SB_T, SB_WIDE), 1) + grp * SB_WIDE
    return c < r


def _head_lanes(hh):
    return slice(hh * SB_HEAD_DIM, (hh + 1) * SB_HEAD_DIM)


def _sb_fwd(qn, kn, vb, proj, name, rider=None):
    s_dim = qn.shape[0]
    nq = s_dim // SB_T
    assert nq % SB_GROUP == 0

    def body(q_ref, k_ref, v_ref, g_ref, og_ref, o_ref, t_ref):
        qi = pl.program_id(1)
        top = qi // SB_GROUP
        after = _tri(SB_T, True, strict=True)
        qs = [q_ref[:, _head_lanes(hh)] for hh in range(SB_HB)]

        def step(grp, masked, carries):
            start = pl.multiple_of(grp * SB_WIDE, SB_WIDE)
            strict = _sb_strict(qi, grp) if masked else None
            out = []
            for hh in range(SB_HB):
                o_acc, cr = carries[hh]
                k = k_ref[pl.ds(start, SB_WIDE), _head_lanes(hh)]
                v = v_ref[pl.ds(start, SB_WIDE), _head_lanes(hh)]
                lb, lm = _sb_logits(qs[hh], k, strict)
                rest = [None] * SB_GROUP
                for t in reversed(range(SB_GROUP)):
                    lm_t = lm[:, t * SB_T:(t + 1) * SB_T]
                    rest[t] = cr + _split_dot(lm_t, after, 2, True)
                    cr = cr + jnp.sum(lm_t, axis=1, keepdims=True)
                a = jnp.exp(lb + jnp.concatenate(rest, axis=1))
                if masked:
                    a = jnp.where(strict, a, 0.0)
                out.append((o_acc + _dot(a, v), cr))
            return tuple(out)

        init = tuple((jnp.zeros((SB_T, SB_HEAD_DIM), F32), jnp.zeros((SB_T, 1), F32)) for _ in range(SB_HB))
        carries = step(top, True, init)
        carries = lax.fori_loop(0, top, lambda i, c: step(top - 1 - i, False, c), carries)
        for hh in range(SB_HB):
            o, tot = carries[hh]
            g = g_ref[:, _head_lanes(hh)]
            o_ref[:, _head_lanes(hh)] = o
            og_ref[:, _head_lanes(hh)] = (o * (g * _sigmoid(g))).astype(og_ref.dtype)
            t_ref[hh] = jnp.broadcast_to(tot, (SB_T, LANES))

    wide = SB_HB * SB_HEAD_DIM
    qb = pl.BlockSpec((SB_T, wide), lambda h, i: (i, h))
    kv = pl.BlockSpec((s_dim, wide), lambda h, i: (0, h))
    outs, rode = _pcall(
        body, grid=(SB_N_HEADS // SB_HB, nq),
        in_specs=[qb, kv, kv, pl.BlockSpec((SB_T, wide), lambda h, i: (i, 3 * SB_N_HEADS // SB_HB + h))],
        out_specs=[qb, qb, pl.BlockSpec((SB_HB, SB_T, LANES), lambda h, i: (h, i, 0))],
        out_shape=[jax.ShapeDtypeStruct((s_dim, SB_WIDTH), BF16), jax.ShapeDtypeStruct((s_dim, SB_WIDTH), F32),
                   jax.ShapeDtypeStruct((SB_N_HEADS, s_dim, LANES), F32)],
        args=[qn, kn, vb, proj], sem=("parallel", "arbitrary"), name=name, rider=rider)
    return (outs, rode) if rider is not None else outs


def _sb_bwd(qn, kn, vb, proj, o, tot, dog, name, rider=None):
    s_dim = qn.shape[0]
    nq = s_dim // SB_T
    assert nq % SB_GROUP == 0

    def body(q_ref, k_ref, v_ref, g_ref, o_ref, t_ref, dog_ref, dq_ref, dk_ref, dv_ref, dvb_ref, dg_ref):
        qi = pl.program_id(1)
        top = qi // SB_GROUP

        @pl.when(qi == 0)
        def _():
            dk_ref[...] = jnp.zeros_like(dk_ref)
            dv_ref[...] = jnp.zeros_like(dv_ref)

        after = _tri(SB_T, True, strict=True)
        before = _tri(SB_T, False, strict=True)
        qs, dos, totals = [], [], []
        for hh in range(SB_HB):
            g = g_ref[:, _head_lanes(hh)]
            sg = _sigmoid(g)
            dog_v = dog_ref[:, _head_lanes(hh)]
            dg_ref[:, _head_lanes(hh)] = (dog_v * o_ref[:, _head_lanes(hh)] * (sg * (1.0 + g * (1.0 - sg)))).astype(dg_ref.dtype)
            dos.append((dog_v * (g * sg)).astype(BF16))
            qs.append(q_ref[:, _head_lanes(hh)])
            totals.append(t_ref[hh][:, 0:1])

        def step(grp, masked, carries):
            start = pl.multiple_of(grp * SB_WIDE, SB_WIDE)
            strict = _sb_strict(qi, grp) if masked else None
            out = []
            for hh in range(SB_HB):
                dq_acc, cp, ce = carries[hh]
                q, do = qs[hh], dos[hh]
                k = k_ref[pl.ds(start, SB_WIDE), _head_lanes(hh)]
                v = v_ref[pl.ds(start, SB_WIDE), _head_lanes(hh)]
                lb, lm = _sb_logits(q, k, strict)
                rest = []
                for t in range(SB_GROUP):
                    lm_t = lm[:, t * SB_T:(t + 1) * SB_T]
                    cp = cp + jnp.sum(lm_t, axis=1, keepdims=True)
                    rest.append((totals[hh] - cp) + _split_dot(lm_t, after, 2, True))
                a = jnp.exp(lb + jnp.concatenate(rest, axis=1))
                if masked:
                    a = jnp.where(strict, a, 0.0)
                e = a * _dot(do, v, "nt")
                excl = []
                for t in range(SB_GROUP):
                    e_t = e[:, t * SB_T:(t + 1) * SB_T]
                    excl.append(ce + _split_dot(e_t, before, 2, True))
                    ce = ce + jnp.sum(e_t, axis=1, keepdims=True)
                eex = jnp.concatenate(excl, axis=1)
                if masked:
                    eex = jnp.where(strict, eex, 0.0)
                sig = jnp.exp(lb)
                dz = (e * (1.0 - sig) - eex * sig) * SB_SCALE
                dv_ref[pl.ds(start, SB_WIDE), _head_lanes(hh)] += _dot(a, do, "tn")
                dk_ref[pl.ds(start, SB_WIDE), _head_lanes(hh)] += _dot(dz, q, "tn")
                out.append((dq_acc + _dot(dz, k), cp, ce))
            return tuple(out)

        zero = jnp.zeros((SB_T, 1), F32)
        init = tuple((jnp.zeros((SB_T, SB_HEAD_DIM), F32), zero, zero) for _ in range(SB_HB))
        carries = lax.fori_loop(0, top, lambda i, c: step(i, False, c), init)
        carries = step(top, True, carries)
        for hh in range(SB_HB):
            dq_ref[:, _head_lanes(hh)] = carries[hh][0]

        @pl.when(qi == nq - 1)
        def _():
            dvb_ref[...] = dv_ref[...].astype(BF16)

    wide = SB_HB * SB_HEAD_DIM
    qb = pl.BlockSpec((SB_T, wide), lambda h, i: (i, h))
    kv = pl.BlockSpec((s_dim, wide), lambda h, i: (0, h))
    outs, rode = _pcall(
        body, grid=(SB_N_HEADS // SB_HB, nq),
        in_specs=[qb, kv, kv, pl.BlockSpec((SB_T, wide), lambda h, i: (i, 3 * SB_N_HEADS // SB_HB + h)), qb,
                  pl.BlockSpec((SB_HB, SB_T, LANES), lambda h, i: (h, i, 0)), qb],
        out_specs=[qb, kv, kv, kv, qb],
        out_shape=[jax.ShapeDtypeStruct((s_dim, SB_WIDTH), F32), jax.ShapeDtypeStruct((s_dim, SB_WIDTH), F32),
                   jax.ShapeDtypeStruct((s_dim, SB_WIDTH), F32), jax.ShapeDtypeStruct((s_dim, SB_WIDTH), BF16),
                   jax.ShapeDtypeStruct((s_dim, SB_WIDTH), BF16)],
        args=[qn, kn, vb, proj, o, tot, dog], sem=("parallel", "arbitrary"), name=name, rider=rider)
    return (outs, rode) if rider is not None else outs


def _adamw_math(w, g, m, v):
    m = ADAM_B1 * m + (1.0 - ADAM_B1) * g
    v = ADAM_B2 * v + (1.0 - ADAM_B2) * (g * g)
    m_hat = m / (1.0 - ADAM_B1 ** ADAM_STEP)
    v_hat = v / (1.0 - ADAM_B2 ** ADAM_STEP)
    delta = -ADAM_LR * (m_hat / (jnp.sqrt(v_hat) + ADAM_EPS) + ADAM_WD * w)
    return delta, m, v


def _row_block(rows, cols, itemsize=4, limit=1 << 20):
    tr = rows
    while tr * cols * itemsize > limit and tr % (2 * BF16_ROWS) == 0:
        tr //= 2
    return tr


def _divisor_block(rows, cols, itemsize=4, limit=2 << 20):
    best = BF16_ROWS
    for t in range(BF16_ROWS, rows + 1, BF16_ROWS):
        if rows % t == 0 and t * cols * itemsize <= limit:
            best = t
    return best


def _adamw(w, g, m, v, name):
    n, rows, cols = w.shape
    tr = rows if rows * cols * 4 <= (1 << 20) else _divisor_block(rows, cols, limit=1 << 20)

    def body(w_ref, g_ref, m_ref, v_ref, d_out, m_out, v_out):
        d, m_new, v_new = _adamw_math(w_ref[...], g_ref[...], m_ref[...], v_ref[...])
        d_out[...] = d
        m_out[...] = m_new
        v_out[...] = v_new

    blk = pl.BlockSpec((None, tr, cols), lambda i, j: (i, j, 0))
    return pl.pallas_call(
        body, grid=(n, rows // tr), in_specs=[blk] * 4, out_specs=[blk] * 3,
        out_shape=[jax.ShapeDtypeStruct(w.shape, F32)] * 3,
        compiler_params=_params("parallel", "parallel"), name=name,
    )(w, g, m, v)


_FLIPS = ((1, 0), (0, 1), (1, 1))


def _place():
    return lax.axis_index("x"), lax.axis_index("y"), lax.axis_index("c")


def _flip(v, f):
    return 1 - v if f else v


def _half_rows(ref, lead, hc, hr):
    return ref.at[(*lead, pl.ds(pl.multiple_of(hc * hr, BF16_ROWS), hr), slice(None))]


def _half_cols(ref, lead, hc, hw):
    return ref.at[(*lead, pl.ds(pl.multiple_of(hc * hw, LANES), hw))]


def _rows_of_chip(chip, r):
    return pl.ds(pl.multiple_of(chip * r, BF16_ROWS), r)


def _slot_half(gathered, shard_shape, chip, l, hc):
    r, c = shard_shape[1:]
    if len(gathered.shape) == 3:
        return _half_cols(gathered, (l, _rows_of_chip(chip, r)), hc, c // 2)
    return _half_rows(gathered, (chip, l), hc, r // 2)


def _shard_half(shard, stacked, l, hc):
    r, c = shard.shape[1:]
    return _half_cols(shard, (l, slice(None)), hc, c // 2) if stacked else _half_rows(shard, (l,), hc, r // 2)


def _remote(src, dst, send, recv, k, to):
    return pltpu.make_async_remote_copy(src_ref=src, dst_ref=dst, send_sem=send.at[k], recv_sem=recv.at[k], device_id=to,
                                        device_id_type=MESH)


def _comm_call(reads, writes, n_sems, phases, name):
    passed = [k for k, w in enumerate(writes) if not isinstance(w, jax.ShapeDtypeStruct)]
    n_rd = len(reads)

    def body(*refs):
        rd = refs[:n_rd]
        wr = refs[n_rd + len(passed):n_rd + len(passed) + len(writes)]
        send, recv = refs[-2:]
        for phase in phases:
            sends, arrivals = phase(rd, wr, send, recv)
            for cp in sends:
                cp.start()
            for cp in arrivals:
                cp.wait_recv()
            for cp in sends:
                cp.wait_send()

    return pl.pallas_call(
        body, in_specs=[_ANY] * (n_rd + len(passed)), out_specs=[_ANY] * len(writes),
        out_shape=[jax.ShapeDtypeStruct(w.shape, w.dtype) for w in writes],
        input_output_aliases={n_rd + pos: k for pos, k in enumerate(passed)},
        scratch_shapes=[pltpu.SemaphoreType.DMA((n_sems,)), pltpu.SemaphoreType.DMA((n_sems,))], name=name,
    )(*reads, *[writes[k] for k in passed])


def _ag_ici(pieces, names, base=0):
    def phase(shards, gathered, send, recv):
        x, y, c = _place()
        me = 2 * x + y
        sends, arrivals = [], []
        for k, (n, l) in enumerate(pieces):
            a = names.index(n)
            shape = shards[a].shape
            src = _shard_half(shards[a], len(gathered[a].shape) == 3, l, c)
            for j, (fx, fy) in enumerate(_FLIPS):
                tx, ty = _flip(x, fx), _flip(y, fy)
                sends.append(_remote(src, _slot_half(gathered[a], shape, me, l, c), send, recv, base + 3 * k + j, (tx, ty, c)))
                arrivals.append(_remote(src, _slot_half(gathered[a], shape, 2 * tx + ty, l, c), send, recv, base + 3 * k + j, (tx, ty, c)))
        return sends, arrivals

    return phase


def _ag_pass_on(pieces, names, shapes, base=0):
    def phase(_, gathered, send, recv):
        x, y, c = _place()
        sibling = (x, y, 1 - c)
        sends, arrivals = [], []
        for k, (n, l) in enumerate(pieces):
            a = names.index(n)
            for j, (fx, fy) in enumerate(_FLIPS):
                chip = 2 * _flip(x, fx) + _flip(y, fy)
                landed = _slot_half(gathered[a], shapes[a], chip, l, c)
                sends.append(_remote(landed, landed, send, recv, base + 3 * k + j, sibling))
                arrivals.append(_remote(landed, _slot_half(gathered[a], shapes[a], chip, l, 1 - c), send, recv, base + 3 * k + j, sibling))
        return sends, arrivals

    return phase


def _other_half(ref, hc):
    if len(ref.shape) == 3:
        return _half_cols(ref, (slice(None), slice(None)), hc, ref.shape[2] // 2)
    return _half_rows(ref, (slice(None), slice(None)), hc, ref.shape[2] // 2)


def _half_shape(shape):
    return shape[:2] + (shape[2] // 2,) if len(shape) == 3 else shape[:2] + (shape[2] // 2, shape[3])


def _pair_exchange(grads, name):
    def phase(ins, outs, send, recv):
        x, y, c = _place()
        cps = [_remote(_other_half(ins[a], 1 - c), outs[a], send, recv, a, (x, y, 1 - c)) for a in range(len(grads))]
        return cps, cps

    outs = [jax.ShapeDtypeStruct(_half_shape(g.shape), g.dtype) for g in grads]
    return _comm_call(grads, outs, len(grads), [phase], name)


def _pair_sum_stacked(g, got, place, name):
    _, rows, hw = got.shape
    tr = _divisor_block(rows, hw)

    def body(place_ref, g_ref, r_ref, o_ref):
        o_ref[...] = (g_ref[...].astype(F32) + r_ref[...].astype(F32)).astype(o_ref.dtype)

    blk = pl.BlockSpec((None, tr, hw), lambda i, pr: (0, i, 0))
    return pl.pallas_call(
        body,
        grid_spec=pltpu.PrefetchScalarGridSpec(
            num_scalar_prefetch=1, grid=(rows // tr,),
            in_specs=[pl.BlockSpec((None, tr, hw), lambda i, pr: (0, i, pr[1])), blk], out_specs=blk),
        out_shape=jax.ShapeDtypeStruct(got.shape, BF16),
        compiler_params=_params("parallel"), name=name,
    )(place, g, got)


def _pair_sum(g, got, place, name):
    if len(g.shape) == 3:
        return _pair_sum_stacked(g, got, place, name)
    _, layers, hr, cols = got.shape
    tr = _row_block(hr, cols)
    per = hr // tr

    def body(place_ref, g_ref, r_ref, o_ref):
        o_ref[...] = (g_ref[...].astype(F32) + r_ref[...].astype(F32)).astype(o_ref.dtype)

    blk = pl.BlockSpec((None, None, tr, cols), lambda k, l, i, pr: (k, l, i, 0))
    return pl.pallas_call(
        body,
        grid_spec=pltpu.PrefetchScalarGridSpec(
            num_scalar_prefetch=1, grid=(4, layers, per),
            in_specs=[pl.BlockSpec((None, None, tr, cols), lambda k, l, i, pr: (k, l, pr[1] * per + i, 0)), blk],
            out_specs=blk),
        out_shape=jax.ShapeDtypeStruct(got.shape, BF16),
        compiler_params=_params("parallel", "parallel", "parallel"), name=name,
    )(place, g, got)


def _scatter_phase(n_arr):
    def phase(ins, outs, send, recv):
        x, y, c = _place()
        cps = []
        for a in range(n_arr):
            for j, (fx, fy) in enumerate(_FLIPS):
                tx, ty = _flip(x, fx), _flip(y, fy)
                if len(ins[a].shape) == 3:
                    src = ins[a].at[:, _rows_of_chip(2 * tx + ty, ins[a].shape[1] // 4), :]
                else:
                    src = ins[a].at[2 * tx + ty]
                cps.append(_remote(src, outs[a].at[j], send, recv, 3 * a + j, (tx, ty, c)))
        return cps, cps

    return phase


def _scatter_outs(pairs):
    return [jax.ShapeDtypeStruct((3, 1, p.shape[1] // 4, p.shape[2]) if len(p.shape) == 3 else (3,) + p.shape[1:], p.dtype) for p in pairs]


def _chip_scatter(pairs, name):
    return _comm_call(pairs, _scatter_outs(pairs), 3 * len(pairs), [_scatter_phase(len(pairs))], name)


def _scatter_rider(pairs):
    return _Rider(pairs, _scatter_outs(pairs), 3 * len(pairs), _scatter_phase(len(pairs)))


def _chip_sum_stacked(p, got, place, layer, layers, o_buf, name):
    _, r, hw = got.shape[1:]
    tr = _divisor_block(r, hw)
    per = r // tr

    def body(place_ref, p_ref, r_ref, *rest):
        o_ref = rest[-1]
        acc = p_ref[...].astype(F32)
        for j in range(3):
            acc = acc + r_ref[j].astype(F32)
        o_ref[...] = acc

    has_buf = o_buf is not None
    return pl.pallas_call(
        body,
        grid_spec=pltpu.PrefetchScalarGridSpec(
            num_scalar_prefetch=1, grid=(per,),
            in_specs=[pl.BlockSpec((None, tr, hw), lambda i, pr: (0, pr[0] * per + i, 0)),
                      pl.BlockSpec((3, None, tr, hw), lambda i, pr: (0, 0, i, 0))] + ([_ANY] if has_buf else []),
            out_specs=pl.BlockSpec((None, tr, hw), lambda i, pr: (layer, i, pr[1]))),
        out_shape=jax.ShapeDtypeStruct((layers, r, 2 * hw), F32),
        input_output_aliases={3: 0} if has_buf else {},
        compiler_params=_params("parallel"), name=name,
    )(*((place, p, got) + ((o_buf,) if has_buf else ())))


def _chip_sum(p, got, place, layer, layers, o_buf, name):
    if len(p.shape) == 3:
        return _chip_sum_stacked(p, got, place, layer, layers, o_buf, name)
    _, _, hr, cols = p.shape
    tr = _row_block(hr, cols)
    per = hr // tr

    def body(place_ref, p_ref, r_ref, *rest):
        o_ref = rest[-1]
        acc = p_ref[...].astype(F32)
        for j in range(3):
            acc = acc + r_ref[j].astype(F32)
        o_ref[...] = acc

    has_buf = o_buf is not None
    return pl.pallas_call(
        body,
        grid_spec=pltpu.PrefetchScalarGridSpec(
            num_scalar_prefetch=1, grid=(per,),
            in_specs=[pl.BlockSpec((None, None, tr, cols), lambda i, pr: (pr[0], 0, i, 0)),
                      pl.BlockSpec((3, None, tr, cols), lambda i, pr: (0, 0, i, 0))] + ([_ANY] if has_buf else []),
            out_specs=pl.BlockSpec((None, tr, cols), lambda i, pr: (layer, pr[1] * per + i, 0))),
        out_shape=jax.ShapeDtypeStruct((layers, 2 * hr, cols), F32),
        input_output_aliases={3: 0} if has_buf else {},
        compiler_params=_params("parallel"), name=name,
    )(*((place, p, got) + ((o_buf,) if has_buf else ())))


def _pair_gather(halves, by_cols):
    def phase(_, bufs, send, recv):
        x, y, c = _place()
        sends, arrivals = [], []
        for a, h in enumerate(halves):
            cut = (lambda hc, a=a, h=h: _half_cols(bufs[a], (slice(None), slice(None)), hc, h.shape[2] // 2)) if by_cols[a] else (
                lambda hc, a=a, h=h: _half_rows(bufs[a], (slice(None),), hc, h.shape[1] // 2))
            sends.append(_remote(cut(c), cut(c), send, recv, a, (x, y, 1 - c)))
            arrivals.append(_remote(cut(c), cut(1 - c), send, recv, a, (x, y, 1 - c)))
        return sends, arrivals

    return _comm_call([], halves, len(halves), [phase], "rs_pair_gather")


def _allreduce_small(v, name):
    rows, cols = v.shape

    def body(v_ref, o_ref, buf, send_sems, recv_sems):
        x, y, c = _place()
        me = 4 * x + 2 * y + c
        buf[0] = v_ref[...]
        cps = []
        for k in range(1, 8):
            kx, ky, kc = (k >> 2) & 1, (k >> 1) & 1, k & 1
            cp = pltpu.make_async_remote_copy(src_ref=v_ref, dst_ref=buf.at[k], send_sem=send_sems.at[k - 1], recv_sem=recv_sems.at[k - 1],
                                              device_id=(_flip(x, kx), _flip(y, ky), _flip(c, kc)), device_id_type=MESH)
            cp.start()
            cps.append(cp)
        for cp in cps:
            cp.wait()
        acc = buf[me]
        for d in range(1, 8):
            acc = acc + buf[jnp.bitwise_xor(d, me)]
        o_ref[...] = acc

    vm = pl.BlockSpec(memory_space=pltpu.VMEM)
    return pl.pallas_call(
        body, in_specs=[vm], out_specs=vm, out_shape=jax.ShapeDtypeStruct((rows, cols), F32),
        scratch_shapes=[pltpu.VMEM((8, rows, cols), F32), pltpu.SemaphoreType.DMA((7,)), pltpu.SemaphoreType.DMA((7,))],
        name=name,
    )(v)


def _pad_lanes(a):
    return jnp.pad(a, ((0, 0), (0, LANES - a.shape[1])))


def _group_lanes(v):
    return jnp.pad(v.reshape(SSD_N_GROUPS, 1, 8), ((0, 0), (0, 0), (0, LANES - 8)))


def kernel(x, p, norm_w, ssd_in_w, ssd_conv_w, ssd_conv_b, ssd_dt_bias, ssd_a_log, ssd_d, ssd_gnorm_w, ssd_out_w, sb_in_w, sb_qn_w, sb_kn_w, sb_out_w, ple_norm_w, ple_gate_w, ple_proj_w, loss_target, m_norm_w, m_ssd_in_w, m_ssd_conv_w, m_ssd_conv_b, m_ssd_dt_bias, m_ssd_a_log, m_ssd_d, m_ssd_gnorm_w, m_ssd_out_w, m_sb_in_w, m_sb_qn_w, m_sb_kn_w, m_sb_out_w, m_ple_norm_w, m_ple_gate_w, m_ple_proj_w, v_norm_w, v_ssd_in_w, v_ssd_conv_w, v_ssd_conv_b, v_ssd_dt_bias, v_ssd_a_log, v_ssd_d, v_ssd_gnorm_w, v_ssd_out_w, v_sb_in_w, v_sb_qn_w, v_sb_kn_w, v_sb_out_w, v_ple_norm_w, v_ple_gate_w, v_ple_proj_w):
    w_in = dict(norm_w=norm_w, ssd_in_w=ssd_in_w, ssd_conv_w=ssd_conv_w, ssd_conv_b=ssd_conv_b, ssd_dt_bias=ssd_dt_bias,
                ssd_a_log=ssd_a_log, ssd_d=ssd_d, ssd_gnorm_w=ssd_gnorm_w, ssd_out_w=ssd_out_w, sb_in_w=sb_in_w, sb_qn_w=sb_qn_w,
                sb_kn_w=sb_kn_w, sb_out_w=sb_out_w, ple_norm_w=ple_norm_w, ple_gate_w=ple_gate_w, ple_proj_w=ple_proj_w)
    m_in = dict(norm_w=m_norm_w, ssd_in_w=m_ssd_in_w, ssd_conv_w=m_ssd_conv_w, ssd_conv_b=m_ssd_conv_b, ssd_dt_bias=m_ssd_dt_bias,
                ssd_a_log=m_ssd_a_log, ssd_d=m_ssd_d, ssd_gnorm_w=m_ssd_gnorm_w, ssd_out_w=m_ssd_out_w, sb_in_w=m_sb_in_w,
                sb_qn_w=m_sb_qn_w, sb_kn_w=m_sb_kn_w, sb_out_w=m_sb_out_w, ple_norm_w=m_ple_norm_w, ple_gate_w=m_ple_gate_w,
                ple_proj_w=m_ple_proj_w)
    v_in = dict(norm_w=v_norm_w, ssd_in_w=v_ssd_in_w, ssd_conv_w=v_ssd_conv_w, ssd_conv_b=v_ssd_conv_b, ssd_dt_bias=v_ssd_dt_bias,
                ssd_a_log=v_ssd_a_log, ssd_d=v_ssd_d, ssd_gnorm_w=v_ssd_gnorm_w, ssd_out_w=v_ssd_out_w, sb_in_w=v_sb_in_w,
                sb_qn_w=v_sb_qn_w, sb_kn_w=v_sb_kn_w, sb_out_w=v_sb_out_w, ple_norm_w=v_ple_norm_w, ple_gate_w=v_ple_gate_w,
                ple_proj_w=v_ple_proj_w)
    ix, iy, ic = lax.axis_index("x"), lax.axis_index("y"), lax.axis_index("c")
    chip = (2 * ix + iy).astype(jnp.int32)
    place = jnp.stack([chip, ic.astype(jnp.int32)])
    zero = jnp.zeros((), jnp.int32)
    big_names = [n for n, _, _ in _BIG]
    layers_of = {n: s[0] for n, s, _ in _BIG}
    cut_of = {n: cut for n, _, cut in _BIG}

    def layer_pieces(i):
        mixer = ("ssd_in_w", "ssd_out_w") if i % 2 == 0 else ("sb_in_w", "sb_out_w")
        return [(mixer[0], i // 2), (mixer[1], i // 2), ("ple_gate_w", i), ("ple_proj_w", i)]

    def names_of(pieces):
        return [n for n in big_names if any(n == q for q, _ in pieces)]

    held = lambda n, a: a.transpose(0, 2, 1) if cut_of[n] == "stack" else a
    mine = {n: held(n, w_in[n]).astype(BF16) for n in big_names}
    shard_shapes = [mine[n].shape for n in big_names]
    room = [jax.ShapeDtypeStruct((s[0], 4 * s[1], s[2]) if cut_of[n] == "stack" else (4,) + s, BF16) for n, s in zip(big_names, shard_shapes)]
    first = layer_pieces(0)
    gathered = _comm_call([mine[n] for n in big_names], room, 6 * len(first),
                          [_ag_ici(first, big_names), _ag_pass_on(first, big_names, shard_shapes, base=3 * len(first))], "allgather_layer0")
    gw = {}
    for n, g in zip(big_names, gathered):
        if cut_of[n] == "stack":
            gw[n] = lax.dynamic_update_slice(g, mine[n], (zero, chip * mine[n].shape[1], zero))
        else:
            gw[n] = lax.dynamic_update_slice(g, mine[n][None], (chip, zero, zero, zero))

    def gather_rider(pieces):
        names = names_of(pieces)
        return names, _Rider([mine[n] for n in names], [gw[n] for n in names], 3 * len(pieces), _ag_ici(pieces, names))

    def landed(names, bufs):
        for n, g in zip(names, bufs):
            gw[n] = g

    def pass_on(pieces, call):
        names = names_of(pieces)
        landed(names, _comm_call([], [gw[n] for n in names], 3 * len(pieces), [_ag_pass_on(pieces, names, [mine[n].shape for n in names])], call))

    onehot = (jnp.arange(4) == chip).astype(F32) * (ic == 0).astype(F32)
    cw_mine = onehot[:, None, None, None] * ssd_conv_w[None]
    cw_full = _allreduce_small(cw_mine.transpose(1, 2, 0, 3).reshape(-1, LANES), "gather_conv_w").reshape(2, SSD_D_CONV, SSD_CONV_DIM)

    def wmm(a, name, layer, *, dn="nn", res=None, call, rider=None):
        return _matmul(a, gw[name], dn=dn, res=res, b_lay=(cut_of[name], layer), name=call, rider=rider)

    h = x[0]
    target = loss_target[0]
    saved = []
    for i in range(DEPTH):
        j = i // 2
        nw = norm_w[i:i + 1]
        pw = ple_norm_w[i:i + 1]
        nxt = layer_pieces(i + 1) if i + 1 < DEPTH else None
        s = dict(h=h)
        u = _rms_fwd(h, nw, f"rms_{i}")
        s["u"] = u
        if i % 2 == 0:
            w_dt = jnp.pad(gw["ssd_in_w"][j, SSD_ZX:], ((0, LANES - SSD_N_HEADS), (0, 0)))
            if nxt:
                names, rider = gather_rider(nxt[1:])
                pzx, rode = _matmul(u, gw["ssd_in_w"], dn="nt", b_lay=("stack", j, SSD_ZX), name=f"ssd_in_{i}", rider=rider)
                landed(names, rode)
            else:
                pzx = _matmul(u, gw["ssd_in_w"], dn="nt", b_lay=("stack", j, SSD_ZX), name=f"ssd_in_{i}")
            pdt = _matmul(u, w_dt, dn="nt", name=f"ssd_indt_{i}")
            act = _conv_fwd(pzx, cw_full[j], ssd_conv_b[j:j + 1], f"conv_{i}")
            dtg = jnp.pad(pdt[:, :SSD_N_HEADS].reshape(-1, SSD_N_GROUPS, 8).transpose(1, 0, 2), ((0, 0), (0, 0), (0, LANES - 8)))
            vecs = (_group_lanes(ssd_dt_bias[j]), _group_lanes(ssd_a_log[j]), _group_lanes(ssd_d[j]))
            if nxt:
                names, rider = gather_rider(nxt[:1])
                (yn, states), rode = _ssd_fwd(act, dtg, *vecs, pzx, ssd_gnorm_w[j:j + 1], f"ssd_{i}", rider=rider)
                landed(names, rode)
                pass_on(nxt, f"allgather_pass_{i + 1}")
            else:
                yn, states = _ssd_fwd(act, dtg, *vecs, pzx, ssd_gnorm_w[j:j + 1], f"ssd_{i}")
            s.update(w_dt=w_dt, pzx=pzx, act=act, dtg=dtg, vecs=vecs, yn=yn, states=states)
            h1 = wmm(yn, "ssd_out_w", j, res=h, call=f"ssd_out_{i}")
        else:
            if nxt:
                names, rider = gather_rider(nxt[2:])
                proj, rode = wmm(u, "sb_in_w", j, call=f"sb_in_{i}", rider=rider)
                landed(names, rode)
            else:
                proj = wmm(u, "sb_in_w", j, call=f"sb_in_{i}")
            qn, kn, vb = _qknorm_fwd(proj, sb_qn_w[j:j + 1], sb_kn_w[j:j + 1], f"qknorm_{i}")
            if nxt:
                names, rider = gather_rider(nxt[:2])
                (og, o, tot), rode = _sb_fwd(qn, kn, vb, proj, f"sb_{i}", rider=rider)
                landed(names, rode)
                pass_on(nxt, f"allgather_pass_{i + 1}")
            else:
                og, o, tot = _sb_fwd(qn, kn, vb, proj, f"sb_{i}")
            s.update(proj=proj, qn=qn, kn=kn, vb=vb, og=og, o=o, tot=tot)
            h1 = wmm(og, "sb_out_w", j, res=h, call=f"sb_out_{i}")
        n2 = _rms_fwd(h1, pw, f"ple_rms_{i}")
        gl = wmm(n2, "ple_gate_w", i, call=f"ple_gate_{i}")
        pp = wmm(p[i, 0], "ple_proj_w", i, call=f"ple_proj_{i}")
        h = _ple_fwd(h1, pp, gl, f"ple_{i}")
        s.update(h1=h1, n2=n2, gl=gl, pp=pp)
        saved.append(s)

    dh, loss_lanes = _loss_bwd(h, target, "loss")

    wg = {}
    gsmall = {n: [None] * s[0] for n, s in _SMALL}
    g_conv_w = [None, None]
    scattered = {}
    pending = None

    def wgrad(a, b, name, layer, call):
        wg[(name, layer)] = _matmul(a, b, dn="tn", out_dtype=BF16, o_lay=(cut_of[name], 0, 1), name=call)

    for i in reversed(range(DEPTH)):
        j = i // 2
        s = saved[i]
        nw = norm_w[i:i + 1]
        pw = ple_norm_w[i:i + 1]
        rider = _scatter_rider(pending[1]) if pending else None
        dpp, dgl = _ple_bwd(dh, s["pp"], s["gl"], f"ple_bwd_{i}")
        wgrad(p[i, 0], dpp, "ple_proj_w", i, f"d_ple_proj_{i}")
        wgrad(s["n2"], dgl, "ple_gate_w", i, f"d_ple_gate_{i}")
        dn2 = wmm(dgl, "ple_gate_w", i, dn="nt", call=f"ple_gate_bwd_{i}")
        dh1, dpw = _rms_bwd(s["h1"], pw, dn2, dh, f"ple_rms_bwd_{i}")
        gsmall["ple_norm_w"][i] = dpw
        if i % 2 == 0:
            wgrad(s["yn"], dh1, "ssd_out_w", j, f"d_ssd_out_{i}")
            dyn = wmm(dh1, "ssd_out_w", j, dn="nt", call=f"ssd_out_bwd_{i}")
            outs = _ssd_bwd(s["act"], s["dtg"], *s["vecs"], s["pzx"], ssd_gnorm_w[j:j + 1], s["states"], dyn, f"ssd_bwd_{i}", rider=rider)
            if rider:
                outs, got = outs
                scattered[pending[0]] = (pending[1], got)
            dxs, dbm, dcm, ddtg, dbias, dalog, ddsk, dz, dgw = outs
            dact = jnp.concatenate([dxs, dbm, dcm], axis=1)
            dzx, dcw, dcb = _conv_bwd(s["pzx"], cw_full[j], ssd_conv_b[j:j + 1], dact, dz, f"conv_bwd_{i}")
            ddt = _pad_lanes(ddtg[:, :, :8].transpose(1, 0, 2).reshape(-1, SSD_N_HEADS)).astype(BF16)
            du = _matmul(dzx, gw["ssd_in_w"], b_lay=("stack", j, SSD_ZX), name=f"ssd_in_bwd_{i}")
            du = _matmul(ddt, s["w_dt"], res=du, name=f"ssd_indt_bwd_{i}")
            dwt = _matmul(dzx, s["u"], dn="tn", out_dtype=BF16, out_rows=SSD_IN_DIM, name=f"d_ssd_in_{i}")
            dwt_dt = _matmul(ddt, s["u"], dn="tn", out_dtype=BF16, name=f"d_ssd_indt_{i}")
            wg[("ssd_in_w", j)] = lax.dynamic_update_slice(dwt, dwt_dt[:SSD_N_HEADS], (SSD_ZX, 0))[None]
            g_conv_w[j] = dcw
            gsmall["ssd_conv_b"][j] = dcb
            gsmall["ssd_dt_bias"][j] = dbias[:, 0, :8].reshape(1, SSD_N_HEADS)
            gsmall["ssd_a_log"][j] = dalog[:, 0, :8].reshape(1, SSD_N_HEADS)
            gsmall["ssd_d"][j] = ddsk[:, 0, :8].reshape(1, SSD_N_HEADS)
            gsmall["ssd_gnorm_w"][j] = dgw
        else:
            wgrad(s["og"], dh1, "sb_out_w", j, f"d_sb_out_{i}")
            dog = wmm(dh1, "sb_out_w", j, dn="nt", call=f"sb_out_bwd_{i}")
            outs = _sb_bwd(s["qn"], s["kn"], s["vb"], s["proj"], s["o"], s["tot"], dog, f"sb_bwd_{i}", rider=rider)
            if rider:
                outs, got = outs
                scattered[pending[0]] = (pending[1], got)
            dqn, dkn, _, dvb, dg = outs
            dq, dk, dqw, dkw = _qknorm_bwd(s["proj"], sb_qn_w[j:j + 1], sb_kn_w[j:j + 1], dqn, dkn, f"qknorm_bwd_{i}")
            dproj = jnp.concatenate([dq, dk, dvb, dg], axis=1)
            du = wmm(dproj, "sb_in_w", j, dn="nt", call=f"sb_in_bwd_{i}")
            wgrad(s["u"], dproj, "sb_in_w", j, f"d_sb_in_{i}")
            gsmall["sb_qn_w"][j] = dqw
            gsmall["sb_kn_w"][j] = dkw
        dh, dnw = _rms_bwd(s["h"], nw, du, dh1, f"rms_bwd_{i}")
        gsmall["norm_w"][i] = dnw
        g_list = [wg[q] for q in layer_pieces(i)]
        pending = (i, [_pair_sum(g, r, place, f"rs_pair_sum_{i}_{k}")
                       for k, (g, r) in enumerate(zip(g_list, _pair_exchange(g_list, f"rs_pair_exchange_{i}")))])
    grad_x = dh[None]
    scattered[0] = (pending[1], _chip_scatter(pending[1], "rs_chip_scatter_0"))

    halves = []
    for n in big_names:
        buf = None
        for l in range(layers_of[n]):
            i = l if n.startswith("ple") else 2 * l + (0 if n.startswith("ssd") else 1)
            k = layer_pieces(i).index((n, l))
            buf = _chip_sum(scattered[i][0][k], scattered[i][1][k], place, l, layers_of[n], buf, f"rs_chip_sum_{n}_{l}")
        halves.append(buf)
    g_big = dict(zip(big_names, _pair_gather(halves, [cut_of[n] == "stack" for n in big_names])))

    small_parts = [jnp.concatenate(gsmall[n], axis=0).reshape(-1) for n, _ in _SMALL]
    small_parts.append(jnp.stack(g_conv_w).reshape(-1))
    small_parts.append(loss_lanes.reshape(-1))
    small_sum = _allreduce_small(jnp.concatenate(small_parts).reshape(-1, LANES), "allreduce_small").reshape(-1)
    g_small, off = {}, 0
    for n, shape in _SMALL:
        size = math.prod(shape)
        g_small[n] = small_sum[off:off + size].reshape(shape)
        off += size
    cw_size = 2 * SSD_D_CONV * SSD_CONV_DIM
    g_cw_full = small_sum[off:off + cw_size].reshape(2, SSD_D_CONV, 4, SSD_CONV_DIM // 4)
    g_small["ssd_conv_w"] = jnp.sum(g_cw_full * (jnp.arange(4) == chip).astype(F32)[None, None, :, None], axis=2)
    loss = 0.5 * jnp.sum(small_sum[off + cw_size:]) / D_MODEL

    grads, delta, new_m, new_v = {}, {}, {}, {}
    for n in big_names:
        grads[n], delta[n], new_m[n], new_v[n] = (
            held(n, a) for a in (g_big[n], *_adamw(held(n, w_in[n]), g_big[n], held(n, m_in[n]), held(n, v_in[n]), f"adamw_{n}")))
    small_names = [n for n, _ in _SMALL] + ["ssd_conv_w"]
    pack = lambda d: jnp.concatenate([d[n].reshape(-1) for n in small_names]).reshape(1, -1, LANES)
    ds, ms, vs = _adamw(pack(w_in), pack(g_small), pack(m_in), pack(v_in), "adamw_small")
    off = 0
    for n in small_names:
        shape = w_in[n].shape
        size = math.prod(shape)
        grads[n] = g_small[n]
        delta[n] = ds.reshape(-1)[off:off + size].reshape(shape)
        new_m[n] = ms.reshape(-1)[off:off + size].reshape(shape)
        new_v[n] = vs.reshape(-1)[off:off + size].reshape(shape)
        off += size

    order = ["norm_w", "ssd_in_w", "ssd_conv_w", "ssd_conv_b", "ssd_dt_bias", "ssd_a_log", "ssd_d", "ssd_gnorm_w", "ssd_out_w",
             "sb_in_w", "sb_qn_w", "sb_kn_w", "sb_out_w", "ple_norm_w", "ple_gate_w", "ple_proj_w"]
    return (loss, grad_x, *[grads[n] for n in order], *[delta[n] for n in order], *[new_m[n] for n in order],
            *[new_v[n] for n in order])
```

```python
import functools
import math

import jax
import jax.numpy as jnp
from jax import lax
from jax.experimental import pallas as pl
from jax.experimental.pallas import tpu as pltpu

F32 = jnp.float32
BF16 = jnp.bfloat16
MESH = pl.DeviceIdType.MESH

D_MODEL = 2048
DEPTH = 4
SSD_D_INNER = 4096
SSD_N_GROUPS = 8
SSD_GROUP_W = SSD_D_INNER // SSD_N_GROUPS
SSD_D_STATE = 128
SSD_CHUNK = 128
SSD_CONV_DIM = 6144
SSD_D_CONV = 4
SSD_N_HEADS = 64
SB_HEAD_DIM = 128
SB_N_HEADS = 16
SB_WIDTH = 2048
NORM_EPS = 1e-6
GATED_NORM_EPS = 1e-5
ADAM_LR = 0.001
ADAM_B1 = 0.9
ADAM_B2 = 0.999
ADAM_EPS = 1e-08
ADAM_WD = 0.01
ADAM_STEP = 10

SSD_ZX = SSD_D_INNER + SSD_CONV_DIM
SSD_IN_DIM = SSD_ZX + SSD_N_HEADS
LANES = 128
BF16_ROWS = 16

_BIG = (
    ("ssd_in_w", (2, 2576, 2048), "stack"),
    ("ssd_out_w", (2, 1024, 2048), "row"),
    ("sb_in_w", (2, 2048, 2048), "col"),
    ("sb_out_w", (2, 512, 2048), "row"),
    ("ple_gate_w", (4, 512, 2048), "row"),
    ("ple_proj_w", (4, 256, 512), "col"),
)
_SMALL = (
    ("norm_w", (4, 2048)),
    ("ssd_conv_b", (2, 6144)),
    ("ssd_dt_bias", (2, 64)),
    ("ssd_a_log", (2, 64)),
    ("ssd_d", (2, 64)),
    ("ssd_gnorm_w", (2, 4096)),
    ("sb_qn_w", (2, 128)),
    ("sb_kn_w", (2, 128)),
    ("ple_norm_w", (4, 2048)),
)

_DN = {
    "nn": (((1,), (0,)), ((), ())),
    "nt": (((1,), (1,)), ((), ())),
    "tn": (((0,), (0,)), ((), ())),
}


def _dot(a, b, dn="nn"):
    return lax.dot_general(a.astype(BF16), b.astype(BF16), _DN[dn], preferred_element_type=F32)


@functools.partial(jax.custom_vjp, nondiff_argnums=(2,))
def _gdot(a, b, dn):
    return _dot(a, b, dn)


def _gdot_fwd(a, b, dn):
    return _dot(a, b, dn), (a, b)


def _gdot_bwd(dn, res, g):
    a, b = res
    if dn == "nn":
        return _dot(g, b, "nt"), _dot(a, g, "tn")
    if dn == "nt":
        return _dot(g, b, "nn"), _dot(g, a, "tn")
    return _dot(b, g, "nt"), _dot(a, g, "nn")


_gdot.defvjp(_gdot_fwd, _gdot_bwd)


def _split_dot(x, t, parts, x_left):
    acc = None
    r = x
    for i in range(parts):
        p = r.astype(BF16)
        d = lax.dot_general(p, t, _DN["nn"], preferred_element_type=F32) if x_left else lax.dot_general(
            t, p, _DN["nn"], preferred_element_type=F32)
        acc = d if acc is None else acc + d
        if i + 1 < parts:
            r = r - p.astype(F32)
    return acc


def _tri(n, lower, strict=False):
    r = lax.broadcasted_iota(jnp.int32, (n, n), 0)
    c = lax.broadcasted_iota(jnp.int32, (n, n), 1)
    keep = (r > c if strict else r >= c) if lower else (r < c if strict else r <= c)
    return jnp.where(keep, 1.0, 0.0).astype(BF16)


def _cumsum_rows_raw(x):
    return _split_dot(x, _tri(x.shape[0], True), 3, False)


@jax.custom_vjp
def _cumsum_rows(x):
    return _cumsum_rows_raw(x)


def _cumsum_rows_fwd(x):
    return _cumsum_rows_raw(x), None


def _cumsum_rows_bwd(_, g):
    return (_split_dot(g, _tri(g.shape[0], False), 3, False),)


_cumsum_rows.defvjp(_cumsum_rows_fwd, _cumsum_rows_bwd)


def _sigmoid(x):
    return 1.0 / (1.0 + jnp.exp(-x))


def _softplus(x):
    return jnp.maximum(x, 0.0) + jnp.log(1.0 + jnp.exp(-jnp.abs(x)))


def _rms(x, w, eps):
    return x * lax.rsqrt(jnp.mean(x * x, axis=-1, keepdims=True) + eps) * w


_ANY = pl.BlockSpec(memory_space=pl.ANY)


def _params(*sem):
    return pltpu.CompilerParams(dimension_semantics=sem)


class _Rider:
    def __init__(self, reads, writes, n_sems, issue):
        self.reads, self.writes, self.n_sems, self.issue = list(reads), list(writes), n_sems, issue


def _pcall(body, *, grid, in_specs, out_specs, out_shape, args, sem, name, scratch_shapes=(), aliases=None, rider=None):
    aliases = dict(aliases or {})
    if rider is None:
        outs = pl.pallas_call(body, grid=grid, in_specs=in_specs, out_specs=out_specs, out_shape=out_shape,
                              scratch_shapes=list(scratch_shapes), input_output_aliases=aliases,
                              compiler_params=_params(*sem), name=name)(*args)
        return list(outs), []
    n_in, n_out, n_scr, n_rd, n_wr = len(args), len(out_shape), len(scratch_shapes), len(rider.reads), len(rider.writes)
    passed = [k for k, w in enumerate(rider.writes) if not isinstance(w, jax.ShapeDtypeStruct)]
    for pos, k in enumerate(passed):
        aliases[n_in + n_rd + pos] = n_out + k

    def wrapped(*refs):
        ins = refs[:n_in]
        reads = refs[n_in:n_in + n_rd]
        base = n_in + n_rd + len(passed)
        outs = refs[base:base + n_out]
        writes = refs[base + n_out:base + n_out + n_wr]
        scr = refs[base + n_out + n_wr:base + n_out + n_wr + n_scr]
        send, recv = refs[-2:]
        first = last = None
        for d, n in enumerate(grid):
            i = pl.program_id(d)
            first = (i == 0) if first is None else first & (i == 0)
            last = (i == n - 1) if last is None else last & (i == n - 1)

        @pl.when(first)
        def _():
            for cp in rider.issue(reads, writes, send, recv)[0]:
                cp.start()

        body(*ins, *outs, *scr)

        @pl.when(last)
        def _():
            sends, arrivals = rider.issue(reads, writes, send, recv)
            for cp in arrivals:
                cp.wait_recv()
            for cp in sends:
                cp.wait_send()

    outs = pl.pallas_call(
        wrapped, grid=grid,
        in_specs=list(in_specs) + [_ANY] * (n_rd + len(passed)),
        out_specs=list(out_specs) + [_ANY] * n_wr,
        out_shape=list(out_shape) + [jax.ShapeDtypeStruct(w.shape, w.dtype) for w in rider.writes],
        scratch_shapes=list(scratch_shapes) + [pltpu.SemaphoreType.DMA((rider.n_sems,)), pltpu.SemaphoreType.DMA((rider.n_sems,))],
        input_output_aliases=aliases, compiler_params=_params(*(["arbitrary"] * len(grid))), name=name,
    )(*args, *rider.reads, *[rider.writes[k] for k in passed])
    return list(outs[:n_out]), list(outs[n_out:])


MM_TK = 2048


def _pick(dim, pref, unit=None):
    t = pref
    while t >= LANES:
        if dim % t == 0 and (unit is None or unit % t == 0):
            return t
        t //= 2
    return dim


def _matmul(a, b, *, dn="nn", res=None, out_dtype=F32, name, b_lay=None, o_lay=None, o_buf=None, out_rows=None, rider=None):
    if dn == "tn":
        k_dim, m_dim = a.shape
    else:
        m_dim, k_dim = a.shape
    unit_m = unit_n = unit_k = None
    if b_lay is None:
        n_dim = b.shape[0] if dn == "nt" else b.shape[1]
    elif b_lay[0] == "stack":
        cut, layer, rows = b_lay
        cols = b.shape[2]
        n_dim = cols if dn == "nn" else rows
        assert k_dim == (rows if dn == "nn" else cols) and dn != "tn"
    else:
        cut, layer = b_lay
        r, c = b.shape[2:]
        rows, cols = (4 * r, c) if cut == "row" else (r, 4 * c)
        n_dim = cols if dn == "nn" else rows
        assert k_dim == (rows if dn == "nn" else cols) and dn != "tn"
        if (cut == "row") == (dn == "nn"):
            unit_k = r if cut == "row" else c
        else:
            unit_n = r if cut == "row" else c
    if o_lay is not None:
        o_cut, o_layer, o_layers = o_lay
        if o_cut == "row":
            unit_m = m_dim // 4
        else:
            unit_n = n_dim // 4
    tm, tn, tk = _pick(m_dim, 1024, unit_m), _pick(n_dim, 1024, unit_n), _pick(k_dim, MM_TK, unit_k)
    nk = k_dim // tk
    a_spec = pl.BlockSpec((tk, tm), lambda i, j, k: (k, i)) if dn == "tn" else pl.BlockSpec((tm, tk), lambda i, j, k: (i, k))
    if b_lay is None:
        b_spec = pl.BlockSpec((tn, tk), lambda i, j, k: (j, k)) if dn == "nt" else pl.BlockSpec((tk, tn), lambda i, j, k: (k, j))
    elif cut == "stack":
        b_spec = (pl.BlockSpec((None, tk, tn), lambda i, j, k: (layer, k, j)) if dn == "nn" else
                  pl.BlockSpec((None, tn, tk), lambda i, j, k: (layer, j, k)))
    elif dn == "nn" and cut == "row":
        per = r // tk
        b_spec = pl.BlockSpec((None, None, tk, tn), lambda i, j, k: (k // per, layer, k % per, j))
    elif dn == "nn":
        per = c // tn
        b_spec = pl.BlockSpec((None, None, tk, tn), lambda i, j, k: (j // per, layer, k, j % per))
    elif cut == "row":
        per = r // tn
        b_spec = pl.BlockSpec((None, None, tn, tk), lambda i, j, k: (j // per, layer, j % per, k))
    else:
        per = c // tk
        b_spec = pl.BlockSpec((None, None, tn, tk), lambda i, j, k: (k // per, layer, j, k % per))
    r_spec = pl.BlockSpec((tm, tn), lambda i, j, k: (i, j))
    if o_lay is None:
        o_spec = r_spec
        out_shape = jax.ShapeDtypeStruct((out_rows or m_dim, n_dim), out_dtype)
    elif o_cut == "row":
        per_o = unit_m // tm
        o_spec = pl.BlockSpec((None, None, tm, tn), lambda i, j, k: (i // per_o, o_layer, i % per_o, j))
        out_shape = jax.ShapeDtypeStruct((4, o_layers, unit_m, n_dim), out_dtype)
    else:
        per_o = unit_n // tn
        o_spec = pl.BlockSpec((None, None, tm, tn), lambda i, j, k: (j // per_o, o_layer, i, j % per_o))
        out_shape = jax.ShapeDtypeStruct((4, o_layers, m_dim, unit_n), out_dtype)
    has_res = res is not None
    has_buf = o_buf is not None

    def body(*refs):
        a_ref, b_ref = refs[:2]
        r_ref = refs[2] if has_res else None
        o_ref = refs[-1] if nk == 1 else refs[-2]

        def finish(v):
            if has_res:
                v = v + r_ref[...]
            o_ref[...] = v.astype(o_ref.dtype)

        if nk == 1:
            finish(_dot(a_ref[...], b_ref[...], dn))
            return
        acc_ref = refs[-1]
        k = pl.program_id(2)

        @pl.when(k == 0)
        def _():
            acc_ref[...] = jnp.zeros_like(acc_ref)

        acc_ref[...] += _dot(a_ref[...], b_ref[...], dn)

        @pl.when(k == nk - 1)
        def _():
            finish(acc_ref[...])

    args = [a, b] + ([res] if has_res else []) + ([o_buf] if has_buf else [])
    outs, rode = _pcall(
        body, grid=(m_dim // tm, n_dim // tn, nk),
        in_specs=[a_spec, b_spec] + ([r_spec] if has_res else []) + ([_ANY] if has_buf else []),
        out_specs=[o_spec], out_shape=[out_shape],
        scratch_shapes=[] if nk == 1 else [pltpu.VMEM((tm, tn), F32)],
        aliases={len(args) - 1: 0} if has_buf else {},
        args=args, sem=("parallel", "parallel", "arbitrary"), name=name, rider=rider)
    return (outs[0], rode) if rider is not None else outs[0]


def _rowcall(fn, rows, consts, outs, accs, *, name, tm=256):
    args = list(rows) + list(consts)
    in_specs = [pl.BlockSpec((tm, r.shape[1]), lambda i: (i, 0)) for r in rows]
    in_specs += [pl.BlockSpec(c.shape, lambda i: (0, 0)) for c in consts]
    s_dim = args[0].shape[0]
    n_in, n_out = len(args), len(outs)
    out_shape = [jax.ShapeDtypeStruct((s_dim, w), dt) for w, dt in outs] + [jax.ShapeDtypeStruct(s, F32) for s in accs]
    out_specs = [pl.BlockSpec((tm, w), lambda i: (i, 0)) for w, _ in outs] + [pl.BlockSpec(s, lambda i: (0, 0)) for s in accs]

    def body(*refs):
        vals = fn(*[r[...] for r in refs[:n_in]])
        o_refs = refs[n_in:n_in + n_out]
        a_refs = refs[n_in + n_out:]
        for o, v in zip(o_refs, vals[:n_out]):
            o[...] = v.astype(o.dtype)
        if a_refs:
            @pl.when(pl.program_id(0) == 0)
            def _():
                for a_ref in a_refs:
                    a_ref[...] = jnp.zeros_like(a_ref)

            for a_ref, v in zip(a_refs, vals[n_out:]):
                a_ref[...] += v

    return pl.pallas_call(
        body, grid=(s_dim // tm,), in_specs=in_specs, out_specs=out_specs, out_shape=out_shape,
        compiler_params=_params("arbitrary"), name=name,
    )(*args)


def _rms_fwd(h, w, name):
    return _rowcall(lambda x, w_: (_rms(x, w_, NORM_EPS),), [h], [w], [(h.shape[1], BF16)], [], name=name)[0]


def _rms_bwd(h, w, dy, dres, name):
    def fn(x, dy_, dres_, w_):
        _, vjp = jax.vjp(lambda a, b: _rms(a, b, NORM_EPS), x, w_)
        dx, dw = vjp(dy_)
        return dx + dres_, dw

    return _rowcall(fn, [h, dy, dres], [w], [(h.shape[1], F32)], [w.shape], name=name)


def _ple_fwd(h1, pp, gl, name):
    return _rowcall(lambda a, b, c: (a + b * _sigmoid(c),), [h1, pp, gl], [], [(h1.shape[1], F32)], [], name=name)[0]


def _ple_bwd(dh2, pp, gl, name):
    def fn(d, b, c):
        gate = _sigmoid(c)
        return d * gate, d * b * gate * (1.0 - gate)

    return _rowcall(fn, [dh2, pp, gl], [], [(dh2.shape[1], BF16), (dh2.shape[1], BF16)], [], name=name)


def _loss_bwd(y, target, name):
    width = y.shape[1]

    def fn(a, t):
        d = a - t
        col = jnp.sum(d * d, axis=0, keepdims=True)
        part = col[:, 0:LANES]
        for j in range(1, width // LANES):
            part = part + col[:, j * LANES:(j + 1) * LANES]
        return d * (1.0 / width), part

    return _rowcall(fn, [y, target], [], [(width, F32)], [(1, LANES)], name=name)


CONV_TC = 256


def _shift_down(x, j):
    if j == 0:
        return x
    row = lax.broadcasted_iota(jnp.int32, x.shape, 0)
    return jnp.where(row >= j, pltpu.roll(x, j, 0), 0.0)


def _shift_up(x, j):
    if j == 0:
        return x
    n = x.shape[0]
    row = lax.broadcasted_iota(jnp.int32, x.shape, 0)
    return jnp.where(row < n - j, pltpu.roll(x, n - j, 0), 0.0)


def _conv_fwd(pzx, cw, cb, name):
    s_dim = pzx.shape[0]
    off = SSD_D_INNER // CONV_TC

    def body(x_ref, w_ref, b_ref, o_ref):
        x = x_ref[...]
        w = w_ref[...]
        y = b_ref[...] + w[3:4, :] * x
        for k in range(SSD_D_CONV - 1):
            y = y + w[k:k + 1, :] * _shift_down(x, SSD_D_CONV - 1 - k)
        o_ref[...] = y * _sigmoid(y)

    return pl.pallas_call(
        body, grid=(SSD_CONV_DIM // CONV_TC,),
        in_specs=[pl.BlockSpec((s_dim, CONV_TC), lambda j: (0, off + j)), pl.BlockSpec((SSD_D_CONV, CONV_TC), lambda j: (0, j)),
                  pl.BlockSpec((1, CONV_TC), lambda j: (0, j))],
        out_specs=pl.BlockSpec((s_dim, CONV_TC), lambda j: (0, j)),
        out_shape=jax.ShapeDtypeStruct((s_dim, SSD_CONV_DIM), F32),
        compiler_params=_params("parallel"), name=name,
    )(pzx, cw, cb)


def _conv_bwd(pzx, cw, cb, dxs, dbm, dcm, dzx, name):
    s_dim = pzx.shape[0]
    off = SSD_D_INNER // CONV_TC
    n_x, n_b = dxs.shape[1] // CONV_TC, dbm.shape[1] // CONV_TC

    def body(x_ref, w_ref, b_ref, dxs_ref, dbm_ref, dcm_ref, _, dx_ref, dw_ref, db_ref):
        j = pl.program_id(0)
        d = jnp.where(j < n_x, dxs_ref[...], jnp.where(j < n_x + n_b, dbm_ref[...], dcm_ref[...]))
        x = x_ref[...]
        w = w_ref[...]
        xs = [_shift_down(x, SSD_D_CONV - 1 - k) for k in range(SSD_D_CONV)]
        y = b_ref[...]
        for k in range(SSD_D_CONV):
            y = y + w[k:k + 1, :] * xs[k]
        sg = _sigmoid(y)
        dy = d * (sg * (1.0 + y * (1.0 - sg)))
        dx = w[3:4, :] * dy
        for k in range(SSD_D_CONV - 1):
            dx = dx + w[k:k + 1, :] * _shift_up(dy, SSD_D_CONV - 1 - k)
        dx_ref[...] = dx.astype(dx_ref.dtype)
        for k in range(SSD_D_CONV):
            dw_ref[k:k + 1, :] = jnp.sum(dy * xs[k], axis=0, keepdims=True)
        db_ref[...] = jnp.sum(dy, axis=0, keepdims=True)

    part = lambda lo, n: pl.BlockSpec((s_dim, CONV_TC), lambda j: (0, jnp.clip(j - lo, 0, n - 1)))
    return pl.pallas_call(
        body, grid=(SSD_CONV_DIM // CONV_TC,),
        in_specs=[pl.BlockSpec((s_dim, CONV_TC), lambda j: (0, off + j)), pl.BlockSpec((SSD_D_CONV, CONV_TC), lambda j: (0, j)),
                  pl.BlockSpec((1, CONV_TC), lambda j: (0, j)), part(0, n_x), part(n_x, n_b), part(n_x + n_b, n_b), _ANY],
        out_specs=[pl.BlockSpec((s_dim, CONV_TC), lambda j: (0, off + j)), pl.BlockSpec((SSD_D_CONV, CONV_TC), lambda j: (0, j)),
                   pl.BlockSpec((1, CONV_TC), lambda j: (0, j))],
        out_shape=[jax.ShapeDtypeStruct(dzx.shape, dzx.dtype), jax.ShapeDtypeStruct((SSD_D_CONV, SSD_CONV_DIM), F32),
                   jax.ShapeDtypeStruct((1, SSD_CONV_DIM), F32)],
        input_output_aliases={6: 0}, compiler_params=_params("arbitrary"), name=name,
    )(pzx, cw, cb, dxs, dbm, dcm, dzx)


def _ssd_step(xs, bm, cm, dtraw, bias, alog, dskip, st_in, z, gw, dot, cumsum):
    n = xs.shape[0]
    lane = lax.broadcasted_iota(jnp.int32, (1, LANES), 1)
    sub = lax.broadcasted_iota(jnp.int32, (LANES, 1), 0)
    left = (lane < 64).astype(F32)
    right = 1.0 - left
    top = (sub < 64).astype(F32)
    bot = 1.0 - top
    row = lax.broadcasted_iota(jnp.int32, (n, n), 0)
    colm = lax.broadcasted_iota(jnp.int32, (n, n), 1)
    causal = row >= colm

    dt = _softplus(dtraw + bias)
    adt = dt * (-jnp.exp(alog))
    acum = cumsum(adt)
    acum_t = acum.T
    last = jnp.sum(adt, axis=0, keepdims=True)
    scores = dot(cm, bm, "nt")

    def lane_of(v, h):
        return jnp.sum(v * (lane == h).astype(F32), axis=1, keepdims=True)

    ys, sts = [], []
    for pr in range(4):
        heads = (2 * pr, 2 * pr + 1)
        ac = [lane_of(acum, h) for h in heads]
        ar = [jnp.sum(acum_t * (sub == h).astype(F32), axis=0, keepdims=True) for h in heads]
        dth = [lane_of(dt, h) for h in heads]
        la = [lane_of(last, h) for h in heads]
        dk = [lane_of(dskip, h) for h in heads]
        x2 = xs[:, pr * LANES:(pr + 1) * LANES]
        xdt = x2 * (dth[0] * left + dth[1] * right)
        yd = None
        for i, side in enumerate((left, right)):
            decay = jnp.where(causal, jnp.exp(jnp.minimum(ac[i] - ar[i], 0.0)), 0.0)
            t = dot(scores * decay, xdt * side, "nn")
            yd = t if yd is None else yd + t
        st2 = st_in[pr * LANES:(pr + 1) * LANES, :]
        yo = dot(cm, st2, "nt") * (jnp.exp(ac[0]) * left + jnp.exp(ac[1]) * right)
        dte = jnp.exp(la[0] - ac[0]) * left + jnp.exp(la[1] - ac[1]) * right
        cs = dot(xdt * dte, bm, "tn")
        sts.append(st2 * (jnp.exp(la[0]) * top + jnp.exp(la[1]) * bot) + cs)
        ys.append(yd + yo + (dk[0] * left + dk[1] * right) * x2)
    y = jnp.concatenate(ys, axis=1)
    yg = y * (z * _sigmoid(z))
    yn = yg * lax.rsqrt(jnp.mean(yg * yg, axis=-1, keepdims=True) + GATED_NORM_EPS) * gw
    return yn, jnp.concatenate(sts, axis=0)


def _ssd_specs(n_chunks, rev):
    ci = (lambda c: n_chunks - 1 - c) if rev else (lambda c: c)
    n_x = SSD_D_INNER // LANES
    return dict(
        xs=pl.BlockSpec((SSD_CHUNK, SSD_GROUP_W), lambda g, c: (ci(c), g)),
        bm=pl.BlockSpec((SSD_CHUNK, LANES), lambda g, c: (ci(c), n_x + g)),
        cm=pl.BlockSpec((SSD_CHUNK, LANES), lambda g, c: (ci(c), n_x + SSD_N_GROUPS + g)),
        dt=pl.BlockSpec((None, SSD_CHUNK, LANES), lambda g, c: (g, ci(c), 0)),
        vec=pl.BlockSpec((None, 1, LANES), lambda g, c: (g, 0, 0)),
        z=pl.BlockSpec((SSD_CHUNK, SSD_GROUP_W), lambda g, c: (ci(c), g)),
        gw=pl.BlockSpec((1, SSD_GROUP_W), lambda g, c: (0, g)),
        st=pl.BlockSpec((None, None, SSD_GROUP_W, SSD_D_STATE), lambda g, c: (g, ci(c), 0, 0)),
    )


def _ssd_fwd(act, dtg, bias, alog, dskip, pzx, gw, name, rider=None):
    s_dim = act.shape[0]
    n_chunks = s_dim // SSD_CHUNK
    sp = _ssd_specs(n_chunks, False)

    def body(xs, bm, cm, dt, b_ref, a_ref, d_ref, z, gw_ref, yn_ref, st_ref, state):
        @pl.when(pl.program_id(1) == 0)
        def _():
            state[...] = jnp.zeros_like(state)

        st_in = state[...]
        st_ref[...] = st_in
        yn, st_out = _ssd_step(xs[...], bm[...], cm[...], dt[...], b_ref[...], a_ref[...], d_ref[...], st_in, z[...], gw_ref[...],
                               _dot, _cumsum_rows_raw)
        yn_ref[...] = yn.astype(yn_ref.dtype)
        state[...] = st_out

    outs, rode = _pcall(
        body, grid=(SSD_N_GROUPS, n_chunks),
        in_specs=[sp["xs"], sp["bm"], sp["cm"], sp["dt"], sp["vec"], sp["vec"], sp["vec"], sp["z"], sp["gw"]],
        out_specs=[sp["xs"], sp["st"]],
        out_shape=[jax.ShapeDtypeStruct((s_dim, SSD_D_INNER), BF16),
                   jax.ShapeDtypeStruct((SSD_N_GROUPS, n_chunks, SSD_GROUP_W, SSD_D_STATE), F32)],
        scratch_shapes=[pltpu.VMEM((SSD_GROUP_W, SSD_D_STATE), F32)],
        args=[act, act, act, dtg, bias, alog, dskip, pzx, gw], sem=("parallel", "arbitrary"), name=name, rider=rider)
    return (outs, rode) if rider is not None else outs


def _ssd_bwd(act, dtg, bias, alog, dskip, pzx, gw, states, dyn, name, rider=None):
    s_dim = act.shape[0]
    n_chunks = s_dim // SSD_CHUNK
    sp = _ssd_specs(n_chunks, True)
    rc = lambda c: n_chunks - 1 - c

    def body(xs, bm, cm, dt, b_ref, a_ref, d_ref, z, gw_ref, st_ref, dyn_ref,
             dxs_ref, dbm_ref, dcm_ref, ddt_ref, db_ref, da_ref, dd_ref, dz_ref, dgw_ref, dstate):
        first = pl.program_id(1) == 0

        @pl.when(first)
        def _():
            dstate[...] = jnp.zeros_like(dstate)
            db_ref[...] = jnp.zeros_like(db_ref)
            da_ref[...] = jnp.zeros_like(da_ref)
            dd_ref[...] = jnp.zeros_like(dd_ref)
            dgw_ref[...] = jnp.zeros_like(dgw_ref)

        fn = functools.partial(_ssd_step, dot=_gdot, cumsum=_cumsum_rows)
        _, vjp = jax.vjp(fn, xs[...], bm[...], cm[...], dt[...], b_ref[...], a_ref[...], d_ref[...], st_ref[...], z[...], gw_ref[...])
        dxs, dbm, dcm, ddt, db, da, dd, dst, dz, dgw = vjp((dyn_ref[...], dstate[...]))
        dxs_ref[...] = dxs
        dbm_ref[...] = dbm
        dcm_ref[...] = dcm
        ddt_ref[...] = ddt
        dz_ref[...] = dz.astype(dz_ref.dtype)
        db_ref[...] += db
        da_ref[...] += da
        dd_ref[...] += dd
        dgw_ref[...] += dgw
        dstate[...] = dst

    bc = pl.BlockSpec((SSD_CHUNK, LANES), lambda g, c: (rc(c), g))
    outs, rode = _pcall(
        body, grid=(SSD_N_GROUPS, n_chunks),
        in_specs=[sp["xs"], sp["bm"], sp["cm"], sp["dt"], sp["vec"], sp["vec"], sp["vec"], sp["z"], sp["gw"], sp["st"], sp["xs"]],
        out_specs=[sp["xs"], bc, bc, sp["dt"], sp["vec"], sp["vec"], sp["vec"], sp["xs"], sp["gw"]],
        out_shape=[jax.ShapeDtypeStruct((s_dim, SSD_D_INNER), F32),
                   jax.ShapeDtypeStruct((s_dim, SSD_N_GROUPS * SSD_D_STATE), F32),
                   jax.ShapeDtypeStruct((s_dim, SSD_N_GROUPS * SSD_D_STATE), F32),
                   jax.ShapeDtypeStruct((SSD_N_GROUPS, s_dim, LANES), F32),
                   jax.ShapeDtypeStruct((SSD_N_GROUPS, 1, LANES), F32),
                   jax.ShapeDtypeStruct((SSD_N_GROUPS, 1, LANES), F32),
                   jax.ShapeDtypeStruct((SSD_N_GROUPS, 1, LANES), F32),
                   jax.ShapeDtypeStruct((s_dim, SSD_ZX), BF16),
                   jax.ShapeDtypeStruct((1, SSD_D_INNER), F32)],
        scratch_shapes=[pltpu.VMEM((SSD_GROUP_W, SSD_D_STATE), F32)],
        args=[act, act, act, dtg, bias, alog, dskip, pzx, gw, states, dyn], sem=("arbitrary", "arbitrary"), name=name, rider=rider)
    return (outs, rode) if rider is not None else outs


SB_T = 128
SB_GROUP = 8
SB_WIDE = SB_GROUP * SB_T
SB_HB = 4
SB_SCALE = 1.0 / math.sqrt(SB_HEAD_DIM)


def _qknorm_fwd(proj, qw, kw, name, tm=512):
    s_dim = proj.shape[0]

    def body(q_ref, k_ref, v_ref, qw_ref, kw_ref, qo, ko, vo):
        qo[...] = _rms(q_ref[...], qw_ref[...], NORM_EPS).astype(BF16)
        ko[...] = _rms(k_ref[...], kw_ref[...], NORM_EPS).astype(BF16)
        vo[...] = v_ref[...].astype(BF16)

    blk = lambda o: pl.BlockSpec((tm, SB_HEAD_DIM), lambda i, h: (i, o + h))
    vec = pl.BlockSpec((1, SB_HEAD_DIM), lambda i, h: (0, 0))
    return pl.pallas_call(
        body, grid=(s_dim // tm, SB_N_HEADS),
        in_specs=[blk(0), blk(SB_N_HEADS), blk(2 * SB_N_HEADS), vec, vec],
        out_specs=[blk(0)] * 3,
        out_shape=[jax.ShapeDtypeStruct((s_dim, SB_WIDTH), BF16)] * 3,
        compiler_params=_params("parallel", "parallel"), name=name,
    )(proj, proj, proj, qw, kw)


def _qknorm_bwd(proj, qw, kw, dqn, dkn, name, tm=512):
    s_dim = proj.shape[0]

    def body(q_ref, k_ref, dq_ref, dk_ref, qw_ref, kw_ref, dqo, dko, dqw, dkw):
        @pl.when((pl.program_id(0) == 0) & (pl.program_id(1) == 0))
        def _():
            dqw[...] = jnp.zeros_like(dqw)
            dkw[...] = jnp.zeros_like(dkw)

        fn = lambda a, b: _rms(a, b, NORM_EPS)
        _, vq = jax.vjp(fn, q_ref[...], qw_ref[...])
        dq, dw = vq(dq_ref[...])
        dqo[...] = dq.astype(BF16)
        dqw[...] += dw
        _, vk = jax.vjp(fn, k_ref[...], kw_ref[...])
        dk, dw = vk(dk_ref[...])
        dko[...] = dk.astype(BF16)
        dkw[...] += dw

    blk = lambda o: pl.BlockSpec((tm, SB_HEAD_DIM), lambda i, h: (i, o + h))
    vec = pl.BlockSpec((1, SB_HEAD_DIM), lambda i, h: (0, 0))
    return pl.pallas_call(
        body, grid=(s_dim // tm, SB_N_HEADS),
        in_specs=[blk(0), blk(SB_N_HEADS), blk(0), blk(0), vec, vec],
        out_specs=[blk(0), blk(0), vec, vec],
        out_shape=[jax.ShapeDtypeStruct((s_dim, SB_WIDTH), BF16)] * 2 + [jax.ShapeDtypeStruct((1, SB_HEAD_DIM), F32)] * 2,
        compiler_params=_params("arbitrary", "arbitrary"), name=name,
    )(proj, proj, dqn, dkn, qw, kw)


def _sb_logits(q, k, strict):
    z = _dot(q, k, "nt") * SB_SCALE
    lb = jnp.minimum(z, 0.0) - jnp.log(1.0 + jnp.exp(-jnp.abs(z)))
    lm = lb - z
    if strict is not None:
        lm = jnp.where(strict, lm, 0.0)
    return lb, lm


def _sb_strict(qi, grp):
    r = lax.broadcasted_iota(jnp.int32, (SB_T, SB_WIDE), 0) + qi * SB_T
    c = lax.broadcasted_iota(jnp.int32, (SB_T, SB_WIDE), 1) + grp * SB_WIDE
    return c < r


def _head_lanes(hh):
    return slice(hh * SB_HEAD_DIM, (hh + 1) * SB_HEAD_DIM)


def _sb_fwd(qn, kn, vb, proj, name, rider=None):
    s_dim = qn.shape[0]
    nq = s_dim // SB_T
    assert nq % SB_GROUP == 0

    def body(q_ref, k_ref, v_ref, g_ref, og_ref, o_ref, t_ref):
        qi = pl.program_id(1)
        top = qi // SB_GROUP
        after = _tri(SB_T, True, strict=True)
        qs = [q_ref[:, _head_lanes(hh)] for hh in range(SB_HB)]

        def step(grp, masked, carries):
            start = pl.multiple_of(grp * SB_WIDE, SB_WIDE)
            strict = _sb_strict(qi, grp) if masked else None
            out = []
            for hh in range(SB_HB):
                o_acc, cr = carries[hh]
                k = k_ref[pl.ds(start, SB_WIDE), _head_lanes(hh)]
                v = v_ref[pl.ds(start, SB_WIDE), _head_lanes(hh)]
                lb, lm = _sb_logits(qs[hh], k, strict)
                rest = [None] * SB_GROUP
                for t in reversed(range(SB_GROUP)):
                    lm_t = lm[:, t * SB_T:(t + 1) * SB_T]
                    rest[t] = cr + _split_dot(lm_t, after, 2, True)
                    cr = cr + jnp.sum(lm_t, axis=1, keepdims=True)
                a = jnp.exp(lb + jnp.concatenate(rest, axis=1))
                if masked:
                    a = jnp.where(strict, a, 0.0)
                out.append((o_acc + _dot(a, v), cr))
            return tuple(out)

        init = tuple((jnp.zeros((SB_T, SB_HEAD_DIM), F32), jnp.zeros((SB_T, 1), F32)) for _ in range(SB_HB))
        carries = step(top, True, init)
        carries = lax.fori_loop(0, top, lambda i, c: step(top - 1 - i, False, c), carries)
        for hh in range(SB_HB):
            o, tot = carries[hh]
            g = g_ref[:, _head_lanes(hh)]
            o_ref[:, _head_lanes(hh)] = o
            og_ref[:, _head_lanes(hh)] = (o * (g * _sigmoid(g))).astype(og_ref.dtype)
            t_ref[hh] = jnp.broadcast_to(tot, (SB_T, LANES))

    wide = SB_HB * SB_HEAD_DIM
    qb = pl.BlockSpec((SB_T, wide), lambda h, i: (i, h))
    kv = pl.BlockSpec((s_dim, wide), lambda h, i: (0, h))
    outs, rode = _pcall(
        body, grid=(SB_N_HEADS // SB_HB, nq),
        in_specs=[qb, kv, kv, pl.BlockSpec((SB_T, wide), lambda h, i: (i, 3 * SB_N_HEADS // SB_HB + h))],
        out_specs=[qb, qb, pl.BlockSpec((SB_HB, SB_T, LANES), lambda h, i: (h, i, 0))],
        out_shape=[jax.ShapeDtypeStruct((s_dim, SB_WIDTH), BF16), jax.ShapeDtypeStruct((s_dim, SB_WIDTH), F32),
                   jax.ShapeDtypeStruct((SB_N_HEADS, s_dim, LANES), F32)],
        args=[qn, kn, vb, proj], sem=("parallel", "arbitrary"), name=name, rider=rider)
    return (outs, rode) if rider is not None else outs


def _sb_bwd(qn, kn, vb, proj, o, tot, dog, name, rider=None):
    s_dim = qn.shape[0]
    nq = s_dim // SB_T
    assert nq % SB_GROUP == 0

    def body(q_ref, k_ref, v_ref, g_ref, o_ref, t_ref, dog_ref, dq_ref, dk_ref, dv_ref, dvb_ref, dg_ref):
        qi = pl.program_id(1)
        top = qi // SB_GROUP

        @pl.when(qi == 0)
        def _():
            dk_ref[...] = jnp.zeros_like(dk_ref)
            dv_ref[...] = jnp.zeros_like(dv_ref)

        after = _tri(SB_T, True, strict=True)
        before = _tri(SB_T, False, strict=True)
        qs, dos, totals = [], [], []
        for hh in range(SB_HB):
            g = g_ref[:, _head_lanes(hh)]
            sg = _sigmoid(g)
            dog_v = dog_ref[:, _head_lanes(hh)]
            dg_ref[:, _head_lanes(hh)] = (dog_v * o_ref[:, _head_lanes(hh)] * (sg * (1.0 + g * (1.0 - sg)))).astype(dg_ref.dtype)
            dos.append((dog_v * (g * sg)).astype(BF16))
            qs.append(q_ref[:, _head_lanes(hh)])
            totals.append(t_ref[hh][:, 0:1])

        def step(grp, masked, carries):
            start = pl.multiple_of(grp * SB_WIDE, SB_WIDE)
            strict = _sb_strict(qi, grp) if masked else None
            out = []
            for hh in range(SB_HB):
                dq_acc, cp, ce = carries[hh]
                q, do = qs[hh], dos[hh]
                k = k_ref[pl.ds(start, SB_WIDE), _head_lanes(hh)]
                v = v_ref[pl.ds(start, SB_WIDE), _head_lanes(hh)]
                lb, lm = _sb_logits(q, k, strict)
                rest = []
                for t in range(SB_GROUP):
                    lm_t = lm[:, t * SB_T:(t + 1) * SB_T]
                    cp = cp + jnp.sum(lm_t, axis=1, keepdims=True)
                    rest.append((totals[hh] - cp) + _split_dot(lm_t, after, 2, True))
                a = jnp.exp(lb + jnp.concatenate(rest, axis=1))
                if masked:
                    a = jnp.where(strict, a, 0.0)
                e = a * _dot(do, v, "nt")
                excl = []
                for t in range(SB_GROUP):
                    e_t = e[:, t * SB_T:(t + 1) * SB_T]
                    excl.append(ce + _split_dot(e_t, before, 2, True))
                    ce = ce + jnp.sum(e_t, axis=1, keepdims=True)
                eex = jnp.concatenate(excl, axis=1)
                if masked:
                    eex = jnp.where(strict, eex, 0.0)
                sig = jnp.exp(lb)
                dz = (e * (1.0 - sig) - eex * sig) * SB_SCALE
                dv_ref[pl.ds(start, SB_WIDE), _head_lanes(hh)] += _dot(a, do, "tn")
                dk_ref[pl.ds(start, SB_WIDE), _head_lanes(hh)] += _dot(dz, q, "tn")
                out.append((dq_acc + _dot(dz, k), cp, ce))
            return tuple(out)

        zero = jnp.zeros((SB_T, 1), F32)
        init = tuple((jnp.zeros((SB_T, SB_HEAD_DIM), F32), zero, zero) for _ in range(SB_HB))
        carries = lax.fori_loop(0, top, lambda i, c: step(i, False, c), init)
        carries = step(top, True, carries)
        for hh in range(SB_HB):
            dq_ref[:, _head_lanes(hh)] = carries[hh][0]

        @pl.when(qi == nq - 1)
        def _():
            dvb_ref[...] = dv_ref[...].astype(BF16)

    wide = SB_HB * SB_HEAD_DIM
    qb = pl.BlockSpec((SB_T, wide), lambda h, i: (i, h))
    kv = pl.BlockSpec((s_dim, wide), lambda h, i: (0, h))
    outs, rode = _pcall(
        body, grid=(SB_N_HEADS // SB_HB, nq),
        in_specs=[qb, kv, kv, pl.BlockSpec((SB_T, wide), lambda h, i: (i, 3 * SB_N_HEADS // SB_HB + h)), qb,
                  pl.BlockSpec((SB_HB, SB_T, LANES), lambda h, i: (h, i, 0)), qb],
        out_specs=[qb, kv, kv, kv, qb],
        out_shape=[jax.ShapeDtypeStruct((s_dim, SB_WIDTH), F32), jax.ShapeDtypeStruct((s_dim, SB_WIDTH), F32),
                   jax.ShapeDtypeStruct((s_dim, SB_WIDTH), F32), jax.ShapeDtypeStruct((s_dim, SB_WIDTH), BF16),
                   jax.ShapeDtypeStruct((s_dim, SB_WIDTH), BF16)],
        args=[qn, kn, vb, proj, o, tot, dog], sem=("parallel", "arbitrary"), name=name, rider=rider)
    return (outs, rode) if rider is not None else outs


def _adamw_math(w, g, m, v):
    m = ADAM_B1 * m + (1.0 - ADAM_B1) * g
    v = ADAM_B2 * v + (1.0 - ADAM_B2) * (g * g)
    m_hat = m / (1.0 - ADAM_B1 ** ADAM_STEP)
    v_hat = v / (1.0 - ADAM_B2 ** ADAM_STEP)
    delta = -ADAM_LR * (m_hat / (jnp.sqrt(v_hat) + ADAM_EPS) + ADAM_WD * w)
    return delta, m, v


def _row_block(rows, cols, itemsize=4, limit=1 << 20):
    tr = rows
    while tr * cols * itemsize > limit and tr % (2 * BF16_ROWS) == 0:
        tr //= 2
    return tr


def _divisor_block(rows, cols, itemsize=4, limit=2 << 20):
    best = BF16_ROWS
    for t in range(BF16_ROWS, rows + 1, BF16_ROWS):
        if rows % t == 0 and t * cols * itemsize <= limit:
            best = t
    return best


def _adamw(w, g, m, v, name):
    n, rows, cols = w.shape
    tr = rows if rows * cols * 4 <= (1 << 20) else _divisor_block(rows, cols, limit=1 << 20)

    def body(w_ref, g_ref, m_ref, v_ref, d_out, m_out, v_out):
        d, m_new, v_new = _adamw_math(w_ref[...], g_ref[...], m_ref[...], v_ref[...])
        d_out[...] = d
        m_out[...] = m_new
        v_out[...] = v_new

    blk = pl.BlockSpec((None, tr, cols), lambda i, j: (i, j, 0))
    return pl.pallas_call(
        body, grid=(n, rows // tr), in_specs=[blk] * 4, out_specs=[blk] * 3,
        out_shape=[jax.ShapeDtypeStruct(w.shape, F32)] * 3,
        compiler_params=_params("parallel", "parallel"), name=name,
    )(w, g, m, v)


_FLIPS = ((1, 0), (0, 1), (1, 1))


def _place():
    return lax.axis_index("x"), lax.axis_index("y"), lax.axis_index("c")


def _flip(v, f):
    return 1 - v if f else v


def _half_rows(ref, lead, hc, hr):
    return ref.at[(*lead, pl.ds(pl.multiple_of(hc * hr, BF16_ROWS), hr), slice(None))]


def _half_cols(ref, lead, hc, hw):
    return ref.at[(*lead, pl.ds(pl.multiple_of(hc * hw, LANES), hw))]


def _rows_of_chip(chip, r):
    return pl.ds(pl.multiple_of(chip * r, BF16_ROWS), r)


def _slot_half(gathered, shard_shape, chip, l, hc):
    r, c = shard_shape[1:]
    if len(gathered.shape) == 3:
        return _half_cols(gathered, (l, _rows_of_chip(chip, r)), hc, c // 2)
    return _half_rows(gathered, (chip, l), hc, r // 2)


def _shard_half(shard, stacked, l, hc):
    r, c = shard.shape[1:]
    return _half_cols(shard, (l, slice(None)), hc, c // 2) if stacked else _half_rows(shard, (l,), hc, r // 2)


def _remote(src, dst, send, recv, k, to):
    return pltpu.make_async_remote_copy(src_ref=src, dst_ref=dst, send_sem=send.at[k], recv_sem=recv.at[k], device_id=to,
                                        device_id_type=MESH)


def _comm_call(reads, writes, n_sems, phases, name):
    passed = [k for k, w in enumerate(writes) if not isinstance(w, jax.ShapeDtypeStruct)]
    n_rd = len(reads)

    def body(*refs):
        rd = refs[:n_rd]
        wr = refs[n_rd + len(passed):n_rd + len(passed) + len(writes)]
        send, recv = refs[-2:]
        for phase in phases:
            sends, arrivals = phase(rd, wr, send, recv)
            for cp in sends:
                cp.start()
            for cp in arrivals:
                cp.wait_recv()
            for cp in sends:
                cp.wait_send()

    return pl.pallas_call(
        body, in_specs=[_ANY] * (n_rd + len(passed)), out_specs=[_ANY] * len(writes),
        out_shape=[jax.ShapeDtypeStruct(w.shape, w.dtype) for w in writes],
        input_output_aliases={n_rd + pos: k for pos, k in enumerate(passed)},
        scratch_shapes=[pltpu.SemaphoreType.DMA((n_sems,)), pltpu.SemaphoreType.DMA((n_sems,))], name=name,
    )(*reads, *[writes[k] for k in passed])


def _ag_ici(pieces, names, base=0):
    def phase(shards, gathered, send, recv):
        x, y, c = _place()
        me = 2 * x + y
        sends, arrivals = [], []
        for k, (n, l) in enumerate(pieces):
            a = names.index(n)
            shape = shards[a].shape
            src = _shard_half(shards[a], len(gathered[a].shape) == 3, l, c)
            for j, (fx, fy) in enumerate(_FLIPS):
                tx, ty = _flip(x, fx), _flip(y, fy)
                sends.append(_remote(src, _slot_half(gathered[a], shape, me, l, c), send, recv, base + 3 * k + j, (tx, ty, c)))
                arrivals.append(_remote(src, _slot_half(gathered[a], shape, 2 * tx + ty, l, c), send, recv, base + 3 * k + j, (tx, ty, c)))
        return sends, arrivals

    return phase


def _ag_pass_on(pieces, names, shapes, base=0):
    def phase(_, gathered, send, recv):
        x, y, c = _place()
        sibling = (x, y, 1 - c)
        sends, arrivals = [], []
        for k, (n, l) in enumerate(pieces):
            a = names.index(n)
            for j, (fx, fy) in enumerate(_FLIPS):
                chip = 2 * _flip(x, fx) + _flip(y, fy)
                landed = _slot_half(gathered[a], shapes[a], chip, l, c)
                sends.append(_remote(landed, landed, send, recv, base + 3 * k + j, sibling))
                arrivals.append(_remote(landed, _slot_half(gathered[a], shapes[a], chip, l, 1 - c), send, recv, base + 3 * k + j, sibling))
        return sends, arrivals

    return phase


def _other_half(ref, hc):
    if len(ref.shape) == 3:
        return _half_cols(ref, (slice(None), slice(None)), hc, ref.shape[2] // 2)
    return _half_rows(ref, (slice(None), slice(None)), hc, ref.shape[2] // 2)


def _half_shape(shape):
    return shape[:2] + (shape[2] // 2,) if len(shape) == 3 else shape[:2] + (shape[2] // 2, shape[3])


def _pair_exchange(grads, name):
    def phase(ins, outs, send, recv):
        x, y, c = _place()
        cps = [_remote(_other_half(ins[a], 1 - c), outs[a], send, recv, a, (x, y, 1 - c)) for a in range(len(grads))]
        return cps, cps

    outs = [jax.ShapeDtypeStruct(_half_shape(g.shape), g.dtype) for g in grads]
    return _comm_call(grads, outs, len(grads), [phase], name)


def _pair_sum_stacked(g, got, place, name):
    _, rows, hw = got.shape
    tr = _divisor_block(rows, hw)

    def body(place_ref, g_ref, r_ref, o_ref):
        o_ref[...] = (g_ref[...].astype(F32) + r_ref[...].astype(F32)).astype(o_ref.dtype)

    blk = pl.BlockSpec((None, tr, hw), lambda i, pr: (0, i, 0))
    return pl.pallas_call(
        body,
        grid_spec=pltpu.PrefetchScalarGridSpec(
            num_scalar_prefetch=1, grid=(rows // tr,),
            in_specs=[pl.BlockSpec((None, tr, hw), lambda i, pr: (0, i, pr[1])), blk], out_specs=blk),
        out_shape=jax.ShapeDtypeStruct(got.shape, BF16),
        compiler_params=_params("parallel"), name=name,
    )(place, g, got)


def _pair_sum(g, got, place, name):
    if len(g.shape) == 3:
        return _pair_sum_stacked(g, got, place, name)
    _, layers, hr, cols = got.shape
    tr = _row_block(hr, cols)
    per = hr // tr

    def body(place_ref, g_ref, r_ref, o_ref):
        o_ref[...] = (g_ref[...].astype(F32) + r_ref[...].astype(F32)).astype(o_ref.dtype)

    blk = pl.BlockSpec((None, None, tr, cols), lambda k, l, i, pr: (k, l, i, 0))
    return pl.pallas_call(
        body,
        grid_spec=pltpu.PrefetchScalarGridSpec(
            num_scalar_prefetch=1, grid=(4, layers, per),
            in_specs=[pl.BlockSpec((None, None, tr, cols), lambda k, l, i, pr: (k, l, pr[1] * per + i, 0)), blk],
            out_specs=blk),
        out_shape=jax.ShapeDtypeStruct(got.shape, BF16),
        compiler_params=_params("parallel", "parallel", "parallel"), name=name,
    )(place, g, got)


def _scatter_phase(n_arr):
    def phase(ins, outs, send, recv):
        x, y, c = _place()
        cps = []
        for a in range(n_arr):
            for j, (fx, fy) in enumerate(_FLIPS):
                tx, ty = _flip(x, fx), _flip(y, fy)
                if len(ins[a].shape) == 3:
                    src = ins[a].at[:, _rows_of_chip(2 * tx + ty, ins[a].shape[1] // 4), :]
                else:
                    src = ins[a].at[2 * tx + ty]
                cps.append(_remote(src, outs[a].at[j], send, recv, 3 * a + j, (tx, ty, c)))
        return cps, cps

    return phase


def _scatter_outs(pairs):
    return [jax.ShapeDtypeStruct((3, 1, p.shape[1] // 4, p.shape[2]) if len(p.shape) == 3 else (3,) + p.shape[1:], p.dtype) for p in pairs]


def _chip_scatter(pairs, name):
    return _comm_call(pairs, _scatter_outs(pairs), 3 * len(pairs), [_scatter_phase(len(pairs))], name)


def _scatter_rider(pairs):
    return _Rider(pairs, _scatter_outs(pairs), 3 * len(pairs), _scatter_phase(len(pairs)))


def _chip_sum_stacked(p, got, place, layer, layers, o_buf, name):
    _, r, hw = got.shape[1:]
    tr = _divisor_block(r, hw)
    per = r // tr

    def body(place_ref, p_ref, r_ref, *rest):
        o_ref = rest[-1]
        acc = p_ref[...].astype(F32)
        for j in range(3):
            acc = acc + r_ref[j].astype(F32)
        o_ref[...] = acc

    has_buf = o_buf is not None
    return pl.pallas_call(
        body,
        grid_spec=pltpu.PrefetchScalarGridSpec(
            num_scalar_prefetch=1, grid=(per,),
            in_specs=[pl.BlockSpec((None, tr, hw), lambda i, pr: (0, pr[0] * per + i, 0)),
                      pl.BlockSpec((3, None, tr, hw), lambda i, pr: (0, 0, i, 0))] + ([_ANY] if has_buf else []),
            out_specs=pl.BlockSpec((None, tr, hw), lambda i, pr: (layer, i, pr[1]))),
        out_shape=jax.ShapeDtypeStruct((layers, r, 2 * hw), F32),
        input_output_aliases={3: 0} if has_buf else {},
        compiler_params=_params("parallel"), name=name,
    )(*((place, p, got) + ((o_buf,) if has_buf else ())))


def _chip_sum(p, got, place, layer, layers, o_buf, name):
    if len(p.shape) == 3:
        return _chip_sum_stacked(p, got, place, layer, layers, o_buf, name)
    _, _, hr, cols = p.shape
    tr = _row_block(hr, cols)
    per = hr // tr

    def body(place_ref, p_ref, r_ref, *rest):
        o_ref = rest[-1]
        acc = p_ref[...].astype(F32)
        for j in range(3):
            acc = acc + r_ref[j].astype(F32)
        o_ref[...] = acc

    has_buf = o_buf is not None
    return pl.pallas_call(
        body,
        grid_spec=pltpu.PrefetchScalarGridSpec(
            num_scalar_prefetch=1, grid=(per,),
            in_specs=[pl.BlockSpec((None, None, tr, cols), lambda i, pr: (pr[0], 0, i, 0)),
                      pl.BlockSpec((3, None, tr, cols), lambda i, pr: (0, 0, i, 0))] + ([_ANY] if has_buf else []),
            out_specs=pl.BlockSpec((None, tr, cols), lambda i, pr: (layer, pr[1] * per + i, 0))),
        out_shape=jax.ShapeDtypeStruct((layers, 2 * hr, cols), F32),
        input_output_aliases={3: 0} if has_buf else {},
        compiler_params=_params("parallel"), name=name,
    )(*((place, p, got) + ((o_buf,) if has_buf else ())))


def _pair_gather(halves, by_cols):
    def phase(_, bufs, send, recv):
        x, y, c = _place()
        sends, arrivals = [], []
        for a, h in enumerate(halves):
            cut = (lambda hc, a=a, h=h: _half_cols(bufs[a], (slice(None), slice(None)), hc, h.shape[2] // 2)) if by_cols[a] else (
                lambda hc, a=a, h=h: _half_rows(bufs[a], (slice(None),), hc, h.shape[1] // 2))
            sends.append(_remote(cut(c), cut(c), send, recv, a, (x, y, 1 - c)))
            arrivals.append(_remote(cut(c), cut(1 - c), send, recv, a, (x, y, 1 - c)))
        return sends, arrivals

    return _comm_call([], halves, len(halves), [phase], "rs_pair_gather")


def _allreduce_small(v, name):
    rows, cols = v.shape

    def body(v_ref, o_ref, buf, send_sems, recv_sems):
        x, y, c = _place()
        me = 4 * x + 2 * y + c
        buf[0] = v_ref[...]
        cps = []
        for k in range(1, 8):
            kx, ky, kc = (k >> 2) & 1, (k >> 1) & 1, k & 1
            cp = pltpu.make_async_remote_copy(src_ref=v_ref, dst_ref=buf.at[k], send_sem=send_sems.at[k - 1], recv_sem=recv_sems.at[k - 1],
                                              device_id=(_flip(x, kx), _flip(y, ky), _flip(c, kc)), device_id_type=MESH)
            cp.start()
            cps.append(cp)
        for cp in cps:
            cp.wait()
        acc = buf[me]
        for d in range(1, 8):
            acc = acc + buf[jnp.bitwise_xor(d, me)]
        o_ref[...] = acc

    vm = pl.BlockSpec(memory_space=pltpu.VMEM)
    return pl.pallas_call(
        body, in_specs=[vm], out_specs=vm, out_shape=jax.ShapeDtypeStruct((rows, cols), F32),
        scratch_shapes=[pltpu.VMEM((8, rows, cols), F32), pltpu.SemaphoreType.DMA((7,)), pltpu.SemaphoreType.DMA((7,))],
        name=name,
    )(v)


def _pad_lanes(a):
    return jnp.pad(a, ((0, 0), (0, LANES - a.shape[1])))


def _group_lanes(v):
    return jnp.pad(v.reshape(SSD_N_GROUPS, 1, 8), ((0, 0), (0, 0), (0, LANES - 8)))


def kernel(x, p, norm_w, ssd_in_w, ssd_conv_w, ssd_conv_b, ssd_dt_bias, ssd_a_log, ssd_d, ssd_gnorm_w, ssd_out_w, sb_in_w, sb_qn_w, sb_kn_w, sb_out_w, ple_norm_w, ple_gate_w, ple_proj_w, loss_target, m_norm_w, m_ssd_in_w, m_ssd_conv_w, m_ssd_conv_b, m_ssd_dt_bias, m_ssd_a_log, m_ssd_d, m_ssd_gnorm_w, m_ssd_out_w, m_sb_in_w, m_sb_qn_w, m_sb_kn_w, m_sb_out_w, m_ple_norm_w, m_ple_gate_w, m_ple_proj_w, v_norm_w, v_ssd_in_w, v_ssd_conv_w, v_ssd_conv_b, v_ssd_dt_bias, v_ssd_a_log, v_ssd_d, v_ssd_gnorm_w, v_ssd_out_w, v_sb_in_w, v_sb_qn_w, v_sb_kn_w, v_sb_out_w, v_ple_norm_w, v_ple_gate_w, v_ple_proj_w):
    w_in = dict(norm_w=norm_w, ssd_in_w=ssd_in_w, ssd_conv_w=ssd_conv_w, ssd_conv_b=ssd_conv_b, ssd_dt_bias=ssd_dt_bias,
                ssd_a_log=ssd_a_log, ssd_d=ssd_d, ssd_gnorm_w=ssd_gnorm_w, ssd_out_w=ssd_out_w, sb_in_w=sb_in_w, sb_qn_w=sb_qn_w,
                sb_kn_w=sb_kn_w, sb_out_w=sb_out_w, ple_norm_w=ple_norm_w, ple_gate_w=ple_gate_w, ple_proj_w=ple_proj_w)
    m_in = dict(norm_w=m_norm_w, ssd_in_w=m_ssd_in_w, ssd_conv_w=m_ssd_conv_w, ssd_conv_b=m_ssd_conv_b, ssd_dt_bias=m_ssd_dt_bias,
                ssd_a_log=m_ssd_a_log, ssd_d=m_ssd_d, ssd_gnorm_w=m_ssd_gnorm_w, ssd_out_w=m_ssd_out_w, sb_in_w=m_sb_in_w,
                sb_qn_w=m_sb_qn_w, sb_kn_w=m_sb_kn_w, sb_out_w=m_sb_out_w, ple_norm_w=m_ple_norm_w, ple_gate_w=m_ple_gate_w,
                ple_proj_w=m_ple_proj_w)
    v_in = dict(norm_w=v_norm_w, ssd_in_w=v_ssd_in_w, ssd_conv_w=v_ssd_conv_w, ssd_conv_b=v_ssd_conv_b, ssd_dt_bias=v_ssd_dt_bias,
                ssd_a_log=v_ssd_a_log, ssd_d=v_ssd_d, ssd_gnorm_w=v_ssd_gnorm_w, ssd_out_w=v_ssd_out_w, sb_in_w=v_sb_in_w,
                sb_qn_w=v_sb_qn_w, sb_kn_w=v_sb_kn_w, sb_out_w=v_sb_out_w, ple_norm_w=v_ple_norm_w, ple_gate_w=v_ple_gate_w,
                ple_proj_w=v_ple_proj_w)
    ix, iy, ic = lax.axis_index("x"), lax.axis_index("y"), lax.axis_index("c")
    chip = (2 * ix + iy).astype(jnp.int32)
    place = jnp.stack([chip, ic.astype(jnp.int32)])
    zero = jnp.zeros((), jnp.int32)
    big_names = [n for n, _, _ in _BIG]
    layers_of = {n: s[0] for n, s, _ in _BIG}
    cut_of = {n: cut for n, _, cut in _BIG}

    def layer_pieces(i):
        mixer = ("ssd_in_w", "ssd_out_w") if i % 2 == 0 else ("sb_in_w", "sb_out_w")
        return [(mixer[0], i // 2), (mixer[1], i // 2), ("ple_gate_w", i), ("ple_proj_w", i)]

    def names_of(pieces):
        return [n for n in big_names if any(n == q for q, _ in pieces)]

    held = lambda n, a: a.transpose(0, 2, 1) if cut_of[n] == "stack" else a
    mine = {n: held(n, w_in[n]).astype(BF16) for n in big_names}
    shard_shapes = [mine[n].shape for n in big_names]
    room = [jax.ShapeDtypeStruct((s[0], 4 * s[1], s[2]) if cut_of[n] == "stack" else (4,) + s, BF16) for n, s in zip(big_names, shard_shapes)]
    first = layer_pieces(0)[:1]
    gathered = _comm_call([mine[n] for n in big_names], room, 6 * len(first),
                          [_ag_ici(first, big_names), _ag_pass_on(first, big_names, shard_shapes, base=3 * len(first))], "allgather_layer0")
    gw = {}
    for n, g in zip(big_names, gathered):
        if cut_of[n] == "stack":
            layers, r, c = mine[n].shape
            gw[n] = lax.dynamic_update_slice(g.reshape(layers, 4, r, c), mine[n][:, None], (zero, chip, zero, zero)).reshape(g.shape)
        else:
            gw[n] = lax.dynamic_update_slice(g, mine[n][None], (chip, zero, zero, zero))

    def gather_rider(pieces):
        names = names_of(pieces)
        return names, _Rider([mine[n] for n in names], [gw[n] for n in names], 3 * len(pieces), _ag_ici(pieces, names))

    def landed(names, bufs):
        for n, g in zip(names, bufs):
            gw[n] = g

    def pass_on(pieces, call):
        names = names_of(pieces)
        landed(names, _comm_call([], [gw[n] for n in names], 3 * len(pieces), [_ag_pass_on(pieces, names, [mine[n].shape for n in names])], call))

    onehot = (jnp.arange(4) == chip).astype(F32) * (ic == 0).astype(F32)
    cw_mine = onehot[:, None, None, None] * ssd_conv_w[None]
    cw_full = _allreduce_small(cw_mine.transpose(1, 2, 0, 3).reshape(-1, LANES), "gather_conv_w").reshape(2, SSD_D_CONV, SSD_CONV_DIM)

    def wmm(a, name, layer, *, dn="nn", res=None, call, rider=None):
        return _matmul(a, gw[name], dn=dn, res=res, b_lay=(cut_of[name], layer), name=call, rider=rider)

    h = x[0]
    target = loss_target[0]
    saved = []
    for i in range(DEPTH):
        j = i // 2
        nw = norm_w[i:i + 1]
        pw = ple_norm_w[i:i + 1]
        nxt = layer_pieces(i + 1) if i + 1 < DEPTH else None
        s = dict(h=h)
        u = _rms_fwd(h, nw, f"rms_{i}")
        s["u"] = u
        if i % 2 == 0:
            w_dt = jnp.pad(gw["ssd_in_w"][j, SSD_ZX:], ((0, LANES - SSD_N_HEADS), (0, 0)))
            if nxt:
                riding = layer_pieces(0)[1:] if i == 0 else nxt[1:]
                names, rider = gather_rider(riding)
                pzx, rode = _matmul(u, gw["ssd_in_w"], dn="nt", b_lay=("stack", j, SSD_ZX), name=f"ssd_in_{i}", rider=rider)
                landed(names, rode)
                if i == 0:
                    pass_on(riding, "allgather_pass_0")
            else:
                pzx = _matmul(u, gw["ssd_in_w"], dn="nt", b_lay=("stack", j, SSD_ZX), name=f"ssd_in_{i}")
            pdt = _matmul(u, w_dt, dn="nt", name=f"ssd_indt_{i}")
            act = _conv_fwd(pzx, cw_full[j], ssd_conv_b[j:j + 1], f"conv_{i}")
            dtg = jnp.pad(pdt[:, :SSD_N_HEADS].reshape(-1, SSD_N_GROUPS, 8).transpose(1, 0, 2), ((0, 0), (0, 0), (0, LANES - 8)))
            vecs = (_group_lanes(ssd_dt_bias[j]), _group_lanes(ssd_a_log[j]), _group_lanes(ssd_d[j]))
            if nxt:
                names, rider = gather_rider(nxt[:1])
                (yn, states), rode = _ssd_fwd(act, dtg, *vecs, pzx, ssd_gnorm_w[j:j + 1], f"ssd_{i}", rider=rider)
                landed(names, rode)
            else:
                yn, states = _ssd_fwd(act, dtg, *vecs, pzx, ssd_gnorm_w[j:j + 1], f"ssd_{i}")
            s.update(w_dt=w_dt, pzx=pzx, act=act, dtg=dtg, vecs=vecs, yn=yn, states=states)
            if i == 0:
                names, rider = gather_rider(nxt[1:])
                h1, rode = wmm(yn, "ssd_out_w", j, res=h, call=f"ssd_out_{i}", rider=rider)
                landed(names, rode)
            else:
                h1 = wmm(yn, "ssd_out_w", j, res=h, call=f"ssd_out_{i}")
            if nxt:
                pass_on(nxt, f"allgather_pass_{i + 1}")
        else:
            if nxt:
                names, rider = gather_rider(nxt[2:])
                proj, rode = wmm(u, "sb_in_w", j, call=f"sb_in_{i}", rider=rider)
                landed(names, rode)
            else:
                proj = wmm(u, "sb_in_w", j, call=f"sb_in_{i}")
            qn, kn, vb = _qknorm_fwd(proj, sb_qn_w[j:j + 1], sb_kn_w[j:j + 1], f"qknorm_{i}")
            if nxt:
                names, rider = gather_rider(nxt[:2])
                (og, o, tot), rode = _sb_fwd(qn, kn, vb, proj, f"sb_{i}", rider=rider)
                landed(names, rode)
                pass_on(nxt, f"allgather_pass_{i + 1}")
            else:
                og, o, tot = _sb_fwd(qn, kn, vb, proj, f"sb_{i}")
            s.update(proj=proj, qn=qn, kn=kn, vb=vb, og=og, o=o, tot=tot)
            h1 = wmm(og, "sb_out_w", j, res=h, call=f"sb_out_{i}")
        n2 = _rms_fwd(h1, pw, f"ple_rms_{i}")
        gl = wmm(n2, "ple_gate_w", i, call=f"ple_gate_{i}")
        pp = wmm(p[i, 0], "ple_proj_w", i, call=f"ple_proj_{i}")
        h = _ple_fwd(h1, pp, gl, f"ple_{i}")
        s.update(h1=h1, n2=n2, gl=gl, pp=pp)
        saved.append(s)

    dh, loss_lanes = _loss_bwd(h, target, "loss")

    wg = {}
    gsmall = {n: [None] * s[0] for n, s in _SMALL}
    g_conv_w = [None, None]
    scat = {}
    pending = None

    def wgrad(a, b, name, layer, call):
        wg[(name, layer)] = _matmul(a, b, dn="tn", out_dtype=BF16, o_lay=(cut_of[name], 0, 1), name=call)

    def pair_summed(pieces, tag):
        g_list = [wg[q] for q in pieces]
        return pieces, [_pair_sum(g, r, place, f"rs_pair_sum_{tag}_{k}")
                        for k, (g, r) in enumerate(zip(g_list, _pair_exchange(g_list, f"rs_pair_exchange_{tag}")))]

    def arrived(sent, got):
        for q, pair, g in zip(sent[0], sent[1], got):
            scat[q] = (pair, g)

    for i in reversed(range(DEPTH)):
        j = i // 2
        s = saved[i]
        nw = norm_w[i:i + 1]
        pw = ple_norm_w[i:i + 1]
        rider = _scatter_rider(pending[1]) if pending else None
        dpp, dgl = _ple_bwd(dh, s["pp"], s["gl"], f"ple_bwd_{i}")
        wgrad(p[i, 0], dpp, "ple_proj_w", i, f"d_ple_proj_{i}")
        wgrad(s["n2"], dgl, "ple_gate_w", i, f"d_ple_gate_{i}")
        dn2 = wmm(dgl, "ple_gate_w", i, dn="nt", call=f"ple_gate_bwd_{i}")
        dh1, dpw = _rms_bwd(s["h1"], pw, dn2, dh, f"ple_rms_bwd_{i}")
        gsmall["ple_norm_w"][i] = dpw
        if i % 2 == 0:
            wgrad(s["yn"], dh1, "ssd_out_w", j, f"d_ssd_out_{i}")
            early = pair_summed(layer_pieces(i)[1:], f"{i}_out")
            dyn = wmm(dh1, "ssd_out_w", j, dn="nt", call=f"ssd_out_bwd_{i}")
            outs = _ssd_bwd(s["act"], s["dtg"], *s["vecs"], s["pzx"], ssd_gnorm_w[j:j + 1], s["states"], dyn, f"ssd_bwd_{i}", rider=rider)
            if rider:
                outs, got = outs
                arrived(pending, got)
            dxs, dbm, dcm, ddtg, dbias, dalog, ddsk, dz, dgw = outs
            dzx, dcw, dcb = _conv_bwd(s["pzx"], cw_full[j], ssd_conv_b[j:j + 1], dxs, dbm, dcm, dz, f"conv_bwd_{i}")
            ddt = _pad_lanes(ddtg[:, :, :8].transpose(1, 0, 2).reshape(-1, SSD_N_HEADS)).astype(BF16)
            du, got = _matmul(dzx, gw["ssd_in_w"], b_lay=("stack", j, SSD_ZX), name=f"ssd_in_bwd_{i}", rider=_scatter_rider(early[1]))
            arrived(early, got)
            du = _matmul(ddt, s["w_dt"], res=du, name=f"ssd_indt_bwd_{i}")
            dwt = _matmul(dzx, s["u"], dn="tn", out_dtype=BF16, out_rows=SSD_IN_DIM, name=f"d_ssd_in_{i}")
            dwt_dt = _matmul(ddt, s["u"], dn="tn", out_dtype=BF16, name=f"d_ssd_indt_{i}")
            wg[("ssd_in_w", j)] = lax.dynamic_update_slice(dwt, dwt_dt[:SSD_N_HEADS], (SSD_ZX, 0))[None]
            g_conv_w[j] = dcw
            gsmall["ssd_conv_b"][j] = dcb
            gsmall["ssd_dt_bias"][j] = dbias[:, 0, :8].reshape(1, SSD_N_HEADS)
            gsmall["ssd_a_log"][j] = dalog[:, 0, :8].reshape(1, SSD_N_HEADS)
            gsmall["ssd_d"][j] = ddsk[:, 0, :8].reshape(1, SSD_N_HEADS)
            gsmall["ssd_gnorm_w"][j] = dgw
        else:
            wgrad(s["og"], dh1, "sb_out_w", j, f"d_sb_out_{i}")
            early = pair_summed(layer_pieces(i)[1:], f"{i}_out")
            dog = wmm(dh1, "sb_out_w", j, dn="nt", call=f"sb_out_bwd_{i}")
            outs = _sb_bwd(s["qn"], s["kn"], s["vb"], s["proj"], s["o"], s["tot"], dog, f"sb_bwd_{i}", rider=rider)
            if rider:
                outs, got = outs
                arrived(pending, got)
            dqn, dkn, _, dvb, dg = outs
            dq, dk, dqw, dkw = _qknorm_bwd(s["proj"], sb_qn_w[j:j + 1], sb_kn_w[j:j + 1], dqn, dkn, f"qknorm_bwd_{i}")
            dproj = jnp.concatenate([dq, dk, dvb, dg], axis=1)
            du, got = wmm(dproj, "sb_in_w", j, dn="nt", call=f"sb_in_bwd_{i}", rider=_scatter_rider(early[1]))
            arrived(early, got)
            wgrad(s["u"], dproj, "sb_in_w", j, f"d_sb_in_{i}")
            gsmall["sb_qn_w"][j] = dqw
            gsmall["sb_kn_w"][j] = dkw
        dh, dnw = _rms_bwd(s["h"], nw, du, dh1, f"rms_bwd_{i}")
        gsmall["norm_w"][i] = dnw
        pending = pair_summed(layer_pieces(i)[:1], f"{i}_in")
    grad_x = dh[None]
    arrived(pending, _chip_scatter(pending[1], "rs_chip_scatter_last"))

    halves = []
    for n in big_names:
        buf = None
        for l in range(layers_of[n]):
            buf = _chip_sum(*scat[(n, l)], place, l, layers_of[n], buf, f"rs_chip_sum_{n}_{l}")
        halves.append(buf)
    g_big = dict(zip(big_names, _pair_gather(halves, [cut_of[n] == "stack" for n in big_names])))

    small_parts = [jnp.concatenate(gsmall[n], axis=0).reshape(-1) for n, _ in _SMALL]
    small_parts.append(jnp.stack(g_conv_w).reshape(-1))
    small_parts.append(loss_lanes.reshape(-1))
    small_sum = _allreduce_small(jnp.concatenate(small_parts).reshape(-1, LANES), "allreduce_small").reshape(-1)
    g_small, off = {}, 0
    for n, shape in _SMALL:
        size = math.prod(shape)
        g_small[n] = small_sum[off:off + size].reshape(shape)
        off += size
    cw_size = 2 * SSD_D_CONV * SSD_CONV_DIM
    g_cw_full = small_sum[off:off + cw_size].reshape(2, SSD_D_CONV, 4, SSD_CONV_DIM // 4)
    g_small["ssd_conv_w"] = jnp.sum(g_cw_full * (jnp.arange(4) == chip).astype(F32)[None, None, :, None], axis=2)
    loss = 0.5 * jnp.sum(small_sum[off + cw_size:]) / D_MODEL

    grads, delta, new_m, new_v = {}, {}, {}, {}
    for n in big_names:
        grads[n], delta[n], new_m[n], new_v[n] = (
            held(n, a) for a in (g_big[n], *_adamw(held(n, w_in[n]), g_big[n], held(n, m_in[n]), held(n, v_in[n]), f"adamw_{n}")))
    small_names = [n for n, _ in _SMALL] + ["ssd_conv_w"]
    pack = lambda d: jnp.concatenate([d[n].reshape(-1) for n in small_names]).reshape(1, -1, LANES)
    ds, ms, vs = _adamw(pack(w_in), pack(g_small), pack(m_in), pack(v_in), "adamw_small")
    off = 0
    for n in small_names:
        shape = w_in[n].shape
        size = math.prod(shape)
        grads[n] = g_small[n]
        delta[n] = ds.reshape(-1)[off:off + size].reshape(shape)
        new_m[n] = ms.reshape(-1)[off:off + size].reshape(shape)
        new_v[n] = vs.reshape(-1)[off:off + size].reshape(shape)
        off += size

    order = ["norm_w", "ssd_in_w", "ssd_conv_w", "ssd_conv_b", "ssd_dt_bias", "ssd_a_log", "ssd_d", "ssd_gnorm_w", "ssd_out_w",
             "sb_in_w", "sb_qn_w", "sb_kn_w", "sb_out_w", "ple_norm_w", "ple_gate_w", "ple_proj_w"]
    return (loss, grad_x, *[grads[n] for n in order], *[delta[n] for n in order], *[new_m[n] for n in order],
            *[new_v[n] for n in order])
```

```python
import functools
import math

import jax
import jax.numpy as jnp
from jax import lax
from jax.experimental import pallas as pl
from jax.experimental.pallas import tpu as pltpu

F32 = jnp.float32
BF16 = jnp.bfloat16
MESH = pl.DeviceIdType.MESH

D_MODEL = 2048
DEPTH = 4
SSD_D_INNER = 4096
SSD_N_GROUPS = 8
SSD_GROUP_W = SSD_D_INNER // SSD_N_GROUPS
SSD_D_STATE = 128
SSD_CHUNK = 128
SSD_CONV_DIM = 6144
SSD_D_CONV = 4
SSD_N_HEADS = 64
SB_HEAD_DIM = 128
SB_N_HEADS = 16
SB_WIDTH = 2048
NORM_EPS = 1e-6
GATED_NORM_EPS = 1e-5
ADAM_LR = 0.001
ADAM_B1 = 0.9
ADAM_B2 = 0.999
ADAM_EPS = 1e-08
ADAM_WD = 0.01
ADAM_STEP = 10

SSD_ZX = SSD_D_INNER + SSD_CONV_DIM
SSD_IN_DIM = SSD_ZX + SSD_N_HEADS
LANES = 128
BF16_ROWS = 16

_BIG = (
    ("ssd_in_w", (2, 2576, 2048), "stack"),
    ("ssd_out_w", (2, 1024, 2048), "row"),
    ("sb_in_w", (2, 2048, 2048), "col"),
    ("sb_out_w", (2, 512, 2048), "row"),
    ("ple_gate_w", (4, 512, 2048), "row"),
    ("ple_proj_w", (4, 256, 512), "col"),
)
_SMALL = (
    ("norm_w", (4, 2048)),
    ("ssd_conv_b", (2, 6144)),
    ("ssd_dt_bias", (2, 64)),
    ("ssd_a_log", (2, 64)),
    ("ssd_d", (2, 64)),
    ("ssd_gnorm_w", (2, 4096)),
    ("sb_qn_w", (2, 128)),
    ("sb_kn_w", (2, 128)),
    ("ple_norm_w", (4, 2048)),
)

_DN = {
    "nn": (((1,), (0,)), ((), ())),
    "nt": (((1,), (1,)), ((), ())),
    "tn": (((0,), (0,)), ((), ())),
}


def _dot(a, b, dn="nn"):
    return lax.dot_general(a.astype(BF16), b.astype(BF16), _DN[dn], preferred_element_type=F32)


@functools.partial(jax.custom_vjp, nondiff_argnums=(2,))
def _gdot(a, b, dn):
    return _dot(a, b, dn)


def _gdot_fwd(a, b, dn):
    return _dot(a, b, dn), (a, b)


def _gdot_bwd(dn, res, g):
    a, b = res
    if dn == "nn":
        return _dot(g, b, "nt"), _dot(a, g, "tn")
    if dn == "nt":
        return _dot(g, b, "nn"), _dot(g, a, "tn")
    return _dot(b, g, "nt"), _dot(a, g, "nn")


_gdot.defvjp(_gdot_fwd, _gdot_bwd)


def _split_dot(x, t, parts, x_left):
    acc = None
    r = x
    for i in range(parts):
        p = r.astype(BF16)
        d = lax.dot_general(p, t, _DN["nn"], preferred_element_type=F32) if x_left else lax.dot_general(
            t, p, _DN["nn"], preferred_element_type=F32)
        acc = d if acc is None else acc + d
        if i + 1 < parts:
            r = r - p.astype(F32)
    return acc


def _tri(n, lower, strict=False):
    r = lax.broadcasted_iota(jnp.int32, (n, n), 0)
    c = lax.broadcasted_iota(jnp.int32, (n, n), 1)
    keep = (r > c if strict else r >= c) if lower else (r < c if strict else r <= c)
    return jnp.where(keep, 1.0, 0.0).astype(BF16)


def _cumsum_rows_raw(x):
    return _split_dot(x, _tri(x.shape[0], True), 3, False)


@jax.custom_vjp
def _cumsum_rows(x):
    return _cumsum_rows_raw(x)


def _cumsum_rows_fwd(x):
    return _cumsum_rows_raw(x), None


def _cumsum_rows_bwd(_, g):
    return (_split_dot(g, _tri(g.shape[0], False), 3, False),)


_cumsum_rows.defvjp(_cumsum_rows_fwd, _cumsum_rows_bwd)


def _sigmoid(x):
    return 1.0 / (1.0 + jnp.exp(-x))


def _softplus(x):
    return jnp.maximum(x, 0.0) + jnp.log(1.0 + jnp.exp(-jnp.abs(x)))


def _rms(x, w, eps):
    return x * lax.rsqrt(jnp.mean(x * x, axis=-1, keepdims=True) + eps) * w


_ANY = pl.BlockSpec(memory_space=pl.ANY)


def _params(*sem):
    return pltpu.CompilerParams(dimension_semantics=sem)


class _Rider:
    def __init__(self, reads, writes, n_sems, issue):
        self.reads, self.writes, self.n_sems, self.issue = list(reads), list(writes), n_sems, issue


def _pcall(body, *, grid, in_specs, out_specs, out_shape, args, sem, name, scratch_shapes=(), aliases=None, rider=None):
    aliases = dict(aliases or {})
    if rider is None:
        outs = pl.pallas_call(body, grid=grid, in_specs=in_specs, out_specs=out_specs, out_shape=out_shape,
                              scratch_shapes=list(scratch_shapes), input_output_aliases=aliases,
                              compiler_params=_params(*sem), name=name)(*args)
        return list(outs), []
    n_in, n_out, n_scr, n_rd, n_wr = len(args), len(out_shape), len(scratch_shapes), len(rider.reads), len(rider.writes)
    passed = [k for k, w in enumerate(rider.writes) if not isinstance(w, jax.ShapeDtypeStruct)]
    for pos, k in enumerate(passed):
        aliases[n_in + n_rd + pos] = n_out + k

    def wrapped(*refs):
        ins = refs[:n_in]
        reads = refs[n_in:n_in + n_rd]
        base = n_in + n_rd + len(passed)
        outs = refs[base:base + n_out]
        writes = refs[base + n_out:base + n_out + n_wr]
        scr = refs[base + n_out + n_wr:base + n_out + n_wr + n_scr]
        send, recv = refs[-2:]
        first = last = None
        for d, n in enumerate(grid):
            i = pl.program_id(d)
            first = (i == 0) if first is None else first & (i == 0)
            last = (i == n - 1) if last is None else last & (i == n - 1)

        @pl.when(first)
        def _():
            for cp in rider.issue(reads, writes, send, recv)[0]:
                cp.start()

        body(*ins, *outs, *scr)

        @pl.when(last)
        def _():
            sends, arrivals = rider.issue(reads, writes, send, recv)
            for cp in arrivals:
                cp.wait_recv()
            for cp in sends:
                cp.wait_send()

    outs = pl.pallas_call(
        wrapped, grid=grid,
        in_specs=list(in_specs) + [_ANY] * (n_rd + len(passed)),
        out_specs=list(out_specs) + [_ANY] * n_wr,
        out_shape=list(out_shape) + [jax.ShapeDtypeStruct(w.shape, w.dtype) for w in rider.writes],
        scratch_shapes=list(scratch_shapes) + [pltpu.SemaphoreType.DMA((rider.n_sems,)), pltpu.SemaphoreType.DMA((rider.n_sems,))],
        input_output_aliases=aliases, compiler_params=_params(*(["arbitrary"] * len(grid))), name=name,
    )(*args, *rider.reads, *[rider.writes[k] for k in passed])
    return list(outs[:n_out]), list(outs[n_out:])


MM_TK = 2048


def _pick(dim, pref, unit=None):
    t = pref
    while t >= LANES:
        if dim % t == 0 and (unit is None or unit % t == 0):
            return t
        t //= 2
    return dim


def _matmul(a, b, *, dn="nn", res=None, out_dtype=F32, name, b_lay=None, o_lay=None, o_buf=None, out_rows=None, rider=None):
    if dn == "tn":
        k_dim, m_dim = a.shape
    else:
        m_dim, k_dim = a.shape
    unit_m = unit_n = unit_k = None
    if b_lay is None:
        n_dim = b.shape[0] if dn == "nt" else b.shape[1]
    elif b_lay[0] == "stack":
        cut, layer, rows = b_lay
        cols = b.shape[2]
        n_dim = cols if dn == "nn" else rows
        assert k_dim == (rows if dn == "nn" else cols) and dn != "tn"
    else:
        cut, layer = b_lay
        r, c = b.shape[2:]
        rows, cols = (4 * r, c) if cut == "row" else (r, 4 * c)
        n_dim = cols if dn == "nn" else rows
        assert k_dim == (rows if dn == "nn" else cols) and dn != "tn"
        if (cut == "row") == (dn == "nn"):
            unit_k = r if cut == "row" else c
        else:
            unit_n = r if cut == "row" else c
    if o_lay is not None:
        o_cut, o_layer, o_layers = o_lay
        if o_cut == "row":
            unit_m = m_dim // 4
        else:
            unit_n = n_dim // 4
    tm, tn, tk = _pick(m_dim, 1024, unit_m), _pick(n_dim, 1024, unit_n), _pick(k_dim, MM_TK, unit_k)
    nk = k_dim // tk
    a_spec = pl.BlockSpec((tk, tm), lambda i, j, k: (k, i)) if dn == "tn" else pl.BlockSpec((tm, tk), lambda i, j, k: (i, k))
    if b_lay is None:
        b_spec = pl.BlockSpec((tn, tk), lambda i, j, k: (j, k)) if dn == "nt" else pl.BlockSpec((tk, tn), lambda i, j, k: (k, j))
    elif cut == "stack":
        b_spec = (pl.BlockSpec((None, tk, tn), lambda i, j, k: (layer, k, j)) if dn == "nn" else
                  pl.BlockSpec((None, tn, tk), lambda i, j, k: (layer, j, k)))
    elif dn == "nn" and cut == "row":
        per = r // tk
        b_spec = pl.BlockSpec((None, None, tk, tn), lambda i, j, k: (k // per, layer, k % per, j))
    elif dn == "nn":
        per = c // tn
        b_spec = pl.BlockSpec((None, None, tk, tn), lambda i, j, k: (j // per, layer, k, j % per))
    elif cut == "row":
        per = r // tn
        b_spec = pl.BlockSpec((None, None, tn, tk), lambda i, j, k: (j // per, layer, j % per, k))
    else:
        per = c // tk
        b_spec = pl.BlockSpec((None, None, tn, tk), lambda i, j, k: (k // per, layer, j, k % per))
    r_spec = pl.BlockSpec((tm, tn), lambda i, j, k: (i, j))
    if o_lay is None:
        o_spec = r_spec
        out_shape = jax.ShapeDtypeStruct((out_rows or m_dim, n_dim), out_dtype)
    elif o_cut == "row":
        per_o = unit_m // tm
        o_spec = pl.BlockSpec((None, None, tm, tn), lambda i, j, k: (i // per_o, o_layer, i % per_o, j))
        out_shape = jax.ShapeDtypeStruct((4, o_layers, unit_m, n_dim), out_dtype)
    else:
        per_o = unit_n // tn
        o_spec = pl.BlockSpec((None, None, tm, tn), lambda i, j, k: (j // per_o, o_layer, i, j % per_o))
        out_shape = jax.ShapeDtypeStruct((4, o_layers, m_dim, unit_n), out_dtype)
    has_res = res is not None
    has_buf = o_buf is not None

    def body(*refs):
        a_ref, b_ref = refs[:2]
        r_ref = refs[2] if has_res else None
        o_ref = refs[-1] if nk == 1 else refs[-2]

        def finish(v):
            if has_res:
                v = v + r_ref[...]
            o_ref[...] = v.astype(o_ref.dtype)

        if nk == 1:
            finish(_dot(a_ref[...], b_ref[...], dn))
            return
        acc_ref = refs[-1]
        k = pl.program_id(2)

        @pl.when(k == 0)
        def _():
            acc_ref[...] = jnp.zeros_like(acc_ref)

        acc_ref[...] += _dot(a_ref[...], b_ref[...], dn)

        @pl.when(k == nk - 1)
        def _():
            finish(acc_ref[...])

    args = [a, b] + ([res] if has_res else []) + ([o_buf] if has_buf else [])
    outs, rode = _pcall(
        body, grid=(m_dim // tm, n_dim // tn, nk),
        in_specs=[a_spec, b_spec] + ([r_spec] if has_res else []) + ([_ANY] if has_buf else []),
        out_specs=[o_spec], out_shape=[out_shape],
        scratch_shapes=[] if nk == 1 else [pltpu.VMEM((tm, tn), F32)],
        aliases={len(args) - 1: 0} if has_buf else {},
        args=args, sem=("parallel", "parallel", "arbitrary"), name=name, rider=rider)
    return (outs[0], rode) if rider is not None else outs[0]


def _rowcall(fn, rows, consts, outs, accs, *, name, tm=256):
    args = list(rows) + list(consts)
    in_specs = [pl.BlockSpec((tm, r.shape[1]), lambda i: (i, 0)) for r in rows]
    in_specs += [pl.BlockSpec(c.shape, lambda i: (0, 0)) for c in consts]
    s_dim = args[0].shape[0]
    n_in, n_out = len(args), len(outs)
    out_shape = [jax.ShapeDtypeStruct((s_dim, w), dt) for w, dt in outs] + [jax.ShapeDtypeStruct(s, F32) for s in accs]
    out_specs = [pl.BlockSpec((tm, w), lambda i: (i, 0)) for w, _ in outs] + [pl.BlockSpec(s, lambda i: (0, 0)) for s in accs]

    def body(*refs):
        vals = fn(*[r[...] for r in refs[:n_in]])
        o_refs = refs[n_in:n_in + n_out]
        a_refs = refs[n_in + n_out:]
        for o, v in zip(o_refs, vals[:n_out]):
            o[...] = v.astype(o.dtype)
        if a_refs:
            @pl.when(pl.program_id(0) == 0)
            def _():
                for a_ref in a_refs:
                    a_ref[...] = jnp.zeros_like(a_ref)

            for a_ref, v in zip(a_refs, vals[n_out:]):
                a_ref[...] += v

    return pl.pallas_call(
        body, grid=(s_dim // tm,), in_specs=in_specs, out_specs=out_specs, out_shape=out_shape,
        compiler_params=_params("arbitrary"), name=name,
    )(*args)


def _rms_fwd(h, w, name):
    return _rowcall(lambda x, w_: (_rms(x, w_, NORM_EPS),), [h], [w], [(h.shape[1], BF16)], [], name=name)[0]


def _rms_bwd(h, w, dy, dres, name):
    def fn(x, dy_, dres_, w_):
        _, vjp = jax.vjp(lambda a, b: _rms(a, b, NORM_EPS), x, w_)
        dx, dw = vjp(dy_)
        return dx + dres_, dw

    return _rowcall(fn, [h, dy, dres], [w], [(h.shape[1], F32)], [w.shape], name=name)


def _ple_fwd(h1, pp, gl, name):
    return _rowcall(lambda a, b, c: (a + b * _sigmoid(c),), [h1, pp, gl], [], [(h1.shape[1], F32)], [], name=name)[0]


def _ple_bwd(dh2, pp, gl, name):
    def fn(d, b, c):
        gate = _sigmoid(c)
        return d * gate, d * b * gate * (1.0 - gate)

    return _rowcall(fn, [dh2, pp, gl], [], [(dh2.shape[1], BF16), (dh2.shape[1], BF16)], [], name=name)


def _loss_bwd(y, target, name):
    width = y.shape[1]

    def fn(a, t):
        d = a - t
        col = jnp.sum(d * d, axis=0, keepdims=True)
        part = col[:, 0:LANES]
        for j in range(1, width // LANES):
            part = part + col[:, j * LANES:(j + 1) * LANES]
        return d * (1.0 / width), part

    return _rowcall(fn, [y, target], [], [(width, F32)], [(1, LANES)], name=name)


CONV_TC = 256


def _shift_down(x, j):
    if j == 0:
        return x
    row = lax.broadcasted_iota(jnp.int32, x.shape, 0)
    return jnp.where(row >= j, pltpu.roll(x, j, 0), 0.0)


def _shift_up(x, j):
    if j == 0:
        return x
    n = x.shape[0]
    row = lax.broadcasted_iota(jnp.int32, x.shape, 0)
    return jnp.where(row < n - j, pltpu.roll(x, n - j, 0), 0.0)


def _conv_fwd(pzx, cw, cb, name):
    s_dim = pzx.shape[0]
    off = SSD_D_INNER // CONV_TC

    def body(x_ref, w_ref, b_ref, o_ref):
        x = x_ref[...]
        w = w_ref[...]
        y = b_ref[...] + w[3:4, :] * x
        for k in range(SSD_D_CONV - 1):
            y = y + w[k:k + 1, :] * _shift_down(x, SSD_D_CONV - 1 - k)
        o_ref[...] = y * _sigmoid(y)

    return pl.pallas_call(
        body, grid=(SSD_CONV_DIM // CONV_TC,),
        in_specs=[pl.BlockSpec((s_dim, CONV_TC), lambda j: (0, off + j)), pl.BlockSpec((SSD_D_CONV, CONV_TC), lambda j: (0, j)),
                  pl.BlockSpec((1, CONV_TC), lambda j: (0, j))],
        out_specs=pl.BlockSpec((s_dim, CONV_TC), lambda j: (0, j)),
        out_shape=jax.ShapeDtypeStruct((s_dim, SSD_CONV_DIM), F32),
        compiler_params=_params("parallel"), name=name,
    )(pzx, cw, cb)


def _conv_bwd(pzx, cw, cb, dxs, dbm, dcm, dzx, name):
    s_dim = pzx.shape[0]
    off = SSD_D_INNER // CONV_TC
    n_x, n_b = dxs.shape[1] // CONV_TC, dbm.shape[1] // CONV_TC

    def body(x_ref, w_ref, b_ref, dxs_ref, dbm_ref, dcm_ref, _, dx_ref, dw_ref, db_ref):
        j = pl.program_id(0)
        d = jnp.where(j < n_x, dxs_ref[...], jnp.where(j < n_x + n_b, dbm_ref[...], dcm_ref[...]))
        x = x_ref[...]
        w = w_ref[...]
        xs = [_shift_down(x, SSD_D_CONV - 1 - k) for k in range(SSD_D_CONV)]
        y = b_ref[...]
        for k in range(SSD_D_CONV):
            y = y + w[k:k + 1, :] * xs[k]
        sg = _sigmoid(y)
        dy = d * (sg * (1.0 + y * (1.0 - sg)))
        dx = w[3:4, :] * dy
        for k in range(SSD_D_CONV - 1):
            dx = dx + w[k:k + 1, :] * _shift_up(dy, SSD_D_CONV - 1 - k)
        dx_ref[...] = dx.astype(dx_ref.dtype)
        for k in range(SSD_D_CONV):
            dw_ref[k:k + 1, :] = jnp.sum(dy * xs[k], axis=0, keepdims=True)
        db_ref[...] = jnp.sum(dy, axis=0, keepdims=True)

    part = lambda lo, n: pl.BlockSpec((s_dim, CONV_TC), lambda j: (0, jnp.clip(j - lo, 0, n - 1)))
    return pl.pallas_call(
        body, grid=(SSD_CONV_DIM // CONV_TC,),
        in_specs=[pl.BlockSpec((s_dim, CONV_TC), lambda j: (0, off + j)), pl.BlockSpec((SSD_D_CONV, CONV_TC), lambda j: (0, j)),
                  pl.BlockSpec((1, CONV_TC), lambda j: (0, j)), part(0, n_x), part(n_x, n_b), part(n_x + n_b, n_b), _ANY],
        out_specs=[pl.BlockSpec((s_dim, CONV_TC), lambda j: (0, off + j)), pl.BlockSpec((SSD_D_CONV, CONV_TC), lambda j: (0, j)),
                   pl.BlockSpec((1, CONV_TC), lambda j: (0, j))],
        out_shape=[jax.ShapeDtypeStruct(dzx.shape, dzx.dtype), jax.ShapeDtypeStruct((SSD_D_CONV, SSD_CONV_DIM), F32),
                   jax.ShapeDtypeStruct((1, SSD_CONV_DIM), F32)],
        input_output_aliases={6: 0}, compiler_params=_params("arbitrary"), name=name,
    )(pzx, cw, cb, dxs, dbm, dcm, dzx)


def _ssd_step(xs, bm, cm, dtraw, bias, alog, dskip, st_in, z, gw, dot, cumsum):
    n = xs.shape[0]
    lane = lax.broadcasted_iota(jnp.int32, (1, LANES), 1)
    sub = lax.broadcasted_iota(jnp.int32, (LANES, 1), 0)
    left = (lane < 64).astype(F32)
    right = 1.0 - left
    top = (sub < 64).astype(F32)
    bot = 1.0 - top
    row = lax.broadcasted_iota(jnp.int32, (n, n), 0)
    colm = lax.broadcasted_iota(jnp.int32, (n, n), 1)
    causal = row >= colm

    dt = _softplus(dtraw + bias)
    adt = dt * (-jnp.exp(alog))
    acum = cumsum(adt)
    acum_t = acum.T
    last = jnp.sum(adt, axis=0, keepdims=True)
    scores = dot(cm, bm, "nt")

    def lane_of(v, h):
        return jnp.sum(v * (lane == h).astype(F32), axis=1, keepdims=True)

    ys, sts = [], []
    for pr in range(4):
        heads = (2 * pr, 2 * pr + 1)
        ac = [lane_of(acum, h) for h in heads]
        ar = [jnp.sum(acum_t * (sub == h).astype(F32), axis=0, keepdims=True) for h in heads]
        dth = [lane_of(dt, h) for h in heads]
        la = [lane_of(last, h) for h in heads]
        dk = [lane_of(dskip, h) for h in heads]
        x2 = xs[:, pr * LANES:(pr + 1) * LANES]
        xdt = x2 * (dth[0] * left + dth[1] * right)
        yd = None
        for i, side in enumerate((left, right)):
            decay = jnp.where(causal, jnp.exp(jnp.minimum(ac[i] - ar[i], 0.0)), 0.0)
            t = dot(scores * decay, xdt * side, "nn")
            yd = t if yd is None else yd + t
        st2 = st_in[pr * LANES:(pr + 1) * LANES, :]
        yo = dot(cm, st2, "nt") * (jnp.exp(ac[0]) * left + jnp.exp(ac[1]) * right)
        dte = jnp.exp(la[0] - ac[0]) * left + jnp.exp(la[1] - ac[1]) * right
        cs = dot(xdt * dte, bm, "tn")
        sts.append(st2 * (jnp.exp(la[0]) * top + jnp.exp(la[1]) * bot) + cs)
        ys.append(yd + yo + (dk[0] * left + dk[1] * right) * x2)
    y = jnp.concatenate(ys, axis=1)
    yg = y * (z * _sigmoid(z))
    yn = yg * lax.rsqrt(jnp.mean(yg * yg, axis=-1, keepdims=True) + GATED_NORM_EPS) * gw
    return yn, jnp.concatenate(sts, axis=0)


def _ssd_specs(n_chunks, rev):
    ci = (lambda c: n_chunks - 1 - c) if rev else (lambda c: c)
    n_x = SSD_D_INNER // LANES
    return dict(
        xs=pl.BlockSpec((SSD_CHUNK, SSD_GROUP_W), lambda g, c: (ci(c), g)),
        bm=pl.BlockSpec((SSD_CHUNK, LANES), lambda g, c: (ci(c), n_x + g)),
        cm=pl.BlockSpec((SSD_CHUNK, LANES), lambda g, c: (ci(c), n_x + SSD_N_GROUPS + g)),
        dt=pl.BlockSpec((None, SSD_CHUNK, LANES), lambda g, c: (g, ci(c), 0)),
        vec=pl.BlockSpec((None, 1, LANES), lambda g, c: (g, 0, 0)),
        z=pl.BlockSpec((SSD_CHUNK, SSD_GROUP_W), lambda g, c: (ci(c), g)),
        gw=pl.BlockSpec((1, SSD_GROUP_W), lambda g, c: (0, g)),
        st=pl.BlockSpec((None, None, SSD_GROUP_W, SSD_D_STATE), lambda g, c: (g, ci(c), 0, 0)),
    )


def _ssd_fwd(act, dtg, bias, alog, dskip, pzx, gw, name, rider=None):
    s_dim = act.shape[0]
    n_chunks = s_dim // SSD_CHUNK
    sp = _ssd_specs(n_chunks, False)

    def body(xs, bm, cm, dt, b_ref, a_ref, d_ref, z, gw_ref, yn_ref, st_ref, state):
        @pl.when(pl.program_id(1) == 0)
        def _():
            state[...] = jnp.zeros_like(state)

        st_in = state[...]
        st_ref[...] = st_in
        yn, st_out = _ssd_step(xs[...], bm[...], cm[...], dt[...], b_ref[...], a_ref[...], d_ref[...], st_in, z[...], gw_ref[...],
                               _dot, _cumsum_rows_raw)
        yn_ref[...] = yn.astype(yn_ref.dtype)
        state[...] = st_out

    outs, rode = _pcall(
        body, grid=(SSD_N_GROUPS, n_chunks),
        in_specs=[sp["xs"], sp["bm"], sp["cm"], sp["dt"], sp["vec"], sp["vec"], sp["vec"], sp["z"], sp["gw"]],
        out_specs=[sp["xs"], sp["st"]],
        out_shape=[jax.ShapeDtypeStruct((s_dim, SSD_D_INNER), BF16),
                   jax.ShapeDtypeStruct((SSD_N_GROUPS, n_chunks, SSD_GROUP_W, SSD_D_STATE), F32)],
        scratch_shapes=[pltpu.VMEM((SSD_GROUP_W, SSD_D_STATE), F32)],
        args=[act, act, act, dtg, bias, alog, dskip, pzx, gw], sem=("parallel", "arbitrary"), name=name, rider=rider)
    return (outs, rode) if rider is not None else outs


def _ssd_bwd(act, dtg, bias, alog, dskip, pzx, gw, states, dyn, name, rider=None):
    s_dim = act.shape[0]
    n_chunks = s_dim // SSD_CHUNK
    sp = _ssd_specs(n_chunks, True)
    rc = lambda c: n_chunks - 1 - c

    def body(xs, bm, cm, dt, b_ref, a_ref, d_ref, z, gw_ref, st_ref, dyn_ref,
             dxs_ref, dbm_ref, dcm_ref, ddt_ref, db_ref, da_ref, dd_ref, dz_ref, dgw_ref, dstate):
        first = pl.program_id(1) == 0

        @pl.when(first)
        def _():
            dstate[...] = jnp.zeros_like(dstate)
            db_ref[...] = jnp.zeros_like(db_ref)
            da_ref[...] = jnp.zeros_like(da_ref)
            dd_ref[...] = jnp.zeros_like(dd_ref)
            dgw_ref[...] = jnp.zeros_like(dgw_ref)

        fn = functools.partial(_ssd_step, dot=_gdot, cumsum=_cumsum_rows)
        _, vjp = jax.vjp(fn, xs[...], bm[...], cm[...], dt[...], b_ref[...], a_ref[...], d_ref[...], st_ref[...], z[...], gw_ref[...])
        dxs, dbm, dcm, ddt, db, da, dd, dst, dz, dgw = vjp((dyn_ref[...], dstate[...]))
        dxs_ref[...] = dxs
        dbm_ref[...] = dbm
        dcm_ref[...] = dcm
        ddt_ref[...] = ddt
        dz_ref[...] = dz.astype(dz_ref.dtype)
        db_ref[...] += db
        da_ref[...] += da
        dd_ref[...] += dd
        dgw_ref[...] += dgw
        dstate[...] = dst

    bc = pl.BlockSpec((SSD_CHUNK, LANES), lambda g, c: (rc(c), g))
    outs, rode = _pcall(
        body, grid=(SSD_N_GROUPS, n_chunks),
        in_specs=[sp["xs"], sp["bm"], sp["cm"], sp["dt"], sp["vec"], sp["vec"], sp["vec"], sp["z"], sp["gw"], sp["st"], sp["xs"]],
        out_specs=[sp["xs"], bc, bc, sp["dt"], sp["vec"], sp["vec"], sp["vec"], sp["xs"], sp["gw"]],
        out_shape=[jax.ShapeDtypeStruct((s_dim, SSD_D_INNER), F32),
                   jax.ShapeDtypeStruct((s_dim, SSD_N_GROUPS * SSD_D_STATE), F32),
                   jax.ShapeDtypeStruct((s_dim, SSD_N_GROUPS * SSD_D_STATE), F32),
                   jax.ShapeDtypeStruct((SSD_N_GROUPS, s_dim, LANES), F32),
                   jax.ShapeDtypeStruct((SSD_N_GROUPS, 1, LANES), F32),
                   jax.ShapeDtypeStruct((SSD_N_GROUPS, 1, LANES), F32),
                   jax.ShapeDtypeStruct((SSD_N_GROUPS, 1, LANES), F32),
                   jax.ShapeDtypeStruct((s_dim, SSD_ZX), BF16),
                   jax.ShapeDtypeStruct((1, SSD_D_INNER), F32)],
        scratch_shapes=[pltpu.VMEM((SSD_GROUP_W, SSD_D_STATE), F32)],
        args=[act, act, act, dtg, bias, alog, dskip, pzx, gw, states, dyn], sem=("arbitrary", "arbitrary"), name=name, rider=rider)
    return (outs, rode) if rider is not None else outs


SB_T = 128
SB_GROUP = 8
SB_WIDE = SB_GROUP * SB_T
SB_HB = 4
SB_SCALE = 1.0 / math.sqrt(SB_HEAD_DIM)


def _qknorm_fwd(proj, qw, kw, name, tm=512):
    s_dim = proj.shape[0]

    def body(q_ref, k_ref, v_ref, qw_ref, kw_ref, qo, ko, vo):
        qo[...] = _rms(q_ref[...], qw_ref[...], NORM_EPS).astype(BF16)
        ko[...] = _rms(k_ref[...], kw_ref[...], NORM_EPS).astype(BF16)
        vo[...] = v_ref[...].astype(BF16)

    blk = lambda o: pl.BlockSpec((tm, SB_HEAD_DIM), lambda i, h: (i, o + h))
    vec = pl.BlockSpec((1, SB_HEAD_DIM), lambda i, h: (0, 0))
    return pl.pallas_call(
        body, grid=(s_dim // tm, SB_N_HEADS),
        in_specs=[blk(0), blk(SB_N_HEADS), blk(2 * SB_N_HEADS), vec, vec],
        out_specs=[blk(0)] * 3,
        out_shape=[jax.ShapeDtypeStruct((s_dim, SB_WIDTH), BF16)] * 3,
        compiler_params=_params("parallel", "parallel"), name=name,
    )(proj, proj, proj, qw, kw)


def _qknorm_bwd(proj, qw, kw, dqn, dkn, name, tm=512):
    s_dim = proj.shape[0]

    def body(q_ref, k_ref, dq_ref, dk_ref, qw_ref, kw_ref, dqo, dko, dqw, dkw):
        @pl.when((pl.program_id(0) == 0) & (pl.program_id(1) == 0))
        def _():
            dqw[...] = jnp.zeros_like(dqw)
            dkw[...] = jnp.zeros_like(dkw)

        fn = lambda a, b: _rms(a, b, NORM_EPS)
        _, vq = jax.vjp(fn, q_ref[...], qw_ref[...])
        dq, dw = vq(dq_ref[...])
        dqo[...] = dq.astype(BF16)
        dqw[...] += dw
        _, vk = jax.vjp(fn, k_ref[...], kw_ref[...])
        dk, dw = vk(dk_ref[...])
        dko[...] = dk.astype(BF16)
        dkw[...] += dw

    blk = lambda o: pl.BlockSpec((tm, SB_HEAD_DIM), lambda i, h: (i, o + h))
    vec = pl.BlockSpec((1, SB_HEAD_DIM), lambda i, h: (0, 0))
    return pl.pallas_call(
        body, grid=(s_dim // tm, SB_N_HEADS),
        in_specs=[blk(0), blk(SB_N_HEADS), blk(0), blk(0), vec, vec],
        out_specs=[blk(0), blk(0), vec, vec],
        out_shape=[jax.ShapeDtypeStruct((s_dim, SB_WIDTH), BF16)] * 2 + [jax.ShapeDtypeStruct((1, SB_HEAD_DIM), F32)] * 2,
        compiler_params=_params("arbitrary", "arbitrary"), name=name,
    )(proj, proj, dqn, dkn, qw, kw)


def _sb_logits(q, k, strict):
    z = _dot(q, k, "nt") * SB_SCALE
    lb = jnp.minimum(z, 0.0) - jnp.log(1.0 + jnp.exp(-jnp.abs(z)))
    lm = lb - z
    if strict is not None:
        lm = jnp.where(strict, lm, 0.0)
    return lb, lm


def _sb_strict(qi, grp):
    r = lax.broadcasted_iota(jnp.int32, (SB_T, SB_WIDE), 0) + qi * SB_T
    c = lax.broadcasted_iota(jnp.int32, (SB_T, SB_WIDE), 1) + grp * SB_WIDE
    return c < r


def _head_lanes(hh):
    return slice(hh * SB_HEAD_DIM, (hh + 1) * SB_HEAD_DIM)


def _sb_fwd(qn, kn, vb, proj, name, rider=None):
    s_dim = qn.shape[0]
    nq = s_dim // SB_T
    assert nq % SB_GROUP == 0

    def body(q_ref, k_ref, v_ref, g_ref, og_ref, o_ref, t_ref):
        qi = pl.program_id(1)
        top = qi // SB_GROUP
        after = _tri(SB_T, True, strict=True)
        qs = [q_ref[:, _head_lanes(hh)] for hh in range(SB_HB)]

        def step(grp, masked, carries):
            start = pl.multiple_of(grp * SB_WIDE, SB_WIDE)
            strict = _sb_strict(qi, grp) if masked else None
            out = []
            for hh in range(SB_HB):
                o_acc, cr = carries[hh]
                k = k_ref[pl.ds(start, SB_WIDE), _head_lanes(hh)]
                v = v_ref[pl.ds(start, SB_WIDE), _head_lanes(hh)]
                lb, lm = _sb_logits(qs[hh], k, strict)
                rest = [None] * SB_GROUP
                for t in reversed(range(SB_GROUP)):
                    lm_t = lm[:, t * SB_T:(t + 1) * SB_T]
                    rest[t] = cr + _split_dot(lm_t, after, 2, True)
                    cr = cr + jnp.sum(lm_t, axis=1, keepdims=True)
                a = jnp.exp(lb + jnp.concatenate(rest, axis=1))
                if masked:
                    a = jnp.where(strict, a, 0.0)
                out.append((o_acc + _dot(a, v), cr))
            return tuple(out)

        init = tuple((jnp.zeros((SB_T, SB_HEAD_DIM), F32), jnp.zeros((SB_T, 1), F32)) for _ in range(SB_HB))
        carries = step(top, True, init)
        carries = lax.fori_loop(0, top, lambda i, c: step(top - 1 - i, False, c), carries)
        for hh in range(SB_HB):
            o, tot = carries[hh]
            g = g_ref[:, _head_lanes(hh)]
            o_ref[:, _head_lanes(hh)] = o
            og_ref[:, _head_lanes(hh)] = (o * (g * _sigmoid(g))).astype(og_ref.dtype)
            t_ref[hh] = jnp.broadcast_to(tot, (SB_T, LANES))

    wide = SB_HB * SB_HEAD_DIM
    qb = pl.BlockSpec((SB_T, wide), lambda h, i: (i, h))
    kv = pl.BlockSpec((s_dim, wide), lambda h, i: (0, h))
    outs, rode = _pcall(
        body, grid=(SB_N_HEADS // SB_HB, nq),
        in_specs=[qb, kv, kv, pl.BlockSpec((SB_T, wide), lambda h, i: (i, 3 * SB_N_HEADS // SB_HB + h))],
        out_specs=[qb, qb, pl.BlockSpec((SB_HB, SB_T, LANES), lambda h, i: (h, i, 0))],
        out_shape=[jax.ShapeDtypeStruct((s_dim, SB_WIDTH), BF16), jax.ShapeDtypeStruct((s_dim, SB_WIDTH), F32),
                   jax.ShapeDtypeStruct((SB_N_HEADS, s_dim, LANES), F32)],
        args=[qn, kn, vb, proj], sem=("parallel", "arbitrary"), name=name, rider=rider)
    return (outs, rode) if rider is not None else outs


def _sb_bwd(qn, kn, vb, proj, o, tot, dog, name, rider=None):
    s_dim = qn.shape[0]
    nq = s_dim // SB_T
    assert nq % SB_GROUP == 0

    def body(q_ref, k_ref, v_ref, g_ref, o_ref, t_ref, dog_ref, dq_ref, dk_ref, dv_ref, dvb_ref, dg_ref):
        qi = pl.program_id(1)
        top = qi // SB_GROUP

        @pl.when(qi == 0)
        def _():
            dk_ref[...] = jnp.zeros_like(dk_ref)
            dv_ref[...] = jnp.zeros_like(dv_ref)

        after = _tri(SB_T, True, strict=True)
        before = _tri(SB_T, False, strict=True)
        qs, dos, totals = [], [], []
        for hh in range(SB_HB):
            g = g_ref[:, _head_lanes(hh)]
            sg = _sigmoid(g)
            dog_v = dog_ref[:, _head_lanes(hh)]
            dg_ref[:, _head_lanes(hh)] = (dog_v * o_ref[:, _head_lanes(hh)] * (sg * (1.0 + g * (1.0 - sg)))).astype(dg_ref.dtype)
            dos.append((dog_v * (g * sg)).astype(BF16))
            qs.append(q_ref[:, _head_lanes(hh)])
            totals.append(t_ref[hh][:, 0:1])

        def step(grp, masked, carries):
            start = pl.multiple_of(grp * SB_WIDE, SB_WIDE)
            strict = _sb_strict(qi, grp) if masked else None
            out = []
            for hh in range(SB_HB):
                dq_acc, cp, ce = carries[hh]
                q, do = qs[hh], dos[hh]
                k = k_ref[pl.ds(start, SB_WIDE), _head_lanes(hh)]
                v = v_ref[pl.ds(start, SB_WIDE), _head_lanes(hh)]
                lb, lm = _sb_logits(q, k, strict)
                rest = []
                for t in range(SB_GROUP):
                    lm_t = lm[:, t * SB_T:(t + 1) * SB_T]
                    cp = cp + jnp.sum(lm_t, axis=1, keepdims=True)
                    rest.append((totals[hh] - cp) + _split_dot(lm_t, after, 2, True))
                a = jnp.exp(lb + jnp.concatenate(rest, axis=1))
                if masked:
                    a = jnp.where(strict, a, 0.0)
                e = a * _dot(do, v, "nt")
                excl = []
                for t in range(SB_GROUP):
                    e_t = e[:, t * SB_T:(t + 1) * SB_T]
                    excl.append(ce + _split_dot(e_t, before, 2, True))
                    ce = ce + jnp.sum(e_t, axis=1, keepdims=True)
                eex = jnp.concatenate(excl, axis=1)
                if masked:
                    eex = jnp.where(strict, eex, 0.0)
                sig = jnp.exp(lb)
                dz = (e * (1.0 - sig) - eex * sig) * SB_SCALE
                dv_ref[pl.ds(start, SB_WIDE), _head_lanes(hh)] += _dot(a, do, "tn")
                dk_ref[pl.ds(start, SB_WIDE), _head_lanes(hh)] += _dot(dz, q, "tn")
                out.append((dq_acc + _dot(dz, k), cp, ce))
            return tuple(out)

        zero = jnp.zeros((SB_T, 1), F32)
        init = tuple((jnp.zeros((SB_T, SB_HEAD_DIM), F32), zero, zero) for _ in range(SB_HB))
        carries = lax.fori_loop(0, top, lambda i, c: step(i, False, c), init)
        carries = step(top, True, carries)
        for hh in range(SB_HB):
            dq_ref[:, _head_lanes(hh)] = carries[hh][0]

        @pl.when(qi == nq - 1)
        def _():
            dvb_ref[...] = dv_ref[...].astype(BF16)

    wide = SB_HB * SB_HEAD_DIM
    qb = pl.BlockSpec((SB_T, wide), lambda h, i: (i, h))
    kv = pl.BlockSpec((s_dim, wide), lambda h, i: (0, h))
    outs, rode = _pcall(
        body, grid=(SB_N_HEADS // SB_HB, nq),
        in_specs=[qb, kv, kv, pl.BlockSpec((SB_T, wide), lambda h, i: (i, 3 * SB_N_HEADS // SB_HB + h)), qb,
                  pl.BlockSpec((SB_HB, SB_T, LANES), lambda h, i: (h, i, 0)), qb],
        out_specs=[qb, kv, kv, kv, qb],
        out_shape=[jax.ShapeDtypeStruct((s_dim, SB_WIDTH), F32), jax.ShapeDtypeStruct((s_dim, SB_WIDTH), F32),
                   jax.ShapeDtypeStruct((s_dim, SB_WIDTH), F32), jax.ShapeDtypeStruct((s_dim, SB_WIDTH), BF16),
                   jax.ShapeDtypeStruct((s_dim, SB_WIDTH), BF16)],
        args=[qn, kn, vb, proj, o, tot, dog], sem=("parallel", "arbitrary"), name=name, rider=rider)
    return (outs, rode) if rider is not None else outs


def _adamw_math(w, g, m, v):
    m = ADAM_B1 * m + (1.0 - ADAM_B1) * g
    v = ADAM_B2 * v + (1.0 - ADAM_B2) * (g * g)
    m_hat = m / (1.0 - ADAM_B1 ** ADAM_STEP)
    v_hat = v / (1.0 - ADAM_B2 ** ADAM_STEP)
    delta = -ADAM_LR * (m_hat / (jnp.sqrt(v_hat) + ADAM_EPS) + ADAM_WD * w)
    return delta, m, v


def _row_block(rows, cols, itemsize=4, limit=1 << 20):
    tr = rows
    while tr * cols * itemsize > limit and tr % (2 * BF16_ROWS) == 0:
        tr //= 2
    return tr


def _divisor_block(rows, cols, itemsize=4, limit=2 << 20):
    best = BF16_ROWS
    for t in range(BF16_ROWS, rows + 1, BF16_ROWS):
        if rows % t == 0 and t * cols * itemsize <= limit:
            best = t
    return best


def _adamw(w, g, m, v, name):
    n, rows, cols = w.shape
    tr = rows if rows * cols * 4 <= (1 << 20) else _divisor_block(rows, cols, limit=1 << 20)

    def body(w_ref, g_ref, m_ref, v_ref, d_out, m_out, v_out):
        d, m_new, v_new = _adamw_math(w_ref[...], g_ref[...], m_ref[...], v_ref[...])
        d_out[...] = d
        m_out[...] = m_new
        v_out[...] = v_new

    blk = pl.BlockSpec((None, tr, cols), lambda i, j: (i, j, 0))
    return pl.pallas_call(
        body, grid=(n, rows // tr), in_specs=[blk] * 4, out_specs=[blk] * 3,
        out_shape=[jax.ShapeDtypeStruct(w.shape, F32)] * 3,
        compiler_params=_params("parallel", "parallel"), name=name,
    )(w, g, m, v)


_FLIPS = ((1, 0), (0, 1), (1, 1))


def _place():
    return lax.axis_index("x"), lax.axis_index("y"), lax.axis_index("c")


def _flip(v, f):
    return 1 - v if f else v


def _half_rows(ref, lead, hc, hr):
    return ref.at[(*lead, pl.ds(pl.multiple_of(hc * hr, BF16_ROWS), hr), slice(None))]


def _half_cols(ref, lead, hc, hw):
    return ref.at[(*lead, pl.ds(pl.multiple_of(hc * hw, LANES), hw))]


def _rows_of_chip(chip, r):
    return pl.ds(pl.multiple_of(chip * r, BF16_ROWS), r)


def _slot_half(gathered, shard_shape, chip, l, hc):
    r, c = shard_shape[1:]
    if len(gathered.shape) == 3:
        return _half_cols(gathered, (l, _rows_of_chip(chip, r)), hc, c // 2)
    return _half_rows(gathered, (chip, l), hc, r // 2)


def _shard_half(shard, stacked, l, hc):
    r, c = shard.shape[1:]
    return _half_cols(shard, (l, slice(None)), hc, c // 2) if stacked else _half_rows(shard, (l,), hc, r // 2)


def _remote(src, dst, send, recv, k, to):
    return pltpu.make_async_remote_copy(src_ref=src, dst_ref=dst, send_sem=send.at[k], recv_sem=recv.at[k], device_id=to,
                                        device_id_type=MESH)


def _comm_call(reads, writes, n_sems, phases, name):
    passed = [k for k, w in enumerate(writes) if not isinstance(w, jax.ShapeDtypeStruct)]
    n_rd = len(reads)

    def body(*refs):
        rd = refs[:n_rd]
        wr = refs[n_rd + len(passed):n_rd + len(passed) + len(writes)]
        send, recv = refs[-2:]
        for phase in phases:
            sends, arrivals = phase(rd, wr, send, recv)
            for cp in sends:
                cp.start()
            for cp in arrivals:
                cp.wait_recv()
            for cp in sends:
                cp.wait_send()

    return pl.pallas_call(
        body, in_specs=[_ANY] * (n_rd + len(passed)), out_specs=[_ANY] * len(writes),
        out_shape=[jax.ShapeDtypeStruct(w.shape, w.dtype) for w in writes],
        input_output_aliases={n_rd + pos: k for pos, k in enumerate(passed)},
        scratch_shapes=[pltpu.SemaphoreType.DMA((n_sems,)), pltpu.SemaphoreType.DMA((n_sems,))], name=name,
    )(*reads, *[writes[k] for k in passed])


def _ag_ici(pieces, names, base=0):
    def phase(shards, gathered, send, recv):
        x, y, c = _place()
        me = 2 * x + y
        sends, arrivals = [], []
        for k, (n, l) in enumerate(pieces):
            a = names.index(n)
            shape = shards[a].shape
            src = _shard_half(shards[a], len(gathered[a].shape) == 3, l, c)
            for j, (fx, fy) in enumerate(_FLIPS):
                tx, ty = _flip(x, fx), _flip(y, fy)
                sends.append(_remote(src, _slot_half(gathered[a], shape, me, l, c), send, recv, base + 3 * k + j, (tx, ty, c)))
                arrivals.append(_remote(src, _slot_half(gathered[a], shape, 2 * tx + ty, l, c), send, recv, base + 3 * k + j, (tx, ty, c)))
        return sends, arrivals

    return phase


def _ag_pass_on(pieces, names, shapes, base=0):
    def phase(_, gathered, send, recv):
        x, y, c = _place()
        sibling = (x, y, 1 - c)
        sends, arrivals = [], []
        for k, (n, l) in enumerate(pieces):
            a = names.index(n)
            for j, (fx, fy) in enumerate(_FLIPS):
                chip = 2 * _flip(x, fx) + _flip(y, fy)
                landed = _slot_half(gathered[a], shapes[a], chip, l, c)
                sends.append(_remote(landed, landed, send, recv, base + 3 * k + j, sibling))
                arrivals.append(_remote(landed, _slot_half(gathered[a], shapes[a], chip, l, 1 - c), send, recv, base + 3 * k + j, sibling))
        return sends, arrivals

    return phase


def _other_half(ref, hc):
    if len(ref.shape) == 3:
        return _half_cols(ref, (slice(None), slice(None)), hc, ref.shape[2] // 2)
    return _half_rows(ref, (slice(None), slice(None)), hc, ref.shape[2] // 2)


def _half_shape(shape):
    return shape[:2] + (shape[2] // 2,) if len(shape) == 3 else shape[:2] + (shape[2] // 2, shape[3])


def _exchange_phase(n_arr):
    def phase(ins, outs, send, recv):
        x, y, c = _place()
        cps = [_remote(_other_half(ins[a], 1 - c), outs[a], send, recv, a, (x, y, 1 - c)) for a in range(n_arr)]
        return cps, cps

    return phase


def _exchange_outs(grads):
    return [jax.ShapeDtypeStruct(_half_shape(g.shape), g.dtype) for g in grads]


def _pair_exchange(grads, name):
    return _comm_call(grads, _exchange_outs(grads), len(grads), [_exchange_phase(len(grads))], name)


def _exchange_rider(grads):
    return _Rider(grads, _exchange_outs(grads), len(grads), _exchange_phase(len(grads)))


def _pair_sum_stacked(g, got, place, name):
    _, rows, hw = got.shape
    tr = _divisor_block(rows, hw)

    def body(place_ref, g_ref, r_ref, o_ref):
        o_ref[...] = (g_ref[...].astype(F32) + r_ref[...].astype(F32)).astype(o_ref.dtype)

    blk = pl.BlockSpec((None, tr, hw), lambda i, pr: (0, i, 0))
    return pl.pallas_call(
        body,
        grid_spec=pltpu.PrefetchScalarGridSpec(
            num_scalar_prefetch=1, grid=(rows // tr,),
            in_specs=[pl.BlockSpec((None, tr, hw), lambda i, pr: (0, i, pr[1])), blk], out_specs=blk),
        out_shape=jax.ShapeDtypeStruct(got.shape, BF16),
        compiler_params=_params("parallel"), name=name,
    )(place, g, got)


def _pair_sum(g, got, place, name):
    if len(g.shape) == 3:
        return _pair_sum_stacked(g, got, place, name)
    _, layers, hr, cols = got.shape
    tr = _row_block(hr, cols)
    per = hr // tr

    def body(place_ref, g_ref, r_ref, o_ref):
        o_ref[...] = (g_ref[...].astype(F32) + r_ref[...].astype(F32)).astype(o_ref.dtype)

    blk = pl.BlockSpec((None, None, tr, cols), lambda k, l, i, pr: (k, l, i, 0))
    return pl.pallas_call(
        body,
        grid_spec=pltpu.PrefetchScalarGridSpec(
            num_scalar_prefetch=1, grid=(4, layers, per),
            in_specs=[pl.BlockSpec((None, None, tr, cols), lambda k, l, i, pr: (k, l, pr[1] * per + i, 0)), blk],
            out_specs=blk),
        out_shape=jax.ShapeDtypeStruct(got.shape, BF16),
        compiler_params=_params("parallel", "parallel", "parallel"), name=name,
    )(place, g, got)


def _scatter_phase(n_arr):
    def phase(ins, outs, send, recv):
        x, y, c = _place()
        cps = []
        for a in range(n_arr):
            for j, (fx, fy) in enumerate(_FLIPS):
                tx, ty = _flip(x, fx), _flip(y, fy)
                if len(ins[a].shape) == 3:
                    src = ins[a].at[:, _rows_of_chip(2 * tx + ty, ins[a].shape[1] // 4), :]
                else:
                    src = ins[a].at[2 * tx + ty]
                cps.append(_remote(src, outs[a].at[j], send, recv, 3 * a + j, (tx, ty, c)))
        return cps, cps

    return phase


def _scatter_outs(pairs):
    return [jax.ShapeDtypeStruct((3, 1, p.shape[1] // 4, p.shape[2]) if len(p.shape) == 3 else (3,) + p.shape[1:], p.dtype) for p in pairs]


def _chip_scatter(pairs, name):
    return _comm_call(pairs, _scatter_outs(pairs), 3 * len(pairs), [_scatter_phase(len(pairs))], name)


def _scatter_rider(pairs):
    return _Rider(pairs, _scatter_outs(pairs), 3 * len(pairs), _scatter_phase(len(pairs)))


def _chip_sum_stacked(p, got, place, layer, layers, o_buf, name):
    _, r, hw = got.shape[1:]
    tr = _divisor_block(r, hw)
    per = r // tr

    def body(place_ref, p_ref, r_ref, *rest):
        o_ref = rest[-1]
        acc = p_ref[...].astype(F32)
        for j in range(3):
            acc = acc + r_ref[j].astype(F32)
        o_ref[...] = acc

    has_buf = o_buf is not None
    return pl.pallas_call(
        body,
        grid_spec=pltpu.PrefetchScalarGridSpec(
            num_scalar_prefetch=1, grid=(per,),
            in_specs=[pl.BlockSpec((None, tr, hw), lambda i, pr: (0, pr[0] * per + i, 0)),
                      pl.BlockSpec((3, None, tr, hw), lambda i, pr: (0, 0, i, 0))] + ([_ANY] if has_buf else []),
            out_specs=pl.BlockSpec((None, tr, hw), lambda i, pr: (layer, i, pr[1]))),
        out_shape=jax.ShapeDtypeStruct((layers, r, 2 * hw), F32),
        input_output_aliases={3: 0} if has_buf else {},
        compiler_params=_params("parallel"), name=name,
    )(*((place, p, got) + ((o_buf,) if has_buf else ())))


def _chip_sum(p, got, place, layer, layers, o_buf, name):
    if len(p.shape) == 3:
        return _chip_sum_stacked(p, got, place, layer, layers, o_buf, name)
    _, _, hr, cols = p.shape
    tr = _row_block(hr, cols)
    per = hr // tr

    def body(place_ref, p_ref, r_ref, *rest):
        o_ref = rest[-1]
        acc = p_ref[...].astype(F32)
        for j in range(3):
            acc = acc + r_ref[j].astype(F32)
        o_ref[...] = acc

    has_buf = o_buf is not None
    return pl.pallas_call(
        body,
        grid_spec=pltpu.PrefetchScalarGridSpec(
            num_scalar_prefetch=1, grid=(per,),
            in_specs=[pl.BlockSpec((None, None, tr, cols), lambda i, pr: (pr[0], 0, i, 0)),
                      pl.BlockSpec((3, None, tr, cols), lambda i, pr: (0, 0, i, 0))] + ([_ANY] if has_buf else []),
            out_specs=pl.BlockSpec((None, tr, cols), lambda i, pr: (layer, pr[1] * per + i, 0))),
        out_shape=jax.ShapeDtypeStruct((layers, 2 * hr, cols), F32),
        input_output_aliases={3: 0} if has_buf else {},
        compiler_params=_params("parallel"), name=name,
    )(*((place, p, got) + ((o_buf,) if has_buf else ())))


def _pair_gather(halves, by_cols):
    def phase(_, bufs, send, recv):
        x, y, c = _place()
        sends, arrivals = [], []
        for a, h in enumerate(halves):
            cut = (lambda hc, a=a, h=h: _half_cols(bufs[a], (slice(None), slice(None)), hc, h.shape[2] // 2)) if by_cols[a] else (
                lambda hc, a=a, h=h: _half_rows(bufs[a], (slice(None),), hc, h.shape[1] // 2))
            sends.append(_remote(cut(c), cut(c), send, recv, a, (x, y, 1 - c)))
            arrivals.append(_remote(cut(c), cut(1 - c), send, recv, a, (x, y, 1 - c)))
        return sends, arrivals

    return _comm_call([], halves, len(halves), [phase], "rs_pair_gather")


def _allreduce_small(v, name):
    rows, cols = v.shape

    def body(v_ref, o_ref, buf, send_sems, recv_sems):
        x, y, c = _place()
        me = 4 * x + 2 * y + c
        buf[0] = v_ref[...]
        cps = []
        for k in range(1, 8):
            kx, ky, kc = (k >> 2) & 1, (k >> 1) & 1, k & 1
            cp = pltpu.make_async_remote_copy(src_ref=v_ref, dst_ref=buf.at[k], send_sem=send_sems.at[k - 1], recv_sem=recv_sems.at[k - 1],
                                              device_id=(_flip(x, kx), _flip(y, ky), _flip(c, kc)), device_id_type=MESH)
            cp.start()
            cps.append(cp)
        for cp in cps:
            cp.wait()
        acc = buf[me]
        for d in range(1, 8):
            acc = acc + buf[jnp.bitwise_xor(d, me)]
        o_ref[...] = acc

    vm = pl.BlockSpec(memory_space=pltpu.VMEM)
    return pl.pallas_call(
        body, in_specs=[vm], out_specs=vm, out_shape=jax.ShapeDtypeStruct((rows, cols), F32),
        scratch_shapes=[pltpu.VMEM((8, rows, cols), F32), pltpu.SemaphoreType.DMA((7,)), pltpu.SemaphoreType.DMA((7,))],
        name=name,
    )(v)


def _pad_lanes(a):
    return jnp.pad(a, ((0, 0), (0, LANES - a.shape[1])))


def _group_lanes(v):
    return jnp.pad(v.reshape(SSD_N_GROUPS, 1, 8), ((0, 0), (0, 0), (0, LANES - 8)))


def kernel(x, p, norm_w, ssd_in_w, ssd_conv_w, ssd_conv_b, ssd_dt_bias, ssd_a_log, ssd_d, ssd_gnorm_w, ssd_out_w, sb_in_w, sb_qn_w, sb_kn_w, sb_out_w, ple_norm_w, ple_gate_w, ple_proj_w, loss_target, m_norm_w, m_ssd_in_w, m_ssd_conv_w, m_ssd_conv_b, m_ssd_dt_bias, m_ssd_a_log, m_ssd_d, m_ssd_gnorm_w, m_ssd_out_w, m_sb_in_w, m_sb_qn_w, m_sb_kn_w, m_sb_out_w, m_ple_norm_w, m_ple_gate_w, m_ple_proj_w, v_norm_w, v_ssd_in_w, v_ssd_conv_w, v_ssd_conv_b, v_ssd_dt_bias, v_ssd_a_log, v_ssd_d, v_ssd_gnorm_w, v_ssd_out_w, v_sb_in_w, v_sb_qn_w, v_sb_kn_w, v_sb_out_w, v_ple_norm_w, v_ple_gate_w, v_ple_proj_w):
    w_in = dict(norm_w=norm_w, ssd_in_w=ssd_in_w, ssd_conv_w=ssd_conv_w, ssd_conv_b=ssd_conv_b, ssd_dt_bias=ssd_dt_bias,
                ssd_a_log=ssd_a_log, ssd_d=ssd_d, ssd_gnorm_w=ssd_gnorm_w, ssd_out_w=ssd_out_w, sb_in_w=sb_in_w, sb_qn_w=sb_qn_w,
                sb_kn_w=sb_kn_w, sb_out_w=sb_out_w, ple_norm_w=ple_norm_w, ple_gate_w=ple_gate_w, ple_proj_w=ple_proj_w)
    m_in = dict(norm_w=m_norm_w, ssd_in_w=m_ssd_in_w, ssd_conv_w=m_ssd_conv_w, ssd_conv_b=m_ssd_conv_b, ssd_dt_bias=m_ssd_dt_bias,
                ssd_a_log=m_ssd_a_log, ssd_d=m_ssd_d, ssd_gnorm_w=m_ssd_gnorm_w, ssd_out_w=m_ssd_out_w, sb_in_w=m_sb_in_w,
                sb_qn_w=m_sb_qn_w, sb_kn_w=m_sb_kn_w, sb_out_w=m_sb_out_w, ple_norm_w=m_ple_norm_w, ple_gate_w=m_ple_gate_w,
                ple_proj_w=m_ple_proj_w)
    v_in = dict(norm_w=v_norm_w, ssd_in_w=v_ssd_in_w, ssd_conv_w=v_ssd_conv_w, ssd_conv_b=v_ssd_conv_b, ssd_dt_bias=v_ssd_dt_bias,
                ssd_a_log=v_ssd_a_log, ssd_d=v_ssd_d, ssd_gnorm_w=v_ssd_gnorm_w, ssd_out_w=v_ssd_out_w, sb_in_w=v_sb_in_w,
                sb_qn_w=v_sb_qn_w, sb_kn_w=v_sb_kn_w, sb_out_w=v_sb_out_w, ple_norm_w=v_ple_norm_w, ple_gate_w=v_ple_gate_w,
                ple_proj_w=v_ple_proj_w)
    ix, iy, ic = lax.axis_index("x"), lax.axis_index("y"), lax.axis_index("c")
    chip = (2 * ix + iy).astype(jnp.int32)
    place = jnp.stack([chip, ic.astype(jnp.int32)])
    zero = jnp.zeros((), jnp.int32)
    big_names = [n for n, _, _ in _BIG]
    layers_of = {n: s[0] for n, s, _ in _BIG}
    cut_of = {n: cut for n, _, cut in _BIG}

    def layer_pieces(i):
        mixer = ("ssd_in_w", "ssd_out_w") if i % 2 == 0 else ("sb_in_w", "sb_out_w")
        return [(mixer[0], i // 2), (mixer[1], i // 2), ("ple_gate_w", i), ("ple_proj_w", i)]

    def names_of(pieces):
        return [n for n in big_names if any(n == q for q, _ in pieces)]

    held = lambda n, a: a.transpose(0, 2, 1) if cut_of[n] == "stack" else a
    mine = {n: held(n, w_in[n]).astype(BF16) for n in big_names}
    shard_shapes = [mine[n].shape for n in big_names]
    room = [jax.ShapeDtypeStruct((s[0], 4 * s[1], s[2]) if cut_of[n] == "stack" else (4,) + s, BF16) for n, s in zip(big_names, shard_shapes)]
    first = layer_pieces(0)[:1]
    gathered = _comm_call([mine[n] for n in big_names], room, 6 * len(first),
                          [_ag_ici(first, big_names), _ag_pass_on(first, big_names, shard_shapes, base=3 * len(first))], "allgather_layer0")
    gw = {}
    for n, g in zip(big_names, gathered):
        if cut_of[n] == "stack":
            layers, r, c = mine[n].shape
            gw[n] = lax.dynamic_update_slice(g.reshape(layers, 4, r, c), mine[n][:, None], (zero, chip, zero, zero)).reshape(g.shape)
        else:
            gw[n] = lax.dynamic_update_slice(g, mine[n][None], (chip, zero, zero, zero))

    def gather_rider(pieces):
        names = names_of(pieces)
        return names, _Rider([mine[n] for n in names], [gw[n] for n in names], 3 * len(pieces), _ag_ici(pieces, names))

    def landed(names, bufs):
        for n, g in zip(names, bufs):
            gw[n] = g

    def pass_on(pieces, call):
        names = names_of(pieces)
        landed(names, _comm_call([], [gw[n] for n in names], 3 * len(pieces), [_ag_pass_on(pieces, names, [mine[n].shape for n in names])], call))

    onehot = (jnp.arange(4) == chip).astype(F32) * (ic == 0).astype(F32)
    cw_mine = onehot[:, None, None, None] * ssd_conv_w[None]
    cw_full = _allreduce_small(cw_mine.transpose(1, 2, 0, 3).reshape(-1, LANES), "gather_conv_w").reshape(2, SSD_D_CONV, SSD_CONV_DIM)

    def wmm(a, name, layer, *, dn="nn", res=None, call, rider=None):
        return _matmul(a, gw[name], dn=dn, res=res, b_lay=(cut_of[name], layer), name=call, rider=rider)

    h = x[0]
    target = loss_target[0]
    saved = []
    for i in range(DEPTH):
        j = i // 2
        nw = norm_w[i:i + 1]
        pw = ple_norm_w[i:i + 1]
        nxt = layer_pieces(i + 1) if i + 1 < DEPTH else None
        s = dict(h=h)
        u = _rms_fwd(h, nw, f"rms_{i}")
        s["u"] = u
        if i % 2 == 0:
            w_dt = jnp.pad(gw["ssd_in_w"][j, SSD_ZX:], ((0, LANES - SSD_N_HEADS), (0, 0)))
            if nxt:
                riding = layer_pieces(0)[1:] if i == 0 else nxt[1:]
                names, rider = gather_rider(riding)
                pzx, rode = _matmul(u, gw["ssd_in_w"], dn="nt", b_lay=("stack", j, SSD_ZX), name=f"ssd_in_{i}", rider=rider)
                landed(names, rode)
                if i == 0:
                    pass_on(riding, "allgather_pass_0")
            else:
                pzx = _matmul(u, gw["ssd_in_w"], dn="nt", b_lay=("stack", j, SSD_ZX), name=f"ssd_in_{i}")
            pdt = _matmul(u, w_dt, dn="nt", name=f"ssd_indt_{i}")
            act = _conv_fwd(pzx, cw_full[j], ssd_conv_b[j:j + 1], f"conv_{i}")
            dtg = jnp.pad(pdt[:, :SSD_N_HEADS].reshape(-1, SSD_N_GROUPS, 8).transpose(1, 0, 2), ((0, 0), (0, 0), (0, LANES - 8)))
            vecs = (_group_lanes(ssd_dt_bias[j]), _group_lanes(ssd_a_log[j]), _group_lanes(ssd_d[j]))
            if nxt:
                names, rider = gather_rider(nxt[:1])
                (yn, states), rode = _ssd_fwd(act, dtg, *vecs, pzx, ssd_gnorm_w[j:j + 1], f"ssd_{i}", rider=rider)
                landed(names, rode)
            else:
                yn, states = _ssd_fwd(act, dtg, *vecs, pzx, ssd_gnorm_w[j:j + 1], f"ssd_{i}")
            s.update(w_dt=w_dt, pzx=pzx, act=act, dtg=dtg, vecs=vecs, yn=yn, states=states)
            if i == 0:
                names, rider = gather_rider(nxt[1:])
                h1, rode = wmm(yn, "ssd_out_w", j, res=h, call=f"ssd_out_{i}", rider=rider)
                landed(names, rode)
            else:
                h1 = wmm(yn, "ssd_out_w", j, res=h, call=f"ssd_out_{i}")
            if nxt:
                pass_on(nxt, f"allgather_pass_{i + 1}")
        else:
            if nxt:
                names, rider = gather_rider(nxt[2:])
                proj, rode = wmm(u, "sb_in_w", j, call=f"sb_in_{i}", rider=rider)
                landed(names, rode)
            else:
                proj = wmm(u, "sb_in_w", j, call=f"sb_in_{i}")
            qn, kn, vb = _qknorm_fwd(proj, sb_qn_w[j:j + 1], sb_kn_w[j:j + 1], f"qknorm_{i}")
            if nxt:
                names, rider = gather_rider(nxt[:2])
                (og, o, tot), rode = _sb_fwd(qn, kn, vb, proj, f"sb_{i}", rider=rider)
                landed(names, rode)
                pass_on(nxt, f"allgather_pass_{i + 1}")
            else:
                og, o, tot = _sb_fwd(qn, kn, vb, proj, f"sb_{i}")
            s.update(proj=proj, qn=qn, kn=kn, vb=vb, og=og, o=o, tot=tot)
            h1 = wmm(og, "sb_out_w", j, res=h, call=f"sb_out_{i}")
        n2 = _rms_fwd(h1, pw, f"ple_rms_{i}")
        gl = wmm(n2, "ple_gate_w", i, call=f"ple_gate_{i}")
        pp = wmm(p[i, 0], "ple_proj_w", i, call=f"ple_proj_{i}")
        h = _ple_fwd(h1, pp, gl, f"ple_{i}")
        s.update(h1=h1, n2=n2, gl=gl, pp=pp)
        saved.append(s)

    dh, loss_lanes = _loss_bwd(h, target, "loss")

    wg = {}
    gsmall = {n: [None] * s[0] for n, s in _SMALL}
    g_conv_w = [None, None]
    scat = {}
    pending = late = None

    def wgrad(a, b, name, layer, call, rider=None):
        out = _matmul(a, b, dn="tn", out_dtype=BF16, o_lay=(cut_of[name], 0, 1), name=call, rider=rider)
        wg[(name, layer)], rode = out if rider is not None else (out, None)
        return rode

    def pair_sums(pieces, got, tag):
        return pieces, [_pair_sum(wg[q], r, place, f"rs_pair_sum_{tag}_{k}") for k, (q, r) in enumerate(zip(pieces, got))]

    def sibling_rider(pieces):
        return _exchange_rider([wg[q] for q in pieces])

    def riding_with(own):
        return (pending[0] + own[0], pending[1] + own[1]) if pending else own

    def arrived(sent, got):
        for q, pair, g in zip(sent[0], sent[1], got):
            scat[q] = (pair, g)

    for i in reversed(range(DEPTH)):
        j = i // 2
        s = saved[i]
        nw = norm_w[i:i + 1]
        pw = ple_norm_w[i:i + 1]
        dpp, dgl = _ple_bwd(dh, s["pp"], s["gl"], f"ple_bwd_{i}")
        wgrad(p[i, 0], dpp, "ple_proj_w", i, f"d_ple_proj_{i}")
        if late is None:
            wgrad(s["n2"], dgl, "ple_gate_w", i, f"d_ple_gate_{i}")
        else:
            pending = pair_sums(late, wgrad(s["n2"], dgl, "ple_gate_w", i, f"d_ple_gate_{i}", rider=sibling_rider(late)), f"{i + 1}_in")
        dn2 = wmm(dgl, "ple_gate_w", i, dn="nt", call=f"ple_gate_bwd_{i}")
        dh1, dpw = _rms_bwd(s["h1"], pw, dn2, dh, f"ple_rms_bwd_{i}")
        gsmall["ple_norm_w"][i] = dpw
        if i % 2 == 0:
            wgrad(s["yn"], dh1, "ssd_out_w", j, f"d_ssd_out_{i}")
            early = layer_pieces(i)[1:]
            dyn, got = wmm(dh1, "ssd_out_w", j, dn="nt", call=f"ssd_out_bwd_{i}", rider=sibling_rider(early))
            riding = riding_with(pair_sums(early, got, f"{i}_out"))
            outs, got = _ssd_bwd(s["act"], s["dtg"], *s["vecs"], s["pzx"], ssd_gnorm_w[j:j + 1], s["states"], dyn, f"ssd_bwd_{i}",
                                 rider=_scatter_rider(riding[1]))
            arrived(riding, got)
            dxs, dbm, dcm, ddtg, dbias, dalog, ddsk, dz, dgw = outs
            dzx, dcw, dcb = _conv_bwd(s["pzx"], cw_full[j], ssd_conv_b[j:j + 1], dxs, dbm, dcm, dz, f"conv_bwd_{i}")
            ddt = _pad_lanes(ddtg[:, :, :8].transpose(1, 0, 2).reshape(-1, SSD_N_HEADS)).astype(BF16)
            du = _matmul(dzx, gw["ssd_in_w"], b_lay=("stack", j, SSD_ZX), name=f"ssd_in_bwd_{i}")
            du = _matmul(ddt, s["w_dt"], res=du, name=f"ssd_indt_bwd_{i}")
            dwt = _matmul(dzx, s["u"], dn="tn", out_dtype=BF16, out_rows=SSD_IN_DIM, name=f"d_ssd_in_{i}")
            dwt_dt = _matmul(ddt, s["u"], dn="tn", out_dtype=BF16, name=f"d_ssd_indt_{i}")
            wg[("ssd_in_w", j)] = lax.dynamic_update_slice(dwt, dwt_dt[:SSD_N_HEADS], (SSD_ZX, 0))[None]
            g_conv_w[j] = dcw
            gsmall["ssd_conv_b"][j] = dcb
            gsmall["ssd_dt_bias"][j] = dbias[:, 0, :8].reshape(1, SSD_N_HEADS)
            gsmall["ssd_a_log"][j] = dalog[:, 0, :8].reshape(1, SSD_N_HEADS)
            gsmall["ssd_d"][j] = ddsk[:, 0, :8].reshape(1, SSD_N_HEADS)
            gsmall["ssd_gnorm_w"][j] = dgw
        else:
            wgrad(s["og"], dh1, "sb_out_w", j, f"d_sb_out_{i}")
            early = layer_pieces(i)[1:]
            dog, got = wmm(dh1, "sb_out_w", j, dn="nt", call=f"sb_out_bwd_{i}", rider=sibling_rider(early))
            riding = riding_with(pair_sums(early, got, f"{i}_out"))
            outs, got = _sb_bwd(s["qn"], s["kn"], s["vb"], s["proj"], s["o"], s["tot"], dog, f"sb_bwd_{i}", rider=_scatter_rider(riding[1]))
            arrived(riding, got)
            dqn, dkn, _, dvb, dg = outs
            dq, dk, dqw, dkw = _qknorm_bwd(s["proj"], sb_qn_w[j:j + 1], sb_kn_w[j:j + 1], dqn, dkn, f"qknorm_bwd_{i}")
            dproj = jnp.concatenate([dq, dk, dvb, dg], axis=1)
            du = wmm(dproj, "sb_in_w", j, dn="nt", call=f"sb_in_bwd_{i}")
            wgrad(s["u"], dproj, "sb_in_w", j, f"d_sb_in_{i}")
            gsmall["sb_qn_w"][j] = dqw
            gsmall["sb_kn_w"][j] = dkw
        dh, dnw = _rms_bwd(s["h"], nw, du, dh1, f"rms_bwd_{i}")
        gsmall["norm_w"][i] = dnw
        late = layer_pieces(i)[:1]
    grad_x = dh[None]
    pending = pair_sums(late, _pair_exchange([wg[q] for q in late], "rs_pair_exchange_last"), "0_in")
    arrived(pending, _chip_scatter(pending[1], "rs_chip_scatter_last"))

    halves = []
    for n in big_names:
        buf = None
        for l in range(layers_of[n]):
            buf = _chip_sum(*scat[(n, l)], place, l, layers_of[n], buf, f"rs_chip_sum_{n}_{l}")
        halves.append(buf)
    g_big = dict(zip(big_names, _pair_gather(halves, [cut_of[n] == "stack" for n in big_names])))

    small_parts = [jnp.concatenate(gsmall[n], axis=0).reshape(-1) for n, _ in _SMALL]
    small_parts.append(jnp.stack(g_conv_w).reshape(-1))
    small_parts.append(loss_lanes.reshape(-1))
    small_sum = _allreduce_small(jnp.concatenate(small_parts).reshape(-1, LANES), "allreduce_small").reshape(-1)
    g_small, off = {}, 0
    for n, shape in _SMALL:
        size = math.prod(shape)
        g_small[n] = small_sum[off:off + size].reshape(shape)
        off += size
    cw_size = 2 * SSD_D_CONV * SSD_CONV_DIM
    g_cw_full = small_sum[off:off + cw_size].reshape(2, SSD_D_CONV, 4, SSD_CONV_DIM // 4)
    g_small["ssd_conv_w"] = jnp.sum(g_cw_full * (jnp.arange(4) == chip).astype(F32)[None, None, :, None], axis=2)
    loss = 0.5 * jnp.sum(small_sum[off + cw_size:]) / D_MODEL

    grads, delta, new_m, new_v = {}, {}, {}, {}
    for n in big_names:
        grads[n], delta[n], new_m[n], new_v[n] = (
            held(n, a) for a in (g_big[n], *_adamw(held(n, w_in[n]), g_big[n], held(n, m_in[n]), held(n, v_in[n]), f"adamw_{n}")))
    small_names = [n for n, _ in _SMALL] + ["ssd_conv_w"]
    pack = lambda d: jnp.concatenate([d[n].reshape(-1) for n in small_names]).reshape(1, -1, LANES)
    ds, ms, vs = _adamw(pack(w_in), pack(g_small), pack(m_in), pack(v_in), "adamw_small")
    off = 0
    for n in small_names:
        shape = w_in[n].shape
        size = math.prod(shape)
        grads[n] = g_small[n]
        delta[n] = ds.reshape(-1)[off:off + size].reshape(shape)
        new_m[n] = ms.reshape(-1)[off:off + size].reshape(shape)
        new_v[n] = vs.reshape(-1)[off:off + size].reshape(shape)
        off += size

    order = ["norm_w", "ssd_in_w", "ssd_conv_w", "ssd_conv_b", "ssd_dt_bias", "ssd_a_log", "ssd_d", "ssd_gnorm_w", "ssd_out_w",
             "sb_in_w", "sb_qn_w", "sb_kn_w", "sb_out_w", "ple_norm_w", "ple_gate_w", "ple_proj_w"]
    return (loss, grad_x, *[grads[n] for n in order], *[delta[n] for n in order], *[new_m[n] for n in order],
            *[new_v[n] for n in order])
```

```python
import functools
import math

import jax
import jax.numpy as jnp
from jax import lax
from jax.experimental import pallas as pl
from jax.experimental.pallas import tpu as pltpu

F32 = jnp.float32
BF16 = jnp.bfloat16
MESH = pl.DeviceIdType.MESH

D_MODEL = 2048
DEPTH = 4
SSD_D_INNER = 4096
SSD_N_GROUPS = 8
SSD_GROUP_W = SSD_D_INNER // SSD_N_GROUPS
SSD_D_STATE = 128
SSD_CHUNK = 128
SSD_CONV_DIM = 6144
SSD_D_CONV = 4
SSD_N_HEADS = 64
SB_HEAD_DIM = 128
SB_N_HEADS = 16
SB_WIDTH = 2048
NORM_EPS = 1e-6
GATED_NORM_EPS = 1e-5
ADAM_LR = 0.001
ADAM_B1 = 0.9
ADAM_B2 = 0.999
ADAM_EPS = 1e-08
ADAM_WD = 0.01
ADAM_STEP = 10

SSD_ZX = SSD_D_INNER + SSD_CONV_DIM
SSD_IN_DIM = SSD_ZX + SSD_N_HEADS
LANES = 128
BF16_ROWS = 16

_BIG = (
    ("ssd_in_w", (2, 2576, 2048), "stack"),
    ("ssd_out_w", (2, 1024, 2048), "row"),
    ("sb_in_w", (2, 2048, 2048), "col"),
    ("sb_out_w", (2, 512, 2048), "row"),
    ("ple_gate_w", (4, 512, 2048), "row"),
    ("ple_proj_w", (4, 256, 512), "col"),
)
_SMALL = (
    ("norm_w", (4, 2048)),
    ("ssd_conv_b", (2, 6144)),
    ("ssd_dt_bias", (2, 64)),
    ("ssd_a_log", (2, 64)),
    ("ssd_d", (2, 64)),
    ("ssd_gnorm_w", (2, 4096)),
    ("sb_qn_w", (2, 128)),
    ("sb_kn_w", (2, 128)),
    ("ple_norm_w", (4, 2048)),
)

_DN = {
    "nn": (((1,), (0,)), ((), ())),
    "nt": (((1,), (1,)), ((), ())),
    "tn": (((0,), (0,)), ((), ())),
}


def _dot(a, b, dn="nn"):
    return lax.dot_general(a.astype(BF16), b.astype(BF16), _DN[dn], preferred_element_type=F32)


@functools.partial(jax.custom_vjp, nondiff_argnums=(2,))
def _gdot(a, b, dn):
    return _dot(a, b, dn)


def _gdot_fwd(a, b, dn):
    return _dot(a, b, dn), (a, b)


def _gdot_bwd(dn, res, g):
    a, b = res
    if dn == "nn":
        return _dot(g, b, "nt"), _dot(a, g, "tn")
    if dn == "nt":
        return _dot(g, b, "nn"), _dot(g, a, "tn")
    return _dot(b, g, "nt"), _dot(a, g, "nn")


_gdot.defvjp(_gdot_fwd, _gdot_bwd)


def _split_dot(x, t, parts, x_left):
    acc = None
    r = x
    for i in range(parts):
        p = r.astype(BF16)
        d = lax.dot_general(p, t, _DN["nn"], preferred_element_type=F32) if x_left else lax.dot_general(
            t, p, _DN["nn"], preferred_element_type=F32)
        acc = d if acc is None else acc + d
        if i + 1 < parts:
            r = r - p.astype(F32)
    return acc


def _tri(n, lower, strict=False):
    r = lax.broadcasted_iota(jnp.int32, (n, n), 0)
    c = lax.broadcasted_iota(jnp.int32, (n, n), 1)
    keep = (r > c if strict else r >= c) if lower else (r < c if strict else r <= c)
    return jnp.where(keep, 1.0, 0.0).astype(BF16)


def _cumsum_rows_raw(x):
    return _split_dot(x, _tri(x.shape[0], True), 3, False)


@jax.custom_vjp
def _cumsum_rows(x):
    return _cumsum_rows_raw(x)


def _cumsum_rows_fwd(x):
    return _cumsum_rows_raw(x), None


def _cumsum_rows_bwd(_, g):
    return (_split_dot(g, _tri(g.shape[0], False), 3, False),)


_cumsum_rows.defvjp(_cumsum_rows_fwd, _cumsum_rows_bwd)


def _sigmoid(x):
    return 1.0 / (1.0 + jnp.exp(-x))


def _softplus(x):
    return jnp.maximum(x, 0.0) + jnp.log(1.0 + jnp.exp(-jnp.abs(x)))


def _rms(x, w, eps):
    return x * lax.rsqrt(jnp.mean(x * x, axis=-1, keepdims=True) + eps) * w


_ANY = pl.BlockSpec(memory_space=pl.ANY)


def _params(*sem):
    return pltpu.CompilerParams(dimension_semantics=sem)


class _Rider:
    def __init__(self, reads, writes, n_sems, issue):
        self.reads, self.writes, self.n_sems, self.issue = list(reads), list(writes), n_sems, issue


def _pcall(body, *, grid, in_specs, out_specs, out_shape, args, sem, name, scratch_shapes=(), aliases=None, rider=None):
    aliases = dict(aliases or {})
    if rider is None:
        outs = pl.pallas_call(body, grid=grid, in_specs=in_specs, out_specs=out_specs, out_shape=out_shape,
                              scratch_shapes=list(scratch_shapes), input_output_aliases=aliases,
                              compiler_params=_params(*sem), name=name)(*args)
        return list(outs), []
    n_in, n_out, n_scr, n_rd, n_wr = len(args), len(out_shape), len(scratch_shapes), len(rider.reads), len(rider.writes)
    passed = [k for k, w in enumerate(rider.writes) if not isinstance(w, jax.ShapeDtypeStruct)]
    for pos, k in enumerate(passed):
        aliases[n_in + n_rd + pos] = n_out + k

    def wrapped(*refs):
        ins = refs[:n_in]
        reads = refs[n_in:n_in + n_rd]
        base = n_in + n_rd + len(passed)
        outs = refs[base:base + n_out]
        writes = refs[base + n_out:base + n_out + n_wr]
        scr = refs[base + n_out + n_wr:base + n_out + n_wr + n_scr]
        send, recv = refs[-2:]
        first = last = None
        for d, n in enumerate(grid):
            i = pl.program_id(d)
            first = (i == 0) if first is None else first & (i == 0)
            last = (i == n - 1) if last is None else last & (i == n - 1)

        @pl.when(first)
        def _():
            for cp in rider.issue(reads, writes, send, recv)[0]:
                cp.start()

        body(*ins, *outs, *scr)

        @pl.when(last)
        def _():
            sends, arrivals = rider.issue(reads, writes, send, recv)
            for cp in arrivals:
                cp.wait_recv()
            for cp in sends:
                cp.wait_send()

    outs = pl.pallas_call(
        wrapped, grid=grid,
        in_specs=list(in_specs) + [_ANY] * (n_rd + len(passed)),
        out_specs=list(out_specs) + [_ANY] * n_wr,
        out_shape=list(out_shape) + [jax.ShapeDtypeStruct(w.shape, w.dtype) for w in rider.writes],
        scratch_shapes=list(scratch_shapes) + [pltpu.SemaphoreType.DMA((rider.n_sems,)), pltpu.SemaphoreType.DMA((rider.n_sems,))],
        input_output_aliases=aliases, compiler_params=_params(*(["arbitrary"] * len(grid))), name=name,
    )(*args, *rider.reads, *[rider.writes[k] for k in passed])
    return list(outs[:n_out]), list(outs[n_out:])


MM_TK = 2048


def _pick(dim, pref, unit=None):
    t = pref
    while t >= LANES:
        if dim % t == 0 and (unit is None or unit % t == 0):
            return t
        t //= 2
    return dim


def _matmul(a, b, *, dn="nn", res=None, out_dtype=F32, name, b_lay=None, o_lay=None, o_buf=None, out_rows=None, rider=None):
    if dn == "tn":
        k_dim, m_dim = a.shape
    else:
        m_dim, k_dim = a.shape
    unit_m = unit_n = unit_k = None
    if b_lay is None:
        n_dim = b.shape[0] if dn == "nt" else b.shape[1]
    elif b_lay[0] == "stack":
        cut, layer, rows = b_lay
        cols = b.shape[2]
        n_dim = cols if dn == "nn" else rows
        assert k_dim == (rows if dn == "nn" else cols) and dn != "tn"
    else:
        cut, layer = b_lay
        r, c = b.shape[2:]
        rows, cols = (4 * r, c) if cut == "row" else (r, 4 * c)
        n_dim = cols if dn == "nn" else rows
        assert k_dim == (rows if dn == "nn" else cols) and dn != "tn"
        if (cut == "row") == (dn == "nn"):
            unit_k = r if cut == "row" else c
        else:
            unit_n = r if cut == "row" else c
    if o_lay is not None:
        o_cut, o_layer, o_layers = o_lay
        if o_cut == "row":
            unit_m = m_dim // 4
        else:
            unit_n = n_dim // 4
    tm, tn, tk = _pick(m_dim, 1024, unit_m), _pick(n_dim, 1024, unit_n), _pick(k_dim, MM_TK, unit_k)
    nk = k_dim // tk
    a_spec = pl.BlockSpec((tk, tm), lambda i, j, k: (k, i)) if dn == "tn" else pl.BlockSpec((tm, tk), lambda i, j, k: (i, k))
    if b_lay is None:
        b_spec = pl.BlockSpec((tn, tk), lambda i, j, k: (j, k)) if dn == "nt" else pl.BlockSpec((tk, tn), lambda i, j, k: (k, j))
    elif cut == "stack":
        b_spec = (pl.BlockSpec((None, tk, tn), lambda i, j, k: (layer, k, j)) if dn == "nn" else
                  pl.BlockSpec((None, tn, tk), lambda i, j, k: (layer, j, k)))
    elif dn == "nn" and cut == "row":
        per = r // tk
        b_spec = pl.BlockSpec((None, None, tk, tn), lambda i, j, k: (k // per, layer, k % per, j))
    elif dn == "nn":
        per = c // tn
        b_spec = pl.BlockSpec((None, None, tk, tn), lambda i, j, k: (j // per, layer, k, j % per))
    elif cut == "row":
        per = r // tn
        b_spec = pl.BlockSpec((None, None, tn, tk), lambda i, j, k: (j // per, layer, j % per, k))
    else:
        per = c // tk
        b_spec = pl.BlockSpec((None, None, tn, tk), lambda i, j, k: (k // per, layer, j, k % per))
    r_spec = pl.BlockSpec((tm, tn), lambda i, j, k: (i, j))
    if o_lay is None:
        o_spec = r_spec
        out_shape = jax.ShapeDtypeStruct((out_rows or m_dim, n_dim), out_dtype)
    elif o_cut == "row":
        per_o = unit_m // tm
        o_spec = pl.BlockSpec((None, None, tm, tn), lambda i, j, k: (i // per_o, o_layer, i % per_o, j))
        out_shape = jax.ShapeDtypeStruct((4, o_layers, unit_m, n_dim), out_dtype)
    else:
        per_o = unit_n // tn
        o_spec = pl.BlockSpec((None, None, tm, tn), lambda i, j, k: (j // per_o, o_layer, i, j % per_o))
        out_shape = jax.ShapeDtypeStruct((4, o_layers, m_dim, unit_n), out_dtype)
    has_res = res is not None
    has_buf = o_buf is not None

    def body(*refs):
        a_ref, b_ref = refs[:2]
        r_ref = refs[2] if has_res else None
        o_ref = refs[-1] if nk == 1 else refs[-2]

        def finish(v):
            if has_res:
                v = v + r_ref[...]
            o_ref[...] = v.astype(o_ref.dtype)

        if nk == 1:
            finish(_dot(a_ref[...], b_ref[...], dn))
            return
        acc_ref = refs[-1]
        k = pl.program_id(2)

        @pl.when(k == 0)
        def _():
            acc_ref[...] = jnp.zeros_like(acc_ref)

        acc_ref[...] += _dot(a_ref[...], b_ref[...], dn)

        @pl.when(k == nk - 1)
        def _():
            finish(acc_ref[...])

    args = [a, b] + ([res] if has_res else []) + ([o_buf] if has_buf else [])
    outs, rode = _pcall(
        body, grid=(m_dim // tm, n_dim // tn, nk),
        in_specs=[a_spec, b_spec] + ([r_spec] if has_res else []) + ([_ANY] if has_buf else []),
        out_specs=[o_spec], out_shape=[out_shape],
        scratch_shapes=[] if nk == 1 else [pltpu.VMEM((tm, tn), F32)],
        aliases={len(args) - 1: 0} if has_buf else {},
        args=args, sem=("parallel", "parallel", "arbitrary"), name=name, rider=rider)
    return (outs[0], rode) if rider is not None else outs[0]


def _rowcall(fn, rows, consts, outs, accs, *, name, tm=256):
    args = list(rows) + list(consts)
    in_specs = [pl.BlockSpec((tm, r.shape[1]), lambda i: (i, 0)) for r in rows]
    in_specs += [pl.BlockSpec(c.shape, lambda i: (0, 0)) for c in consts]
    s_dim = args[0].shape[0]
    n_in, n_out = len(args), len(outs)
    out_shape = [jax.ShapeDtypeStruct((s_dim, w), dt) for w, dt in outs] + [jax.ShapeDtypeStruct(s, F32) for s in accs]
    out_specs = [pl.BlockSpec((tm, w), lambda i: (i, 0)) for w, _ in outs] + [pl.BlockSpec(s, lambda i: (0, 0)) for s in accs]

    def body(*refs):
        vals = fn(*[r[...] for r in refs[:n_in]])
        o_refs = refs[n_in:n_in + n_out]
        a_refs = refs[n_in + n_out:]
        for o, v in zip(o_refs, vals[:n_out]):
            o[...] = v.astype(o.dtype)
        if a_refs:
            @pl.when(pl.program_id(0) == 0)
            def _():
                for a_ref in a_refs:
                    a_ref[...] = jnp.zeros_like(a_ref)

            for a_ref, v in zip(a_refs, vals[n_out:]):
                a_ref[...] += v

    return pl.pallas_call(
        body, grid=(s_dim // tm,), in_specs=in_specs, out_specs=out_specs, out_shape=out_shape,
        compiler_params=_params("arbitrary"), name=name,
    )(*args)


def _rms_fwd(h, w, name):
    return _rowcall(lambda x, w_: (_rms(x, w_, NORM_EPS),), [h], [w], [(h.shape[1], BF16)], [], name=name)[0]


def _rms_bwd(h, w, dy, dres, name):
    def fn(x, dy_, dres_, w_):
        _, vjp = jax.vjp(lambda a, b: _rms(a, b, NORM_EPS), x, w_)
        dx, dw = vjp(dy_)
        return dx + dres_, dw

    return _rowcall(fn, [h, dy, dres], [w], [(h.shape[1], F32)], [w.shape], name=name)


def _ple_fwd(h1, pp, gl, name):
    return _rowcall(lambda a, b, c: (a + b * _sigmoid(c),), [h1, pp, gl], [], [(h1.shape[1], F32)], [], name=name)[0]


def _ple_bwd(dh2, pp, gl, name):
    def fn(d, b, c):
        gate = _sigmoid(c)
        return d * gate, d * b * gate * (1.0 - gate)

    return _rowcall(fn, [dh2, pp, gl], [], [(dh2.shape[1], BF16), (dh2.shape[1], BF16)], [], name=name)


def _loss_bwd(y, target, name):
    width = y.shape[1]

    def fn(a, t):
        d = a - t
        col = jnp.sum(d * d, axis=0, keepdims=True)
        part = col[:, 0:LANES]
        for j in range(1, width // LANES):
            part = part + col[:, j * LANES:(j + 1) * LANES]
        return d * (1.0 / width), part

    return _rowcall(fn, [y, target], [], [(width, F32)], [(1, LANES)], name=name)


CONV_TC = 256


def _shift_down(x, j):
    if j == 0:
        return x
    row = lax.broadcasted_iota(jnp.int32, x.shape, 0)
    return jnp.where(row >= j, pltpu.roll(x, j, 0), 0.0)


def _shift_up(x, j):
    if j == 0:
        return x
    n = x.shape[0]
    row = lax.broadcasted_iota(jnp.int32, x.shape, 0)
    return jnp.where(row < n - j, pltpu.roll(x, n - j, 0), 0.0)


def _conv_fwd(pzx, cw, cb, name):
    s_dim = pzx.shape[0]
    off = SSD_D_INNER // CONV_TC

    def body(x_ref, w_ref, b_ref, o_ref):
        x = x_ref[...]
        w = w_ref[...]
        y = b_ref[...] + w[3:4, :] * x
        for k in range(SSD_D_CONV - 1):
            y = y + w[k:k + 1, :] * _shift_down(x, SSD_D_CONV - 1 - k)
        o_ref[...] = y * _sigmoid(y)

    return pl.pallas_call(
        body, grid=(SSD_CONV_DIM // CONV_TC,),
        in_specs=[pl.BlockSpec((s_dim, CONV_TC), lambda j: (0, off + j)), pl.BlockSpec((SSD_D_CONV, CONV_TC), lambda j: (0, j)),
                  pl.BlockSpec((1, CONV_TC), lambda j: (0, j))],
        out_specs=pl.BlockSpec((s_dim, CONV_TC), lambda j: (0, j)),
        out_shape=jax.ShapeDtypeStruct((s_dim, SSD_CONV_DIM), F32),
        compiler_params=_params("parallel"), name=name,
    )(pzx, cw, cb)


def _conv_bwd(pzx, cw, cb, dxs, dbm, dcm, dzx, name):
    s_dim = pzx.shape[0]
    off = SSD_D_INNER // CONV_TC
    n_x, n_b = dxs.shape[1] // CONV_TC, dbm.shape[1] // CONV_TC

    def body(x_ref, w_ref, b_ref, dxs_ref, dbm_ref, dcm_ref, _, dx_ref, dw_ref, db_ref):
        j = pl.program_id(0)
        d = jnp.where(j < n_x, dxs_ref[...], jnp.where(j < n_x + n_b, dbm_ref[...], dcm_ref[...]))
        x = x_ref[...]
        w = w_ref[...]
        xs = [_shift_down(x, SSD_D_CONV - 1 - k) for k in range(SSD_D_CONV)]
        y = b_ref[...]
        for k in range(SSD_D_CONV):
            y = y + w[k:k + 1, :] * xs[k]
        sg = _sigmoid(y)
        dy = d * (sg * (1.0 + y * (1.0 - sg)))
        dx = w[3:4, :] * dy
        for k in range(SSD_D_CONV - 1):
            dx = dx + w[k:k + 1, :] * _shift_up(dy, SSD_D_CONV - 1 - k)
        dx_ref[...] = dx.astype(dx_ref.dtype)
        for k in range(SSD_D_CONV):
            dw_ref[k:k + 1, :] = jnp.sum(dy * xs[k], axis=0, keepdims=True)
        db_ref[...] = jnp.sum(dy, axis=0, keepdims=True)

    part = lambda lo, n: pl.BlockSpec((s_dim, CONV_TC), lambda j: (0, jnp.clip(j - lo, 0, n - 1)))
    return pl.pallas_call(
        body, grid=(SSD_CONV_DIM // CONV_TC,),
        in_specs=[pl.BlockSpec((s_dim, CONV_TC), lambda j: (0, off + j)), pl.BlockSpec((SSD_D_CONV, CONV_TC), lambda j: (0, j)),
                  pl.BlockSpec((1, CONV_TC), lambda j: (0, j)), part(0, n_x), part(n_x, n_b), part(n_x + n_b, n_b), _ANY],
        out_specs=[pl.BlockSpec((s_dim, CONV_TC), lambda j: (0, off + j)), pl.BlockSpec((SSD_D_CONV, CONV_TC), lambda j: (0, j)),
                   pl.BlockSpec((1, CONV_TC), lambda j: (0, j))],
        out_shape=[jax.ShapeDtypeStruct(dzx.shape, dzx.dtype), jax.ShapeDtypeStruct((SSD_D_CONV, SSD_CONV_DIM), F32),
                   jax.ShapeDtypeStruct((1, SSD_CONV_DIM), F32)],
        input_output_aliases={6: 0}, compiler_params=_params("arbitrary"), name=name,
    )(pzx, cw, cb, dxs, dbm, dcm, dzx)


def _ssd_step(xs, bm, cm, dtraw, bias, alog, dskip, st_in, z, gw, dot, cumsum):
    n = xs.shape[0]
    lane = lax.broadcasted_iota(jnp.int32, (1, LANES), 1)
    sub = lax.broadcasted_iota(jnp.int32, (LANES, 1), 0)
    left = (lane < 64).astype(F32)
    right = 1.0 - left
    top = (sub < 64).astype(F32)
    bot = 1.0 - top
    row = lax.broadcasted_iota(jnp.int32, (n, n), 0)
    colm = lax.broadcasted_iota(jnp.int32, (n, n), 1)
    causal = row >= colm

    dt = _softplus(dtraw + bias)
    adt = dt * (-jnp.exp(alog))
    acum = cumsum(adt)
    acum_t = acum.T
    last = jnp.sum(adt, axis=0, keepdims=True)
    scores = dot(cm, bm, "nt")

    def lane_of(v, h):
        return jnp.sum(v * (lane == h).astype(F32), axis=1, keepdims=True)

    ys, sts = [], []
    for pr in range(4):
        heads = (2 * pr, 2 * pr + 1)
        ac = [lane_of(acum, h) for h in heads]
        ar = [jnp.sum(acum_t * (sub == h).astype(F32), axis=0, keepdims=True) for h in heads]
        dth = [lane_of(dt, h) for h in heads]
        la = [lane_of(last, h) for h in heads]
        dk = [lane_of(dskip, h) for h in heads]
        x2 = xs[:, pr * LANES:(pr + 1) * LANES]
        xdt = x2 * (dth[0] * left + dth[1] * right)
        yd = None
        for i, side in enumerate((left, right)):
            decay = jnp.where(causal, jnp.exp(jnp.minimum(ac[i] - ar[i], 0.0)), 0.0)
            t = dot(scores * decay, xdt * side, "nn")
            yd = t if yd is None else yd + t
        st2 = st_in[pr * LANES:(pr + 1) * LANES, :]
        yo = dot(cm, st2, "nt") * (jnp.exp(ac[0]) * left + jnp.exp(ac[1]) * right)
        dte = jnp.exp(la[0] - ac[0]) * left + jnp.exp(la[1] - ac[1]) * right
        cs = dot(xdt * dte, bm, "tn")
        sts.append(st2 * (jnp.exp(la[0]) * top + jnp.exp(la[1]) * bot) + cs)
        ys.append(yd + yo + (dk[0] * left + dk[1] * right) * x2)
    y = jnp.concatenate(ys, axis=1)
    yg = y * (z * _sigmoid(z))
    yn = yg * lax.rsqrt(jnp.mean(yg * yg, axis=-1, keepdims=True) + GATED_NORM_EPS) * gw
    return yn, jnp.concatenate(sts, axis=0)


def _ssd_specs(n_chunks, rev):
    ci = (lambda c: n_chunks - 1 - c) if rev else (lambda c: c)
    n_x = SSD_D_INNER // LANES
    return dict(
        xs=pl.BlockSpec((SSD_CHUNK, SSD_GROUP_W), lambda g, c: (ci(c), g)),
        bm=pl.BlockSpec((SSD_CHUNK, LANES), lambda g, c: (ci(c), n_x + g)),
        cm=pl.BlockSpec((SSD_CHUNK, LANES), lambda g, c: (ci(c), n_x + SSD_N_GROUPS + g)),
        dt=pl.BlockSpec((None, SSD_CHUNK, LANES), lambda g, c: (g, ci(c), 0)),
        vec=pl.BlockSpec((None, 1, LANES), lambda g, c: (g, 0, 0)),
        z=pl.BlockSpec((SSD_CHUNK, SSD_GROUP_W), lambda g, c: (ci(c), g)),
        gw=pl.BlockSpec((1, SSD_GROUP_W), lambda g, c: (0, g)),
        st=pl.BlockSpec((None, None, SSD_GROUP_W, SSD_D_STATE), lambda g, c: (g, ci(c), 0, 0)),
    )


def _ssd_fwd(act, dtg, bias, alog, dskip, pzx, gw, name, rider=None):
    s_dim = act.shape[0]
    n_chunks = s_dim // SSD_CHUNK
    sp = _ssd_specs(n_chunks, False)

    def body(xs, bm, cm, dt, b_ref, a_ref, d_ref, z, gw_ref, yn_ref, st_ref, state):
        @pl.when(pl.program_id(1) == 0)
        def _():
            state[...] = jnp.zeros_like(state)

        st_in = state[...]
        st_ref[...] = st_in
        yn, st_out = _ssd_step(xs[...], bm[...], cm[...], dt[...], b_ref[...], a_ref[...], d_ref[...], st_in, z[...], gw_ref[...],
                               _dot, _cumsum_rows_raw)
        yn_ref[...] = yn.astype(yn_ref.dtype)
        state[...] = st_out

    outs, rode = _pcall(
        body, grid=(SSD_N_GROUPS, n_chunks),
        in_specs=[sp["xs"], sp["bm"], sp["cm"], sp["dt"], sp["vec"], sp["vec"], sp["vec"], sp["z"], sp["gw"]],
        out_specs=[sp["xs"], sp["st"]],
        out_shape=[jax.ShapeDtypeStruct((s_dim, SSD_D_INNER), BF16),
                   jax.ShapeDtypeStruct((SSD_N_GROUPS, n_chunks, SSD_GROUP_W, SSD_D_STATE), F32)],
        scratch_shapes=[pltpu.VMEM((SSD_GROUP_W, SSD_D_STATE), F32)],
        args=[act, act, act, dtg, bias, alog, dskip, pzx, gw], sem=("parallel", "arbitrary"), name=name, rider=rider)
    return (outs, rode) if rider is not None else outs


def _ssd_bwd(act, dtg, bias, alog, dskip, pzx, gw, states, dyn, name, rider=None):
    s_dim = act.shape[0]
    n_chunks = s_dim // SSD_CHUNK
    sp = _ssd_specs(n_chunks, True)
    rc = lambda c: n_chunks - 1 - c

    def body(xs, bm, cm, dt, b_ref, a_ref, d_ref, z, gw_ref, st_ref, dyn_ref,
             dxs_ref, dbm_ref, dcm_ref, ddt_ref, db_ref, da_ref, dd_ref, dz_ref, dgw_ref, dstate):
        first = pl.program_id(1) == 0

        @pl.when(first)
        def _():
            dstate[...] = jnp.zeros_like(dstate)
            db_ref[...] = jnp.zeros_like(db_ref)
            da_ref[...] = jnp.zeros_like(da_ref)
            dd_ref[...] = jnp.zeros_like(dd_ref)
            dgw_ref[...] = jnp.zeros_like(dgw_ref)

        fn = functools.partial(_ssd_step, dot=_gdot, cumsum=_cumsum_rows)
        _, vjp = jax.vjp(fn, xs[...], bm[...], cm[...], dt[...], b_ref[...], a_ref[...], d_ref[...], st_ref[...], z[...], gw_ref[...])
        dxs, dbm, dcm, ddt, db, da, dd, dst, dz, dgw = vjp((dyn_ref[...], dstate[...]))
        dxs_ref[...] = dxs
        dbm_ref[...] = dbm
        dcm_ref[...] = dcm
        ddt_ref[...] = ddt
        dz_ref[...] = dz.astype(dz_ref.dtype)
        db_ref[...] += db
        da_ref[...] += da
        dd_ref[...] += dd
        dgw_ref[...] += dgw
        dstate[...] = dst

    bc = pl.BlockSpec((SSD_CHUNK, LANES), lambda g, c: (rc(c), g))
    outs, rode = _pcall(
        body, grid=(SSD_N_GROUPS, n_chunks),
        in_specs=[sp["xs"], sp["bm"], sp["cm"], sp["dt"], sp["vec"], sp["vec"], sp["vec"], sp["z"], sp["gw"], sp["st"], sp["xs"]],
        out_specs=[sp["xs"], bc, bc, sp["dt"], sp["vec"], sp["vec"], sp["vec"], sp["xs"], sp["gw"]],
        out_shape=[jax.ShapeDtypeStruct((s_dim, SSD_D_INNER), F32),
                   jax.ShapeDtypeStruct((s_dim, SSD_N_GROUPS * SSD_D_STATE), F32),
                   jax.ShapeDtypeStruct((s_dim, SSD_N_GROUPS * SSD_D_STATE), F32),
                   jax.ShapeDtypeStruct((SSD_N_GROUPS, s_dim, LANES), F32),
                   jax.ShapeDtypeStruct((SSD_N_GROUPS, 1, LANES), F32),
                   jax.ShapeDtypeStruct((SSD_N_GROUPS, 1, LANES), F32),
                   jax.ShapeDtypeStruct((SSD_N_GROUPS, 1, LANES), F32),
                   jax.ShapeDtypeStruct((s_dim, SSD_ZX), BF16),
                   jax.ShapeDtypeStruct((1, SSD_D_INNER), F32)],
        scratch_shapes=[pltpu.VMEM((SSD_GROUP_W, SSD_D_STATE), F32)],
        args=[act, act, act, dtg, bias, alog, dskip, pzx, gw, states, dyn], sem=("arbitrary", "arbitrary"), name=name, rider=rider)
    return (outs, rode) if rider is not None else outs


SB_T = 128
SB_GROUP = 8
SB_WIDE = SB_GROUP * SB_T
SB_HB = 4
SB_SCALE = 1.0 / math.sqrt(SB_HEAD_DIM)


def _qknorm_fwd(proj, qw, kw, name, tm=512):
    s_dim = proj.shape[0]

    def body(q_ref, k_ref, v_ref, qw_ref, kw_ref, qo, ko, vo):
        qo[...] = _rms(q_ref[...], qw_ref[...], NORM_EPS).astype(BF16)
        ko[...] = _rms(k_ref[...], kw_ref[...], NORM_EPS).astype(BF16)
        vo[...] = v_ref[...].astype(BF16)

    blk = lambda o: pl.BlockSpec((tm, SB_HEAD_DIM), lambda i, h: (i, o + h))
    vec = pl.BlockSpec((1, SB_HEAD_DIM), lambda i, h: (0, 0))
    return pl.pallas_call(
        body, grid=(s_dim // tm, SB_N_HEADS),
        in_specs=[blk(0), blk(SB_N_HEADS), blk(2 * SB_N_HEADS), vec, vec],
        out_specs=[blk(0)] * 3,
        out_shape=[jax.ShapeDtypeStruct((s_dim, SB_WIDTH), BF16)] * 3,
        compiler_params=_params("parallel", "parallel"), name=name,
    )(proj, proj, proj, qw, kw)


def _qknorm_bwd(proj, qw, kw, dqn, dkn, name, tm=512):
    s_dim = proj.shape[0]

    def body(q_ref, k_ref, dq_ref, dk_ref, qw_ref, kw_ref, dqo, dko, dqw, dkw):
        @pl.when((pl.program_id(0) == 0) & (pl.program_id(1) == 0))
        def _():
            dqw[...] = jnp.zeros_like(dqw)
            dkw[...] = jnp.zeros_like(dkw)

        fn = lambda a, b: _rms(a, b, NORM_EPS)
        _, vq = jax.vjp(fn, q_ref[...], qw_ref[...])
        dq, dw = vq(dq_ref[...])
        dqo[...] = dq.astype(BF16)
        dqw[...] += dw
        _, vk = jax.vjp(fn, k_ref[...], kw_ref[...])
        dk, dw = vk(dk_ref[...])
        dko[...] = dk.astype(BF16)
        dkw[...] += dw

    blk = lambda o: pl.BlockSpec((tm, SB_HEAD_DIM), lambda i, h: (i, o + h))
    vec = pl.BlockSpec((1, SB_HEAD_DIM), lambda i, h: (0, 0))
    return pl.pallas_call(
        body, grid=(s_dim // tm, SB_N_HEADS),
        in_specs=[blk(0), blk(SB_N_HEADS), blk(0), blk(0), vec, vec],
        out_specs=[blk(0), blk(0), vec, vec],
        out_shape=[jax.ShapeDtypeStruct((s_dim, SB_WIDTH), BF16)] * 2 + [jax.ShapeDtypeStruct((1, SB_HEAD_DIM), F32)] * 2,
        compiler_params=_params("arbitrary", "arbitrary"), name=name,
    )(proj, proj, dqn, dkn, qw, kw)


def _sb_logits(q, k, strict):
    z = _dot(q, k, "nt") * SB_SCALE
    lb = jnp.minimum(z, 0.0) - jnp.log(1.0 + jnp.exp(-jnp.abs(z)))
    lm = lb - z
    if strict is not None:
        lm = jnp.where(strict, lm, 0.0)
    return lb, lm


def _sb_strict(qi, grp):
    r = lax.broadcasted_iota(jnp.int32, (SB_T, SB_WIDE), 0) + qi * SB_T
    c = lax.broadcasted_iota(jnp.int32, (SB_T, SB_WIDE), 1) + grp * SB_WIDE
    return c < r


def _head_lanes(hh):
    return slice(hh * SB_HEAD_DIM, (hh + 1) * SB_HEAD_DIM)


def _sb_fwd(qn, kn, vb, proj, name, rider=None):
    s_dim = qn.shape[0]
    nq = s_dim // SB_T
    assert nq % SB_GROUP == 0

    def body(q_ref, k_ref, v_ref, g_ref, og_ref, o_ref, t_ref):
        qi = pl.program_id(1)
        top = qi // SB_GROUP
        after = _tri(SB_T, True, strict=True)
        qs = [q_ref[:, _head_lanes(hh)] for hh in range(SB_HB)]

        def step(grp, masked, carries):
            start = pl.multiple_of(grp * SB_WIDE, SB_WIDE)
            strict = _sb_strict(qi, grp) if masked else None
            out = []
            for hh in range(SB_HB):
                o_acc, cr = carries[hh]
                k = k_ref[pl.ds(start, SB_WIDE), _head_lanes(hh)]
                v = v_ref[pl.ds(start, SB_WIDE), _head_lanes(hh)]
                lb, lm = _sb_logits(qs[hh], k, strict)
                rest = [None] * SB_GROUP
                for t in reversed(range(SB_GROUP)):
                    lm_t = lm[:, t * SB_T:(t + 1) * SB_T]
                    rest[t] = cr + _split_dot(lm_t, after, 2, True)
                    cr = cr + jnp.sum(lm_t, axis=1, keepdims=True)
                a = jnp.exp(lb + jnp.concatenate(rest, axis=1))
                if masked:
                    a = jnp.where(strict, a, 0.0)
                out.append((o_acc + _dot(a, v), cr))
            return tuple(out)

        init = tuple((jnp.zeros((SB_T, SB_HEAD_DIM), F32), jnp.zeros((SB_T, 1), F32)) for _ in range(SB_HB))
        carries = step(top, True, init)
        carries = lax.fori_loop(0, top, lambda i, c: step(top - 1 - i, False, c), carries)
        for hh in range(SB_HB):
            o, tot = carries[hh]
            g = g_ref[:, _head_lanes(hh)]
            o_ref[:, _head_lanes(hh)] = o
            og_ref[:, _head_lanes(hh)] = (o * (g * _sigmoid(g))).astype(og_ref.dtype)
            t_ref[hh] = jnp.broadcast_to(tot, (SB_T, LANES))

    wide = SB_HB * SB_HEAD_DIM
    qb = pl.BlockSpec((SB_T, wide), lambda h, i: (i, h))
    kv = pl.BlockSpec((s_dim, wide), lambda h, i: (0, h))
    outs, rode = _pcall(
        body, grid=(SB_N_HEADS // SB_HB, nq),
        in_specs=[qb, kv, kv, pl.BlockSpec((SB_T, wide), lambda h, i: (i, 3 * SB_N_HEADS // SB_HB + h))],
        out_specs=[qb, qb, pl.BlockSpec((SB_HB, SB_T, LANES), lambda h, i: (h, i, 0))],
        out_shape=[jax.ShapeDtypeStruct((s_dim, SB_WIDTH), BF16), jax.ShapeDtypeStruct((s_dim, SB_WIDTH), F32),
                   jax.ShapeDtypeStruct((SB_N_HEADS, s_dim, LANES), F32)],
        args=[qn, kn, vb, proj], sem=("parallel", "arbitrary"), name=name, rider=rider)
    return (outs, rode) if rider is not None else outs


def _sb_bwd(qn, kn, vb, proj, o, tot, dog, name, rider=None):
    s_dim = qn.shape[0]
    nq = s_dim // SB_T
    assert nq % SB_GROUP == 0

    def body(q_ref, k_ref, v_ref, g_ref, o_ref, t_ref, dog_ref, dq_ref, dk_ref, dv_ref, dvb_ref, dg_ref):
        qi = pl.program_id(1)
        top = qi // SB_GROUP

        @pl.when(qi == 0)
        def _():
            dk_ref[...] = jnp.zeros_like(dk_ref)
            dv_ref[...] = jnp.zeros_like(dv_ref)

        after = _tri(SB_T, True, strict=True)
        before = _tri(SB_T, False, strict=True)
        qs, dos, totals = [], [], []
        for hh in range(SB_HB):
            g = g_ref[:, _head_lanes(hh)]
            sg = _sigmoid(g)
            dog_v = dog_ref[:, _head_lanes(hh)]
            dg_ref[:, _head_lanes(hh)] = (dog_v * o_ref[:, _head_lanes(hh)] * (sg * (1.0 + g * (1.0 - sg)))).astype(dg_ref.dtype)
            dos.append((dog_v * (g * sg)).astype(BF16))
            qs.append(q_ref[:, _head_lanes(hh)])
            totals.append(t_ref[hh][:, 0:1])

        def step(grp, masked, carries):
            start = pl.multiple_of(grp * SB_WIDE, SB_WIDE)
            strict = _sb_strict(qi, grp) if masked else None
            out = []
            for hh in range(SB_HB):
                dq_acc, cp, ce = carries[hh]
                q, do = qs[hh], dos[hh]
                k = k_ref[pl.ds(start, SB_WIDE), _head_lanes(hh)]
                v = v_ref[pl.ds(start, SB_WIDE), _head_lanes(hh)]
                lb, lm = _sb_logits(q, k, strict)
                rest = []
                for t in range(SB_GROUP):
                    lm_t = lm[:, t * SB_T:(t + 1) * SB_T]
                    cp = cp + jnp.sum(lm_t, axis=1, keepdims=True)
                    rest.append((totals[hh] - cp) + _split_dot(lm_t, after, 2, True))
                a = jnp.exp(lb + jnp.concatenate(rest, axis=1))
                if masked:
                    a = jnp.where(strict, a, 0.0)
                e = a * _dot(do, v, "nt")
                excl = []
                for t in range(SB_GROUP):
                    e_t = e[:, t * SB_T:(t + 1) * SB_T]
                    excl.append(ce + _split_dot(e_t, before, 2, True))
                    ce = ce + jnp.sum(e_t, axis=1, keepdims=True)
                eex = jnp.concatenate(excl, axis=1)
                if masked:
                    eex = jnp.where(strict, eex, 0.0)
                sig = jnp.exp(lb)
                dz = (e * (1.0 - sig) - eex * sig) * SB_SCALE
                dv_ref[pl.ds(start, SB_WIDE), _head_lanes(hh)] += _dot(a, do, "tn")
                dk_ref[pl.ds(start, SB_WIDE), _head_lanes(hh)] += _dot(dz, q, "tn")
                out.append((dq_acc + _dot(dz, k), cp, ce))
            return tuple(out)

        zero = jnp.zeros((SB_T, 1), F32)
        init = tuple((jnp.zeros((SB_T, SB_HEAD_DIM), F32), zero, zero) for _ in range(SB_HB))
        carries = lax.fori_loop(0, top, lambda i, c: step(i, False, c), init)
        carries = step(top, True, carries)
        for hh in range(SB_HB):
            dq_ref[:, _head_lanes(hh)] = carries[hh][0]

        @pl.when(qi == nq - 1)
        def _():
            dvb_ref[...] = dv_ref[...].astype(BF16)

    wide = SB_HB * SB_HEAD_DIM
    qb = pl.BlockSpec((SB_T, wide), lambda h, i: (i, h))
    kv = pl.BlockSpec((s_dim, wide), lambda h, i: (0, h))
    outs, rode = _pcall(
        body, grid=(SB_N_HEADS // SB_HB, nq),
        in_specs=[qb, kv, kv, pl.BlockSpec((SB_T, wide), lambda h, i: (i, 3 * SB_N_HEADS // SB_HB + h)), qb,
                  pl.BlockSpec((SB_HB, SB_T, LANES), lambda h, i: (h, i, 0)), qb],
        out_specs=[qb, kv, kv, kv, qb],
        out_shape=[jax.ShapeDtypeStruct((s_dim, SB_WIDTH), F32), jax.ShapeDtypeStruct((s_dim, SB_WIDTH), F32),
                   jax.ShapeDtypeStruct((s_dim, SB_WIDTH), F32), jax.ShapeDtypeStruct((s_dim, SB_WIDTH), BF16),
                   jax.ShapeDtypeStruct((s_dim, SB_WIDTH), BF16)],
        args=[qn, kn, vb, proj, o, tot, dog], sem=("parallel", "arbitrary"), name=name, rider=rider)
    return (outs, rode) if rider is not None else outs


def _adamw_math(w, g, m, v):
    m = ADAM_B1 * m + (1.0 - ADAM_B1) * g
    v = ADAM_B2 * v + (1.0 - ADAM_B2) * (g * g)
    m_hat = m / (1.0 - ADAM_B1 ** ADAM_STEP)
    v_hat = v / (1.0 - ADAM_B2 ** ADAM_STEP)
    delta = -ADAM_LR * (m_hat / (jnp.sqrt(v_hat) + ADAM_EPS) + ADAM_WD * w)
    return delta, m, v


def _row_block(rows, cols, itemsize=4, limit=1 << 20):
    tr = rows
    while tr * cols * itemsize > limit and tr % (2 * BF16_ROWS) == 0:
        tr //= 2
    return tr


def _divisor_block(rows, cols, itemsize=4, limit=2 << 20):
    best = BF16_ROWS
    for t in range(BF16_ROWS, rows + 1, BF16_ROWS):
        if rows % t == 0 and t * cols * itemsize <= limit:
            best = t
    return best


def _adamw(w, g, m, v, name):
    n, rows, cols = w.shape
    tr = rows if rows * cols * 4 <= (1 << 20) else _divisor_block(rows, cols, limit=1 << 20)

    def body(w_ref, g_ref, m_ref, v_ref, d_out, m_out, v_out):
        d, m_new, v_new = _adamw_math(w_ref[...], g_ref[...], m_ref[...], v_ref[...])
        d_out[...] = d
        m_out[...] = m_new
        v_out[...] = v_new

    blk = pl.BlockSpec((None, tr, cols), lambda i, j: (i, j, 0))
    return pl.pallas_call(
        body, grid=(n, rows // tr), in_specs=[blk] * 4, out_specs=[blk] * 3,
        out_shape=[jax.ShapeDtypeStruct(w.shape, F32)] * 3,
        compiler_params=_params("parallel", "parallel"), name=name,
    )(w, g, m, v)


_FLIPS = ((1, 0), (0, 1), (1, 1))


def _place():
    return lax.axis_index("x"), lax.axis_index("y"), lax.axis_index("c")


def _flip(v, f):
    return 1 - v if f else v


def _half_rows(ref, lead, hc, hr):
    return ref.at[(*lead, pl.ds(pl.multiple_of(hc * hr, BF16_ROWS), hr), slice(None))]


def _half_cols(ref, lead, hc, hw):
    return ref.at[(*lead, pl.ds(pl.multiple_of(hc * hw, LANES), hw))]


def _rows_of_chip(chip, r):
    return pl.ds(pl.multiple_of(chip * r, BF16_ROWS), r)


def _slot_half(gathered, shard_shape, chip, l, hc):
    r, c = shard_shape[1:]
    if len(gathered.shape) == 3:
        return _half_cols(gathered, (l, _rows_of_chip(chip, r)), hc, c // 2)
    return _half_rows(gathered, (chip, l), hc, r // 2)


def _shard_half(shard, stacked, l, hc):
    r, c = shard.shape[1:]
    return _half_cols(shard, (l, slice(None)), hc, c // 2) if stacked else _half_rows(shard, (l,), hc, r // 2)


def _remote(src, dst, send, recv, k, to):
    return pltpu.make_async_remote_copy(src_ref=src, dst_ref=dst, send_sem=send.at[k], recv_sem=recv.at[k], device_id=to,
                                        device_id_type=MESH)


def _comm_call(reads, writes, n_sems, phases, name):
    passed = [k for k, w in enumerate(writes) if not isinstance(w, jax.ShapeDtypeStruct)]
    n_rd = len(reads)

    def body(*refs):
        rd = refs[:n_rd]
        wr = refs[n_rd + len(passed):n_rd + len(passed) + len(writes)]
        send, recv = refs[-2:]
        for phase in phases:
            sends, arrivals = phase(rd, wr, send, recv)
            for cp in sends:
                cp.start()
            for cp in arrivals:
                cp.wait_recv()
            for cp in sends:
                cp.wait_send()

    return pl.pallas_call(
        body, in_specs=[_ANY] * (n_rd + len(passed)), out_specs=[_ANY] * len(writes),
        out_shape=[jax.ShapeDtypeStruct(w.shape, w.dtype) for w in writes],
        input_output_aliases={n_rd + pos: k for pos, k in enumerate(passed)},
        scratch_shapes=[pltpu.SemaphoreType.DMA((n_sems,)), pltpu.SemaphoreType.DMA((n_sems,))], name=name,
    )(*reads, *[writes[k] for k in passed])


def _ag_ici(pieces, names, base=0):
    def phase(shards, gathered, send, recv):
        x, y, c = _place()
        me = 2 * x + y
        sends, arrivals = [], []
        for k, (n, l) in enumerate(pieces):
            a = names.index(n)
            shape = shards[a].shape
            src = _shard_half(shards[a], len(gathered[a].shape) == 3, l, c)
            for j, (fx, fy) in enumerate(_FLIPS):
                tx, ty = _flip(x, fx), _flip(y, fy)
                sends.append(_remote(src, _slot_half(gathered[a], shape, me, l, c), send, recv, base + 3 * k + j, (tx, ty, c)))
                arrivals.append(_remote(src, _slot_half(gathered[a], shape, 2 * tx + ty, l, c), send, recv, base + 3 * k + j, (tx, ty, c)))
        return sends, arrivals

    return phase


def _ag_pass_on(pieces, names, shapes, base=0):
    def phase(_, gathered, send, recv):
        x, y, c = _place()
        sibling = (x, y, 1 - c)
        sends, arrivals = [], []
        for k, (n, l) in enumerate(pieces):
            a = names.index(n)
            for j, (fx, fy) in enumerate(_FLIPS):
                chip = 2 * _flip(x, fx) + _flip(y, fy)
                landed = _slot_half(gathered[a], shapes[a], chip, l, c)
                sends.append(_remote(landed, landed, send, recv, base + 3 * k + j, sibling))
                arrivals.append(_remote(landed, _slot_half(gathered[a], shapes[a], chip, l, 1 - c), send, recv, base + 3 * k + j, sibling))
        return sends, arrivals

    return phase


def _other_half(ref, hc):
    if len(ref.shape) == 3:
        return _half_cols(ref, (slice(None), slice(None)), hc, ref.shape[2] // 2)
    return _half_rows(ref, (slice(None), slice(None)), hc, ref.shape[2] // 2)


def _half_shape(shape):
    return shape[:2] + (shape[2] // 2,) if len(shape) == 3 else shape[:2] + (shape[2] // 2, shape[3])


def _exchange_phase(n_arr):
    def phase(ins, outs, send, recv):
        x, y, c = _place()
        cps = [_remote(_other_half(ins[a], 1 - c), outs[a], send, recv, a, (x, y, 1 - c)) for a in range(n_arr)]
        return cps, cps

    return phase


def _exchange_outs(grads):
    return [jax.ShapeDtypeStruct(_half_shape(g.shape), g.dtype) for g in grads]


def _pair_exchange(grads, name):
    return _comm_call(grads, _exchange_outs(grads), len(grads), [_exchange_phase(len(grads))], name)


def _exchange_rider(grads):
    return _Rider(grads, _exchange_outs(grads), len(grads), _exchange_phase(len(grads)))


def _pair_sum_stacked(g, got, place, name):
    _, rows, hw = got.shape
    tr = _divisor_block(rows, hw)

    def body(place_ref, g_ref, r_ref, o_ref):
        o_ref[...] = (g_ref[...].astype(F32) + r_ref[...].astype(F32)).astype(o_ref.dtype)

    blk = pl.BlockSpec((None, tr, hw), lambda i, pr: (0, i, 0))
    return pl.pallas_call(
        body,
        grid_spec=pltpu.PrefetchScalarGridSpec(
            num_scalar_prefetch=1, grid=(rows // tr,),
            in_specs=[pl.BlockSpec((None, tr, hw), lambda i, pr: (0, i, pr[1])), blk], out_specs=blk),
        out_shape=jax.ShapeDtypeStruct(got.shape, BF16),
        compiler_params=_params("parallel"), name=name,
    )(place, g, got)


def _pair_sum(g, got, place, name):
    if len(g.shape) == 3:
        return _pair_sum_stacked(g, got, place, name)
    _, layers, hr, cols = got.shape
    tr = _row_block(hr, cols)
    per = hr // tr

    def body(place_ref, g_ref, r_ref, o_ref):
        o_ref[...] = (g_ref[...].astype(F32) + r_ref[...].astype(F32)).astype(o_ref.dtype)

    blk = pl.BlockSpec((None, None, tr, cols), lambda k, l, i, pr: (k, l, i, 0))
    return pl.pallas_call(
        body,
        grid_spec=pltpu.PrefetchScalarGridSpec(
            num_scalar_prefetch=1, grid=(4, layers, per),
            in_specs=[pl.BlockSpec((None, None, tr, cols), lambda k, l, i, pr: (k, l, pr[1] * per + i, 0)), blk],
            out_specs=blk),
        out_shape=jax.ShapeDtypeStruct(got.shape, BF16),
        compiler_params=_params("parallel", "parallel", "parallel"), name=name,
    )(place, g, got)


def _scatter_phase(n_arr):
    def phase(ins, outs, send, recv):
        x, y, c = _place()
        cps = []
        for a in range(n_arr):
            for j, (fx, fy) in enumerate(_FLIPS):
                tx, ty = _flip(x, fx), _flip(y, fy)
                if len(ins[a].shape) == 3:
                    src = ins[a].at[:, _rows_of_chip(2 * tx + ty, ins[a].shape[1] // 4), :]
                else:
                    src = ins[a].at[2 * tx + ty]
                cps.append(_remote(src, outs[a].at[j], send, recv, 3 * a + j, (tx, ty, c)))
        return cps, cps

    return phase


def _scatter_outs(pairs):
    return [jax.ShapeDtypeStruct((3, 1, p.shape[1] // 4, p.shape[2]) if len(p.shape) == 3 else (3,) + p.shape[1:], p.dtype) for p in pairs]


def _chip_scatter(pairs, name):
    return _comm_call(pairs, _scatter_outs(pairs), 3 * len(pairs), [_scatter_phase(len(pairs))], name)


def _scatter_rider(pairs):
    return _Rider(pairs, _scatter_outs(pairs), 3 * len(pairs), _scatter_phase(len(pairs)))


def _chip_sum_stacked(p, got, place, layer, layers, o_buf, name):
    _, r, hw = got.shape[1:]
    tr = _divisor_block(r, hw)
    per = r // tr

    def body(place_ref, p_ref, r_ref, *rest):
        o_ref = rest[-1]
        acc = p_ref[...].astype(F32)
        for j in range(3):
            acc = acc + r_ref[j].astype(F32)
        o_ref[...] = acc

    has_buf = o_buf is not None
    return pl.pallas_call(
        body,
        grid_spec=pltpu.PrefetchScalarGridSpec(
            num_scalar_prefetch=1, grid=(per,),
            in_specs=[pl.BlockSpec((None, tr, hw), lambda i, pr: (0, pr[0] * per + i, 0)),
                      pl.BlockSpec((3, None, tr, hw), lambda i, pr: (0, 0, i, 0))] + ([_ANY] if has_buf else []),
            out_specs=pl.BlockSpec((None, tr, hw), lambda i, pr: (layer, i, pr[1]))),
        out_shape=jax.ShapeDtypeStruct((layers, r, 2 * hw), F32),
        input_output_aliases={3: 0} if has_buf else {},
        compiler_params=_params("parallel"), name=name,
    )(*((place, p, got) + ((o_buf,) if has_buf else ())))


def _chip_sum(p, got, place, layer, layers, o_buf, name):
    if len(p.shape) == 3:
        return _chip_sum_stacked(p, got, place, layer, layers, o_buf, name)
    _, _, hr, cols = p.shape
    tr = _row_block(hr, cols)
    per = hr // tr

    def body(place_ref, p_ref, r_ref, *rest):
        o_ref = rest[-1]
        acc = p_ref[...].astype(F32)
        for j in range(3):
            acc = acc + r_ref[j].astype(F32)
        o_ref[...] = acc

    has_buf = o_buf is not None
    return pl.pallas_call(
        body,
        grid_spec=pltpu.PrefetchScalarGridSpec(
            num_scalar_prefetch=1, grid=(per,),
            in_specs=[pl.BlockSpec((None, None, tr, cols), lambda i, pr: (pr[0], 0, i, 0)),
                      pl.BlockSpec((3, None, tr, cols), lambda i, pr: (0, 0, i, 0))] + ([_ANY] if has_buf else []),
            out_specs=pl.BlockSpec((None, tr, cols), lambda i, pr: (layer, pr[1] * per + i, 0))),
        out_shape=jax.ShapeDtypeStruct((layers, 2 * hr, cols), F32),
        input_output_aliases={3: 0} if has_buf else {},
        compiler_params=_params("parallel"), name=name,
    )(*((place, p, got) + ((o_buf,) if has_buf else ())))


def _pair_gather(halves, by_cols):
    def phase(_, bufs, send, recv):
        x, y, c = _place()
        sends, arrivals = [], []
        for a, h in enumerate(halves):
            cut = (lambda hc, a=a, h=h: _half_cols(bufs[a], (slice(None), slice(None)), hc, h.shape[2] // 2)) if by_cols[a] else (
                lambda hc, a=a, h=h: _half_rows(bufs[a], (slice(None),), hc, h.shape[1] // 2))
            sends.append(_remote(cut(c), cut(c), send, recv, a, (x, y, 1 - c)))
            arrivals.append(_remote(cut(c), cut(1 - c), send, recv, a, (x, y, 1 - c)))
        return sends, arrivals

    return _comm_call([], halves, len(halves), [phase], "rs_pair_gather")


def _allreduce_small(v, name):
    rows, cols = v.shape

    def body(v_ref, o_ref, buf, send_sems, recv_sems):
        x, y, c = _place()
        me = 4 * x + 2 * y + c
        buf[0] = v_ref[...]
        cps = []
        for k in range(1, 8):
            kx, ky, kc = (k >> 2) & 1, (k >> 1) & 1, k & 1
            cp = pltpu.make_async_remote_copy(src_ref=v_ref, dst_ref=buf.at[k], send_sem=send_sems.at[k - 1], recv_sem=recv_sems.at[k - 1],
                                              device_id=(_flip(x, kx), _flip(y, ky), _flip(c, kc)), device_id_type=MESH)
            cp.start()
            cps.append(cp)
        for cp in cps:
            cp.wait()
        acc = buf[me]
        for d in range(1, 8):
            acc = acc + buf[jnp.bitwise_xor(d, me)]
        o_ref[...] = acc

    vm = pl.BlockSpec(memory_space=pltpu.VMEM)
    return pl.pallas_call(
        body, in_specs=[vm], out_specs=vm, out_shape=jax.ShapeDtypeStruct((rows, cols), F32),
        scratch_shapes=[pltpu.VMEM((8, rows, cols), F32), pltpu.SemaphoreType.DMA((7,)), pltpu.SemaphoreType.DMA((7,))],
        name=name,
    )(v)


def _pad_lanes(a):
    return jnp.pad(a, ((0, 0), (0, LANES - a.shape[1])))


def _group_lanes(v):
    return jnp.pad(v.reshape(SSD_N_GROUPS, 1, 8), ((0, 0), (0, 0), (0, LANES - 8)))


def kernel(x, p, norm_w, ssd_in_w, ssd_conv_w, ssd_conv_b, ssd_dt_bias, ssd_a_log, ssd_d, ssd_gnorm_w, ssd_out_w, sb_in_w, sb_qn_w, sb_kn_w, sb_out_w, ple_norm_w, ple_gate_w, ple_proj_w, loss_target, m_norm_w, m_ssd_in_w, m_ssd_conv_w, m_ssd_conv_b, m_ssd_dt_bias, m_ssd_a_log, m_ssd_d, m_ssd_gnorm_w, m_ssd_out_w, m_sb_in_w, m_sb_qn_w, m_sb_kn_w, m_sb_out_w, m_ple_norm_w, m_ple_gate_w, m_ple_proj_w, v_norm_w, v_ssd_in_w, v_ssd_conv_w, v_ssd_conv_b, v_ssd_dt_bias, v_ssd_a_log, v_ssd_d, v_ssd_gnorm_w, v_ssd_out_w, v_sb_in_w, v_sb_qn_w, v_sb_kn_w, v_sb_out_w, v_ple_norm_w, v_ple_gate_w, v_ple_proj_w):
    w_in = dict(norm_w=norm_w, ssd_in_w=ssd_in_w, ssd_conv_w=ssd_conv_w, ssd_conv_b=ssd_conv_b, ssd_dt_bias=ssd_dt_bias,
                ssd_a_log=ssd_a_log, ssd_d=ssd_d, ssd_gnorm_w=ssd_gnorm_w, ssd_out_w=ssd_out_w, sb_in_w=sb_in_w, sb_qn_w=sb_qn_w,
                sb_kn_w=sb_kn_w, sb_out_w=sb_out_w, ple_norm_w=ple_norm_w, ple_gate_w=ple_gate_w, ple_proj_w=ple_proj_w)
    m_in = dict(norm_w=m_norm_w, ssd_in_w=m_ssd_in_w, ssd_conv_w=m_ssd_conv_w, ssd_conv_b=m_ssd_conv_b, ssd_dt_bias=m_ssd_dt_bias,
                ssd_a_log=m_ssd_a_log, ssd_d=m_ssd_d, ssd_gnorm_w=m_ssd_gnorm_w, ssd_out_w=m_ssd_out_w, sb_in_w=m_sb_in_w,
                sb_qn_w=m_sb_qn_w, sb_kn_w=m_sb_kn_w, sb_out_w=m_sb_out_w, ple_norm_w=m_ple_norm_w, ple_gate_w=m_ple_gate_w,
                ple_proj_w=m_ple_proj_w)
    v_in = dict(norm_w=v_norm_w, ssd_in_w=v_ssd_in_w, ssd_conv_w=v_ssd_conv_w, ssd_conv_b=v_ssd_conv_b, ssd_dt_bias=v_ssd_dt_bias,
                ssd_a_log=v_ssd_a_log, ssd_d=v_ssd_d, ssd_gnorm_w=v_ssd_gnorm_w, ssd_out_w=v_ssd_out_w, sb_in_w=v_sb_in_w,
                sb_qn_w=v_sb_qn_w, sb_kn_w=v_sb_kn_w, sb_out_w=v_sb_out_w, ple_norm_w=v_ple_norm_w, ple_gate_w=v_ple_gate_w,
                ple_proj_w=v_ple_proj_w)
    ix, iy, ic = lax.axis_index("x"), lax.axis_index("y"), lax.axis_index("c")
    chip = (2 * ix + iy).astype(jnp.int32)
    place = jnp.stack([chip, ic.astype(jnp.int32)])
    zero = jnp.zeros((), jnp.int32)
    big_names = [n for n, _, _ in _BIG]
    layers_of = {n: s[0] for n, s, _ in _BIG}
    cut_of = {n: cut for n, _, cut in _BIG}

    def layer_pieces(i):
        mixer = ("ssd_in_w", "ssd_out_w") if i % 2 == 0 else ("sb_in_w", "sb_out_w")
        return [(mixer[0], i // 2), (mixer[1], i // 2), ("ple_gate_w", i), ("ple_proj_w", i)]

    def names_of(pieces):
        return [n for n in big_names if any(n == q for q, _ in pieces)]

    held = lambda n, a: a.transpose(0, 2, 1) if cut_of[n] == "stack" else a
    mine = {n: held(n, w_in[n]).astype(BF16) for n in big_names}
    shard_shapes = [mine[n].shape for n in big_names]
    room = [jax.ShapeDtypeStruct((s[0], 4 * s[1], s[2]) if cut_of[n] == "stack" else (4,) + s, BF16) for n, s in zip(big_names, shard_shapes)]
    first = layer_pieces(0)[:1]
    gathered = _comm_call([mine[n] for n in big_names], room, 6 * len(first),
                          [_ag_ici(first, big_names), _ag_pass_on(first, big_names, shard_shapes, base=3 * len(first))], "allgather_layer0")
    gw = {}
    for n, g in zip(big_names, gathered):
        if cut_of[n] == "stack":
            layers, r, c = mine[n].shape
            gw[n] = lax.dynamic_update_slice(g.reshape(layers, 4, r, c), mine[n][:, None], (zero, chip, zero, zero)).reshape(g.shape)
        else:
            gw[n] = lax.dynamic_update_slice(g, mine[n][None], (chip, zero, zero, zero))

    lp = [layer_pieces(i) for i in range(DEPTH)]
    carries = {
        "ssd_in_0": (lp[0][1:], []), "ssd_0": (lp[1][:1], lp[0][1:]), "ssd_out_0": (lp[1][1:], lp[1][:1]),
        "sb_in_1": (lp[2][2:], lp[1][1:]), "sb_1": (lp[2][:2], []), "sb_out_1": ([], lp[2][:2]),
        "ssd_in_2": (lp[3][1:], lp[2][2:]), "ssd_2": (lp[3][:1], []), "ssd_out_2": ([], lp[3][:1]),
        "sb_in_3": ([], lp[3][1:]),
    }

    def gather_rider(call):
        if call not in carries:
            return None, lambda outs: outs
        ici, passing = carries[call]
        names = names_of(ici + passing)
        phases = ([_ag_ici(ici, names)] if ici else []) + (
            [_ag_pass_on(passing, names, [mine[n].shape for n in names], base=3 * len(ici))] if passing else [])

        def issue(rd, wr, send, recv):
            both = [ph(rd, wr, send, recv) for ph in phases]
            return sum((b[0] for b in both), []), sum((b[1] for b in both), [])

        def land(outs):
            outs, bufs = outs
            for n, g in zip(names, bufs):
                gw[n] = g
            return outs

        return _Rider([mine[n] for n in names], [gw[n] for n in names], 3 * (len(ici) + len(passing)), issue), land

    onehot = (jnp.arange(4) == chip).astype(F32) * (ic == 0).astype(F32)
    cw_mine = onehot[:, None, None, None] * ssd_conv_w[None]
    cw_full = _allreduce_small(cw_mine.transpose(1, 2, 0, 3).reshape(-1, LANES), "gather_conv_w").reshape(2, SSD_D_CONV, SSD_CONV_DIM)

    def wmm(a, name, layer, *, dn="nn", res=None, call, rider=None):
        return _matmul(a, gw[name], dn=dn, res=res, b_lay=(cut_of[name], layer), name=call, rider=rider)

    h = x[0]
    target = loss_target[0]
    saved = []
    for i in range(DEPTH):
        j = i // 2
        nw = norm_w[i:i + 1]
        pw = ple_norm_w[i:i + 1]
        s = dict(h=h)
        u = _rms_fwd(h, nw, f"rms_{i}")
        s["u"] = u
        if i % 2 == 0:
            w_dt = jnp.pad(gw["ssd_in_w"][j, SSD_ZX:], ((0, LANES - SSD_N_HEADS), (0, 0)))
            rider, land = gather_rider(f"ssd_in_{i}")
            pzx = land(_matmul(u, gw["ssd_in_w"], dn="nt", b_lay=("stack", j, SSD_ZX), name=f"ssd_in_{i}", rider=rider))
            pdt = _matmul(u, w_dt, dn="nt", name=f"ssd_indt_{i}")
            act = _conv_fwd(pzx, cw_full[j], ssd_conv_b[j:j + 1], f"conv_{i}")
            dtg = jnp.pad(pdt[:, :SSD_N_HEADS].reshape(-1, SSD_N_GROUPS, 8).transpose(1, 0, 2), ((0, 0), (0, 0), (0, LANES - 8)))
            vecs = (_group_lanes(ssd_dt_bias[j]), _group_lanes(ssd_a_log[j]), _group_lanes(ssd_d[j]))
            rider, land = gather_rider(f"ssd_{i}")
            yn, states = land(_ssd_fwd(act, dtg, *vecs, pzx, ssd_gnorm_w[j:j + 1], f"ssd_{i}", rider=rider))
            s.update(w_dt=w_dt, pzx=pzx, act=act, dtg=dtg, vecs=vecs, yn=yn, states=states)
            rider, land = gather_rider(f"ssd_out_{i}")
            h1 = land(wmm(yn, "ssd_out_w", j, res=h, call=f"ssd_out_{i}", rider=rider))
        else:
            rider, land = gather_rider(f"sb_in_{i}")
            proj = land(wmm(u, "sb_in_w", j, call=f"sb_in_{i}", rider=rider))
            qn, kn, vb = _qknorm_fwd(proj, sb_qn_w[j:j + 1], sb_kn_w[j:j + 1], f"qknorm_{i}")
            rider, land = gather_rider(f"sb_{i}")
            og, o, tot = land(_sb_fwd(qn, kn, vb, proj, f"sb_{i}", rider=rider))
            s.update(proj=proj, qn=qn, kn=kn, vb=vb, og=og, o=o, tot=tot)
            rider, land = gather_rider(f"sb_out_{i}")
            h1 = land(wmm(og, "sb_out_w", j, res=h, call=f"sb_out_{i}", rider=rider))
        n2 = _rms_fwd(h1, pw, f"ple_rms_{i}")
        gl = wmm(n2, "ple_gate_w", i, call=f"ple_gate_{i}")
        pp = wmm(p[i, 0], "ple_proj_w", i, call=f"ple_proj_{i}")
        h = _ple_fwd(h1, pp, gl, f"ple_{i}")
        s.update(h1=h1, n2=n2, gl=gl, pp=pp)
        saved.append(s)

    dh, loss_lanes = _loss_bwd(h, target, "loss")

    wg = {}
    gsmall = {n: [None] * s[0] for n, s in _SMALL}
    g_conv_w = [None, None]
    scat = {}
    pending = late = None

    def wgrad(a, b, name, layer, call, rider=None):
        out = _matmul(a, b, dn="tn", out_dtype=BF16, o_lay=(cut_of[name], 0, 1), name=call, rider=rider)
        wg[(name, layer)], rode = out if rider is not None else (out, None)
        return rode

    def pair_sums(pieces, got, tag):
        return pieces, [_pair_sum(wg[q], r, place, f"rs_pair_sum_{tag}_{k}") for k, (q, r) in enumerate(zip(pieces, got))]

    def sibling_rider(pieces):
        return _exchange_rider([wg[q] for q in pieces])

    def riding_with(own):
        return (pending[0] + own[0], pending[1] + own[1]) if pending else own

    def arrived(sent, got):
        for q, pair, g in zip(sent[0], sent[1], got):
            scat[q] = (pair, g)

    for i in reversed(range(DEPTH)):
        j = i // 2
        s = saved[i]
        nw = norm_w[i:i + 1]
        pw = ple_norm_w[i:i + 1]
        dpp, dgl = _ple_bwd(dh, s["pp"], s["gl"], f"ple_bwd_{i}")
        wgrad(p[i, 0], dpp, "ple_proj_w", i, f"d_ple_proj_{i}")
        if late is None:
            wgrad(s["n2"], dgl, "ple_gate_w", i, f"d_ple_gate_{i}")
        else:
            pending = pair_sums(late, wgrad(s["n2"], dgl, "ple_gate_w", i, f"d_ple_gate_{i}", rider=sibling_rider(late)), f"{i + 1}_in")
        dn2 = wmm(dgl, "ple_gate_w", i, dn="nt", call=f"ple_gate_bwd_{i}")
        dh1, dpw = _rms_bwd(s["h1"], pw, dn2, dh, f"ple_rms_bwd_{i}")
        gsmall["ple_norm_w"][i] = dpw
        if i % 2 == 0:
            wgrad(s["yn"], dh1, "ssd_out_w", j, f"d_ssd_out_{i}")
            early = layer_pieces(i)[1:]
            dyn, got = wmm(dh1, "ssd_out_w", j, dn="nt", call=f"ssd_out_bwd_{i}", rider=sibling_rider(early))
            riding = riding_with(pair_sums(early, got, f"{i}_out"))
            outs, got = _ssd_bwd(s["act"], s["dtg"], *s["vecs"], s["pzx"], ssd_gnorm_w[j:j + 1], s["states"], dyn, f"ssd_bwd_{i}",
                                 rider=_scatter_rider(riding[1]))
            arrived(riding, got)
            dxs, dbm, dcm, ddtg, dbias, dalog, ddsk, dz, dgw = outs
            dzx, dcw, dcb = _conv_bwd(s["pzx"], cw_full[j], ssd_conv_b[j:j + 1], dxs, dbm, dcm, dz, f"conv_bwd_{i}")
            ddt = _pad_lanes(ddtg[:, :, :8].transpose(1, 0, 2).reshape(-1, SSD_N_HEADS)).astype(BF16)
            du = _matmul(dzx, gw["ssd_in_w"], b_lay=("stack", j, SSD_ZX), name=f"ssd_in_bwd_{i}")
            du = _matmul(ddt, s["w_dt"], res=du, name=f"ssd_indt_bwd_{i}")
            dwt = _matmul(dzx, s["u"], dn="tn", out_dtype=BF16, out_rows=SSD_IN_DIM, name=f"d_ssd_in_{i}")
            dwt_dt = _matmul(ddt, s["u"], dn="tn", out_dtype=BF16, name=f"d_ssd_indt_{i}")
            wg[("ssd_in_w", j)] = lax.dynamic_update_slice(dwt, dwt_dt[:SSD_N_HEADS], (SSD_ZX, 0))[None]
            g_conv_w[j] = dcw
            gsmall["ssd_conv_b"][j] = dcb
            gsmall["ssd_dt_bias"][j] = dbias[:, 0, :8].reshape(1, SSD_N_HEADS)
            gsmall["ssd_a_log"][j] = dalog[:, 0, :8].reshape(1, SSD_N_HEADS)
            gsmall["ssd_d"][j] = ddsk[:, 0, :8].reshape(1, SSD_N_HEADS)
            gsmall["ssd_gnorm_w"][j] = dgw
        else:
            wgrad(s["og"], dh1, "sb_out_w", j, f"d_sb_out_{i}")
            early = layer_pieces(i)[1:]
            dog, got = wmm(dh1, "sb_out_w", j, dn="nt", call=f"sb_out_bwd_{i}", rider=sibling_rider(early))
            riding = riding_with(pair_sums(early, got, f"{i}_out"))
            outs, got = _sb_bwd(s["qn"], s["kn"], s["vb"], s["proj"], s["o"], s["tot"], dog, f"sb_bwd_{i}", rider=_scatter_rider(riding[1]))
            arrived(riding, got)
            dqn, dkn, _, dvb, dg = outs
            dq, dk, dqw, dkw = _qknorm_bwd(s["proj"], sb_qn_w[j:j + 1], sb_kn_w[j:j + 1], dqn, dkn, f"qknorm_bwd_{i}")
            dproj = jnp.concatenate([dq, dk, dvb, dg], axis=1)
            du = wmm(dproj, "sb_in_w", j, dn="nt", call=f"sb_in_bwd_{i}")
            wgrad(s["u"], dproj, "sb_in_w", j, f"d_sb_in_{i}")
            gsmall["sb_qn_w"][j] = dqw
            gsmall["sb_kn_w"][j] = dkw
        dh, dnw = _rms_bwd(s["h"], nw, du, dh1, f"rms_bwd_{i}")
        gsmall["norm_w"][i] = dnw
        late = layer_pieces(i)[:1]
    grad_x = dh[None]
    pending = pair_sums(late, _pair_exchange([wg[q] for q in late], "rs_pair_exchange_last"), "0_in")
    arrived(pending, _chip_scatter(pending[1], "rs_chip_scatter_last"))

    halves = []
    for n in big_names:
        buf = None
        for l in range(layers_of[n]):
            buf = _chip_sum(*scat[(n, l)], place, l, layers_of[n], buf, f"rs_chip_sum_{n}_{l}")
        halves.append(buf)
    g_big = dict(zip(big_names, _pair_gather(halves, [cut_of[n] == "stack" for n in big_names])))

    small_parts = [jnp.concatenate(gsmall[n], axis=0).reshape(-1) for n, _ in _SMALL]
    small_parts.append(jnp.stack(g_conv_w).reshape(-1))
    small_parts.append(loss_lanes.reshape(-1))
    small_sum = _allreduce_small(jnp.concatenate(small_parts).reshape(-1, LANES), "allreduce_small").reshape(-1)
    g_small, off = {}, 0
    for n, shape in _SMALL:
        size = math.prod(shape)
        g_small[n] = small_sum[off:off + size].reshape(shape)
        off += size
    cw_size = 2 * SSD_D_CONV * SSD_CONV_DIM
    g_cw_full = small_sum[off:off + cw_size].reshape(2, SSD_D_CONV, 4, SSD_CONV_DIM // 4)
    g_small["ssd_conv_w"] = jnp.sum(g_cw_full * (jnp.arange(4) == chip).astype(F32)[None, None, :, None], axis=2)
    loss = 0.5 * jnp.sum(small_sum[off + cw_size:]) / D_MODEL

    grads, delta, new_m, new_v = {}, {}, {}, {}
    for n in big_names:
        grads[n], delta[n], new_m[n], new_v[n] = (
            held(n, a) for a in (g_big[n], *_adamw(held(n, w_in[n]), g_big[n], held(n, m_in[n]), held(n, v_in[n]), f"adamw_{n}")))
    small_names = [n for n, _ in _SMALL] + ["ssd_conv_w"]
    pack = lambda d: jnp.concatenate([d[n].reshape(-1) for n in small_names]).reshape(1, -1, LANES)
    ds, ms, vs = _adamw(pack(w_in), pack(g_small), pack(m_in), pack(v_in), "adamw_small")
    off = 0
    for n in small_names:
        shape = w_in[n].shape
        size = math.prod(shape)
        grads[n] = g_small[n]
        delta[n] = ds.reshape(-1)[off:off + size].reshape(shape)
        new_m[n] = ms.reshape(-1)[off:off + size].reshape(shape)
        new_v[n] = vs.reshape(-1)[off:off + size].reshape(shape)
        off += size

    order = ["norm_w", "ssd_in_w", "ssd_conv_w", "ssd_conv_b", "ssd_dt_bias", "ssd_a_log", "ssd_d", "ssd_gnorm_w", "ssd_out_w",
             "sb_in_w", "sb_qn_w", "sb_kn_w", "sb_out_w", "ple_norm_w", "ple_gate_w", "ple_proj_w"]
    return (loss, grad_x, *[grads[n] for n in order], *[delta[n] for n in order], *[new_m[n] for n in order],
            *[new_v[n] for n in order])
```

```python
import functools
import math

import jax
import jax.numpy as jnp
from jax import lax
from jax.experimental import pallas as pl
from jax.experimental.pallas import tpu as pltpu

F32 = jnp.float32
BF16 = jnp.bfloat16
MESH = pl.DeviceIdType.MESH

D_MODEL = 2048
DEPTH = 4
SSD_D_INNER = 4096
SSD_N_GROUPS = 8
SSD_GROUP_W = SSD_D_INNER // SSD_N_GROUPS
SSD_D_STATE = 128
SSD_CHUNK = 128
SSD_CONV_DIM = 6144
SSD_D_CONV = 4
SSD_N_HEADS = 64
SB_HEAD_DIM = 128
SB_N_HEADS = 16
SB_WIDTH = 2048
NORM_EPS = 1e-6
GATED_NORM_EPS = 1e-5
ADAM_LR = 0.001
ADAM_B1 = 0.9
ADAM_B2 = 0.999
ADAM_EPS = 1e-08
ADAM_WD = 0.01
ADAM_STEP = 10

SSD_ZX = SSD_D_INNER + SSD_CONV_DIM
SSD_IN_DIM = SSD_ZX + SSD_N_HEADS
LANES = 128
BF16_ROWS = 16

_BIG = (
    ("ssd_in_w", (2, 2576, 2048), "stack"),
    ("ssd_out_w", (2, 1024, 2048), "row"),
    ("sb_in_w", (2, 2048, 2048), "col"),
    ("sb_out_w", (2, 512, 2048), "row"),
    ("ple_gate_w", (4, 512, 2048), "row"),
    ("ple_proj_w", (4, 256, 512), "col"),
)
_SMALL = (
    ("norm_w", (4, 2048)),
    ("ssd_conv_b", (2, 6144)),
    ("ssd_dt_bias", (2, 64)),
    ("ssd_a_log", (2, 64)),
    ("ssd_d", (2, 64)),
    ("ssd_gnorm_w", (2, 4096)),
    ("sb_qn_w", (2, 128)),
    ("sb_kn_w", (2, 128)),
    ("ple_norm_w", (4, 2048)),
)

_DN = {
    "nn": (((1,), (0,)), ((), ())),
    "nt": (((1,), (1,)), ((), ())),
    "tn": (((0,), (0,)), ((), ())),
}


def _dot(a, b, dn="nn"):
    return lax.dot_general(a.astype(BF16), b.astype(BF16), _DN[dn], preferred_element_type=F32)


@functools.partial(jax.custom_vjp, nondiff_argnums=(2,))
def _gdot(a, b, dn):
    return _dot(a, b, dn)


def _gdot_fwd(a, b, dn):
    return _dot(a, b, dn), (a, b)


def _gdot_bwd(dn, res, g):
    a, b = res
    if dn == "nn":
        return _dot(g, b, "nt"), _dot(a, g, "tn")
    if dn == "nt":
        return _dot(g, b, "nn"), _dot(g, a, "tn")
    return _dot(b, g, "nt"), _dot(a, g, "nn")


_gdot.defvjp(_gdot_fwd, _gdot_bwd)


def _split_dot(x, t, parts, x_left):
    acc = None
    r = x
    for i in range(parts):
        p = r.astype(BF16)
        d = lax.dot_general(p, t, _DN["nn"], preferred_element_type=F32) if x_left else lax.dot_general(
            t, p, _DN["nn"], preferred_element_type=F32)
        acc = d if acc is None else acc + d
        if i + 1 < parts:
            r = r - p.astype(F32)
    return acc


def _tri(n, lower, strict=False):
    r = lax.broadcasted_iota(jnp.int32, (n, n), 0)
    c = lax.broadcasted_iota(jnp.int32, (n, n), 1)
    keep = (r > c if strict else r >= c) if lower else (r < c if strict else r <= c)
    return jnp.where(keep, 1.0, 0.0).astype(BF16)


def _cumsum_rows_raw(x):
    return _split_dot(x, _tri(x.shape[0], True), 3, False)


@jax.custom_vjp
def _cumsum_rows(x):
    return _cumsum_rows_raw(x)


def _cumsum_rows_fwd(x):
    return _cumsum_rows_raw(x), None


def _cumsum_rows_bwd(_, g):
    return (_split_dot(g, _tri(g.shape[0], False), 3, False),)


_cumsum_rows.defvjp(_cumsum_rows_fwd, _cumsum_rows_bwd)


def _sigmoid(x):
    return 1.0 / (1.0 + jnp.exp(-x))


def _softplus(x):
    return jnp.maximum(x, 0.0) + jnp.log(1.0 + jnp.exp(-jnp.abs(x)))


def _rms(x, w, eps):
    return x * lax.rsqrt(jnp.mean(x * x, axis=-1, keepdims=True) + eps) * w


_ANY = pl.BlockSpec(memory_space=pl.ANY)


def _params(*sem):
    return pltpu.CompilerParams(dimension_semantics=sem)


class _Rider:
    def __init__(self, reads, writes, n_sems, issue):
        self.reads, self.writes, self.n_sems, self.issue = list(reads), list(writes), n_sems, issue


def _pcall(body, *, grid, in_specs, out_specs, out_shape, args, sem, name, scratch_shapes=(), aliases=None, rider=None):
    aliases = dict(aliases or {})
    if rider is None:
        outs = pl.pallas_call(body, grid=grid, in_specs=in_specs, out_specs=out_specs, out_shape=out_shape,
                              scratch_shapes=list(scratch_shapes), input_output_aliases=aliases,
                              compiler_params=_params(*sem), name=name)(*args)
        return list(outs), []
    n_in, n_out, n_scr, n_rd, n_wr = len(args), len(out_shape), len(scratch_shapes), len(rider.reads), len(rider.writes)
    passed = [k for k, w in enumerate(rider.writes) if not isinstance(w, jax.ShapeDtypeStruct)]
    for pos, k in enumerate(passed):
        aliases[n_in + n_rd + pos] = n_out + k

    def wrapped(*refs):
        ins = refs[:n_in]
        reads = refs[n_in:n_in + n_rd]
        base = n_in + n_rd + len(passed)
        outs = refs[base:base + n_out]
        writes = refs[base + n_out:base + n_out + n_wr]
        scr = refs[base + n_out + n_wr:base + n_out + n_wr + n_scr]
        send, recv = refs[-2:]
        first = last = None
        for d, n in enumerate(grid):
            i = pl.program_id(d)
            first = (i == 0) if first is None else first & (i == 0)
            last = (i == n - 1) if last is None else last & (i == n - 1)

        @pl.when(first)
        def _():
            for cp in rider.issue(reads, writes, send, recv)[0]:
                cp.start()

        body(*ins, *outs, *scr)

        @pl.when(last)
        def _():
            sends, arrivals = rider.issue(reads, writes, send, recv)
            for cp in arrivals:
                cp.wait_recv()
            for cp in sends:
                cp.wait_send()

    outs = pl.pallas_call(
        wrapped, grid=grid,
        in_specs=list(in_specs) + [_ANY] * (n_rd + len(passed)),
        out_specs=list(out_specs) + [_ANY] * n_wr,
        out_shape=list(out_shape) + [jax.ShapeDtypeStruct(w.shape, w.dtype) for w in rider.writes],
        scratch_shapes=list(scratch_shapes) + [pltpu.SemaphoreType.DMA((rider.n_sems,)), pltpu.SemaphoreType.DMA((rider.n_sems,))],
        input_output_aliases=aliases, compiler_params=_params(*(["arbitrary"] * len(grid))), name=name,
    )(*args, *rider.reads, *[rider.writes[k] for k in passed])
    return list(outs[:n_out]), list(outs[n_out:])


MM_TK = 2048


def _pick(dim, pref, unit=None):
    t = pref
    while t >= LANES:
        if dim % t == 0 and (unit is None or unit % t == 0):
            return t
        t //= 2
    return dim


def _matmul(a, b, *, dn="nn", res=None, out_dtype=F32, name, b_lay=None, o_lay=None, o_buf=None, out_rows=None, rider=None):
    if dn == "tn":
        k_dim, m_dim = a.shape
    else:
        m_dim, k_dim = a.shape
    unit_m = unit_n = unit_k = None
    if b_lay is None:
        n_dim = b.shape[0] if dn == "nt" else b.shape[1]
    elif b_lay[0] == "stack":
        cut, layer, rows = b_lay
        cols = b.shape[2]
        n_dim = cols if dn == "nn" else rows
        assert k_dim == (rows if dn == "nn" else cols) and dn != "tn"
    else:
        cut, layer = b_lay
        r, c = b.shape[2:]
        rows, cols = (4 * r, c) if cut == "row" else (r, 4 * c)
        n_dim = cols if dn == "nn" else rows
        assert k_dim == (rows if dn == "nn" else cols) and dn != "tn"
        if (cut == "row") == (dn == "nn"):
            unit_k = r if cut == "row" else c
        else:
            unit_n = r if cut == "row" else c
    if o_lay is not None:
        o_cut, o_layer, o_layers = o_lay
        if o_cut == "row":
            unit_m = m_dim // 4
        else:
            unit_n = n_dim // 4
    tm, tn, tk = _pick(m_dim, 1024, unit_m), _pick(n_dim, 1024, unit_n), _pick(k_dim, MM_TK, unit_k)
    nk = k_dim // tk
    a_spec = pl.BlockSpec((tk, tm), lambda i, j, k: (k, i)) if dn == "tn" else pl.BlockSpec((tm, tk), lambda i, j, k: (i, k))
    if b_lay is None:
        b_spec = pl.BlockSpec((tn, tk), lambda i, j, k: (j, k)) if dn == "nt" else pl.BlockSpec((tk, tn), lambda i, j, k: (k, j))
    elif cut == "stack":
        b_spec = (pl.BlockSpec((None, tk, tn), lambda i, j, k: (layer, k, j)) if dn == "nn" else
                  pl.BlockSpec((None, tn, tk), lambda i, j, k: (layer, j, k)))
    elif dn == "nn" and cut == "row":
        per = r // tk
        b_spec = pl.BlockSpec((None, None, tk, tn), lambda i, j, k: (k // per, layer, k % per, j))
    elif dn == "nn":
        per = c // tn
        b_spec = pl.BlockSpec((None, None, tk, tn), lambda i, j, k: (j // per, layer, k, j % per))
    elif cut == "row":
        per = r // tn
        b_spec = pl.BlockSpec((None, None, tn, tk), lambda i, j, k: (j // per, layer, j % per, k))
    else:
        per = c // tk
        b_spec = pl.BlockSpec((None, None, tn, tk), lambda i, j, k: (k // per, layer, j, k % per))
    r_spec = pl.BlockSpec((tm, tn), lambda i, j, k: (i, j))
    if o_lay is None:
        o_spec = r_spec
        out_shape = jax.ShapeDtypeStruct((out_rows or m_dim, n_dim), out_dtype)
    elif o_cut == "row":
        per_o = unit_m // tm
        o_spec = pl.BlockSpec((None, None, tm, tn), lambda i, j, k: (i // per_o, o_layer, i % per_o, j))
        out_shape = jax.ShapeDtypeStruct((4, o_layers, unit_m, n_dim), out_dtype)
    else:
        per_o = unit_n // tn
        o_spec = pl.BlockSpec((None, None, tm, tn), lambda i, j, k: (j // per_o, o_layer, i, j % per_o))
        out_shape = jax.ShapeDtypeStruct((4, o_layers, m_dim, unit_n), out_dtype)
    has_res = res is not None
    has_buf = o_buf is not None

    def body(*refs):
        a_ref, b_ref = refs[:2]
        r_ref = refs[2] if has_res else None
        o_ref = refs[-1] if nk == 1 else refs[-2]

        def finish(v):
            if has_res:
                v = v + r_ref[...]
            o_ref[...] = v.astype(o_ref.dtype)

        if nk == 1:
            finish(_dot(a_ref[...], b_ref[...], dn))
            return
        acc_ref = refs[-1]
        k = pl.program_id(2)

        @pl.when(k == 0)
        def _():
            acc_ref[...] = jnp.zeros_like(acc_ref)

        acc_ref[...] += _dot(a_ref[...], b_ref[...], dn)

        @pl.when(k == nk - 1)
        def _():
            finish(acc_ref[...])

    args = [a, b] + ([res] if has_res else []) + ([o_buf] if has_buf else [])
    outs, rode = _pcall(
        body, grid=(m_dim // tm, n_dim // tn, nk),
        in_specs=[a_spec, b_spec] + ([r_spec] if has_res else []) + ([_ANY] if has_buf else []),
        out_specs=[o_spec], out_shape=[out_shape],
        scratch_shapes=[] if nk == 1 else [pltpu.VMEM((tm, tn), F32)],
        aliases={len(args) - 1: 0} if has_buf else {},
        args=args, sem=("parallel", "parallel", "arbitrary"), name=name, rider=rider)
    return (outs[0], rode) if rider is not None else outs[0]


def _rowcall(fn, rows, consts, outs, accs, *, name, tm=256):
    args = list(rows) + list(consts)
    in_specs = [pl.BlockSpec((tm, r.shape[1]), lambda i: (i, 0)) for r in rows]
    in_specs += [pl.BlockSpec(c.shape, lambda i: (0, 0)) for c in consts]
    s_dim = args[0].shape[0]
    n_in, n_out = len(args), len(outs)
    out_shape = [jax.ShapeDtypeStruct((s_dim, w), dt) for w, dt in outs] + [jax.ShapeDtypeStruct(s, F32) for s in accs]
    out_specs = [pl.BlockSpec((tm, w), lambda i: (i, 0)) for w, _ in outs] + [pl.BlockSpec(s, lambda i: (0, 0)) for s in accs]

    def body(*refs):
        vals = fn(*[r[...] for r in refs[:n_in]])
        o_refs = refs[n_in:n_in + n_out]
        a_refs = refs[n_in + n_out:]
        for o, v in zip(o_refs, vals[:n_out]):
            o[...] = v.astype(o.dtype)
        if a_refs:
            @pl.when(pl.program_id(0) == 0)
            def _():
                for a_ref in a_refs:
                    a_ref[...] = jnp.zeros_like(a_ref)

            for a_ref, v in zip(a_refs, vals[n_out:]):
                a_ref[...] += v

    return pl.pallas_call(
        body, grid=(s_dim // tm,), in_specs=in_specs, out_specs=out_specs, out_shape=out_shape,
        compiler_params=_params("arbitrary"), name=name,
    )(*args)


def _rms_fwd(h, w, name):
    return _rowcall(lambda x, w_: (_rms(x, w_, NORM_EPS),), [h], [w], [(h.shape[1], BF16)], [], name=name)[0]


def _rms_bwd(h, w, dy, dres, name):
    def fn(x, dy_, dres_, w_):
        _, vjp = jax.vjp(lambda a, b: _rms(a, b, NORM_EPS), x, w_)
        dx, dw = vjp(dy_)
        return dx + dres_, dw

    return _rowcall(fn, [h, dy, dres], [w], [(h.shape[1], F32)], [w.shape], name=name)


def _ple_fwd(h1, pp, gl, name):
    return _rowcall(lambda a, b, c: (a + b * _sigmoid(c),), [h1, pp, gl], [], [(h1.shape[1], F32)], [], name=name)[0]


def _ple_bwd(dh2, pp, gl, name):
    def fn(d, b, c):
        gate = _sigmoid(c)
        return d * gate, d * b * gate * (1.0 - gate)

    return _rowcall(fn, [dh2, pp, gl], [], [(dh2.shape[1], BF16), (dh2.shape[1], BF16)], [], name=name)


def _loss_bwd(y, target, name):
    width = y.shape[1]

    def fn(a, t):
        d = a - t
        col = jnp.sum(d * d, axis=0, keepdims=True)
        part = col[:, 0:LANES]
        for j in range(1, width // LANES):
            part = part + col[:, j * LANES:(j + 1) * LANES]
        return d * (1.0 / width), part

    return _rowcall(fn, [y, target], [], [(width, F32)], [(1, LANES)], name=name)


CONV_TC = 256


def _shift_down(x, j):
    if j == 0:
        return x
    row = lax.broadcasted_iota(jnp.int32, x.shape, 0)
    return jnp.where(row >= j, pltpu.roll(x, j, 0), 0.0)


def _shift_up(x, j):
    if j == 0:
        return x
    n = x.shape[0]
    row = lax.broadcasted_iota(jnp.int32, x.shape, 0)
    return jnp.where(row < n - j, pltpu.roll(x, n - j, 0), 0.0)


def _conv_fwd(pzx, cw, cb, name, rider=None):
    s_dim = pzx.shape[0]
    off = SSD_D_INNER // CONV_TC

    def body(x_ref, w_ref, b_ref, o_ref):
        x = x_ref[...]
        w = w_ref[...]
        y = b_ref[...] + w[3:4, :] * x
        for k in range(SSD_D_CONV - 1):
            y = y + w[k:k + 1, :] * _shift_down(x, SSD_D_CONV - 1 - k)
        o_ref[...] = y * _sigmoid(y)

    outs, rode = _pcall(
        body, grid=(SSD_CONV_DIM // CONV_TC,),
        in_specs=[pl.BlockSpec((s_dim, CONV_TC), lambda j: (0, off + j)), pl.BlockSpec((SSD_D_CONV, CONV_TC), lambda j: (0, j)),
                  pl.BlockSpec((1, CONV_TC), lambda j: (0, j))],
        out_specs=[pl.BlockSpec((s_dim, CONV_TC), lambda j: (0, j))],
        out_shape=[jax.ShapeDtypeStruct((s_dim, SSD_CONV_DIM), F32)],
        args=[pzx, cw, cb], sem=("parallel",), name=name, rider=rider)
    return (outs[0], rode) if rider is not None else outs[0]


def _conv_bwd(pzx, cw, cb, dxs, dbm, dcm, dzx, name):
    s_dim = pzx.shape[0]
    off = SSD_D_INNER // CONV_TC
    n_x, n_b = dxs.shape[1] // CONV_TC, dbm.shape[1] // CONV_TC

    def body(x_ref, w_ref, b_ref, dxs_ref, dbm_ref, dcm_ref, _, dx_ref, dw_ref, db_ref):
        j = pl.program_id(0)
        d = jnp.where(j < n_x, dxs_ref[...], jnp.where(j < n_x + n_b, dbm_ref[...], dcm_ref[...]))
        x = x_ref[...]
        w = w_ref[...]
        xs = [_shift_down(x, SSD_D_CONV - 1 - k) for k in range(SSD_D_CONV)]
        y = b_ref[...]
        for k in range(SSD_D_CONV):
            y = y + w[k:k + 1, :] * xs[k]
        sg = _sigmoid(y)
        dy = d * (sg * (1.0 + y * (1.0 - sg)))
        dx = w[3:4, :] * dy
        for k in range(SSD_D_CONV - 1):
            dx = dx + w[k:k + 1, :] * _shift_up(dy, SSD_D_CONV - 1 - k)
        dx_ref[...] = dx.astype(dx_ref.dtype)
        for k in range(SSD_D_CONV):
            dw_ref[k:k + 1, :] = jnp.sum(dy * xs[k], axis=0, keepdims=True)
        db_ref[...] = jnp.sum(dy, axis=0, keepdims=True)

    part = lambda lo, n: pl.BlockSpec((s_dim, CONV_TC), lambda j: (0, jnp.clip(j - lo, 0, n - 1)))
    return pl.pallas_call(
        body, grid=(SSD_CONV_DIM // CONV_TC,),
        in_specs=[pl.BlockSpec((s_dim, CONV_TC), lambda j: (0, off + j)), pl.BlockSpec((SSD_D_CONV, CONV_TC), lambda j: (0, j)),
                  pl.BlockSpec((1, CONV_TC), lambda j: (0, j)), part(0, n_x), part(n_x, n_b), part(n_x + n_b, n_b), _ANY],
        out_specs=[pl.BlockSpec((s_dim, CONV_TC), lambda j: (0, off + j)), pl.BlockSpec((SSD_D_CONV, CONV_TC), lambda j: (0, j)),
                   pl.BlockSpec((1, CONV_TC), lambda j: (0, j))],
        out_shape=[jax.ShapeDtypeStruct(dzx.shape, dzx.dtype), jax.ShapeDtypeStruct((SSD_D_CONV, SSD_CONV_DIM), F32),
                   jax.ShapeDtypeStruct((1, SSD_CONV_DIM), F32)],
        input_output_aliases={6: 0}, compiler_params=_params("arbitrary"), name=name,
    )(pzx, cw, cb, dxs, dbm, dcm, dzx)


def _ssd_step(xs, bm, cm, dtraw, bias, alog, dskip, st_in, z, gw, dot, cumsum):
    n = xs.shape[0]
    lane = lax.broadcasted_iota(jnp.int32, (1, LANES), 1)
    sub = lax.broadcasted_iota(jnp.int32, (LANES, 1), 0)
    left = (lane < 64).astype(F32)
    right = 1.0 - left
    top = (sub < 64).astype(F32)
    bot = 1.0 - top
    row = lax.broadcasted_iota(jnp.int32, (n, n), 0)
    colm = lax.broadcasted_iota(jnp.int32, (n, n), 1)
    causal = row >= colm

    dt = _softplus(dtraw + bias)
    adt = dt * (-jnp.exp(alog))
    acum = cumsum(adt)
    acum_t = acum.T
    last = jnp.sum(adt, axis=0, keepdims=True)
    scores = dot(cm, bm, "nt")

    def lane_of(v, h):
        return jnp.sum(v * (lane == h).astype(F32), axis=1, keepdims=True)

    ys, sts = [], []
    for pr in range(4):
        heads = (2 * pr, 2 * pr + 1)
        ac = [lane_of(acum, h) for h in heads]
        ar = [jnp.sum(acum_t * (sub == h).astype(F32), axis=0, keepdims=True) for h in heads]
        dth = [lane_of(dt, h) for h in heads]
        la = [lane_of(last, h) for h in heads]
        dk = [lane_of(dskip, h) for h in heads]
        x2 = xs[:, pr * LANES:(pr + 1) * LANES]
        xdt = x2 * (dth[0] * left + dth[1] * right)
        yd = None
        for i, side in enumerate((left, right)):
            decay = jnp.where(causal, jnp.exp(jnp.minimum(ac[i] - ar[i], 0.0)), 0.0)
            t = dot(scores * decay, xdt * side, "nn")
            yd = t if yd is None else yd + t
        st2 = st_in[pr * LANES:(pr + 1) * LANES, :]
        yo = dot(cm, st2, "nt") * (jnp.exp(ac[0]) * left + jnp.exp(ac[1]) * right)
        dte = jnp.exp(la[0] - ac[0]) * left + jnp.exp(la[1] - ac[1]) * right
        cs = dot(xdt * dte, bm, "tn")
        sts.append(st2 * (jnp.exp(la[0]) * top + jnp.exp(la[1]) * bot) + cs)
        ys.append(yd + yo + (dk[0] * left + dk[1] * right) * x2)
    y = jnp.concatenate(ys, axis=1)
    yg = y * (z * _sigmoid(z))
    yn = yg * lax.rsqrt(jnp.mean(yg * yg, axis=-1, keepdims=True) + GATED_NORM_EPS) * gw
    return yn, jnp.concatenate(sts, axis=0)


def _ssd_specs(n_chunks, rev):
    ci = (lambda c: n_chunks - 1 - c) if rev else (lambda c: c)
    n_x = SSD_D_INNER // LANES
    return dict(
        xs=pl.BlockSpec((SSD_CHUNK, SSD_GROUP_W), lambda g, c: (ci(c), g)),
        bm=pl.BlockSpec((SSD_CHUNK, LANES), lambda g, c: (ci(c), n_x + g)),
        cm=pl.BlockSpec((SSD_CHUNK, LANES), lambda g, c: (ci(c), n_x + SSD_N_GROUPS + g)),
        dt=pl.BlockSpec((None, SSD_CHUNK, LANES), lambda g, c: (g, ci(c), 0)),
        vec=pl.BlockSpec((None, 1, LANES), lambda g, c: (g, 0, 0)),
        z=pl.BlockSpec((SSD_CHUNK, SSD_GROUP_W), lambda g, c: (ci(c), g)),
        gw=pl.BlockSpec((1, SSD_GROUP_W), lambda g, c: (0, g)),
        st=pl.BlockSpec((None, None, SSD_GROUP_W, SSD_D_STATE), lambda g, c: (g, ci(c), 0, 0)),
    )


def _ssd_fwd(act, dtg, bias, alog, dskip, pzx, gw, name, rider=None):
    s_dim = act.shape[0]
    n_chunks = s_dim // SSD_CHUNK
    sp = _ssd_specs(n_chunks, False)

    def body(xs, bm, cm, dt, b_ref, a_ref, d_ref, z, gw_ref, yn_ref, st_ref, state):
        @pl.when(pl.program_id(1) == 0)
        def _():
            state[...] = jnp.zeros_like(state)

        st_in = state[...]
        st_ref[...] = st_in
        yn, st_out = _ssd_step(xs[...], bm[...], cm[...], dt[...], b_ref[...], a_ref[...], d_ref[...], st_in, z[...], gw_ref[...],
                               _dot, _cumsum_rows_raw)
        yn_ref[...] = yn.astype(yn_ref.dtype)
        state[...] = st_out

    outs, rode = _pcall(
        body, grid=(SSD_N_GROUPS, n_chunks),
        in_specs=[sp["xs"], sp["bm"], sp["cm"], sp["dt"], sp["vec"], sp["vec"], sp["vec"], sp["z"], sp["gw"]],
        out_specs=[sp["xs"], sp["st"]],
        out_shape=[jax.ShapeDtypeStruct((s_dim, SSD_D_INNER), BF16),
                   jax.ShapeDtypeStruct((SSD_N_GROUPS, n_chunks, SSD_GROUP_W, SSD_D_STATE), F32)],
        scratch_shapes=[pltpu.VMEM((SSD_GROUP_W, SSD_D_STATE), F32)],
        args=[act, act, act, dtg, bias, alog, dskip, pzx, gw], sem=("parallel", "arbitrary"), name=name, rider=rider)
    return (outs, rode) if rider is not None else outs


def _ssd_bwd(act, dtg, bias, alog, dskip, pzx, gw, states, dyn, name, rider=None):
    s_dim = act.shape[0]
    n_chunks = s_dim // SSD_CHUNK
    sp = _ssd_specs(n_chunks, True)
    rc = lambda c: n_chunks - 1 - c

    def body(xs, bm, cm, dt, b_ref, a_ref, d_ref, z, gw_ref, st_ref, dyn_ref,
             dxs_ref, dbm_ref, dcm_ref, ddt_ref, db_ref, da_ref, dd_ref, dz_ref, dgw_ref, dstate):
        first = pl.program_id(1) == 0

        @pl.when(first)
        def _():
            dstate[...] = jnp.zeros_like(dstate)
            db_ref[...] = jnp.zeros_like(db_ref)
            da_ref[...] = jnp.zeros_like(da_ref)
            dd_ref[...] = jnp.zeros_like(dd_ref)
            dgw_ref[...] = jnp.zeros_like(dgw_ref)

        fn = functools.partial(_ssd_step, dot=_gdot, cumsum=_cumsum_rows)
        _, vjp = jax.vjp(fn, xs[...], bm[...], cm[...], dt[...], b_ref[...], a_ref[...], d_ref[...], st_ref[...], z[...], gw_ref[...])
        dxs, dbm, dcm, ddt, db, da, dd, dst, dz, dgw = vjp((dyn_ref[...], dstate[...]))
        dxs_ref[...] = dxs
        dbm_ref[...] = dbm
        dcm_ref[...] = dcm
        ddt_ref[...] = ddt
        dz_ref[...] = dz.astype(dz_ref.dtype)
        db_ref[...] += db
        da_ref[...] += da
        dd_ref[...] += dd
        dgw_ref[...] += dgw
        dstate[...] = dst

    bc = pl.BlockSpec((SSD_CHUNK, LANES), lambda g, c: (rc(c), g))
    outs, rode = _pcall(
        body, grid=(SSD_N_GROUPS, n_chunks),
        in_specs=[sp["xs"], sp["bm"], sp["cm"], sp["dt"], sp["vec"], sp["vec"], sp["vec"], sp["z"], sp["gw"], sp["st"], sp["xs"]],
        out_specs=[sp["xs"], bc, bc, sp["dt"], sp["vec"], sp["vec"], sp["vec"], sp["xs"], sp["gw"]],
        out_shape=[jax.ShapeDtypeStruct((s_dim, SSD_D_INNER), F32),
                   jax.ShapeDtypeStruct((s_dim, SSD_N_GROUPS * SSD_D_STATE), F32),
                   jax.ShapeDtypeStruct((s_dim, SSD_N_GROUPS * SSD_D_STATE), F32),
                   jax.ShapeDtypeStruct((SSD_N_GROUPS, s_dim, LANES), F32),
                   jax.ShapeDtypeStruct((SSD_N_GROUPS, 1, LANES), F32),
                   jax.ShapeDtypeStruct((SSD_N_GROUPS, 1, LANES), F32),
                   jax.ShapeDtypeStruct((SSD_N_GROUPS, 1, LANES), F32),
                   jax.ShapeDtypeStruct((s_dim, SSD_ZX), BF16),
                   jax.ShapeDtypeStruct((1, SSD_D_INNER), F32)],
        scratch_shapes=[pltpu.VMEM((SSD_GROUP_W, SSD_D_STATE), F32)],
        args=[act, act, act, dtg, bias, alog, dskip, pzx, gw, states, dyn], sem=("arbitrary", "arbitrary"), name=name, rider=rider)
    return (outs, rode) if rider is not None else outs


SB_T = 128
SB_GROUP = 8
SB_WIDE = SB_GROUP * SB_T
SB_HB = 4
SB_SCALE = 1.0 / math.sqrt(SB_HEAD_DIM)


def _qknorm_fwd(proj, qw, kw, name, tm=512):
    s_dim = proj.shape[0]

    def body(q_ref, k_ref, v_ref, qw_ref, kw_ref, qo, ko, vo):
        qo[...] = _rms(q_ref[...], qw_ref[...], NORM_EPS).astype(BF16)
        ko[...] = _rms(k_ref[...], kw_ref[...], NORM_EPS).astype(BF16)
        vo[...] = v_ref[...].astype(BF16)

    blk = lambda o: pl.BlockSpec((tm, SB_HEAD_DIM), lambda i, h: (i, o + h))
    vec = pl.BlockSpec((1, SB_HEAD_DIM), lambda i, h: (0, 0))
    return pl.pallas_call(
        body, grid=(s_dim // tm, SB_N_HEADS),
        in_specs=[blk(0), blk(SB_N_HEADS), blk(2 * SB_N_HEADS), vec, vec],
        out_specs=[blk(0)] * 3,
        out_shape=[jax.ShapeDtypeStruct((s_dim, SB_WIDTH), BF16)] * 3,
        compiler_params=_params("parallel", "parallel"), name=name,
    )(proj, proj, proj, qw, kw)


def _qknorm_bwd(proj, qw, kw, dqn, dkn, name, tm=512):
    s_dim = proj.shape[0]

    def body(q_ref, k_ref, dq_ref, dk_ref, qw_ref, kw_ref, dqo, dko, dqw, dkw):
        @pl.when((pl.program_id(0) == 0) & (pl.program_id(1) == 0))
        def _():
            dqw[...] = jnp.zeros_like(dqw)
            dkw[...] = jnp.zeros_like(dkw)

        fn = lambda a, b: _rms(a, b, NORM_EPS)
        _, vq = jax.vjp(fn, q_ref[...], qw_ref[...])
        dq, dw = vq(dq_ref[...])
        dqo[...] = dq.astype(BF16)
        dqw[...] += dw
        _, vk = jax.vjp(fn, k_ref[...], kw_ref[...])
        dk, dw = vk(dk_ref[...])
        dko[...] = dk.astype(BF16)
        dkw[...] += dw

    blk = lambda o: pl.BlockSpec((tm, SB_HEAD_DIM), lambda i, h: (i, o + h))
    vec = pl.BlockSpec((1, SB_HEAD_DIM), lambda i, h: (0, 0))
    return pl.pallas_call(
        body, grid=(s_dim // tm, SB_N_HEADS),
        in_specs=[blk(0), blk(SB_N_HEADS), blk(0), blk(0), vec, vec],
        out_specs=[blk(0), blk(0), vec, vec],
        out_shape=[jax.ShapeDtypeStruct((s_dim, SB_WIDTH), BF16)] * 2 + [jax.ShapeDtypeStruct((1, SB_HEAD_DIM), F32)] * 2,
        compiler_params=_params("arbitrary", "arbitrary"), name=name,
    )(proj, proj, dqn, dkn, qw, kw)


def _sb_logits(q, k, strict):
    z = _dot(q, k, "nt") * SB_SCALE
    lb = jnp.minimum(z, 0.0) - jnp.log(1.0 + jnp.exp(-jnp.abs(z)))
    lm = lb - z
    if strict is not None:
        lm = jnp.where(strict, lm, 0.0)
    return lb, lm


def _sb_strict(qi, grp):
    r = lax.broadcasted_iota(jnp.int32, (SB_T, SB_WIDE), 0) + qi * SB_T
    c = lax.broadcasted_iota(jnp.int32, (SB_T, SB_WIDE), 1) + grp * SB_WIDE
    return c < r


def _head_lanes(hh):
    return slice(hh * SB_HEAD_DIM, (hh + 1) * SB_HEAD_DIM)


def _sb_fwd(qn, kn, vb, proj, name, rider=None):
    s_dim = qn.shape[0]
    nq = s_dim // SB_T
    assert nq % SB_GROUP == 0

    def body(q_ref, k_ref, v_ref, g_ref, og_ref, o_ref, t_ref):
        qi = pl.program_id(1)
        top = qi // SB_GROUP
        after = _tri(SB_T, True, strict=True)
        qs = [q_ref[:, _head_lanes(hh)] for hh in range(SB_HB)]

        def step(grp, masked, carries):
            start = pl.multiple_of(grp * SB_WIDE, SB_WIDE)
            strict = _sb_strict(qi, grp) if masked else None
            out = []
            for hh in range(SB_HB):
                o_acc, cr = carries[hh]
                k = k_ref[pl.ds(start, SB_WIDE), _head_lanes(hh)]
                v = v_ref[pl.ds(start, SB_WIDE), _head_lanes(hh)]
                lb, lm = _sb_logits(qs[hh], k, strict)
                rest = [None] * SB_GROUP
                for t in reversed(range(SB_GROUP)):
                    lm_t = lm[:, t * SB_T:(t + 1) * SB_T]
                    rest[t] = cr + _split_dot(lm_t, after, 2, True)
                    cr = cr + jnp.sum(lm_t, axis=1, keepdims=True)
                a = jnp.exp(lb + jnp.concatenate(rest, axis=1))
                if masked:
                    a = jnp.where(strict, a, 0.0)
                out.append((o_acc + _dot(a, v), cr))
            return tuple(out)

        init = tuple((jnp.zeros((SB_T, SB_HEAD_DIM), F32), jnp.zeros((SB_T, 1), F32)) for _ in range(SB_HB))
        carries = step(top, True, init)
        carries = lax.fori_loop(0, top, lambda i, c: step(top - 1 - i, False, c), carries)
        for hh in range(SB_HB):
            o, tot = carries[hh]
            g = g_ref[:, _head_lanes(hh)]
            o_ref[:, _head_lanes(hh)] = o
            og_ref[:, _head_lanes(hh)] = (o * (g * _sigmoid(g))).astype(og_ref.dtype)
            t_ref[hh] = jnp.broadcast_to(tot, (SB_T, LANES))

    wide = SB_HB * SB_HEAD_DIM
    qb = pl.BlockSpec((SB_T, wide), lambda h, i: (i, h))
    kv = pl.BlockSpec((s_dim, wide), lambda h, i: (0, h))
    outs, rode = _pcall(
        body, grid=(SB_N_HEADS // SB_HB, nq),
        in_specs=[qb, kv, kv, pl.BlockSpec((SB_T, wide), lambda h, i: (i, 3 * SB_N_HEADS // SB_HB + h))],
        out_specs=[qb, qb, pl.BlockSpec((SB_HB, SB_T, LANES), lambda h, i: (h, i, 0))],
        out_shape=[jax.ShapeDtypeStruct((s_dim, SB_WIDTH), BF16), jax.ShapeDtypeStruct((s_dim, SB_WIDTH), F32),
                   jax.ShapeDtypeStruct((SB_N_HEADS, s_dim, LANES), F32)],
        args=[qn, kn, vb, proj], sem=("parallel", "arbitrary"), name=name, rider=rider)
    return (outs, rode) if rider is not None else outs


def _sb_bwd(qn, kn, vb, proj, o, tot, dog, name, rider=None):
    s_dim = qn.shape[0]
    nq = s_dim // SB_T
    assert nq % SB_GROUP == 0

    def body(q_ref, k_ref, v_ref, g_ref, o_ref, t_ref, dog_ref, dq_ref, dk_ref, dv_ref, dvb_ref, dg_ref):
        qi = pl.program_id(1)
        top = qi // SB_GROUP

        @pl.when(qi == 0)
        def _():
            dk_ref[...] = jnp.zeros_like(dk_ref)
            dv_ref[...] = jnp.zeros_like(dv_ref)

        after = _tri(SB_T, True, strict=True)
        before = _tri(SB_T, False, strict=True)
        qs, dos, totals = [], [], []
        for hh in range(SB_HB):
            g = g_ref[:, _head_lanes(hh)]
            sg = _sigmoid(g)
            dog_v = dog_ref[:, _head_lanes(hh)]
            dg_ref[:, _head_lanes(hh)] = (dog_v * o_ref[:, _head_lanes(hh)] * (sg * (1.0 + g * (1.0 - sg)))).astype(dg_ref.dtype)
            dos.append((dog_v * (g * sg)).astype(BF16))
            qs.append(q_ref[:, _head_lanes(hh)])
            totals.append(t_ref[hh][:, 0:1])

        def step(grp, masked, carries):
            start = pl.multiple_of(grp * SB_WIDE, SB_WIDE)
            strict = _sb_strict(qi, grp) if masked else None
            out = []
            for hh in range(SB_HB):
                dq_acc, cp, ce = carries[hh]
                q, do = qs[hh], dos[hh]
                k = k_ref[pl.ds(start, SB_WIDE), _head_lanes(hh)]
                v = v_ref[pl.ds(start, SB_WIDE), _head_lanes(hh)]
                lb, lm = _sb_logits(q, k, strict)
                rest = []
                for t in range(SB_GROUP):
                    lm_t = lm[:, t * SB_T:(t + 1) * SB_T]
                    cp = cp + jnp.sum(lm_t, axis=1, keepdims=True)
                    rest.append((totals[hh] - cp) + _split_dot(lm_t, after, 2, True))
                a = jnp.exp(lb + jnp.concatenate(rest, axis=1))
                if masked:
                    a = jnp.where(strict, a, 0.0)
                e = a * _dot(do, v, "nt")
                excl = []
                for t in range(SB_GROUP):
                    e_t = e[:, t * SB_T:(t + 1) * SB_T]
                    excl.append(ce + _split_dot(e_t, before, 2, True))
                    ce = ce + jnp.sum(e_t, axis=1, keepdims=True)
                eex = jnp.concatenate(excl, axis=1)
                if masked:
                    eex = jnp.where(strict, eex, 0.0)
                sig = jnp.exp(lb)
                dz = (e * (1.0 - sig) - eex * sig) * SB_SCALE
                dv_ref[pl.ds(start, SB_WIDE), _head_lanes(hh)] += _dot(a, do, "tn")
                dk_ref[pl.ds(start, SB_WIDE), _head_lanes(hh)] += _dot(dz, q, "tn")
                out.append((dq_acc + _dot(dz, k), cp, ce))
            return tuple(out)

        zero = jnp.zeros((SB_T, 1), F32)
        init = tuple((jnp.zeros((SB_T, SB_HEAD_DIM), F32), zero, zero) for _ in range(SB_HB))
        carries = lax.fori_loop(0, top, lambda i, c: step(i, False, c), init)
        carries = step(top, True, carries)
        for hh in range(SB_HB):
            dq_ref[:, _head_lanes(hh)] = carries[hh][0]

        @pl.when(qi == nq - 1)
        def _():
            dvb_ref[...] = dv_ref[...].astype(BF16)

    wide = SB_HB * SB_HEAD_DIM
    qb = pl.BlockSpec((SB_T, wide), lambda h, i: (i, h))
    kv = pl.BlockSpec((s_dim, wide), lambda h, i: (0, h))
    outs, rode = _pcall(
        body, grid=(SB_N_HEADS // SB_HB, nq),
        in_specs=[qb, kv, kv, pl.BlockSpec((SB_T, wide), lambda h, i: (i, 3 * SB_N_HEADS // SB_HB + h)), qb,
                  pl.BlockSpec((SB_HB, SB_T, LANES), lambda h, i: (h, i, 0)), qb],
        out_specs=[qb, kv, kv, kv, qb],
        out_shape=[jax.ShapeDtypeStruct((s_dim, SB_WIDTH), F32), jax.ShapeDtypeStruct((s_dim, SB_WIDTH), F32),
                   jax.ShapeDtypeStruct((s_dim, SB_WIDTH), F32), jax.ShapeDtypeStruct((s_dim, SB_WIDTH), BF16),
                   jax.ShapeDtypeStruct((s_dim, SB_WIDTH), BF16)],
        args=[qn, kn, vb, proj, o, tot, dog], sem=("parallel", "arbitrary"), name=name, rider=rider)
    return (outs, rode) if rider is not None else outs


def _adamw_math(w, g, m, v):
    m = ADAM_B1 * m + (1.0 - ADAM_B1) * g
    v = ADAM_B2 * v + (1.0 - ADAM_B2) * (g * g)
    m_hat = m / (1.0 - ADAM_B1 ** ADAM_STEP)
    v_hat = v / (1.0 - ADAM_B2 ** ADAM_STEP)
    delta = -ADAM_LR * (m_hat / (jnp.sqrt(v_hat) + ADAM_EPS) + ADAM_WD * w)
    return delta, m, v


def _row_block(rows, cols, itemsize=4, limit=1 << 20):
    tr = rows
    while tr * cols * itemsize > limit and tr % (2 * BF16_ROWS) == 0:
        tr //= 2
    return tr


def _divisor_block(rows, cols, itemsize=4, limit=2 << 20):
    best = BF16_ROWS
    for t in range(BF16_ROWS, rows + 1, BF16_ROWS):
        if rows % t == 0 and t * cols * itemsize <= limit:
            best = t
    return best


def _adamw(w, g, m, v, name, rider=None):
    n, rows, cols = w.shape
    tr = rows if rows * cols * 4 <= (1 << 20) else _divisor_block(rows, cols, limit=1 << 20)

    def body(w_ref, g_ref, m_ref, v_ref, d_out, m_out, v_out):
        d, m_new, v_new = _adamw_math(w_ref[...], g_ref[...], m_ref[...], v_ref[...])
        d_out[...] = d
        m_out[...] = m_new
        v_out[...] = v_new

    blk = pl.BlockSpec((None, tr, cols), lambda i, j: (i, j, 0))
    outs, rode = _pcall(
        body, grid=(n, rows // tr), in_specs=[blk] * 4, out_specs=[blk] * 3,
        out_shape=[jax.ShapeDtypeStruct(w.shape, F32)] * 3,
        args=[w, g, m, v], sem=("parallel", "parallel"), name=name, rider=rider)
    return (outs, rode) if rider is not None else outs


_FLIPS = ((1, 0), (0, 1), (1, 1))


def _place():
    return lax.axis_index("x"), lax.axis_index("y"), lax.axis_index("c")


def _flip(v, f):
    return 1 - v if f else v


def _half_rows(ref, lead, hc, hr):
    return ref.at[(*lead, pl.ds(pl.multiple_of(hc * hr, BF16_ROWS), hr), slice(None))]


def _half_cols(ref, lead, hc, hw):
    return ref.at[(*lead, pl.ds(pl.multiple_of(hc * hw, LANES), hw))]


def _rows_of_chip(chip, r):
    return pl.ds(pl.multiple_of(chip * r, BF16_ROWS), r)


def _slot_half(gathered, shard_shape, chip, l, hc):
    r, c = shard_shape[1:]
    if len(gathered.shape) == 3:
        return _half_cols(gathered, (l, _rows_of_chip(chip, r)), hc, c // 2)
    return _half_rows(gathered, (chip, l), hc, r // 2)


def _shard_half(shard, stacked, l, hc):
    r, c = shard.shape[1:]
    return _half_cols(shard, (l, slice(None)), hc, c // 2) if stacked else _half_rows(shard, (l,), hc, r // 2)


def _remote(src, dst, send, recv, k, to):
    return pltpu.make_async_remote_copy(src_ref=src, dst_ref=dst, send_sem=send.at[k], recv_sem=recv.at[k], device_id=to,
                                        device_id_type=MESH)


def _comm_call(reads, writes, n_sems, phases, name):
    passed = [k for k, w in enumerate(writes) if not isinstance(w, jax.ShapeDtypeStruct)]
    n_rd = len(reads)

    def body(*refs):
        rd = refs[:n_rd]
        wr = refs[n_rd + len(passed):n_rd + len(passed) + len(writes)]
        send, recv = refs[-2:]
        for phase in phases:
            sends, arrivals = phase(rd, wr, send, recv)
            for cp in sends:
                cp.start()
            for cp in arrivals:
                cp.wait_recv()
            for cp in sends:
                cp.wait_send()

    return pl.pallas_call(
        body, in_specs=[_ANY] * (n_rd + len(passed)), out_specs=[_ANY] * len(writes),
        out_shape=[jax.ShapeDtypeStruct(w.shape, w.dtype) for w in writes],
        input_output_aliases={n_rd + pos: k for pos, k in enumerate(passed)},
        scratch_shapes=[pltpu.SemaphoreType.DMA((n_sems,)), pltpu.SemaphoreType.DMA((n_sems,))], name=name,
    )(*reads, *[writes[k] for k in passed])


def _ag_ici(pieces, names, base=0):
    def phase(shards, gathered, send, recv):
        x, y, c = _place()
        me = 2 * x + y
        sends, arrivals = [], []
        for k, (n, l) in enumerate(pieces):
            a = names.index(n)
            shape = shards[a].shape
            src = _shard_half(shards[a], len(gathered[a].shape) == 3, l, c)
            for j, (fx, fy) in enumerate(_FLIPS):
                tx, ty = _flip(x, fx), _flip(y, fy)
                sends.append(_remote(src, _slot_half(gathered[a], shape, me, l, c), send, recv, base + 3 * k + j, (tx, ty, c)))
                arrivals.append(_remote(src, _slot_half(gathered[a], shape, 2 * tx + ty, l, c), send, recv, base + 3 * k + j, (tx, ty, c)))
        return sends, arrivals

    return phase


def _ag_pass_on(pieces, names, shapes, base=0):
    def phase(_, gathered, send, recv):
        x, y, c = _place()
        sibling = (x, y, 1 - c)
        sends, arrivals = [], []
        for k, (n, l) in enumerate(pieces):
            a = names.index(n)
            for j, (fx, fy) in enumerate(_FLIPS):
                chip = 2 * _flip(x, fx) + _flip(y, fy)
                landed = _slot_half(gathered[a], shapes[a], chip, l, c)
                sends.append(_remote(landed, landed, send, recv, base + 3 * k + j, sibling))
                arrivals.append(_remote(landed, _slot_half(gathered[a], shapes[a], chip, l, 1 - c), send, recv, base + 3 * k + j, sibling))
        return sends, arrivals

    return phase


def _other_half(ref, hc):
    if len(ref.shape) == 3:
        return _half_cols(ref, (slice(None), slice(None)), hc, ref.shape[2] // 2)
    return _half_rows(ref, (slice(None), slice(None)), hc, ref.shape[2] // 2)


def _half_shape(shape):
    return shape[:2] + (shape[2] // 2,) if len(shape) == 3 else shape[:2] + (shape[2] // 2, shape[3])


def _exchange_phase(n_arr):
    def phase(ins, outs, send, recv):
        x, y, c = _place()
        cps = [_remote(_other_half(ins[a], 1 - c), outs[a], send, recv, a, (x, y, 1 - c)) for a in range(n_arr)]
        return cps, cps

    return phase


def _exchange_outs(grads):
    return [jax.ShapeDtypeStruct(_half_shape(g.shape), g.dtype) for g in grads]


def _pair_exchange(grads, name):
    return _comm_call(grads, _exchange_outs(grads), len(grads), [_exchange_phase(len(grads))], name)


def _exchange_rider(grads):
    return _Rider(grads, _exchange_outs(grads), len(grads), _exchange_phase(len(grads)))


def _pair_sum_stacked(g, got, place, name):
    _, rows, hw = got.shape
    tr = _divisor_block(rows, hw)

    def body(place_ref, g_ref, r_ref, o_ref):
        o_ref[...] = (g_ref[...].astype(F32) + r_ref[...].astype(F32)).astype(o_ref.dtype)

    blk = pl.BlockSpec((None, tr, hw), lambda i, pr: (0, i, 0))
    return pl.pallas_call(
        body,
        grid_spec=pltpu.PrefetchScalarGridSpec(
            num_scalar_prefetch=1, grid=(rows // tr,),
            in_specs=[pl.BlockSpec((None, tr, hw), lambda i, pr: (0, i, pr[1])), blk], out_specs=blk),
        out_shape=jax.ShapeDtypeStruct(got.shape, BF16),
        compiler_params=_params("parallel"), name=name,
    )(place, g, got)


def _pair_sum(g, got, place, name):
    if len(g.shape) == 3:
        return _pair_sum_stacked(g, got, place, name)
    _, layers, hr, cols = got.shape
    tr = _row_block(hr, cols)
    per = hr // tr

    def body(place_ref, g_ref, r_ref, o_ref):
        o_ref[...] = (g_ref[...].astype(F32) + r_ref[...].astype(F32)).astype(o_ref.dtype)

    blk = pl.BlockSpec((None, None, tr, cols), lambda k, l, i, pr: (k, l, i, 0))
    return pl.pallas_call(
        body,
        grid_spec=pltpu.PrefetchScalarGridSpec(
            num_scalar_prefetch=1, grid=(4, layers, per),
            in_specs=[pl.BlockSpec((None, None, tr, cols), lambda k, l, i, pr: (k, l, pr[1] * per + i, 0)), blk],
            out_specs=blk),
        out_shape=jax.ShapeDtypeStruct(got.shape, BF16),
        compiler_params=_params("parallel", "parallel", "parallel"), name=name,
    )(place, g, got)


def _scatter_phase(n_arr):
    def phase(ins, outs, send, recv):
        x, y, c = _place()
        cps = []
        for a in range(n_arr):
            for j, (fx, fy) in enumerate(_FLIPS):
                tx, ty = _flip(x, fx), _flip(y, fy)
                if len(ins[a].shape) == 3:
                    src = ins[a].at[:, _rows_of_chip(2 * tx + ty, ins[a].shape[1] // 4), :]
                else:
                    src = ins[a].at[2 * tx + ty]
                cps.append(_remote(src, outs[a].at[j], send, recv, 3 * a + j, (tx, ty, c)))
        return cps, cps

    return phase


def _scatter_outs(pairs):
    return [jax.ShapeDtypeStruct((3, 1, p.shape[1] // 4, p.shape[2]) if len(p.shape) == 3 else (3,) + p.shape[1:], p.dtype) for p in pairs]


def _chip_scatter(pairs, name):
    return _comm_call(pairs, _scatter_outs(pairs), 3 * len(pairs), [_scatter_phase(len(pairs))], name)


def _scatter_rider(pairs):
    return _Rider(pairs, _scatter_outs(pairs), 3 * len(pairs), _scatter_phase(len(pairs)))


def _chip_sum_stacked(p, got, place, layer, layers, o_buf, name):
    _, r, hw = got.shape[1:]
    tr = _divisor_block(r, hw)
    per = r // tr

    def body(place_ref, p_ref, r_ref, *rest):
        o_ref = rest[-1]
        acc = p_ref[...].astype(F32)
        for j in range(3):
            acc = acc + r_ref[j].astype(F32)
        o_ref[...] = acc

    has_buf = o_buf is not None
    return pl.pallas_call(
        body,
        grid_spec=pltpu.PrefetchScalarGridSpec(
            num_scalar_prefetch=1, grid=(per,),
            in_specs=[pl.BlockSpec((None, tr, hw), lambda i, pr: (0, pr[0] * per + i, 0)),
                      pl.BlockSpec((3, None, tr, hw), lambda i, pr: (0, 0, i, 0))] + ([_ANY] if has_buf else []),
            out_specs=pl.BlockSpec((None, tr, hw), lambda i, pr: (layer, i, pr[1]))),
        out_shape=jax.ShapeDtypeStruct((layers, r, 2 * hw), F32),
        input_output_aliases={3: 0} if has_buf else {},
        compiler_params=_params("parallel"), name=name,
    )(*((place, p, got) + ((o_buf,) if has_buf else ())))


def _chip_sum(p, got, place, layer, layers, o_buf, name):
    if len(p.shape) == 3:
        return _chip_sum_stacked(p, got, place, layer, layers, o_buf, name)
    _, _, hr, cols = p.shape
    tr = _row_block(hr, cols)
    per = hr // tr

    def body(place_ref, p_ref, r_ref, *rest):
        o_ref = rest[-1]
        acc = p_ref[...].astype(F32)
        for j in range(3):
            acc = acc + r_ref[j].astype(F32)
        o_ref[...] = acc

    has_buf = o_buf is not None
    return pl.pallas_call(
        body,
        grid_spec=pltpu.PrefetchScalarGridSpec(
            num_scalar_prefetch=1, grid=(per,),
            in_specs=[pl.BlockSpec((None, None, tr, cols), lambda i, pr: (pr[0], 0, i, 0)),
                      pl.BlockSpec((3, None, tr, cols), lambda i, pr: (0, 0, i, 0))] + ([_ANY] if has_buf else []),
            out_specs=pl.BlockSpec((None, tr, cols), lambda i, pr: (layer, pr[1] * per + i, 0))),
        out_shape=jax.ShapeDtypeStruct((layers, 2 * hr, cols), F32),
        input_output_aliases={3: 0} if has_buf else {},
        compiler_params=_params("parallel"), name=name,
    )(*((place, p, got) + ((o_buf,) if has_buf else ())))


def _pair_gather(halves, by_cols, name):
    def phase(_, bufs, send, recv):
        x, y, c = _place()
        sends, arrivals = [], []
        for a, h in enumerate(halves):
            cut = (lambda hc, a=a, h=h: _half_cols(bufs[a], (slice(None), slice(None)), hc, h.shape[2] // 2)) if by_cols[a] else (
                lambda hc, a=a, h=h: _half_rows(bufs[a], (slice(None),), hc, h.shape[1] // 2))
            sends.append(_remote(cut(c), cut(c), send, recv, a, (x, y, 1 - c)))
            arrivals.append(_remote(cut(c), cut(1 - c), send, recv, a, (x, y, 1 - c)))
        return sends, arrivals

    return _comm_call([], halves, len(halves), [phase], name)


def _allreduce_small(v, name):
    rows, cols = v.shape

    def body(v_ref, o_ref, buf, send_sems, recv_sems):
        x, y, c = _place()
        me = 4 * x + 2 * y + c
        buf[0] = v_ref[...]
        cps = []
        for k in range(1, 8):
            kx, ky, kc = (k >> 2) & 1, (k >> 1) & 1, k & 1
            cp = pltpu.make_async_remote_copy(src_ref=v_ref, dst_ref=buf.at[k], send_sem=send_sems.at[k - 1], recv_sem=recv_sems.at[k - 1],
                                              device_id=(_flip(x, kx), _flip(y, ky), _flip(c, kc)), device_id_type=MESH)
            cp.start()
            cps.append(cp)
        for cp in cps:
            cp.wait()
        acc = buf[me]
        for d in range(1, 8):
            acc = acc + buf[jnp.bitwise_xor(d, me)]
        o_ref[...] = acc

    vm = pl.BlockSpec(memory_space=pltpu.VMEM)
    return pl.pallas_call(
        body, in_specs=[vm], out_specs=vm, out_shape=jax.ShapeDtypeStruct((rows, cols), F32),
        scratch_shapes=[pltpu.VMEM((8, rows, cols), F32), pltpu.SemaphoreType.DMA((7,)), pltpu.SemaphoreType.DMA((7,))],
        name=name,
    )(v)


def _pad_lanes(a):
    return jnp.pad(a, ((0, 0), (0, LANES - a.shape[1])))


def _group_lanes(v):
    return jnp.pad(v.reshape(SSD_N_GROUPS, 1, 8), ((0, 0), (0, 0), (0, LANES - 8)))


def kernel(x, p, norm_w, ssd_in_w, ssd_conv_w, ssd_conv_b, ssd_dt_bias, ssd_a_log, ssd_d, ssd_gnorm_w, ssd_out_w, sb_in_w, sb_qn_w, sb_kn_w, sb_out_w, ple_norm_w, ple_gate_w, ple_proj_w, loss_target, m_norm_w, m_ssd_in_w, m_ssd_conv_w, m_ssd_conv_b, m_ssd_dt_bias, m_ssd_a_log, m_ssd_d, m_ssd_gnorm_w, m_ssd_out_w, m_sb_in_w, m_sb_qn_w, m_sb_kn_w, m_sb_out_w, m_ple_norm_w, m_ple_gate_w, m_ple_proj_w, v_norm_w, v_ssd_in_w, v_ssd_conv_w, v_ssd_conv_b, v_ssd_dt_bias, v_ssd_a_log, v_ssd_d, v_ssd_gnorm_w, v_ssd_out_w, v_sb_in_w, v_sb_qn_w, v_sb_kn_w, v_sb_out_w, v_ple_norm_w, v_ple_gate_w, v_ple_proj_w):
    w_in = dict(norm_w=norm_w, ssd_in_w=ssd_in_w, ssd_conv_w=ssd_conv_w, ssd_conv_b=ssd_conv_b, ssd_dt_bias=ssd_dt_bias,
                ssd_a_log=ssd_a_log, ssd_d=ssd_d, ssd_gnorm_w=ssd_gnorm_w, ssd_out_w=ssd_out_w, sb_in_w=sb_in_w, sb_qn_w=sb_qn_w,
                sb_kn_w=sb_kn_w, sb_out_w=sb_out_w, ple_norm_w=ple_norm_w, ple_gate_w=ple_gate_w, ple_proj_w=ple_proj_w)
    m_in = dict(norm_w=m_norm_w, ssd_in_w=m_ssd_in_w, ssd_conv_w=m_ssd_conv_w, ssd_conv_b=m_ssd_conv_b, ssd_dt_bias=m_ssd_dt_bias,
                ssd_a_log=m_ssd_a_log, ssd_d=m_ssd_d, ssd_gnorm_w=m_ssd_gnorm_w, ssd_out_w=m_ssd_out_w, sb_in_w=m_sb_in_w,
                sb_qn_w=m_sb_qn_w, sb_kn_w=m_sb_kn_w, sb_out_w=m_sb_out_w, ple_norm_w=m_ple_norm_w, ple_gate_w=m_ple_gate_w,
                ple_proj_w=m_ple_proj_w)
    v_in = dict(norm_w=v_norm_w, ssd_in_w=v_ssd_in_w, ssd_conv_w=v_ssd_conv_w, ssd_conv_b=v_ssd_conv_b, ssd_dt_bias=v_ssd_dt_bias,
                ssd_a_log=v_ssd_a_log, ssd_d=v_ssd_d, ssd_gnorm_w=v_ssd_gnorm_w, ssd_out_w=v_ssd_out_w, sb_in_w=v_sb_in_w,
                sb_qn_w=v_sb_qn_w, sb_kn_w=v_sb_kn_w, sb_out_w=v_sb_out_w, ple_norm_w=v_ple_norm_w, ple_gate_w=v_ple_gate_w,
                ple_proj_w=v_ple_proj_w)
    ix, iy, ic = lax.axis_index("x"), lax.axis_index("y"), lax.axis_index("c")
    chip = (2 * ix + iy).astype(jnp.int32)
    place = jnp.stack([chip, ic.astype(jnp.int32)])
    zero = jnp.zeros((), jnp.int32)
    big_names = [n for n, _, _ in _BIG]
    layers_of = {n: s[0] for n, s, _ in _BIG}
    cut_of = {n: cut for n, _, cut in _BIG}

    def layer_pieces(i):
        mixer = ("ssd_in_w", "ssd_out_w") if i % 2 == 0 else ("sb_in_w", "sb_out_w")
        return [(mixer[0], i // 2), (mixer[1], i // 2), ("ple_gate_w", i), ("ple_proj_w", i)]

    def names_of(pieces):
        return [n for n in big_names if any(n == q for q, _ in pieces)]

    held = lambda n, a: a.transpose(0, 2, 1) if cut_of[n] == "stack" else a
    mine = {n: held(n, w_in[n]).astype(BF16) for n in big_names}
    shard_shapes = [mine[n].shape for n in big_names]
    room = [jax.ShapeDtypeStruct((s[0], 4 * s[1], s[2]) if cut_of[n] == "stack" else (4,) + s, BF16) for n, s in zip(big_names, shard_shapes)]
    first = layer_pieces(0)[:1]
    gathered = _comm_call([mine[n] for n in big_names], room, 6 * len(first),
                          [_ag_ici(first, big_names), _ag_pass_on(first, big_names, shard_shapes, base=3 * len(first))], "allgather_layer0")
    gw = {}
    for n, g in zip(big_names, gathered):
        if cut_of[n] == "stack":
            layers, r, c = mine[n].shape
            gw[n] = lax.dynamic_update_slice(g.reshape(layers, 4, r, c), mine[n][:, None], (zero, chip, zero, zero)).reshape(g.shape)
        else:
            gw[n] = lax.dynamic_update_slice(g, mine[n][None], (chip, zero, zero, zero))

    lp = [layer_pieces(i) for i in range(DEPTH)]
    carries = {
        "ssd_in_0": (lp[0][1:2], []), "conv_0": (lp[0][2:], lp[0][1:2]), "ssd_0": (lp[1][:1], lp[0][2:]),
        "ssd_out_0": (lp[1][1:2], lp[1][:1]), "sb_in_1": (lp[1][2:], lp[1][1:2]), "sb_1": (lp[2][:2], lp[1][2:]),
        "sb_out_1": (lp[2][2:], lp[2][:2]), "ssd_in_2": (lp[3][1:], lp[2][2:]), "ssd_2": (lp[3][:1], lp[3][1:]),
        "ssd_out_2": ([], lp[3][:1]),
    }

    def gather_rider(call):
        if call not in carries:
            return None, lambda outs: outs
        ici, passing = carries[call]
        names = names_of(ici + passing)
        phases = ([_ag_ici(ici, names)] if ici else []) + (
            [_ag_pass_on(passing, names, [mine[n].shape for n in names], base=3 * len(ici))] if passing else [])

        def issue(rd, wr, send, recv):
            both = [ph(rd, wr, send, recv) for ph in phases]
            return sum((b[0] for b in both), []), sum((b[1] for b in both), [])

        def land(outs):
            outs, bufs = outs
            for n, g in zip(names, bufs):
                gw[n] = g
            return outs

        return _Rider([mine[n] for n in names], [gw[n] for n in names], 3 * (len(ici) + len(passing)), issue), land

    onehot = (jnp.arange(4) == chip).astype(F32) * (ic == 0).astype(F32)
    cw_mine = onehot[:, None, None, None] * ssd_conv_w[None]
    cw_full = _allreduce_small(cw_mine.transpose(1, 2, 0, 3).reshape(-1, LANES), "gather_conv_w").reshape(2, SSD_D_CONV, SSD_CONV_DIM)

    def wmm(a, name, layer, *, dn="nn", res=None, call, rider=None):
        return _matmul(a, gw[name], dn=dn, res=res, b_lay=(cut_of[name], layer), name=call, rider=rider)

    h = x[0]
    target = loss_target[0]
    saved = []
    for i in range(DEPTH):
        j = i // 2
        nw = norm_w[i:i + 1]
        pw = ple_norm_w[i:i + 1]
        s = dict(h=h)
        u = _rms_fwd(h, nw, f"rms_{i}")
        s["u"] = u
        if i % 2 == 0:
            w_dt = jnp.pad(gw["ssd_in_w"][j, SSD_ZX:], ((0, LANES - SSD_N_HEADS), (0, 0)))
            rider, land = gather_rider(f"ssd_in_{i}")
            pzx = land(_matmul(u, gw["ssd_in_w"], dn="nt", b_lay=("stack", j, SSD_ZX), name=f"ssd_in_{i}", rider=rider))
            pdt = _matmul(u, w_dt, dn="nt", name=f"ssd_indt_{i}")
            rider, land = gather_rider(f"conv_{i}")
            act = land(_conv_fwd(pzx, cw_full[j], ssd_conv_b[j:j + 1], f"conv_{i}", rider=rider))
            dtg = jnp.pad(pdt[:, :SSD_N_HEADS].reshape(-1, SSD_N_GROUPS, 8).transpose(1, 0, 2), ((0, 0), (0, 0), (0, LANES - 8)))
            vecs = (_group_lanes(ssd_dt_bias[j]), _group_lanes(ssd_a_log[j]), _group_lanes(ssd_d[j]))
            rider, land = gather_rider(f"ssd_{i}")
            yn, states = land(_ssd_fwd(act, dtg, *vecs, pzx, ssd_gnorm_w[j:j + 1], f"ssd_{i}", rider=rider))
            s.update(w_dt=w_dt, pzx=pzx, act=act, dtg=dtg, vecs=vecs, yn=yn, states=states)
            rider, land = gather_rider(f"ssd_out_{i}")
            h1 = land(wmm(yn, "ssd_out_w", j, res=h, call=f"ssd_out_{i}", rider=rider))
        else:
            rider, land = gather_rider(f"sb_in_{i}")
            proj = land(wmm(u, "sb_in_w", j, call=f"sb_in_{i}", rider=rider))
            qn, kn, vb = _qknorm_fwd(proj, sb_qn_w[j:j + 1], sb_kn_w[j:j + 1], f"qknorm_{i}")
            rider, land = gather_rider(f"sb_{i}")
            og, o, tot = land(_sb_fwd(qn, kn, vb, proj, f"sb_{i}", rider=rider))
            s.update(proj=proj, qn=qn, kn=kn, vb=vb, og=og, o=o, tot=tot)
            rider, land = gather_rider(f"sb_out_{i}")
            h1 = land(wmm(og, "sb_out_w", j, res=h, call=f"sb_out_{i}", rider=rider))
        n2 = _rms_fwd(h1, pw, f"ple_rms_{i}")
        gl = wmm(n2, "ple_gate_w", i, call=f"ple_gate_{i}")
        pp = wmm(p[i, 0], "ple_proj_w", i, call=f"ple_proj_{i}")
        h = _ple_fwd(h1, pp, gl, f"ple_{i}")
        s.update(h1=h1, n2=n2, gl=gl, pp=pp)
        saved.append(s)

    dh, loss_lanes = _loss_bwd(h, target, "loss")

    wg = {}
    gsmall = {n: [None] * s[0] for n, s in _SMALL}
    g_conv_w = [None, None]
    scat = {}
    pending = late = None

    def wgrad(a, b, name, layer, call, rider=None):
        out = _matmul(a, b, dn="tn", out_dtype=BF16, o_lay=(cut_of[name], 0, 1), name=call, rider=rider)
        wg[(name, layer)], rode = out if rider is not None else (out, None)
        return rode

    def pair_sums(pieces, got, tag):
        return pieces, [_pair_sum(wg[q], r, place, f"rs_pair_sum_{tag}_{k}") for k, (q, r) in enumerate(zip(pieces, got))]

    def sibling_rider(pieces):
        return _exchange_rider([wg[q] for q in pieces])

    def riding_with(own):
        return (pending[0] + own[0], pending[1] + own[1]) if pending else own

    def arrived(sent, got):
        for q, pair, g in zip(sent[0], sent[1], got):
            scat[q] = (pair, g)

    for i in reversed(range(DEPTH)):
        j = i // 2
        s = saved[i]
        nw = norm_w[i:i + 1]
        pw = ple_norm_w[i:i + 1]
        dpp, dgl = _ple_bwd(dh, s["pp"], s["gl"], f"ple_bwd_{i}")
        wgrad(p[i, 0], dpp, "ple_proj_w", i, f"d_ple_proj_{i}")
        if late is None:
            wgrad(s["n2"], dgl, "ple_gate_w", i, f"d_ple_gate_{i}")
        else:
            pending = pair_sums(late, wgrad(s["n2"], dgl, "ple_gate_w", i, f"d_ple_gate_{i}", rider=sibling_rider(late)), f"{i + 1}_in")
        dn2 = wmm(dgl, "ple_gate_w", i, dn="nt", call=f"ple_gate_bwd_{i}")
        dh1, dpw = _rms_bwd(s["h1"], pw, dn2, dh, f"ple_rms_bwd_{i}")
        gsmall["ple_norm_w"][i] = dpw
        if i % 2 == 0:
            wgrad(s["yn"], dh1, "ssd_out_w", j, f"d_ssd_out_{i}")
            early = layer_pieces(i)[1:]
            dyn, got = wmm(dh1, "ssd_out_w", j, dn="nt", call=f"ssd_out_bwd_{i}", rider=sibling_rider(early))
            riding = riding_with(pair_sums(early, got, f"{i}_out"))
            outs, got = _ssd_bwd(s["act"], s["dtg"], *s["vecs"], s["pzx"], ssd_gnorm_w[j:j + 1], s["states"], dyn, f"ssd_bwd_{i}",
                                 rider=_scatter_rider(riding[1]))
            arrived(riding, got)
            dxs, dbm, dcm, ddtg, dbias, dalog, ddsk, dz, dgw = outs
            dzx, dcw, dcb = _conv_bwd(s["pzx"], cw_full[j], ssd_conv_b[j:j + 1], dxs, dbm, dcm, dz, f"conv_bwd_{i}")
            ddt = _pad_lanes(ddtg[:, :, :8].transpose(1, 0, 2).reshape(-1, SSD_N_HEADS)).astype(BF16)
            du = _matmul(dzx, gw["ssd_in_w"], b_lay=("stack", j, SSD_ZX), name=f"ssd_in_bwd_{i}")
            du = _matmul(ddt, s["w_dt"], res=du, name=f"ssd_indt_bwd_{i}")
            dwt = _matmul(dzx, s["u"], dn="tn", out_dtype=BF16, out_rows=SSD_IN_DIM, name=f"d_ssd_in_{i}")
            dwt_dt = _matmul(ddt, s["u"], dn="tn", out_dtype=BF16, name=f"d_ssd_indt_{i}")
            wg[("ssd_in_w", j)] = lax.dynamic_update_slice(dwt, dwt_dt[:SSD_N_HEADS], (SSD_ZX, 0))[None]
            g_conv_w[j] = dcw
            gsmall["ssd_conv_b"][j] = dcb
            gsmall["ssd_dt_bias"][j] = dbias[:, 0, :8].reshape(1, SSD_N_HEADS)
            gsmall["ssd_a_log"][j] = dalog[:, 0, :8].reshape(1, SSD_N_HEADS)
            gsmall["ssd_d"][j] = ddsk[:, 0, :8].reshape(1, SSD_N_HEADS)
            gsmall["ssd_gnorm_w"][j] = dgw
        else:
            wgrad(s["og"], dh1, "sb_out_w", j, f"d_sb_out_{i}")
            early = layer_pieces(i)[1:]
            dog, got = wmm(dh1, "sb_out_w", j, dn="nt", call=f"sb_out_bwd_{i}", rider=sibling_rider(early))
            riding = riding_with(pair_sums(early, got, f"{i}_out"))
            outs, got = _sb_bwd(s["qn"], s["kn"], s["vb"], s["proj"], s["o"], s["tot"], dog, f"sb_bwd_{i}", rider=_scatter_rider(riding[1]))
            arrived(riding, got)
            dqn, dkn, _, dvb, dg = outs
            dq, dk, dqw, dkw = _qknorm_bwd(s["proj"], sb_qn_w[j:j + 1], sb_kn_w[j:j + 1], dqn, dkn, f"qknorm_bwd_{i}")
            dproj = jnp.concatenate([dq, dk, dvb, dg], axis=1)
            du = wmm(dproj, "sb_in_w", j, dn="nt", call=f"sb_in_bwd_{i}")
            wgrad(s["u"], dproj, "sb_in_w", j, f"d_sb_in_{i}")
            gsmall["sb_qn_w"][j] = dqw
            gsmall["sb_kn_w"][j] = dkw
        dh, dnw = _rms_bwd(s["h"], nw, du, dh1, f"rms_bwd_{i}")
        gsmall["norm_w"][i] = dnw
        late = layer_pieces(i)[:1]
    grad_x = dh[None]

    def reduced(names, call):
        halves = []
        for n in names:
            buf = None
            for l in range(layers_of[n]):
                buf = _chip_sum(*scat[(n, l)], place, l, layers_of[n], buf, f"rs_chip_sum_{n}_{l}")
            halves.append(buf)
        return dict(zip(names, _pair_gather(halves, [cut_of[n] == "stack" for n in names], call)))

    def updated(n, rider=None):
        return _adamw(held(n, w_in[n]), g_big[n], held(n, m_in[n]), held(n, v_in[n]), f"adamw_{n}", rider=rider)

    done_early = ["sb_in_w", "sb_out_w"]
    g_big = reduced(done_early, "rs_pair_gather_sb")
    pending = pair_sums(late, _pair_exchange([wg[q] for q in late], "rs_pair_exchange_last"), "0_in")
    step = {}
    step["sb_in_w"], got = updated("sb_in_w", rider=_scatter_rider(pending[1]))
    arrived(pending, got)
    g_big.update(reduced([n for n in big_names if n not in done_early], "rs_pair_gather"))

    small_parts = [jnp.concatenate(gsmall[n], axis=0).reshape(-1) for n, _ in _SMALL]
    small_parts.append(jnp.stack(g_conv_w).reshape(-1))
    small_parts.append(loss_lanes.reshape(-1))
    small_sum = _allreduce_small(jnp.concatenate(small_parts).reshape(-1, LANES), "allreduce_small").reshape(-1)
    g_small, off = {}, 0
    for n, shape in _SMALL:
        size = math.prod(shape)
        g_small[n] = small_sum[off:off + size].reshape(shape)
        off += size
    cw_size = 2 * SSD_D_CONV * SSD_CONV_DIM
    g_cw_full = small_sum[off:off + cw_size].reshape(2, SSD_D_CONV, 4, SSD_CONV_DIM // 4)
    g_small["ssd_conv_w"] = jnp.sum(g_cw_full * (jnp.arange(4) == chip).astype(F32)[None, None, :, None], axis=2)
    loss = 0.5 * jnp.sum(small_sum[off + cw_size:]) / D_MODEL

    grads, delta, new_m, new_v = {}, {}, {}, {}
    for n in big_names:
        grads[n], delta[n], new_m[n], new_v[n] = (held(n, a) for a in (g_big[n], *(step[n] if n in step else updated(n))))
    small_names = [n for n, _ in _SMALL] + ["ssd_conv_w"]
    pack = lambda d: jnp.concatenate([d[n].reshape(-1) for n in small_names]).reshape(1, -1, LANES)
    ds, ms, vs = _adamw(pack(w_in), pack(g_small), pack(m_in), pack(v_in), "adamw_small")
    off = 0
    for n in small_names:
        shape = w_in[n].shape
        size = math.prod(shape)
        grads[n] = g_small[n]
        delta[n] = ds.reshape(-1)[off:off + size].reshape(shape)
        new_m[n] = ms.reshape(-1)[off:off + size].reshape(shape)
        new_v[n] = vs.reshape(-1)[off:off + size].reshape(shape)
        off += size

    order = ["norm_w", "ssd_in_w", "ssd_conv_w", "ssd_conv_b", "ssd_dt_bias", "ssd_a_log", "ssd_d", "ssd_gnorm_w", "ssd_out_w",
             "sb_in_w", "sb_qn_w", "sb_kn_w", "sb_out_w", "ple_norm_w", "ple_gate_w", "ple_proj_w"]
    return (loss, grad_x, *[grads[n] for n in order], *[delta[n] for n in order], *[new_m[n] for n in order],
            *[new_v[n] for n in order])
```

```python
import functools
import math

import jax
import jax.numpy as jnp
from jax import lax
from jax.experimental import pallas as pl
from jax.experimental.pallas import tpu as pltpu

F32 = jnp.float32
BF16 = jnp.bfloat16
MESH = pl.DeviceIdType.MESH

D_MODEL = 2048
DEPTH = 4
SSD_D_INNER = 4096
SSD_N_GROUPS = 8
SSD_GROUP_W = SSD_D_INNER // SSD_N_GROUPS
SSD_D_STATE = 128
SSD_CHUNK = 128
SSD_CONV_DIM = 6144
SSD_D_CONV = 4
SSD_N_HEADS = 64
SB_HEAD_DIM = 128
SB_N_HEADS = 16
SB_WIDTH = 2048
NORM_EPS = 1e-6
GATED_NORM_EPS = 1e-5
ADAM_LR = 0.001
ADAM_B1 = 0.9
ADAM_B2 = 0.999
ADAM_EPS = 1e-08
ADAM_WD = 0.01
ADAM_STEP = 10

SSD_ZX = SSD_D_INNER + SSD_CONV_DIM
SSD_IN_DIM = SSD_ZX + SSD_N_HEADS
LAST_SPLIT = 1104
LANES = 128
BF16_ROWS = 16

_BIG = (
    ("ssd_in_w", (2, 2576, 2048), "stack"),
    ("ssd_out_w", (2, 1024, 2048), "row"),
    ("sb_in_w", (2, 2048, 2048), "col"),
    ("sb_out_w", (2, 512, 2048), "row"),
    ("ple_gate_w", (4, 512, 2048), "row"),
    ("ple_proj_w", (4, 256, 512), "col"),
)
_SMALL = (
    ("norm_w", (4, 2048)),
    ("ssd_conv_b", (2, 6144)),
    ("ssd_dt_bias", (2, 64)),
    ("ssd_a_log", (2, 64)),
    ("ssd_d", (2, 64)),
    ("ssd_gnorm_w", (2, 4096)),
    ("sb_qn_w", (2, 128)),
    ("sb_kn_w", (2, 128)),
    ("ple_norm_w", (4, 2048)),
)

_DN = {
    "nn": (((1,), (0,)), ((), ())),
    "nt": (((1,), (1,)), ((), ())),
    "tn": (((0,), (0,)), ((), ())),
}


def _dot(a, b, dn="nn"):
    return lax.dot_general(a.astype(BF16), b.astype(BF16), _DN[dn], preferred_element_type=F32)


@functools.partial(jax.custom_vjp, nondiff_argnums=(2,))
def _gdot(a, b, dn):
    return _dot(a, b, dn)


def _gdot_fwd(a, b, dn):
    return _dot(a, b, dn), (a, b)


def _gdot_bwd(dn, res, g):
    a, b = res
    if dn == "nn":
        return _dot(g, b, "nt"), _dot(a, g, "tn")
    if dn == "nt":
        return _dot(g, b, "nn"), _dot(g, a, "tn")
    return _dot(b, g, "nt"), _dot(a, g, "nn")


_gdot.defvjp(_gdot_fwd, _gdot_bwd)


def _split_dot(x, t, parts, x_left):
    acc = None
    r = x
    for i in range(parts):
        p = r.astype(BF16)
        d = lax.dot_general(p, t, _DN["nn"], preferred_element_type=F32) if x_left else lax.dot_general(
            t, p, _DN["nn"], preferred_element_type=F32)
        acc = d if acc is None else acc + d
        if i + 1 < parts:
            r = r - p.astype(F32)
    return acc


def _tri(n, lower, strict=False):
    r = lax.broadcasted_iota(jnp.int32, (n, n), 0)
    c = lax.broadcasted_iota(jnp.int32, (n, n), 1)
    keep = (r > c if strict else r >= c) if lower else (r < c if strict else r <= c)
    return jnp.where(keep, 1.0, 0.0).astype(BF16)


def _cumsum_rows_raw(x):
    return _split_dot(x, _tri(x.shape[0], True), 3, False)


@jax.custom_vjp
def _cumsum_rows(x):
    return _cumsum_rows_raw(x)


def _cumsum_rows_fwd(x):
    return _cumsum_rows_raw(x), None


def _cumsum_rows_bwd(_, g):
    return (_split_dot(g, _tri(g.shape[0], False), 3, False),)


_cumsum_rows.defvjp(_cumsum_rows_fwd, _cumsum_rows_bwd)


def _sigmoid(x):
    return 1.0 / (1.0 + jnp.exp(-x))


def _softplus(x):
    return jnp.maximum(x, 0.0) + jnp.log(1.0 + jnp.exp(-jnp.abs(x)))


def _rms(x, w, eps):
    return x * lax.rsqrt(jnp.mean(x * x, axis=-1, keepdims=True) + eps) * w


_ANY = pl.BlockSpec(memory_space=pl.ANY)


def _params(*sem):
    return pltpu.CompilerParams(dimension_semantics=sem)


class _Rider:
    def __init__(self, reads, writes, n_sems, issue):
        self.reads, self.writes, self.n_sems, self.issue = list(reads), list(writes), n_sems, issue


def _pcall(body, *, grid, in_specs, out_specs, out_shape, args, sem, name, scratch_shapes=(), aliases=None, rider=None):
    aliases = dict(aliases or {})
    if rider is None:
        outs = pl.pallas_call(body, grid=grid, in_specs=in_specs, out_specs=out_specs, out_shape=out_shape,
                              scratch_shapes=list(scratch_shapes), input_output_aliases=aliases,
                              compiler_params=_params(*sem), name=name)(*args)
        return list(outs), []
    n_in, n_out, n_scr, n_rd, n_wr = len(args), len(out_shape), len(scratch_shapes), len(rider.reads), len(rider.writes)
    passed = [k for k, w in enumerate(rider.writes) if not isinstance(w, jax.ShapeDtypeStruct)]
    for pos, k in enumerate(passed):
        aliases[n_in + n_rd + pos] = n_out + k

    def wrapped(*refs):
        ins = refs[:n_in]
        reads = refs[n_in:n_in + n_rd]
        base = n_in + n_rd + len(passed)
        outs = refs[base:base + n_out]
        writes = refs[base + n_out:base + n_out + n_wr]
        scr = refs[base + n_out + n_wr:base + n_out + n_wr + n_scr]
        send, recv = refs[-2:]
        first = last = None
        for d, n in enumerate(grid):
            i = pl.program_id(d)
            first = (i == 0) if first is None else first & (i == 0)
            last = (i == n - 1) if last is None else last & (i == n - 1)

        @pl.when(first)
        def _():
            for cp in rider.issue(reads, writes, send, recv)[0]:
                cp.start()

        body(*ins, *outs, *scr)

        @pl.when(last)
        def _():
            sends, arrivals = rider.issue(reads, writes, send, recv)
            for cp in arrivals:
                cp.wait_recv()
            for cp in sends:
                cp.wait_send()

    outs = pl.pallas_call(
        wrapped, grid=grid,
        in_specs=list(in_specs) + [_ANY] * (n_rd + len(passed)),
        out_specs=list(out_specs) + [_ANY] * n_wr,
        out_shape=list(out_shape) + [jax.ShapeDtypeStruct(w.shape, w.dtype) for w in rider.writes],
        scratch_shapes=list(scratch_shapes) + [pltpu.SemaphoreType.DMA((rider.n_sems,)), pltpu.SemaphoreType.DMA((rider.n_sems,))],
        input_output_aliases=aliases, compiler_params=_params(*(["arbitrary"] * len(grid))), name=name,
    )(*args, *rider.reads, *[rider.writes[k] for k in passed])
    return list(outs[:n_out]), list(outs[n_out:])


MM_TK = 2048


def _pick(dim, pref, unit=None):
    t = pref
    while t >= LANES:
        if dim % t == 0 and (unit is None or unit % t == 0):
            return t
        t //= 2
    return dim


def _matmul(a, b, *, dn="nn", res=None, out_dtype=F32, name, b_lay=None, o_lay=None, o_buf=None, out_rows=None, rider=None):
    if dn == "tn":
        k_dim, m_dim = a.shape
    else:
        m_dim, k_dim = a.shape
    unit_m = unit_n = unit_k = None
    if b_lay is None:
        n_dim = b.shape[0] if dn == "nt" else b.shape[1]
    elif b_lay[0] == "stack":
        cut, layer, rows = b_lay
        cols = b.shape[2]
        n_dim = cols if dn == "nn" else rows
        assert k_dim == (rows if dn == "nn" else cols) and dn != "tn"
    else:
        cut, layer = b_lay
        r, c = b.shape[2:]
        rows, cols = (4 * r, c) if cut == "row" else (r, 4 * c)
        n_dim = cols if dn == "nn" else rows
        assert k_dim == (rows if dn == "nn" else cols) and dn != "tn"
        if (cut == "row") == (dn == "nn"):
            unit_k = r if cut == "row" else c
        else:
            unit_n = r if cut == "row" else c
    if o_lay is not None:
        o_cut, o_layer, o_layers = o_lay
        if o_cut == "row":
            unit_m = m_dim // 4
        else:
            unit_n = n_dim // 4
    tm, tn, tk = _pick(m_dim, 1024, unit_m), _pick(n_dim, 1024, unit_n), _pick(k_dim, MM_TK, unit_k)
    nk = k_dim // tk
    a_spec = pl.BlockSpec((tk, tm), lambda i, j, k: (k, i)) if dn == "tn" else pl.BlockSpec((tm, tk), lambda i, j, k: (i, k))
    if b_lay is None:
        b_spec = pl.BlockSpec((tn, tk), lambda i, j, k: (j, k)) if dn == "nt" else pl.BlockSpec((tk, tn), lambda i, j, k: (k, j))
    elif cut == "stack":
        b_spec = (pl.BlockSpec((None, tk, tn), lambda i, j, k: (layer, k, j)) if dn == "nn" else
                  pl.BlockSpec((None, tn, tk), lambda i, j, k: (layer, j, k)))
    elif dn == "nn" and cut == "row":
        per = r // tk
        b_spec = pl.BlockSpec((None, None, tk, tn), lambda i, j, k: (k // per, layer, k % per, j))
    elif dn == "nn":
        per = c // tn
        b_spec = pl.BlockSpec((None, None, tk, tn), lambda i, j, k: (j // per, layer, k, j % per))
    elif cut == "row":
        per = r // tn
        b_spec = pl.BlockSpec((None, None, tn, tk), lambda i, j, k: (j // per, layer, j % per, k))
    else:
        per = c // tk
        b_spec = pl.BlockSpec((None, None, tn, tk), lambda i, j, k: (k // per, layer, j, k % per))
    r_spec = pl.BlockSpec((tm, tn), lambda i, j, k: (i, j))
    if o_lay is None:
        o_spec = r_spec
        out_shape = jax.ShapeDtypeStruct((out_rows or m_dim, n_dim), out_dtype)
    elif o_cut == "row":
        per_o = unit_m // tm
        o_spec = pl.BlockSpec((None, None, tm, tn), lambda i, j, k: (i // per_o, o_layer, i % per_o, j))
        out_shape = jax.ShapeDtypeStruct((4, o_layers, unit_m, n_dim), out_dtype)
    else:
        per_o = unit_n // tn
        o_spec = pl.BlockSpec((None, None, tm, tn), lambda i, j, k: (j // per_o, o_layer, i, j % per_o))
        out_shape = jax.ShapeDtypeStruct((4, o_layers, m_dim, unit_n), out_dtype)
    has_res = res is not None
    has_buf = o_buf is not None

    def body(*refs):
        a_ref, b_ref = refs[:2]
        r_ref = refs[2] if has_res else None
        o_ref = refs[-1] if nk == 1 else refs[-2]

        def finish(v):
            if has_res:
                v = v + r_ref[...]
            o_ref[...] = v.astype(o_ref.dtype)

        if nk == 1:
            finish(_dot(a_ref[...], b_ref[...], dn))
            return
        acc_ref = refs[-1]
        k = pl.program_id(2)

        @pl.when(k == 0)
        def _():
            acc_ref[...] = jnp.zeros_like(acc_ref)

        acc_ref[...] += _dot(a_ref[...], b_ref[...], dn)

        @pl.when(k == nk - 1)
        def _():
            finish(acc_ref[...])

    args = [a, b] + ([res] if has_res else []) + ([o_buf] if has_buf else [])
    outs, rode = _pcall(
        body, grid=(m_dim // tm, n_dim // tn, nk),
        in_specs=[a_spec, b_spec] + ([r_spec] if has_res else []) + ([_ANY] if has_buf else []),
        out_specs=[o_spec], out_shape=[out_shape],
        scratch_shapes=[] if nk == 1 else [pltpu.VMEM((tm, tn), F32)],
        aliases={len(args) - 1: 0} if has_buf else {},
        args=args, sem=("parallel", "parallel", "arbitrary"), name=name, rider=rider)
    return (outs[0], rode) if rider is not None else outs[0]


def _rowcall(fn, rows, consts, outs, accs, *, name, tm=256):
    args = list(rows) + list(consts)
    in_specs = [pl.BlockSpec((tm, r.shape[1]), lambda i: (i, 0)) for r in rows]
    in_specs += [pl.BlockSpec(c.shape, lambda i: (0, 0)) for c in consts]
    s_dim = args[0].shape[0]
    n_in, n_out = len(args), len(outs)
    out_shape = [jax.ShapeDtypeStruct((s_dim, w), dt) for w, dt in outs] + [jax.ShapeDtypeStruct(s, F32) for s in accs]
    out_specs = [pl.BlockSpec((tm, w), lambda i: (i, 0)) for w, _ in outs] + [pl.BlockSpec(s, lambda i: (0, 0)) for s in accs]

    def body(*refs):
        vals = fn(*[r[...] for r in refs[:n_in]])
        o_refs = refs[n_in:n_in + n_out]
        a_refs = refs[n_in + n_out:]
        for o, v in zip(o_refs, vals[:n_out]):
            o[...] = v.astype(o.dtype)
        if a_refs:
            @pl.when(pl.program_id(0) == 0)
            def _():
                for a_ref in a_refs:
                    a_ref[...] = jnp.zeros_like(a_ref)

            for a_ref, v in zip(a_refs, vals[n_out:]):
                a_ref[...] += v

    return pl.pallas_call(
        body, grid=(s_dim // tm,), in_specs=in_specs, out_specs=out_specs, out_shape=out_shape,
        compiler_params=_params("arbitrary"), name=name,
    )(*args)


def _rms_fwd(h, w, name):
    return _rowcall(lambda x, w_: (_rms(x, w_, NORM_EPS),), [h], [w], [(h.shape[1], BF16)], [], name=name)[0]


def _rms_bwd(h, w, dy, dres, name):
    def fn(x, dy_, dres_, w_):
        _, vjp = jax.vjp(lambda a, b: _rms(a, b, NORM_EPS), x, w_)
        dx, dw = vjp(dy_)
        return dx + dres_, dw

    return _rowcall(fn, [h, dy, dres], [w], [(h.shape[1], F32)], [w.shape], name=name)


def _ple_fwd(h1, pp, gl, name):
    return _rowcall(lambda a, b, c: (a + b * _sigmoid(c),), [h1, pp, gl], [], [(h1.shape[1], F32)], [], name=name)[0]


def _ple_bwd(dh2, pp, gl, name):
    def fn(d, b, c):
        gate = _sigmoid(c)
        return d * gate, d * b * gate * (1.0 - gate)

    return _rowcall(fn, [dh2, pp, gl], [], [(dh2.shape[1], BF16), (dh2.shape[1], BF16)], [], name=name)


def _loss_bwd(y, target, name):
    width = y.shape[1]

    def fn(a, t):
        d = a - t
        col = jnp.sum(d * d, axis=0, keepdims=True)
        part = col[:, 0:LANES]
        for j in range(1, width // LANES):
            part = part + col[:, j * LANES:(j + 1) * LANES]
        return d * (1.0 / width), part

    return _rowcall(fn, [y, target], [], [(width, F32)], [(1, LANES)], name=name)


CONV_TC = 256


def _shift_down(x, j):
    if j == 0:
        return x
    row = lax.broadcasted_iota(jnp.int32, x.shape, 0)
    return jnp.where(row >= j, pltpu.roll(x, j, 0), 0.0)


def _shift_up(x, j):
    if j == 0:
        return x
    n = x.shape[0]
    row = lax.broadcasted_iota(jnp.int32, x.shape, 0)
    return jnp.where(row < n - j, pltpu.roll(x, n - j, 0), 0.0)


def _conv_fwd(pzx, cw, cb, name, rider=None):
    s_dim = pzx.shape[0]
    off = SSD_D_INNER // CONV_TC

    def body(x_ref, w_ref, b_ref, o_ref):
        x = x_ref[...]
        w = w_ref[...]
        y = b_ref[...] + w[3:4, :] * x
        for k in range(SSD_D_CONV - 1):
            y = y + w[k:k + 1, :] * _shift_down(x, SSD_D_CONV - 1 - k)
        o_ref[...] = y * _sigmoid(y)

    outs, rode = _pcall(
        body, grid=(SSD_CONV_DIM // CONV_TC,),
        in_specs=[pl.BlockSpec((s_dim, CONV_TC), lambda j: (0, off + j)), pl.BlockSpec((SSD_D_CONV, CONV_TC), lambda j: (0, j)),
                  pl.BlockSpec((1, CONV_TC), lambda j: (0, j))],
        out_specs=[pl.BlockSpec((s_dim, CONV_TC), lambda j: (0, j))],
        out_shape=[jax.ShapeDtypeStruct((s_dim, SSD_CONV_DIM), F32)],
        args=[pzx, cw, cb], sem=("parallel",), name=name, rider=rider)
    return (outs[0], rode) if rider is not None else outs[0]


def _conv_bwd(pzx, cw, cb, dxs, dbm, dcm, dzx, name):
    s_dim = pzx.shape[0]
    off = SSD_D_INNER // CONV_TC
    n_x, n_b = dxs.shape[1] // CONV_TC, dbm.shape[1] // CONV_TC

    def body(x_ref, w_ref, b_ref, dxs_ref, dbm_ref, dcm_ref, _, dx_ref, dw_ref, db_ref):
        j = pl.program_id(0)
        d = jnp.where(j < n_x, dxs_ref[...], jnp.where(j < n_x + n_b, dbm_ref[...], dcm_ref[...]))
        x = x_ref[...]
        w = w_ref[...]
        xs = [_shift_down(x, SSD_D_CONV - 1 - k) for k in range(SSD_D_CONV)]
        y = b_ref[...]
        for k in range(SSD_D_CONV):
            y = y + w[k:k + 1, :] * xs[k]
        sg = _sigmoid(y)
        dy = d * (sg * (1.0 + y * (1.0 - sg)))
        dx = w[3:4, :] * dy
        for k in range(SSD_D_CONV - 1):
            dx = dx + w[k:k + 1, :] * _shift_up(dy, SSD_D_CONV - 1 - k)
        dx_ref[...] = dx.astype(dx_ref.dtype)
        for k in range(SSD_D_CONV):
            dw_ref[k:k + 1, :] = jnp.sum(dy * xs[k], axis=0, keepdims=True)
        db_ref[...] = jnp.sum(dy, axis=0, keepdims=True)

    part = lambda lo, n: pl.BlockSpec((s_dim, CONV_TC), lambda j: (0, jnp.clip(j - lo, 0, n - 1)))
    return pl.pallas_call(
        body, grid=(SSD_CONV_DIM // CONV_TC,),
        in_specs=[pl.BlockSpec((s_dim, CONV_TC), lambda j: (0, off + j)), pl.BlockSpec((SSD_D_CONV, CONV_TC), lambda j: (0, j)),
                  pl.BlockSpec((1, CONV_TC), lambda j: (0, j)), part(0, n_x), part(n_x, n_b), part(n_x + n_b, n_b), _ANY],
        out_specs=[pl.BlockSpec((s_dim, CONV_TC), lambda j: (0, off + j)), pl.BlockSpec((SSD_D_CONV, CONV_TC), lambda j: (0, j)),
                   pl.BlockSpec((1, CONV_TC), lambda j: (0, j))],
        out_shape=[jax.ShapeDtypeStruct(dzx.shape, dzx.dtype), jax.ShapeDtypeStruct((SSD_D_CONV, SSD_CONV_DIM), F32),
                   jax.ShapeDtypeStruct((1, SSD_CONV_DIM), F32)],
        input_output_aliases={6: 0}, compiler_params=_params("arbitrary"), name=name,
    )(pzx, cw, cb, dxs, dbm, dcm, dzx)


def _ssd_step(xs, bm, cm, dtraw, bias, alog, dskip, st_in, z, gw, dot, cumsum):
    n = xs.shape[0]
    lane = lax.broadcasted_iota(jnp.int32, (1, LANES), 1)
    sub = lax.broadcasted_iota(jnp.int32, (LANES, 1), 0)
    left = (lane < 64).astype(F32)
    right = 1.0 - left
    top = (sub < 64).astype(F32)
    bot = 1.0 - top
    row = lax.broadcasted_iota(jnp.int32, (n, n), 0)
    colm = lax.broadcasted_iota(jnp.int32, (n, n), 1)
    causal = row >= colm

    dt = _softplus(dtraw + bias)
    adt = dt * (-jnp.exp(alog))
    acum = cumsum(adt)
    acum_t = acum.T
    last = jnp.sum(adt, axis=0, keepdims=True)
    scores = dot(cm, bm, "nt")

    def lane_of(v, h):
        return jnp.sum(v * (lane == h).astype(F32), axis=1, keepdims=True)

    ys, sts = [], []
    for pr in range(4):
        heads = (2 * pr, 2 * pr + 1)
        ac = [lane_of(acum, h) for h in heads]
        ar = [jnp.sum(acum_t * (sub == h).astype(F32), axis=0, keepdims=True) for h in heads]
        dth = [lane_of(dt, h) for h in heads]
        la = [lane_of(last, h) for h in heads]
        dk = [lane_of(dskip, h) for h in heads]
        x2 = xs[:, pr * LANES:(pr + 1) * LANES]
        xdt = x2 * (dth[0] * left + dth[1] * right)
        yd = None
        for i, side in enumerate((left, right)):
            decay = jnp.where(causal, jnp.exp(jnp.minimum(ac[i] - ar[i], 0.0)), 0.0)
            t = dot(scores * decay, xdt * side, "nn")
            yd = t if yd is None else yd + t
        st2 = st_in[pr * LANES:(pr + 1) * LANES, :]
        yo = dot(cm, st2, "nt") * (jnp.exp(ac[0]) * left + jnp.exp(ac[1]) * right)
        dte = jnp.exp(la[0] - ac[0]) * left + jnp.exp(la[1] - ac[1]) * right
        cs = dot(xdt * dte, bm, "tn")
        sts.append(st2 * (jnp.exp(la[0]) * top + jnp.exp(la[1]) * bot) + cs)
        ys.append(yd + yo + (dk[0] * left + dk[1] * right) * x2)
    y = jnp.concatenate(ys, axis=1)
    yg = y * (z * _sigmoid(z))
    yn = yg * lax.rsqrt(jnp.mean(yg * yg, axis=-1, keepdims=True) + GATED_NORM_EPS) * gw
    return yn, jnp.concatenate(sts, axis=0)


def _ssd_specs(n_chunks, rev):
    ci = (lambda c: n_chunks - 1 - c) if rev else (lambda c: c)
    n_x = SSD_D_INNER // LANES
    return dict(
        xs=pl.BlockSpec((SSD_CHUNK, SSD_GROUP_W), lambda g, c: (ci(c), g)),
        bm=pl.BlockSpec((SSD_CHUNK, LANES), lambda g, c: (ci(c), n_x + g)),
        cm=pl.BlockSpec((SSD_CHUNK, LANES), lambda g, c: (ci(c), n_x + SSD_N_GROUPS + g)),
        dt=pl.BlockSpec((None, SSD_CHUNK, LANES), lambda g, c: (g, ci(c), 0)),
        vec=pl.BlockSpec((None, 1, LANES), lambda g, c: (g, 0, 0)),
        z=pl.BlockSpec((SSD_CHUNK, SSD_GROUP_W), lambda g, c: (ci(c), g)),
        gw=pl.BlockSpec((1, SSD_GROUP_W), lambda g, c: (0, g)),
        st=pl.BlockSpec((None, None, SSD_GROUP_W, SSD_D_STATE), lambda g, c: (g, ci(c), 0, 0)),
    )


def _ssd_fwd(act, dtg, bias, alog, dskip, pzx, gw, name, rider=None):
    s_dim = act.shape[0]
    n_chunks = s_dim // SSD_CHUNK
    sp = _ssd_specs(n_chunks, False)

    def body(xs, bm, cm, dt, b_ref, a_ref, d_ref, z, gw_ref, yn_ref, st_ref, state):
        @pl.when(pl.program_id(1) == 0)
        def _():
            state[...] = jnp.zeros_like(state)

        st_in = state[...]
        st_ref[...] = st_in
        yn, st_out = _ssd_step(xs[...], bm[...], cm[...], dt[...], b_ref[...], a_ref[...], d_ref[...], st_in, z[...], gw_ref[...],
                               _dot, _cumsum_rows_raw)
        yn_ref[...] = yn.astype(yn_ref.dtype)
        state[...] = st_out

    outs, rode = _pcall(
        body, grid=(SSD_N_GROUPS, n_chunks),
        in_specs=[sp["xs"], sp["bm"], sp["cm"], sp["dt"], sp["vec"], sp["vec"], sp["vec"], sp["z"], sp["gw"]],
        out_specs=[sp["xs"], sp["st"]],
        out_shape=[jax.ShapeDtypeStruct((s_dim, SSD_D_INNER), BF16),
                   jax.ShapeDtypeStruct((SSD_N_GROUPS, n_chunks, SSD_GROUP_W, SSD_D_STATE), F32)],
        scratch_shapes=[pltpu.VMEM((SSD_GROUP_W, SSD_D_STATE), F32)],
        args=[act, act, act, dtg, bias, alog, dskip, pzx, gw], sem=("parallel", "arbitrary"), name=name, rider=rider)
    return (outs, rode) if rider is not None else outs


def _ssd_bwd(act, dtg, bias, alog, dskip, pzx, gw, states, dyn, name, rider=None):
    s_dim = act.shape[0]
    n_chunks = s_dim // SSD_CHUNK
    sp = _ssd_specs(n_chunks, True)
    rc = lambda c: n_chunks - 1 - c

    def body(xs, bm, cm, dt, b_ref, a_ref, d_ref, z, gw_ref, st_ref, dyn_ref,
             dxs_ref, dbm_ref, dcm_ref, ddt_ref, db_ref, da_ref, dd_ref, dz_ref, dgw_ref, dstate):
        first = pl.program_id(1) == 0

        @pl.when(first)
        def _():
            dstate[...] = jnp.zeros_like(dstate)
            db_ref[...] = jnp.zeros_like(db_ref)
            da_ref[...] = jnp.zeros_like(da_ref)
            dd_ref[...] = jnp.zeros_like(dd_ref)
            dgw_ref[...] = jnp.zeros_like(dgw_ref)

        fn = functools.partial(_ssd_step, dot=_gdot, cumsum=_cumsum_rows)
        _, vjp = jax.vjp(fn, xs[...], bm[...], cm[...], dt[...], b_ref[...], a_ref[...], d_ref[...], st_ref[...], z[...], gw_ref[...])
        dxs, dbm, dcm, ddt, db, da, dd, dst, dz, dgw = vjp((dyn_ref[...], dstate[...]))
        dxs_ref[...] = dxs
        dbm_ref[...] = dbm
        dcm_ref[...] = dcm
        ddt_ref[...] = ddt
        dz_ref[...] = dz.astype(dz_ref.dtype)
        db_ref[...] += db
        da_ref[...] += da
        dd_ref[...] += dd
        dgw_ref[...] += dgw
        dstate[...] = dst

    bc = pl.BlockSpec((SSD_CHUNK, LANES), lambda g, c: (rc(c), g))
    outs, rode = _pcall(
        body, grid=(SSD_N_GROUPS, n_chunks),
        in_specs=[sp["xs"], sp["bm"], sp["cm"], sp["dt"], sp["vec"], sp["vec"], sp["vec"], sp["z"], sp["gw"], sp["st"], sp["xs"]],
        out_specs=[sp["xs"], bc, bc, sp["dt"], sp["vec"], sp["vec"], sp["vec"], sp["xs"], sp["gw"]],
        out_shape=[jax.ShapeDtypeStruct((s_dim, SSD_D_INNER), F32),
                   jax.ShapeDtypeStruct((s_dim, SSD_N_GROUPS * SSD_D_STATE), F32),
                   jax.ShapeDtypeStruct((s_dim, SSD_N_GROUPS * SSD_D_STATE), F32),
                   jax.ShapeDtypeStruct((SSD_N_GROUPS, s_dim, LANES), F32),
                   jax.ShapeDtypeStruct((SSD_N_GROUPS, 1, LANES), F32),
                   jax.ShapeDtypeStruct((SSD_N_GROUPS, 1, LANES), F32),
                   jax.ShapeDtypeStruct((SSD_N_GROUPS, 1, LANES), F32),
                   jax.ShapeDtypeStruct((s_dim, SSD_ZX), BF16),
                   jax.ShapeDtypeStruct((1, SSD_D_INNER), F32)],
        scratch_shapes=[pltpu.VMEM((SSD_GROUP_W, SSD_D_STATE), F32)],
        args=[act, act, act, dtg, bias, alog, dskip, pzx, gw, states, dyn], sem=("arbitrary", "arbitrary"), name=name, rider=rider)
    return (outs, rode) if rider is not None else outs


SB_T = 128
SB_GROUP = 8
SB_WIDE = SB_GROUP * SB_T
SB_HB = 4
SB_SCALE = 1.0 / math.sqrt(SB_HEAD_DIM)


def _qknorm_fwd(proj, qw, kw, name, tm=512):
    s_dim = proj.shape[0]

    def body(q_ref, k_ref, v_ref, qw_ref, kw_ref, qo, ko, vo):
        qo[...] = _rms(q_ref[...], qw_ref[...], NORM_EPS).astype(BF16)
        ko[...] = _rms(k_ref[...], kw_ref[...], NORM_EPS).astype(BF16)
        vo[...] = v_ref[...].astype(BF16)

    blk = lambda o: pl.BlockSpec((tm, SB_HEAD_DIM), lambda i, h: (i, o + h))
    vec = pl.BlockSpec((1, SB_HEAD_DIM), lambda i, h: (0, 0))
    return pl.pallas_call(
        body, grid=(s_dim // tm, SB_N_HEADS),
        in_specs=[blk(0), blk(SB_N_HEADS), blk(2 * SB_N_HEADS), vec, vec],
        out_specs=[blk(0)] * 3,
        out_shape=[jax.ShapeDtypeStruct((s_dim, SB_WIDTH), BF16)] * 3,
        compiler_params=_params("parallel", "parallel"), name=name,
    )(proj, proj, proj, qw, kw)


def _qknorm_bwd(proj, qw, kw, dqn, dkn, name, tm=512):
    s_dim = proj.shape[0]

    def body(q_ref, k_ref, dq_ref, dk_ref, qw_ref, kw_ref, dqo, dko, dqw, dkw):
        @pl.when((pl.program_id(0) == 0) & (pl.program_id(1) == 0))
        def _():
            dqw[...] = jnp.zeros_like(dqw)
            dkw[...] = jnp.zeros_like(dkw)

        fn = lambda a, b: _rms(a, b, NORM_EPS)
        _, vq = jax.vjp(fn, q_ref[...], qw_ref[...])
        dq, dw = vq(dq_ref[...])
        dqo[...] = dq.astype(BF16)
        dqw[...] += dw
        _, vk = jax.vjp(fn, k_ref[...], kw_ref[...])
        dk, dw = vk(dk_ref[...])
        dko[...] = dk.astype(BF16)
        dkw[...] += dw

    blk = lambda o: pl.BlockSpec((tm, SB_HEAD_DIM), lambda i, h: (i, o + h))
    vec = pl.BlockSpec((1, SB_HEAD_DIM), lambda i, h: (0, 0))
    return pl.pallas_call(
        body, grid=(s_dim // tm, SB_N_HEADS),
        in_specs=[blk(0), blk(SB_N_HEADS), blk(0), blk(0), vec, vec],
        out_specs=[blk(0), blk(0), vec, vec],
        out_shape=[jax.ShapeDtypeStruct((s_dim, SB_WIDTH), BF16)] * 2 + [jax.ShapeDtypeStruct((1, SB_HEAD_DIM), F32)] * 2,
        compiler_params=_params("arbitrary", "arbitrary"), name=name,
    )(proj, proj, dqn, dkn, qw, kw)


def _sb_logits(q, k, strict):
    z = _dot(q, k, "nt") * SB_SCALE
    lb = jnp.minimum(z, 0.0) - jnp.log(1.0 + jnp.exp(-jnp.abs(z)))
    lm = lb - z
    if strict is not None:
        lm = jnp.where(strict, lm, 0.0)
    return lb, lm


def _sb_strict(qi, grp):
    r = lax.broadcasted_iota(jnp.int32, (SB_T, SB_WIDE), 0) + qi * SB_T
    c = lax.broadcasted_iota(jnp.int32, (SB_T, SB_WIDE), 1) + grp * SB_WIDE
    return c < r


def _head_lanes(hh):
    return slice(hh * SB_HEAD_DIM, (hh + 1) * SB_HEAD_DIM)


def _sb_fwd(qn, kn, vb, proj, name, rider=None):
    s_dim = qn.shape[0]
    nq = s_dim // SB_T
    assert nq % SB_GROUP == 0

    def body(q_ref, k_ref, v_ref, g_ref, og_ref, o_ref, t_ref):
        qi = pl.program_id(1)
        top = qi // SB_GROUP
        after = _tri(SB_T, True, strict=True)
        qs = [q_ref[:, _head_lanes(hh)] for hh in range(SB_HB)]

        def step(grp, masked, carries):
            start = pl.multiple_of(grp * SB_WIDE, SB_WIDE)
            strict = _sb_strict(qi, grp) if masked else None
            out = []
            for hh in range(SB_HB):
                o_acc, cr = carries[hh]
                k = k_ref[pl.ds(start, SB_WIDE), _head_lanes(hh)]
                v = v_ref[pl.ds(start, SB_WIDE), _head_lanes(hh)]
                lb, lm = _sb_logits(qs[hh], k, strict)
                rest = [None] * SB_GROUP
                for t in reversed(range(SB_GROUP)):
                    lm_t = lm[:, t * SB_T:(t + 1) * SB_T]
                    rest[t] = cr + _split_dot(lm_t, after, 2, True)
                    cr = cr + jnp.sum(lm_t, axis=1, keepdims=True)
                a = jnp.exp(lb + jnp.concatenate(rest, axis=1))
                if masked:
                    a = jnp.where(strict, a, 0.0)
                out.append((o_acc + _dot(a, v), cr))
            return tuple(out)

        init = tuple((jnp.zeros((SB_T, SB_HEAD_DIM), F32), jnp.zeros((SB_T, 1), F32)) for _ in range(SB_HB))
        carries = step(top, True, init)
        carries = lax.fori_loop(0, top, lambda i, c: step(top - 1 - i, False, c), carries)
        for hh in range(SB_HB):
            o, tot = carries[hh]
            g = g_ref[:, _head_lanes(hh)]
            o_ref[:, _head_lanes(hh)] = o
            og_ref[:, _head_lanes(hh)] = (o * (g * _sigmoid(g))).astype(og_ref.dtype)
            t_ref[hh] = jnp.broadcast_to(tot, (SB_T, LANES))

    wide = SB_HB * SB_HEAD_DIM
    qb = pl.BlockSpec((SB_T, wide), lambda h, i: (i, h))
    kv = pl.BlockSpec((s_dim, wide), lambda h, i: (0, h))
    outs, rode = _pcall(
        body, grid=(SB_N_HEADS // SB_HB, nq),
        in_specs=[qb, kv, kv, pl.BlockSpec((SB_T, wide), lambda h, i: (i, 3 * SB_N_HEADS // SB_HB + h))],
        out_specs=[qb, qb, pl.BlockSpec((SB_HB, SB_T, LANES), lambda h, i: (h, i, 0))],
        out_shape=[jax.ShapeDtypeStruct((s_dim, SB_WIDTH), BF16), jax.ShapeDtypeStruct((s_dim, SB_WIDTH), F32),
                   jax.ShapeDtypeStruct((SB_N_HEADS, s_dim, LANES), F32)],
        args=[qn, kn, vb, proj], sem=("parallel", "arbitrary"), name=name, rider=rider)
    return (outs, rode) if rider is not None else outs


def _sb_bwd(qn, kn, vb, proj, o, tot, dog, name, rider=None):
    s_dim = qn.shape[0]
    nq = s_dim // SB_T
    assert nq % SB_GROUP == 0

    def body(q_ref, k_ref, v_ref, g_ref, o_ref, t_ref, dog_ref, dq_ref, dk_ref, dv_ref, dvb_ref, dg_ref):
        qi = pl.program_id(1)
        top = qi // SB_GROUP

        @pl.when(qi == 0)
        def _():
            dk_ref[...] = jnp.zeros_like(dk_ref)
            dv_ref[...] = jnp.zeros_like(dv_ref)

        after = _tri(SB_T, True, strict=True)
        before = _tri(SB_T, False, strict=True)
        qs, dos, totals = [], [], []
        for hh in range(SB_HB):
            g = g_ref[:, _head_lanes(hh)]
            sg = _sigmoid(g)
            dog_v = dog_ref[:, _head_lanes(hh)]
            dg_ref[:, _head_lanes(hh)] = (dog_v * o_ref[:, _head_lanes(hh)] * (sg * (1.0 + g * (1.0 - sg)))).astype(dg_ref.dtype)
            dos.append((dog_v * (g * sg)).astype(BF16))
            qs.append(q_ref[:, _head_lanes(hh)])
            totals.append(t_ref[hh][:, 0:1])

        def step(grp, masked, carries):
            start = pl.multiple_of(grp * SB_WIDE, SB_WIDE)
            strict = _sb_strict(qi, grp) if masked else None
            out = []
            for hh in range(SB_HB):
                dq_acc, cp, ce = carries[hh]
                q, do = qs[hh], dos[hh]
                k = k_ref[pl.ds(start, SB_WIDE), _head_lanes(hh)]
                v = v_ref[pl.ds(start, SB_WIDE), _head_lanes(hh)]
                lb, lm = _sb_logits(q, k, strict)
                rest = []
                for t in range(SB_GROUP):
                    lm_t = lm[:, t * SB_T:(t + 1) * SB_T]
                    cp = cp + jnp.sum(lm_t, axis=1, keepdims=True)
                    rest.append((totals[hh] - cp) + _split_dot(lm_t, after, 2, True))
                a = jnp.exp(lb + jnp.concatenate(rest, axis=1))
                if masked:
                    a = jnp.where(strict, a, 0.0)
                e = a * _dot(do, v, "nt")
                excl = []
                for t in range(SB_GROUP):
                    e_t = e[:, t * SB_T:(t + 1) * SB_T]
                    excl.append(ce + _split_dot(e_t, before, 2, True))
                    ce = ce + jnp.sum(e_t, axis=1, keepdims=True)
                eex = jnp.concatenate(excl, axis=1)
                if masked:
                    eex = jnp.where(strict, eex, 0.0)
                sig = jnp.exp(lb)
                dz = (e * (1.0 - sig) - eex * sig) * SB_SCALE
                dv_ref[pl.ds(start, SB_WIDE), _head_lanes(hh)] += _dot(a, do, "tn")
                dk_ref[pl.ds(start, SB_WIDE), _head_lanes(hh)] += _dot(dz, q, "tn")
                out.append((dq_acc + _dot(dz, k), cp, ce))
            return tuple(out)

        zero = jnp.zeros((SB_T, 1), F32)
        init = tuple((jnp.zeros((SB_T, SB_HEAD_DIM), F32), zero, zero) for _ in range(SB_HB))
        carries = lax.fori_loop(0, top, lambda i, c: step(i, False, c), init)
        carries = step(top, True, carries)
        for hh in range(SB_HB):
            dq_ref[:, _head_lanes(hh)] = carries[hh][0]

        @pl.when(qi == nq - 1)
        def _():
            dvb_ref[...] = dv_ref[...].astype(BF16)

    wide = SB_HB * SB_HEAD_DIM
    qb = pl.BlockSpec((SB_T, wide), lambda h, i: (i, h))
    kv = pl.BlockSpec((s_dim, wide), lambda h, i: (0, h))
    outs, rode = _pcall(
        body, grid=(SB_N_HEADS // SB_HB, nq),
        in_specs=[qb, kv, kv, pl.BlockSpec((SB_T, wide), lambda h, i: (i, 3 * SB_N_HEADS // SB_HB + h)), qb,
                  pl.BlockSpec((SB_HB, SB_T, LANES), lambda h, i: (h, i, 0)), qb],
        out_specs=[qb, kv, kv, kv, qb],
        out_shape=[jax.ShapeDtypeStruct((s_dim, SB_WIDTH), F32), jax.ShapeDtypeStruct((s_dim, SB_WIDTH), F32),
                   jax.ShapeDtypeStruct((s_dim, SB_WIDTH), F32), jax.ShapeDtypeStruct((s_dim, SB_WIDTH), BF16),
                   jax.ShapeDtypeStruct((s_dim, SB_WIDTH), BF16)],
        args=[qn, kn, vb, proj, o, tot, dog], sem=("parallel", "arbitrary"), name=name, rider=rider)
    return (outs, rode) if rider is not None else outs


def _adamw_math(w, g, m, v):
    m = ADAM_B1 * m + (1.0 - ADAM_B1) * g
    v = ADAM_B2 * v + (1.0 - ADAM_B2) * (g * g)
    m_hat = m / (1.0 - ADAM_B1 ** ADAM_STEP)
    v_hat = v / (1.0 - ADAM_B2 ** ADAM_STEP)
    delta = -ADAM_LR * (m_hat / (jnp.sqrt(v_hat) + ADAM_EPS) + ADAM_WD * w)
    return delta, m, v


def _row_block(rows, cols, itemsize=4, limit=1 << 20):
    tr = rows
    while tr * cols * itemsize > limit and tr % (2 * BF16_ROWS) == 0:
        tr //= 2
    return tr


def _divisor_block(rows, cols, itemsize=4, limit=2 << 20):
    best = BF16_ROWS
    for t in range(BF16_ROWS, rows + 1, BF16_ROWS):
        if rows % t == 0 and t * cols * itemsize <= limit:
            best = t
    return best


def _adamw(w, g, m, v, name, rider=None):
    n, rows, cols = w.shape
    tr = rows if rows * cols * 4 <= (1 << 20) else _divisor_block(rows, cols, limit=1 << 20)

    def body(w_ref, g_ref, m_ref, v_ref, d_out, m_out, v_out):
        d, m_new, v_new = _adamw_math(w_ref[...], g_ref[...], m_ref[...], v_ref[...])
        d_out[...] = d
        m_out[...] = m_new
        v_out[...] = v_new

    blk = pl.BlockSpec((None, tr, cols), lambda i, j: (i, j, 0))
    outs, rode = _pcall(
        body, grid=(n, rows // tr), in_specs=[blk] * 4, out_specs=[blk] * 3,
        out_shape=[jax.ShapeDtypeStruct(w.shape, F32)] * 3,
        args=[w, g, m, v], sem=("parallel", "parallel"), name=name, rider=rider)
    return (outs, rode) if rider is not None else outs


_FLIPS = ((1, 0), (0, 1), (1, 1))


def _place():
    return lax.axis_index("x"), lax.axis_index("y"), lax.axis_index("c")


def _flip(v, f):
    return 1 - v if f else v


def _half_rows(ref, lead, hc, hr):
    return ref.at[(*lead, pl.ds(pl.multiple_of(hc * hr, BF16_ROWS), hr), slice(None))]


def _half_cols(ref, lead, hc, hw):
    return ref.at[(*lead, pl.ds(pl.multiple_of(hc * hw, LANES), hw))]


def _rows_of_chip(chip, r):
    return pl.ds(pl.multiple_of(chip * r, BF16_ROWS), r)


def _slot_half(gathered, shard_shape, chip, l, hc):
    r, c = shard_shape[1:]
    if len(gathered.shape) == 3:
        return _half_cols(gathered, (l, _rows_of_chip(chip, r)), hc, c // 2)
    return _half_rows(gathered, (chip, l), hc, r // 2)


def _shard_half(shard, stacked, l, hc):
    r, c = shard.shape[1:]
    return _half_cols(shard, (l, slice(None)), hc, c // 2) if stacked else _half_rows(shard, (l,), hc, r // 2)


def _remote(src, dst, send, recv, k, to):
    return pltpu.make_async_remote_copy(src_ref=src, dst_ref=dst, send_sem=send.at[k], recv_sem=recv.at[k], device_id=to,
                                        device_id_type=MESH)


def _comm_call(reads, writes, n_sems, phases, name):
    passed = [k for k, w in enumerate(writes) if not isinstance(w, jax.ShapeDtypeStruct)]
    n_rd = len(reads)

    def body(*refs):
        rd = refs[:n_rd]
        wr = refs[n_rd + len(passed):n_rd + len(passed) + len(writes)]
        send, recv = refs[-2:]
        for phase in phases:
            sends, arrivals = phase(rd, wr, send, recv)
            for cp in sends:
                cp.start()
            for cp in arrivals:
                cp.wait_recv()
            for cp in sends:
                cp.wait_send()

    return pl.pallas_call(
        body, in_specs=[_ANY] * (n_rd + len(passed)), out_specs=[_ANY] * len(writes),
        out_shape=[jax.ShapeDtypeStruct(w.shape, w.dtype) for w in writes],
        input_output_aliases={n_rd + pos: k for pos, k in enumerate(passed)},
        scratch_shapes=[pltpu.SemaphoreType.DMA((n_sems,)), pltpu.SemaphoreType.DMA((n_sems,))], name=name,
    )(*reads, *[writes[k] for k in passed])


def _ag_ici(pieces, names, base=0):
    def phase(shards, gathered, send, recv):
        x, y, c = _place()
        me = 2 * x + y
        sends, arrivals = [], []
        for k, (n, l) in enumerate(pieces):
            a = names.index(n)
            shape = shards[a].shape
            src = _shard_half(shards[a], len(gathered[a].shape) == 3, l, c)
            for j, (fx, fy) in enumerate(_FLIPS):
                tx, ty = _flip(x, fx), _flip(y, fy)
                sends.append(_remote(src, _slot_half(gathered[a], shape, me, l, c), send, recv, base + 3 * k + j, (tx, ty, c)))
                arrivals.append(_remote(src, _slot_half(gathered[a], shape, 2 * tx + ty, l, c), send, recv, base + 3 * k + j, (tx, ty, c)))
        return sends, arrivals

    return phase


def _ag_pass_on(pieces, names, shapes, base=0):
    def phase(_, gathered, send, recv):
        x, y, c = _place()
        sibling = (x, y, 1 - c)
        sends, arrivals = [], []
        for k, (n, l) in enumerate(pieces):
            a = names.index(n)
            for j, (fx, fy) in enumerate(_FLIPS):
                chip = 2 * _flip(x, fx) + _flip(y, fy)
                landed = _slot_half(gathered[a], shapes[a], chip, l, c)
                sends.append(_remote(landed, landed, send, recv, base + 3 * k + j, sibling))
                arrivals.append(_remote(landed, _slot_half(gathered[a], shapes[a], chip, l, 1 - c), send, recv, base + 3 * k + j, sibling))
        return sends, arrivals

    return phase


def _other_half(ref, hc):
    if len(ref.shape) == 3:
        return _half_cols(ref, (slice(None), slice(None)), hc, ref.shape[2] // 2)
    return _half_rows(ref, (slice(None), slice(None)), hc, ref.shape[2] // 2)


def _half_shape(shape):
    return shape[:2] + (shape[2] // 2,) if len(shape) == 3 else shape[:2] + (shape[2] // 2, shape[3])


def _exchange_phase(n_arr):
    def phase(ins, outs, send, recv):
        x, y, c = _place()
        cps = [_remote(_other_half(ins[a], 1 - c), outs[a], send, recv, a, (x, y, 1 - c)) for a in range(n_arr)]
        return cps, cps

    return phase


def _exchange_outs(grads):
    return [jax.ShapeDtypeStruct(_half_shape(g.shape), g.dtype) for g in grads]


def _pair_exchange(grads, name):
    return _comm_call(grads, _exchange_outs(grads), len(grads), [_exchange_phase(len(grads))], name)


def _exchange_rider(grads):
    return _Rider(grads, _exchange_outs(grads), len(grads), _exchange_phase(len(grads)))


def _pair_sum_stacked(g, got, place, name):
    _, rows, hw = got.shape
    tr = _divisor_block(rows, hw)

    def body(place_ref, g_ref, r_ref, o_ref):
        o_ref[...] = (g_ref[...].astype(F32) + r_ref[...].astype(F32)).astype(o_ref.dtype)

    blk = pl.BlockSpec((None, tr, hw), lambda i, pr: (0, i, 0))
    return pl.pallas_call(
        body,
        grid_spec=pltpu.PrefetchScalarGridSpec(
            num_scalar_prefetch=1, grid=(rows // tr,),
            in_specs=[pl.BlockSpec((None, tr, hw), lambda i, pr: (0, i, pr[1])), blk], out_specs=blk),
        out_shape=jax.ShapeDtypeStruct(got.shape, BF16),
        compiler_params=_params("parallel"), name=name,
    )(place, g, got)


def _pair_sum(g, got, place, name):
    if len(g.shape) == 3:
        return _pair_sum_stacked(g, got, place, name)
    _, layers, hr, cols = got.shape
    tr = _row_block(hr, cols)
    per = hr // tr

    def body(place_ref, g_ref, r_ref, o_ref):
        o_ref[...] = (g_ref[...].astype(F32) + r_ref[...].astype(F32)).astype(o_ref.dtype)

    blk = pl.BlockSpec((None, None, tr, cols), lambda k, l, i, pr: (k, l, i, 0))
    return pl.pallas_call(
        body,
        grid_spec=pltpu.PrefetchScalarGridSpec(
            num_scalar_prefetch=1, grid=(4, layers, per),
            in_specs=[pl.BlockSpec((None, None, tr, cols), lambda k, l, i, pr: (k, l, pr[1] * per + i, 0)), blk],
            out_specs=blk),
        out_shape=jax.ShapeDtypeStruct(got.shape, BF16),
        compiler_params=_params("parallel", "parallel", "parallel"), name=name,
    )(place, g, got)


def _scatter_phase(n_arr):
    def phase(ins, outs, send, recv):
        x, y, c = _place()
        cps = []
        for a in range(n_arr):
            for j, (fx, fy) in enumerate(_FLIPS):
                tx, ty = _flip(x, fx), _flip(y, fy)
                if len(ins[a].shape) == 3:
                    src = ins[a].at[:, _rows_of_chip(2 * tx + ty, ins[a].shape[1] // 4), :]
                else:
                    src = ins[a].at[2 * tx + ty]
                cps.append(_remote(src, outs[a].at[j], send, recv, 3 * a + j, (tx, ty, c)))
        return cps, cps

    return phase


def _scatter_outs(pairs):
    return [jax.ShapeDtypeStruct((3, 1, p.shape[1] // 4, p.shape[2]) if len(p.shape) == 3 else (3,) + p.shape[1:], p.dtype) for p in pairs]


def _chip_scatter(pairs, name):
    return _comm_call(pairs, _scatter_outs(pairs), 3 * len(pairs), [_scatter_phase(len(pairs))], name)


def _scatter_rider(pairs):
    return _Rider(pairs, _scatter_outs(pairs), 3 * len(pairs), _scatter_phase(len(pairs)))


def _chip_sum_stacked(p, got, place, layer, layers, o_buf, name, row0=0, rows=None):
    _, r, hw = got.shape[1:]
    rows = rows or r
    tr = _divisor_block(math.gcd(r, row0) if row0 else r, hw)
    per = r // tr
    first = row0 // tr

    def body(place_ref, p_ref, r_ref, *rest):
        o_ref = rest[-1]
        acc = p_ref[...].astype(F32)
        for j in range(3):
            acc = acc + r_ref[j].astype(F32)
        o_ref[...] = acc

    has_buf = o_buf is not None
    return pl.pallas_call(
        body,
        grid_spec=pltpu.PrefetchScalarGridSpec(
            num_scalar_prefetch=1, grid=(per,),
            in_specs=[pl.BlockSpec((None, tr, hw), lambda i, pr: (0, pr[0] * per + i, 0)),
                      pl.BlockSpec((3, None, tr, hw), lambda i, pr: (0, 0, i, 0))] + ([_ANY] if has_buf else []),
            out_specs=pl.BlockSpec((None, tr, hw), lambda i, pr: (layer, first + i, pr[1]))),
        out_shape=jax.ShapeDtypeStruct((layers, rows, 2 * hw), F32),
        input_output_aliases={3: 0} if has_buf else {},
        compiler_params=_params("parallel"), name=name,
    )(*((place, p, got) + ((o_buf,) if has_buf else ())))


def _chip_sum(p, got, place, layer, layers, o_buf, name):
    if len(p.shape) == 3:
        return _chip_sum_stacked(p, got, place, layer, layers, o_buf, name)
    _, _, hr, cols = p.shape
    tr = _row_block(hr, cols)
    per = hr // tr

    def body(place_ref, p_ref, r_ref, *rest):
        o_ref = rest[-1]
        acc = p_ref[...].astype(F32)
        for j in range(3):
            acc = acc + r_ref[j].astype(F32)
        o_ref[...] = acc

    has_buf = o_buf is not None
    return pl.pallas_call(
        body,
        grid_spec=pltpu.PrefetchScalarGridSpec(
            num_scalar_prefetch=1, grid=(per,),
            in_specs=[pl.BlockSpec((None, None, tr, cols), lambda i, pr: (pr[0], 0, i, 0)),
                      pl.BlockSpec((3, None, tr, cols), lambda i, pr: (0, 0, i, 0))] + ([_ANY] if has_buf else []),
            out_specs=pl.BlockSpec((None, tr, cols), lambda i, pr: (layer, pr[1] * per + i, 0))),
        out_shape=jax.ShapeDtypeStruct((layers, 2 * hr, cols), F32),
        input_output_aliases={3: 0} if has_buf else {},
        compiler_params=_params("parallel"), name=name,
    )(*((place, p, got) + ((o_buf,) if has_buf else ())))


def _pair_gather(halves, by_cols, name):
    def phase(_, bufs, send, recv):
        x, y, c = _place()
        sends, arrivals = [], []
        for a, h in enumerate(halves):
            cut = (lambda hc, a=a, h=h: _half_cols(bufs[a], (slice(None), slice(None)), hc, h.shape[2] // 2)) if by_cols[a] else (
                lambda hc, a=a, h=h: _half_rows(bufs[a], (slice(None),), hc, h.shape[1] // 2))
            sends.append(_remote(cut(c), cut(c), send, recv, a, (x, y, 1 - c)))
            arrivals.append(_remote(cut(c), cut(1 - c), send, recv, a, (x, y, 1 - c)))
        return sends, arrivals

    return _comm_call([], halves, len(halves), [phase], name)


def _allreduce_small(v, name):
    rows, cols = v.shape

    def body(v_ref, o_ref, buf, send_sems, recv_sems):
        x, y, c = _place()
        me = 4 * x + 2 * y + c
        buf[0] = v_ref[...]
        cps = []
        for k in range(1, 8):
            kx, ky, kc = (k >> 2) & 1, (k >> 1) & 1, k & 1
            cp = pltpu.make_async_remote_copy(src_ref=v_ref, dst_ref=buf.at[k], send_sem=send_sems.at[k - 1], recv_sem=recv_sems.at[k - 1],
                                              device_id=(_flip(x, kx), _flip(y, ky), _flip(c, kc)), device_id_type=MESH)
            cp.start()
            cps.append(cp)
        for cp in cps:
            cp.wait()
        acc = buf[me]
        for d in range(1, 8):
            acc = acc + buf[jnp.bitwise_xor(d, me)]
        o_ref[...] = acc

    vm = pl.BlockSpec(memory_space=pltpu.VMEM)
    return pl.pallas_call(
        body, in_specs=[vm], out_specs=vm, out_shape=jax.ShapeDtypeStruct((rows, cols), F32),
        scratch_shapes=[pltpu.VMEM((8, rows, cols), F32), pltpu.SemaphoreType.DMA((7,)), pltpu.SemaphoreType.DMA((7,))],
        name=name,
    )(v)


def _pad_lanes(a):
    return jnp.pad(a, ((0, 0), (0, LANES - a.shape[1])))


def _group_lanes(v):
    return jnp.pad(v.reshape(SSD_N_GROUPS, 1, 8), ((0, 0), (0, 0), (0, LANES - 8)))


def kernel(x, p, norm_w, ssd_in_w, ssd_conv_w, ssd_conv_b, ssd_dt_bias, ssd_a_log, ssd_d, ssd_gnorm_w, ssd_out_w, sb_in_w, sb_qn_w, sb_kn_w, sb_out_w, ple_norm_w, ple_gate_w, ple_proj_w, loss_target, m_norm_w, m_ssd_in_w, m_ssd_conv_w, m_ssd_conv_b, m_ssd_dt_bias, m_ssd_a_log, m_ssd_d, m_ssd_gnorm_w, m_ssd_out_w, m_sb_in_w, m_sb_qn_w, m_sb_kn_w, m_sb_out_w, m_ple_norm_w, m_ple_gate_w, m_ple_proj_w, v_norm_w, v_ssd_in_w, v_ssd_conv_w, v_ssd_conv_b, v_ssd_dt_bias, v_ssd_a_log, v_ssd_d, v_ssd_gnorm_w, v_ssd_out_w, v_sb_in_w, v_sb_qn_w, v_sb_kn_w, v_sb_out_w, v_ple_norm_w, v_ple_gate_w, v_ple_proj_w):
    w_in = dict(norm_w=norm_w, ssd_in_w=ssd_in_w, ssd_conv_w=ssd_conv_w, ssd_conv_b=ssd_conv_b, ssd_dt_bias=ssd_dt_bias,
                ssd_a_log=ssd_a_log, ssd_d=ssd_d, ssd_gnorm_w=ssd_gnorm_w, ssd_out_w=ssd_out_w, sb_in_w=sb_in_w, sb_qn_w=sb_qn_w,
                sb_kn_w=sb_kn_w, sb_out_w=sb_out_w, ple_norm_w=ple_norm_w, ple_gate_w=ple_gate_w, ple_proj_w=ple_proj_w)
    m_in = dict(norm_w=m_norm_w, ssd_in_w=m_ssd_in_w, ssd_conv_w=m_ssd_conv_w, ssd_conv_b=m_ssd_conv_b, ssd_dt_bias=m_ssd_dt_bias,
                ssd_a_log=m_ssd_a_log, ssd_d=m_ssd_d, ssd_gnorm_w=m_ssd_gnorm_w, ssd_out_w=m_ssd_out_w, sb_in_w=m_sb_in_w,
                sb_qn_w=m_sb_qn_w, sb_kn_w=m_sb_kn_w, sb_out_w=m_sb_out_w, ple_norm_w=m_ple_norm_w, ple_gate_w=m_ple_gate_w,
                ple_proj_w=m_ple_proj_w)
    v_in = dict(norm_w=v_norm_w, ssd_in_w=v_ssd_in_w, ssd_conv_w=v_ssd_conv_w, ssd_conv_b=v_ssd_conv_b, ssd_dt_bias=v_ssd_dt_bias,
                ssd_a_log=v_ssd_a_log, ssd_d=v_ssd_d, ssd_gnorm_w=v_ssd_gnorm_w, ssd_out_w=v_ssd_out_w, sb_in_w=v_sb_in_w,
                sb_qn_w=v_sb_qn_w, sb_kn_w=v_sb_kn_w, sb_out_w=v_sb_out_w, ple_norm_w=v_ple_norm_w, ple_gate_w=v_ple_gate_w,
                ple_proj_w=v_ple_proj_w)
    ix, iy, ic = lax.axis_index("x"), lax.axis_index("y"), lax.axis_index("c")
    chip = (2 * ix + iy).astype(jnp.int32)
    place = jnp.stack([chip, ic.astype(jnp.int32)])
    zero = jnp.zeros((), jnp.int32)
    big_names = [n for n, _, _ in _BIG]
    layers_of = {n: s[0] for n, s, _ in _BIG}
    cut_of = {n: cut for n, _, cut in _BIG}

    def layer_pieces(i):
        mixer = ("ssd_in_w", "ssd_out_w") if i % 2 == 0 else ("sb_in_w", "sb_out_w")
        return [(mixer[0], i // 2), (mixer[1], i // 2), ("ple_gate_w", i), ("ple_proj_w", i)]

    def names_of(pieces):
        return [n for n in big_names if any(n == q for q, _ in pieces)]

    held = lambda n, a: a.transpose(0, 2, 1) if cut_of[n] == "stack" else a
    mine = {n: held(n, w_in[n]).astype(BF16) for n in big_names}
    shard_shapes = [mine[n].shape for n in big_names]
    room = [jax.ShapeDtypeStruct((s[0], 4 * s[1], s[2]) if cut_of[n] == "stack" else (4,) + s, BF16) for n, s in zip(big_names, shard_shapes)]
    first = layer_pieces(0)[:1]
    gathered = _comm_call([mine[n] for n in big_names], room, 6 * len(first),
                          [_ag_ici(first, big_names), _ag_pass_on(first, big_names, shard_shapes, base=3 * len(first))], "allgather_layer0")
    gw = {}
    for n, g in zip(big_names, gathered):
        if cut_of[n] == "stack":
            layers, r, c = mine[n].shape
            gw[n] = lax.dynamic_update_slice(g.reshape(layers, 4, r, c), mine[n][:, None], (zero, chip, zero, zero)).reshape(g.shape)
        else:
            gw[n] = lax.dynamic_update_slice(g, mine[n][None], (chip, zero, zero, zero))

    lp = [layer_pieces(i) for i in range(DEPTH)]
    carries = {
        "ssd_in_0": (lp[0][1:2], []), "conv_0": (lp[0][2:], lp[0][1:2]), "ssd_0": (lp[1][:1], lp[0][2:]),
        "ssd_out_0": (lp[1][1:2], lp[1][:1]), "sb_in_1": (lp[1][2:], lp[1][1:2]), "sb_1": (lp[2][:2], lp[1][2:]),
        "sb_out_1": (lp[2][2:], lp[2][:2]), "ssd_in_2": (lp[3][1:], lp[2][2:]), "ssd_2": (lp[3][:1], lp[3][1:]),
        "ssd_out_2": ([], lp[3][:1]),
    }

    def gather_rider(call):
        if call not in carries:
            return None, lambda outs: outs
        ici, passing = carries[call]
        names = names_of(ici + passing)
        phases = ([_ag_ici(ici, names)] if ici else []) + (
            [_ag_pass_on(passing, names, [mine[n].shape for n in names], base=3 * len(ici))] if passing else [])

        def issue(rd, wr, send, recv):
            both = [ph(rd, wr, send, recv) for ph in phases]
            return sum((b[0] for b in both), []), sum((b[1] for b in both), [])

        def land(outs):
            outs, bufs = outs
            for n, g in zip(names, bufs):
                gw[n] = g
            return outs

        return _Rider([mine[n] for n in names], [gw[n] for n in names], 3 * (len(ici) + len(passing)), issue), land

    onehot = (jnp.arange(4) == chip).astype(F32) * (ic == 0).astype(F32)
    cw_mine = onehot[:, None, None, None] * ssd_conv_w[None]
    cw_full = _allreduce_small(cw_mine.transpose(1, 2, 0, 3).reshape(-1, LANES), "gather_conv_w").reshape(2, SSD_D_CONV, SSD_CONV_DIM)

    def wmm(a, name, layer, *, dn="nn", res=None, call, rider=None):
        return _matmul(a, gw[name], dn=dn, res=res, b_lay=(cut_of[name], layer), name=call, rider=rider)

    h = x[0]
    target = loss_target[0]
    saved = []
    for i in range(DEPTH):
        j = i // 2
        nw = norm_w[i:i + 1]
        pw = ple_norm_w[i:i + 1]
        s = dict(h=h)
        u = _rms_fwd(h, nw, f"rms_{i}")
        s["u"] = u
        if i % 2 == 0:
            w_dt = jnp.pad(gw["ssd_in_w"][j, SSD_ZX:], ((0, LANES - SSD_N_HEADS), (0, 0)))
            rider, land = gather_rider(f"ssd_in_{i}")
            pzx = land(_matmul(u, gw["ssd_in_w"], dn="nt", b_lay=("stack", j, SSD_ZX), name=f"ssd_in_{i}", rider=rider))
            pdt = _matmul(u, w_dt, dn="nt", name=f"ssd_indt_{i}")
            rider, land = gather_rider(f"conv_{i}")
            act = land(_conv_fwd(pzx, cw_full[j], ssd_conv_b[j:j + 1], f"conv_{i}", rider=rider))
            dtg = jnp.pad(pdt[:, :SSD_N_HEADS].reshape(-1, SSD_N_GROUPS, 8).transpose(1, 0, 2), ((0, 0), (0, 0), (0, LANES - 8)))
            vecs = (_group_lanes(ssd_dt_bias[j]), _group_lanes(ssd_a_log[j]), _group_lanes(ssd_d[j]))
            rider, land = gather_rider(f"ssd_{i}")
            yn, states = land(_ssd_fwd(act, dtg, *vecs, pzx, ssd_gnorm_w[j:j + 1], f"ssd_{i}", rider=rider))
            s.update(w_dt=w_dt, pzx=pzx, act=act, dtg=dtg, vecs=vecs, yn=yn, states=states)
            rider, land = gather_rider(f"ssd_out_{i}")
            h1 = land(wmm(yn, "ssd_out_w", j, res=h, call=f"ssd_out_{i}", rider=rider))
        else:
            rider, land = gather_rider(f"sb_in_{i}")
            proj = land(wmm(u, "sb_in_w", j, call=f"sb_in_{i}", rider=rider))
            qn, kn, vb = _qknorm_fwd(proj, sb_qn_w[j:j + 1], sb_kn_w[j:j + 1], f"qknorm_{i}")
            rider, land = gather_rider(f"sb_{i}")
            og, o, tot = land(_sb_fwd(qn, kn, vb, proj, f"sb_{i}", rider=rider))
            s.update(proj=proj, qn=qn, kn=kn, vb=vb, og=og, o=o, tot=tot)
            rider, land = gather_rider(f"sb_out_{i}")
            h1 = land(wmm(og, "sb_out_w", j, res=h, call=f"sb_out_{i}", rider=rider))
        n2 = _rms_fwd(h1, pw, f"ple_rms_{i}")
        gl = wmm(n2, "ple_gate_w", i, call=f"ple_gate_{i}")
        pp = wmm(p[i, 0], "ple_proj_w", i, call=f"ple_proj_{i}")
        h = _ple_fwd(h1, pp, gl, f"ple_{i}")
        s.update(h1=h1, n2=n2, gl=gl, pp=pp)
        saved.append(s)

    dh, loss_lanes = _loss_bwd(h, target, "loss")

    wg = {}
    gsmall = {n: [None] * s[0] for n, s in _SMALL}
    g_conv_w = [None, None]
    scat = {}
    pending = late = None

    def wgrad(a, b, name, layer, call, rider=None):
        out = _matmul(a, b, dn="tn", out_dtype=BF16, o_lay=(cut_of[name], 0, 1), name=call, rider=rider)
        wg[(name, layer)], rode = out if rider is not None else (out, None)
        return rode

    def pair_sums(pieces, got, tag):
        return pieces, [_pair_sum(wg[q], r, place, f"rs_pair_sum_{tag}_{k}") for k, (q, r) in enumerate(zip(pieces, got))]

    def sibling_rider(pieces):
        return _exchange_rider([wg[q] for q in pieces])

    def riding_with(own):
        return (pending[0] + own[0], pending[1] + own[1]) if pending else own

    def arrived(sent, got):
        for q, pair, g in zip(sent[0], sent[1], got):
            scat[q] = (pair, g)

    for i in reversed(range(DEPTH)):
        j = i // 2
        s = saved[i]
        nw = norm_w[i:i + 1]
        pw = ple_norm_w[i:i + 1]
        dpp, dgl = _ple_bwd(dh, s["pp"], s["gl"], f"ple_bwd_{i}")
        wgrad(p[i, 0], dpp, "ple_proj_w", i, f"d_ple_proj_{i}")
        if late is None:
            wgrad(s["n2"], dgl, "ple_gate_w", i, f"d_ple_gate_{i}")
        else:
            pending = pair_sums(late, wgrad(s["n2"], dgl, "ple_gate_w", i, f"d_ple_gate_{i}", rider=sibling_rider(late)), f"{i + 1}_in")
        dn2 = wmm(dgl, "ple_gate_w", i, dn="nt", call=f"ple_gate_bwd_{i}")
        dh1, dpw = _rms_bwd(s["h1"], pw, dn2, dh, f"ple_rms_bwd_{i}")
        gsmall["ple_norm_w"][i] = dpw
        if i % 2 == 0:
            wgrad(s["yn"], dh1, "ssd_out_w", j, f"d_ssd_out_{i}")
            early = layer_pieces(i)[1:]
            dyn, got = wmm(dh1, "ssd_out_w", j, dn="nt", call=f"ssd_out_bwd_{i}", rider=sibling_rider(early))
            riding = riding_with(pair_sums(early, got, f"{i}_out"))
            outs, got = _ssd_bwd(s["act"], s["dtg"], *s["vecs"], s["pzx"], ssd_gnorm_w[j:j + 1], s["states"], dyn, f"ssd_bwd_{i}",
                                 rider=_scatter_rider(riding[1]))
            arrived(riding, got)
            dxs, dbm, dcm, ddtg, dbias, dalog, ddsk, dz, dgw = outs
            dzx, dcw, dcb = _conv_bwd(s["pzx"], cw_full[j], ssd_conv_b[j:j + 1], dxs, dbm, dcm, dz, f"conv_bwd_{i}")
            ddt = _pad_lanes(ddtg[:, :, :8].transpose(1, 0, 2).reshape(-1, SSD_N_HEADS)).astype(BF16)
            dwt = _matmul(dzx, s["u"], dn="tn", out_dtype=BF16, out_rows=SSD_IN_DIM, name=f"d_ssd_in_{i}")
            dwt_dt = _matmul(ddt, s["u"], dn="tn", out_dtype=BF16, name=f"d_ssd_indt_{i}")
            wg[("ssd_in_w", j)] = lax.dynamic_update_slice(dwt, dwt_dt[:SSD_N_HEADS], (SSD_ZX, 0))[None]
            if i == 0:
                by_shard = wg[("ssd_in_w", 0)].reshape(4, -1, D_MODEL)
                parts = [("ssd_in_w", 0, 0), ("ssd_in_w", 0, 1)]
                wg[parts[0]] = by_shard[:, :LAST_SPLIT].reshape(1, -1, D_MODEL)
                wg[parts[1]] = by_shard[:, LAST_SPLIT:].reshape(1, -1, D_MODEL)
                last = pair_sums(parts, _pair_exchange([wg[q] for q in parts], "rs_pair_exchange_last"), "0_in")
                du, got = _matmul(dzx, gw["ssd_in_w"], b_lay=("stack", j, SSD_ZX), name=f"ssd_in_bwd_{i}",
                                  rider=_scatter_rider(last[1][1:]))
                arrived((parts[1:], last[1][1:]), got)
            else:
                du = _matmul(dzx, gw["ssd_in_w"], b_lay=("stack", j, SSD_ZX), name=f"ssd_in_bwd_{i}")
            du = _matmul(ddt, s["w_dt"], res=du, name=f"ssd_indt_bwd_{i}")
            g_conv_w[j] = dcw
            gsmall["ssd_conv_b"][j] = dcb
            gsmall["ssd_dt_bias"][j] = dbias[:, 0, :8].reshape(1, SSD_N_HEADS)
            gsmall["ssd_a_log"][j] = dalog[:, 0, :8].reshape(1, SSD_N_HEADS)
            gsmall["ssd_d"][j] = ddsk[:, 0, :8].reshape(1, SSD_N_HEADS)
            gsmall["ssd_gnorm_w"][j] = dgw
        else:
            wgrad(s["og"], dh1, "sb_out_w", j, f"d_sb_out_{i}")
            early = layer_pieces(i)[1:]
            dog, got = wmm(dh1, "sb_out_w", j, dn="nt", call=f"sb_out_bwd_{i}", rider=sibling_rider(early))
            riding = riding_with(pair_sums(early, got, f"{i}_out"))
            outs, got = _sb_bwd(s["qn"], s["kn"], s["vb"], s["proj"], s["o"], s["tot"], dog, f"sb_bwd_{i}", rider=_scatter_rider(riding[1]))
            arrived(riding, got)
            dqn, dkn, _, dvb, dg = outs
            dq, dk, dqw, dkw = _qknorm_bwd(s["proj"], sb_qn_w[j:j + 1], sb_kn_w[j:j + 1], dqn, dkn, f"qknorm_bwd_{i}")
            dproj = jnp.concatenate([dq, dk, dvb, dg], axis=1)
            du = wmm(dproj, "sb_in_w", j, dn="nt", call=f"sb_in_bwd_{i}")
            wgrad(s["u"], dproj, "sb_in_w", j, f"d_sb_in_{i}")
            gsmall["sb_qn_w"][j] = dqw
            gsmall["sb_kn_w"][j] = dkw
        dh, dnw = _rms_bwd(s["h"], nw, du, dh1, f"rms_bwd_{i}")
        gsmall["norm_w"][i] = dnw
        late = layer_pieces(i)[:1]
    grad_x = dh[None]

    def reduced(names, call):
        halves = []
        for n in names:
            buf = None
            for l in range(layers_of[n]):
                if (n, l, 0) in scat:
                    r = shard_shapes[big_names.index(n)][1]
                    for part, row0 in ((0, 0), (1, LAST_SPLIT)):
                        buf = _chip_sum_stacked(*scat[(n, l, part)], place, l, layers_of[n], buf, f"rs_chip_sum_{n}_{l}_{part}", row0, r)
                else:
                    buf = _chip_sum(*scat[(n, l)], place, l, layers_of[n], buf, f"rs_chip_sum_{n}_{l}")
            halves.append(buf)
        return dict(zip(names, _pair_gather(halves, [cut_of[n] == "stack" for n in names], call)))

    def updated(n, rider=None):
        return _adamw(held(n, w_in[n]), g_big[n], held(n, m_in[n]), held(n, v_in[n]), f"adamw_{n}", rider=rider)

    done_early = ["sb_in_w", "sb_out_w"]
    g_big = reduced(done_early, "rs_pair_gather_sb")
    step = {}
    step["sb_in_w"], got = updated("sb_in_w", rider=_scatter_rider(last[1][:1]))
    arrived((last[0][:1], last[1][:1]), got)
    g_big.update(reduced([n for n in big_names if n not in done_early], "rs_pair_gather"))

    small_parts = [jnp.concatenate(gsmall[n], axis=0).reshape(-1) for n, _ in _SMALL]
    small_parts.append(jnp.stack(g_conv_w).reshape(-1))
    small_parts.append(loss_lanes.reshape(-1))
    small_sum = _allreduce_small(jnp.concatenate(small_parts).reshape(-1, LANES), "allreduce_small").reshape(-1)
    g_small, off = {}, 0
    for n, shape in _SMALL:
        size = math.prod(shape)
        g_small[n] = small_sum[off:off + size].reshape(shape)
        off += size
    cw_size = 2 * SSD_D_CONV * SSD_CONV_DIM
    g_cw_full = small_sum[off:off + cw_size].reshape(2, SSD_D_CONV, 4, SSD_CONV_DIM // 4)
    g_small["ssd_conv_w"] = jnp.sum(g_cw_full * (jnp.arange(4) == chip).astype(F32)[None, None, :, None], axis=2)
    loss = 0.5 * jnp.sum(small_sum[off + cw_size:]) / D_MODEL

    grads, delta, new_m, new_v = {}, {}, {}, {}
    for n in big_names:
        grads[n], delta[n], new_m[n], new_v[n] = (held(n, a) for a in (g_big[n], *(step[n] if n in step else updated(n))))
    small_names = [n for n, _ in _SMALL] + ["ssd_conv_w"]
    pack = lambda d: jnp.concatenate([d[n].reshape(-1) for n in small_names]).reshape(1, -1, LANES)
    ds, ms, vs = _adamw(pack(w_in), pack(g_small), pack(m_in), pack(v_in), "adamw_small")
    off = 0
    for n in small_names:
        shape = w_in[n].shape
        size = math.prod(shape)
        grads[n] = g_small[n]
        delta[n] = ds.reshape(-1)[off:off + size].reshape(shape)
        new_m[n] = ms.reshape(-1)[off:off + size].reshape(shape)
        new_v[n] = vs.reshape(-1)[off:off + size].reshape(shape)
        off += size

    order = ["norm_w", "ssd_in_w", "ssd_conv_w", "ssd_conv_b", "ssd_dt_bias", "ssd_a_log", "ssd_d", "ssd_gnorm_w", "ssd_out_w",
             "sb_in_w", "sb_qn_w", "sb_kn_w", "sb_out_w", "ple_norm_w", "ple_gate_w", "ple_proj_w"]
    return (loss, grad_x, *[grads[n] for n in order], *[delta[n] for n in order], *[new_m[n] for n in order],
            *[new_v[n] for n in order])
```

```python
import functools
import math

import jax
import jax.numpy as jnp
from jax import lax
from jax.experimental import pallas as pl
from jax.experimental.pallas import tpu as pltpu

F32 = jnp.float32
BF16 = jnp.bfloat16
MESH = pl.DeviceIdType.MESH

D_MODEL = 2048
DEPTH = 4
SSD_D_INNER = 4096
SSD_N_GROUPS = 8
SSD_GROUP_W = SSD_D_INNER // SSD_N_GROUPS
SSD_D_STATE = 128
SSD_CHUNK = 128
SSD_CONV_DIM = 6144
SSD_D_CONV = 4
SSD_N_HEADS = 64
SB_HEAD_DIM = 128
SB_N_HEADS = 16
SB_WIDTH = 2048
NORM_EPS = 1e-6
GATED_NORM_EPS = 1e-5
ADAM_LR = 0.001
ADAM_B1 = 0.9
ADAM_B2 = 0.999
ADAM_EPS = 1e-08
ADAM_WD = 0.01
ADAM_STEP = 10

SSD_ZX = SSD_D_INNER + SSD_CONV_DIM
SSD_IN_DIM = SSD_ZX + SSD_N_HEADS
LAST_SPLIT = 1104
LANES = 128
BF16_ROWS = 16

_BIG = (
    ("ssd_in_w", (2, 2576, 2048), "stack"),
    ("ssd_out_w", (2, 1024, 2048), "row"),
    ("sb_in_w", (2, 2048, 2048), "col"),
    ("sb_out_w", (2, 512, 2048), "row"),
    ("ple_gate_w", (4, 512, 2048), "row"),
    ("ple_proj_w", (4, 256, 512), "col"),
)
_SMALL = (
    ("norm_w", (4, 2048)),
    ("ssd_conv_b", (2, 6144)),
    ("ssd_dt_bias", (2, 64)),
    ("ssd_a_log", (2, 64)),
    ("ssd_d", (2, 64)),
    ("ssd_gnorm_w", (2, 4096)),
    ("sb_qn_w", (2, 128)),
    ("sb_kn_w", (2, 128)),
    ("ple_norm_w", (4, 2048)),
)

_DN = {
    "nn": (((1,), (0,)), ((), ())),
    "nt": (((1,), (1,)), ((), ())),
    "tn": (((0,), (0,)), ((), ())),
}


def _dot(a, b, dn="nn"):
    return lax.dot_general(a.astype(BF16), b.astype(BF16), _DN[dn], preferred_element_type=F32)


@functools.partial(jax.custom_vjp, nondiff_argnums=(2,))
def _gdot(a, b, dn):
    return _dot(a, b, dn)


def _gdot_fwd(a, b, dn):
    return _dot(a, b, dn), (a, b)


def _gdot_bwd(dn, res, g):
    a, b = res
    if dn == "nn":
        return _dot(g, b, "nt"), _dot(a, g, "tn")
    if dn == "nt":
        return _dot(g, b, "nn"), _dot(g, a, "tn")
    return _dot(b, g, "nt"), _dot(a, g, "nn")


_gdot.defvjp(_gdot_fwd, _gdot_bwd)


def _split_dot(x, t, parts, x_left):
    acc = None
    r = x
    for i in range(parts):
        p = r.astype(BF16)
        d = lax.dot_general(p, t, _DN["nn"], preferred_element_type=F32) if x_left else lax.dot_general(
            t, p, _DN["nn"], preferred_element_type=F32)
        acc = d if acc is None else acc + d
        if i + 1 < parts:
            r = r - p.astype(F32)
    return acc


def _tri(n, lower, strict=False):
    r = lax.broadcasted_iota(jnp.int32, (n, n), 0)
    c = lax.broadcasted_iota(jnp.int32, (n, n), 1)
    keep = (r > c if strict else r >= c) if lower else (r < c if strict else r <= c)
    return jnp.where(keep, 1.0, 0.0).astype(BF16)


def _cumsum_rows_raw(x):
    return _split_dot(x, _tri(x.shape[0], True), 3, False)


@jax.custom_vjp
def _cumsum_rows(x):
    return _cumsum_rows_raw(x)


def _cumsum_rows_fwd(x):
    return _cumsum_rows_raw(x), None


def _cumsum_rows_bwd(_, g):
    return (_split_dot(g, _tri(g.shape[0], False), 3, False),)


_cumsum_rows.defvjp(_cumsum_rows_fwd, _cumsum_rows_bwd)


def _sigmoid(x):
    return 1.0 / (1.0 + jnp.exp(-x))


def _softplus(x):
    return jnp.maximum(x, 0.0) + jnp.log(1.0 + jnp.exp(-jnp.abs(x)))


def _rms(x, w, eps):
    return x * lax.rsqrt(jnp.mean(x * x, axis=-1, keepdims=True) + eps) * w


_ANY = pl.BlockSpec(memory_space=pl.ANY)


def _params(*sem):
    return pltpu.CompilerParams(dimension_semantics=sem)


class _Rider:
    def __init__(self, reads, writes, n_sems, issue):
        self.reads, self.writes, self.n_sems, self.issue = list(reads), list(writes), n_sems, issue


def _pcall(body, *, grid, in_specs, out_specs, out_shape, args, sem, name, scratch_shapes=(), aliases=None, rider=None):
    aliases = dict(aliases or {})
    if rider is None:
        outs = pl.pallas_call(body, grid=grid, in_specs=in_specs, out_specs=out_specs, out_shape=out_shape,
                              scratch_shapes=list(scratch_shapes), input_output_aliases=aliases,
                              compiler_params=_params(*sem), name=name)(*args)
        return list(outs), []
    n_in, n_out, n_scr, n_rd, n_wr = len(args), len(out_shape), len(scratch_shapes), len(rider.reads), len(rider.writes)
    passed = [k for k, w in enumerate(rider.writes) if not isinstance(w, jax.ShapeDtypeStruct)]
    for pos, k in enumerate(passed):
        aliases[n_in + n_rd + pos] = n_out + k

    def wrapped(*refs):
        ins = refs[:n_in]
        reads = refs[n_in:n_in + n_rd]
        base = n_in + n_rd + len(passed)
        outs = refs[base:base + n_out]
        writes = refs[base + n_out:base + n_out + n_wr]
        scr = refs[base + n_out + n_wr:base + n_out + n_wr + n_scr]
        send, recv = refs[-2:]
        first = last = None
        for d, n in enumerate(grid):
            i = pl.program_id(d)
            first = (i == 0) if first is None else first & (i == 0)
            last = (i == n - 1) if last is None else last & (i == n - 1)

        @pl.when(first)
        def _():
            for cp in rider.issue(reads, writes, send, recv)[0]:
                cp.start()

        body(*ins, *outs, *scr)

        @pl.when(last)
        def _():
            sends, arrivals = rider.issue(reads, writes, send, recv)
            for cp in arrivals:
                cp.wait_recv()
            for cp in sends:
                cp.wait_send()

    outs = pl.pallas_call(
        wrapped, grid=grid,
        in_specs=list(in_specs) + [_ANY] * (n_rd + len(passed)),
        out_specs=list(out_specs) + [_ANY] * n_wr,
        out_shape=list(out_shape) + [jax.ShapeDtypeStruct(w.shape, w.dtype) for w in rider.writes],
        scratch_shapes=list(scratch_shapes) + [pltpu.SemaphoreType.DMA((rider.n_sems,)), pltpu.SemaphoreType.DMA((rider.n_sems,))],
        input_output_aliases=aliases, compiler_params=_params(*(["arbitrary"] * len(grid))), name=name,
    )(*args, *rider.reads, *[rider.writes[k] for k in passed])
    return list(outs[:n_out]), list(outs[n_out:])


MM_TK = 2048


def _pick(dim, pref, unit=None):
    t = pref
    while t >= LANES:
        if dim % t == 0 and (unit is None or unit % t == 0):
            return t
        t //= 2
    return dim


def _matmul(a, b, *, dn="nn", res=None, out_dtype=F32, name, b_lay=None, o_lay=None, o_buf=None, out_rows=None, rider=None):
    if dn == "tn":
        k_dim, m_dim = a.shape
    else:
        m_dim, k_dim = a.shape
    unit_m = unit_n = unit_k = None
    if b_lay is None:
        n_dim = b.shape[0] if dn == "nt" else b.shape[1]
    elif b_lay[0] == "stack":
        cut, layer, rows = b_lay
        cols = b.shape[2]
        n_dim = cols if dn == "nn" else rows
        assert k_dim == (rows if dn == "nn" else cols) and dn != "tn"
    else:
        cut, layer = b_lay
        r, c = b.shape[2:]
        rows, cols = (4 * r, c) if cut == "row" else (r, 4 * c)
        n_dim = cols if dn == "nn" else rows
        assert k_dim == (rows if dn == "nn" else cols) and dn != "tn"
        if (cut == "row") == (dn == "nn"):
            unit_k = r if cut == "row" else c
        else:
            unit_n = r if cut == "row" else c
    if o_lay is not None:
        o_cut, o_layer, o_layers = o_lay
        if o_cut == "row":
            unit_m = m_dim // 4
        else:
            unit_n = n_dim // 4
    tm, tn, tk = _pick(m_dim, 1024, unit_m), _pick(n_dim, 1024, unit_n), _pick(k_dim, MM_TK, unit_k)
    nk = k_dim // tk
    a_spec = pl.BlockSpec((tk, tm), lambda i, j, k: (k, i)) if dn == "tn" else pl.BlockSpec((tm, tk), lambda i, j, k: (i, k))
    if b_lay is None:
        b_spec = pl.BlockSpec((tn, tk), lambda i, j, k: (j, k)) if dn == "nt" else pl.BlockSpec((tk, tn), lambda i, j, k: (k, j))
    elif cut == "stack":
        b_spec = (pl.BlockSpec((None, tk, tn), lambda i, j, k: (layer, k, j)) if dn == "nn" else
                  pl.BlockSpec((None, tn, tk), lambda i, j, k: (layer, j, k)))
    elif dn == "nn" and cut == "row":
        per = r // tk
        b_spec = pl.BlockSpec((None, None, tk, tn), lambda i, j, k: (k // per, layer, k % per, j))
    elif dn == "nn":
        per = c // tn
        b_spec = pl.BlockSpec((None, None, tk, tn), lambda i, j, k: (j // per, layer, k, j % per))
    elif cut == "row":
        per = r // tn
        b_spec = pl.BlockSpec((None, None, tn, tk), lambda i, j, k: (j // per, layer, j % per, k))
    else:
        per = c // tk
        b_spec = pl.BlockSpec((None, None, tn, tk), lambda i, j, k: (k // per, layer, j, k % per))
    r_spec = pl.BlockSpec((tm, tn), lambda i, j, k: (i, j))
    if o_lay is None:
        o_spec = r_spec
        out_shape = jax.ShapeDtypeStruct((out_rows or m_dim, n_dim), out_dtype)
    elif o_cut == "row":
        per_o = unit_m // tm
        o_spec = pl.BlockSpec((None, None, tm, tn), lambda i, j, k: (i // per_o, o_layer, i % per_o, j))
        out_shape = jax.ShapeDtypeStruct((4, o_layers, unit_m, n_dim), out_dtype)
    else:
        per_o = unit_n // tn
        o_spec = pl.BlockSpec((None, None, tm, tn), lambda i, j, k: (j // per_o, o_layer, i, j % per_o))
        out_shape = jax.ShapeDtypeStruct((4, o_layers, m_dim, unit_n), out_dtype)
    has_res = res is not None
    has_buf = o_buf is not None

    def body(*refs):
        a_ref, b_ref = refs[:2]
        r_ref = refs[2] if has_res else None
        o_ref = refs[-1] if nk == 1 else refs[-2]

        def finish(v):
            if has_res:
                v = v + r_ref[...]
            o_ref[...] = v.astype(o_ref.dtype)

        if nk == 1:
            finish(_dot(a_ref[...], b_ref[...], dn))
            return
        acc_ref = refs[-1]
        k = pl.program_id(2)

        @pl.when(k == 0)
        def _():
            acc_ref[...] = jnp.zeros_like(acc_ref)

        acc_ref[...] += _dot(a_ref[...], b_ref[...], dn)

        @pl.when(k == nk - 1)
        def _():
            finish(acc_ref[...])

    args = [a, b] + ([res] if has_res else []) + ([o_buf] if has_buf else [])
    outs, rode = _pcall(
        body, grid=(m_dim // tm, n_dim // tn, nk),
        in_specs=[a_spec, b_spec] + ([r_spec] if has_res else []) + ([_ANY] if has_buf else []),
        out_specs=[o_spec], out_shape=[out_shape],
        scratch_shapes=[] if nk == 1 else [pltpu.VMEM((tm, tn), F32)],
        aliases={len(args) - 1: 0} if has_buf else {},
        args=args, sem=("parallel", "parallel", "arbitrary"), name=name, rider=rider)
    return (outs[0], rode) if rider is not None else outs[0]


def _rowcall(fn, rows, consts, outs, accs, *, name, tm=256):
    args = list(rows) + list(consts)
    in_specs = [pl.BlockSpec((tm, r.shape[1]), lambda i: (i, 0)) for r in rows]
    in_specs += [pl.BlockSpec(c.shape, lambda i: (0, 0)) for c in consts]
    s_dim = args[0].shape[0]
    n_in, n_out = len(args), len(outs)
    out_shape = [jax.ShapeDtypeStruct((s_dim, w), dt) for w, dt in outs] + [jax.ShapeDtypeStruct(s, F32) for s in accs]
    out_specs = [pl.BlockSpec((tm, w), lambda i: (i, 0)) for w, _ in outs] + [pl.BlockSpec(s, lambda i: (0, 0)) for s in accs]

    def body(*refs):
        vals = fn(*[r[...] for r in refs[:n_in]])
        o_refs = refs[n_in:n_in + n_out]
        a_refs = refs[n_in + n_out:]
        for o, v in zip(o_refs, vals[:n_out]):
            o[...] = v.astype(o.dtype)
        if a_refs:
            @pl.when(pl.program_id(0) == 0)
            def _():
                for a_ref in a_refs:
                    a_ref[...] = jnp.zeros_like(a_ref)

            for a_ref, v in zip(a_refs, vals[n_out:]):
                a_ref[...] += v

    return pl.pallas_call(
        body, grid=(s_dim // tm,), in_specs=in_specs, out_specs=out_specs, out_shape=out_shape,
        compiler_params=_params("arbitrary"), name=name,
    )(*args)


def _rms_fwd(h, w, name):
    return _rowcall(lambda x, w_: (_rms(x, w_, NORM_EPS),), [h], [w], [(h.shape[1], BF16)], [], name=name)[0]


def _rms_bwd(h, w, dy, dres, name):
    def fn(x, dy_, dres_, w_):
        _, vjp = jax.vjp(lambda a, b: _rms(a, b, NORM_EPS), x, w_)
        dx, dw = vjp(dy_)
        return dx + dres_, dw

    return _rowcall(fn, [h, dy, dres], [w], [(h.shape[1], F32)], [w.shape], name=name)


def _ple_fwd(h1, pp, gl, name):
    return _rowcall(lambda a, b, c: (a + b * _sigmoid(c),), [h1, pp, gl], [], [(h1.shape[1], F32)], [], name=name)[0]


def _ple_bwd(dh2, pp, gl, name):
    def fn(d, b, c):
        gate = _sigmoid(c)
        return d * gate, d * b * gate * (1.0 - gate)

    return _rowcall(fn, [dh2, pp, gl], [], [(dh2.shape[1], BF16), (dh2.shape[1], BF16)], [], name=name)


def _loss_bwd(y, target, name):
    width = y.shape[1]

    def fn(a, t):
        d = a - t
        col = jnp.sum(d * d, axis=0, keepdims=True)
        part = col[:, 0:LANES]
        for j in range(1, width // LANES):
            part = part + col[:, j * LANES:(j + 1) * LANES]
        return d * (1.0 / width), part

    return _rowcall(fn, [y, target], [], [(width, F32)], [(1, LANES)], name=name)


CONV_TC = 256


def _shift_down(x, j):
    if j == 0:
        return x
    row = lax.broadcasted_iota(jnp.int32, x.shape, 0)
    return jnp.where(row >= j, pltpu.roll(x, j, 0), 0.0)


def _shift_up(x, j):
    if j == 0:
        return x
    n = x.shape[0]
    row = lax.broadcasted_iota(jnp.int32, x.shape, 0)
    return jnp.where(row < n - j, pltpu.roll(x, n - j, 0), 0.0)


def _conv_fwd(pzx, cw, cb, name, rider=None):
    s_dim = pzx.shape[0]
    off = SSD_D_INNER // CONV_TC

    def body(x_ref, w_ref, b_ref, o_ref):
        x = x_ref[...]
        w = w_ref[...]
        y = b_ref[...] + w[3:4, :] * x
        for k in range(SSD_D_CONV - 1):
            y = y + w[k:k + 1, :] * _shift_down(x, SSD_D_CONV - 1 - k)
        o_ref[...] = y * _sigmoid(y)

    outs, rode = _pcall(
        body, grid=(SSD_CONV_DIM // CONV_TC,),
        in_specs=[pl.BlockSpec((s_dim, CONV_TC), lambda j: (0, off + j)), pl.BlockSpec((SSD_D_CONV, CONV_TC), lambda j: (0, j)),
                  pl.BlockSpec((1, CONV_TC), lambda j: (0, j))],
        out_specs=[pl.BlockSpec((s_dim, CONV_TC), lambda j: (0, j))],
        out_shape=[jax.ShapeDtypeStruct((s_dim, SSD_CONV_DIM), F32)],
        args=[pzx, cw, cb], sem=("parallel",), name=name, rider=rider)
    return (outs[0], rode) if rider is not None else outs[0]


def _conv_bwd(pzx, cw, cb, dxs, dbm, dcm, dzx, name):
    s_dim = pzx.shape[0]
    off = SSD_D_INNER // CONV_TC
    n_x, n_b = dxs.shape[1] // CONV_TC, dbm.shape[1] // CONV_TC

    def body(x_ref, w_ref, b_ref, dxs_ref, dbm_ref, dcm_ref, _, dx_ref, dw_ref, db_ref):
        j = pl.program_id(0)
        d = jnp.where(j < n_x, dxs_ref[...], jnp.where(j < n_x + n_b, dbm_ref[...], dcm_ref[...]))
        x = x_ref[...]
        w = w_ref[...]
        xs = [_shift_down(x, SSD_D_CONV - 1 - k) for k in range(SSD_D_CONV)]
        y = b_ref[...]
        for k in range(SSD_D_CONV):
            y = y + w[k:k + 1, :] * xs[k]
        sg = _sigmoid(y)
        dy = d * (sg * (1.0 + y * (1.0 - sg)))
        dx = w[3:4, :] * dy
        for k in range(SSD_D_CONV - 1):
            dx = dx + w[k:k + 1, :] * _shift_up(dy, SSD_D_CONV - 1 - k)
        dx_ref[...] = dx.astype(dx_ref.dtype)
        for k in range(SSD_D_CONV):
            dw_ref[k:k + 1, :] = jnp.sum(dy * xs[k], axis=0, keepdims=True)
        db_ref[...] = jnp.sum(dy, axis=0, keepdims=True)

    part = lambda lo, n: pl.BlockSpec((s_dim, CONV_TC), lambda j: (0, jnp.clip(j - lo, 0, n - 1)))
    return pl.pallas_call(
        body, grid=(SSD_CONV_DIM // CONV_TC,),
        in_specs=[pl.BlockSpec((s_dim, CONV_TC), lambda j: (0, off + j)), pl.BlockSpec((SSD_D_CONV, CONV_TC), lambda j: (0, j)),
                  pl.BlockSpec((1, CONV_TC), lambda j: (0, j)), part(0, n_x), part(n_x, n_b), part(n_x + n_b, n_b), _ANY],
        out_specs=[pl.BlockSpec((s_dim, CONV_TC), lambda j: (0, off + j)), pl.BlockSpec((SSD_D_CONV, CONV_TC), lambda j: (0, j)),
                   pl.BlockSpec((1, CONV_TC), lambda j: (0, j))],
        out_shape=[jax.ShapeDtypeStruct(dzx.shape, dzx.dtype), jax.ShapeDtypeStruct((SSD_D_CONV, SSD_CONV_DIM), F32),
                   jax.ShapeDtypeStruct((1, SSD_CONV_DIM), F32)],
        input_output_aliases={6: 0}, compiler_params=_params("arbitrary"), name=name,
    )(pzx, cw, cb, dxs, dbm, dcm, dzx)


def _ssd_step(xs, bm, cm, dtraw, bias, alog, dskip, st_in, z, gw, dot, cumsum):
    n = xs.shape[0]
    lane = lax.broadcasted_iota(jnp.int32, (1, LANES), 1)
    sub = lax.broadcasted_iota(jnp.int32, (LANES, 1), 0)
    left = (lane < 64).astype(F32)
    right = 1.0 - left
    top = (sub < 64).astype(F32)
    bot = 1.0 - top
    row = lax.broadcasted_iota(jnp.int32, (n, n), 0)
    colm = lax.broadcasted_iota(jnp.int32, (n, n), 1)
    causal = row >= colm

    dt = _softplus(dtraw + bias)
    adt = dt * (-jnp.exp(alog))
    acum = cumsum(adt)
    acum_t = acum.T
    last = jnp.sum(adt, axis=0, keepdims=True)
    scores = dot(cm, bm, "nt")

    def lane_of(v, h):
        return jnp.sum(v * (lane == h).astype(F32), axis=1, keepdims=True)

    ys, sts = [], []
    for pr in range(4):
        heads = (2 * pr, 2 * pr + 1)
        ac = [lane_of(acum, h) for h in heads]
        ar = [jnp.sum(acum_t * (sub == h).astype(F32), axis=0, keepdims=True) for h in heads]
        dth = [lane_of(dt, h) for h in heads]
        la = [lane_of(last, h) for h in heads]
        dk = [lane_of(dskip, h) for h in heads]
        x2 = xs[:, pr * LANES:(pr + 1) * LANES]
        xdt = x2 * (dth[0] * left + dth[1] * right)
        yd = None
        for i, side in enumerate((left, right)):
            decay = jnp.where(causal, jnp.exp(jnp.minimum(ac[i] - ar[i], 0.0)), 0.0)
            t = dot(scores * decay, xdt * side, "nn")
            yd = t if yd is None else yd + t
        st2 = st_in[pr * LANES:(pr + 1) * LANES, :]
        yo = dot(cm, st2, "nt") * (jnp.exp(ac[0]) * left + jnp.exp(ac[1]) * right)
        dte = jnp.exp(la[0] - ac[0]) * left + jnp.exp(la[1] - ac[1]) * right
        cs = dot(xdt * dte, bm, "tn")
        sts.append(st2 * (jnp.exp(la[0]) * top + jnp.exp(la[1]) * bot) + cs)
        ys.append(yd + yo + (dk[0] * left + dk[1] * right) * x2)
    y = jnp.concatenate(ys, axis=1)
    yg = y * (z * _sigmoid(z))
    yn = yg * lax.rsqrt(jnp.mean(yg * yg, axis=-1, keepdims=True) + GATED_NORM_EPS) * gw
    return yn, jnp.concatenate(sts, axis=0)


def _ssd_specs(n_chunks, rev):
    ci = (lambda c: n_chunks - 1 - c) if rev else (lambda c: c)
    n_x = SSD_D_INNER // LANES
    return dict(
        xs=pl.BlockSpec((SSD_CHUNK, SSD_GROUP_W), lambda g, c: (ci(c), g)),
        bm=pl.BlockSpec((SSD_CHUNK, LANES), lambda g, c: (ci(c), n_x + g)),
        cm=pl.BlockSpec((SSD_CHUNK, LANES), lambda g, c: (ci(c), n_x + SSD_N_GROUPS + g)),
        dt=pl.BlockSpec((None, SSD_CHUNK, LANES), lambda g, c: (g, ci(c), 0)),
        vec=pl.BlockSpec((None, 1, LANES), lambda g, c: (g, 0, 0)),
        z=pl.BlockSpec((SSD_CHUNK, SSD_GROUP_W), lambda g, c: (ci(c), g)),
        gw=pl.BlockSpec((1, SSD_GROUP_W), lambda g, c: (0, g)),
        st=pl.BlockSpec((None, None, SSD_GROUP_W, SSD_D_STATE), lambda g, c: (g, ci(c), 0, 0)),
    )


def _ssd_fwd(act, dtg, bias, alog, dskip, pzx, gw, name, rider=None):
    s_dim = act.shape[0]
    n_chunks = s_dim // SSD_CHUNK
    sp = _ssd_specs(n_chunks, False)

    def body(xs, bm, cm, dt, b_ref, a_ref, d_ref, z, gw_ref, yn_ref, st_ref, state):
        @pl.when(pl.program_id(1) == 0)
        def _():
            state[...] = jnp.zeros_like(state)

        st_in = state[...]
        st_ref[...] = st_in
        yn, st_out = _ssd_step(xs[...], bm[...], cm[...], dt[...], b_ref[...], a_ref[...], d_ref[...], st_in, z[...], gw_ref[...],
                               _dot, _cumsum_rows_raw)
        yn_ref[...] = yn.astype(yn_ref.dtype)
        state[...] = st_out

    outs, rode = _pcall(
        body, grid=(SSD_N_GROUPS, n_chunks),
        in_specs=[sp["xs"], sp["bm"], sp["cm"], sp["dt"], sp["vec"], sp["vec"], sp["vec"], sp["z"], sp["gw"]],
        out_specs=[sp["xs"], sp["st"]],
        out_shape=[jax.ShapeDtypeStruct((s_dim, SSD_D_INNER), BF16),
                   jax.ShapeDtypeStruct((SSD_N_GROUPS, n_chunks, SSD_GROUP_W, SSD_D_STATE), F32)],
        scratch_shapes=[pltpu.VMEM((SSD_GROUP_W, SSD_D_STATE), F32)],
        args=[act, act, act, dtg, bias, alog, dskip, pzx, gw], sem=("parallel", "arbitrary"), name=name, rider=rider)
    return (outs, rode) if rider is not None else outs


def _ssd_bwd(act, dtg, bias, alog, dskip, pzx, gw, states, dyn, name, rider=None):
    s_dim = act.shape[0]
    n_chunks = s_dim // SSD_CHUNK
    sp = _ssd_specs(n_chunks, True)
    rc = lambda c: n_chunks - 1 - c

    def body(xs, bm, cm, dt, b_ref, a_ref, d_ref, z, gw_ref, st_ref, dyn_ref,
             dxs_ref, dbm_ref, dcm_ref, ddt_ref, db_ref, da_ref, dd_ref, dz_ref, dgw_ref, dstate):
        first = pl.program_id(1) == 0

        @pl.when(first)
        def _():
            dstate[...] = jnp.zeros_like(dstate)
            db_ref[...] = jnp.zeros_like(db_ref)
            da_ref[...] = jnp.zeros_like(da_ref)
            dd_ref[...] = jnp.zeros_like(dd_ref)
            dgw_ref[...] = jnp.zeros_like(dgw_ref)

        fn = functools.partial(_ssd_step, dot=_gdot, cumsum=_cumsum_rows)
        _, vjp = jax.vjp(fn, xs[...], bm[...], cm[...], dt[...], b_ref[...], a_ref[...], d_ref[...], st_ref[...], z[...], gw_ref[...])
        dxs, dbm, dcm, ddt, db, da, dd, dst, dz, dgw = vjp((dyn_ref[...], dstate[...]))
        dxs_ref[...] = dxs
        dbm_ref[...] = dbm
        dcm_ref[...] = dcm
        ddt_ref[...] = ddt
        dz_ref[...] = dz.astype(dz_ref.dtype)
        db_ref[...] += db
        da_ref[...] += da
        dd_ref[...] += dd
        dgw_ref[...] += dgw
        dstate[...] = dst

    bc = pl.BlockSpec((SSD_CHUNK, LANES), lambda g, c: (rc(c), g))
    outs, rode = _pcall(
        body, grid=(SSD_N_GROUPS, n_chunks),
        in_specs=[sp["xs"], sp["bm"], sp["cm"], sp["dt"], sp["vec"], sp["vec"], sp["vec"], sp["z"], sp["gw"], sp["st"], sp["xs"]],
        out_specs=[sp["xs"], bc, bc, sp["dt"], sp["vec"], sp["vec"], sp["vec"], sp["xs"], sp["gw"]],
        out_shape=[jax.ShapeDtypeStruct((s_dim, SSD_D_INNER), F32),
                   jax.ShapeDtypeStruct((s_dim, SSD_N_GROUPS * SSD_D_STATE), F32),
                   jax.ShapeDtypeStruct((s_dim, SSD_N_GROUPS * SSD_D_STATE), F32),
                   jax.ShapeDtypeStruct((SSD_N_GROUPS, s_dim, LANES), F32),
                   jax.ShapeDtypeStruct((SSD_N_GROUPS, 1, LANES), F32),
                   jax.ShapeDtypeStruct((SSD_N_GROUPS, 1, LANES), F32),
                   jax.ShapeDtypeStruct((SSD_N_GROUPS, 1, LANES), F32),
                   jax.ShapeDtypeStruct((s_dim, SSD_ZX), BF16),
                   jax.ShapeDtypeStruct((1, SSD_D_INNER), F32)],
        scratch_shapes=[pltpu.VMEM((SSD_GROUP_W, SSD_D_STATE), F32)],
        args=[act, act, act, dtg, bias, alog, dskip, pzx, gw, states, dyn], sem=("arbitrary", "arbitrary"), name=name, rider=rider)
    return (outs, rode) if rider is not None else outs


SB_T = 128
SB_GROUP = 8
SB_WIDE = SB_GROUP * SB_T
SB_HB = 4
SB_SCALE = 1.0 / math.sqrt(SB_HEAD_DIM)


def _qknorm_fwd(proj, qw, kw, name, tm=512):
    s_dim = proj.shape[0]

    def body(q_ref, k_ref, v_ref, qw_ref, kw_ref, qo, ko, vo):
        for hh in range(SB_HB):
            qo[:, _head_lanes(hh)] = _rms(q_ref[:, _head_lanes(hh)], qw_ref[...], NORM_EPS).astype(BF16)
            ko[:, _head_lanes(hh)] = _rms(k_ref[:, _head_lanes(hh)], kw_ref[...], NORM_EPS).astype(BF16)
        vo[...] = v_ref[...].astype(BF16)

    groups = SB_N_HEADS // SB_HB
    blk = lambda o: pl.BlockSpec((tm, SB_HB * SB_HEAD_DIM), lambda i, h: (i, o + h))
    vec = pl.BlockSpec((1, SB_HEAD_DIM), lambda i, h: (0, 0))
    return pl.pallas_call(
        body, grid=(s_dim // tm, groups),
        in_specs=[blk(0), blk(groups), blk(2 * groups), vec, vec],
        out_specs=[blk(0)] * 3,
        out_shape=[jax.ShapeDtypeStruct((s_dim, SB_WIDTH), BF16)] * 3,
        compiler_params=_params("parallel", "parallel"), name=name,
    )(proj, proj, proj, qw, kw)


def _qknorm_bwd(proj, qw, kw, dqn, dkn, name, tm=512):
    s_dim = proj.shape[0]

    def body(q_ref, k_ref, dq_ref, dk_ref, qw_ref, kw_ref, dqo, dko, dqw, dkw):
        @pl.when((pl.program_id(0) == 0) & (pl.program_id(1) == 0))
        def _():
            dqw[...] = jnp.zeros_like(dqw)
            dkw[...] = jnp.zeros_like(dkw)

        fn = lambda a, b: _rms(a, b, NORM_EPS)
        for hh in range(SB_HB):
            lanes = _head_lanes(hh)
            for x_ref, w_ref, d_ref, dx_out, dw_out in ((q_ref, qw_ref, dq_ref, dqo, dqw), (k_ref, kw_ref, dk_ref, dko, dkw)):
                _, vjp = jax.vjp(fn, x_ref[:, lanes], w_ref[...])
                dx, dw = vjp(d_ref[:, lanes])
                dx_out[:, lanes] = dx.astype(BF16)
                dw_out[...] += dw

    groups = SB_N_HEADS // SB_HB
    blk = lambda o: pl.BlockSpec((tm, SB_HB * SB_HEAD_DIM), lambda i, h: (i, o + h))
    vec = pl.BlockSpec((1, SB_HEAD_DIM), lambda i, h: (0, 0))
    return pl.pallas_call(
        body, grid=(s_dim // tm, groups),
        in_specs=[blk(0), blk(groups), blk(0), blk(0), vec, vec],
        out_specs=[blk(0), blk(0), vec, vec],
        out_shape=[jax.ShapeDtypeStruct((s_dim, SB_WIDTH), BF16)] * 2 + [jax.ShapeDtypeStruct((1, SB_HEAD_DIM), F32)] * 2,
        compiler_params=_params("arbitrary", "arbitrary"), name=name,
    )(proj, proj, dqn, dkn, qw, kw)


def _sb_logits(q, k, strict):
    z = _dot(q, k, "nt") * SB_SCALE
    lb = jnp.minimum(z, 0.0) - jnp.log(1.0 + jnp.exp(-jnp.abs(z)))
    lm = lb - z
    if strict is not None:
        lm = jnp.where(strict, lm, 0.0)
    return lb, lm


def _sb_strict(qi, grp):
    r = lax.broadcasted_iota(jnp.int32, (SB_T, SB_WIDE), 0) + qi * SB_T
    c = lax.broadcasted_iota(jnp.int32, (SB_T, SB_WIDE), 1) + grp * SB_WIDE
    return c < r


def _head_lanes(hh):
    return slice(hh * SB_HEAD_DIM, (hh + 1) * SB_HEAD_DIM)


def _sb_fwd(qn, kn, vb, proj, name, rider=None):
    s_dim = qn.shape[0]
    nq = s_dim // SB_T
    assert nq % SB_GROUP == 0

    def body(q_ref, k_ref, v_ref, g_ref, og_ref, o_ref, t_ref):
        qi = pl.program_id(1)
        top = qi // SB_GROUP
        after = _tri(SB_T, True, strict=True)
        qs = [q_ref[:, _head_lanes(hh)] for hh in range(SB_HB)]

        def step(grp, masked, carries):
            start = pl.multiple_of(grp * SB_WIDE, SB_WIDE)
            strict = _sb_strict(qi, grp) if masked else None
            out = []
            for hh in range(SB_HB):
                o_acc, cr = carries[hh]
                k = k_ref[pl.ds(start, SB_WIDE), _head_lanes(hh)]
                v = v_ref[pl.ds(start, SB_WIDE), _head_lanes(hh)]
                lb, lm = _sb_logits(qs[hh], k, strict)
                rest = [None] * SB_GROUP
                for t in reversed(range(SB_GROUP)):
                    lm_t = lm[:, t * SB_T:(t + 1) * SB_T]
                    rest[t] = cr + _split_dot(lm_t, after, 2, True)
                    cr = cr + jnp.sum(lm_t, axis=1, keepdims=True)
                a = jnp.exp(lb + jnp.concatenate(rest, axis=1))
                if masked:
                    a = jnp.where(strict, a, 0.0)
                out.append((o_acc + _dot(a, v), cr))
            return tuple(out)

        init = tuple((jnp.zeros((SB_T, SB_HEAD_DIM), F32), jnp.zeros((SB_T, 1), F32)) for _ in range(SB_HB))
        carries = step(top, True, init)
        carries = lax.fori_loop(0, top, lambda i, c: step(top - 1 - i, False, c), carries)
        for hh in range(SB_HB):
            o, tot = carries[hh]
            g = g_ref[:, _head_lanes(hh)]
            o_ref[:, _head_lanes(hh)] = o
            og_ref[:, _head_lanes(hh)] = (o * (g * _sigmoid(g))).astype(og_ref.dtype)
            t_ref[hh] = jnp.broadcast_to(tot, (SB_T, LANES))

    wide = SB_HB * SB_HEAD_DIM
    qb = pl.BlockSpec((SB_T, wide), lambda h, i: (i, h))
    kv = pl.BlockSpec((s_dim, wide), lambda h, i: (0, h))
    outs, rode = _pcall(
        body, grid=(SB_N_HEADS // SB_HB, nq),
        in_specs=[qb, kv, kv, pl.BlockSpec((SB_T, wide), lambda h, i: (i, 3 * SB_N_HEADS // SB_HB + h))],
        out_specs=[qb, qb, pl.BlockSpec((SB_HB, SB_T, LANES), lambda h, i: (h, i, 0))],
        out_shape=[jax.ShapeDtypeStruct((s_dim, SB_WIDTH), BF16), jax.ShapeDtypeStruct((s_dim, SB_WIDTH), F32),
                   jax.ShapeDtypeStruct((SB_N_HEADS, s_dim, LANES), F32)],
        args=[qn, kn, vb, proj], sem=("parallel", "arbitrary"), name=name, rider=rider)
    return (outs, rode) if rider is not None else outs


def _sb_bwd(qn, kn, vb, proj, o, tot, dog, name, rider=None):
    s_dim = qn.shape[0]
    nq = s_dim // SB_T
    assert nq % SB_GROUP == 0

    def body(q_ref, k_ref, v_ref, g_ref, o_ref, t_ref, dog_ref, dq_ref, dk_ref, dv_ref, dvb_ref, dg_ref):
        qi = pl.program_id(1)
        top = qi // SB_GROUP

        @pl.when(qi == 0)
        def _():
            dk_ref[...] = jnp.zeros_like(dk_ref)
            dv_ref[...] = jnp.zeros_like(dv_ref)

        after = _tri(SB_T, True, strict=True)
        before = _tri(SB_T, False, strict=True)
        qs, dos, totals = [], [], []
        for hh in range(SB_HB):
            g = g_ref[:, _head_lanes(hh)]
            sg = _sigmoid(g)
            dog_v = dog_ref[:, _head_lanes(hh)]
            dg_ref[:, _head_lanes(hh)] = (dog_v * o_ref[:, _head_lanes(hh)] * (sg * (1.0 + g * (1.0 - sg)))).astype(dg_ref.dtype)
            dos.append((dog_v * (g * sg)).astype(BF16))
            qs.append(q_ref[:, _head_lanes(hh)])
            totals.append(t_ref[hh][:, 0:1])

        def step(grp, masked, carries):
            start = pl.multiple_of(grp * SB_WIDE, SB_WIDE)
            strict = _sb_strict(qi, grp) if masked else None
            out = []
            for hh in range(SB_HB):
                dq_acc, cp, ce = carries[hh]
                q, do = qs[hh], dos[hh]
                k = k_ref[pl.ds(start, SB_WIDE), _head_lanes(hh)]
                v = v_ref[pl.ds(start, SB_WIDE), _head_lanes(hh)]
                lb, lm = _sb_logits(q, k, strict)
                rest = []
                for t in range(SB_GROUP):
                    lm_t = lm[:, t * SB_T:(t + 1) * SB_T]
                    cp = cp + jnp.sum(lm_t, axis=1, keepdims=True)
                    rest.append((totals[hh] - cp) + _split_dot(lm_t, after, 2, True))
                a = jnp.exp(lb + jnp.concatenate(rest, axis=1))
                if masked:
                    a = jnp.where(strict, a, 0.0)
                e = a * _dot(do, v, "nt")
                excl = []
                for t in range(SB_GROUP):
                    e_t = e[:, t * SB_T:(t + 1) * SB_T]
                    excl.append(ce + _split_dot(e_t, before, 1, True))
                    ce = ce + jnp.sum(e_t, axis=1, keepdims=True)
                eex = jnp.concatenate(excl, axis=1)
                if masked:
                    eex = jnp.where(strict, eex, 0.0)
                sig = jnp.exp(lb)
                dz = (e * (1.0 - sig) - eex * sig) * SB_SCALE
                dv_ref[pl.ds(start, SB_WIDE), _head_lanes(hh)] += _dot(a, do, "tn")
                dk_ref[pl.ds(start, SB_WIDE), _head_lanes(hh)] += _dot(dz, q, "tn")
                out.append((dq_acc + _dot(dz, k), cp, ce))
            return tuple(out)

        zero = jnp.zeros((SB_T, 1), F32)
        init = tuple((jnp.zeros((SB_T, SB_HEAD_DIM), F32), zero, zero) for _ in range(SB_HB))
        carries = lax.fori_loop(0, top, lambda i, c: step(i, False, c), init)
        carries = step(top, True, carries)
        for hh in range(SB_HB):
            dq_ref[:, _head_lanes(hh)] = carries[hh][0]

        @pl.when(qi == nq - 1)
        def _():
            dvb_ref[...] = dv_ref[...].astype(BF16)

    wide = SB_HB * SB_HEAD_DIM
    qb = pl.BlockSpec((SB_T, wide), lambda h, i: (i, h))
    kv = pl.BlockSpec((s_dim, wide), lambda h, i: (0, h))
    outs, rode = _pcall(
        body, grid=(SB_N_HEADS // SB_HB, nq),
        in_specs=[qb, kv, kv, pl.BlockSpec((SB_T, wide), lambda h, i: (i, 3 * SB_N_HEADS // SB_HB + h)), qb,
                  pl.BlockSpec((SB_HB, SB_T, LANES), lambda h, i: (h, i, 0)), qb],
        out_specs=[qb, kv, kv, kv, qb],
        out_shape=[jax.ShapeDtypeStruct((s_dim, SB_WIDTH), F32), jax.ShapeDtypeStruct((s_dim, SB_WIDTH), F32),
                   jax.ShapeDtypeStruct((s_dim, SB_WIDTH), F32), jax.ShapeDtypeStruct((s_dim, SB_WIDTH), BF16),
                   jax.ShapeDtypeStruct((s_dim, SB_WIDTH), BF16)],
        args=[qn, kn, vb, proj, o, tot, dog], sem=("parallel", "arbitrary"), name=name, rider=rider)
    return (outs, rode) if rider is not None else outs


def _adamw_math(w, g, m, v):
    m = ADAM_B1 * m + (1.0 - ADAM_B1) * g
    v = ADAM_B2 * v + (1.0 - ADAM_B2) * (g * g)
    m_hat = m / (1.0 - ADAM_B1 ** ADAM_STEP)
    v_hat = v / (1.0 - ADAM_B2 ** ADAM_STEP)
    delta = -ADAM_LR * (m_hat / (jnp.sqrt(v_hat) + ADAM_EPS) + ADAM_WD * w)
    return delta, m, v


def _row_block(rows, cols, itemsize=4, limit=1 << 20):
    tr = rows
    while tr * cols * itemsize > limit and tr % (2 * BF16_ROWS) == 0:
        tr //= 2
    return tr


def _divisor_block(rows, cols, itemsize=4, limit=2 << 20):
    best = BF16_ROWS
    for t in range(BF16_ROWS, rows + 1, BF16_ROWS):
        if rows % t == 0 and t * cols * itemsize <= limit:
            best = t
    return best


def _adamw(w, g, m, v, name, rider=None):
    n, rows, cols = w.shape
    tr = rows if rows * cols * 4 <= (1 << 20) else _divisor_block(rows, cols, limit=1 << 20)

    def body(w_ref, g_ref, m_ref, v_ref, d_out, m_out, v_out):
        d, m_new, v_new = _adamw_math(w_ref[...], g_ref[...], m_ref[...], v_ref[...])
        d_out[...] = d
        m_out[...] = m_new
        v_out[...] = v_new

    blk = pl.BlockSpec((None, tr, cols), lambda i, j: (i, j, 0))
    outs, rode = _pcall(
        body, grid=(n, rows // tr), in_specs=[blk] * 4, out_specs=[blk] * 3,
        out_shape=[jax.ShapeDtypeStruct(w.shape, F32)] * 3,
        args=[w, g, m, v], sem=("parallel", "parallel"), name=name, rider=rider)
    return (outs, rode) if rider is not None else outs


_FLIPS = ((1, 0), (0, 1), (1, 1))


def _place():
    return lax.axis_index("x"), lax.axis_index("y"), lax.axis_index("c")


def _flip(v, f):
    return 1 - v if f else v


def _half_rows(ref, lead, hc, hr):
    return ref.at[(*lead, pl.ds(pl.multiple_of(hc * hr, BF16_ROWS), hr), slice(None))]


def _half_cols(ref, lead, hc, hw):
    return ref.at[(*lead, pl.ds(pl.multiple_of(hc * hw, LANES), hw))]


def _rows_of_chip(chip, r):
    return pl.ds(pl.multiple_of(chip * r, BF16_ROWS), r)


def _slot_half(gathered, shard_shape, chip, l, hc):
    r, c = shard_shape[1:]
    if len(gathered.shape) == 3:
        return _half_cols(gathered, (l, _rows_of_chip(chip, r)), hc, c // 2)
    return _half_rows(gathered, (chip, l), hc, r // 2)


def _shard_half(shard, stacked, l, hc):
    r, c = shard.shape[1:]
    return _half_cols(shard, (l, slice(None)), hc, c // 2) if stacked else _half_rows(shard, (l,), hc, r // 2)


def _remote(src, dst, send, recv, k, to):
    return pltpu.make_async_remote_copy(src_ref=src, dst_ref=dst, send_sem=send.at[k], recv_sem=recv.at[k], device_id=to,
                                        device_id_type=MESH)


def _comm_call(reads, writes, n_sems, phases, name):
    passed = [k for k, w in enumerate(writes) if not isinstance(w, jax.ShapeDtypeStruct)]
    n_rd = len(reads)

    def body(*refs):
        rd = refs[:n_rd]
        wr = refs[n_rd + len(passed):n_rd + len(passed) + len(writes)]
        send, recv = refs[-2:]
        for phase in phases:
            sends, arrivals = phase(rd, wr, send, recv)
            for cp in sends:
                cp.start()
            for cp in arrivals:
                cp.wait_recv()
            for cp in sends:
                cp.wait_send()

    return pl.pallas_call(
        body, in_specs=[_ANY] * (n_rd + len(passed)), out_specs=[_ANY] * len(writes),
        out_shape=[jax.ShapeDtypeStruct(w.shape, w.dtype) for w in writes],
        input_output_aliases={n_rd + pos: k for pos, k in enumerate(passed)},
        scratch_shapes=[pltpu.SemaphoreType.DMA((n_sems,)), pltpu.SemaphoreType.DMA((n_sems,))], name=name,
    )(*reads, *[writes[k] for k in passed])


def _ag_ici(pieces, names, base=0):
    def phase(shards, gathered, send, recv):
        x, y, c = _place()
        me = 2 * x + y
        sends, arrivals = [], []
        for k, (n, l) in enumerate(pieces):
            a = names.index(n)
            shape = shards[a].shape
            src = _shard_half(shards[a], len(gathered[a].shape) == 3, l, c)
            for j, (fx, fy) in enumerate(_FLIPS):
                tx, ty = _flip(x, fx), _flip(y, fy)
                sends.append(_remote(src, _slot_half(gathered[a], shape, me, l, c), send, recv, base + 3 * k + j, (tx, ty, c)))
                arrivals.append(_remote(src, _slot_half(gathered[a], shape, 2 * tx + ty, l, c), send, recv, base + 3 * k + j, (tx, ty, c)))
        return sends, arrivals

    return phase


def _ag_pass_on(pieces, names, shapes, base=0):
    def phase(_, gathered, send, recv):
        x, y, c = _place()
        sibling = (x, y, 1 - c)
        sends, arrivals = [], []
        for k, (n, l) in enumerate(pieces):
            a = names.index(n)
            for j, (fx, fy) in enumerate(_FLIPS):
                chip = 2 * _flip(x, fx) + _flip(y, fy)
                landed = _slot_half(gathered[a], shapes[a], chip, l, c)
                sends.append(_remote(landed, landed, send, recv, base + 3 * k + j, sibling))
                arrivals.append(_remote(landed, _slot_half(gathered[a], shapes[a], chip, l, 1 - c), send, recv, base + 3 * k + j, sibling))
        return sends, arrivals

    return phase


def _other_half(ref, hc):
    if len(ref.shape) == 3:
        return _half_cols(ref, (slice(None), slice(None)), hc, ref.shape[2] // 2)
    return _half_rows(ref, (slice(None), slice(None)), hc, ref.shape[2] // 2)


def _half_shape(shape):
    return shape[:2] + (shape[2] // 2,) if len(shape) == 3 else shape[:2] + (shape[2] // 2, shape[3])


def _exchange_phase(n_arr):
    def phase(ins, outs, send, recv):
        x, y, c = _place()
        cps = [_remote(_other_half(ins[a], 1 - c), outs[a], send, recv, a, (x, y, 1 - c)) for a in range(n_arr)]
        return cps, cps

    return phase


def _exchange_outs(grads):
    return [jax.ShapeDtypeStruct(_half_shape(g.shape), g.dtype) for g in grads]


def _pair_exchange(grads, name):
    return _comm_call(grads, _exchange_outs(grads), len(grads), [_exchange_phase(len(grads))], name)


def _exchange_rider(grads):
    return _Rider(grads, _exchange_outs(grads), len(grads), _exchange_phase(len(grads)))


def _pair_sum_stacked(g, got, place, name):
    _, rows, hw = got.shape
    tr = _divisor_block(rows, hw)

    def body(place_ref, g_ref, r_ref, o_ref):
        o_ref[...] = (g_ref[...].astype(F32) + r_ref[...].astype(F32)).astype(o_ref.dtype)

    blk = pl.BlockSpec((None, tr, hw), lambda i, pr: (0, i, 0))
    return pl.pallas_call(
        body,
        grid_spec=pltpu.PrefetchScalarGridSpec(
            num_scalar_prefetch=1, grid=(rows // tr,),
            in_specs=[pl.BlockSpec((None, tr, hw), lambda i, pr: (0, i, pr[1])), blk], out_specs=blk),
        out_shape=jax.ShapeDtypeStruct(got.shape, BF16),
        compiler_params=_params("parallel"), name=name,
    )(place, g, got)


def _pair_sum(g, got, place, name):
    if len(g.shape) == 3:
        return _pair_sum_stacked(g, got, place, name)
    _, layers, hr, cols = got.shape
    tr = _row_block(hr, cols, limit=2 << 20)
    per = hr // tr

    def body(place_ref, g_ref, r_ref, o_ref):
        o_ref[...] = (g_ref[...].astype(F32) + r_ref[...].astype(F32)).astype(o_ref.dtype)

    blk = pl.BlockSpec((None, None, tr, cols), lambda k, l, i, pr: (k, l, i, 0))
    return pl.pallas_call(
        body,
        grid_spec=pltpu.PrefetchScalarGridSpec(
            num_scalar_prefetch=1, grid=(4, layers, per),
            in_specs=[pl.BlockSpec((None, None, tr, cols), lambda k, l, i, pr: (k, l, pr[1] * per + i, 0)), blk],
            out_specs=blk),
        out_shape=jax.ShapeDtypeStruct(got.shape, BF16),
        compiler_params=_params("parallel", "parallel", "parallel"), name=name,
    )(place, g, got)


def _scatter_phase(n_arr):
    def phase(ins, outs, send, recv):
        x, y, c = _place()
        cps = []
        for a in range(n_arr):
            for j, (fx, fy) in enumerate(_FLIPS):
                tx, ty = _flip(x, fx), _flip(y, fy)
                if len(ins[a].shape) == 3:
                    src = ins[a].at[:, _rows_of_chip(2 * tx + ty, ins[a].shape[1] // 4), :]
                else:
                    src = ins[a].at[2 * tx + ty]
                cps.append(_remote(src, outs[a].at[j], send, recv, 3 * a + j, (tx, ty, c)))
        return cps, cps

    return phase


def _scatter_outs(pairs):
    return [jax.ShapeDtypeStruct((3, 1, p.shape[1] // 4, p.shape[2]) if len(p.shape) == 3 else (3,) + p.shape[1:], p.dtype) for p in pairs]


def _chip_scatter(pairs, name):
    return _comm_call(pairs, _scatter_outs(pairs), 3 * len(pairs), [_scatter_phase(len(pairs))], name)


def _scatter_rider(pairs):
    return _Rider(pairs, _scatter_outs(pairs), 3 * len(pairs), _scatter_phase(len(pairs)))


def _chip_sum_stacked(p, got, place, layer, layers, o_buf, name, row0=0, rows=None):
    _, r, hw = got.shape[1:]
    rows = rows or r
    tr = _divisor_block(math.gcd(r, row0) if row0 else r, hw)
    per = r // tr
    first = row0 // tr

    def body(place_ref, p_ref, r_ref, *rest):
        o_ref = rest[-1]
        acc = p_ref[...].astype(F32)
        for j in range(3):
            acc = acc + r_ref[j].astype(F32)
        o_ref[...] = acc

    has_buf = o_buf is not None
    return pl.pallas_call(
        body,
        grid_spec=pltpu.PrefetchScalarGridSpec(
            num_scalar_prefetch=1, grid=(per,),
            in_specs=[pl.BlockSpec((None, tr, hw), lambda i, pr: (0, pr[0] * per + i, 0)),
                      pl.BlockSpec((3, None, tr, hw), lambda i, pr: (0, 0, i, 0))] + ([_ANY] if has_buf else []),
            out_specs=pl.BlockSpec((None, tr, hw), lambda i, pr: (layer, first + i, pr[1]))),
        out_shape=jax.ShapeDtypeStruct((layers, rows, 2 * hw), F32),
        input_output_aliases={3: 0} if has_buf else {},
        compiler_params=_params("parallel"), name=name,
    )(*((place, p, got) + ((o_buf,) if has_buf else ())))


def _chip_sum(p, got, place, layer, layers, o_buf, name):
    if len(p.shape) == 3:
        return _chip_sum_stacked(p, got, place, layer, layers, o_buf, name)
    _, _, hr, cols = p.shape
    tr = _row_block(hr, cols, limit=2 << 20)
    per = hr // tr

    def body(place_ref, p_ref, r_ref, *rest):
        o_ref = rest[-1]
        acc = p_ref[...].astype(F32)
        for j in range(3):
            acc = acc + r_ref[j].astype(F32)
        o_ref[...] = acc

    has_buf = o_buf is not None
    return pl.pallas_call(
        body,
        grid_spec=pltpu.PrefetchScalarGridSpec(
            num_scalar_prefetch=1, grid=(per,),
            in_specs=[pl.BlockSpec((None, None, tr, cols), lambda i, pr: (pr[0], 0, i, 0)),
                      pl.BlockSpec((3, None, tr, cols), lambda i, pr: (0, 0, i, 0))] + ([_ANY] if has_buf else []),
            out_specs=pl.BlockSpec((None, tr, cols), lambda i, pr: (layer, pr[1] * per + i, 0))),
        out_shape=jax.ShapeDtypeStruct((layers, 2 * hr, cols), F32),
        input_output_aliases={3: 0} if has_buf else {},
        compiler_params=_params("parallel"), name=name,
    )(*((place, p, got) + ((o_buf,) if has_buf else ())))


def _pair_gather(halves, by_cols, name):
    def phase(_, bufs, send, recv):
        x, y, c = _place()
        sends, arrivals = [], []
        for a, h in enumerate(halves):
            cut = (lambda hc, a=a, h=h: _half_cols(bufs[a], (slice(None), slice(None)), hc, h.shape[2] // 2)) if by_cols[a] else (
                lambda hc, a=a, h=h: _half_rows(bufs[a], (slice(None),), hc, h.shape[1] // 2))
            sends.append(_remote(cut(c), cut(c), send, recv, a, (x, y, 1 - c)))
            arrivals.append(_remote(cut(c), cut(1 - c), send, recv, a, (x, y, 1 - c)))
        return sends, arrivals

    return _comm_call([], halves, len(halves), [phase], name)


def _allreduce_small(v, name):
    rows, cols = v.shape

    def body(v_ref, o_ref, buf, send_sems, recv_sems):
        x, y, c = _place()
        me = 4 * x + 2 * y + c
        buf[0] = v_ref[...]
        cps = []
        for k in range(1, 8):
            kx, ky, kc = (k >> 2) & 1, (k >> 1) & 1, k & 1
            cp = pltpu.make_async_remote_copy(src_ref=v_ref, dst_ref=buf.at[k], send_sem=send_sems.at[k - 1], recv_sem=recv_sems.at[k - 1],
                                              device_id=(_flip(x, kx), _flip(y, ky), _flip(c, kc)), device_id_type=MESH)
            cp.start()
            cps.append(cp)
        for cp in cps:
            cp.wait()
        acc = buf[me]
        for d in range(1, 8):
            acc = acc + buf[jnp.bitwise_xor(d, me)]
        o_ref[...] = acc

    vm = pl.BlockSpec(memory_space=pltpu.VMEM)
    return pl.pallas_call(
        body, in_specs=[vm], out_specs=vm, out_shape=jax.ShapeDtypeStruct((rows, cols), F32),
        scratch_shapes=[pltpu.VMEM((8, rows, cols), F32), pltpu.SemaphoreType.DMA((7,)), pltpu.SemaphoreType.DMA((7,))],
        name=name,
    )(v)


def _pad_lanes(a):
    return jnp.pad(a, ((0, 0), (0, LANES - a.shape[1])))


def _group_lanes(v):
    return jnp.pad(v.reshape(SSD_N_GROUPS, 1, 8), ((0, 0), (0, 0), (0, LANES - 8)))


def kernel(x, p, norm_w, ssd_in_w, ssd_conv_w, ssd_conv_b, ssd_dt_bias, ssd_a_log, ssd_d, ssd_gnorm_w, ssd_out_w, sb_in_w, sb_qn_w, sb_kn_w, sb_out_w, ple_norm_w, ple_gate_w, ple_proj_w, loss_target, m_norm_w, m_ssd_in_w, m_ssd_conv_w, m_ssd_conv_b, m_ssd_dt_bias, m_ssd_a_log, m_ssd_d, m_ssd_gnorm_w, m_ssd_out_w, m_sb_in_w, m_sb_qn_w, m_sb_kn_w, m_sb_out_w, m_ple_norm_w, m_ple_gate_w, m_ple_proj_w, v_norm_w, v_ssd_in_w, v_ssd_conv_w, v_ssd_conv_b, v_ssd_dt_bias, v_ssd_a_log, v_ssd_d, v_ssd_gnorm_w, v_ssd_out_w, v_sb_in_w, v_sb_qn_w, v_sb_kn_w, v_sb_out_w, v_ple_norm_w, v_ple_gate_w, v_ple_proj_w):
    w_in = dict(norm_w=norm_w, ssd_in_w=ssd_in_w, ssd_conv_w=ssd_conv_w, ssd_conv_b=ssd_conv_b, ssd_dt_bias=ssd_dt_bias,
                ssd_a_log=ssd_a_log, ssd_d=ssd_d, ssd_gnorm_w=ssd_gnorm_w, ssd_out_w=ssd_out_w, sb_in_w=sb_in_w, sb_qn_w=sb_qn_w,
                sb_kn_w=sb_kn_w, sb_out_w=sb_out_w, ple_norm_w=ple_norm_w, ple_gate_w=ple_gate_w, ple_proj_w=ple_proj_w)
    m_in = dict(norm_w=m_norm_w, ssd_in_w=m_ssd_in_w, ssd_conv_w=m_ssd_conv_w, ssd_conv_b=m_ssd_conv_b, ssd_dt_bias=m_ssd_dt_bias,
                ssd_a_log=m_ssd_a_log, ssd_d=m_ssd_d, ssd_gnorm_w=m_ssd_gnorm_w, ssd_out_w=m_ssd_out_w, sb_in_w=m_sb_in_w,
                sb_qn_w=m_sb_qn_w, sb_kn_w=m_sb_kn_w, sb_out_w=m_sb_out_w, ple_norm_w=m_ple_norm_w, ple_gate_w=m_ple_gate_w,
                ple_proj_w=m_ple_proj_w)
    v_in = dict(norm_w=v_norm_w, ssd_in_w=v_ssd_in_w, ssd_conv_w=v_ssd_conv_w, ssd_conv_b=v_ssd_conv_b, ssd_dt_bias=v_ssd_dt_bias,
                ssd_a_log=v_ssd_a_log, ssd_d=v_ssd_d, ssd_gnorm_w=v_ssd_gnorm_w, ssd_out_w=v_ssd_out_w, sb_in_w=v_sb_in_w,
                sb_qn_w=v_sb_qn_w, sb_kn_w=v_sb_kn_w, sb_out_w=v_sb_out_w, ple_norm_w=v_ple_norm_w, ple_gate_w=v_ple_gate_w,
                ple_proj_w=v_ple_proj_w)
    ix, iy, ic = lax.axis_index("x"), lax.axis_index("y"), lax.axis_index("c")
    chip = (2 * ix + iy).astype(jnp.int32)
    place = jnp.stack([chip, ic.astype(jnp.int32)])
    zero = jnp.zeros((), jnp.int32)
    big_names = [n for n, _, _ in _BIG]
    layers_of = {n: s[0] for n, s, _ in _BIG}
    cut_of = {n: cut for n, _, cut in _BIG}

    def layer_pieces(i):
        mixer = ("ssd_in_w", "ssd_out_w") if i % 2 == 0 else ("sb_in_w", "sb_out_w")
        return [(mixer[0], i // 2), (mixer[1], i // 2), ("ple_gate_w", i), ("ple_proj_w", i)]

    def names_of(pieces):
        return [n for n in big_names if any(n == q for q, _ in pieces)]

    held = lambda n, a: a.transpose(0, 2, 1) if cut_of[n] == "stack" else a
    mine = {n: held(n, w_in[n]).astype(BF16) for n in big_names}
    shard_shapes = [mine[n].shape for n in big_names]
    room = [jax.ShapeDtypeStruct((s[0], 4 * s[1], s[2]) if cut_of[n] == "stack" else (4,) + s, BF16) for n, s in zip(big_names, shard_shapes)]
    first = layer_pieces(0)[:1]
    gathered = _comm_call([mine[n] for n in big_names], room, 6 * len(first),
                          [_ag_ici(first, big_names), _ag_pass_on(first, big_names, shard_shapes, base=3 * len(first))], "allgather_layer0")
    gw = {}
    for n, g in zip(big_names, gathered):
        if cut_of[n] == "stack":
            layers, r, c = mine[n].shape
            gw[n] = lax.dynamic_update_slice(g.reshape(layers, 4, r, c), mine[n][:, None], (zero, chip, zero, zero)).reshape(g.shape)
        else:
            gw[n] = lax.dynamic_update_slice(g, mine[n][None], (chip, zero, zero, zero))

    lp = [layer_pieces(i) for i in range(DEPTH)]
    carries = {
        "ssd_in_0": (lp[0][1:2], []), "conv_0": (lp[0][2:], lp[0][1:2]), "ssd_0": (lp[1][:1], lp[0][2:]),
        "ssd_out_0": (lp[1][1:2], lp[1][:1]), "sb_in_1": (lp[1][2:], lp[1][1:2]), "sb_1": (lp[2][:2], lp[1][2:]),
        "sb_out_1": (lp[2][2:], lp[2][:2]), "ssd_in_2": (lp[3][1:], lp[2][2:]), "ssd_2": (lp[3][:1], lp[3][1:]),
        "ssd_out_2": ([], lp[3][:1]),
    }

    def gather_rider(call):
        if call not in carries:
            return None, lambda outs: outs
        ici, passing = carries[call]
        names = names_of(ici + passing)
        phases = ([_ag_ici(ici, names)] if ici else []) + (
            [_ag_pass_on(passing, names, [mine[n].shape for n in names], base=3 * len(ici))] if passing else [])

        def issue(rd, wr, send, recv):
            both = [ph(rd, wr, send, recv) for ph in phases]
            return sum((b[0] for b in both), []), sum((b[1] for b in both), [])

        def land(outs):
            outs, bufs = outs
            for n, g in zip(names, bufs):
                gw[n] = g
            return outs

        return _Rider([mine[n] for n in names], [gw[n] for n in names], 3 * (len(ici) + len(passing)), issue), land

    onehot = (jnp.arange(4) == chip).astype(F32) * (ic == 0).astype(F32)
    cw_mine = onehot[:, None, None, None] * ssd_conv_w[None]
    cw_full = _allreduce_small(cw_mine.transpose(1, 2, 0, 3).reshape(-1, LANES), "gather_conv_w").reshape(2, SSD_D_CONV, SSD_CONV_DIM)

    def wmm(a, name, layer, *, dn="nn", res=None, call, rider=None):
        return _matmul(a, gw[name], dn=dn, res=res, b_lay=(cut_of[name], layer), name=call, rider=rider)

    h = x[0]
    target = loss_target[0]
    saved = []
    for i in range(DEPTH):
        j = i // 2
        nw = norm_w[i:i + 1]
        pw = ple_norm_w[i:i + 1]
        s = dict(h=h)
        u = _rms_fwd(h, nw, f"rms_{i}")
        s["u"] = u
        if i % 2 == 0:
            w_dt = jnp.pad(gw["ssd_in_w"][j, SSD_ZX:], ((0, LANES - SSD_N_HEADS), (0, 0)))
            rider, land = gather_rider(f"ssd_in_{i}")
            pzx = land(_matmul(u, gw["ssd_in_w"], dn="nt", b_lay=("stack", j, SSD_ZX), name=f"ssd_in_{i}", rider=rider))
            pdt = _matmul(u, w_dt, dn="nt", name=f"ssd_indt_{i}")
            rider, land = gather_rider(f"conv_{i}")
            act = land(_conv_fwd(pzx, cw_full[j], ssd_conv_b[j:j + 1], f"conv_{i}", rider=rider))
            dtg = jnp.pad(pdt[:, :SSD_N_HEADS].reshape(-1, SSD_N_GROUPS, 8).transpose(1, 0, 2), ((0, 0), (0, 0), (0, LANES - 8)))
            vecs = (_group_lanes(ssd_dt_bias[j]), _group_lanes(ssd_a_log[j]), _group_lanes(ssd_d[j]))
            rider, land = gather_rider(f"ssd_{i}")
            yn, states = land(_ssd_fwd(act, dtg, *vecs, pzx, ssd_gnorm_w[j:j + 1], f"ssd_{i}", rider=rider))
            s.update(w_dt=w_dt, pzx=pzx, act=act, dtg=dtg, vecs=vecs, yn=yn, states=states)
            rider, land = gather_rider(f"ssd_out_{i}")
            h1 = land(wmm(yn, "ssd_out_w", j, res=h, call=f"ssd_out_{i}", rider=rider))
        else:
            rider, land = gather_rider(f"sb_in_{i}")
            proj = land(wmm(u, "sb_in_w", j, call=f"sb_in_{i}", rider=rider))
            qn, kn, vb = _qknorm_fwd(proj, sb_qn_w[j:j + 1], sb_kn_w[j:j + 1], f"qknorm_{i}")
            rider, land = gather_rider(f"sb_{i}")
            og, o, tot = land(_sb_fwd(qn, kn, vb, proj, f"sb_{i}", rider=rider))
            s.update(proj=proj, qn=qn, kn=kn, vb=vb, og=og, o=o, tot=tot)
            rider, land = gather_rider(f"sb_out_{i}")
            h1 = land(wmm(og, "sb_out_w", j, res=h, call=f"sb_out_{i}", rider=rider))
        n2 = _rms_fwd(h1, pw, f"ple_rms_{i}")
        gl = wmm(n2, "ple_gate_w", i, call=f"ple_gate_{i}")
        pp = wmm(p[i, 0], "ple_proj_w", i, call=f"ple_proj_{i}")
        h = _ple_fwd(h1, pp, gl, f"ple_{i}")
        s.update(h1=h1, n2=n2, gl=gl, pp=pp)
        saved.append(s)

    dh, loss_lanes = _loss_bwd(h, target, "loss")

    wg = {}
    gsmall = {n: [None] * s[0] for n, s in _SMALL}
    g_conv_w = [None, None]
    scat = {}
    pending = late = None

    def wgrad(a, b, name, layer, call, rider=None):
        out = _matmul(a, b, dn="tn", out_dtype=BF16, o_lay=(cut_of[name], 0, 1), name=call, rider=rider)
        wg[(name, layer)], rode = out if rider is not None else (out, None)
        return rode

    def pair_sums(pieces, got, tag):
        return pieces, [_pair_sum(wg[q], r, place, f"rs_pair_sum_{tag}_{k}") for k, (q, r) in enumerate(zip(pieces, got))]

    def sibling_rider(pieces):
        return _exchange_rider([wg[q] for q in pieces])

    def riding_with(own):
        return (pending[0] + own[0], pending[1] + own[1]) if pending else own

    def arrived(sent, got):
        for q, pair, g in zip(sent[0], sent[1], got):
            scat[q] = (pair, g)

    for i in reversed(range(DEPTH)):
        j = i // 2
        s = saved[i]
        nw = norm_w[i:i + 1]
        pw = ple_norm_w[i:i + 1]
        dpp, dgl = _ple_bwd(dh, s["pp"], s["gl"], f"ple_bwd_{i}")
        wgrad(p[i, 0], dpp, "ple_proj_w", i, f"d_ple_proj_{i}")
        if late is None:
            wgrad(s["n2"], dgl, "ple_gate_w", i, f"d_ple_gate_{i}")
        else:
            pending = pair_sums(late, wgrad(s["n2"], dgl, "ple_gate_w", i, f"d_ple_gate_{i}", rider=sibling_rider(late)), f"{i + 1}_in")
        dn2 = wmm(dgl, "ple_gate_w", i, dn="nt", call=f"ple_gate_bwd_{i}")
        dh1, dpw = _rms_bwd(s["h1"], pw, dn2, dh, f"ple_rms_bwd_{i}")
        gsmall["ple_norm_w"][i] = dpw
        if i % 2 == 0:
            wgrad(s["yn"], dh1, "ssd_out_w", j, f"d_ssd_out_{i}")
            early = layer_pieces(i)[1:]
            dyn, got = wmm(dh1, "ssd_out_w", j, dn="nt", call=f"ssd_out_bwd_{i}", rider=sibling_rider(early))
            riding = riding_with(pair_sums(early, got, f"{i}_out"))
            outs, got = _ssd_bwd(s["act"], s["dtg"], *s["vecs"], s["pzx"], ssd_gnorm_w[j:j + 1], s["states"], dyn, f"ssd_bwd_{i}",
                                 rider=_scatter_rider(riding[1]))
            arrived(riding, got)
            dxs, dbm, dcm, ddtg, dbias, dalog, ddsk, dz, dgw = outs
            dzx, dcw, dcb = _conv_bwd(s["pzx"], cw_full[j], ssd_conv_b[j:j + 1], dxs, dbm, dcm, dz, f"conv_bwd_{i}")
            ddt = _pad_lanes(ddtg[:, :, :8].transpose(1, 0, 2).reshape(-1, SSD_N_HEADS)).astype(BF16)
            dwt = _matmul(dzx, s["u"], dn="tn", out_dtype=BF16, out_rows=SSD_IN_DIM, name=f"d_ssd_in_{i}")
            dwt_dt = _matmul(ddt, s["u"], dn="tn", out_dtype=BF16, name=f"d_ssd_indt_{i}")
            wg[("ssd_in_w", j)] = lax.dynamic_update_slice(dwt, dwt_dt[:SSD_N_HEADS], (SSD_ZX, 0))[None]
            if i == 0:
                by_shard = wg[("ssd_in_w", 0)].reshape(4, -1, D_MODEL)
                parts = [("ssd_in_w", 0, 0), ("ssd_in_w", 0, 1)]
                wg[parts[0]] = by_shard[:, :LAST_SPLIT].reshape(1, -1, D_MODEL)
                wg[parts[1]] = by_shard[:, LAST_SPLIT:].reshape(1, -1, D_MODEL)
                last = pair_sums(parts, _pair_exchange([wg[q] for q in parts], "rs_pair_exchange_last"), "0_in")
                du, got = _matmul(dzx, gw["ssd_in_w"], b_lay=("stack", j, SSD_ZX), name=f"ssd_in_bwd_{i}",
                                  rider=_scatter_rider(last[1][1:]))
                arrived((parts[1:], last[1][1:]), got)
            else:
                du = _matmul(dzx, gw["ssd_in_w"], b_lay=("stack", j, SSD_ZX), name=f"ssd_in_bwd_{i}")
            du = _matmul(ddt, s["w_dt"], res=du, name=f"ssd_indt_bwd_{i}")
            g_conv_w[j] = dcw
            gsmall["ssd_conv_b"][j] = dcb
            gsmall["ssd_dt_bias"][j] = dbias[:, 0, :8].reshape(1, SSD_N_HEADS)
            gsmall["ssd_a_log"][j] = dalog[:, 0, :8].reshape(1, SSD_N_HEADS)
            gsmall["ssd_d"][j] = ddsk[:, 0, :8].reshape(1, SSD_N_HEADS)
            gsmall["ssd_gnorm_w"][j] = dgw
        else:
            wgrad(s["og"], dh1, "sb_out_w", j, f"d_sb_out_{i}")
            early = layer_pieces(i)[1:]
            dog, got = wmm(dh1, "sb_out_w", j, dn="nt", call=f"sb_out_bwd_{i}", rider=sibling_rider(early))
            riding = riding_with(pair_sums(early, got, f"{i}_out"))
            outs, got = _sb_bwd(s["qn"], s["kn"], s["vb"], s["proj"], s["o"], s["tot"], dog, f"sb_bwd_{i}", rider=_scatter_rider(riding[1]))
            arrived(riding, got)
            dqn, dkn, _, dvb, dg = outs
            dq, dk, dqw, dkw = _qknorm_bwd(s["proj"], sb_qn_w[j:j + 1], sb_kn_w[j:j + 1], dqn, dkn, f"qknorm_bwd_{i}")
            dproj = jnp.concatenate([dq, dk, dvb, dg], axis=1)
            du = wmm(dproj, "sb_in_w", j, dn="nt", call=f"sb_in_bwd_{i}")
            wgrad(s["u"], dproj, "sb_in_w", j, f"d_sb_in_{i}")
            gsmall["sb_qn_w"][j] = dqw
            gsmall["sb_kn_w"][j] = dkw
        dh, dnw = _rms_bwd(s["h"], nw, du, dh1, f"rms_bwd_{i}")
        gsmall["norm_w"][i] = dnw
        late = layer_pieces(i)[:1]
    grad_x = dh[None]

    def reduced(names, call):
        halves = []
        for n in names:
            buf = None
            for l in range(layers_of[n]):
                if (n, l, 0) in scat:
                    r = shard_shapes[big_names.index(n)][1]
                    for part, row0 in ((0, 0), (1, LAST_SPLIT)):
                        buf = _chip_sum_stacked(*scat[(n, l, part)], place, l, layers_of[n], buf, f"rs_chip_sum_{n}_{l}_{part}", row0, r)
                else:
                    buf = _chip_sum(*scat[(n, l)], place, l, layers_of[n], buf, f"rs_chip_sum_{n}_{l}")
            halves.append(buf)
        return dict(zip(names, _pair_gather(halves, [cut_of[n] == "stack" for n in names], call)))

    def updated(n, rider=None):
        return _adamw(held(n, w_in[n]), g_big[n], held(n, m_in[n]), held(n, v_in[n]), f"adamw_{n}", rider=rider)

    done_early = ["sb_in_w", "sb_out_w"]
    g_big = reduced(done_early, "rs_pair_gather_sb")
    step = {}
    step["sb_in_w"], got = updated("sb_in_w", rider=_scatter_rider(last[1][:1]))
    arrived((last[0][:1], last[1][:1]), got)
    g_big.update(reduced([n for n in big_names if n not in done_early], "rs_pair_gather"))

    small_parts = [jnp.concatenate(gsmall[n], axis=0).reshape(-1) for n, _ in _SMALL]
    small_parts.append(jnp.stack(g_conv_w).reshape(-1))
    small_parts.append(loss_lanes.reshape(-1))
    small_sum = _allreduce_small(jnp.concatenate(small_parts).reshape(-1, LANES), "allreduce_small").reshape(-1)
    g_small, off = {}, 0
    for n, shape in _SMALL:
        size = math.prod(shape)
        g_small[n] = small_sum[off:off + size].reshape(shape)
        off += size
    cw_size = 2 * SSD_D_CONV * SSD_CONV_DIM
    g_cw_full = small_sum[off:off + cw_size].reshape(2, SSD_D_CONV, 4, SSD_CONV_DIM // 4)
    g_small["ssd_conv_w"] = jnp.sum(g_cw_full * (jnp.arange(4) == chip).astype(F32)[None, None, :, None], axis=2)
    loss = 0.5 * jnp.sum(small_sum[off + cw_size:]) / D_MODEL

    grads, delta, new_m, new_v = {}, {}, {}, {}
    for n in big_names:
        grads[n], delta[n], new_m[n], new_v[n] = (held(n, a) for a in (g_big[n], *(step[n] if n in step else updated(n))))
    small_names = [n for n, _ in _SMALL] + ["ssd_conv_w"]
    pack = lambda d: jnp.concatenate([d[n].reshape(-1) for n in small_names]).reshape(1, -1, LANES)
    ds, ms, vs = _adamw(pack(w_in), pack(g_small), pack(m_in), pack(v_in), "adamw_small")
    off = 0
    for n in small_names:
        shape = w_in[n].shape
        size = math.prod(shape)
        grads[n] = g_small[n]
        delta[n] = ds.reshape(-1)[off:off + size].reshape(shape)
        new_m[n] = ms.reshape(-1)[off:off + size].reshape(shape)
        new_v[n] = vs.reshape(-1)[off:off + size].reshape(shape)
        off += size

    order = ["norm_w", "ssd_in_w", "ssd_conv_w", "ssd_conv_b", "ssd_dt_bias", "ssd_a_log", "ssd_d", "ssd_gnorm_w", "ssd_out_w",
             "sb_in_w", "sb_qn_w", "sb_kn_w", "sb_out_w", "ple_norm_w", "ple_gate_w", "ple_proj_w"]
    return (loss, grad_x, *[grads[n] for n in order], *[delta[n] for n in order], *[new_m[n] for n in order],
            *[new_v[n] for n in order])
```

```python
import functools
import math

import jax
import jax.numpy as jnp
from jax import lax
from jax.experimental import pallas as pl
from jax.experimental.pallas import tpu as pltpu

F32 = jnp.float32
BF16 = jnp.bfloat16
MESH = pl.DeviceIdType.MESH

D_MODEL = 2048
DEPTH = 4
SSD_D_INNER = 4096
SSD_N_GROUPS = 8
SSD_GROUP_W = SSD_D_INNER // SSD_N_GROUPS
SSD_D_STATE = 128
SSD_CHUNK = 128
SSD_CONV_DIM = 6144
SSD_D_CONV = 4
SSD_N_HEADS = 64
SB_HEAD_DIM = 128
SB_N_HEADS = 16
SB_WIDTH = 2048
NORM_EPS = 1e-6
GATED_NORM_EPS = 1e-5
ADAM_LR = 0.001
ADAM_B1 = 0.9
ADAM_B2 = 0.999
ADAM_EPS = 1e-08
ADAM_WD = 0.01
ADAM_STEP = 10

SSD_ZX = SSD_D_INNER + SSD_CONV_DIM
SSD_IN_DIM = SSD_ZX + SSD_N_HEADS
LAST_SPLIT = 1104
LANES = 128
BF16_ROWS = 16

_BIG = (
    ("ssd_in_w", (2, 2576, 2048), "stack"),
    ("ssd_out_w", (2, 1024, 2048), "row"),
    ("sb_in_w", (2, 2048, 2048), "col"),
    ("sb_out_w", (2, 512, 2048), "row"),
    ("ple_gate_w", (4, 512, 2048), "row"),
    ("ple_proj_w", (4, 256, 512), "col"),
)
_SMALL = (
    ("norm_w", (4, 2048)),
    ("ssd_conv_b", (2, 6144)),
    ("ssd_dt_bias", (2, 64)),
    ("ssd_a_log", (2, 64)),
    ("ssd_d", (2, 64)),
    ("ssd_gnorm_w", (2, 4096)),
    ("sb_qn_w", (2, 128)),
    ("sb_kn_w", (2, 128)),
    ("ple_norm_w", (4, 2048)),
)

_DN = {
    "nn": (((1,), (0,)), ((), ())),
    "nt": (((1,), (1,)), ((), ())),
    "tn": (((0,), (0,)), ((), ())),
}


def _dot(a, b, dn="nn"):
    return lax.dot_general(a.astype(BF16), b.astype(BF16), _DN[dn], preferred_element_type=F32)


@functools.partial(jax.custom_vjp, nondiff_argnums=(2,))
def _gdot(a, b, dn):
    return _dot(a, b, dn)


def _gdot_fwd(a, b, dn):
    return _dot(a, b, dn), (a, b)


def _gdot_bwd(dn, res, g):
    a, b = res
    if dn == "nn":
        return _dot(g, b, "nt"), _dot(a, g, "tn")
    if dn == "nt":
        return _dot(g, b, "nn"), _dot(g, a, "tn")
    return _dot(b, g, "nt"), _dot(a, g, "nn")


_gdot.defvjp(_gdot_fwd, _gdot_bwd)


def _split_dot(x, t, parts, x_left):
    acc = None
    r = x
    for i in range(parts):
        p = r.astype(BF16)
        d = lax.dot_general(p, t, _DN["nn"], preferred_element_type=F32) if x_left else lax.dot_general(
            t, p, _DN["nn"], preferred_element_type=F32)
        acc = d if acc is None else acc + d
        if i + 1 < parts:
            r = r - p.astype(F32)
    return acc


def _tri(n, lower, strict=False):
    r = lax.broadcasted_iota(jnp.int32, (n, n), 0)
    c = lax.broadcasted_iota(jnp.int32, (n, n), 1)
    keep = (r > c if strict else r >= c) if lower else (r < c if strict else r <= c)
    return jnp.where(keep, 1.0, 0.0).astype(BF16)


def _cumsum_rows_raw(x):
    return _split_dot(x, _tri(x.shape[0], True), 3, False)


@jax.custom_vjp
def _cumsum_rows(x):
    return _cumsum_rows_raw(x)


def _cumsum_rows_fwd(x):
    return _cumsum_rows_raw(x), None


def _cumsum_rows_bwd(_, g):
    return (_split_dot(g, _tri(g.shape[0], False), 3, False),)


_cumsum_rows.defvjp(_cumsum_rows_fwd, _cumsum_rows_bwd)


def _sigmoid(x):
    return 1.0 / (1.0 + jnp.exp(-x))


def _softplus(x):
    return jnp.maximum(x, 0.0) + jnp.log(1.0 + jnp.exp(-jnp.abs(x)))


def _rms(x, w, eps):
    return x * lax.rsqrt(jnp.mean(x * x, axis=-1, keepdims=True) + eps) * w


_ANY = pl.BlockSpec(memory_space=pl.ANY)


def _params(*sem):
    return pltpu.CompilerParams(dimension_semantics=sem)


class _Rider:
    def __init__(self, reads, writes, n_sems, issue):
        self.reads, self.writes, self.n_sems, self.issue = list(reads), list(writes), n_sems, issue


def _pcall(body, *, grid, in_specs, out_specs, out_shape, args, sem, name, scratch_shapes=(), aliases=None, rider=None):
    aliases = dict(aliases or {})
    if rider is None:
        outs = pl.pallas_call(body, grid=grid, in_specs=in_specs, out_specs=out_specs, out_shape=out_shape,
                              scratch_shapes=list(scratch_shapes), input_output_aliases=aliases,
                              compiler_params=_params(*sem), name=name)(*args)
        return list(outs), []
    n_in, n_out, n_scr, n_rd, n_wr = len(args), len(out_shape), len(scratch_shapes), len(rider.reads), len(rider.writes)
    passed = [k for k, w in enumerate(rider.writes) if not isinstance(w, jax.ShapeDtypeStruct)]
    for pos, k in enumerate(passed):
        aliases[n_in + n_rd + pos] = n_out + k

    def wrapped(*refs):
        ins = refs[:n_in]
        reads = refs[n_in:n_in + n_rd]
        base = n_in + n_rd + len(passed)
        outs = refs[base:base + n_out]
        writes = refs[base + n_out:base + n_out + n_wr]
        scr = refs[base + n_out + n_wr:base + n_out + n_wr + n_scr]
        send, recv = refs[-2:]
        first = last = None
        for d, n in enumerate(grid):
            i = pl.program_id(d)
            first = (i == 0) if first is None else first & (i == 0)
            last = (i == n - 1) if last is None else last & (i == n - 1)

        @pl.when(first)
        def _():
            for cp in rider.issue(reads, writes, send, recv)[0]:
                cp.start()

        body(*ins, *outs, *scr)

        @pl.when(last)
        def _():
            sends, arrivals = rider.issue(reads, writes, send, recv)
            for cp in arrivals:
                cp.wait_recv()
            for cp in sends:
                cp.wait_send()

    outs = pl.pallas_call(
        wrapped, grid=grid,
        in_specs=list(in_specs) + [_ANY] * (n_rd + len(passed)),
        out_specs=list(out_specs) + [_ANY] * n_wr,
        out_shape=list(out_shape) + [jax.ShapeDtypeStruct(w.shape, w.dtype) for w in rider.writes],
        scratch_shapes=list(scratch_shapes) + [pltpu.SemaphoreType.DMA((rider.n_sems,)), pltpu.SemaphoreType.DMA((rider.n_sems,))],
        input_output_aliases=aliases, compiler_params=_params(*(["arbitrary"] * len(grid))), name=name,
    )(*args, *rider.reads, *[rider.writes[k] for k in passed])
    return list(outs[:n_out]), list(outs[n_out:])


MM_TK = 2048


def _pick(dim, pref, unit=None):
    t = pref
    while t >= LANES:
        if dim % t == 0 and (unit is None or unit % t == 0):
            return t
        t //= 2
    return dim


def _matmul(a, b, *, dn="nn", res=None, out_dtype=F32, name, b_lay=None, o_lay=None, o_buf=None, out_rows=None, rider=None):
    if dn == "tn":
        k_dim, m_dim = a.shape
    else:
        m_dim, k_dim = a.shape
    unit_m = unit_n = unit_k = None
    if b_lay is None:
        n_dim = b.shape[0] if dn == "nt" else b.shape[1]
    elif b_lay[0] == "stack":
        cut, layer, rows = b_lay
        cols = b.shape[2]
        n_dim = cols if dn == "nn" else rows
        assert k_dim == (rows if dn == "nn" else cols) and dn != "tn"
    else:
        cut, layer = b_lay
        r, c = b.shape[2:]
        rows, cols = (4 * r, c) if cut == "row" else (r, 4 * c)
        n_dim = cols if dn == "nn" else rows
        assert k_dim == (rows if dn == "nn" else cols) and dn != "tn"
        if (cut == "row") == (dn == "nn"):
            unit_k = r if cut == "row" else c
        else:
            unit_n = r if cut == "row" else c
    if o_lay is not None:
        o_cut, o_layer, o_layers = o_lay
        if o_cut == "row":
            unit_m = m_dim // 4
        else:
            unit_n = n_dim // 4
    tm, tn, tk = _pick(m_dim, 1024, unit_m), _pick(n_dim, 1024, unit_n), _pick(k_dim, MM_TK, unit_k)
    nk = k_dim // tk
    a_spec = pl.BlockSpec((tk, tm), lambda i, j, k: (k, i)) if dn == "tn" else pl.BlockSpec((tm, tk), lambda i, j, k: (i, k))
    if b_lay is None:
        b_spec = pl.BlockSpec((tn, tk), lambda i, j, k: (j, k)) if dn == "nt" else pl.BlockSpec((tk, tn), lambda i, j, k: (k, j))
    elif cut == "stack":
        b_spec = (pl.BlockSpec((None, tk, tn), lambda i, j, k: (layer, k, j)) if dn == "nn" else
                  pl.BlockSpec((None, tn, tk), lambda i, j, k: (layer, j, k)))
    elif dn == "nn" and cut == "row":
        per = r // tk
        b_spec = pl.BlockSpec((None, None, tk, tn), lambda i, j, k: (k // per, layer, k % per, j))
    elif dn == "nn":
        per = c // tn
        b_spec = pl.BlockSpec((None, None, tk, tn), lambda i, j, k: (j // per, layer, k, j % per))
    elif cut == "row":
        per = r // tn
        b_spec = pl.BlockSpec((None, None, tn, tk), lambda i, j, k: (j // per, layer, j % per, k))
    else:
        per = c // tk
        b_spec = pl.BlockSpec((None, None, tn, tk), lambda i, j, k: (k // per, layer, j, k % per))
    r_spec = pl.BlockSpec((tm, tn), lambda i, j, k: (i, j))
    if o_lay is None:
        o_spec = r_spec
        out_shape = jax.ShapeDtypeStruct((out_rows or m_dim, n_dim), out_dtype)
    elif o_cut == "row":
        per_o = unit_m // tm
        o_spec = pl.BlockSpec((None, None, tm, tn), lambda i, j, k: (i // per_o, o_layer, i % per_o, j))
        out_shape = jax.ShapeDtypeStruct((4, o_layers, unit_m, n_dim), out_dtype)
    else:
        per_o = unit_n // tn
        o_spec = pl.BlockSpec((None, None, tm, tn), lambda i, j, k: (j // per_o, o_layer, i, j % per_o))
        out_shape = jax.ShapeDtypeStruct((4, o_layers, m_dim, unit_n), out_dtype)
    has_res = res is not None
    has_buf = o_buf is not None

    def body(*refs):
        a_ref, b_ref = refs[:2]
        r_ref = refs[2] if has_res else None
        o_ref = refs[-1] if nk == 1 else refs[-2]

        def finish(v):
            if has_res:
                v = v + r_ref[...]
            o_ref[...] = v.astype(o_ref.dtype)

        if nk == 1:
            finish(_dot(a_ref[...], b_ref[...], dn))
            return
        acc_ref = refs[-1]
        k = pl.program_id(2)

        @pl.when(k == 0)
        def _():
            acc_ref[...] = jnp.zeros_like(acc_ref)

        acc_ref[...] += _dot(a_ref[...], b_ref[...], dn)

        @pl.when(k == nk - 1)
        def _():
            finish(acc_ref[...])

    args = [a, b] + ([res] if has_res else []) + ([o_buf] if has_buf else [])
    outs, rode = _pcall(
        body, grid=(m_dim // tm, n_dim // tn, nk),
        in_specs=[a_spec, b_spec] + ([r_spec] if has_res else []) + ([_ANY] if has_buf else []),
        out_specs=[o_spec], out_shape=[out_shape],
        scratch_shapes=[] if nk == 1 else [pltpu.VMEM((tm, tn), F32)],
        aliases={len(args) - 1: 0} if has_buf else {},
        args=args, sem=("parallel", "parallel", "arbitrary"), name=name, rider=rider)
    return (outs[0], rode) if rider is not None else outs[0]


def _rowcall(fn, rows, consts, outs, accs, *, name, tm=512):
    args = list(rows) + list(consts)
    in_specs = [pl.BlockSpec((tm, r.shape[1]), lambda i: (i, 0)) for r in rows]
    in_specs += [pl.BlockSpec(c.shape, lambda i: (0, 0)) for c in consts]
    s_dim = args[0].shape[0]
    n_in, n_out = len(args), len(outs)
    out_shape = [jax.ShapeDtypeStruct((s_dim, w), dt) for w, dt in outs] + [jax.ShapeDtypeStruct(s, F32) for s in accs]
    out_specs = [pl.BlockSpec((tm, w), lambda i: (i, 0)) for w, _ in outs] + [pl.BlockSpec(s, lambda i: (0, 0)) for s in accs]

    def body(*refs):
        vals = fn(*[r[...] for r in refs[:n_in]])
        o_refs = refs[n_in:n_in + n_out]
        a_refs = refs[n_in + n_out:]
        for o, v in zip(o_refs, vals[:n_out]):
            o[...] = v.astype(o.dtype)
        if a_refs:
            @pl.when(pl.program_id(0) == 0)
            def _():
                for a_ref in a_refs:
                    a_ref[...] = jnp.zeros_like(a_ref)

            for a_ref, v in zip(a_refs, vals[n_out:]):
                a_ref[...] += v

    return pl.pallas_call(
        body, grid=(s_dim // tm,), in_specs=in_specs, out_specs=out_specs, out_shape=out_shape,
        compiler_params=_params("arbitrary"), name=name,
    )(*args)


def _rms_fwd(h, w, name):
    return _rowcall(lambda x, w_: (_rms(x, w_, NORM_EPS),), [h], [w], [(h.shape[1], BF16)], [], name=name)[0]


def _rms_bwd(h, w, dy, dres, name):
    def fn(x, dy_, dres_, w_):
        _, vjp = jax.vjp(lambda a, b: _rms(a, b, NORM_EPS), x, w_)
        dx, dw = vjp(dy_)
        return dx + dres_, dw

    return _rowcall(fn, [h, dy, dres], [w], [(h.shape[1], F32)], [w.shape], name=name)


def _ple_fwd(h1, pp, gl, name):
    return _rowcall(lambda a, b, c: (a + b * _sigmoid(c),), [h1, pp, gl], [], [(h1.shape[1], F32)], [], name=name)[0]


def _ple_bwd(dh2, pp, gl, name):
    def fn(d, b, c):
        gate = _sigmoid(c)
        return d * gate, d * b * gate * (1.0 - gate)

    return _rowcall(fn, [dh2, pp, gl], [], [(dh2.shape[1], BF16), (dh2.shape[1], BF16)], [], name=name)


def _loss_bwd(y, target, name):
    width = y.shape[1]

    def fn(a, t):
        d = a - t
        col = jnp.sum(d * d, axis=0, keepdims=True)
        part = col[:, 0:LANES]
        for j in range(1, width // LANES):
            part = part + col[:, j * LANES:(j + 1) * LANES]
        return d * (1.0 / width), part

    return _rowcall(fn, [y, target], [], [(width, F32)], [(1, LANES)], name=name)


CONV_TC = 256


def _shift_down(x, j):
    if j == 0:
        return x
    row = lax.broadcasted_iota(jnp.int32, x.shape, 0)
    return jnp.where(row >= j, pltpu.roll(x, j, 0), 0.0)


def _shift_up(x, j):
    if j == 0:
        return x
    n = x.shape[0]
    row = lax.broadcasted_iota(jnp.int32, x.shape, 0)
    return jnp.where(row < n - j, pltpu.roll(x, n - j, 0), 0.0)


def _conv_fwd(pzx, cw, cb, name, rider=None):
    s_dim = pzx.shape[0]
    off = SSD_D_INNER // CONV_TC

    def body(x_ref, w_ref, b_ref, o_ref):
        x = x_ref[...]
        w = w_ref[...]
        y = b_ref[...] + w[3:4, :] * x
        for k in range(SSD_D_CONV - 1):
            y = y + w[k:k + 1, :] * _shift_down(x, SSD_D_CONV - 1 - k)
        o_ref[...] = y * _sigmoid(y)

    outs, rode = _pcall(
        body, grid=(SSD_CONV_DIM // CONV_TC,),
        in_specs=[pl.BlockSpec((s_dim, CONV_TC), lambda j: (0, off + j)), pl.BlockSpec((SSD_D_CONV, CONV_TC), lambda j: (0, j)),
                  pl.BlockSpec((1, CONV_TC), lambda j: (0, j))],
        out_specs=[pl.BlockSpec((s_dim, CONV_TC), lambda j: (0, j))],
        out_shape=[jax.ShapeDtypeStruct((s_dim, SSD_CONV_DIM), F32)],
        args=[pzx, cw, cb], sem=("parallel",), name=name, rider=rider)
    return (outs[0], rode) if rider is not None else outs[0]


def _conv_bwd(pzx, cw, cb, dxs, dbm, dcm, dzx, name):
    s_dim = pzx.shape[0]
    off = SSD_D_INNER // CONV_TC
    n_x, n_b = dxs.shape[1] // CONV_TC, dbm.shape[1] // CONV_TC

    def body(x_ref, w_ref, b_ref, dxs_ref, dbm_ref, dcm_ref, _, dx_ref, dw_ref, db_ref):
        j = pl.program_id(0)
        d = jnp.where(j < n_x, dxs_ref[...], jnp.where(j < n_x + n_b, dbm_ref[...], dcm_ref[...]))
        x = x_ref[...]
        w = w_ref[...]
        xs = [_shift_down(x, SSD_D_CONV - 1 - k) for k in range(SSD_D_CONV)]
        y = b_ref[...]
        for k in range(SSD_D_CONV):
            y = y + w[k:k + 1, :] * xs[k]
        sg = _sigmoid(y)
        dy = d * (sg * (1.0 + y * (1.0 - sg)))
        dx = w[3:4, :] * dy
        for k in range(SSD_D_CONV - 1):
            dx = dx + w[k:k + 1, :] * _shift_up(dy, SSD_D_CONV - 1 - k)
        dx_ref[...] = dx.astype(dx_ref.dtype)
        for k in range(SSD_D_CONV):
            dw_ref[k:k + 1, :] = jnp.sum(dy * xs[k], axis=0, keepdims=True)
        db_ref[...] = jnp.sum(dy, axis=0, keepdims=True)

    part = lambda lo, n: pl.BlockSpec((s_dim, CONV_TC), lambda j: (0, jnp.clip(j - lo, 0, n - 1)))
    return pl.pallas_call(
        body, grid=(SSD_CONV_DIM // CONV_TC,),
        in_specs=[pl.BlockSpec((s_dim, CONV_TC), lambda j: (0, off + j)), pl.BlockSpec((SSD_D_CONV, CONV_TC), lambda j: (0, j)),
                  pl.BlockSpec((1, CONV_TC), lambda j: (0, j)), part(0, n_x), part(n_x, n_b), part(n_x + n_b, n_b), _ANY],
        out_specs=[pl.BlockSpec((s_dim, CONV_TC), lambda j: (0, off + j)), pl.BlockSpec((SSD_D_CONV, CONV_TC), lambda j: (0, j)),
                   pl.BlockSpec((1, CONV_TC), lambda j: (0, j))],
        out_shape=[jax.ShapeDtypeStruct(dzx.shape, dzx.dtype), jax.ShapeDtypeStruct((SSD_D_CONV, SSD_CONV_DIM), F32),
                   jax.ShapeDtypeStruct((1, SSD_CONV_DIM), F32)],
        input_output_aliases={6: 0}, compiler_params=_params("arbitrary"), name=name,
    )(pzx, cw, cb, dxs, dbm, dcm, dzx)


def _ssd_step(xs, bm, cm, dtraw, bias, alog, dskip, st_in, z, gw, dot, cumsum):
    n = xs.shape[0]
    lane = lax.broadcasted_iota(jnp.int32, (1, LANES), 1)
    sub = lax.broadcasted_iota(jnp.int32, (LANES, 1), 0)
    left = (lane < 64).astype(F32)
    right = 1.0 - left
    top = (sub < 64).astype(F32)
    bot = 1.0 - top
    row = lax.broadcasted_iota(jnp.int32, (n, n), 0)
    colm = lax.broadcasted_iota(jnp.int32, (n, n), 1)
    causal = row >= colm

    dt = _softplus(dtraw + bias)
    adt = dt * (-jnp.exp(alog))
    acum = cumsum(adt)
    acum_t = acum.T
    last = jnp.sum(adt, axis=0, keepdims=True)
    scores = dot(cm, bm, "nt")

    def lane_of(v, h):
        return jnp.sum(v * (lane == h).astype(F32), axis=1, keepdims=True)

    ys, sts = [], []
    for pr in range(4):
        heads = (2 * pr, 2 * pr + 1)
        ac = [lane_of(acum, h) for h in heads]
        ar = [jnp.sum(acum_t * (sub == h).astype(F32), axis=0, keepdims=True) for h in heads]
        dth = [lane_of(dt, h) for h in heads]
        la = [lane_of(last, h) for h in heads]
        dk = [lane_of(dskip, h) for h in heads]
        x2 = xs[:, pr * LANES:(pr + 1) * LANES]
        xdt = x2 * (dth[0] * left + dth[1] * right)
        yd = None
        for i, side in enumerate((left, right)):
            decay = jnp.where(causal, jnp.exp(jnp.minimum(ac[i] - ar[i], 0.0)), 0.0)
            t = dot(scores * decay, xdt * side, "nn")
            yd = t if yd is None else yd + t
        st2 = st_in[pr * LANES:(pr + 1) * LANES, :]
        yo = dot(cm, st2, "nt") * (jnp.exp(ac[0]) * left + jnp.exp(ac[1]) * right)
        dte = jnp.exp(la[0] - ac[0]) * left + jnp.exp(la[1] - ac[1]) * right
        cs = dot(xdt * dte, bm, "tn")
        sts.append(st2 * (jnp.exp(la[0]) * top + jnp.exp(la[1]) * bot) + cs)
        ys.append(yd + yo + (dk[0] * left + dk[1] * right) * x2)
    y = jnp.concatenate(ys, axis=1)
    yg = y * (z * _sigmoid(z))
    yn = yg * lax.rsqrt(jnp.mean(yg * yg, axis=-1, keepdims=True) + GATED_NORM_EPS) * gw
    return yn, jnp.concatenate(sts, axis=0)


def _ssd_specs(n_chunks, rev):
    ci = (lambda c: n_chunks - 1 - c) if rev else (lambda c: c)
    n_x = SSD_D_INNER // LANES
    return dict(
        xs=pl.BlockSpec((SSD_CHUNK, SSD_GROUP_W), lambda g, c: (ci(c), g)),
        bm=pl.BlockSpec((SSD_CHUNK, LANES), lambda g, c: (ci(c), n_x + g)),
        cm=pl.BlockSpec((SSD_CHUNK, LANES), lambda g, c: (ci(c), n_x + SSD_N_GROUPS + g)),
        dt=pl.BlockSpec((None, SSD_CHUNK, LANES), lambda g, c: (g, ci(c), 0)),
        vec=pl.BlockSpec((None, 1, LANES), lambda g, c: (g, 0, 0)),
        z=pl.BlockSpec((SSD_CHUNK, SSD_GROUP_W), lambda g, c: (ci(c), g)),
        gw=pl.BlockSpec((1, SSD_GROUP_W), lambda g, c: (0, g)),
        st=pl.BlockSpec((None, None, SSD_GROUP_W, SSD_D_STATE), lambda g, c: (g, ci(c), 0, 0)),
    )


def _ssd_fwd(act, dtg, bias, alog, dskip, pzx, gw, name, rider=None):
    s_dim = act.shape[0]
    n_chunks = s_dim // SSD_CHUNK
    sp = _ssd_specs(n_chunks, False)

    def body(xs, bm, cm, dt, b_ref, a_ref, d_ref, z, gw_ref, yn_ref, st_ref, state):
        @pl.when(pl.program_id(1) == 0)
        def _():
            state[...] = jnp.zeros_like(state)

        st_in = state[...]
        st_ref[...] = st_in
        yn, st_out = _ssd_step(xs[...], bm[...], cm[...], dt[...], b_ref[...], a_ref[...], d_ref[...], st_in, z[...], gw_ref[...],
                               _dot, _cumsum_rows_raw)
        yn_ref[...] = yn.astype(yn_ref.dtype)
        state[...] = st_out

    outs, rode = _pcall(
        body, grid=(SSD_N_GROUPS, n_chunks),
        in_specs=[sp["xs"], sp["bm"], sp["cm"], sp["dt"], sp["vec"], sp["vec"], sp["vec"], sp["z"], sp["gw"]],
        out_specs=[sp["xs"], sp["st"]],
        out_shape=[jax.ShapeDtypeStruct((s_dim, SSD_D_INNER), BF16),
                   jax.ShapeDtypeStruct((SSD_N_GROUPS, n_chunks, SSD_GROUP_W, SSD_D_STATE), F32)],
        scratch_shapes=[pltpu.VMEM((SSD_GROUP_W, SSD_D_STATE), F32)],
        args=[act, act, act, dtg, bias, alog, dskip, pzx, gw], sem=("parallel", "arbitrary"), name=name, rider=rider)
    return (outs, rode) if rider is not None else outs


def _ssd_bwd(act, dtg, bias, alog, dskip, pzx, gw, states, dyn, name, rider=None):
    s_dim = act.shape[0]
    n_chunks = s_dim // SSD_CHUNK
    sp = _ssd_specs(n_chunks, True)
    rc = lambda c: n_chunks - 1 - c

    def body(xs, bm, cm, dt, b_ref, a_ref, d_ref, z, gw_ref, st_ref, dyn_ref,
             dxs_ref, dbm_ref, dcm_ref, ddt_ref, db_ref, da_ref, dd_ref, dz_ref, dgw_ref, dstate):
        first = pl.program_id(1) == 0

        @pl.when(first)
        def _():
            dstate[...] = jnp.zeros_like(dstate)
            db_ref[...] = jnp.zeros_like(db_ref)
            da_ref[...] = jnp.zeros_like(da_ref)
            dd_ref[...] = jnp.zeros_like(dd_ref)
            dgw_ref[...] = jnp.zeros_like(dgw_ref)

        fn = functools.partial(_ssd_step, dot=_gdot, cumsum=_cumsum_rows)
        _, vjp = jax.vjp(fn, xs[...], bm[...], cm[...], dt[...], b_ref[...], a_ref[...], d_ref[...], st_ref[...], z[...], gw_ref[...])
        dxs, dbm, dcm, ddt, db, da, dd, dst, dz, dgw = vjp((dyn_ref[...], dstate[...]))
        dxs_ref[...] = dxs
        dbm_ref[...] = dbm
        dcm_ref[...] = dcm
        ddt_ref[...] = ddt
        dz_ref[...] = dz.astype(dz_ref.dtype)
        db_ref[...] += db
        da_ref[...] += da
        dd_ref[...] += dd
        dgw_ref[...] += dgw
        dstate[...] = dst

    bc = pl.BlockSpec((SSD_CHUNK, LANES), lambda g, c: (rc(c), g))
    outs, rode = _pcall(
        body, grid=(SSD_N_GROUPS, n_chunks),
        in_specs=[sp["xs"], sp["bm"], sp["cm"], sp["dt"], sp["vec"], sp["vec"], sp["vec"], sp["z"], sp["gw"], sp["st"], sp["xs"]],
        out_specs=[sp["xs"], bc, bc, sp["dt"], sp["vec"], sp["vec"], sp["vec"], sp["xs"], sp["gw"]],
        out_shape=[jax.ShapeDtypeStruct((s_dim, SSD_D_INNER), F32),
                   jax.ShapeDtypeStruct((s_dim, SSD_N_GROUPS * SSD_D_STATE), F32),
                   jax.ShapeDtypeStruct((s_dim, SSD_N_GROUPS * SSD_D_STATE), F32),
                   jax.ShapeDtypeStruct((SSD_N_GROUPS, s_dim, LANES), F32),
                   jax.ShapeDtypeStruct((SSD_N_GROUPS, 1, LANES), F32),
                   jax.ShapeDtypeStruct((SSD_N_GROUPS, 1, LANES), F32),
                   jax.ShapeDtypeStruct((SSD_N_GROUPS, 1, LANES), F32),
                   jax.ShapeDtypeStruct((s_dim, SSD_ZX), BF16),
                   jax.ShapeDtypeStruct((1, SSD_D_INNER), F32)],
        scratch_shapes=[pltpu.VMEM((SSD_GROUP_W, SSD_D_STATE), F32)],
        args=[act, act, act, dtg, bias, alog, dskip, pzx, gw, states, dyn], sem=("arbitrary", "arbitrary"), name=name, rider=rider)
    return (outs, rode) if rider is not None else outs


SB_T = 128
SB_GROUP = 8
SB_WIDE = SB_GROUP * SB_T
SB_HB = 4
SB_SCALE = 1.0 / math.sqrt(SB_HEAD_DIM)


def _qknorm_fwd(proj, qw, kw, name, tm=512):
    s_dim = proj.shape[0]

    def body(q_ref, k_ref, v_ref, qw_ref, kw_ref, qo, ko, vo):
        for hh in range(SB_HB):
            qo[:, _head_lanes(hh)] = _rms(q_ref[:, _head_lanes(hh)], qw_ref[...], NORM_EPS).astype(BF16)
            ko[:, _head_lanes(hh)] = _rms(k_ref[:, _head_lanes(hh)], kw_ref[...], NORM_EPS).astype(BF16)
        vo[...] = v_ref[...].astype(BF16)

    groups = SB_N_HEADS // SB_HB
    blk = lambda o: pl.BlockSpec((tm, SB_HB * SB_HEAD_DIM), lambda i, h: (i, o + h))
    vec = pl.BlockSpec((1, SB_HEAD_DIM), lambda i, h: (0, 0))
    return pl.pallas_call(
        body, grid=(s_dim // tm, groups),
        in_specs=[blk(0), blk(groups), blk(2 * groups), vec, vec],
        out_specs=[blk(0)] * 3,
        out_shape=[jax.ShapeDtypeStruct((s_dim, SB_WIDTH), BF16)] * 3,
        compiler_params=_params("parallel", "parallel"), name=name,
    )(proj, proj, proj, qw, kw)


def _qknorm_bwd(proj, qw, kw, dqn, dkn, name, tm=512):
    s_dim = proj.shape[0]

    def body(q_ref, k_ref, dq_ref, dk_ref, qw_ref, kw_ref, dqo, dko, dqw, dkw):
        @pl.when((pl.program_id(0) == 0) & (pl.program_id(1) == 0))
        def _():
            dqw[...] = jnp.zeros_like(dqw)
            dkw[...] = jnp.zeros_like(dkw)

        fn = lambda a, b: _rms(a, b, NORM_EPS)
        for hh in range(SB_HB):
            lanes = _head_lanes(hh)
            for x_ref, w_ref, d_ref, dx_out, dw_out in ((q_ref, qw_ref, dq_ref, dqo, dqw), (k_ref, kw_ref, dk_ref, dko, dkw)):
                _, vjp = jax.vjp(fn, x_ref[:, lanes], w_ref[...])
                dx, dw = vjp(d_ref[:, lanes])
                dx_out[:, lanes] = dx.astype(BF16)
                dw_out[...] += dw

    groups = SB_N_HEADS // SB_HB
    blk = lambda o: pl.BlockSpec((tm, SB_HB * SB_HEAD_DIM), lambda i, h: (i, o + h))
    vec = pl.BlockSpec((1, SB_HEAD_DIM), lambda i, h: (0, 0))
    return pl.pallas_call(
        body, grid=(s_dim // tm, groups),
        in_specs=[blk(0), blk(groups), blk(0), blk(0), vec, vec],
        out_specs=[blk(0), blk(0), vec, vec],
        out_shape=[jax.ShapeDtypeStruct((s_dim, SB_WIDTH), BF16)] * 2 + [jax.ShapeDtypeStruct((1, SB_HEAD_DIM), F32)] * 2,
        compiler_params=_params("arbitrary", "arbitrary"), name=name,
    )(proj, proj, dqn, dkn, qw, kw)


def _sb_logits(q, k, strict):
    z = _dot(q, k, "nt") * SB_SCALE
    lb = jnp.minimum(z, 0.0) - jnp.log(1.0 + jnp.exp(-jnp.abs(z)))
    lm = lb - z
    if strict is not None:
        lm = jnp.where(strict, lm, 0.0)
    return lb, lm


def _sb_strict(qi, grp):
    r = lax.broadcasted_iota(jnp.int32, (SB_T, SB_WIDE), 0) + qi * SB_T
    c = lax.broadcasted_iota(jnp.int32, (SB_T, SB_WIDE), 1) + grp * SB_WIDE
    return c < r


def _head_lanes(hh):
    return slice(hh * SB_HEAD_DIM, (hh + 1) * SB_HEAD_DIM)


def _sb_fwd(qn, kn, vb, proj, name, rider=None):
    s_dim = qn.shape[0]
    nq = s_dim // SB_T
    assert nq % SB_GROUP == 0

    def body(q_ref, k_ref, v_ref, g_ref, og_ref, o_ref, t_ref):
        qi = pl.program_id(1)
        top = qi // SB_GROUP
        after = _tri(SB_T, True, strict=True)
        qs = [q_ref[:, _head_lanes(hh)] for hh in range(SB_HB)]

        def step(grp, masked, carries):
            start = pl.multiple_of(grp * SB_WIDE, SB_WIDE)
            strict = _sb_strict(qi, grp) if masked else None
            out = []
            for hh in range(SB_HB):
                o_acc, cr = carries[hh]
                k = k_ref[pl.ds(start, SB_WIDE), _head_lanes(hh)]
                v = v_ref[pl.ds(start, SB_WIDE), _head_lanes(hh)]
                lb, lm = _sb_logits(qs[hh], k, strict)
                rest = [None] * SB_GROUP
                for t in reversed(range(SB_GROUP)):
                    lm_t = lm[:, t * SB_T:(t + 1) * SB_T]
                    rest[t] = cr + _split_dot(lm_t, after, 2, True)
                    cr = cr + jnp.sum(lm_t, axis=1, keepdims=True)
                a = jnp.exp(lb + jnp.concatenate(rest, axis=1))
                if masked:
                    a = jnp.where(strict, a, 0.0)
                out.append((o_acc + _dot(a, v), cr))
            return tuple(out)

        init = tuple((jnp.zeros((SB_T, SB_HEAD_DIM), F32), jnp.zeros((SB_T, 1), F32)) for _ in range(SB_HB))
        carries = step(top, True, init)
        carries = lax.fori_loop(0, top, lambda i, c: step(top - 1 - i, False, c), carries)
        for hh in range(SB_HB):
            o, tot = carries[hh]
            g = g_ref[:, _head_lanes(hh)]
            o_ref[:, _head_lanes(hh)] = o
            og_ref[:, _head_lanes(hh)] = (o * (g * _sigmoid(g))).astype(og_ref.dtype)
            t_ref[hh] = jnp.broadcast_to(tot, (SB_T, LANES))

    wide = SB_HB * SB_HEAD_DIM
    qb = pl.BlockSpec((SB_T, wide), lambda h, i: (i, h))
    kv = pl.BlockSpec((s_dim, wide), lambda h, i: (0, h))
    outs, rode = _pcall(
        body, grid=(SB_N_HEADS // SB_HB, nq),
        in_specs=[qb, kv, kv, pl.BlockSpec((SB_T, wide), lambda h, i: (i, 3 * SB_N_HEADS // SB_HB + h))],
        out_specs=[qb, qb, pl.BlockSpec((SB_HB, SB_T, LANES), lambda h, i: (h, i, 0))],
        out_shape=[jax.ShapeDtypeStruct((s_dim, SB_WIDTH), BF16), jax.ShapeDtypeStruct((s_dim, SB_WIDTH), F32),
                   jax.ShapeDtypeStruct((SB_N_HEADS, s_dim, LANES), F32)],
        args=[qn, kn, vb, proj], sem=("parallel", "arbitrary"), name=name, rider=rider)
    return (outs, rode) if rider is not None else outs


def _sb_bwd(qn, kn, vb, proj, o, tot, dog, name, rider=None):
    s_dim = qn.shape[0]
    nq = s_dim // SB_T
    assert nq % SB_GROUP == 0

    def body(q_ref, k_ref, v_ref, g_ref, o_ref, t_ref, dog_ref, dq_ref, dk_ref, dv_ref, dvb_ref, dg_ref):
        qi = pl.program_id(1)
        top = qi // SB_GROUP

        @pl.when(qi == 0)
        def _():
            dk_ref[...] = jnp.zeros_like(dk_ref)
            dv_ref[...] = jnp.zeros_like(dv_ref)

        after = _tri(SB_T, True, strict=True)
        before = _tri(SB_T, False, strict=True)
        qs, dos, totals = [], [], []
        for hh in range(SB_HB):
            g = g_ref[:, _head_lanes(hh)]
            sg = _sigmoid(g)
            dog_v = dog_ref[:, _head_lanes(hh)]
            dg_ref[:, _head_lanes(hh)] = (dog_v * o_ref[:, _head_lanes(hh)] * (sg * (1.0 + g * (1.0 - sg)))).astype(dg_ref.dtype)
            dos.append((dog_v * (g * sg)).astype(BF16))
            qs.append(q_ref[:, _head_lanes(hh)])
            totals.append(t_ref[hh][:, 0:1])

        def step(grp, masked, carries):
            start = pl.multiple_of(grp * SB_WIDE, SB_WIDE)
            strict = _sb_strict(qi, grp) if masked else None
            out = []
            for hh in range(SB_HB):
                dq_acc, cp, ce = carries[hh]
                q, do = qs[hh], dos[hh]
                k = k_ref[pl.ds(start, SB_WIDE), _head_lanes(hh)]
                v = v_ref[pl.ds(start, SB_WIDE), _head_lanes(hh)]
                lb, lm = _sb_logits(q, k, strict)
                rest = []
                for t in range(SB_GROUP):
                    lm_t = lm[:, t * SB_T:(t + 1) * SB_T]
                    cp = cp + jnp.sum(lm_t, axis=1, keepdims=True)
                    rest.append((totals[hh] - cp) + _split_dot(lm_t, after, 2, True))
                a = jnp.exp(lb + jnp.concatenate(rest, axis=1))
                if masked:
                    a = jnp.where(strict, a, 0.0)
                e = a * _dot(do, v, "nt")
                excl = []
                for t in range(SB_GROUP):
                    e_t = e[:, t * SB_T:(t + 1) * SB_T]
                    excl.append(ce + _split_dot(e_t, before, 1, True))
                    ce = ce + jnp.sum(e_t, axis=1, keepdims=True)
                eex = jnp.concatenate(excl, axis=1)
                if masked:
                    eex = jnp.where(strict, eex, 0.0)
                sig = jnp.exp(lb)
                dz = (e * (1.0 - sig) - eex * sig) * SB_SCALE
                dv_ref[pl.ds(start, SB_WIDE), _head_lanes(hh)] += _dot(a, do, "tn")
                dk_ref[pl.ds(start, SB_WIDE), _head_lanes(hh)] += _dot(dz, q, "tn")
                out.append((dq_acc + _dot(dz, k), cp, ce))
            return tuple(out)

        zero = jnp.zeros((SB_T, 1), F32)
        init = tuple((jnp.zeros((SB_T, SB_HEAD_DIM), F32), zero, zero) for _ in range(SB_HB))
        carries = lax.fori_loop(0, top, lambda i, c: step(i, False, c), init)
        carries = step(top, True, carries)
        for hh in range(SB_HB):
            dq_ref[:, _head_lanes(hh)] = carries[hh][0]

        @pl.when(qi == nq - 1)
        def _():
            dvb_ref[...] = dv_ref[...].astype(BF16)

    wide = SB_HB * SB_HEAD_DIM
    qb = pl.BlockSpec((SB_T, wide), lambda h, i: (i, h))
    kv = pl.BlockSpec((s_dim, wide), lambda h, i: (0, h))
    outs, rode = _pcall(
        body, grid=(SB_N_HEADS // SB_HB, nq),
        in_specs=[qb, kv, kv, pl.BlockSpec((SB_T, wide), lambda h, i: (i, 3 * SB_N_HEADS // SB_HB + h)), qb,
                  pl.BlockSpec((SB_HB, SB_T, LANES), lambda h, i: (h, i, 0)), qb],
        out_specs=[qb, kv, kv, kv, qb],
        out_shape=[jax.ShapeDtypeStruct((s_dim, SB_WIDTH), F32), jax.ShapeDtypeStruct((s_dim, SB_WIDTH), F32),
                   jax.ShapeDtypeStruct((s_dim, SB_WIDTH), F32), jax.ShapeDtypeStruct((s_dim, SB_WIDTH), BF16),
                   jax.ShapeDtypeStruct((s_dim, SB_WIDTH), BF16)],
        args=[qn, kn, vb, proj, o, tot, dog], sem=("parallel", "arbitrary"), name=name, rider=rider)
    return (outs, rode) if rider is not None else outs


def _adamw_math(w, g, m, v):
    m = ADAM_B1 * m + (1.0 - ADAM_B1) * g
    v = ADAM_B2 * v + (1.0 - ADAM_B2) * (g * g)
    m_hat = m / (1.0 - ADAM_B1 ** ADAM_STEP)
    v_hat = v / (1.0 - ADAM_B2 ** ADAM_STEP)
    delta = -ADAM_LR * (m_hat / (jnp.sqrt(v_hat) + ADAM_EPS) + ADAM_WD * w)
    return delta, m, v


def _row_block(rows, cols, itemsize=4, limit=1 << 20):
    tr = rows
    while tr * cols * itemsize > limit and tr % (2 * BF16_ROWS) == 0:
        tr //= 2
    return tr


def _divisor_block(rows, cols, itemsize=4, limit=2 << 20):
    best = BF16_ROWS
    for t in range(BF16_ROWS, rows + 1, BF16_ROWS):
        if rows % t == 0 and t * cols * itemsize <= limit:
            best = t
    return best


def _adamw(w, g, m, v, name, rider=None):
    n, rows, cols = w.shape
    tr = rows if rows * cols * 4 <= (2 << 20) else _divisor_block(rows, cols)

    def body(w_ref, g_ref, m_ref, v_ref, d_out, m_out, v_out):
        d, m_new, v_new = _adamw_math(w_ref[...], g_ref[...], m_ref[...], v_ref[...])
        d_out[...] = d
        m_out[...] = m_new
        v_out[...] = v_new

    blk = pl.BlockSpec((None, tr, cols), lambda i, j: (i, j, 0))
    outs, rode = _pcall(
        body, grid=(n, rows // tr), in_specs=[blk] * 4, out_specs=[blk] * 3,
        out_shape=[jax.ShapeDtypeStruct(w.shape, F32)] * 3,
        args=[w, g, m, v], sem=("parallel", "parallel"), name=name, rider=rider)
    return (outs, rode) if rider is not None else outs


_FLIPS = ((1, 0), (0, 1), (1, 1))


def _place():
    return lax.axis_index("x"), lax.axis_index("y"), lax.axis_index("c")


def _flip(v, f):
    return 1 - v if f else v


def _half_rows(ref, lead, hc, hr):
    return ref.at[(*lead, pl.ds(pl.multiple_of(hc * hr, BF16_ROWS), hr), slice(None))]


def _half_cols(ref, lead, hc, hw):
    return ref.at[(*lead, pl.ds(pl.multiple_of(hc * hw, LANES), hw))]


def _rows_of_chip(chip, r):
    return pl.ds(pl.multiple_of(chip * r, BF16_ROWS), r)


def _slot_half(gathered, shard_shape, chip, l, hc):
    r, c = shard_shape[1:]
    if len(gathered.shape) == 3:
        return _half_cols(gathered, (l, _rows_of_chip(chip, r)), hc, c // 2)
    return _half_rows(gathered, (chip, l), hc, r // 2)


def _shard_half(shard, stacked, l, hc):
    r, c = shard.shape[1:]
    return _half_cols(shard, (l, slice(None)), hc, c // 2) if stacked else _half_rows(shard, (l,), hc, r // 2)


def _remote(src, dst, send, recv, k, to):
    return pltpu.make_async_remote_copy(src_ref=src, dst_ref=dst, send_sem=send.at[k], recv_sem=recv.at[k], device_id=to,
                                        device_id_type=MESH)


def _comm_call(reads, writes, n_sems, phases, name):
    passed = [k for k, w in enumerate(writes) if not isinstance(w, jax.ShapeDtypeStruct)]
    n_rd = len(reads)

    def body(*refs):
        rd = refs[:n_rd]
        wr = refs[n_rd + len(passed):n_rd + len(passed) + len(writes)]
        send, recv = refs[-2:]
        for phase in phases:
            sends, arrivals = phase(rd, wr, send, recv)
            for cp in sends:
                cp.start()
            for cp in arrivals:
                cp.wait_recv()
            for cp in sends:
                cp.wait_send()

    return pl.pallas_call(
        body, in_specs=[_ANY] * (n_rd + len(passed)), out_specs=[_ANY] * len(writes),
        out_shape=[jax.ShapeDtypeStruct(w.shape, w.dtype) for w in writes],
        input_output_aliases={n_rd + pos: k for pos, k in enumerate(passed)},
        scratch_shapes=[pltpu.SemaphoreType.DMA((n_sems,)), pltpu.SemaphoreType.DMA((n_sems,))], name=name,
    )(*reads, *[writes[k] for k in passed])


def _ag_ici(pieces, names, base=0):
    def phase(shards, gathered, send, recv):
        x, y, c = _place()
        me = 2 * x + y
        sends, arrivals = [], []
        for k, (n, l) in enumerate(pieces):
            a = names.index(n)
            shape = shards[a].shape
            src = _shard_half(shards[a], len(gathered[a].shape) == 3, l, c)
            for j, (fx, fy) in enumerate(_FLIPS):
                tx, ty = _flip(x, fx), _flip(y, fy)
                sends.append(_remote(src, _slot_half(gathered[a], shape, me, l, c), send, recv, base + 3 * k + j, (tx, ty, c)))
                arrivals.append(_remote(src, _slot_half(gathered[a], shape, 2 * tx + ty, l, c), send, recv, base + 3 * k + j, (tx, ty, c)))
        return sends, arrivals

    return phase


def _ag_pass_on(pieces, names, shapes, base=0):
    def phase(_, gathered, send, recv):
        x, y, c = _place()
        sibling = (x, y, 1 - c)
        sends, arrivals = [], []
        for k, (n, l) in enumerate(pieces):
            a = names.index(n)
            for j, (fx, fy) in enumerate(_FLIPS):
                chip = 2 * _flip(x, fx) + _flip(y, fy)
                landed = _slot_half(gathered[a], shapes[a], chip, l, c)
                sends.append(_remote(landed, landed, send, recv, base + 3 * k + j, sibling))
                arrivals.append(_remote(landed, _slot_half(gathered[a], shapes[a], chip, l, 1 - c), send, recv, base + 3 * k + j, sibling))
        return sends, arrivals

    return phase


def _other_half(ref, hc):
    if len(ref.shape) == 3:
        return _half_cols(ref, (slice(None), slice(None)), hc, ref.shape[2] // 2)
    return _half_rows(ref, (slice(None), slice(None)), hc, ref.shape[2] // 2)


def _half_shape(shape):
    return shape[:2] + (shape[2] // 2,) if len(shape) == 3 else shape[:2] + (shape[2] // 2, shape[3])


def _exchange_phase(n_arr):
    def phase(ins, outs, send, recv):
        x, y, c = _place()
        cps = [_remote(_other_half(ins[a], 1 - c), outs[a], send, recv, a, (x, y, 1 - c)) for a in range(n_arr)]
        return cps, cps

    return phase


def _exchange_outs(grads):
    return [jax.ShapeDtypeStruct(_half_shape(g.shape), g.dtype) for g in grads]


def _pair_exchange(grads, name):
    return _comm_call(grads, _exchange_outs(grads), len(grads), [_exchange_phase(len(grads))], name)


def _exchange_rider(grads):
    return _Rider(grads, _exchange_outs(grads), len(grads), _exchange_phase(len(grads)))


def _pair_sum_stacked(g, got, place, name):
    _, rows, hw = got.shape
    tr = _divisor_block(rows, hw)

    def body(place_ref, g_ref, r_ref, o_ref):
        o_ref[...] = (g_ref[...].astype(F32) + r_ref[...].astype(F32)).astype(o_ref.dtype)

    blk = pl.BlockSpec((None, tr, hw), lambda i, pr: (0, i, 0))
    return pl.pallas_call(
        body,
        grid_spec=pltpu.PrefetchScalarGridSpec(
            num_scalar_prefetch=1, grid=(rows // tr,),
            in_specs=[pl.BlockSpec((None, tr, hw), lambda i, pr: (0, i, pr[1])), blk], out_specs=blk),
        out_shape=jax.ShapeDtypeStruct(got.shape, BF16),
        compiler_params=_params("parallel"), name=name,
    )(place, g, got)


def _pair_sum(g, got, place, name):
    if len(g.shape) == 3:
        return _pair_sum_stacked(g, got, place, name)
    _, layers, hr, cols = got.shape
    tr = _row_block(hr, cols, limit=2 << 20)
    per = hr // tr

    def body(place_ref, g_ref, r_ref, o_ref):
        o_ref[...] = (g_ref[...].astype(F32) + r_ref[...].astype(F32)).astype(o_ref.dtype)

    blk = pl.BlockSpec((None, None, tr, cols), lambda k, l, i, pr: (k, l, i, 0))
    return pl.pallas_call(
        body,
        grid_spec=pltpu.PrefetchScalarGridSpec(
            num_scalar_prefetch=1, grid=(4, layers, per),
            in_specs=[pl.BlockSpec((None, None, tr, cols), lambda k, l, i, pr: (k, l, pr[1] * per + i, 0)), blk],
            out_specs=blk),
        out_shape=jax.ShapeDtypeStruct(got.shape, BF16),
        compiler_params=_params("parallel", "parallel", "parallel"), name=name,
    )(place, g, got)


def _scatter_phase(n_arr):
    def phase(ins, outs, send, recv):
        x, y, c = _place()
        cps = []
        for a in range(n_arr):
            for j, (fx, fy) in enumerate(_FLIPS):
                tx, ty = _flip(x, fx), _flip(y, fy)
                if len(ins[a].shape) == 3:
                    src = ins[a].at[:, _rows_of_chip(2 * tx + ty, ins[a].shape[1] // 4), :]
                else:
                    src = ins[a].at[2 * tx + ty]
                cps.append(_remote(src, outs[a].at[j], send, recv, 3 * a + j, (tx, ty, c)))
        return cps, cps

    return phase


def _scatter_outs(pairs):
    return [jax.ShapeDtypeStruct((3, 1, p.shape[1] // 4, p.shape[2]) if len(p.shape) == 3 else (3,) + p.shape[1:], p.dtype) for p in pairs]


def _chip_scatter(pairs, name):
    return _comm_call(pairs, _scatter_outs(pairs), 3 * len(pairs), [_scatter_phase(len(pairs))], name)


def _scatter_rider(pairs):
    return _Rider(pairs, _scatter_outs(pairs), 3 * len(pairs), _scatter_phase(len(pairs)))


def _chip_sum_stacked(p, got, place, layer, layers, o_buf, name, row0=0, rows=None):
    _, r, hw = got.shape[1:]
    rows = rows or r
    tr = _divisor_block(math.gcd(r, row0) if row0 else r, hw)
    per = r // tr
    first = row0 // tr

    def body(place_ref, p_ref, r_ref, *rest):
        o_ref = rest[-1]
        acc = p_ref[...].astype(F32)
        for j in range(3):
            acc = acc + r_ref[j].astype(F32)
        o_ref[...] = acc

    has_buf = o_buf is not None
    return pl.pallas_call(
        body,
        grid_spec=pltpu.PrefetchScalarGridSpec(
            num_scalar_prefetch=1, grid=(per,),
            in_specs=[pl.BlockSpec((None, tr, hw), lambda i, pr: (0, pr[0] * per + i, 0)),
                      pl.BlockSpec((3, None, tr, hw), lambda i, pr: (0, 0, i, 0))] + ([_ANY] if has_buf else []),
            out_specs=pl.BlockSpec((None, tr, hw), lambda i, pr: (layer, first + i, pr[1]))),
        out_shape=jax.ShapeDtypeStruct((layers, rows, 2 * hw), F32),
        input_output_aliases={3: 0} if has_buf else {},
        compiler_params=_params("parallel"), name=name,
    )(*((place, p, got) + ((o_buf,) if has_buf else ())))


def _chip_sum(p, got, place, layer, layers, o_buf, name):
    if len(p.shape) == 3:
        return _chip_sum_stacked(p, got, place, layer, layers, o_buf, name)
    _, _, hr, cols = p.shape
    tr = _row_block(hr, cols, limit=2 << 20)
    per = hr // tr

    def body(place_ref, p_ref, r_ref, *rest):
        o_ref = rest[-1]
        acc = p_ref[...].astype(F32)
        for j in range(3):
            acc = acc + r_ref[j].astype(F32)
        o_ref[...] = acc

    has_buf = o_buf is not None
    return pl.pallas_call(
        body,
        grid_spec=pltpu.PrefetchScalarGridSpec(
            num_scalar_prefetch=1, grid=(per,),
            in_specs=[pl.BlockSpec((None, None, tr, cols), lambda i, pr: (pr[0], 0, i, 0)),
                      pl.BlockSpec((3, None, tr, cols), lambda i, pr: (0, 0, i, 0))] + ([_ANY] if has_buf else []),
            out_specs=pl.BlockSpec((None, tr, cols), lambda i, pr: (layer, pr[1] * per + i, 0))),
        out_shape=jax.ShapeDtypeStruct((layers, 2 * hr, cols), F32),
        input_output_aliases={3: 0} if has_buf else {},
        compiler_params=_params("parallel"), name=name,
    )(*((place, p, got) + ((o_buf,) if has_buf else ())))


def _pair_gather(halves, by_cols, name):
    def phase(_, bufs, send, recv):
        x, y, c = _place()
        sends, arrivals = [], []
        for a, h in enumerate(halves):
            cut = (lambda hc, a=a, h=h: _half_cols(bufs[a], (slice(None), slice(None)), hc, h.shape[2] // 2)) if by_cols[a] else (
                lambda hc, a=a, h=h: _half_rows(bufs[a], (slice(None),), hc, h.shape[1] // 2))
            sends.append(_remote(cut(c), cut(c), send, recv, a, (x, y, 1 - c)))
            arrivals.append(_remote(cut(c), cut(1 - c), send, recv, a, (x, y, 1 - c)))
        return sends, arrivals

    return _comm_call([], halves, len(halves), [phase], name)


def _allreduce_small(v, name):
    rows, cols = v.shape

    def body(v_ref, o_ref, buf, send_sems, recv_sems):
        x, y, c = _place()
        me = 4 * x + 2 * y + c
        buf[0] = v_ref[...]
        cps = []
        for k in range(1, 8):
            kx, ky, kc = (k >> 2) & 1, (k >> 1) & 1, k & 1
            cp = pltpu.make_async_remote_copy(src_ref=v_ref, dst_ref=buf.at[k], send_sem=send_sems.at[k - 1], recv_sem=recv_sems.at[k - 1],
                                              device_id=(_flip(x, kx), _flip(y, ky), _flip(c, kc)), device_id_type=MESH)
            cp.start()
            cps.append(cp)
        for cp in cps:
            cp.wait()
        acc = buf[me]
        for d in range(1, 8):
            acc = acc + buf[jnp.bitwise_xor(d, me)]
        o_ref[...] = acc

    vm = pl.BlockSpec(memory_space=pltpu.VMEM)
    return pl.pallas_call(
        body, in_specs=[vm], out_specs=vm, out_shape=jax.ShapeDtypeStruct((rows, cols), F32),
        scratch_shapes=[pltpu.VMEM((8, rows, cols), F32), pltpu.SemaphoreType.DMA((7,)), pltpu.SemaphoreType.DMA((7,))],
        name=name,
    )(v)


def _pad_lanes(a):
    return jnp.pad(a, ((0, 0), (0, LANES - a.shape[1])))


def _group_lanes(v):
    return jnp.pad(v.reshape(SSD_N_GROUPS, 1, 8), ((0, 0), (0, 0), (0, LANES - 8)))


def kernel(x, p, norm_w, ssd_in_w, ssd_conv_w, ssd_conv_b, ssd_dt_bias, ssd_a_log, ssd_d, ssd_gnorm_w, ssd_out_w, sb_in_w, sb_qn_w, sb_kn_w, sb_out_w, ple_norm_w, ple_gate_w, ple_proj_w, loss_target, m_norm_w, m_ssd_in_w, m_ssd_conv_w, m_ssd_conv_b, m_ssd_dt_bias, m_ssd_a_log, m_ssd_d, m_ssd_gnorm_w, m_ssd_out_w, m_sb_in_w, m_sb_qn_w, m_sb_kn_w, m_sb_out_w, m_ple_norm_w, m_ple_gate_w, m_ple_proj_w, v_norm_w, v_ssd_in_w, v_ssd_conv_w, v_ssd_conv_b, v_ssd_dt_bias, v_ssd_a_log, v_ssd_d, v_ssd_gnorm_w, v_ssd_out_w, v_sb_in_w, v_sb_qn_w, v_sb_kn_w, v_sb_out_w, v_ple_norm_w, v_ple_gate_w, v_ple_proj_w):
    w_in = dict(norm_w=norm_w, ssd_in_w=ssd_in_w, ssd_conv_w=ssd_conv_w, ssd_conv_b=ssd_conv_b, ssd_dt_bias=ssd_dt_bias,
                ssd_a_log=ssd_a_log, ssd_d=ssd_d, ssd_gnorm_w=ssd_gnorm_w, ssd_out_w=ssd_out_w, sb_in_w=sb_in_w, sb_qn_w=sb_qn_w,
                sb_kn_w=sb_kn_w, sb_out_w=sb_out_w, ple_norm_w=ple_norm_w, ple_gate_w=ple_gate_w, ple_proj_w=ple_proj_w)
    m_in = dict(norm_w=m_norm_w, ssd_in_w=m_ssd_in_w, ssd_conv_w=m_ssd_conv_w, ssd_conv_b=m_ssd_conv_b, ssd_dt_bias=m_ssd_dt_bias,
                ssd_a_log=m_ssd_a_log, ssd_d=m_ssd_d, ssd_gnorm_w=m_ssd_gnorm_w, ssd_out_w=m_ssd_out_w, sb_in_w=m_sb_in_w,
                sb_qn_w=m_sb_qn_w, sb_kn_w=m_sb_kn_w, sb_out_w=m_sb_out_w, ple_norm_w=m_ple_norm_w, ple_gate_w=m_ple_gate_w,
                ple_proj_w=m_ple_proj_w)
    v_in = dict(norm_w=v_norm_w, ssd_in_w=v_ssd_in_w, ssd_conv_w=v_ssd_conv_w, ssd_conv_b=v_ssd_conv_b, ssd_dt_bias=v_ssd_dt_bias,
                ssd_a_log=v_ssd_a_log, ssd_d=v_ssd_d, ssd_gnorm_w=v_ssd_gnorm_w, ssd_out_w=v_ssd_out_w, sb_in_w=v_sb_in_w,
                sb_qn_w=v_sb_qn_w, sb_kn_w=v_sb_kn_w, sb_out_w=v_sb_out_w, ple_norm_w=v_ple_norm_w, ple_gate_w=v_ple_gate_w,
                ple_proj_w=v_ple_proj_w)
    ix, iy, ic = lax.axis_index("x"), lax.axis_index("y"), lax.axis_index("c")
    chip = (2 * ix + iy).astype(jnp.int32)
    place = jnp.stack([chip, ic.astype(jnp.int32)])
    zero = jnp.zeros((), jnp.int32)
    big_names = [n for n, _, _ in _BIG]
    layers_of = {n: s[0] for n, s, _ in _BIG}
    cut_of = {n: cut for n, _, cut in _BIG}

    def layer_pieces(i):
        mixer = ("ssd_in_w", "ssd_out_w") if i % 2 == 0 else ("sb_in_w", "sb_out_w")
        return [(mixer[0], i // 2), (mixer[1], i // 2), ("ple_gate_w", i), ("ple_proj_w", i)]

    def names_of(pieces):
        return [n for n in big_names if any(n == q for q, _ in pieces)]

    held = lambda n, a: a.transpose(0, 2, 1) if cut_of[n] == "stack" else a
    mine = {n: held(n, w_in[n]).astype(BF16) for n in big_names}
    shard_shapes = [mine[n].shape for n in big_names]
    room = [jax.ShapeDtypeStruct((s[0], 4 * s[1], s[2]) if cut_of[n] == "stack" else (4,) + s, BF16) for n, s in zip(big_names, shard_shapes)]
    first = layer_pieces(0)[:1]
    gathered = _comm_call([mine[n] for n in big_names], room, 6 * len(first),
                          [_ag_ici(first, big_names), _ag_pass_on(first, big_names, shard_shapes, base=3 * len(first))], "allgather_layer0")
    gw = {}
    for n, g in zip(big_names, gathered):
        if cut_of[n] == "stack":
            layers, r, c = mine[n].shape
            gw[n] = lax.dynamic_update_slice(g.reshape(layers, 4, r, c), mine[n][:, None], (zero, chip, zero, zero)).reshape(g.shape)
        else:
            gw[n] = lax.dynamic_update_slice(g, mine[n][None], (chip, zero, zero, zero))

    lp = [layer_pieces(i) for i in range(DEPTH)]
    carries = {
        "ssd_in_0": (lp[0][1:2], []), "conv_0": (lp[0][2:], lp[0][1:2]), "ssd_0": (lp[1][:1], lp[0][2:]),
        "ssd_out_0": (lp[1][1:2], lp[1][:1]), "sb_in_1": (lp[1][2:], lp[1][1:2]), "sb_1": (lp[2][:2], lp[1][2:]),
        "sb_out_1": (lp[2][2:], lp[2][:2]), "ssd_in_2": (lp[3][1:], lp[2][2:]), "ssd_2": (lp[3][:1], lp[3][1:]),
        "ssd_out_2": ([], lp[3][:1]),
    }

    def gather_rider(call):
        if call not in carries:
            return None, lambda outs: outs
        ici, passing = carries[call]
        names = names_of(ici + passing)
        phases = ([_ag_ici(ici, names)] if ici else []) + (
            [_ag_pass_on(passing, names, [mine[n].shape for n in names], base=3 * len(ici))] if passing else [])

        def issue(rd, wr, send, recv):
            both = [ph(rd, wr, send, recv) for ph in phases]
            return sum((b[0] for b in both), []), sum((b[1] for b in both), [])

        def land(outs):
            outs, bufs = outs
            for n, g in zip(names, bufs):
                gw[n] = g
            return outs

        return _Rider([mine[n] for n in names], [gw[n] for n in names], 3 * (len(ici) + len(passing)), issue), land

    onehot = (jnp.arange(4) == chip).astype(F32) * (ic == 0).astype(F32)
    cw_mine = onehot[:, None, None, None] * ssd_conv_w[None]
    cw_full = _allreduce_small(cw_mine.transpose(1, 2, 0, 3).reshape(-1, LANES), "gather_conv_w").reshape(2, SSD_D_CONV, SSD_CONV_DIM)

    def wmm(a, name, layer, *, dn="nn", res=None, call, rider=None):
        return _matmul(a, gw[name], dn=dn, res=res, b_lay=(cut_of[name], layer), name=call, rider=rider)

    h = x[0]
    target = loss_target[0]
    saved = []
    for i in range(DEPTH):
        j = i // 2
        nw = norm_w[i:i + 1]
        pw = ple_norm_w[i:i + 1]
        s = dict(h=h)
        u = _rms_fwd(h, nw, f"rms_{i}")
        s["u"] = u
        if i % 2 == 0:
            w_dt = jnp.pad(gw["ssd_in_w"][j, SSD_ZX:], ((0, LANES - SSD_N_HEADS), (0, 0)))
            rider, land = gather_rider(f"ssd_in_{i}")
            pzx = land(_matmul(u, gw["ssd_in_w"], dn="nt", b_lay=("stack", j, SSD_ZX), name=f"ssd_in_{i}", rider=rider))
            pdt = _matmul(u, w_dt, dn="nt", name=f"ssd_indt_{i}")
            rider, land = gather_rider(f"conv_{i}")
            act = land(_conv_fwd(pzx, cw_full[j], ssd_conv_b[j:j + 1], f"conv_{i}", rider=rider))
            dtg = jnp.pad(pdt[:, :SSD_N_HEADS].reshape(-1, SSD_N_GROUPS, 8).transpose(1, 0, 2), ((0, 0), (0, 0), (0, LANES - 8)))
            vecs = (_group_lanes(ssd_dt_bias[j]), _group_lanes(ssd_a_log[j]), _group_lanes(ssd_d[j]))
            rider, land = gather_rider(f"ssd_{i}")
            yn, states = land(_ssd_fwd(act, dtg, *vecs, pzx, ssd_gnorm_w[j:j + 1], f"ssd_{i}", rider=rider))
            s.update(w_dt=w_dt, pzx=pzx, act=act, dtg=dtg, vecs=vecs, yn=yn, states=states)
            rider, land = gather_rider(f"ssd_out_{i}")
            h1 = land(wmm(yn, "ssd_out_w", j, res=h, call=f"ssd_out_{i}", rider=rider))
        else:
            rider, land = gather_rider(f"sb_in_{i}")
            proj = land(wmm(u, "sb_in_w", j, call=f"sb_in_{i}", rider=rider))
            qn, kn, vb = _qknorm_fwd(proj, sb_qn_w[j:j + 1], sb_kn_w[j:j + 1], f"qknorm_{i}")
            rider, land = gather_rider(f"sb_{i}")
            og, o, tot = land(_sb_fwd(qn, kn, vb, proj, f"sb_{i}", rider=rider))
            s.update(proj=proj, qn=qn, kn=kn, vb=vb, og=og, o=o, tot=tot)
            rider, land = gather_rider(f"sb_out_{i}")
            h1 = land(wmm(og, "sb_out_w", j, res=h, call=f"sb_out_{i}", rider=rider))
        n2 = _rms_fwd(h1, pw, f"ple_rms_{i}")
        gl = wmm(n2, "ple_gate_w", i, call=f"ple_gate_{i}")
        pp = wmm(p[i, 0], "ple_proj_w", i, call=f"ple_proj_{i}")
        h = _ple_fwd(h1, pp, gl, f"ple_{i}")
        s.update(h1=h1, n2=n2, gl=gl, pp=pp)
        saved.append(s)

    dh, loss_lanes = _loss_bwd(h, target, "loss")

    wg = {}
    gsmall = {n: [None] * s[0] for n, s in _SMALL}
    g_conv_w = [None, None]
    scat = {}
    pending = late = None

    def wgrad(a, b, name, layer, call, rider=None):
        out = _matmul(a, b, dn="tn", out_dtype=BF16, o_lay=(cut_of[name], 0, 1), name=call, rider=rider)
        wg[(name, layer)], rode = out if rider is not None else (out, None)
        return rode

    def pair_sums(pieces, got, tag):
        return pieces, [_pair_sum(wg[q], r, place, f"rs_pair_sum_{tag}_{k}") for k, (q, r) in enumerate(zip(pieces, got))]

    def sibling_rider(pieces):
        return _exchange_rider([wg[q] for q in pieces])

    def riding_with(own):
        return (pending[0] + own[0], pending[1] + own[1]) if pending else own

    def arrived(sent, got):
        for q, pair, g in zip(sent[0], sent[1], got):
            scat[q] = (pair, g)

    for i in reversed(range(DEPTH)):
        j = i // 2
        s = saved[i]
        nw = norm_w[i:i + 1]
        pw = ple_norm_w[i:i + 1]
        dpp, dgl = _ple_bwd(dh, s["pp"], s["gl"], f"ple_bwd_{i}")
        wgrad(p[i, 0], dpp, "ple_proj_w", i, f"d_ple_proj_{i}")
        if late is None:
            wgrad(s["n2"], dgl, "ple_gate_w", i, f"d_ple_gate_{i}")
        else:
            pending = pair_sums(late, wgrad(s["n2"], dgl, "ple_gate_w", i, f"d_ple_gate_{i}", rider=sibling_rider(late)), f"{i + 1}_in")
        dn2 = wmm(dgl, "ple_gate_w", i, dn="nt", call=f"ple_gate_bwd_{i}")
        dh1, dpw = _rms_bwd(s["h1"], pw, dn2, dh, f"ple_rms_bwd_{i}")
        gsmall["ple_norm_w"][i] = dpw
        if i % 2 == 0:
            wgrad(s["yn"], dh1, "ssd_out_w", j, f"d_ssd_out_{i}")
            early = layer_pieces(i)[1:]
            dyn, got = wmm(dh1, "ssd_out_w", j, dn="nt", call=f"ssd_out_bwd_{i}", rider=sibling_rider(early))
            riding = riding_with(pair_sums(early, got, f"{i}_out"))
            outs, got = _ssd_bwd(s["act"], s["dtg"], *s["vecs"], s["pzx"], ssd_gnorm_w[j:j + 1], s["states"], dyn, f"ssd_bwd_{i}",
                                 rider=_scatter_rider(riding[1]))
            arrived(riding, got)
            dxs, dbm, dcm, ddtg, dbias, dalog, ddsk, dz, dgw = outs
            dzx, dcw, dcb = _conv_bwd(s["pzx"], cw_full[j], ssd_conv_b[j:j + 1], dxs, dbm, dcm, dz, f"conv_bwd_{i}")
            ddt = _pad_lanes(ddtg[:, :, :8].transpose(1, 0, 2).reshape(-1, SSD_N_HEADS)).astype(BF16)
            dwt = _matmul(dzx, s["u"], dn="tn", out_dtype=BF16, out_rows=SSD_IN_DIM, name=f"d_ssd_in_{i}")
            dwt_dt = _matmul(ddt, s["u"], dn="tn", out_dtype=BF16, name=f"d_ssd_indt_{i}")
            wg[("ssd_in_w", j)] = lax.dynamic_update_slice(dwt, dwt_dt[:SSD_N_HEADS], (SSD_ZX, 0))[None]
            if i == 0:
                by_shard = wg[("ssd_in_w", 0)].reshape(4, -1, D_MODEL)
                parts = [("ssd_in_w", 0, 0), ("ssd_in_w", 0, 1)]
                wg[parts[0]] = by_shard[:, :LAST_SPLIT].reshape(1, -1, D_MODEL)
                wg[parts[1]] = by_shard[:, LAST_SPLIT:].reshape(1, -1, D_MODEL)
                last = pair_sums(parts, _pair_exchange([wg[q] for q in parts], "rs_pair_exchange_last"), "0_in")
                du, got = _matmul(dzx, gw["ssd_in_w"], b_lay=("stack", j, SSD_ZX), name=f"ssd_in_bwd_{i}",
                                  rider=_scatter_rider(last[1][1:]))
                arrived((parts[1:], last[1][1:]), got)
            else:
                du = _matmul(dzx, gw["ssd_in_w"], b_lay=("stack", j, SSD_ZX), name=f"ssd_in_bwd_{i}")
            du = _matmul(ddt, s["w_dt"], res=du, name=f"ssd_indt_bwd_{i}")
            g_conv_w[j] = dcw
            gsmall["ssd_conv_b"][j] = dcb
            gsmall["ssd_dt_bias"][j] = dbias[:, 0, :8].reshape(1, SSD_N_HEADS)
            gsmall["ssd_a_log"][j] = dalog[:, 0, :8].reshape(1, SSD_N_HEADS)
            gsmall["ssd_d"][j] = ddsk[:, 0, :8].reshape(1, SSD_N_HEADS)
            gsmall["ssd_gnorm_w"][j] = dgw
        else:
            wgrad(s["og"], dh1, "sb_out_w", j, f"d_sb_out_{i}")
            early = layer_pieces(i)[1:]
            dog, got = wmm(dh1, "sb_out_w", j, dn="nt", call=f"sb_out_bwd_{i}", rider=sibling_rider(early))
            riding = riding_with(pair_sums(early, got, f"{i}_out"))
            outs, got = _sb_bwd(s["qn"], s["kn"], s["vb"], s["proj"], s["o"], s["tot"], dog, f"sb_bwd_{i}", rider=_scatter_rider(riding[1]))
            arrived(riding, got)
            dqn, dkn, _, dvb, dg = outs
            dq, dk, dqw, dkw = _qknorm_bwd(s["proj"], sb_qn_w[j:j + 1], sb_kn_w[j:j + 1], dqn, dkn, f"qknorm_bwd_{i}")
            dproj = jnp.concatenate([dq, dk, dvb, dg], axis=1)
            du = wmm(dproj, "sb_in_w", j, dn="nt", call=f"sb_in_bwd_{i}")
            wgrad(s["u"], dproj, "sb_in_w", j, f"d_sb_in_{i}")
            gsmall["sb_qn_w"][j] = dqw
            gsmall["sb_kn_w"][j] = dkw
        dh, dnw = _rms_bwd(s["h"], nw, du, dh1, f"rms_bwd_{i}")
        gsmall["norm_w"][i] = dnw
        late = layer_pieces(i)[:1]
    grad_x = dh[None]

    def reduced(names, call):
        halves = []
        for n in names:
            buf = None
            for l in range(layers_of[n]):
                if (n, l, 0) in scat:
                    r = shard_shapes[big_names.index(n)][1]
                    for part, row0 in ((0, 0), (1, LAST_SPLIT)):
                        buf = _chip_sum_stacked(*scat[(n, l, part)], place, l, layers_of[n], buf, f"rs_chip_sum_{n}_{l}_{part}", row0, r)
                else:
                    buf = _chip_sum(*scat[(n, l)], place, l, layers_of[n], buf, f"rs_chip_sum_{n}_{l}")
            halves.append(buf)
        return dict(zip(names, _pair_gather(halves, [cut_of[n] == "stack" for n in names], call)))

    def updated(n, rider=None):
        return _adamw(held(n, w_in[n]), g_big[n], held(n, m_in[n]), held(n, v_in[n]), f"adamw_{n}", rider=rider)

    done_early = ["sb_in_w", "sb_out_w"]
    g_big = reduced(done_early, "rs_pair_gather_sb")
    step = {}
    step["sb_in_w"], got = updated("sb_in_w", rider=_scatter_rider(last[1][:1]))
    arrived((last[0][:1], last[1][:1]), got)
    g_big.update(reduced([n for n in big_names if n not in done_early], "rs_pair_gather"))

    small_parts = [jnp.concatenate(gsmall[n], axis=0).reshape(-1) for n, _ in _SMALL]
    small_parts.append(jnp.stack(g_conv_w).reshape(-1))
    small_parts.append(loss_lanes.reshape(-1))
    small_sum = _allreduce_small(jnp.concatenate(small_parts).reshape(-1, LANES), "allreduce_small").reshape(-1)
    g_small, off = {}, 0
    for n, shape in _SMALL:
        size = math.prod(shape)
        g_small[n] = small_sum[off:off + size].reshape(shape)
        off += size
    cw_size = 2 * SSD_D_CONV * SSD_CONV_DIM
    g_cw_full = small_sum[off:off + cw_size].reshape(2, SSD_D_CONV, 4, SSD_CONV_DIM // 4)
    g_small["ssd_conv_w"] = jnp.sum(g_cw_full * (jnp.arange(4) == chip).astype(F32)[None, None, :, None], axis=2)
    loss = 0.5 * jnp.sum(small_sum[off + cw_size:]) / D_MODEL

    grads, delta, new_m, new_v = {}, {}, {}, {}
    for n in big_names:
        grads[n], delta[n], new_m[n], new_v[n] = (held(n, a) for a in (g_big[n], *(step[n] if n in step else updated(n))))
    small_names = [n for n, _ in _SMALL] + ["ssd_conv_w"]
    pack = lambda d: jnp.concatenate([d[n].reshape(-1) for n in small_names]).reshape(1, -1, LANES)
    ds, ms, vs = _adamw(pack(w_in), pack(g_small), pack(m_in), pack(v_in), "adamw_small")
    off = 0
    for n in small_names:
        shape = w_in[n].shape
        size = math.prod(shape)
        grads[n] = g_small[n]
        delta[n] = ds.reshape(-1)[off:off + size].reshape(shape)
        new_m[n] = ms.reshape(-1)[off:off + size].reshape(shape)
        new_v[n] = vs.reshape(-1)[off:off + size].reshape(shape)
        off += size

    order = ["norm_w", "ssd_in_w", "ssd_conv_w", "ssd_conv_b", "ssd_dt_bias", "ssd_a_log", "ssd_d", "ssd_gnorm_w", "ssd_out_w",
             "sb_in_w", "sb_qn_w", "sb_kn_w", "sb_out_w", "ple_norm_w", "ple_gate_w", "ple_proj_w"]
    return (loss, grad_x, *[grads[n] for n in order], *[delta[n] for n in order], *[new_m[n] for n in order],
            *[new_v[n] for n in order])
```

```python
import functools
import math

import jax
import jax.numpy as jnp
from jax import lax
from jax.experimental import pallas as pl
from jax.experimental.pallas import tpu as pltpu

F32 = jnp.float32
BF16 = jnp.bfloat16
MESH = pl.DeviceIdType.MESH

D_MODEL = 2048
DEPTH = 4
SSD_D_INNER = 4096
SSD_N_GROUPS = 8
SSD_GROUP_W = SSD_D_INNER // SSD_N_GROUPS
SSD_D_STATE = 128
SSD_CHUNK = 128
SSD_CONV_DIM = 6144
SSD_D_CONV = 4
SSD_N_HEADS = 64
SB_HEAD_DIM = 128
SB_N_HEADS = 16
SB_WIDTH = 2048
NORM_EPS = 1e-6
GATED_NORM_EPS = 1e-5
ADAM_LR = 0.001
ADAM_B1 = 0.9
ADAM_B2 = 0.999
ADAM_EPS = 1e-08
ADAM_WD = 0.01
ADAM_STEP = 10

SSD_ZX = SSD_D_INNER + SSD_CONV_DIM
SSD_IN_DIM = SSD_ZX + SSD_N_HEADS
LAST_SPLIT = 1104
LANES = 128
BF16_ROWS = 16

_BIG = (
    ("ssd_in_w", (2, 2576, 2048), "stack"),
    ("ssd_out_w", (2, 1024, 2048), "row"),
    ("sb_in_w", (2, 2048, 2048), "col"),
    ("sb_out_w", (2, 512, 2048), "row"),
    ("ple_gate_w", (4, 512, 2048), "row"),
    ("ple_proj_w", (4, 256, 512), "col"),
)
_SMALL = (
    ("norm_w", (4, 2048)),
    ("ssd_conv_b", (2, 6144)),
    ("ssd_dt_bias", (2, 64)),
    ("ssd_a_log", (2, 64)),
    ("ssd_d", (2, 64)),
    ("ssd_gnorm_w", (2, 4096)),
    ("sb_qn_w", (2, 128)),
    ("sb_kn_w", (2, 128)),
    ("ple_norm_w", (4, 2048)),
)

_DN = {
    "nn": (((1,), (0,)), ((), ())),
    "nt": (((1,), (1,)), ((), ())),
    "tn": (((0,), (0,)), ((), ())),
}


def _dot(a, b, dn="nn"):
    return lax.dot_general(a.astype(BF16), b.astype(BF16), _DN[dn], preferred_element_type=F32)


@functools.partial(jax.custom_vjp, nondiff_argnums=(2,))
def _gdot(a, b, dn):
    return _dot(a, b, dn)


def _gdot_fwd(a, b, dn):
    return _dot(a, b, dn), (a, b)


def _gdot_bwd(dn, res, g):
    a, b = res
    if dn == "nn":
        return _dot(g, b, "nt"), _dot(a, g, "tn")
    if dn == "nt":
        return _dot(g, b, "nn"), _dot(g, a, "tn")
    return _dot(b, g, "nt"), _dot(a, g, "nn")


_gdot.defvjp(_gdot_fwd, _gdot_bwd)


def _split_dot(x, t, parts, x_left):
    acc = None
    r = x
    for i in range(parts):
        p = r.astype(BF16)
        d = lax.dot_general(p, t, _DN["nn"], preferred_element_type=F32) if x_left else lax.dot_general(
            t, p, _DN["nn"], preferred_element_type=F32)
        acc = d if acc is None else acc + d
        if i + 1 < parts:
            r = r - p.astype(F32)
    return acc


def _tri(n, lower, strict=False):
    r = lax.broadcasted_iota(jnp.int32, (n, n), 0)
    c = lax.broadcasted_iota(jnp.int32, (n, n), 1)
    keep = (r > c if strict else r >= c) if lower else (r < c if strict else r <= c)
    return jnp.where(keep, 1.0, 0.0).astype(BF16)


def _cumsum_rows_raw(x):
    return _split_dot(x, _tri(x.shape[0], True), 3, False)


@jax.custom_vjp
def _cumsum_rows(x):
    return _cumsum_rows_raw(x)


def _cumsum_rows_fwd(x):
    return _cumsum_rows_raw(x), None


def _cumsum_rows_bwd(_, g):
    return (_split_dot(g, _tri(g.shape[0], False), 3, False),)


_cumsum_rows.defvjp(_cumsum_rows_fwd, _cumsum_rows_bwd)


def _sigmoid(x):
    return 1.0 / (1.0 + jnp.exp(-x))


def _softplus(x):
    return jnp.maximum(x, 0.0) + jnp.log(1.0 + jnp.exp(-jnp.abs(x)))


def _rms(x, w, eps):
    return x * lax.rsqrt(jnp.mean(x * x, axis=-1, keepdims=True) + eps) * w


_ANY = pl.BlockSpec(memory_space=pl.ANY)


def _params(*sem):
    return pltpu.CompilerParams(dimension_semantics=sem)


class _Rider:
    def __init__(self, reads, writes, n_sems, issue):
        self.reads, self.writes, self.n_sems, self.issue = list(reads), list(writes), n_sems, issue


def _pcall(body, *, grid, in_specs, out_specs, out_shape, args, sem, name, scratch_shapes=(), aliases=None, rider=None):
    aliases = dict(aliases or {})
    if rider is None:
        outs = pl.pallas_call(body, grid=grid, in_specs=in_specs, out_specs=out_specs, out_shape=out_shape,
                              scratch_shapes=list(scratch_shapes), input_output_aliases=aliases,
                              compiler_params=_params(*sem), name=name)(*args)
        return list(outs), []
    n_in, n_out, n_scr, n_rd, n_wr = len(args), len(out_shape), len(scratch_shapes), len(rider.reads), len(rider.writes)
    passed = [k for k, w in enumerate(rider.writes) if not isinstance(w, jax.ShapeDtypeStruct)]
    for pos, k in enumerate(passed):
        aliases[n_in + n_rd + pos] = n_out + k

    def wrapped(*refs):
        ins = refs[:n_in]
        reads = refs[n_in:n_in + n_rd]
        base = n_in + n_rd + len(passed)
        outs = refs[base:base + n_out]
        writes = refs[base + n_out:base + n_out + n_wr]
        scr = refs[base + n_out + n_wr:base + n_out + n_wr + n_scr]
        send, recv = refs[-2:]
        first = last = None
        for d, n in enumerate(grid):
            i = pl.program_id(d)
            first = (i == 0) if first is None else first & (i == 0)
            last = (i == n - 1) if last is None else last & (i == n - 1)

        @pl.when(first)
        def _():
            for cp in rider.issue(reads, writes, send, recv)[0]:
                cp.start()

        body(*ins, *outs, *scr)

        @pl.when(last)
        def _():
            sends, arrivals = rider.issue(reads, writes, send, recv)
            for cp in arrivals:
                cp.wait_recv()
            for cp in sends:
                cp.wait_send()

    outs = pl.pallas_call(
        wrapped, grid=grid,
        in_specs=list(in_specs) + [_ANY] * (n_rd + len(passed)),
        out_specs=list(out_specs) + [_ANY] * n_wr,
        out_shape=list(out_shape) + [jax.ShapeDtypeStruct(w.shape, w.dtype) for w in rider.writes],
        scratch_shapes=list(scratch_shapes) + [pltpu.SemaphoreType.DMA((rider.n_sems,)), pltpu.SemaphoreType.DMA((rider.n_sems,))],
        input_output_aliases=aliases, compiler_params=_params(*(["arbitrary"] * len(grid))), name=name,
    )(*args, *rider.reads, *[rider.writes[k] for k in passed])
    return list(outs[:n_out]), list(outs[n_out:])


MM_TK = 2048


def _pick(dim, pref, unit=None):
    t = pref
    while t >= LANES:
        if dim % t == 0 and (unit is None or unit % t == 0):
            return t
        t //= 2
    return dim


def _matmul(a, b, *, dn="nn", res=None, out_dtype=F32, name, b_lay=None, o_lay=None, o_buf=None, out_rows=None, rider=None):
    if dn == "tn":
        k_dim, m_dim = a.shape
    else:
        m_dim, k_dim = a.shape
    unit_m = unit_n = unit_k = None
    if b_lay is None:
        n_dim = b.shape[0] if dn == "nt" else b.shape[1]
    elif b_lay[0] == "stack":
        cut, layer, rows = b_lay
        cols = b.shape[2]
        n_dim = cols if dn == "nn" else rows
        assert k_dim == (rows if dn == "nn" else cols) and dn != "tn"
    else:
        cut, layer = b_lay
        r, c = b.shape[2:]
        rows, cols = (4 * r, c) if cut == "row" else (r, 4 * c)
        n_dim = cols if dn == "nn" else rows
        assert k_dim == (rows if dn == "nn" else cols) and dn != "tn"
        if (cut == "row") == (dn == "nn"):
            unit_k = r if cut == "row" else c
        else:
            unit_n = r if cut == "row" else c
    if o_lay is not None:
        o_cut, o_layer, o_layers = o_lay
        if o_cut == "row":
            unit_m = m_dim // 4
        else:
            unit_n = n_dim // 4
    tm, tn, tk = _pick(m_dim, 1024, unit_m), _pick(n_dim, 1024, unit_n), _pick(k_dim, MM_TK, unit_k)
    nk = k_dim // tk
    a_spec = pl.BlockSpec((tk, tm), lambda i, j, k: (k, i)) if dn == "tn" else pl.BlockSpec((tm, tk), lambda i, j, k: (i, k))
    if b_lay is None:
        b_spec = pl.BlockSpec((tn, tk), lambda i, j, k: (j, k)) if dn == "nt" else pl.BlockSpec((tk, tn), lambda i, j, k: (k, j))
    elif cut == "stack":
        b_spec = (pl.BlockSpec((None, tk, tn), lambda i, j, k: (layer, k, j)) if dn == "nn" else
                  pl.BlockSpec((None, tn, tk), lambda i, j, k: (layer, j, k)))
    elif dn == "nn" and cut == "row":
        per = r // tk
        b_spec = pl.BlockSpec((None, None, tk, tn), lambda i, j, k: (k // per, layer, k % per, j))
    elif dn == "nn":
        per = c // tn
        b_spec = pl.BlockSpec((None, None, tk, tn), lambda i, j, k: (j // per, layer, k, j % per))
    elif cut == "row":
        per = r // tn
        b_spec = pl.BlockSpec((None, None, tn, tk), lambda i, j, k: (j // per, layer, j % per, k))
    else:
        per = c // tk
        b_spec = pl.BlockSpec((None, None, tn, tk), lambda i, j, k: (k // per, layer, j, k % per))
    r_spec = pl.BlockSpec((tm, tn), lambda i, j, k: (i, j))
    if o_lay is None:
        o_spec = r_spec
        out_shape = jax.ShapeDtypeStruct((out_rows or m_dim, n_dim), out_dtype)
    elif o_cut == "row":
        per_o = unit_m // tm
        o_spec = pl.BlockSpec((None, None, tm, tn), lambda i, j, k: (i // per_o, o_layer, i % per_o, j))
        out_shape = jax.ShapeDtypeStruct((4, o_layers, unit_m, n_dim), out_dtype)
    else:
        per_o = unit_n // tn
        o_spec = pl.BlockSpec((None, None, tm, tn), lambda i, j, k: (j // per_o, o_layer, i, j % per_o))
        out_shape = jax.ShapeDtypeStruct((4, o_layers, m_dim, unit_n), out_dtype)
    has_res = res is not None
    has_buf = o_buf is not None

    def body(*refs):
        a_ref, b_ref = refs[:2]
        r_ref = refs[2] if has_res else None
        o_ref = refs[-1] if nk == 1 else refs[-2]

        def finish(v):
            if has_res:
                v = v + r_ref[...]
            o_ref[...] = v.astype(o_ref.dtype)

        if nk == 1:
            finish(_dot(a_ref[...], b_ref[...], dn))
            return
        acc_ref = refs[-1]
        k = pl.program_id(2)

        @pl.when(k == 0)
        def _():
            acc_ref[...] = jnp.zeros_like(acc_ref)

        acc_ref[...] += _dot(a_ref[...], b_ref[...], dn)

        @pl.when(k == nk - 1)
        def _():
            finish(acc_ref[...])

    args = [a, b] + ([res] if has_res else []) + ([o_buf] if has_buf else [])
    outs, rode = _pcall(
        body, grid=(m_dim // tm, n_dim // tn, nk),
        in_specs=[a_spec, b_spec] + ([r_spec] if has_res else []) + ([_ANY] if has_buf else []),
        out_specs=[o_spec], out_shape=[out_shape],
        scratch_shapes=[] if nk == 1 else [pltpu.VMEM((tm, tn), F32)],
        aliases={len(args) - 1: 0} if has_buf else {},
        args=args, sem=("parallel", "parallel", "arbitrary"), name=name, rider=rider)
    return (outs[0], rode) if rider is not None else outs[0]


def _rowcall(fn, rows, consts, outs, accs, *, name, tm=512):
    args = list(rows) + list(consts)
    in_specs = [pl.BlockSpec((tm, r.shape[1]), lambda i: (i, 0)) for r in rows]
    in_specs += [pl.BlockSpec(c.shape, lambda i: (0, 0)) for c in consts]
    s_dim = args[0].shape[0]
    n_in, n_out = len(args), len(outs)
    out_shape = [jax.ShapeDtypeStruct((s_dim, w), dt) for w, dt in outs] + [jax.ShapeDtypeStruct(s, F32) for s in accs]
    out_specs = [pl.BlockSpec((tm, w), lambda i: (i, 0)) for w, _ in outs] + [pl.BlockSpec(s, lambda i: (0, 0)) for s in accs]

    def body(*refs):
        vals = fn(*[r[...] for r in refs[:n_in]])
        o_refs = refs[n_in:n_in + n_out]
        a_refs = refs[n_in + n_out:]
        for o, v in zip(o_refs, vals[:n_out]):
            o[...] = v.astype(o.dtype)
        if a_refs:
            @pl.when(pl.program_id(0) == 0)
            def _():
                for a_ref in a_refs:
                    a_ref[...] = jnp.zeros_like(a_ref)

            for a_ref, v in zip(a_refs, vals[n_out:]):
                a_ref[...] += v

    return pl.pallas_call(
        body, grid=(s_dim // tm,), in_specs=in_specs, out_specs=out_specs, out_shape=out_shape,
        compiler_params=_params("arbitrary"), name=name,
    )(*args)


def _rms_fwd(h, w, name):
    return _rowcall(lambda x, w_: (_rms(x, w_, NORM_EPS),), [h], [w], [(h.shape[1], BF16)], [], name=name)[0]


def _rms_bwd(h, w, dy, dres, name):
    def fn(x, dy_, dres_, w_):
        _, vjp = jax.vjp(lambda a, b: _rms(a, b, NORM_EPS), x, w_)
        dx, dw = vjp(dy_)
        return dx + dres_, dw

    return _rowcall(fn, [h, dy, dres], [w], [(h.shape[1], F32)], [w.shape], name=name)


def _ple_fwd(h1, pp, gl, name):
    return _rowcall(lambda a, b, c: (a + b * _sigmoid(c),), [h1, pp, gl], [], [(h1.shape[1], F32)], [], name=name)[0]


def _ple_bwd(dh2, pp, gl, name):
    def fn(d, b, c):
        gate = _sigmoid(c)
        return d * gate, d * b * gate * (1.0 - gate)

    return _rowcall(fn, [dh2, pp, gl], [], [(dh2.shape[1], BF16), (dh2.shape[1], BF16)], [], name=name)


def _loss_bwd(y, target, name):
    width = y.shape[1]

    def fn(a, t):
        d = a - t
        col = jnp.sum(d * d, axis=0, keepdims=True)
        part = col[:, 0:LANES]
        for j in range(1, width // LANES):
            part = part + col[:, j * LANES:(j + 1) * LANES]
        return d * (1.0 / width), part

    return _rowcall(fn, [y, target], [], [(width, F32)], [(1, LANES)], name=name)


CONV_TC = 256


def _shift_down(x, j):
    if j == 0:
        return x
    row = lax.broadcasted_iota(jnp.int32, x.shape, 0)
    return jnp.where(row >= j, pltpu.roll(x, j, 0), 0.0)


def _shift_up(x, j):
    if j == 0:
        return x
    n = x.shape[0]
    row = lax.broadcasted_iota(jnp.int32, x.shape, 0)
    return jnp.where(row < n - j, pltpu.roll(x, n - j, 0), 0.0)


def _conv_fwd(pzx, cw, cb, name, rider=None):
    s_dim = pzx.shape[0]
    off = SSD_D_INNER // CONV_TC

    def body(x_ref, w_ref, b_ref, o_ref):
        x = x_ref[...]
        w = w_ref[...]
        y = b_ref[...] + w[3:4, :] * x
        for k in range(SSD_D_CONV - 1):
            y = y + w[k:k + 1, :] * _shift_down(x, SSD_D_CONV - 1 - k)
        o_ref[...] = y * _sigmoid(y)

    outs, rode = _pcall(
        body, grid=(SSD_CONV_DIM // CONV_TC,),
        in_specs=[pl.BlockSpec((s_dim, CONV_TC), lambda j: (0, off + j)), pl.BlockSpec((SSD_D_CONV, CONV_TC), lambda j: (0, j)),
                  pl.BlockSpec((1, CONV_TC), lambda j: (0, j))],
        out_specs=[pl.BlockSpec((s_dim, CONV_TC), lambda j: (0, j))],
        out_shape=[jax.ShapeDtypeStruct((s_dim, SSD_CONV_DIM), F32)],
        args=[pzx, cw, cb], sem=("parallel",), name=name, rider=rider)
    return (outs[0], rode) if rider is not None else outs[0]


def _conv_bwd(pzx, cw, cb, dxs, dbm, dcm, dzx, name):
    s_dim = pzx.shape[0]
    off = SSD_D_INNER // CONV_TC
    n_x, n_b = dxs.shape[1] // CONV_TC, dbm.shape[1] // CONV_TC

    def body(x_ref, w_ref, b_ref, dxs_ref, dbm_ref, dcm_ref, _, dx_ref, dw_ref, db_ref):
        j = pl.program_id(0)
        d = jnp.where(j < n_x, dxs_ref[...], jnp.where(j < n_x + n_b, dbm_ref[...], dcm_ref[...]))
        x = x_ref[...]
        w = w_ref[...]
        xs = [_shift_down(x, SSD_D_CONV - 1 - k) for k in range(SSD_D_CONV)]
        y = b_ref[...]
        for k in range(SSD_D_CONV):
            y = y + w[k:k + 1, :] * xs[k]
        sg = _sigmoid(y)
        dy = d * (sg * (1.0 + y * (1.0 - sg)))
        dx = w[3:4, :] * dy
        for k in range(SSD_D_CONV - 1):
            dx = dx + w[k:k + 1, :] * _shift_up(dy, SSD_D_CONV - 1 - k)
        dx_ref[...] = dx.astype(dx_ref.dtype)
        for k in range(SSD_D_CONV):
            dw_ref[k:k + 1, :] = jnp.sum(dy * xs[k], axis=0, keepdims=True)
        db_ref[...] = jnp.sum(dy, axis=0, keepdims=True)

    part = lambda lo, n: pl.BlockSpec((s_dim, CONV_TC), lambda j: (0, jnp.clip(j - lo, 0, n - 1)))
    return pl.pallas_call(
        body, grid=(SSD_CONV_DIM // CONV_TC,),
        in_specs=[pl.BlockSpec((s_dim, CONV_TC), lambda j: (0, off + j)), pl.BlockSpec((SSD_D_CONV, CONV_TC), lambda j: (0, j)),
                  pl.BlockSpec((1, CONV_TC), lambda j: (0, j)), part(0, n_x), part(n_x, n_b), part(n_x + n_b, n_b), _ANY],
        out_specs=[pl.BlockSpec((s_dim, CONV_TC), lambda j: (0, off + j)), pl.BlockSpec((SSD_D_CONV, CONV_TC), lambda j: (0, j)),
                   pl.BlockSpec((1, CONV_TC), lambda j: (0, j))],
        out_shape=[jax.ShapeDtypeStruct(dzx.shape, dzx.dtype), jax.ShapeDtypeStruct((SSD_D_CONV, SSD_CONV_DIM), F32),
                   jax.ShapeDtypeStruct((1, SSD_CONV_DIM), F32)],
        input_output_aliases={6: 0}, compiler_params=_params("arbitrary"), name=name,
    )(pzx, cw, cb, dxs, dbm, dcm, dzx)


def _ssd_step(xs, bm, cm, dtraw, bias, alog, dskip, st_in, z, gw, dot, cumsum):
    n = xs.shape[0]
    lane = lax.broadcasted_iota(jnp.int32, (1, LANES), 1)
    sub = lax.broadcasted_iota(jnp.int32, (LANES, 1), 0)
    left = (lane < 64).astype(F32)
    right = 1.0 - left
    top = (sub < 64).astype(F32)
    bot = 1.0 - top
    row = lax.broadcasted_iota(jnp.int32, (n, n), 0)
    colm = lax.broadcasted_iota(jnp.int32, (n, n), 1)
    causal = row >= colm

    dt = _softplus(dtraw + bias)
    adt = dt * (-jnp.exp(alog))
    acum = cumsum(adt)
    acum_t = acum.T
    last = jnp.sum(adt, axis=0, keepdims=True)
    scores = dot(cm, bm, "nt")

    def lane_of(v, h):
        return jnp.sum(v * (lane == h).astype(F32), axis=1, keepdims=True)

    ys, sts = [], []
    for pr in range(4):
        heads = (2 * pr, 2 * pr + 1)
        ac = [lane_of(acum, h) for h in heads]
        ar = [jnp.sum(acum_t * (sub == h).astype(F32), axis=0, keepdims=True) for h in heads]
        dth = [lane_of(dt, h) for h in heads]
        la = [lane_of(last, h) for h in heads]
        dk = [lane_of(dskip, h) for h in heads]
        x2 = xs[:, pr * LANES:(pr + 1) * LANES]
        xdt = x2 * (dth[0] * left + dth[1] * right)
        yd = None
        for i, side in enumerate((left, right)):
            decay = jnp.where(causal, jnp.exp(jnp.minimum(ac[i] - ar[i], 0.0)), 0.0)
            t = dot(scores * decay, xdt * side, "nn")
            yd = t if yd is None else yd + t
        st2 = st_in[pr * LANES:(pr + 1) * LANES, :]
        yo = dot(cm, st2, "nt") * (jnp.exp(ac[0]) * left + jnp.exp(ac[1]) * right)
        dte = jnp.exp(la[0] - ac[0]) * left + jnp.exp(la[1] - ac[1]) * right
        cs = dot(xdt * dte, bm, "tn")
        sts.append(st2 * (jnp.exp(la[0]) * top + jnp.exp(la[1]) * bot) + cs)
        ys.append(yd + yo + (dk[0] * left + dk[1] * right) * x2)
    y = jnp.concatenate(ys, axis=1)
    yg = y * (z * _sigmoid(z))
    yn = yg * lax.rsqrt(jnp.mean(yg * yg, axis=-1, keepdims=True) + GATED_NORM_EPS) * gw
    return yn, jnp.concatenate(sts, axis=0)


SSD_GB_FWD, SSD_GB_BWD = 4, 2


def _ssd_specs(n_chunks, rev, gb):
    ci = (lambda c: n_chunks - 1 - c) if rev else (lambda c: c)
    n_x = SSD_D_INNER // (gb * LANES)
    n_g = SSD_N_GROUPS // gb
    return dict(
        xs=pl.BlockSpec((SSD_CHUNK, gb * SSD_GROUP_W), lambda g, c: (ci(c), g)),
        bm=pl.BlockSpec((SSD_CHUNK, gb * LANES), lambda g, c: (ci(c), n_x + g)),
        cm=pl.BlockSpec((SSD_CHUNK, gb * LANES), lambda g, c: (ci(c), n_x + n_g + g)),
        dt=pl.BlockSpec((gb, SSD_CHUNK, LANES), lambda g, c: (g, ci(c), 0)),
        vec=pl.BlockSpec((gb, 1, LANES), lambda g, c: (g, 0, 0)),
        z=pl.BlockSpec((SSD_CHUNK, gb * SSD_GROUP_W), lambda g, c: (ci(c), g)),
        gw=pl.BlockSpec((1, gb * SSD_GROUP_W), lambda g, c: (0, g)),
        st=pl.BlockSpec((gb, None, SSD_GROUP_W, SSD_D_STATE), lambda g, c: (g, ci(c), 0, 0)),
    )


def _group_cols(k, width):
    return slice(k * width, (k + 1) * width)


def _ssd_fwd(act, dtg, bias, alog, dskip, pzx, gw, name, rider=None):
    s_dim = act.shape[0]
    n_chunks = s_dim // SSD_CHUNK
    gb = SSD_GB_FWD
    sp = _ssd_specs(n_chunks, False, gb)

    def body(xs, bm, cm, dt, b_ref, a_ref, d_ref, z, gw_ref, yn_ref, st_ref, state):
        @pl.when(pl.program_id(1) == 0)
        def _():
            state[...] = jnp.zeros_like(state)

        for k in range(gb):
            wide, lanes = _group_cols(k, SSD_GROUP_W), _group_cols(k, LANES)
            st_in = state[k]
            st_ref[k] = st_in
            yn, st_out = _ssd_step(xs[:, wide], bm[:, lanes], cm[:, lanes], dt[k], b_ref[k], a_ref[k], d_ref[k], st_in, z[:, wide],
                                   gw_ref[:, wide], _dot, _cumsum_rows_raw)
            yn_ref[:, wide] = yn.astype(yn_ref.dtype)
            state[k] = st_out

    outs, rode = _pcall(
        body, grid=(SSD_N_GROUPS // gb, n_chunks),
        in_specs=[sp["xs"], sp["bm"], sp["cm"], sp["dt"], sp["vec"], sp["vec"], sp["vec"], sp["z"], sp["gw"]],
        out_specs=[sp["xs"], sp["st"]],
        out_shape=[jax.ShapeDtypeStruct((s_dim, SSD_D_INNER), BF16),
                   jax.ShapeDtypeStruct((SSD_N_GROUPS, n_chunks, SSD_GROUP_W, SSD_D_STATE), F32)],
        scratch_shapes=[pltpu.VMEM((gb, SSD_GROUP_W, SSD_D_STATE), F32)],
        args=[act, act, act, dtg, bias, alog, dskip, pzx, gw], sem=("parallel", "arbitrary"), name=name, rider=rider)
    return (outs, rode) if rider is not None else outs


def _ssd_bwd(act, dtg, bias, alog, dskip, pzx, gw, states, dyn, name, rider=None):
    s_dim = act.shape[0]
    n_chunks = s_dim // SSD_CHUNK
    gb = SSD_GB_BWD
    sp = _ssd_specs(n_chunks, True, gb)
    rc = lambda c: n_chunks - 1 - c

    def body(xs, bm, cm, dt, b_ref, a_ref, d_ref, z, gw_ref, st_ref, dyn_ref,
             dxs_ref, dbm_ref, dcm_ref, ddt_ref, db_ref, da_ref, dd_ref, dz_ref, dgw_ref, dstate):
        first = pl.program_id(1) == 0

        @pl.when(first)
        def _():
            dstate[...] = jnp.zeros_like(dstate)
            db_ref[...] = jnp.zeros_like(db_ref)
            da_ref[...] = jnp.zeros_like(da_ref)
            dd_ref[...] = jnp.zeros_like(dd_ref)
            dgw_ref[...] = jnp.zeros_like(dgw_ref)

        fn = functools.partial(_ssd_step, dot=_gdot, cumsum=_cumsum_rows)
        for k in range(gb):
            wide, lanes = _group_cols(k, SSD_GROUP_W), _group_cols(k, LANES)
            _, vjp = jax.vjp(fn, xs[:, wide], bm[:, lanes], cm[:, lanes], dt[k], b_ref[k], a_ref[k], d_ref[k], st_ref[k], z[:, wide],
                             gw_ref[:, wide])
            dxs, dbm, dcm, ddt, db, da, dd, dst, dz, dgw = vjp((dyn_ref[:, wide], dstate[k]))
            dxs_ref[:, wide] = dxs
            dbm_ref[:, lanes] = dbm
            dcm_ref[:, lanes] = dcm
            ddt_ref[k] = ddt
            dz_ref[:, wide] = dz.astype(dz_ref.dtype)
            db_ref[k] += db
            da_ref[k] += da
            dd_ref[k] += dd
            dgw_ref[:, wide] += dgw
            dstate[k] = dst

    bc = pl.BlockSpec((SSD_CHUNK, gb * LANES), lambda g, c: (rc(c), g))
    outs, rode = _pcall(
        body, grid=(SSD_N_GROUPS // gb, n_chunks),
        in_specs=[sp["xs"], sp["bm"], sp["cm"], sp["dt"], sp["vec"], sp["vec"], sp["vec"], sp["z"], sp["gw"], sp["st"], sp["xs"]],
        out_specs=[sp["xs"], bc, bc, sp["dt"], sp["vec"], sp["vec"], sp["vec"], sp["xs"], sp["gw"]],
        out_shape=[jax.ShapeDtypeStruct((s_dim, SSD_D_INNER), F32),
                   jax.ShapeDtypeStruct((s_dim, SSD_N_GROUPS * SSD_D_STATE), F32),
                   jax.ShapeDtypeStruct((s_dim, SSD_N_GROUPS * SSD_D_STATE), F32),
                   jax.ShapeDtypeStruct((SSD_N_GROUPS, s_dim, LANES), F32),
                   jax.ShapeDtypeStruct((SSD_N_GROUPS, 1, LANES), F32),
                   jax.ShapeDtypeStruct((SSD_N_GROUPS, 1, LANES), F32),
                   jax.ShapeDtypeStruct((SSD_N_GROUPS, 1, LANES), F32),
                   jax.ShapeDtypeStruct((s_dim, SSD_ZX), BF16),
                   jax.ShapeDtypeStruct((1, SSD_D_INNER), F32)],
        scratch_shapes=[pltpu.VMEM((gb, SSD_GROUP_W, SSD_D_STATE), F32)],
        args=[act, act, act, dtg, bias, alog, dskip, pzx, gw, states, dyn], sem=("arbitrary", "arbitrary"), name=name, rider=rider)
    return (outs, rode) if rider is not None else outs


SB_T = 128
SB_GROUP = 8
SB_WIDE = SB_GROUP * SB_T
SB_HB = 4
SB_SCALE = 1.0 / math.sqrt(SB_HEAD_DIM)


def _qknorm_fwd(proj, qw, kw, name, tm=512):
    s_dim = proj.shape[0]

    def body(q_ref, k_ref, v_ref, qw_ref, kw_ref, qo, ko, vo):
        for hh in range(SB_HB):
            qo[:, _head_lanes(hh)] = _rms(q_ref[:, _head_lanes(hh)], qw_ref[...], NORM_EPS).astype(BF16)
            ko[:, _head_lanes(hh)] = _rms(k_ref[:, _head_lanes(hh)], kw_ref[...], NORM_EPS).astype(BF16)
        vo[...] = v_ref[...].astype(BF16)

    groups = SB_N_HEADS // SB_HB
    blk = lambda o: pl.BlockSpec((tm, SB_HB * SB_HEAD_DIM), lambda i, h: (i, o + h))
    vec = pl.BlockSpec((1, SB_HEAD_DIM), lambda i, h: (0, 0))
    return pl.pallas_call(
        body, grid=(s_dim // tm, groups),
        in_specs=[blk(0), blk(groups), blk(2 * groups), vec, vec],
        out_specs=[blk(0)] * 3,
        out_shape=[jax.ShapeDtypeStruct((s_dim, SB_WIDTH), BF16)] * 3,
        compiler_params=_params("parallel", "parallel"), name=name,
    )(proj, proj, proj, qw, kw)


def _qknorm_bwd(proj, qw, kw, dqn, dkn, name, tm=512):
    s_dim = proj.shape[0]

    def body(q_ref, k_ref, dq_ref, dk_ref, qw_ref, kw_ref, dqo, dko, dqw, dkw):
        @pl.when((pl.program_id(0) == 0) & (pl.program_id(1) == 0))
        def _():
            dqw[...] = jnp.zeros_like(dqw)
            dkw[...] = jnp.zeros_like(dkw)

        fn = lambda a, b: _rms(a, b, NORM_EPS)
        for hh in range(SB_HB):
            lanes = _head_lanes(hh)
            for x_ref, w_ref, d_ref, dx_out, dw_out in ((q_ref, qw_ref, dq_ref, dqo, dqw), (k_ref, kw_ref, dk_ref, dko, dkw)):
                _, vjp = jax.vjp(fn, x_ref[:, lanes], w_ref[...])
                dx, dw = vjp(d_ref[:, lanes])
                dx_out[:, lanes] = dx.astype(BF16)
                dw_out[...] += dw

    groups = SB_N_HEADS // SB_HB
    blk = lambda o: pl.BlockSpec((tm, SB_HB * SB_HEAD_DIM), lambda i, h: (i, o + h))
    vec = pl.BlockSpec((1, SB_HEAD_DIM), lambda i, h: (0, 0))
    return pl.pallas_call(
        body, grid=(s_dim // tm, groups),
        in_specs=[blk(0), blk(groups), blk(0), blk(0), vec, vec],
        out_specs=[blk(0), blk(0), vec, vec],
        out_shape=[jax.ShapeDtypeStruct((s_dim, SB_WIDTH), BF16)] * 2 + [jax.ShapeDtypeStruct((1, SB_HEAD_DIM), F32)] * 2,
        compiler_params=_params("arbitrary", "arbitrary"), name=name,
    )(proj, proj, dqn, dkn, qw, kw)


def _sb_logits(q, k, strict):
    z = _dot(q, k, "nt") * SB_SCALE
    lb = jnp.minimum(z, 0.0) - jnp.log(1.0 + jnp.exp(-jnp.abs(z)))
    lm = lb - z
    if strict is not None:
        lm = jnp.where(strict, lm, 0.0)
    return lb, lm


def _sb_strict(qi, grp):
    r = lax.broadcasted_iota(jnp.int32, (SB_T, SB_WIDE), 0) + qi * SB_T
    c = lax.broadcasted_iota(jnp.int32, (SB_T, SB_WIDE), 1) + grp * SB_WIDE
    return c < r


def _head_lanes(hh):
    return slice(hh * SB_HEAD_DIM, (hh + 1) * SB_HEAD_DIM)


def _sb_fwd(qn, kn, vb, proj, name, rider=None):
    s_dim = qn.shape[0]
    nq = s_dim // SB_T
    assert nq % SB_GROUP == 0

    def body(q_ref, k_ref, v_ref, g_ref, og_ref, o_ref, t_ref):
        qi = pl.program_id(1)
        top = qi // SB_GROUP
        after = _tri(SB_T, True, strict=True)
        qs = [q_ref[:, _head_lanes(hh)] for hh in range(SB_HB)]

        def step(grp, masked, carries):
            start = pl.multiple_of(grp * SB_WIDE, SB_WIDE)
            strict = _sb_strict(qi, grp) if masked else None
            out = []
            for hh in range(SB_HB):
                o_acc, cr = carries[hh]
                k = k_ref[pl.ds(start, SB_WIDE), _head_lanes(hh)]
                v = v_ref[pl.ds(start, SB_WIDE), _head_lanes(hh)]
                lb, lm = _sb_logits(qs[hh], k, strict)
                rest = [None] * SB_GROUP
                for t in reversed(range(SB_GROUP)):
                    lm_t = lm[:, t * SB_T:(t + 1) * SB_T]
                    rest[t] = cr + _split_dot(lm_t, after, 2, True)
                    cr = cr + jnp.sum(lm_t, axis=1, keepdims=True)
                a = jnp.exp(lb + jnp.concatenate(rest, axis=1))
                if masked:
                    a = jnp.where(strict, a, 0.0)
                out.append((o_acc + _dot(a, v), cr))
            return tuple(out)

        init = tuple((jnp.zeros((SB_T, SB_HEAD_DIM), F32), jnp.zeros((SB_T, 1), F32)) for _ in range(SB_HB))
        carries = step(top, True, init)
        carries = lax.fori_loop(0, top, lambda i, c: step(top - 1 - i, False, c), carries)
        for hh in range(SB_HB):
            o, tot = carries[hh]
            g = g_ref[:, _head_lanes(hh)]
            o_ref[:, _head_lanes(hh)] = o
            og_ref[:, _head_lanes(hh)] = (o * (g * _sigmoid(g))).astype(og_ref.dtype)
            t_ref[hh] = jnp.broadcast_to(tot, (SB_T, LANES))

    wide = SB_HB * SB_HEAD_DIM
    qb = pl.BlockSpec((SB_T, wide), lambda h, i: (i, h))
    kv = pl.BlockSpec((s_dim, wide), lambda h, i: (0, h))
    outs, rode = _pcall(
        body, grid=(SB_N_HEADS // SB_HB, nq),
        in_specs=[qb, kv, kv, pl.BlockSpec((SB_T, wide), lambda h, i: (i, 3 * SB_N_HEADS // SB_HB + h))],
        out_specs=[qb, qb, pl.BlockSpec((SB_HB, SB_T, LANES), lambda h, i: (h, i, 0))],
        out_shape=[jax.ShapeDtypeStruct((s_dim, SB_WIDTH), BF16), jax.ShapeDtypeStruct((s_dim, SB_WIDTH), F32),
                   jax.ShapeDtypeStruct((SB_N_HEADS, s_dim, LANES), F32)],
        args=[qn, kn, vb, proj], sem=("parallel", "arbitrary"), name=name, rider=rider)
    return (outs, rode) if rider is not None else outs


def _sb_bwd(qn, kn, vb, proj, o, tot, dog, name, rider=None):
    s_dim = qn.shape[0]
    nq = s_dim // SB_T
    assert nq % SB_GROUP == 0

    def body(q_ref, k_ref, v_ref, g_ref, o_ref, t_ref, dog_ref, dq_ref, dk_ref, dv_ref, dvb_ref, dg_ref):
        qi = pl.program_id(1)
        top = qi // SB_GROUP

        @pl.when(qi == 0)
        def _():
            dk_ref[...] = jnp.zeros_like(dk_ref)
            dv_ref[...] = jnp.zeros_like(dv_ref)

        after = _tri(SB_T, True, strict=True)
        before = _tri(SB_T, False, strict=True)
        qs, dos, totals = [], [], []
        for hh in range(SB_HB):
            g = g_ref[:, _head_lanes(hh)]
            sg = _sigmoid(g)
            dog_v = dog_ref[:, _head_lanes(hh)]
            dg_ref[:, _head_lanes(hh)] = (dog_v * o_ref[:, _head_lanes(hh)] * (sg * (1.0 + g * (1.0 - sg)))).astype(dg_ref.dtype)
            dos.append((dog_v * (g * sg)).astype(BF16))
            qs.append(q_ref[:, _head_lanes(hh)])
            totals.append(t_ref[hh][:, 0:1])

        def step(grp, masked, carries):
            start = pl.multiple_of(grp * SB_WIDE, SB_WIDE)
            strict = _sb_strict(qi, grp) if masked else None
            out = []
            for hh in range(SB_HB):
                dq_acc, cp, ce = carries[hh]
                q, do = qs[hh], dos[hh]
                k = k_ref[pl.ds(start, SB_WIDE), _head_lanes(hh)]
                v = v_ref[pl.ds(start, SB_WIDE), _head_lanes(hh)]
                lb, lm = _sb_logits(q, k, strict)
                rest = []
                for t in range(SB_GROUP):
                    lm_t = lm[:, t * SB_T:(t + 1) * SB_T]
                    cp = cp + jnp.sum(lm_t, axis=1, keepdims=True)
                    rest.append((totals[hh] - cp) + _split_dot(lm_t, after, 2, True))
                a = jnp.exp(lb + jnp.concatenate(rest, axis=1))
                if masked:
                    a = jnp.where(strict, a, 0.0)
                e = a * _dot(do, v, "nt")
                excl = []
                for t in range(SB_GROUP):
                    e_t = e[:, t * SB_T:(t + 1) * SB_T]
                    excl.append(ce + _split_dot(e_t, before, 1, True))
                    ce = ce + jnp.sum(e_t, axis=1, keepdims=True)
                eex = jnp.concatenate(excl, axis=1)
                if masked:
                    eex = jnp.where(strict, eex, 0.0)
                sig = jnp.exp(lb)
                dz = (e * (1.0 - sig) - eex * sig) * SB_SCALE
                dv_ref[pl.ds(start, SB_WIDE), _head_lanes(hh)] += _dot(a, do, "tn")
                dk_ref[pl.ds(start, SB_WIDE), _head_lanes(hh)] += _dot(dz, q, "tn")
                out.append((dq_acc + _dot(dz, k), cp, ce))
            return tuple(out)

        zero = jnp.zeros((SB_T, 1), F32)
        init = tuple((jnp.zeros((SB_T, SB_HEAD_DIM), F32), zero, zero) for _ in range(SB_HB))
        carries = lax.fori_loop(0, top, lambda i, c: step(i, False, c), init)
        carries = step(top, True, carries)
        for hh in range(SB_HB):
            dq_ref[:, _head_lanes(hh)] = carries[hh][0]

        @pl.when(qi == nq - 1)
        def _():
            dvb_ref[...] = dv_ref[...].astype(BF16)

    wide = SB_HB * SB_HEAD_DIM
    qb = pl.BlockSpec((SB_T, wide), lambda h, i: (i, h))
    kv = pl.BlockSpec((s_dim, wide), lambda h, i: (0, h))
    outs, rode = _pcall(
        body, grid=(SB_N_HEADS // SB_HB, nq),
        in_specs=[qb, kv, kv, pl.BlockSpec((SB_T, wide), lambda h, i: (i, 3 * SB_N_HEADS // SB_HB + h)), qb,
                  pl.BlockSpec((SB_HB, SB_T, LANES), lambda h, i: (h, i, 0)), qb],
        out_specs=[qb, kv, kv, kv, qb],
        out_shape=[jax.ShapeDtypeStruct((s_dim, SB_WIDTH), F32), jax.ShapeDtypeStruct((s_dim, SB_WIDTH), F32),
                   jax.ShapeDtypeStruct((s_dim, SB_WIDTH), F32), jax.ShapeDtypeStruct((s_dim, SB_WIDTH), BF16),
                   jax.ShapeDtypeStruct((s_dim, SB_WIDTH), BF16)],
        args=[qn, kn, vb, proj, o, tot, dog], sem=("parallel", "arbitrary"), name=name, rider=rider)
    return (outs, rode) if rider is not None else outs


def _adamw_math(w, g, m, v):
    m = ADAM_B1 * m + (1.0 - ADAM_B1) * g
    v = ADAM_B2 * v + (1.0 - ADAM_B2) * (g * g)
    m_hat = m / (1.0 - ADAM_B1 ** ADAM_STEP)
    v_hat = v / (1.0 - ADAM_B2 ** ADAM_STEP)
    delta = -ADAM_LR * (m_hat / (jnp.sqrt(v_hat) + ADAM_EPS) + ADAM_WD * w)
    return delta, m, v


def _row_block(rows, cols, itemsize=4, limit=1 << 20):
    tr = rows
    while tr * cols * itemsize > limit and tr % (2 * BF16_ROWS) == 0:
        tr //= 2
    return tr


def _divisor_block(rows, cols, itemsize=4, limit=2 << 20):
    best = BF16_ROWS
    for t in range(BF16_ROWS, rows + 1, BF16_ROWS):
        if rows % t == 0 and t * cols * itemsize <= limit:
            best = t
    return best


def _adamw(w, g, m, v, name, rider=None):
    n, rows, cols = w.shape
    tr = rows if rows * cols * 4 <= (2 << 20) else _divisor_block(rows, cols)

    def body(w_ref, g_ref, m_ref, v_ref, d_out, m_out, v_out):
        d, m_new, v_new = _adamw_math(w_ref[...], g_ref[...], m_ref[...], v_ref[...])
        d_out[...] = d
        m_out[...] = m_new
        v_out[...] = v_new

    blk = pl.BlockSpec((None, tr, cols), lambda i, j: (i, j, 0))
    outs, rode = _pcall(
        body, grid=(n, rows // tr), in_specs=[blk] * 4, out_specs=[blk] * 3,
        out_shape=[jax.ShapeDtypeStruct(w.shape, F32)] * 3,
        args=[w, g, m, v], sem=("parallel", "parallel"), name=name, rider=rider)
    return (outs, rode) if rider is not None else outs


_FLIPS = ((1, 0), (0, 1), (1, 1))


def _place():
    return lax.axis_index("x"), lax.axis_index("y"), lax.axis_index("c")


def _flip(v, f):
    return 1 - v if f else v


def _half_rows(ref, lead, hc, hr):
    return ref.at[(*lead, pl.ds(pl.multiple_of(hc * hr, BF16_ROWS), hr), slice(None))]


def _half_cols(ref, lead, hc, hw):
    return ref.at[(*lead, pl.ds(pl.multiple_of(hc * hw, LANES), hw))]


def _rows_of_chip(chip, r):
    return pl.ds(pl.multiple_of(chip * r, BF16_ROWS), r)


def _slot_half(gathered, shard_shape, chip, l, hc):
    r, c = shard_shape[1:]
    if len(gathered.shape) == 3:
        return _half_cols(gathered, (l, _rows_of_chip(chip, r)), hc, c // 2)
    return _half_rows(gathered, (chip, l), hc, r // 2)


def _shard_half(shard, stacked, l, hc):
    r, c = shard.shape[1:]
    return _half_cols(shard, (l, slice(None)), hc, c // 2) if stacked else _half_rows(shard, (l,), hc, r // 2)


def _remote(src, dst, send, recv, k, to):
    return pltpu.make_async_remote_copy(src_ref=src, dst_ref=dst, send_sem=send.at[k], recv_sem=recv.at[k], device_id=to,
                                        device_id_type=MESH)


def _comm_call(reads, writes, n_sems, phases, name):
    passed = [k for k, w in enumerate(writes) if not isinstance(w, jax.ShapeDtypeStruct)]
    n_rd = len(reads)

    def body(*refs):
        rd = refs[:n_rd]
        wr = refs[n_rd + len(passed):n_rd + len(passed) + len(writes)]
        send, recv = refs[-2:]
        for phase in phases:
            sends, arrivals = phase(rd, wr, send, recv)
            for cp in sends:
                cp.start()
            for cp in arrivals:
                cp.wait_recv()
            for cp in sends:
                cp.wait_send()

    return pl.pallas_call(
        body, in_specs=[_ANY] * (n_rd + len(passed)), out_specs=[_ANY] * len(writes),
        out_shape=[jax.ShapeDtypeStruct(w.shape, w.dtype) for w in writes],
        input_output_aliases={n_rd + pos: k for pos, k in enumerate(passed)},
        scratch_shapes=[pltpu.SemaphoreType.DMA((n_sems,)), pltpu.SemaphoreType.DMA((n_sems,))], name=name,
    )(*reads, *[writes[k] for k in passed])


def _ag_ici(pieces, names, base=0):
    def phase(shards, gathered, send, recv):
        x, y, c = _place()
        me = 2 * x + y
        sends, arrivals = [], []
        for k, (n, l) in enumerate(pieces):
            a = names.index(n)
            shape = shards[a].shape
            src = _shard_half(shards[a], len(gathered[a].shape) == 3, l, c)
            for j, (fx, fy) in enumerate(_FLIPS):
                tx, ty = _flip(x, fx), _flip(y, fy)
                sends.append(_remote(src, _slot_half(gathered[a], shape, me, l, c), send, recv, base + 3 * k + j, (tx, ty, c)))
                arrivals.append(_remote(src, _slot_half(gathered[a], shape, 2 * tx + ty, l, c), send, recv, base + 3 * k + j, (tx, ty, c)))
        return sends, arrivals

    return phase


def _ag_pass_on(pieces, names, shapes, base=0):
    def phase(_, gathered, send, recv):
        x, y, c = _place()
        sibling = (x, y, 1 - c)
        sends, arrivals = [], []
        for k, (n, l) in enumerate(pieces):
            a = names.index(n)
            for j, (fx, fy) in enumerate(_FLIPS):
                chip = 2 * _flip(x, fx) + _flip(y, fy)
                landed = _slot_half(gathered[a], shapes[a], chip, l, c)
                sends.append(_remote(landed, landed, send, recv, base + 3 * k + j, sibling))
                arrivals.append(_remote(landed, _slot_half(gathered[a], shapes[a], chip, l, 1 - c), send, recv, base + 3 * k + j, sibling))
        return sends, arrivals

    return phase


def _other_half(ref, hc):
    if len(ref.shape) == 3:
        return _half_cols(ref, (slice(None), slice(None)), hc, ref.shape[2] // 2)
    return _half_rows(ref, (slice(None), slice(None)), hc, ref.shape[2] // 2)


def _half_shape(shape):
    return shape[:2] + (shape[2] // 2,) if len(shape) == 3 else shape[:2] + (shape[2] // 2, shape[3])


def _exchange_phase(n_arr):
    def phase(ins, outs, send, recv):
        x, y, c = _place()
        cps = [_remote(_other_half(ins[a], 1 - c), outs[a], send, recv, a, (x, y, 1 - c)) for a in range(n_arr)]
        return cps, cps

    return phase


def _exchange_outs(grads):
    return [jax.ShapeDtypeStruct(_half_shape(g.shape), g.dtype) for g in grads]


def _pair_exchange(grads, name):
    return _comm_call(grads, _exchange_outs(grads), len(grads), [_exchange_phase(len(grads))], name)


def _exchange_rider(grads):
    return _Rider(grads, _exchange_outs(grads), len(grads), _exchange_phase(len(grads)))


def _pair_sum_stacked(g, got, place, name):
    _, rows, hw = got.shape
    tr = _divisor_block(rows, hw)

    def body(place_ref, g_ref, r_ref, o_ref):
        o_ref[...] = (g_ref[...].astype(F32) + r_ref[...].astype(F32)).astype(o_ref.dtype)

    blk = pl.BlockSpec((None, tr, hw), lambda i, pr: (0, i, 0))
    return pl.pallas_call(
        body,
        grid_spec=pltpu.PrefetchScalarGridSpec(
            num_scalar_prefetch=1, grid=(rows // tr,),
            in_specs=[pl.BlockSpec((None, tr, hw), lambda i, pr: (0, i, pr[1])), blk], out_specs=blk),
        out_shape=jax.ShapeDtypeStruct(got.shape, BF16),
        compiler_params=_params("parallel"), name=name,
    )(place, g, got)


def _pair_sum(g, got, place, name):
    if len(g.shape) == 3:
        return _pair_sum_stacked(g, got, place, name)
    _, layers, hr, cols = got.shape
    tr = _row_block(hr, cols, limit=2 << 20)
    per = hr // tr

    def body(place_ref, g_ref, r_ref, o_ref):
        o_ref[...] = (g_ref[...].astype(F32) + r_ref[...].astype(F32)).astype(o_ref.dtype)

    blk = pl.BlockSpec((None, None, tr, cols), lambda k, l, i, pr: (k, l, i, 0))
    return pl.pallas_call(
        body,
        grid_spec=pltpu.PrefetchScalarGridSpec(
            num_scalar_prefetch=1, grid=(4, layers, per),
            in_specs=[pl.BlockSpec((None, None, tr, cols), lambda k, l, i, pr: (k, l, pr[1] * per + i, 0)), blk],
            out_specs=blk),
        out_shape=jax.ShapeDtypeStruct(got.shape, BF16),
        compiler_params=_params("parallel", "parallel", "parallel"), name=name,
    )(place, g, got)


def _scatter_phase(n_arr):
    def phase(ins, outs, send, recv):
        x, y, c = _place()
        cps = []
        for a in range(n_arr):
            for j, (fx, fy) in enumerate(_FLIPS):
                tx, ty = _flip(x, fx), _flip(y, fy)
                if len(ins[a].shape) == 3:
                    src = ins[a].at[:, _rows_of_chip(2 * tx + ty, ins[a].shape[1] // 4), :]
                else:
                    src = ins[a].at[2 * tx + ty]
                cps.append(_remote(src, outs[a].at[j], send, recv, 3 * a + j, (tx, ty, c)))
        return cps, cps

    return phase


def _scatter_outs(pairs):
    return [jax.ShapeDtypeStruct((3, 1, p.shape[1] // 4, p.shape[2]) if len(p.shape) == 3 else (3,) + p.shape[1:], p.dtype) for p in pairs]


def _chip_scatter(pairs, name):
    return _comm_call(pairs, _scatter_outs(pairs), 3 * len(pairs), [_scatter_phase(len(pairs))], name)


def _scatter_rider(pairs):
    return _Rider(pairs, _scatter_outs(pairs), 3 * len(pairs), _scatter_phase(len(pairs)))


def _chip_sum_stacked(p, got, place, layer, layers, o_buf, name, row0=0, rows=None):
    _, r, hw = got.shape[1:]
    rows = rows or r
    tr = _divisor_block(math.gcd(r, row0) if row0 else r, hw)
    per = r // tr
    first = row0 // tr

    def body(place_ref, p_ref, r_ref, *rest):
        o_ref = rest[-1]
        acc = p_ref[...].astype(F32)
        for j in range(3):
            acc = acc + r_ref[j].astype(F32)
        o_ref[...] = acc

    has_buf = o_buf is not None
    return pl.pallas_call(
        body,
        grid_spec=pltpu.PrefetchScalarGridSpec(
            num_scalar_prefetch=1, grid=(per,),
            in_specs=[pl.BlockSpec((None, tr, hw), lambda i, pr: (0, pr[0] * per + i, 0)),
                      pl.BlockSpec((3, None, tr, hw), lambda i, pr: (0, 0, i, 0))] + ([_ANY] if has_buf else []),
            out_specs=pl.BlockSpec((None, tr, hw), lambda i, pr: (layer, first + i, pr[1]))),
        out_shape=jax.ShapeDtypeStruct((layers, rows, 2 * hw), F32),
        input_output_aliases={3: 0} if has_buf else {},
        compiler_params=_params("parallel"), name=name,
    )(*((place, p, got) + ((o_buf,) if has_buf else ())))


def _chip_sum(p, got, place, layer, layers, o_buf, name):
    if len(p.shape) == 3:
        return _chip_sum_stacked(p, got, place, layer, layers, o_buf, name)
    _, _, hr, cols = p.shape
    tr = _row_block(hr, cols, limit=2 << 20)
    per = hr // tr

    def body(place_ref, p_ref, r_ref, *rest):
        o_ref = rest[-1]
        acc = p_ref[...].astype(F32)
        for j in range(3):
            acc = acc + r_ref[j].astype(F32)
        o_ref[...] = acc

    has_buf = o_buf is not None
    return pl.pallas_call(
        body,
        grid_spec=pltpu.PrefetchScalarGridSpec(
            num_scalar_prefetch=1, grid=(per,),
            in_specs=[pl.BlockSpec((None, None, tr, cols), lambda i, pr: (pr[0], 0, i, 0)),
                      pl.BlockSpec((3, None, tr, cols), lambda i, pr: (0, 0, i, 0))] + ([_ANY] if has_buf else []),
            out_specs=pl.BlockSpec((None, tr, cols), lambda i, pr: (layer, pr[1] * per + i, 0))),
        out_shape=jax.ShapeDtypeStruct((layers, 2 * hr, cols), F32),
        input_output_aliases={3: 0} if has_buf else {},
        compiler_params=_params("parallel"), name=name,
    )(*((place, p, got) + ((o_buf,) if has_buf else ())))


def _pair_gather(halves, by_cols, name):
    def phase(_, bufs, send, recv):
        x, y, c = _place()
        sends, arrivals = [], []
        for a, h in enumerate(halves):
            cut = (lambda hc, a=a, h=h: _half_cols(bufs[a], (slice(None), slice(None)), hc, h.shape[2] // 2)) if by_cols[a] else (
                lambda hc, a=a, h=h: _half_rows(bufs[a], (slice(None),), hc, h.shape[1] // 2))
            sends.append(_remote(cut(c), cut(c), send, recv, a, (x, y, 1 - c)))
            arrivals.append(_remote(cut(c), cut(1 - c), send, recv, a, (x, y, 1 - c)))
        return sends, arrivals

    return _comm_call([], halves, len(halves), [phase], name)


def _allreduce_small(v, name):
    rows, cols = v.shape

    def body(v_ref, o_ref, buf, send_sems, recv_sems):
        x, y, c = _place()
        me = 4 * x + 2 * y + c
        buf[0] = v_ref[...]
        cps = []
        for k in range(1, 8):
            kx, ky, kc = (k >> 2) & 1, (k >> 1) & 1, k & 1
            cp = pltpu.make_async_remote_copy(src_ref=v_ref, dst_ref=buf.at[k], send_sem=send_sems.at[k - 1], recv_sem=recv_sems.at[k - 1],
                                              device_id=(_flip(x, kx), _flip(y, ky), _flip(c, kc)), device_id_type=MESH)
            cp.start()
            cps.append(cp)
        for cp in cps:
            cp.wait()
        acc = buf[me]
        for d in range(1, 8):
            acc = acc + buf[jnp.bitwise_xor(d, me)]
        o_ref[...] = acc

    vm = pl.BlockSpec(memory_space=pltpu.VMEM)
    return pl.pallas_call(
        body, in_specs=[vm], out_specs=vm, out_shape=jax.ShapeDtypeStruct((rows, cols), F32),
        scratch_shapes=[pltpu.VMEM((8, rows, cols), F32), pltpu.SemaphoreType.DMA((7,)), pltpu.SemaphoreType.DMA((7,))],
        name=name,
    )(v)


def _pad_lanes(a):
    return jnp.pad(a, ((0, 0), (0, LANES - a.shape[1])))


def _group_lanes(v):
    return jnp.pad(v.reshape(SSD_N_GROUPS, 1, 8), ((0, 0), (0, 0), (0, LANES - 8)))


def kernel(x, p, norm_w, ssd_in_w, ssd_conv_w, ssd_conv_b, ssd_dt_bias, ssd_a_log, ssd_d, ssd_gnorm_w, ssd_out_w, sb_in_w, sb_qn_w, sb_kn_w, sb_out_w, ple_norm_w, ple_gate_w, ple_proj_w, loss_target, m_norm_w, m_ssd_in_w, m_ssd_conv_w, m_ssd_conv_b, m_ssd_dt_bias, m_ssd_a_log, m_ssd_d, m_ssd_gnorm_w, m_ssd_out_w, m_sb_in_w, m_sb_qn_w, m_sb_kn_w, m_sb_out_w, m_ple_norm_w, m_ple_gate_w, m_ple_proj_w, v_norm_w, v_ssd_in_w, v_ssd_conv_w, v_ssd_conv_b, v_ssd_dt_bias, v_ssd_a_log, v_ssd_d, v_ssd_gnorm_w, v_ssd_out_w, v_sb_in_w, v_sb_qn_w, v_sb_kn_w, v_sb_out_w, v_ple_norm_w, v_ple_gate_w, v_ple_proj_w):
    w_in = dict(norm_w=norm_w, ssd_in_w=ssd_in_w, ssd_conv_w=ssd_conv_w, ssd_conv_b=ssd_conv_b, ssd_dt_bias=ssd_dt_bias,
                ssd_a_log=ssd_a_log, ssd_d=ssd_d, ssd_gnorm_w=ssd_gnorm_w, ssd_out_w=ssd_out_w, sb_in_w=sb_in_w, sb_qn_w=sb_qn_w,
                sb_kn_w=sb_kn_w, sb_out_w=sb_out_w, ple_norm_w=ple_norm_w, ple_gate_w=ple_gate_w, ple_proj_w=ple_proj_w)
    m_in = dict(norm_w=m_norm_w, ssd_in_w=m_ssd_in_w, ssd_conv_w=m_ssd_conv_w, ssd_conv_b=m_ssd_conv_b, ssd_dt_bias=m_ssd_dt_bias,
                ssd_a_log=m_ssd_a_log, ssd_d=m_ssd_d, ssd_gnorm_w=m_ssd_gnorm_w, ssd_out_w=m_ssd_out_w, sb_in_w=m_sb_in_w,
                sb_qn_w=m_sb_qn_w, sb_kn_w=m_sb_kn_w, sb_out_w=m_sb_out_w, ple_norm_w=m_ple_norm_w, ple_gate_w=m_ple_gate_w,
                ple_proj_w=m_ple_proj_w)
    v_in = dict(norm_w=v_norm_w, ssd_in_w=v_ssd_in_w, ssd_conv_w=v_ssd_conv_w, ssd_conv_b=v_ssd_conv_b, ssd_dt_bias=v_ssd_dt_bias,
                ssd_a_log=v_ssd_a_log, ssd_d=v_ssd_d, ssd_gnorm_w=v_ssd_gnorm_w, ssd_out_w=v_ssd_out_w, sb_in_w=v_sb_in_w,
                sb_qn_w=v_sb_qn_w, sb_kn_w=v_sb_kn_w, sb_out_w=v_sb_out_w, ple_norm_w=v_ple_norm_w, ple_gate_w=v_ple_gate_w,
                ple_proj_w=v_ple_proj_w)
    ix, iy, ic = lax.axis_index("x"), lax.axis_index("y"), lax.axis_index("c")
    chip = (2 * ix + iy).astype(jnp.int32)
    place = jnp.stack([chip, ic.astype(jnp.int32)])
    zero = jnp.zeros((), jnp.int32)
    big_names = [n for n, _, _ in _BIG]
    layers_of = {n: s[0] for n, s, _ in _BIG}
    cut_of = {n: cut for n, _, cut in _BIG}

    def layer_pieces(i):
        mixer = ("ssd_in_w", "ssd_out_w") if i % 2 == 0 else ("sb_in_w", "sb_out_w")
        return [(mixer[0], i // 2), (mixer[1], i // 2), ("ple_gate_w", i), ("ple_proj_w", i)]

    def names_of(pieces):
        return [n for n in big_names if any(n == q for q, _ in pieces)]

    held = lambda n, a: a.transpose(0, 2, 1) if cut_of[n] == "stack" else a
    mine = {n: held(n, w_in[n]).astype(BF16) for n in big_names}
    shard_shapes = [mine[n].shape for n in big_names]
    room = [jax.ShapeDtypeStruct((s[0], 4 * s[1], s[2]) if cut_of[n] == "stack" else (4,) + s, BF16) for n, s in zip(big_names, shard_shapes)]
    first = layer_pieces(0)[:1]
    gathered = _comm_call([mine[n] for n in big_names], room, 6 * len(first),
                          [_ag_ici(first, big_names), _ag_pass_on(first, big_names, shard_shapes, base=3 * len(first))], "allgather_layer0")
    gw = {}
    for n, g in zip(big_names, gathered):
        if cut_of[n] == "stack":
            layers, r, c = mine[n].shape
            gw[n] = lax.dynamic_update_slice(g.reshape(layers, 4, r, c), mine[n][:, None], (zero, chip, zero, zero)).reshape(g.shape)
        else:
            gw[n] = lax.dynamic_update_slice(g, mine[n][None], (chip, zero, zero, zero))

    lp = [layer_pieces(i) for i in range(DEPTH)]
    carries = {
        "ssd_in_0": (lp[0][1:2], []), "conv_0": (lp[0][2:], lp[0][1:2]), "ssd_0": (lp[1][:1], lp[0][2:]),
        "ssd_out_0": (lp[1][1:2], lp[1][:1]), "sb_in_1": (lp[1][2:], lp[1][1:2]), "sb_1": (lp[2][:2], lp[1][2:]),
        "sb_out_1": (lp[2][2:], lp[2][:2]), "ssd_in_2": (lp[3][1:], lp[2][2:]), "ssd_2": (lp[3][:1], lp[3][1:]),
        "ssd_out_2": ([], lp[3][:1]),
    }

    def gather_rider(call):
        if call not in carries:
            return None, lambda outs: outs
        ici, passing = carries[call]
        names = names_of(ici + passing)
        phases = ([_ag_ici(ici, names)] if ici else []) + (
            [_ag_pass_on(passing, names, [mine[n].shape for n in names], base=3 * len(ici))] if passing else [])

        def issue(rd, wr, send, recv):
            both = [ph(rd, wr, send, recv) for ph in phases]
            return sum((b[0] for b in both), []), sum((b[1] for b in both), [])

        def land(outs):
            outs, bufs = outs
            for n, g in zip(names, bufs):
                gw[n] = g
            return outs

        return _Rider([mine[n] for n in names], [gw[n] for n in names], 3 * (len(ici) + len(passing)), issue), land

    onehot = (jnp.arange(4) == chip).astype(F32) * (ic == 0).astype(F32)
    cw_mine = onehot[:, None, None, None] * ssd_conv_w[None]
    cw_full = _allreduce_small(cw_mine.transpose(1, 2, 0, 3).reshape(-1, LANES), "gather_conv_w").reshape(2, SSD_D_CONV, SSD_CONV_DIM)

    def wmm(a, name, layer, *, dn="nn", res=None, call, rider=None):
        return _matmul(a, gw[name], dn=dn, res=res, b_lay=(cut_of[name], layer), name=call, rider=rider)

    h = x[0]
    target = loss_target[0]
    saved = []
    for i in range(DEPTH):
        j = i // 2
        nw = norm_w[i:i + 1]
        pw = ple_norm_w[i:i + 1]
        s = dict(h=h)
        u = _rms_fwd(h, nw, f"rms_{i}")
        s["u"] = u
        if i % 2 == 0:
            w_dt = jnp.pad(gw["ssd_in_w"][j, SSD_ZX:], ((0, LANES - SSD_N_HEADS), (0, 0)))
            rider, land = gather_rider(f"ssd_in_{i}")
            pzx = land(_matmul(u, gw["ssd_in_w"], dn="nt", b_lay=("stack", j, SSD_ZX), name=f"ssd_in_{i}", rider=rider))
            pdt = _matmul(u, w_dt, dn="nt", name=f"ssd_indt_{i}")
            rider, land = gather_rider(f"conv_{i}")
            act = land(_conv_fwd(pzx, cw_full[j], ssd_conv_b[j:j + 1], f"conv_{i}", rider=rider))
            dtg = jnp.pad(pdt[:, :SSD_N_HEADS].reshape(-1, SSD_N_GROUPS, 8).transpose(1, 0, 2), ((0, 0), (0, 0), (0, LANES - 8)))
            vecs = (_group_lanes(ssd_dt_bias[j]), _group_lanes(ssd_a_log[j]), _group_lanes(ssd_d[j]))
            rider, land = gather_rider(f"ssd_{i}")
            yn, states = land(_ssd_fwd(act, dtg, *vecs, pzx, ssd_gnorm_w[j:j + 1], f"ssd_{i}", rider=rider))
            s.update(w_dt=w_dt, pzx=pzx, act=act, dtg=dtg, vecs=vecs, yn=yn, states=states)
            rider, land = gather_rider(f"ssd_out_{i}")
            h1 = land(wmm(yn, "ssd_out_w", j, res=h, call=f"ssd_out_{i}", rider=rider))
        else:
            rider, land = gather_rider(f"sb_in_{i}")
            proj = land(wmm(u, "sb_in_w", j, call=f"sb_in_{i}", rider=rider))
            qn, kn, vb = _qknorm_fwd(proj, sb_qn_w[j:j + 1], sb_kn_w[j:j + 1], f"qknorm_{i}")
            rider, land = gather_rider(f"sb_{i}")
            og, o, tot = land(_sb_fwd(qn, kn, vb, proj, f"sb_{i}", rider=rider))
            s.update(proj=proj, qn=qn, kn=kn, vb=vb, og=og, o=o, tot=tot)
            rider, land = gather_rider(f"sb_out_{i}")
            h1 = land(wmm(og, "sb_out_w", j, res=h, call=f"sb_out_{i}", rider=rider))
        n2 = _rms_fwd(h1, pw, f"ple_rms_{i}")
        gl = wmm(n2, "ple_gate_w", i, call=f"ple_gate_{i}")
        pp = wmm(p[i, 0], "ple_proj_w", i, call=f"ple_proj_{i}")
        h = _ple_fwd(h1, pp, gl, f"ple_{i}")
        s.update(h1=h1, n2=n2, gl=gl, pp=pp)
        saved.append(s)

    dh, loss_lanes = _loss_bwd(h, target, "loss")

    wg = {}
    gsmall = {n: [None] * s[0] for n, s in _SMALL}
    g_conv_w = [None, None]
    scat = {}
    pending = late = None

    def wgrad(a, b, name, layer, call, rider=None):
        out = _matmul(a, b, dn="tn", out_dtype=BF16, o_lay=(cut_of[name], 0, 1), name=call, rider=rider)
        wg[(name, layer)], rode = out if rider is not None else (out, None)
        return rode

    def pair_sums(pieces, got, tag):
        return pieces, [_pair_sum(wg[q], r, place, f"rs_pair_sum_{tag}_{k}") for k, (q, r) in enumerate(zip(pieces, got))]

    def sibling_rider(pieces):
        return _exchange_rider([wg[q] for q in pieces])

    def riding_with(own):
        return (pending[0] + own[0], pending[1] + own[1]) if pending else own

    def arrived(sent, got):
        for q, pair, g in zip(sent[0], sent[1], got):
            scat[q] = (pair, g)

    for i in reversed(range(DEPTH)):
        j = i // 2
        s = saved[i]
        nw = norm_w[i:i + 1]
        pw = ple_norm_w[i:i + 1]
        dpp, dgl = _ple_bwd(dh, s["pp"], s["gl"], f"ple_bwd_{i}")
        wgrad(p[i, 0], dpp, "ple_proj_w", i, f"d_ple_proj_{i}")
        if late is None:
            wgrad(s["n2"], dgl, "ple_gate_w", i, f"d_ple_gate_{i}")
        else:
            pending = pair_sums(late, wgrad(s["n2"], dgl, "ple_gate_w", i, f"d_ple_gate_{i}", rider=sibling_rider(late)), f"{i + 1}_in")
        dn2 = wmm(dgl, "ple_gate_w", i, dn="nt", call=f"ple_gate_bwd_{i}")
        dh1, dpw = _rms_bwd(s["h1"], pw, dn2, dh, f"ple_rms_bwd_{i}")
        gsmall["ple_norm_w"][i] = dpw
        if i % 2 == 0:
            wgrad(s["yn"], dh1, "ssd_out_w", j, f"d_ssd_out_{i}")
            early = layer_pieces(i)[1:]
            dyn, got = wmm(dh1, "ssd_out_w", j, dn="nt", call=f"ssd_out_bwd_{i}", rider=sibling_rider(early))
            riding = riding_with(pair_sums(early, got, f"{i}_out"))
            outs, got = _ssd_bwd(s["act"], s["dtg"], *s["vecs"], s["pzx"], ssd_gnorm_w[j:j + 1], s["states"], dyn, f"ssd_bwd_{i}",
                                 rider=_scatter_rider(riding[1]))
            arrived(riding, got)
            dxs, dbm, dcm, ddtg, dbias, dalog, ddsk, dz, dgw = outs
            dzx, dcw, dcb = _conv_bwd(s["pzx"], cw_full[j], ssd_conv_b[j:j + 1], dxs, dbm, dcm, dz, f"conv_bwd_{i}")
            ddt = _pad_lanes(ddtg[:, :, :8].transpose(1, 0, 2).reshape(-1, SSD_N_HEADS)).astype(BF16)
            dwt = _matmul(dzx, s["u"], dn="tn", out_dtype=BF16, out_rows=SSD_IN_DIM, name=f"d_ssd_in_{i}")
            dwt_dt = _matmul(ddt, s["u"], dn="tn", out_dtype=BF16, name=f"d_ssd_indt_{i}")
            wg[("ssd_in_w", j)] = lax.dynamic_update_slice(dwt, dwt_dt[:SSD_N_HEADS], (SSD_ZX, 0))[None]
            if i == 0:
                by_shard = wg[("ssd_in_w", 0)].reshape(4, -1, D_MODEL)
                parts = [("ssd_in_w", 0, 0), ("ssd_in_w", 0, 1)]
                wg[parts[0]] = by_shard[:, :LAST_SPLIT].reshape(1, -1, D_MODEL)
                wg[parts[1]] = by_shard[:, LAST_SPLIT:].reshape(1, -1, D_MODEL)
                last = pair_sums(parts, _pair_exchange([wg[q] for q in parts], "rs_pair_exchange_last"), "0_in")
                du, got = _matmul(dzx, gw["ssd_in_w"], b_lay=("stack", j, SSD_ZX), name=f"ssd_in_bwd_{i}",
                                  rider=_scatter_rider(last[1][1:]))
                arrived((parts[1:], last[1][1:]), got)
            else:
                du = _matmul(dzx, gw["ssd_in_w"], b_lay=("stack", j, SSD_ZX), name=f"ssd_in_bwd_{i}")
            du = _matmul(ddt, s["w_dt"], res=du, name=f"ssd_indt_bwd_{i}")
            g_conv_w[j] = dcw
            gsmall["ssd_conv_b"][j] = dcb
            gsmall["ssd_dt_bias"][j] = dbias[:, 0, :8].reshape(1, SSD_N_HEADS)
            gsmall["ssd_a_log"][j] = dalog[:, 0, :8].reshape(1, SSD_N_HEADS)
            gsmall["ssd_d"][j] = ddsk[:, 0, :8].reshape(1, SSD_N_HEADS)
            gsmall["ssd_gnorm_w"][j] = dgw
        else:
            wgrad(s["og"], dh1, "sb_out_w", j, f"d_sb_out_{i}")
            early = layer_pieces(i)[1:]
            dog, got = wmm(dh1, "sb_out_w", j, dn="nt", call=f"sb_out_bwd_{i}", rider=sibling_rider(early))
            riding = riding_with(pair_sums(early, got, f"{i}_out"))
            outs, got = _sb_bwd(s["qn"], s["kn"], s["vb"], s["proj"], s["o"], s["tot"], dog, f"sb_bwd_{i}", rider=_scatter_rider(riding[1]))
            arrived(riding, got)
            dqn, dkn, _, dvb, dg = outs
            dq, dk, dqw, dkw = _qknorm_bwd(s["proj"], sb_qn_w[j:j + 1], sb_kn_w[j:j + 1], dqn, dkn, f"qknorm_bwd_{i}")
            dproj = jnp.concatenate([dq, dk, dvb, dg], axis=1)
            du = wmm(dproj, "sb_in_w", j, dn="nt", call=f"sb_in_bwd_{i}")
            wgrad(s["u"], dproj, "sb_in_w", j, f"d_sb_in_{i}")
            gsmall["sb_qn_w"][j] = dqw
            gsmall["sb_kn_w"][j] = dkw
        dh, dnw = _rms_bwd(s["h"], nw, du, dh1, f"rms_bwd_{i}")
        gsmall["norm_w"][i] = dnw
        late = layer_pieces(i)[:1]
    grad_x = dh[None]

    def reduced(names, call):
        halves = []
        for n in names:
            buf = None
            for l in range(layers_of[n]):
                if (n, l, 0) in scat:
                    r = shard_shapes[big_names.index(n)][1]
                    for part, row0 in ((0, 0), (1, LAST_SPLIT)):
                        buf = _chip_sum_stacked(*scat[(n, l, part)], place, l, layers_of[n], buf, f"rs_chip_sum_{n}_{l}_{part}", row0, r)
                else:
                    buf = _chip_sum(*scat[(n, l)], place, l, layers_of[n], buf, f"rs_chip_sum_{n}_{l}")
            halves.append(buf)
        return dict(zip(names, _pair_gather(halves, [cut_of[n] == "stack" for n in names], call)))

    def updated(n, rider=None):
        return _adamw(held(n, w_in[n]), g_big[n], held(n, m_in[n]), held(n, v_in[n]), f"adamw_{n}", rider=rider)

    done_early = ["sb_in_w", "sb_out_w"]
    g_big = reduced(done_early, "rs_pair_gather_sb")
    step = {}
    step["sb_in_w"], got = updated("sb_in_w", rider=_scatter_rider(last[1][:1]))
    arrived((last[0][:1], last[1][:1]), got)
    g_big.update(reduced([n for n in big_names if n not in done_early], "rs_pair_gather"))

    small_parts = [jnp.concatenate(gsmall[n], axis=0).reshape(-1) for n, _ in _SMALL]
    small_parts.append(jnp.stack(g_conv_w).reshape(-1))
    small_parts.append(loss_lanes.reshape(-1))
    small_sum = _allreduce_small(jnp.concatenate(small_parts).reshape(-1, LANES), "allreduce_small").reshape(-1)
    g_small, off = {}, 0
    for n, shape in _SMALL:
        size = math.prod(shape)
        g_small[n] = small_sum[off:off + size].reshape(shape)
        off += size
    cw_size = 2 * SSD_D_CONV * SSD_CONV_DIM
    g_cw_full = small_sum[off:off + cw_size].reshape(2, SSD_D_CONV, 4, SSD_CONV_DIM // 4)
    g_small["ssd_conv_w"] = jnp.sum(g_cw_full * (jnp.arange(4) == chip).astype(F32)[None, None, :, None], axis=2)
    loss = 0.5 * jnp.sum(small_sum[off + cw_size:]) / D_MODEL

    grads, delta, new_m, new_v = {}, {}, {}, {}
    for n in big_names:
        grads[n], delta[n], new_m[n], new_v[n] = (held(n, a) for a in (g_big[n], *(step[n] if n in step else updated(n))))
    small_names = [n for n, _ in _SMALL] + ["ssd_conv_w"]
    pack = lambda d: jnp.concatenate([d[n].reshape(-1) for n in small_names]).reshape(1, -1, LANES)
    ds, ms, vs = _adamw(pack(w_in), pack(g_small), pack(m_in), pack(v_in), "adamw_small")
    off = 0
    for n in small_names:
        shape = w_in[n].shape
        size = math.prod(shape)
        grads[n] = g_small[n]
        delta[n] = ds.reshape(-1)[off:off + size].reshape(shape)
        new_m[n] = ms.reshape(-1)[off:off + size].reshape(shape)
        new_v[n] = vs.reshape(-1)[off:off + size].reshape(shape)
        off += size

    order = ["norm_w", "ssd_in_w", "ssd_conv_w", "ssd_conv_b", "ssd_dt_bias", "ssd_a_log", "ssd_d", "ssd_gnorm_w", "ssd_out_w",
             "sb_in_w", "sb_qn_w", "sb_kn_w", "sb_out_w", "ple_norm_w", "ple_gate_w", "ple_proj_w"]
    return (loss, grad_x, *[grads[n] for n in order], *[delta[n] for n in order], *[new_m[n] for n in order],
            *[new_v[n] for n in order])
```

```python
import functools
import math

import jax
import jax.numpy as jnp
from jax import lax
from jax.experimental import pallas as pl
from jax.experimental.pallas import tpu as pltpu

F32 = jnp.float32
BF16 = jnp.bfloat16
MESH = pl.DeviceIdType.MESH

D_MODEL = 2048
DEPTH = 4
SSD_D_INNER = 4096
SSD_N_GROUPS = 8
SSD_GROUP_W = SSD_D_INNER // SSD_N_GROUPS
SSD_D_STATE = 128
SSD_CHUNK = 128
SSD_CONV_DIM = 6144
SSD_D_CONV = 4
SSD_N_HEADS = 64
SB_HEAD_DIM = 128
SB_N_HEADS = 16
SB_WIDTH = 2048
NORM_EPS = 1e-6
GATED_NORM_EPS = 1e-5
ADAM_LR = 0.001
ADAM_B1 = 0.9
ADAM_B2 = 0.999
ADAM_EPS = 1e-08
ADAM_WD = 0.01
ADAM_STEP = 10

SSD_ZX = SSD_D_INNER + SSD_CONV_DIM
SSD_IN_DIM = SSD_ZX + SSD_N_HEADS
LAST_SPLIT = 1104
LANES = 128
BF16_ROWS = 16

_BIG = (
    ("ssd_in_w", (2, 2576, 2048), "stack"),
    ("ssd_out_w", (2, 1024, 2048), "row"),
    ("sb_in_w", (2, 2048, 2048), "col"),
    ("sb_out_w", (2, 512, 2048), "row"),
    ("ple_gate_w", (4, 512, 2048), "row"),
    ("ple_proj_w", (4, 256, 512), "col"),
)
_SMALL = (
    ("norm_w", (4, 2048)),
    ("ssd_conv_b", (2, 6144)),
    ("ssd_dt_bias", (2, 64)),
    ("ssd_a_log", (2, 64)),
    ("ssd_d", (2, 64)),
    ("ssd_gnorm_w", (2, 4096)),
    ("sb_qn_w", (2, 128)),
    ("sb_kn_w", (2, 128)),
    ("ple_norm_w", (4, 2048)),
)

_DN = {
    "nn": (((1,), (0,)), ((), ())),
    "nt": (((1,), (1,)), ((), ())),
    "tn": (((0,), (0,)), ((), ())),
}


def _dot(a, b, dn="nn"):
    return lax.dot_general(a.astype(BF16), b.astype(BF16), _DN[dn], preferred_element_type=F32)


@functools.partial(jax.custom_vjp, nondiff_argnums=(2,))
def _gdot(a, b, dn):
    return _dot(a, b, dn)


def _gdot_fwd(a, b, dn):
    return _dot(a, b, dn), (a, b)


def _gdot_bwd(dn, res, g):
    a, b = res
    if dn == "nn":
        return _dot(g, b, "nt"), _dot(a, g, "tn")
    if dn == "nt":
        return _dot(g, b, "nn"), _dot(g, a, "tn")
    return _dot(b, g, "nt"), _dot(a, g, "nn")


_gdot.defvjp(_gdot_fwd, _gdot_bwd)


def _split_dot(x, t, parts, x_left):
    acc = None
    r = x
    for i in range(parts):
        p = r.astype(BF16)
        d = lax.dot_general(p, t, _DN["nn"], preferred_element_type=F32) if x_left else lax.dot_general(
            t, p, _DN["nn"], preferred_element_type=F32)
        acc = d if acc is None else acc + d
        if i + 1 < parts:
            r = r - p.astype(F32)
    return acc


def _tri(n, lower, strict=False):
    r = lax.broadcasted_iota(jnp.int32, (n, n), 0)
    c = lax.broadcasted_iota(jnp.int32, (n, n), 1)
    keep = (r > c if strict else r >= c) if lower else (r < c if strict else r <= c)
    return jnp.where(keep, 1.0, 0.0).astype(BF16)


def _cumsum_rows_raw(x):
    return _split_dot(x, _tri(x.shape[0], True), 3, False)


@jax.custom_vjp
def _cumsum_rows(x):
    return _cumsum_rows_raw(x)


def _cumsum_rows_fwd(x):
    return _cumsum_rows_raw(x), None


def _cumsum_rows_bwd(_, g):
    return (_split_dot(g, _tri(g.shape[0], False), 3, False),)


_cumsum_rows.defvjp(_cumsum_rows_fwd, _cumsum_rows_bwd)


def _sigmoid(x):
    return 1.0 / (1.0 + jnp.exp(-x))


def _softplus(x):
    return jnp.maximum(x, 0.0) + jnp.log(1.0 + jnp.exp(-jnp.abs(x)))


def _rms(x, w, eps):
    return x * lax.rsqrt(jnp.mean(x * x, axis=-1, keepdims=True) + eps) * w


_ANY = pl.BlockSpec(memory_space=pl.ANY)


def _params(*sem):
    return pltpu.CompilerParams(dimension_semantics=sem)


class _Rider:
    def __init__(self, reads, writes, n_sems, issue):
        self.reads, self.writes, self.n_sems, self.issue = list(reads), list(writes), n_sems, issue


def _pcall(body, *, grid, in_specs, out_specs, out_shape, args, sem, name, scratch_shapes=(), aliases=None, rider=None):
    aliases = dict(aliases or {})
    if rider is None:
        outs = pl.pallas_call(body, grid=grid, in_specs=in_specs, out_specs=out_specs, out_shape=out_shape,
                              scratch_shapes=list(scratch_shapes), input_output_aliases=aliases,
                              compiler_params=_params(*sem), name=name)(*args)
        return list(outs), []
    n_in, n_out, n_scr, n_rd, n_wr = len(args), len(out_shape), len(scratch_shapes), len(rider.reads), len(rider.writes)
    passed = [k for k, w in enumerate(rider.writes) if not isinstance(w, jax.ShapeDtypeStruct)]
    for pos, k in enumerate(passed):
        aliases[n_in + n_rd + pos] = n_out + k

    def wrapped(*refs):
        ins = refs[:n_in]
        reads = refs[n_in:n_in + n_rd]
        base = n_in + n_rd + len(passed)
        outs = refs[base:base + n_out]
        writes = refs[base + n_out:base + n_out + n_wr]
        scr = refs[base + n_out + n_wr:base + n_out + n_wr + n_scr]
        send, recv = refs[-2:]
        first = last = None
        for d, n in enumerate(grid):
            i = pl.program_id(d)
            first = (i == 0) if first is None else first & (i == 0)
            last = (i == n - 1) if last is None else last & (i == n - 1)

        @pl.when(first)
        def _():
            for cp in rider.issue(reads, writes, send, recv)[0]:
                cp.start()

        body(*ins, *outs, *scr)

        @pl.when(last)
        def _():
            sends, arrivals = rider.issue(reads, writes, send, recv)
            for cp in arrivals:
                cp.wait_recv()
            for cp in sends:
                cp.wait_send()

    outs = pl.pallas_call(
        wrapped, grid=grid,
        in_specs=list(in_specs) + [_ANY] * (n_rd + len(passed)),
        out_specs=list(out_specs) + [_ANY] * n_wr,
        out_shape=list(out_shape) + [jax.ShapeDtypeStruct(w.shape, w.dtype) for w in rider.writes],
        scratch_shapes=list(scratch_shapes) + [pltpu.SemaphoreType.DMA((rider.n_sems,)), pltpu.SemaphoreType.DMA((rider.n_sems,))],
        input_output_aliases=aliases, compiler_params=_params(*(["arbitrary"] * len(grid))), name=name,
    )(*args, *rider.reads, *[rider.writes[k] for k in passed])
    return list(outs[:n_out]), list(outs[n_out:])


MM_TK = 2048


def _pick(dim, pref, unit=None):
    t = pref
    while t >= LANES:
        if dim % t == 0 and (unit is None or unit % t == 0):
            return t
        t //= 2
    return dim


def _matmul(a, b, *, dn="nn", res=None, out_dtype=F32, name, b_lay=None, o_lay=None, o_buf=None, out_rows=None, rider=None):
    if dn == "tn":
        k_dim, m_dim = a.shape
    else:
        m_dim, k_dim = a.shape
    unit_m = unit_n = unit_k = None
    if b_lay is None:
        n_dim = b.shape[0] if dn == "nt" else b.shape[1]
    elif b_lay[0] == "stack":
        cut, layer, rows = b_lay
        cols = b.shape[2]
        n_dim = cols if dn == "nn" else rows
        assert k_dim == (rows if dn == "nn" else cols) and dn != "tn"
    else:
        cut, layer = b_lay
        r, c = b.shape[2:]
        rows, cols = (4 * r, c) if cut == "row" else (r, 4 * c)
        n_dim = cols if dn == "nn" else rows
        assert k_dim == (rows if dn == "nn" else cols) and dn != "tn"
        if (cut == "row") == (dn == "nn"):
            unit_k = r if cut == "row" else c
        else:
            unit_n = r if cut == "row" else c
    if o_lay is not None:
        o_cut, o_layer, o_layers = o_lay
        if o_cut == "row":
            unit_m = m_dim // 4
        else:
            unit_n = n_dim // 4
    tm, tn, tk = _pick(m_dim, 1024, unit_m), _pick(n_dim, 1024, unit_n), _pick(k_dim, MM_TK, unit_k)
    nk = k_dim // tk
    a_spec = pl.BlockSpec((tk, tm), lambda i, j, k: (k, i)) if dn == "tn" else pl.BlockSpec((tm, tk), lambda i, j, k: (i, k))
    if b_lay is None:
        b_spec = pl.BlockSpec((tn, tk), lambda i, j, k: (j, k)) if dn == "nt" else pl.BlockSpec((tk, tn), lambda i, j, k: (k, j))
    elif cut == "stack":
        b_spec = (pl.BlockSpec((None, tk, tn), lambda i, j, k: (layer, k, j)) if dn == "nn" else
                  pl.BlockSpec((None, tn, tk), lambda i, j, k: (layer, j, k)))
    elif dn == "nn" and cut == "row":
        per = r // tk
        b_spec = pl.BlockSpec((None, None, tk, tn), lambda i, j, k: (k // per, layer, k % per, j))
    elif dn == "nn":
        per = c // tn
        b_spec = pl.BlockSpec((None, None, tk, tn), lambda i, j, k: (j // per, layer, k, j % per))
    elif cut == "row":
        per = r // tn
        b_spec = pl.BlockSpec((None, None, tn, tk), lambda i, j, k: (j // per, layer, j % per, k))
    else:
        per = c // tk
        b_spec = pl.BlockSpec((None, None, tn, tk), lambda i, j, k: (k // per, layer, j, k % per))
    r_spec = pl.BlockSpec((tm, tn), lambda i, j, k: (i, j))
    if o_lay is None:
        o_spec = r_spec
        out_shape = jax.ShapeDtypeStruct((out_rows or m_dim, n_dim), out_dtype)
    elif o_cut == "row":
        per_o = unit_m // tm
        o_spec = pl.BlockSpec((None, None, tm, tn), lambda i, j, k: (i // per_o, o_layer, i % per_o, j))
        out_shape = jax.ShapeDtypeStruct((4, o_layers, unit_m, n_dim), out_dtype)
    else:
        per_o = unit_n // tn
        o_spec = pl.BlockSpec((None, None, tm, tn), lambda i, j, k: (j // per_o, o_layer, i, j % per_o))
        out_shape = jax.ShapeDtypeStruct((4, o_layers, m_dim, unit_n), out_dtype)
    has_res = res is not None
    has_buf = o_buf is not None

    def body(*refs):
        a_ref, b_ref = refs[:2]
        r_ref = refs[2] if has_res else None
        o_ref = refs[-1] if nk == 1 else refs[-2]

        def finish(v):
            if has_res:
                v = v + r_ref[...]
            o_ref[...] = v.astype(o_ref.dtype)

        if nk == 1:
            finish(_dot(a_ref[...], b_ref[...], dn))
            return
        acc_ref = refs[-1]
        k = pl.program_id(2)

        @pl.when(k == 0)
        def _():
            acc_ref[...] = jnp.zeros_like(acc_ref)

        acc_ref[...] += _dot(a_ref[...], b_ref[...], dn)

        @pl.when(k == nk - 1)
        def _():
            finish(acc_ref[...])

    args = [a, b] + ([res] if has_res else []) + ([o_buf] if has_buf else [])
    outs, rode = _pcall(
        body, grid=(m_dim // tm, n_dim // tn, nk),
        in_specs=[a_spec, b_spec] + ([r_spec] if has_res else []) + ([_ANY] if has_buf else []),
        out_specs=[o_spec], out_shape=[out_shape],
        scratch_shapes=[] if nk == 1 else [pltpu.VMEM((tm, tn), F32)],
        aliases={len(args) - 1: 0} if has_buf else {},
        args=args, sem=("parallel", "parallel", "arbitrary"), name=name, rider=rider)
    return (outs[0], rode) if rider is not None else outs[0]


def _rowcall(fn, rows, consts, outs, accs, *, name, tm=512):
    args = list(rows) + list(consts)
    in_specs = [pl.BlockSpec((tm, r.shape[1]), lambda i: (i, 0)) for r in rows]
    in_specs += [pl.BlockSpec(c.shape, lambda i: (0, 0)) for c in consts]
    s_dim = args[0].shape[0]
    n_in, n_out = len(args), len(outs)
    out_shape = [jax.ShapeDtypeStruct((s_dim, w), dt) for w, dt in outs] + [jax.ShapeDtypeStruct(s, F32) for s in accs]
    out_specs = [pl.BlockSpec((tm, w), lambda i: (i, 0)) for w, _ in outs] + [pl.BlockSpec(s, lambda i: (0, 0)) for s in accs]

    def body(*refs):
        vals = fn(*[r[...] for r in refs[:n_in]])
        o_refs = refs[n_in:n_in + n_out]
        a_refs = refs[n_in + n_out:]
        for o, v in zip(o_refs, vals[:n_out]):
            o[...] = v.astype(o.dtype)
        if a_refs:
            @pl.when(pl.program_id(0) == 0)
            def _():
                for a_ref in a_refs:
                    a_ref[...] = jnp.zeros_like(a_ref)

            for a_ref, v in zip(a_refs, vals[n_out:]):
                a_ref[...] += v

    return pl.pallas_call(
        body, grid=(s_dim // tm,), in_specs=in_specs, out_specs=out_specs, out_shape=out_shape,
        compiler_params=_params("arbitrary"), name=name,
    )(*args)


def _rms_fwd(h, w, name):
    return _rowcall(lambda x, w_: (_rms(x, w_, NORM_EPS),), [h], [w], [(h.shape[1], BF16)], [], name=name)[0]


def _rms_bwd(h, w, dy, dres, name):
    def fn(x, dy_, dres_, w_):
        _, vjp = jax.vjp(lambda a, b: _rms(a, b, NORM_EPS), x, w_)
        dx, dw = vjp(dy_)
        return dx + dres_, dw

    return _rowcall(fn, [h, dy, dres], [w], [(h.shape[1], F32)], [w.shape], name=name)


def _ple_fwd(h1, pp, gl, name):
    return _rowcall(lambda a, b, c: (a + b * _sigmoid(c),), [h1, pp, gl], [], [(h1.shape[1], F32)], [], name=name)[0]


def _ple_bwd(dh2, pp, gl, name):
    def fn(d, b, c):
        gate = _sigmoid(c)
        return d * gate, d * b * gate * (1.0 - gate)

    return _rowcall(fn, [dh2, pp, gl], [], [(dh2.shape[1], BF16), (dh2.shape[1], BF16)], [], name=name)


def _loss_bwd(y, target, name):
    width = y.shape[1]

    def fn(a, t):
        d = a - t
        col = jnp.sum(d * d, axis=0, keepdims=True)
        part = col[:, 0:LANES]
        for j in range(1, width // LANES):
            part = part + col[:, j * LANES:(j + 1) * LANES]
        return d * (1.0 / width), part

    return _rowcall(fn, [y, target], [], [(width, F32)], [(1, LANES)], name=name)


CONV_TC = 256


def _shift_down(x, j):
    if j == 0:
        return x
    row = lax.broadcasted_iota(jnp.int32, x.shape, 0)
    return jnp.where(row >= j, pltpu.roll(x, j, 0), 0.0)


def _shift_up(x, j):
    if j == 0:
        return x
    n = x.shape[0]
    row = lax.broadcasted_iota(jnp.int32, x.shape, 0)
    return jnp.where(row < n - j, pltpu.roll(x, n - j, 0), 0.0)


def _conv_fwd(pzx, cw, cb, name, rider=None):
    s_dim = pzx.shape[0]
    off = SSD_D_INNER // CONV_TC

    def body(x_ref, w_ref, b_ref, o_ref):
        x = x_ref[...]
        w = w_ref[...]
        y = b_ref[...] + w[3:4, :] * x
        for k in range(SSD_D_CONV - 1):
            y = y + w[k:k + 1, :] * _shift_down(x, SSD_D_CONV - 1 - k)
        o_ref[...] = y * _sigmoid(y)

    outs, rode = _pcall(
        body, grid=(SSD_CONV_DIM // CONV_TC,),
        in_specs=[pl.BlockSpec((s_dim, CONV_TC), lambda j: (0, off + j)), pl.BlockSpec((SSD_D_CONV, CONV_TC), lambda j: (0, j)),
                  pl.BlockSpec((1, CONV_TC), lambda j: (0, j))],
        out_specs=[pl.BlockSpec((s_dim, CONV_TC), lambda j: (0, j))],
        out_shape=[jax.ShapeDtypeStruct((s_dim, SSD_CONV_DIM), F32)],
        args=[pzx, cw, cb], sem=("parallel",), name=name, rider=rider)
    return (outs[0], rode) if rider is not None else outs[0]


def _conv_bwd(pzx, cw, cb, dxs, dbm, dcm, dzx, name):
    s_dim = pzx.shape[0]
    off = SSD_D_INNER // CONV_TC
    n_x, n_b = dxs.shape[1] // CONV_TC, dbm.shape[1] // CONV_TC

    def body(x_ref, w_ref, b_ref, dxs_ref, dbm_ref, dcm_ref, _, dx_ref, dw_ref, db_ref):
        j = pl.program_id(0)
        d = jnp.where(j < n_x, dxs_ref[...], jnp.where(j < n_x + n_b, dbm_ref[...], dcm_ref[...]))
        x = x_ref[...]
        w = w_ref[...]
        xs = [_shift_down(x, SSD_D_CONV - 1 - k) for k in range(SSD_D_CONV)]
        y = b_ref[...]
        for k in range(SSD_D_CONV):
            y = y + w[k:k + 1, :] * xs[k]
        sg = _sigmoid(y)
        dy = d * (sg * (1.0 + y * (1.0 - sg)))
        dx = w[3:4, :] * dy
        for k in range(SSD_D_CONV - 1):
            dx = dx + w[k:k + 1, :] * _shift_up(dy, SSD_D_CONV - 1 - k)
        dx_ref[...] = dx.astype(dx_ref.dtype)
        for k in range(SSD_D_CONV):
            dw_ref[k:k + 1, :] = jnp.sum(dy * xs[k], axis=0, keepdims=True)
        db_ref[...] = jnp.sum(dy, axis=0, keepdims=True)

    part = lambda lo, n: pl.BlockSpec((s_dim, CONV_TC), lambda j: (0, jnp.clip(j - lo, 0, n - 1)))
    return pl.pallas_call(
        body, grid=(SSD_CONV_DIM // CONV_TC,),
        in_specs=[pl.BlockSpec((s_dim, CONV_TC), lambda j: (0, off + j)), pl.BlockSpec((SSD_D_CONV, CONV_TC), lambda j: (0, j)),
                  pl.BlockSpec((1, CONV_TC), lambda j: (0, j)), part(0, n_x), part(n_x, n_b), part(n_x + n_b, n_b), _ANY],
        out_specs=[pl.BlockSpec((s_dim, CONV_TC), lambda j: (0, off + j)), pl.BlockSpec((SSD_D_CONV, CONV_TC), lambda j: (0, j)),
                   pl.BlockSpec((1, CONV_TC), lambda j: (0, j))],
        out_shape=[jax.ShapeDtypeStruct(dzx.shape, dzx.dtype), jax.ShapeDtypeStruct((SSD_D_CONV, SSD_CONV_DIM), F32),
                   jax.ShapeDtypeStruct((1, SSD_CONV_DIM), F32)],
        input_output_aliases={6: 0}, compiler_params=_params("arbitrary"), name=name,
    )(pzx, cw, cb, dxs, dbm, dcm, dzx)


def _ssd_step(xs, bm, cm, dtraw, bias, alog, dskip, st_in, z, gw, dot, cumsum):
    n = xs.shape[0]
    lane = lax.broadcasted_iota(jnp.int32, (1, LANES), 1)
    sub = lax.broadcasted_iota(jnp.int32, (LANES, 1), 0)
    left = (lane < 64).astype(F32)
    right = 1.0 - left
    top = (sub < 64).astype(F32)
    bot = 1.0 - top
    row = lax.broadcasted_iota(jnp.int32, (n, n), 0)
    colm = lax.broadcasted_iota(jnp.int32, (n, n), 1)
    causal = row >= colm

    dt = _softplus(dtraw + bias)
    adt = dt * (-jnp.exp(alog))
    acum = cumsum(adt)
    acum_t = acum.T
    last = jnp.sum(adt, axis=0, keepdims=True)
    scores = dot(cm, bm, "nt")

    def lane_of(v, h):
        return jnp.sum(v * (lane == h).astype(F32), axis=1, keepdims=True)

    ys, sts = [], []
    for pr in range(4):
        heads = (2 * pr, 2 * pr + 1)
        ac = [lane_of(acum, h) for h in heads]
        ar = [jnp.sum(acum_t * (sub == h).astype(F32), axis=0, keepdims=True) for h in heads]
        dth = [lane_of(dt, h) for h in heads]
        la = [lane_of(last, h) for h in heads]
        dk = [lane_of(dskip, h) for h in heads]
        x2 = xs[:, pr * LANES:(pr + 1) * LANES]
        xdt = x2 * (dth[0] * left + dth[1] * right)
        yd = None
        for i, side in enumerate((left, right)):
            decay = jnp.where(causal, jnp.exp(jnp.minimum(ac[i] - ar[i], 0.0)), 0.0)
            t = dot(scores * decay, xdt * side, "nn")
            yd = t if yd is None else yd + t
        st2 = st_in[pr * LANES:(pr + 1) * LANES, :]
        yo = dot(cm, st2, "nt") * (jnp.exp(ac[0]) * left + jnp.exp(ac[1]) * right)
        dte = jnp.exp(la[0] - ac[0]) * left + jnp.exp(la[1] - ac[1]) * right
        cs = dot(xdt * dte, bm, "tn")
        sts.append(st2 * (jnp.exp(la[0]) * top + jnp.exp(la[1]) * bot) + cs)
        ys.append(yd + yo + (dk[0] * left + dk[1] * right) * x2)
    y = jnp.concatenate(ys, axis=1)
    yg = y * (z * _sigmoid(z))
    yn = yg * lax.rsqrt(jnp.mean(yg * yg, axis=-1, keepdims=True) + GATED_NORM_EPS) * gw
    return yn, jnp.concatenate(sts, axis=0)


SSD_GB_FWD, SSD_GB_BWD = 4, 2


def _ssd_specs(n_chunks, rev, gb):
    ci = (lambda c: n_chunks - 1 - c) if rev else (lambda c: c)
    n_x = SSD_D_INNER // (gb * LANES)
    n_g = SSD_N_GROUPS // gb
    return dict(
        xs=pl.BlockSpec((SSD_CHUNK, gb * SSD_GROUP_W), lambda g, c: (ci(c), g)),
        bm=pl.BlockSpec((SSD_CHUNK, gb * LANES), lambda g, c: (ci(c), n_x + g)),
        cm=pl.BlockSpec((SSD_CHUNK, gb * LANES), lambda g, c: (ci(c), n_x + n_g + g)),
        dt=pl.BlockSpec((gb, SSD_CHUNK, LANES), lambda g, c: (g, ci(c), 0)),
        vec=pl.BlockSpec((gb, 1, LANES), lambda g, c: (g, 0, 0)),
        z=pl.BlockSpec((SSD_CHUNK, gb * SSD_GROUP_W), lambda g, c: (ci(c), g)),
        gw=pl.BlockSpec((1, gb * SSD_GROUP_W), lambda g, c: (0, g)),
        st=pl.BlockSpec((gb, None, SSD_GROUP_W, SSD_D_STATE), lambda g, c: (g, ci(c), 0, 0)),
    )


def _group_cols(k, width):
    return slice(k * width, (k + 1) * width)


def _ssd_fwd(act, dtg, bias, alog, dskip, pzx, gw, name, rider=None):
    s_dim = act.shape[0]
    n_chunks = s_dim // SSD_CHUNK
    gb = SSD_GB_FWD
    sp = _ssd_specs(n_chunks, False, gb)

    def body(xs, bm, cm, dt, b_ref, a_ref, d_ref, z, gw_ref, yn_ref, st_ref, state):
        @pl.when(pl.program_id(1) == 0)
        def _():
            state[...] = jnp.zeros_like(state)

        for k in range(gb):
            wide, lanes = _group_cols(k, SSD_GROUP_W), _group_cols(k, LANES)
            st_in = state[k]
            st_ref[k] = st_in
            yn, st_out = _ssd_step(xs[:, wide], bm[:, lanes], cm[:, lanes], dt[k], b_ref[k], a_ref[k], d_ref[k], st_in, z[:, wide],
                                   gw_ref[:, wide], _dot, _cumsum_rows_raw)
            yn_ref[:, wide] = yn.astype(yn_ref.dtype)
            state[k] = st_out

    outs, rode = _pcall(
        body, grid=(SSD_N_GROUPS // gb, n_chunks),
        in_specs=[sp["xs"], sp["bm"], sp["cm"], sp["dt"], sp["vec"], sp["vec"], sp["vec"], sp["z"], sp["gw"]],
        out_specs=[sp["xs"], sp["st"]],
        out_shape=[jax.ShapeDtypeStruct((s_dim, SSD_D_INNER), BF16),
                   jax.ShapeDtypeStruct((SSD_N_GROUPS, n_chunks, SSD_GROUP_W, SSD_D_STATE), F32)],
        scratch_shapes=[pltpu.VMEM((gb, SSD_GROUP_W, SSD_D_STATE), F32)],
        args=[act, act, act, dtg, bias, alog, dskip, pzx, gw], sem=("parallel", "arbitrary"), name=name, rider=rider)
    return (outs, rode) if rider is not None else outs


def _ssd_bwd(act, dtg, bias, alog, dskip, pzx, gw, states, dyn, name, rider=None):
    s_dim = act.shape[0]
    n_chunks = s_dim // SSD_CHUNK
    gb = SSD_GB_BWD
    sp = _ssd_specs(n_chunks, True, gb)
    rc = lambda c: n_chunks - 1 - c

    def body(xs, bm, cm, dt, b_ref, a_ref, d_ref, z, gw_ref, st_ref, dyn_ref,
             dxs_ref, dbm_ref, dcm_ref, ddt_ref, db_ref, da_ref, dd_ref, dz_ref, dgw_ref, dstate):
        first = pl.program_id(1) == 0

        @pl.when(first)
        def _():
            dstate[...] = jnp.zeros_like(dstate)
            db_ref[...] = jnp.zeros_like(db_ref)
            da_ref[...] = jnp.zeros_like(da_ref)
            dd_ref[...] = jnp.zeros_like(dd_ref)
            dgw_ref[...] = jnp.zeros_like(dgw_ref)

        fn = functools.partial(_ssd_step, dot=_gdot, cumsum=_cumsum_rows)
        for k in range(gb):
            wide, lanes = _group_cols(k, SSD_GROUP_W), _group_cols(k, LANES)
            _, vjp = jax.vjp(fn, xs[:, wide], bm[:, lanes], cm[:, lanes], dt[k], b_ref[k], a_ref[k], d_ref[k], st_ref[k], z[:, wide],
                             gw_ref[:, wide])
            dxs, dbm, dcm, ddt, db, da, dd, dst, dz, dgw = vjp((dyn_ref[:, wide], dstate[k]))
            dxs_ref[:, wide] = dxs
            dbm_ref[:, lanes] = dbm
            dcm_ref[:, lanes] = dcm
            ddt_ref[k] = ddt
            dz_ref[:, wide] = dz.astype(dz_ref.dtype)
            db_ref[k] += db
            da_ref[k] += da
            dd_ref[k] += dd
            dgw_ref[:, wide] += dgw
            dstate[k] = dst

    bc = pl.BlockSpec((SSD_CHUNK, gb * LANES), lambda g, c: (rc(c), g))
    outs, rode = _pcall(
        body, grid=(SSD_N_GROUPS // gb, n_chunks),
        in_specs=[sp["xs"], sp["bm"], sp["cm"], sp["dt"], sp["vec"], sp["vec"], sp["vec"], sp["z"], sp["gw"], sp["st"], sp["xs"]],
        out_specs=[sp["xs"], bc, bc, sp["dt"], sp["vec"], sp["vec"], sp["vec"], sp["xs"], sp["gw"]],
        out_shape=[jax.ShapeDtypeStruct((s_dim, SSD_D_INNER), F32),
                   jax.ShapeDtypeStruct((s_dim, SSD_N_GROUPS * SSD_D_STATE), F32),
                   jax.ShapeDtypeStruct((s_dim, SSD_N_GROUPS * SSD_D_STATE), F32),
                   jax.ShapeDtypeStruct((SSD_N_GROUPS, s_dim, LANES), F32),
                   jax.ShapeDtypeStruct((SSD_N_GROUPS, 1, LANES), F32),
                   jax.ShapeDtypeStruct((SSD_N_GROUPS, 1, LANES), F32),
                   jax.ShapeDtypeStruct((SSD_N_GROUPS, 1, LANES), F32),
                   jax.ShapeDtypeStruct((s_dim, SSD_ZX), BF16),
                   jax.ShapeDtypeStruct((1, SSD_D_INNER), F32)],
        scratch_shapes=[pltpu.VMEM((gb, SSD_GROUP_W, SSD_D_STATE), F32)],
        args=[act, act, act, dtg, bias, alog, dskip, pzx, gw, states, dyn], sem=("arbitrary", "arbitrary"), name=name, rider=rider)
    return (outs, rode) if rider is not None else outs


SB_T = 128
SB_GROUP = 8
SB_WIDE = SB_GROUP * SB_T
SB_HB = 4
SB_SCALE = 1.0 / math.sqrt(SB_HEAD_DIM)


def _qknorm_fwd(proj, qw, kw, name, tm=512):
    s_dim = proj.shape[0]

    def body(q_ref, k_ref, v_ref, qw_ref, kw_ref, qo, ko, vo):
        for hh in range(SB_HB):
            qo[:, _head_lanes(hh)] = _rms(q_ref[:, _head_lanes(hh)], qw_ref[...], NORM_EPS).astype(BF16)
            ko[:, _head_lanes(hh)] = _rms(k_ref[:, _head_lanes(hh)], kw_ref[...], NORM_EPS).astype(BF16)
        vo[...] = v_ref[...].astype(BF16)

    groups = SB_N_HEADS // SB_HB
    blk = lambda o: pl.BlockSpec((tm, SB_HB * SB_HEAD_DIM), lambda i, h: (i, o + h))
    vec = pl.BlockSpec((1, SB_HEAD_DIM), lambda i, h: (0, 0))
    return pl.pallas_call(
        body, grid=(s_dim // tm, groups),
        in_specs=[blk(0), blk(groups), blk(2 * groups), vec, vec],
        out_specs=[blk(0)] * 3,
        out_shape=[jax.ShapeDtypeStruct((s_dim, SB_WIDTH), BF16)] * 3,
        compiler_params=_params("parallel", "parallel"), name=name,
    )(proj, proj, proj, qw, kw)


def _qknorm_bwd(proj, qw, kw, dqn, dkn, name, tm=512):
    s_dim = proj.shape[0]

    def body(q_ref, k_ref, dq_ref, dk_ref, qw_ref, kw_ref, dqo, dko, dqw, dkw):
        @pl.when((pl.program_id(0) == 0) & (pl.program_id(1) == 0))
        def _():
            dqw[...] = jnp.zeros_like(dqw)
            dkw[...] = jnp.zeros_like(dkw)

        fn = lambda a, b: _rms(a, b, NORM_EPS)
        for hh in range(SB_HB):
            lanes = _head_lanes(hh)
            for x_ref, w_ref, d_ref, dx_out, dw_out in ((q_ref, qw_ref, dq_ref, dqo, dqw), (k_ref, kw_ref, dk_ref, dko, dkw)):
                _, vjp = jax.vjp(fn, x_ref[:, lanes], w_ref[...])
                dx, dw = vjp(d_ref[:, lanes])
                dx_out[:, lanes] = dx.astype(BF16)
                dw_out[...] += dw

    groups = SB_N_HEADS // SB_HB
    blk = lambda o: pl.BlockSpec((tm, SB_HB * SB_HEAD_DIM), lambda i, h: (i, o + h))
    vec = pl.BlockSpec((1, SB_HEAD_DIM), lambda i, h: (0, 0))
    return pl.pallas_call(
        body, grid=(s_dim // tm, groups),
        in_specs=[blk(0), blk(groups), blk(0), blk(0), vec, vec],
        out_specs=[blk(0), blk(0), vec, vec],
        out_shape=[jax.ShapeDtypeStruct((s_dim, SB_WIDTH), BF16)] * 2 + [jax.ShapeDtypeStruct((1, SB_HEAD_DIM), F32)] * 2,
        compiler_params=_params("arbitrary", "arbitrary"), name=name,
    )(proj, proj, dqn, dkn, qw, kw)


def _sb_logits(q, k, strict):
    z = _dot(q, k, "nt") * SB_SCALE
    lb = jnp.minimum(z, 0.0) - jnp.log(1.0 + jnp.exp(-jnp.abs(z)))
    lm = lb - z
    if strict is not None:
        lm = jnp.where(strict, lm, 0.0)
    return lb, lm


def _sb_strict(qi, grp):
    r = lax.broadcasted_iota(jnp.int32, (SB_T, SB_WIDE), 0) + qi * SB_T
    c = lax.broadcasted_iota(jnp.int32, (SB_T, SB_WIDE), 1) + grp * SB_WIDE
    return c < r


def _head_lanes(hh):
    return slice(hh * SB_HEAD_DIM, (hh + 1) * SB_HEAD_DIM)


def _sb_fwd(qn, kn, vb, proj, name, rider=None):
    s_dim = qn.shape[0]
    nq = s_dim // SB_T
    assert nq % SB_GROUP == 0

    def body(q_ref, k_ref, v_ref, g_ref, og_ref, o_ref, t_ref):
        qi = pl.program_id(1)
        top = qi // SB_GROUP
        after = _tri(SB_T, True, strict=True)
        qs = [q_ref[:, _head_lanes(hh)] for hh in range(SB_HB)]

        def step(grp, masked, carries):
            start = pl.multiple_of(grp * SB_WIDE, SB_WIDE)
            strict = _sb_strict(qi, grp) if masked else None
            out = []
            for hh in range(SB_HB):
                o_acc, cr = carries[hh]
                k = k_ref[pl.ds(start, SB_WIDE), _head_lanes(hh)]
                v = v_ref[pl.ds(start, SB_WIDE), _head_lanes(hh)]
                lb, lm = _sb_logits(qs[hh], k, strict)
                rest = [None] * SB_GROUP
                for t in reversed(range(SB_GROUP)):
                    lm_t = lm[:, t * SB_T:(t + 1) * SB_T]
                    rest[t] = cr + _split_dot(lm_t, after, 2, True)
                    cr = cr + jnp.sum(lm_t, axis=1, keepdims=True)
                a = jnp.exp(lb + jnp.concatenate(rest, axis=1))
                if masked:
                    a = jnp.where(strict, a, 0.0)
                out.append((o_acc + _dot(a, v), cr))
            return tuple(out)

        init = tuple((jnp.zeros((SB_T, SB_HEAD_DIM), F32), jnp.zeros((SB_T, 1), F32)) for _ in range(SB_HB))
        carries = step(top, True, init)
        carries = lax.fori_loop(0, top, lambda i, c: step(top - 1 - i, False, c), carries)
        for hh in range(SB_HB):
            o, tot = carries[hh]
            g = g_ref[:, _head_lanes(hh)]
            o_ref[:, _head_lanes(hh)] = o
            og_ref[:, _head_lanes(hh)] = (o * (g * _sigmoid(g))).astype(og_ref.dtype)
            t_ref[hh] = jnp.broadcast_to(tot, (SB_T, LANES))

    wide = SB_HB * SB_HEAD_DIM
    qb = pl.BlockSpec((SB_T, wide), lambda h, i: (i, h))
    kv = pl.BlockSpec((s_dim, wide), lambda h, i: (0, h))
    outs, rode = _pcall(
        body, grid=(SB_N_HEADS // SB_HB, nq),
        in_specs=[qb, kv, kv, pl.BlockSpec((SB_T, wide), lambda h, i: (i, 3 * SB_N_HEADS // SB_HB + h))],
        out_specs=[qb, qb, pl.BlockSpec((SB_HB, SB_T, LANES), lambda h, i: (h, i, 0))],
        out_shape=[jax.ShapeDtypeStruct((s_dim, SB_WIDTH), BF16), jax.ShapeDtypeStruct((s_dim, SB_WIDTH), F32),
                   jax.ShapeDtypeStruct((SB_N_HEADS, s_dim, LANES), F32)],
        args=[qn, kn, vb, proj], sem=("parallel", "arbitrary"), name=name, rider=rider)
    return (outs, rode) if rider is not None else outs


def _sb_bwd(qn, kn, vb, proj, o, tot, dog, name, rider=None):
    s_dim = qn.shape[0]
    nq = s_dim // SB_T
    assert nq % SB_GROUP == 0

    def body(q_ref, k_ref, v_ref, g_ref, o_ref, t_ref, dog_ref, dq_ref, dk_ref, dv_ref, dvb_ref, dg_ref):
        qi = pl.program_id(1)
        top = qi // SB_GROUP

        @pl.when(qi == 0)
        def _():
            dk_ref[...] = jnp.zeros_like(dk_ref)
            dv_ref[...] = jnp.zeros_like(dv_ref)

        after = _tri(SB_T, True, strict=True)
        before = _tri(SB_T, False, strict=True)
        qs, dos, totals = [], [], []
        for hh in range(SB_HB):
            g = g_ref[:, _head_lanes(hh)]
            sg = _sigmoid(g)
            dog_v = dog_ref[:, _head_lanes(hh)]
            dg_ref[:, _head_lanes(hh)] = (dog_v * o_ref[:, _head_lanes(hh)] * (sg * (1.0 + g * (1.0 - sg)))).astype(dg_ref.dtype)
            dos.append((dog_v * (g * sg)).astype(BF16))
            qs.append(q_ref[:, _head_lanes(hh)])
            totals.append(t_ref[hh][:, 0:1])

        def step(grp, masked, carries):
            start = pl.multiple_of(grp * SB_WIDE, SB_WIDE)
            strict = _sb_strict(qi, grp) if masked else None
            out = []
            for hh in range(SB_HB):
                dq_acc, cp, ce = carries[hh]
                q, do = qs[hh], dos[hh]
                k = k_ref[pl.ds(start, SB_WIDE), _head_lanes(hh)]
                v = v_ref[pl.ds(start, SB_WIDE), _head_lanes(hh)]
                lb, lm = _sb_logits(q, k, strict)
                rest = []
                for t in range(SB_GROUP):
                    lm_t = lm[:, t * SB_T:(t + 1) * SB_T]
                    cp = cp + jnp.sum(lm_t, axis=1, keepdims=True)
                    rest.append((totals[hh] - cp) + _split_dot(lm_t, after, 2, True))
                a = jnp.exp(lb + jnp.concatenate(rest, axis=1))
                if masked:
                    a = jnp.where(strict, a, 0.0)
                e = a * _dot(do, v, "nt")
                excl = []
                for t in range(SB_GROUP):
                    e_t = e[:, t * SB_T:(t + 1) * SB_T]
                    excl.append(ce + _split_dot(e_t, before, 1, True))
                    ce = ce + jnp.sum(e_t, axis=1, keepdims=True)
                eex = jnp.concatenate(excl, axis=1)
                if masked:
                    eex = jnp.where(strict, eex, 0.0)
                sig = jnp.exp(lb)
                dz = (e * (1.0 - sig) - eex * sig) * SB_SCALE
                dv_ref[pl.ds(start, SB_WIDE), _head_lanes(hh)] += _dot(a, do, "tn")
                dk_ref[pl.ds(start, SB_WIDE), _head_lanes(hh)] += _dot(dz, q, "tn")
                out.append((dq_acc + _dot(dz, k), cp, ce))
            return tuple(out)

        zero = jnp.zeros((SB_T, 1), F32)
        init = tuple((jnp.zeros((SB_T, SB_HEAD_DIM), F32), zero, zero) for _ in range(SB_HB))
        carries = lax.fori_loop(0, top, lambda i, c: step(i, False, c), init)
        carries = step(top, True, carries)
        for hh in range(SB_HB):
            dq_ref[:, _head_lanes(hh)] = carries[hh][0]

        @pl.when(qi == nq - 1)
        def _():
            dvb_ref[...] = dv_ref[...].astype(BF16)

    wide = SB_HB * SB_HEAD_DIM
    qb = pl.BlockSpec((SB_T, wide), lambda h, i: (i, h))
    kv = pl.BlockSpec((s_dim, wide), lambda h, i: (0, h))
    outs, rode = _pcall(
        body, grid=(SB_N_HEADS // SB_HB, nq),
        in_specs=[qb, kv, kv, pl.BlockSpec((SB_T, wide), lambda h, i: (i, 3 * SB_N_HEADS // SB_HB + h)), qb,
                  pl.BlockSpec((SB_HB, SB_T, LANES), lambda h, i: (h, i, 0)), qb],
        out_specs=[qb, kv, kv, kv, qb],
        out_shape=[jax.ShapeDtypeStruct((s_dim, SB_WIDTH), F32), jax.ShapeDtypeStruct((s_dim, SB_WIDTH), F32),
                   jax.ShapeDtypeStruct((s_dim, SB_WIDTH), F32), jax.ShapeDtypeStruct((s_dim, SB_WIDTH), BF16),
                   jax.ShapeDtypeStruct((s_dim, SB_WIDTH), BF16)],
        args=[qn, kn, vb, proj, o, tot, dog], sem=("parallel", "arbitrary"), name=name, rider=rider)
    return (outs, rode) if rider is not None else outs


def _adamw_math(w, g, m, v):
    m = ADAM_B1 * m + (1.0 - ADAM_B1) * g
    v = ADAM_B2 * v + (1.0 - ADAM_B2) * (g * g)
    m_hat = m / (1.0 - ADAM_B1 ** ADAM_STEP)
    v_hat = v / (1.0 - ADAM_B2 ** ADAM_STEP)
    delta = -ADAM_LR * (m_hat / (jnp.sqrt(v_hat) + ADAM_EPS) + ADAM_WD * w)
    return delta, m, v


def _row_block(rows, cols, itemsize=4, limit=1 << 20):
    tr = rows
    while tr * cols * itemsize > limit and tr % (2 * BF16_ROWS) == 0:
        tr //= 2
    return tr


def _divisor_block(rows, cols, itemsize=4, limit=2 << 20):
    best = BF16_ROWS
    for t in range(BF16_ROWS, rows + 1, BF16_ROWS):
        if rows % t == 0 and t * cols * itemsize <= limit:
            best = t
    return best


def _adamw(w, g, m, v, name, rider=None):
    n, rows, cols = w.shape
    tr = rows if rows * cols * 4 <= (2 << 20) else _divisor_block(rows, cols)

    def body(w_ref, g_ref, m_ref, v_ref, d_out, m_out, v_out):
        d, m_new, v_new = _adamw_math(w_ref[...], g_ref[...], m_ref[...], v_ref[...])
        d_out[...] = d
        m_out[...] = m_new
        v_out[...] = v_new

    blk = pl.BlockSpec((None, tr, cols), lambda i, j: (i, j, 0))
    outs, rode = _pcall(
        body, grid=(n, rows // tr), in_specs=[blk] * 4, out_specs=[blk] * 3,
        out_shape=[jax.ShapeDtypeStruct(w.shape, F32)] * 3,
        args=[w, g, m, v], sem=("parallel", "parallel"), name=name, rider=rider)
    return (outs, rode) if rider is not None else outs


_FLIPS = ((1, 0), (0, 1), (1, 1))


def _place():
    return lax.axis_index("x"), lax.axis_index("y"), lax.axis_index("c")


def _flip(v, f):
    return 1 - v if f else v


def _half_rows(ref, lead, hc, hr, sub=(0, 1)):
    part = hr // sub[1]
    return ref.at[(*lead, pl.ds(pl.multiple_of(hc * hr + sub[0] * part, BF16_ROWS), part), slice(None))]


def _half_cols(ref, lead, hc, hw):
    return ref.at[(*lead, pl.ds(pl.multiple_of(hc * hw, LANES), hw))]


def _rows_of_chip(chip, r):
    return pl.ds(pl.multiple_of(chip * r, BF16_ROWS), r)


def _slot_half(gathered, shard_shape, chip, l, hc, sub=(0, 1)):
    r, c = shard_shape[1:]
    if len(gathered.shape) == 3:
        assert sub == (0, 1)
        return _half_cols(gathered, (l, _rows_of_chip(chip, r)), hc, c // 2)
    return _half_rows(gathered, (chip, l), hc, r // 2, sub)


def _shard_half(shard, stacked, l, hc, sub=(0, 1)):
    r, c = shard.shape[1:]
    if stacked:
        assert sub == (0, 1)
        return _half_cols(shard, (l, slice(None)), hc, c // 2)
    return _half_rows(shard, (l,), hc, r // 2, sub)


def _piece(piece):
    return piece[0], piece[1], tuple(piece[2:]) or (0, 1)


def _remote(src, dst, send, recv, k, to):
    return pltpu.make_async_remote_copy(src_ref=src, dst_ref=dst, send_sem=send.at[k], recv_sem=recv.at[k], device_id=to,
                                        device_id_type=MESH)


def _comm_call(reads, writes, n_sems, phases, name):
    passed = [k for k, w in enumerate(writes) if not isinstance(w, jax.ShapeDtypeStruct)]
    n_rd = len(reads)

    def body(*refs):
        rd = refs[:n_rd]
        wr = refs[n_rd + len(passed):n_rd + len(passed) + len(writes)]
        send, recv = refs[-2:]
        for phase in phases:
            sends, arrivals = phase(rd, wr, send, recv)
            for cp in sends:
                cp.start()
            for cp in arrivals:
                cp.wait_recv()
            for cp in sends:
                cp.wait_send()

    return pl.pallas_call(
        body, in_specs=[_ANY] * (n_rd + len(passed)), out_specs=[_ANY] * len(writes),
        out_shape=[jax.ShapeDtypeStruct(w.shape, w.dtype) for w in writes],
        input_output_aliases={n_rd + pos: k for pos, k in enumerate(passed)},
        scratch_shapes=[pltpu.SemaphoreType.DMA((n_sems,)), pltpu.SemaphoreType.DMA((n_sems,))], name=name,
    )(*reads, *[writes[k] for k in passed])


def _ag_ici(pieces, names, base=0):
    def phase(shards, gathered, send, recv):
        x, y, c = _place()
        me = 2 * x + y
        sends, arrivals = [], []
        for k, piece in enumerate(pieces):
            n, l, sub = _piece(piece)
            a = names.index(n)
            shape = shards[a].shape
            src = _shard_half(shards[a], len(gathered[a].shape) == 3, l, c, sub)
            for j, (fx, fy) in enumerate(_FLIPS):
                tx, ty = _flip(x, fx), _flip(y, fy)
                sends.append(_remote(src, _slot_half(gathered[a], shape, me, l, c, sub), send, recv, base + 3 * k + j, (tx, ty, c)))
                arrivals.append(_remote(src, _slot_half(gathered[a], shape, 2 * tx + ty, l, c, sub), send, recv, base + 3 * k + j,
                                        (tx, ty, c)))
        return sends, arrivals

    return phase


def _ag_pass_on(pieces, names, shapes, base=0):
    def phase(_, gathered, send, recv):
        x, y, c = _place()
        sibling = (x, y, 1 - c)
        sends, arrivals = [], []
        for k, piece in enumerate(pieces):
            n, l, sub = _piece(piece)
            a = names.index(n)
            for j, (fx, fy) in enumerate(_FLIPS):
                chip = 2 * _flip(x, fx) + _flip(y, fy)
                landed = _slot_half(gathered[a], shapes[a], chip, l, c, sub)
                sends.append(_remote(landed, landed, send, recv, base + 3 * k + j, sibling))
                arrivals.append(_remote(landed, _slot_half(gathered[a], shapes[a], chip, l, 1 - c, sub), send, recv, base + 3 * k + j, sibling))
        return sends, arrivals

    return phase


def _other_half(ref, hc):
    if len(ref.shape) == 3:
        return _half_cols(ref, (slice(None), slice(None)), hc, ref.shape[2] // 2)
    return _half_rows(ref, (slice(None), slice(None)), hc, ref.shape[2] // 2)


def _half_shape(shape):
    return shape[:2] + (shape[2] // 2,) if len(shape) == 3 else shape[:2] + (shape[2] // 2, shape[3])


def _exchange_phase(n_arr):
    def phase(ins, outs, send, recv):
        x, y, c = _place()
        cps = [_remote(_other_half(ins[a], 1 - c), outs[a], send, recv, a, (x, y, 1 - c)) for a in range(n_arr)]
        return cps, cps

    return phase


def _exchange_outs(grads):
    return [jax.ShapeDtypeStruct(_half_shape(g.shape), g.dtype) for g in grads]


def _pair_exchange(grads, name):
    return _comm_call(grads, _exchange_outs(grads), len(grads), [_exchange_phase(len(grads))], name)


def _exchange_rider(grads):
    return _Rider(grads, _exchange_outs(grads), len(grads), _exchange_phase(len(grads)))


def _pair_sum_stacked(g, got, place, name):
    _, rows, hw = got.shape
    tr = _divisor_block(rows, hw)

    def body(place_ref, g_ref, r_ref, o_ref):
        o_ref[...] = (g_ref[...].astype(F32) + r_ref[...].astype(F32)).astype(o_ref.dtype)

    blk = pl.BlockSpec((None, tr, hw), lambda i, pr: (0, i, 0))
    return pl.pallas_call(
        body,
        grid_spec=pltpu.PrefetchScalarGridSpec(
            num_scalar_prefetch=1, grid=(rows // tr,),
            in_specs=[pl.BlockSpec((None, tr, hw), lambda i, pr: (0, i, pr[1])), blk], out_specs=blk),
        out_shape=jax.ShapeDtypeStruct(got.shape, BF16),
        compiler_params=_params("parallel"), name=name,
    )(place, g, got)


def _pair_sum(g, got, place, name):
    if len(g.shape) == 3:
        return _pair_sum_stacked(g, got, place, name)
    _, layers, hr, cols = got.shape
    tr = _row_block(hr, cols, limit=2 << 20)
    per = hr // tr

    def body(place_ref, g_ref, r_ref, o_ref):
        o_ref[...] = (g_ref[...].astype(F32) + r_ref[...].astype(F32)).astype(o_ref.dtype)

    blk = pl.BlockSpec((None, None, tr, cols), lambda k, l, i, pr: (k, l, i, 0))
    return pl.pallas_call(
        body,
        grid_spec=pltpu.PrefetchScalarGridSpec(
            num_scalar_prefetch=1, grid=(4, layers, per),
            in_specs=[pl.BlockSpec((None, None, tr, cols), lambda k, l, i, pr: (k, l, pr[1] * per + i, 0)), blk],
            out_specs=blk),
        out_shape=jax.ShapeDtypeStruct(got.shape, BF16),
        compiler_params=_params("parallel", "parallel", "parallel"), name=name,
    )(place, g, got)


def _scatter_phase(n_arr):
    def phase(ins, outs, send, recv):
        x, y, c = _place()
        cps = []
        for a in range(n_arr):
            for j, (fx, fy) in enumerate(_FLIPS):
                tx, ty = _flip(x, fx), _flip(y, fy)
                if len(ins[a].shape) == 3:
                    src = ins[a].at[:, _rows_of_chip(2 * tx + ty, ins[a].shape[1] // 4), :]
                else:
                    src = ins[a].at[2 * tx + ty]
                cps.append(_remote(src, outs[a].at[j], send, recv, 3 * a + j, (tx, ty, c)))
        return cps, cps

    return phase


def _scatter_outs(pairs):
    return [jax.ShapeDtypeStruct((3, 1, p.shape[1] // 4, p.shape[2]) if len(p.shape) == 3 else (3,) + p.shape[1:], p.dtype) for p in pairs]


def _chip_scatter(pairs, name):
    return _comm_call(pairs, _scatter_outs(pairs), 3 * len(pairs), [_scatter_phase(len(pairs))], name)


def _scatter_rider(pairs):
    return _Rider(pairs, _scatter_outs(pairs), 3 * len(pairs), _scatter_phase(len(pairs)))


def _chip_sum_stacked(p, got, place, layer, layers, o_buf, name, row0=0, rows=None):
    _, r, hw = got.shape[1:]
    rows = rows or r
    tr = _divisor_block(math.gcd(r, row0) if row0 else r, hw)
    per = r // tr
    first = row0 // tr

    def body(place_ref, p_ref, r_ref, *rest):
        o_ref = rest[-1]
        acc = p_ref[...].astype(F32)
        for j in range(3):
            acc = acc + r_ref[j].astype(F32)
        o_ref[...] = acc

    has_buf = o_buf is not None
    return pl.pallas_call(
        body,
        grid_spec=pltpu.PrefetchScalarGridSpec(
            num_scalar_prefetch=1, grid=(per,),
            in_specs=[pl.BlockSpec((None, tr, hw), lambda i, pr: (0, pr[0] * per + i, 0)),
                      pl.BlockSpec((3, None, tr, hw), lambda i, pr: (0, 0, i, 0))] + ([_ANY] if has_buf else []),
            out_specs=pl.BlockSpec((None, tr, hw), lambda i, pr: (layer, first + i, pr[1]))),
        out_shape=jax.ShapeDtypeStruct((layers, rows, 2 * hw), F32),
        input_output_aliases={3: 0} if has_buf else {},
        compiler_params=_params("parallel"), name=name,
    )(*((place, p, got) + ((o_buf,) if has_buf else ())))


def _chip_sum(p, got, place, layer, layers, o_buf, name):
    if len(p.shape) == 3:
        return _chip_sum_stacked(p, got, place, layer, layers, o_buf, name)
    _, _, hr, cols = p.shape
    tr = _row_block(hr, cols, limit=2 << 20)
    per = hr // tr

    def body(place_ref, p_ref, r_ref, *rest):
        o_ref = rest[-1]
        acc = p_ref[...].astype(F32)
        for j in range(3):
            acc = acc + r_ref[j].astype(F32)
        o_ref[...] = acc

    has_buf = o_buf is not None
    return pl.pallas_call(
        body,
        grid_spec=pltpu.PrefetchScalarGridSpec(
            num_scalar_prefetch=1, grid=(per,),
            in_specs=[pl.BlockSpec((None, None, tr, cols), lambda i, pr: (pr[0], 0, i, 0)),
                      pl.BlockSpec((3, None, tr, cols), lambda i, pr: (0, 0, i, 0))] + ([_ANY] if has_buf else []),
            out_specs=pl.BlockSpec((None, tr, cols), lambda i, pr: (layer, pr[1] * per + i, 0))),
        out_shape=jax.ShapeDtypeStruct((layers, 2 * hr, cols), F32),
        input_output_aliases={3: 0} if has_buf else {},
        compiler_params=_params("parallel"), name=name,
    )(*((place, p, got) + ((o_buf,) if has_buf else ())))


def _pair_gather(halves, by_cols, name):
    def phase(_, bufs, send, recv):
        x, y, c = _place()
        sends, arrivals = [], []
        for a, h in enumerate(halves):
            cut = (lambda hc, a=a, h=h: _half_cols(bufs[a], (slice(None), slice(None)), hc, h.shape[2] // 2)) if by_cols[a] else (
                lambda hc, a=a, h=h: _half_rows(bufs[a], (slice(None),), hc, h.shape[1] // 2))
            sends.append(_remote(cut(c), cut(c), send, recv, a, (x, y, 1 - c)))
            arrivals.append(_remote(cut(c), cut(1 - c), send, recv, a, (x, y, 1 - c)))
        return sends, arrivals

    return _comm_call([], halves, len(halves), [phase], name)


def _allreduce_small(v, name):
    rows, cols = v.shape

    def body(v_ref, o_ref, buf, send_sems, recv_sems):
        x, y, c = _place()
        me = 4 * x + 2 * y + c
        buf[0] = v_ref[...]
        cps = []
        for k in range(1, 8):
            kx, ky, kc = (k >> 2) & 1, (k >> 1) & 1, k & 1
            cp = pltpu.make_async_remote_copy(src_ref=v_ref, dst_ref=buf.at[k], send_sem=send_sems.at[k - 1], recv_sem=recv_sems.at[k - 1],
                                              device_id=(_flip(x, kx), _flip(y, ky), _flip(c, kc)), device_id_type=MESH)
            cp.start()
            cps.append(cp)
        for cp in cps:
            cp.wait()
        acc = buf[me]
        for d in range(1, 8):
            acc = acc + buf[jnp.bitwise_xor(d, me)]
        o_ref[...] = acc

    vm = pl.BlockSpec(memory_space=pltpu.VMEM)
    return pl.pallas_call(
        body, in_specs=[vm], out_specs=vm, out_shape=jax.ShapeDtypeStruct((rows, cols), F32),
        scratch_shapes=[pltpu.VMEM((8, rows, cols), F32), pltpu.SemaphoreType.DMA((7,)), pltpu.SemaphoreType.DMA((7,))],
        name=name,
    )(v)


def _pad_lanes(a):
    return jnp.pad(a, ((0, 0), (0, LANES - a.shape[1])))


def _group_lanes(v):
    return jnp.pad(v.reshape(SSD_N_GROUPS, 1, 8), ((0, 0), (0, 0), (0, LANES - 8)))


def kernel(x, p, norm_w, ssd_in_w, ssd_conv_w, ssd_conv_b, ssd_dt_bias, ssd_a_log, ssd_d, ssd_gnorm_w, ssd_out_w, sb_in_w, sb_qn_w, sb_kn_w, sb_out_w, ple_norm_w, ple_gate_w, ple_proj_w, loss_target, m_norm_w, m_ssd_in_w, m_ssd_conv_w, m_ssd_conv_b, m_ssd_dt_bias, m_ssd_a_log, m_ssd_d, m_ssd_gnorm_w, m_ssd_out_w, m_sb_in_w, m_sb_qn_w, m_sb_kn_w, m_sb_out_w, m_ple_norm_w, m_ple_gate_w, m_ple_proj_w, v_norm_w, v_ssd_in_w, v_ssd_conv_w, v_ssd_conv_b, v_ssd_dt_bias, v_ssd_a_log, v_ssd_d, v_ssd_gnorm_w, v_ssd_out_w, v_sb_in_w, v_sb_qn_w, v_sb_kn_w, v_sb_out_w, v_ple_norm_w, v_ple_gate_w, v_ple_proj_w):
    w_in = dict(norm_w=norm_w, ssd_in_w=ssd_in_w, ssd_conv_w=ssd_conv_w, ssd_conv_b=ssd_conv_b, ssd_dt_bias=ssd_dt_bias,
                ssd_a_log=ssd_a_log, ssd_d=ssd_d, ssd_gnorm_w=ssd_gnorm_w, ssd_out_w=ssd_out_w, sb_in_w=sb_in_w, sb_qn_w=sb_qn_w,
                sb_kn_w=sb_kn_w, sb_out_w=sb_out_w, ple_norm_w=ple_norm_w, ple_gate_w=ple_gate_w, ple_proj_w=ple_proj_w)
    m_in = dict(norm_w=m_norm_w, ssd_in_w=m_ssd_in_w, ssd_conv_w=m_ssd_conv_w, ssd_conv_b=m_ssd_conv_b, ssd_dt_bias=m_ssd_dt_bias,
                ssd_a_log=m_ssd_a_log, ssd_d=m_ssd_d, ssd_gnorm_w=m_ssd_gnorm_w, ssd_out_w=m_ssd_out_w, sb_in_w=m_sb_in_w,
                sb_qn_w=m_sb_qn_w, sb_kn_w=m_sb_kn_w, sb_out_w=m_sb_out_w, ple_norm_w=m_ple_norm_w, ple_gate_w=m_ple_gate_w,
                ple_proj_w=m_ple_proj_w)
    v_in = dict(norm_w=v_norm_w, ssd_in_w=v_ssd_in_w, ssd_conv_w=v_ssd_conv_w, ssd_conv_b=v_ssd_conv_b, ssd_dt_bias=v_ssd_dt_bias,
                ssd_a_log=v_ssd_a_log, ssd_d=v_ssd_d, ssd_gnorm_w=v_ssd_gnorm_w, ssd_out_w=v_ssd_out_w, sb_in_w=v_sb_in_w,
                sb_qn_w=v_sb_qn_w, sb_kn_w=v_sb_kn_w, sb_out_w=v_sb_out_w, ple_norm_w=v_ple_norm_w, ple_gate_w=v_ple_gate_w,
                ple_proj_w=v_ple_proj_w)
    ix, iy, ic = lax.axis_index("x"), lax.axis_index("y"), lax.axis_index("c")
    chip = (2 * ix + iy).astype(jnp.int32)
    place = jnp.stack([chip, ic.astype(jnp.int32)])
    zero = jnp.zeros((), jnp.int32)
    big_names = [n for n, _, _ in _BIG]
    layers_of = {n: s[0] for n, s, _ in _BIG}
    cut_of = {n: cut for n, _, cut in _BIG}

    def layer_pieces(i):
        mixer = ("ssd_in_w", "ssd_out_w") if i % 2 == 0 else ("sb_in_w", "sb_out_w")
        return [(mixer[0], i // 2), (mixer[1], i // 2), ("ple_gate_w", i), ("ple_proj_w", i)]

    def names_of(pieces):
        return [n for n in big_names if any(n == q[0] for q in pieces)]

    held = lambda n, a: a.transpose(0, 2, 1) if cut_of[n] == "stack" else a
    mine = {n: held(n, w_in[n]).astype(BF16) for n in big_names}
    shard_shapes = [mine[n].shape for n in big_names]
    room = [jax.ShapeDtypeStruct((s[0], 4 * s[1], s[2]) if cut_of[n] == "stack" else (4,) + s, BF16) for n, s in zip(big_names, shard_shapes)]
    first = layer_pieces(0)[:1]
    gathered = _comm_call([mine[n] for n in big_names], room, 6 * len(first),
                          [_ag_ici(first, big_names), _ag_pass_on(first, big_names, shard_shapes, base=3 * len(first))], "allgather_layer0")
    gw = {}
    for n, g in zip(big_names, gathered):
        if cut_of[n] == "stack":
            layers, r, c = mine[n].shape
            gw[n] = lax.dynamic_update_slice(g.reshape(layers, 4, r, c), mine[n][:, None], (zero, chip, zero, zero)).reshape(g.shape)
        else:
            gw[n] = lax.dynamic_update_slice(g, mine[n][None], (chip, zero, zero, zero))

    lp = [layer_pieces(i) for i in range(DEPTH)]
    in3 = [lp[3][0] + (k, 2) for k in range(2)]
    carries = {
        "ssd_in_0": (lp[0][1:2], []), "conv_0": (lp[0][2:], lp[0][1:2]), "ssd_0": (lp[1][:1], lp[0][2:]),
        "ssd_out_0": (lp[1][1:2], lp[1][:1]), "sb_in_1": (lp[1][2:] + lp[2][2:], lp[1][1:2]), "sb_1": (lp[2][:2], lp[1][2:]),
        "sb_out_1": ([], lp[2][:2]), "ssd_in_2": ([in3[0]], lp[2][2:]), "conv_2": (lp[3][1:2], [in3[0]]),
        "ssd_2": ([in3[1]] + lp[3][2:], lp[3][1:2]), "ssd_out_2": ([], [in3[1]]), "sb_in_3": ([], lp[3][2:]),
    }

    def gather_rider(call):
        if call not in carries:
            return None, lambda outs: outs
        ici, passing = carries[call]
        names = names_of(ici + passing)
        phases = ([_ag_ici(ici, names)] if ici else []) + (
            [_ag_pass_on(passing, names, [mine[n].shape for n in names], base=3 * len(ici))] if passing else [])

        def issue(rd, wr, send, recv):
            both = [ph(rd, wr, send, recv) for ph in phases]
            return sum((b[0] for b in both), []), sum((b[1] for b in both), [])

        def land(outs):
            outs, bufs = outs
            for n, g in zip(names, bufs):
                gw[n] = g
            return outs

        return _Rider([mine[n] for n in names], [gw[n] for n in names], 3 * (len(ici) + len(passing)), issue), land

    onehot = (jnp.arange(4) == chip).astype(F32) * (ic == 0).astype(F32)
    cw_mine = onehot[:, None, None, None] * ssd_conv_w[None]
    cw_full = _allreduce_small(cw_mine.transpose(1, 2, 0, 3).reshape(-1, LANES), "gather_conv_w").reshape(2, SSD_D_CONV, SSD_CONV_DIM)

    def wmm(a, name, layer, *, dn="nn", res=None, call, rider=None):
        return _matmul(a, gw[name], dn=dn, res=res, b_lay=(cut_of[name], layer), name=call, rider=rider)

    h = x[0]
    target = loss_target[0]
    saved = []
    for i in range(DEPTH):
        j = i // 2
        nw = norm_w[i:i + 1]
        pw = ple_norm_w[i:i + 1]
        s = dict(h=h)
        u = _rms_fwd(h, nw, f"rms_{i}")
        s["u"] = u
        if i % 2 == 0:
            w_dt = jnp.pad(gw["ssd_in_w"][j, SSD_ZX:], ((0, LANES - SSD_N_HEADS), (0, 0)))
            rider, land = gather_rider(f"ssd_in_{i}")
            pzx = land(_matmul(u, gw["ssd_in_w"], dn="nt", b_lay=("stack", j, SSD_ZX), name=f"ssd_in_{i}", rider=rider))
            pdt = _matmul(u, w_dt, dn="nt", name=f"ssd_indt_{i}")
            rider, land = gather_rider(f"conv_{i}")
            act = land(_conv_fwd(pzx, cw_full[j], ssd_conv_b[j:j + 1], f"conv_{i}", rider=rider))
            dtg = jnp.pad(pdt[:, :SSD_N_HEADS].reshape(-1, SSD_N_GROUPS, 8).transpose(1, 0, 2), ((0, 0), (0, 0), (0, LANES - 8)))
            vecs = (_group_lanes(ssd_dt_bias[j]), _group_lanes(ssd_a_log[j]), _group_lanes(ssd_d[j]))
            rider, land = gather_rider(f"ssd_{i}")
            yn, states = land(_ssd_fwd(act, dtg, *vecs, pzx, ssd_gnorm_w[j:j + 1], f"ssd_{i}", rider=rider))
            s.update(w_dt=w_dt, pzx=pzx, act=act, dtg=dtg, vecs=vecs, yn=yn, states=states)
            rider, land = gather_rider(f"ssd_out_{i}")
            h1 = land(wmm(yn, "ssd_out_w", j, res=h, call=f"ssd_out_{i}", rider=rider))
        else:
            rider, land = gather_rider(f"sb_in_{i}")
            proj = land(wmm(u, "sb_in_w", j, call=f"sb_in_{i}", rider=rider))
            qn, kn, vb = _qknorm_fwd(proj, sb_qn_w[j:j + 1], sb_kn_w[j:j + 1], f"qknorm_{i}")
            rider, land = gather_rider(f"sb_{i}")
            og, o, tot = land(_sb_fwd(qn, kn, vb, proj, f"sb_{i}", rider=rider))
            s.update(proj=proj, qn=qn, kn=kn, vb=vb, og=og, o=o, tot=tot)
            rider, land = gather_rider(f"sb_out_{i}")
            h1 = land(wmm(og, "sb_out_w", j, res=h, call=f"sb_out_{i}", rider=rider))
        n2 = _rms_fwd(h1, pw, f"ple_rms_{i}")
        gl = wmm(n2, "ple_gate_w", i, call=f"ple_gate_{i}")
        pp = wmm(p[i, 0], "ple_proj_w", i, call=f"ple_proj_{i}")
        h = _ple_fwd(h1, pp, gl, f"ple_{i}")
        s.update(h1=h1, n2=n2, gl=gl, pp=pp)
        saved.append(s)

    dh, loss_lanes = _loss_bwd(h, target, "loss")

    wg = {}
    gsmall = {n: [None] * s[0] for n, s in _SMALL}
    g_conv_w = [None, None]
    scat = {}
    pending = late = None

    def wgrad(a, b, name, layer, call, rider=None):
        out = _matmul(a, b, dn="tn", out_dtype=BF16, o_lay=(cut_of[name], 0, 1), name=call, rider=rider)
        wg[(name, layer)], rode = out if rider is not None else (out, None)
        return rode

    def pair_sums(pieces, got, tag):
        return pieces, [_pair_sum(wg[q], r, place, f"rs_pair_sum_{tag}_{k}") for k, (q, r) in enumerate(zip(pieces, got))]

    def sibling_rider(pieces):
        return _exchange_rider([wg[q] for q in pieces])

    def riding_with(own):
        return (pending[0] + own[0], pending[1] + own[1]) if pending else own

    def arrived(sent, got):
        for q, pair, g in zip(sent[0], sent[1], got):
            scat[q] = (pair, g)

    for i in reversed(range(DEPTH)):
        j = i // 2
        s = saved[i]
        nw = norm_w[i:i + 1]
        pw = ple_norm_w[i:i + 1]
        dpp, dgl = _ple_bwd(dh, s["pp"], s["gl"], f"ple_bwd_{i}")
        wgrad(p[i, 0], dpp, "ple_proj_w", i, f"d_ple_proj_{i}")
        if late is None:
            wgrad(s["n2"], dgl, "ple_gate_w", i, f"d_ple_gate_{i}")
        else:
            pending = pair_sums(late, wgrad(s["n2"], dgl, "ple_gate_w", i, f"d_ple_gate_{i}", rider=sibling_rider(late)), f"{i + 1}_in")
        dn2 = wmm(dgl, "ple_gate_w", i, dn="nt", call=f"ple_gate_bwd_{i}")
        dh1, dpw = _rms_bwd(s["h1"], pw, dn2, dh, f"ple_rms_bwd_{i}")
        gsmall["ple_norm_w"][i] = dpw
        if i % 2 == 0:
            wgrad(s["yn"], dh1, "ssd_out_w", j, f"d_ssd_out_{i}")
            early = layer_pieces(i)[1:]
            dyn, got = wmm(dh1, "ssd_out_w", j, dn="nt", call=f"ssd_out_bwd_{i}", rider=sibling_rider(early))
            riding = riding_with(pair_sums(early, got, f"{i}_out"))
            outs, got = _ssd_bwd(s["act"], s["dtg"], *s["vecs"], s["pzx"], ssd_gnorm_w[j:j + 1], s["states"], dyn, f"ssd_bwd_{i}",
                                 rider=_scatter_rider(riding[1]))
            arrived(riding, got)
            dxs, dbm, dcm, ddtg, dbias, dalog, ddsk, dz, dgw = outs
            dzx, dcw, dcb = _conv_bwd(s["pzx"], cw_full[j], ssd_conv_b[j:j + 1], dxs, dbm, dcm, dz, f"conv_bwd_{i}")
            ddt = _pad_lanes(ddtg[:, :, :8].transpose(1, 0, 2).reshape(-1, SSD_N_HEADS)).astype(BF16)
            dwt = _matmul(dzx, s["u"], dn="tn", out_dtype=BF16, out_rows=SSD_IN_DIM, name=f"d_ssd_in_{i}")
            dwt_dt = _matmul(ddt, s["u"], dn="tn", out_dtype=BF16, name=f"d_ssd_indt_{i}")
            wg[("ssd_in_w", j)] = lax.dynamic_update_slice(dwt, dwt_dt[:SSD_N_HEADS], (SSD_ZX, 0))[None]
            if i == 0:
                by_shard = wg[("ssd_in_w", 0)].reshape(4, -1, D_MODEL)
                parts = [("ssd_in_w", 0, 0), ("ssd_in_w", 0, 1)]
                wg[parts[0]] = by_shard[:, :LAST_SPLIT].reshape(1, -1, D_MODEL)
                wg[parts[1]] = by_shard[:, LAST_SPLIT:].reshape(1, -1, D_MODEL)
                last = pair_sums(parts, _pair_exchange([wg[q] for q in parts], "rs_pair_exchange_last"), "0_in")
                du, got = _matmul(dzx, gw["ssd_in_w"], b_lay=("stack", j, SSD_ZX), name=f"ssd_in_bwd_{i}",
                                  rider=_scatter_rider(last[1][1:]))
                arrived((parts[1:], last[1][1:]), got)
            else:
                du = _matmul(dzx, gw["ssd_in_w"], b_lay=("stack", j, SSD_ZX), name=f"ssd_in_bwd_{i}")
            du = _matmul(ddt, s["w_dt"], res=du, name=f"ssd_indt_bwd_{i}")
            g_conv_w[j] = dcw
            gsmall["ssd_conv_b"][j] = dcb
            gsmall["ssd_dt_bias"][j] = dbias[:, 0, :8].reshape(1, SSD_N_HEADS)
            gsmall["ssd_a_log"][j] = dalog[:, 0, :8].reshape(1, SSD_N_HEADS)
            gsmall["ssd_d"][j] = ddsk[:, 0, :8].reshape(1, SSD_N_HEADS)
            gsmall["ssd_gnorm_w"][j] = dgw
        else:
            wgrad(s["og"], dh1, "sb_out_w", j, f"d_sb_out_{i}")
            early = layer_pieces(i)[1:]
            dog, got = wmm(dh1, "sb_out_w", j, dn="nt", call=f"sb_out_bwd_{i}", rider=sibling_rider(early))
            riding = riding_with(pair_sums(early, got, f"{i}_out"))
            outs, got = _sb_bwd(s["qn"], s["kn"], s["vb"], s["proj"], s["o"], s["tot"], dog, f"sb_bwd_{i}", rider=_scatter_rider(riding[1]))
            arrived(riding, got)
            dqn, dkn, _, dvb, dg = outs
            dq, dk, dqw, dkw = _qknorm_bwd(s["proj"], sb_qn_w[j:j + 1], sb_kn_w[j:j + 1], dqn, dkn, f"qknorm_bwd_{i}")
            dproj = jnp.concatenate([dq, dk, dvb, dg], axis=1)
            du = wmm(dproj, "sb_in_w", j, dn="nt", call=f"sb_in_bwd_{i}")
            wgrad(s["u"], dproj, "sb_in_w", j, f"d_sb_in_{i}")
            gsmall["sb_qn_w"][j] = dqw
            gsmall["sb_kn_w"][j] = dkw
        dh, dnw = _rms_bwd(s["h"], nw, du, dh1, f"rms_bwd_{i}")
        gsmall["norm_w"][i] = dnw
        late = layer_pieces(i)[:1]
    grad_x = dh[None]

    def reduced(names, call):
        halves = []
        for n in names:
            buf = None
            for l in range(layers_of[n]):
                if (n, l, 0) in scat:
                    r = shard_shapes[big_names.index(n)][1]
                    for part, row0 in ((0, 0), (1, LAST_SPLIT)):
                        buf = _chip_sum_stacked(*scat[(n, l, part)], place, l, layers_of[n], buf, f"rs_chip_sum_{n}_{l}_{part}", row0, r)
                else:
                    buf = _chip_sum(*scat[(n, l)], place, l, layers_of[n], buf, f"rs_chip_sum_{n}_{l}")
            halves.append(buf)
        return dict(zip(names, _pair_gather(halves, [cut_of[n] == "stack" for n in names], call)))

    def updated(n, rider=None):
        return _adamw(held(n, w_in[n]), g_big[n], held(n, m_in[n]), held(n, v_in[n]), f"adamw_{n}", rider=rider)

    done_early = ["sb_in_w", "sb_out_w"]
    g_big = reduced(done_early, "rs_pair_gather_sb")
    step = {}
    step["sb_in_w"], got = updated("sb_in_w", rider=_scatter_rider(last[1][:1]))
    arrived((last[0][:1], last[1][:1]), got)
    g_big.update(reduced([n for n in big_names if n not in done_early], "rs_pair_gather"))

    small_parts = [jnp.concatenate(gsmall[n], axis=0).reshape(-1) for n, _ in _SMALL]
    small_parts.append(jnp.stack(g_conv_w).reshape(-1))
    small_parts.append(loss_lanes.reshape(-1))
    small_sum = _allreduce_small(jnp.concatenate(small_parts).reshape(-1, LANES), "allreduce_small").reshape(-1)
    g_small, off = {}, 0
    for n, shape in _SMALL:
        size = math.prod(shape)
        g_small[n] = small_sum[off:off + size].reshape(shape)
        off += size
    cw_size = 2 * SSD_D_CONV * SSD_CONV_DIM
    g_cw_full = small_sum[off:off + cw_size].reshape(2, SSD_D_CONV, 4, SSD_CONV_DIM // 4)
    g_small["ssd_conv_w"] = jnp.sum(g_cw_full * (jnp.arange(4) == chip).astype(F32)[None, None, :, None], axis=2)
    loss = 0.5 * jnp.sum(small_sum[off + cw_size:]) / D_MODEL

    grads, delta, new_m, new_v = {}, {}, {}, {}
    for n in big_names:
        grads[n], delta[n], new_m[n], new_v[n] = (held(n, a) for a in (g_big[n], *(step[n] if n in step else updated(n))))
    small_names = [n for n, _ in _SMALL] + ["ssd_conv_w"]
    pack = lambda d: jnp.concatenate([d[n].reshape(-1) for n in small_names]).reshape(1, -1, LANES)
    ds, ms, vs = _adamw(pack(w_in), pack(g_small), pack(m_in), pack(v_in), "adamw_small")
    off = 0
    for n in small_names:
        shape = w_in[n].shape
        size = math.prod(shape)
        grads[n] = g_small[n]
        delta[n] = ds.reshape(-1)[off:off + size].reshape(shape)
        new_m[n] = ms.reshape(-1)[off:off + size].reshape(shape)
        new_v[n] = vs.reshape(-1)[off:off + size].reshape(shape)
        off += size

    order = ["norm_w", "ssd_in_w", "ssd_conv_w", "ssd_conv_b", "ssd_dt_bias", "ssd_a_log", "ssd_d", "ssd_gnorm_w", "ssd_out_w",
             "sb_in_w", "sb_qn_w", "sb_kn_w", "sb_out_w", "ple_norm_w", "ple_gate_w", "ple_proj_w"]
    return (loss, grad_x, *[grads[n] for n in order], *[delta[n] for n in order], *[new_m[n] for n in order],
            *[new_v[n] for n in order])
```

```python
import functools
import math

import jax
import jax.numpy as jnp
from jax import lax
from jax.experimental import pallas as pl
from jax.experimental.pallas import tpu as pltpu

F32 = jnp.float32
BF16 = jnp.bfloat16
MESH = pl.DeviceIdType.MESH

D_MODEL = 2048
DEPTH = 4
SSD_D_INNER = 4096
SSD_N_GROUPS = 8
SSD_GROUP_W = SSD_D_INNER // SSD_N_GROUPS
SSD_D_STATE = 128
SSD_CHUNK = 128
SSD_CONV_DIM = 6144
SSD_D_CONV = 4
SSD_N_HEADS = 64
SB_HEAD_DIM = 128
SB_N_HEADS = 16
SB_WIDTH = 2048
NORM_EPS = 1e-6
GATED_NORM_EPS = 1e-5
ADAM_LR = 0.001
ADAM_B1 = 0.9
ADAM_B2 = 0.999
ADAM_EPS = 1e-08
ADAM_WD = 0.01
ADAM_STEP = 10

SSD_ZX = SSD_D_INNER + SSD_CONV_DIM
SSD_IN_DIM = SSD_ZX + SSD_N_HEADS
LAST_SPLIT = 1104
LANES = 128
BF16_ROWS = 16

_BIG = (
    ("ssd_in_w", (2, 2576, 2048), "stack"),
    ("ssd_out_w", (2, 1024, 2048), "row"),
    ("sb_in_w", (2, 2048, 2048), "col"),
    ("sb_out_w", (2, 512, 2048), "row"),
    ("ple_gate_w", (4, 512, 2048), "row"),
    ("ple_proj_w", (4, 256, 512), "col"),
)
_SMALL = (
    ("norm_w", (4, 2048)),
    ("ssd_conv_b", (2, 6144)),
    ("ssd_dt_bias", (2, 64)),
    ("ssd_a_log", (2, 64)),
    ("ssd_d", (2, 64)),
    ("ssd_gnorm_w", (2, 4096)),
    ("sb_qn_w", (2, 128)),
    ("sb_kn_w", (2, 128)),
    ("ple_norm_w", (4, 2048)),
)

_DN = {
    "nn": (((1,), (0,)), ((), ())),
    "nt": (((1,), (1,)), ((), ())),
    "tn": (((0,), (0,)), ((), ())),
}


def _dot(a, b, dn="nn"):
    return lax.dot_general(a.astype(BF16), b.astype(BF16), _DN[dn], preferred_element_type=F32)


@functools.partial(jax.custom_vjp, nondiff_argnums=(2,))
def _gdot(a, b, dn):
    return _dot(a, b, dn)


def _gdot_fwd(a, b, dn):
    return _dot(a, b, dn), (a, b)


def _gdot_bwd(dn, res, g):
    a, b = res
    if dn == "nn":
        return _dot(g, b, "nt"), _dot(a, g, "tn")
    if dn == "nt":
        return _dot(g, b, "nn"), _dot(g, a, "tn")
    return _dot(b, g, "nt"), _dot(a, g, "nn")


_gdot.defvjp(_gdot_fwd, _gdot_bwd)


def _split_dot(x, t, parts, x_left):
    acc = None
    r = x
    for i in range(parts):
        p = r.astype(BF16)
        d = lax.dot_general(p, t, _DN["nn"], preferred_element_type=F32) if x_left else lax.dot_general(
            t, p, _DN["nn"], preferred_element_type=F32)
        acc = d if acc is None else acc + d
        if i + 1 < parts:
            r = r - p.astype(F32)
    return acc


def _tri(n, lower, strict=False):
    r = lax.broadcasted_iota(jnp.int32, (n, n), 0)
    c = lax.broadcasted_iota(jnp.int32, (n, n), 1)
    keep = (r > c if strict else r >= c) if lower else (r < c if strict else r <= c)
    return jnp.where(keep, 1.0, 0.0).astype(BF16)


def _cumsum_rows_raw(x):
    return _split_dot(x, _tri(x.shape[0], True), 3, False)


@jax.custom_vjp
def _cumsum_rows(x):
    return _cumsum_rows_raw(x)


def _cumsum_rows_fwd(x):
    return _cumsum_rows_raw(x), None


def _cumsum_rows_bwd(_, g):
    return (_split_dot(g, _tri(g.shape[0], False), 3, False),)


_cumsum_rows.defvjp(_cumsum_rows_fwd, _cumsum_rows_bwd)


def _sigmoid(x):
    return 1.0 / (1.0 + jnp.exp(-x))


def _softplus(x):
    return jnp.maximum(x, 0.0) + jnp.log(1.0 + jnp.exp(-jnp.abs(x)))


def _rms(x, w, eps):
    return x * lax.rsqrt(jnp.mean(x * x, axis=-1, keepdims=True) + eps) * w


_ANY = pl.BlockSpec(memory_space=pl.ANY)


def _params(*sem):
    return pltpu.CompilerParams(dimension_semantics=sem)


class _Rider:
    def __init__(self, reads, writes, n_sems, issue):
        self.reads, self.writes, self.n_sems, self.issue = list(reads), list(writes), n_sems, issue


def _pcall(body, *, grid, in_specs, out_specs, out_shape, args, sem, name, scratch_shapes=(), aliases=None, rider=None):
    aliases = dict(aliases or {})
    if rider is None:
        outs = pl.pallas_call(body, grid=grid, in_specs=in_specs, out_specs=out_specs, out_shape=out_shape,
                              scratch_shapes=list(scratch_shapes), input_output_aliases=aliases,
                              compiler_params=_params(*sem), name=name)(*args)
        return list(outs), []
    n_in, n_out, n_scr, n_rd, n_wr = len(args), len(out_shape), len(scratch_shapes), len(rider.reads), len(rider.writes)
    passed = [k for k, w in enumerate(rider.writes) if not isinstance(w, jax.ShapeDtypeStruct)]
    for pos, k in enumerate(passed):
        aliases[n_in + n_rd + pos] = n_out + k

    def wrapped(*refs):
        ins = refs[:n_in]
        reads = refs[n_in:n_in + n_rd]
        base = n_in + n_rd + len(passed)
        outs = refs[base:base + n_out]
        writes = refs[base + n_out:base + n_out + n_wr]
        scr = refs[base + n_out + n_wr:base + n_out + n_wr + n_scr]
        send, recv = refs[-2:]
        first = last = None
        for d, n in enumerate(grid):
            i = pl.program_id(d)
            first = (i == 0) if first is None else first & (i == 0)
            last = (i == n - 1) if last is None else last & (i == n - 1)

        @pl.when(first)
        def _():
            for cp in rider.issue(reads, writes, send, recv)[0]:
                cp.start()

        body(*ins, *outs, *scr)

        @pl.when(last)
        def _():
            sends, arrivals = rider.issue(reads, writes, send, recv)
            for cp in arrivals:
                cp.wait_recv()
            for cp in sends:
                cp.wait_send()

    outs = pl.pallas_call(
        wrapped, grid=grid,
        in_specs=list(in_specs) + [_ANY] * (n_rd + len(passed)),
        out_specs=list(out_specs) + [_ANY] * n_wr,
        out_shape=list(out_shape) + [jax.ShapeDtypeStruct(w.shape, w.dtype) for w in rider.writes],
        scratch_shapes=list(scratch_shapes) + [pltpu.SemaphoreType.DMA((rider.n_sems,)), pltpu.SemaphoreType.DMA((rider.n_sems,))],
        input_output_aliases=aliases, compiler_params=_params(*(["arbitrary"] * len(grid))), name=name,
    )(*args, *rider.reads, *[rider.writes[k] for k in passed])
    return list(outs[:n_out]), list(outs[n_out:])


MM_TK = 2048


def _pick(dim, pref, unit=None):
    t = pref
    while t >= LANES:
        if dim % t == 0 and (unit is None or unit % t == 0):
            return t
        t //= 2
    return dim


def _matmul(a, b, *, dn="nn", res=None, out_dtype=F32, name, b_lay=None, o_lay=None, o_buf=None, out_rows=None, rider=None):
    if dn == "tn":
        k_dim, m_dim = a.shape
    else:
        m_dim, k_dim = a.shape
    unit_m = unit_n = unit_k = None
    if b_lay is None:
        n_dim = b.shape[0] if dn == "nt" else b.shape[1]
    elif b_lay[0] == "stack":
        cut, layer, rows = b_lay
        cols = b.shape[2]
        n_dim = cols if dn == "nn" else rows
        assert k_dim == (rows if dn == "nn" else cols) and dn != "tn"
    else:
        cut, layer = b_lay
        r, c = b.shape[2:]
        rows, cols = (4 * r, c) if cut == "row" else (r, 4 * c)
        n_dim = cols if dn == "nn" else rows
        assert k_dim == (rows if dn == "nn" else cols) and dn != "tn"
        if (cut == "row") == (dn == "nn"):
            unit_k = r if cut == "row" else c
        else:
            unit_n = r if cut == "row" else c
    if o_lay is not None:
        o_cut, o_layer, o_layers = o_lay
        if o_cut == "row":
            unit_m = m_dim // 4
        else:
            unit_n = n_dim // 4
    tm, tn, tk = _pick(m_dim, 1024, unit_m), _pick(n_dim, 1024, unit_n), _pick(k_dim, MM_TK, unit_k)
    nk = k_dim // tk
    a_spec = pl.BlockSpec((tk, tm), lambda i, j, k: (k, i)) if dn == "tn" else pl.BlockSpec((tm, tk), lambda i, j, k: (i, k))
    if b_lay is None:
        b_spec = pl.BlockSpec((tn, tk), lambda i, j, k: (j, k)) if dn == "nt" else pl.BlockSpec((tk, tn), lambda i, j, k: (k, j))
    elif cut == "stack":
        b_spec = (pl.BlockSpec((None, tk, tn), lambda i, j, k: (layer, k, j)) if dn == "nn" else
                  pl.BlockSpec((None, tn, tk), lambda i, j, k: (layer, j, k)))
    elif dn == "nn" and cut == "row":
        per = r // tk
        b_spec = pl.BlockSpec((None, None, tk, tn), lambda i, j, k: (k // per, layer, k % per, j))
    elif dn == "nn":
        per = c // tn
        b_spec = pl.BlockSpec((None, None, tk, tn), lambda i, j, k: (j // per, layer, k, j % per))
    elif cut == "row":
        per = r // tn
        b_spec = pl.BlockSpec((None, None, tn, tk), lambda i, j, k: (j // per, layer, j % per, k))
    else:
        per = c // tk
        b_spec = pl.BlockSpec((None, None, tn, tk), lambda i, j, k: (k // per, layer, j, k % per))
    r_spec = pl.BlockSpec((tm, tn), lambda i, j, k: (i, j))
    if o_lay is None:
        o_spec = r_spec
        out_shape = jax.ShapeDtypeStruct((out_rows or m_dim, n_dim), out_dtype)
    elif o_cut == "row":
        per_o = unit_m // tm
        o_spec = pl.BlockSpec((None, None, tm, tn), lambda i, j, k: (i // per_o, o_layer, i % per_o, j))
        out_shape = jax.ShapeDtypeStruct((4, o_layers, unit_m, n_dim), out_dtype)
    else:
        per_o = unit_n // tn
        o_spec = pl.BlockSpec((None, None, tm, tn), lambda i, j, k: (j // per_o, o_layer, i, j % per_o))
        out_shape = jax.ShapeDtypeStruct((4, o_layers, m_dim, unit_n), out_dtype)
    has_res = res is not None
    has_buf = o_buf is not None

    def body(*refs):
        a_ref, b_ref = refs[:2]
        r_ref = refs[2] if has_res else None
        o_ref = refs[-1] if nk == 1 else refs[-2]

        def finish(v):
            if has_res:
                v = v + r_ref[...]
            o_ref[...] = v.astype(o_ref.dtype)

        if nk == 1:
            finish(_dot(a_ref[...], b_ref[...], dn))
            return
        acc_ref = refs[-1]
        k = pl.program_id(2)

        @pl.when(k == 0)
        def _():
            acc_ref[...] = jnp.zeros_like(acc_ref)

        acc_ref[...] += _dot(a_ref[...], b_ref[...], dn)

        @pl.when(k == nk - 1)
        def _():
            finish(acc_ref[...])

    args = [a, b] + ([res] if has_res else []) + ([o_buf] if has_buf else [])
    outs, rode = _pcall(
        body, grid=(m_dim // tm, n_dim // tn, nk),
        in_specs=[a_spec, b_spec] + ([r_spec] if has_res else []) + ([_ANY] if has_buf else []),
        out_specs=[o_spec], out_shape=[out_shape],
        scratch_shapes=[] if nk == 1 else [pltpu.VMEM((tm, tn), F32)],
        aliases={len(args) - 1: 0} if has_buf else {},
        args=args, sem=("parallel", "parallel", "arbitrary"), name=name, rider=rider)
    return (outs[0], rode) if rider is not None else outs[0]


def _rowcall(fn, rows, consts, outs, accs, *, name, tm=512):
    args = list(rows) + list(consts)
    in_specs = [pl.BlockSpec((tm, r.shape[1]), lambda i: (i, 0)) for r in rows]
    in_specs += [pl.BlockSpec(c.shape, lambda i: (0, 0)) for c in consts]
    s_dim = args[0].shape[0]
    n_in, n_out = len(args), len(outs)
    out_shape = [jax.ShapeDtypeStruct((s_dim, w), dt) for w, dt in outs] + [jax.ShapeDtypeStruct(s, F32) for s in accs]
    out_specs = [pl.BlockSpec((tm, w), lambda i: (i, 0)) for w, _ in outs] + [pl.BlockSpec(s, lambda i: (0, 0)) for s in accs]

    def body(*refs):
        vals = fn(*[r[...] for r in refs[:n_in]])
        o_refs = refs[n_in:n_in + n_out]
        a_refs = refs[n_in + n_out:]
        for o, v in zip(o_refs, vals[:n_out]):
            o[...] = v.astype(o.dtype)
        if a_refs:
            @pl.when(pl.program_id(0) == 0)
            def _():
                for a_ref in a_refs:
                    a_ref[...] = jnp.zeros_like(a_ref)

            for a_ref, v in zip(a_refs, vals[n_out:]):
                a_ref[...] += v

    return pl.pallas_call(
        body, grid=(s_dim // tm,), in_specs=in_specs, out_specs=out_specs, out_shape=out_shape,
        compiler_params=_params("arbitrary"), name=name,
    )(*args)


def _rms_fwd(h, w, name):
    return _rowcall(lambda x, w_: (_rms(x, w_, NORM_EPS),), [h], [w], [(h.shape[1], BF16)], [], name=name)[0]


def _rms_bwd(h, w, dy, dres, name):
    def fn(x, dy_, dres_, w_):
        _, vjp = jax.vjp(lambda a, b: _rms(a, b, NORM_EPS), x, w_)
        dx, dw = vjp(dy_)
        return dx + dres_, dw

    return _rowcall(fn, [h, dy, dres], [w], [(h.shape[1], F32)], [w.shape], name=name)


def _ple_fwd(h1, pp, gl, name):
    return _rowcall(lambda a, b, c: (a + b * _sigmoid(c),), [h1, pp, gl], [], [(h1.shape[1], F32)], [], name=name)[0]


def _ple_bwd(dh2, pp, gl, name):
    def fn(d, b, c):
        gate = _sigmoid(c)
        return d * gate, d * b * gate * (1.0 - gate)

    return _rowcall(fn, [dh2, pp, gl], [], [(dh2.shape[1], BF16), (dh2.shape[1], BF16)], [], name=name)


def _loss_bwd(y, target, name):
    width = y.shape[1]

    def fn(a, t):
        d = a - t
        col = jnp.sum(d * d, axis=0, keepdims=True)
        part = col[:, 0:LANES]
        for j in range(1, width // LANES):
            part = part + col[:, j * LANES:(j + 1) * LANES]
        return d * (1.0 / width), part

    return _rowcall(fn, [y, target], [], [(width, F32)], [(1, LANES)], name=name)


CONV_TC = 256


def _shift_down(x, j):
    if j == 0:
        return x
    row = lax.broadcasted_iota(jnp.int32, x.shape, 0)
    return jnp.where(row >= j, pltpu.roll(x, j, 0), 0.0)


def _shift_up(x, j):
    if j == 0:
        return x
    n = x.shape[0]
    row = lax.broadcasted_iota(jnp.int32, x.shape, 0)
    return jnp.where(row < n - j, pltpu.roll(x, n - j, 0), 0.0)


def _conv_fwd(pzx, cw, cb, name, rider=None):
    s_dim = pzx.shape[0]
    off = SSD_D_INNER // CONV_TC

    def body(x_ref, w_ref, b_ref, o_ref):
        x = x_ref[...]
        w = w_ref[...]
        y = b_ref[...] + w[3:4, :] * x
        for k in range(SSD_D_CONV - 1):
            y = y + w[k:k + 1, :] * _shift_down(x, SSD_D_CONV - 1 - k)
        o_ref[...] = y * _sigmoid(y)

    outs, rode = _pcall(
        body, grid=(SSD_CONV_DIM // CONV_TC,),
        in_specs=[pl.BlockSpec((s_dim, CONV_TC), lambda j: (0, off + j)), pl.BlockSpec((SSD_D_CONV, CONV_TC), lambda j: (0, j)),
                  pl.BlockSpec((1, CONV_TC), lambda j: (0, j))],
        out_specs=[pl.BlockSpec((s_dim, CONV_TC), lambda j: (0, j))],
        out_shape=[jax.ShapeDtypeStruct((s_dim, SSD_CONV_DIM), F32)],
        args=[pzx, cw, cb], sem=("parallel",), name=name, rider=rider)
    return (outs[0], rode) if rider is not None else outs[0]


def _conv_bwd(pzx, cw, cb, dxs, dbm, dcm, dzx, name):
    s_dim = pzx.shape[0]
    off = SSD_D_INNER // CONV_TC
    n_x, n_b = dxs.shape[1] // CONV_TC, dbm.shape[1] // CONV_TC

    def body(x_ref, w_ref, b_ref, dxs_ref, dbm_ref, dcm_ref, _, dx_ref, dw_ref, db_ref):
        j = pl.program_id(0)
        d = jnp.where(j < n_x, dxs_ref[...], jnp.where(j < n_x + n_b, dbm_ref[...], dcm_ref[...]))
        x = x_ref[...]
        w = w_ref[...]
        xs = [_shift_down(x, SSD_D_CONV - 1 - k) for k in range(SSD_D_CONV)]
        y = b_ref[...]
        for k in range(SSD_D_CONV):
            y = y + w[k:k + 1, :] * xs[k]
        sg = _sigmoid(y)
        dy = d * (sg * (1.0 + y * (1.0 - sg)))
        dx = w[3:4, :] * dy
        for k in range(SSD_D_CONV - 1):
            dx = dx + w[k:k + 1, :] * _shift_up(dy, SSD_D_CONV - 1 - k)
        dx_ref[...] = dx.astype(dx_ref.dtype)
        for k in range(SSD_D_CONV):
            dw_ref[k:k + 1, :] = jnp.sum(dy * xs[k], axis=0, keepdims=True)
        db_ref[...] = jnp.sum(dy, axis=0, keepdims=True)

    part = lambda lo, n: pl.BlockSpec((s_dim, CONV_TC), lambda j: (0, jnp.clip(j - lo, 0, n - 1)))
    return pl.pallas_call(
        body, grid=(SSD_CONV_DIM // CONV_TC,),
        in_specs=[pl.BlockSpec((s_dim, CONV_TC), lambda j: (0, off + j)), pl.BlockSpec((SSD_D_CONV, CONV_TC), lambda j: (0, j)),
                  pl.BlockSpec((1, CONV_TC), lambda j: (0, j)), part(0, n_x), part(n_x, n_b), part(n_x + n_b, n_b), _ANY],
        out_specs=[pl.BlockSpec((s_dim, CONV_TC), lambda j: (0, off + j)), pl.BlockSpec((SSD_D_CONV, CONV_TC), lambda j: (0, j)),
                   pl.BlockSpec((1, CONV_TC), lambda j: (0, j))],
        out_shape=[jax.ShapeDtypeStruct(dzx.shape, dzx.dtype), jax.ShapeDtypeStruct((SSD_D_CONV, SSD_CONV_DIM), F32),
                   jax.ShapeDtypeStruct((1, SSD_CONV_DIM), F32)],
        input_output_aliases={6: 0}, compiler_params=_params("arbitrary"), name=name,
    )(pzx, cw, cb, dxs, dbm, dcm, dzx)


def _ssd_step(xs, bm, cm, dtraw, bias, alog, dskip, st_in, z, gw, dot, cumsum):
    n = xs.shape[0]
    lane = lax.broadcasted_iota(jnp.int32, (1, LANES), 1)
    sub = lax.broadcasted_iota(jnp.int32, (LANES, 1), 0)
    left = (lane < 64).astype(F32)
    right = 1.0 - left
    top = (sub < 64).astype(F32)
    bot = 1.0 - top
    row = lax.broadcasted_iota(jnp.int32, (n, n), 0)
    colm = lax.broadcasted_iota(jnp.int32, (n, n), 1)
    causal = row >= colm

    dt = _softplus(dtraw + bias)
    adt = dt * (-jnp.exp(alog))
    acum = cumsum(adt)
    acum_t = acum.T
    last = jnp.sum(adt, axis=0, keepdims=True)
    scores = dot(cm, bm, "nt")

    def lane_of(v, h):
        return jnp.sum(v * (lane == h).astype(F32), axis=1, keepdims=True)

    ys, sts = [], []
    for pr in range(4):
        heads = (2 * pr, 2 * pr + 1)
        ac = [lane_of(acum, h) for h in heads]
        ar = [jnp.sum(acum_t * (sub == h).astype(F32), axis=0, keepdims=True) for h in heads]
        dth = [lane_of(dt, h) for h in heads]
        la = [lane_of(last, h) for h in heads]
        dk = [lane_of(dskip, h) for h in heads]
        x2 = xs[:, pr * LANES:(pr + 1) * LANES]
        xdt = x2 * (dth[0] * left + dth[1] * right)
        yd = None
        for i, side in enumerate((left, right)):
            decay = jnp.where(causal, jnp.exp(jnp.minimum(ac[i] - ar[i], 0.0)), 0.0)
            t = dot(scores * decay, xdt * side, "nn")
            yd = t if yd is None else yd + t
        st2 = st_in[pr * LANES:(pr + 1) * LANES, :]
        yo = dot(cm, st2, "nt") * (jnp.exp(ac[0]) * left + jnp.exp(ac[1]) * right)
        dte = jnp.exp(la[0] - ac[0]) * left + jnp.exp(la[1] - ac[1]) * right
        cs = dot(xdt * dte, bm, "tn")
        sts.append(st2 * (jnp.exp(la[0]) * top + jnp.exp(la[1]) * bot) + cs)
        ys.append(yd + yo + (dk[0] * left + dk[1] * right) * x2)
    y = jnp.concatenate(ys, axis=1)
    yg = y * (z * _sigmoid(z))
    yn = yg * lax.rsqrt(jnp.mean(yg * yg, axis=-1, keepdims=True) + GATED_NORM_EPS) * gw
    return yn, jnp.concatenate(sts, axis=0)


SSD_GB_FWD, SSD_GB_BWD = 4, 2


def _ssd_specs(n_chunks, rev, gb):
    ci = (lambda c: n_chunks - 1 - c) if rev else (lambda c: c)
    n_x = SSD_D_INNER // (gb * LANES)
    n_g = SSD_N_GROUPS // gb
    return dict(
        xs=pl.BlockSpec((SSD_CHUNK, gb * SSD_GROUP_W), lambda g, c: (ci(c), g)),
        bm=pl.BlockSpec((SSD_CHUNK, gb * LANES), lambda g, c: (ci(c), n_x + g)),
        cm=pl.BlockSpec((SSD_CHUNK, gb * LANES), lambda g, c: (ci(c), n_x + n_g + g)),
        dt=pl.BlockSpec((gb, SSD_CHUNK, LANES), lambda g, c: (g, ci(c), 0)),
        vec=pl.BlockSpec((gb, 1, LANES), lambda g, c: (g, 0, 0)),
        z=pl.BlockSpec((SSD_CHUNK, gb * SSD_GROUP_W), lambda g, c: (ci(c), g)),
        gw=pl.BlockSpec((1, gb * SSD_GROUP_W), lambda g, c: (0, g)),
        st=pl.BlockSpec((gb, None, SSD_GROUP_W, SSD_D_STATE), lambda g, c: (g, ci(c), 0, 0)),
    )


def _group_cols(k, width):
    return slice(k * width, (k + 1) * width)


def _ssd_fwd(act, dtg, bias, alog, dskip, pzx, gw, name, rider=None):
    s_dim = act.shape[0]
    n_chunks = s_dim // SSD_CHUNK
    gb = SSD_GB_FWD
    sp = _ssd_specs(n_chunks, False, gb)

    def body(xs, bm, cm, dt, b_ref, a_ref, d_ref, z, gw_ref, yn_ref, st_ref, state):
        @pl.when(pl.program_id(1) == 0)
        def _():
            state[...] = jnp.zeros_like(state)

        for k in range(gb):
            wide, lanes = _group_cols(k, SSD_GROUP_W), _group_cols(k, LANES)
            st_in = state[k]
            st_ref[k] = st_in
            yn, st_out = _ssd_step(xs[:, wide], bm[:, lanes], cm[:, lanes], dt[k], b_ref[k], a_ref[k], d_ref[k], st_in, z[:, wide],
                                   gw_ref[:, wide], _dot, _cumsum_rows_raw)
            yn_ref[:, wide] = yn.astype(yn_ref.dtype)
            state[k] = st_out

    outs, rode = _pcall(
        body, grid=(SSD_N_GROUPS // gb, n_chunks),
        in_specs=[sp["xs"], sp["bm"], sp["cm"], sp["dt"], sp["vec"], sp["vec"], sp["vec"], sp["z"], sp["gw"]],
        out_specs=[sp["xs"], sp["st"]],
        out_shape=[jax.ShapeDtypeStruct((s_dim, SSD_D_INNER), BF16),
                   jax.ShapeDtypeStruct((SSD_N_GROUPS, n_chunks, SSD_GROUP_W, SSD_D_STATE), F32)],
        scratch_shapes=[pltpu.VMEM((gb, SSD_GROUP_W, SSD_D_STATE), F32)],
        args=[act, act, act, dtg, bias, alog, dskip, pzx, gw], sem=("parallel", "arbitrary"), name=name, rider=rider)
    return (outs, rode) if rider is not None else outs


def _ssd_bwd(act, dtg, bias, alog, dskip, pzx, gw, states, dyn, name, rider=None):
    s_dim = act.shape[0]
    n_chunks = s_dim // SSD_CHUNK
    gb = SSD_GB_BWD
    sp = _ssd_specs(n_chunks, True, gb)
    rc = lambda c: n_chunks - 1 - c

    def body(xs, bm, cm, dt, b_ref, a_ref, d_ref, z, gw_ref, st_ref, dyn_ref,
             dxs_ref, dbm_ref, dcm_ref, ddt_ref, db_ref, da_ref, dd_ref, dz_ref, dgw_ref, dstate):
        first = pl.program_id(1) == 0

        @pl.when(first)
        def _():
            dstate[...] = jnp.zeros_like(dstate)
            db_ref[...] = jnp.zeros_like(db_ref)
            da_ref[...] = jnp.zeros_like(da_ref)
            dd_ref[...] = jnp.zeros_like(dd_ref)
            dgw_ref[...] = jnp.zeros_like(dgw_ref)

        fn = functools.partial(_ssd_step, dot=_gdot, cumsum=_cumsum_rows)
        for k in range(gb):
            wide, lanes = _group_cols(k, SSD_GROUP_W), _group_cols(k, LANES)
            _, vjp = jax.vjp(fn, xs[:, wide], bm[:, lanes], cm[:, lanes], dt[k], b_ref[k], a_ref[k], d_ref[k], st_ref[k], z[:, wide],
                             gw_ref[:, wide])
            dxs, dbm, dcm, ddt, db, da, dd, dst, dz, dgw = vjp((dyn_ref[:, wide], dstate[k]))
            dxs_ref[:, wide] = dxs
            dbm_ref[:, lanes] = dbm
            dcm_ref[:, lanes] = dcm
            ddt_ref[k] = ddt
            dz_ref[:, wide] = dz.astype(dz_ref.dtype)
            db_ref[k] += db
            da_ref[k] += da
            dd_ref[k] += dd
            dgw_ref[:, wide] += dgw
            dstate[k] = dst

    bc = pl.BlockSpec((SSD_CHUNK, gb * LANES), lambda g, c: (rc(c), g))
    outs, rode = _pcall(
        body, grid=(SSD_N_GROUPS // gb, n_chunks),
        in_specs=[sp["xs"], sp["bm"], sp["cm"], sp["dt"], sp["vec"], sp["vec"], sp["vec"], sp["z"], sp["gw"], sp["st"], sp["xs"]],
        out_specs=[sp["xs"], bc, bc, sp["dt"], sp["vec"], sp["vec"], sp["vec"], sp["xs"], sp["gw"]],
        out_shape=[jax.ShapeDtypeStruct((s_dim, SSD_D_INNER), F32),
                   jax.ShapeDtypeStruct((s_dim, SSD_N_GROUPS * SSD_D_STATE), F32),
                   jax.ShapeDtypeStruct((s_dim, SSD_N_GROUPS * SSD_D_STATE), F32),
                   jax.ShapeDtypeStruct((SSD_N_GROUPS, s_dim, LANES), F32),
                   jax.ShapeDtypeStruct((SSD_N_GROUPS, 1, LANES), F32),
                   jax.ShapeDtypeStruct((SSD_N_GROUPS, 1, LANES), F32),
                   jax.ShapeDtypeStruct((SSD_N_GROUPS, 1, LANES), F32),
                   jax.ShapeDtypeStruct((s_dim, SSD_ZX), BF16),
                   jax.ShapeDtypeStruct((1, SSD_D_INNER), F32)],
        scratch_shapes=[pltpu.VMEM((gb, SSD_GROUP_W, SSD_D_STATE), F32)],
        args=[act, act, act, dtg, bias, alog, dskip, pzx, gw, states, dyn], sem=("arbitrary", "arbitrary"), name=name, rider=rider)
    return (outs, rode) if rider is not None else outs


SB_T = 128
SB_GROUP = 8
SB_WIDE = SB_GROUP * SB_T
SB_HB = 4
SB_SCALE = 1.0 / math.sqrt(SB_HEAD_DIM)


def _qknorm_fwd(proj, qw, kw, name, tm=512):
    s_dim = proj.shape[0]

    def body(q_ref, k_ref, v_ref, qw_ref, kw_ref, qo, ko, vo):
        for hh in range(SB_HB):
            qo[:, _head_lanes(hh)] = _rms(q_ref[:, _head_lanes(hh)], qw_ref[...], NORM_EPS).astype(BF16)
            ko[:, _head_lanes(hh)] = _rms(k_ref[:, _head_lanes(hh)], kw_ref[...], NORM_EPS).astype(BF16)
        vo[...] = v_ref[...].astype(BF16)

    groups = SB_N_HEADS // SB_HB
    blk = lambda o: pl.BlockSpec((tm, SB_HB * SB_HEAD_DIM), lambda i, h: (i, o + h))
    vec = pl.BlockSpec((1, SB_HEAD_DIM), lambda i, h: (0, 0))
    return pl.pallas_call(
        body, grid=(s_dim // tm, groups),
        in_specs=[blk(0), blk(groups), blk(2 * groups), vec, vec],
        out_specs=[blk(0)] * 3,
        out_shape=[jax.ShapeDtypeStruct((s_dim, SB_WIDTH), BF16)] * 3,
        compiler_params=_params("parallel", "parallel"), name=name,
    )(proj, proj, proj, qw, kw)


def _qknorm_bwd(proj, qw, kw, dqn, dkn, name, tm=512):
    s_dim = proj.shape[0]

    def body(q_ref, k_ref, dq_ref, dk_ref, qw_ref, kw_ref, dqo, dko, dqw, dkw):
        @pl.when((pl.program_id(0) == 0) & (pl.program_id(1) == 0))
        def _():
            dqw[...] = jnp.zeros_like(dqw)
            dkw[...] = jnp.zeros_like(dkw)

        fn = lambda a, b: _rms(a, b, NORM_EPS)
        for hh in range(SB_HB):
            lanes = _head_lanes(hh)
            for x_ref, w_ref, d_ref, dx_out, dw_out in ((q_ref, qw_ref, dq_ref, dqo, dqw), (k_ref, kw_ref, dk_ref, dko, dkw)):
                _, vjp = jax.vjp(fn, x_ref[:, lanes], w_ref[...])
                dx, dw = vjp(d_ref[:, lanes])
                dx_out[:, lanes] = dx.astype(BF16)
                dw_out[...] += dw

    groups = SB_N_HEADS // SB_HB
    blk = lambda o: pl.BlockSpec((tm, SB_HB * SB_HEAD_DIM), lambda i, h: (i, o + h))
    vec = pl.BlockSpec((1, SB_HEAD_DIM), lambda i, h: (0, 0))
    return pl.pallas_call(
        body, grid=(s_dim // tm, groups),
        in_specs=[blk(0), blk(groups), blk(0), blk(0), vec, vec],
        out_specs=[blk(0), blk(0), vec, vec],
        out_shape=[jax.ShapeDtypeStruct((s_dim, SB_WIDTH), BF16)] * 2 + [jax.ShapeDtypeStruct((1, SB_HEAD_DIM), F32)] * 2,
        compiler_params=_params("arbitrary", "arbitrary"), name=name,
    )(proj, proj, dqn, dkn, qw, kw)


def _sb_logits(q, k, strict):
    z = _dot(q, k, "nt") * SB_SCALE
    lb = jnp.minimum(z, 0.0) - jnp.log(1.0 + jnp.exp(-jnp.abs(z)))
    lm = lb - z
    if strict is not None:
        lm = jnp.where(strict, lm, 0.0)
    return lb, lm


def _sb_strict(qi, grp):
    r = lax.broadcasted_iota(jnp.int32, (SB_T, SB_WIDE), 0) + qi * SB_T
    c = lax.broadcasted_iota(jnp.int32, (SB_T, SB_WIDE), 1) + grp * SB_WIDE
    return c < r


def _head_lanes(hh):
    return slice(hh * SB_HEAD_DIM, (hh + 1) * SB_HEAD_DIM)


def _sb_fwd(qn, kn, vb, proj, name, rider=None):
    s_dim = qn.shape[0]
    nq = s_dim // SB_T
    assert nq % SB_GROUP == 0

    def body(q_ref, k_ref, v_ref, g_ref, og_ref, o_ref, t_ref):
        qi = pl.program_id(1)
        top = qi // SB_GROUP
        after = _tri(SB_T, True, strict=True)
        qs = [q_ref[:, _head_lanes(hh)] for hh in range(SB_HB)]

        def step(grp, masked, carries):
            start = pl.multiple_of(grp * SB_WIDE, SB_WIDE)
            strict = _sb_strict(qi, grp) if masked else None
            out = []
            for hh in range(SB_HB):
                o_acc, cr = carries[hh]
                k = k_ref[pl.ds(start, SB_WIDE), _head_lanes(hh)]
                v = v_ref[pl.ds(start, SB_WIDE), _head_lanes(hh)]
                lb, lm = _sb_logits(qs[hh], k, strict)
                rest = [None] * SB_GROUP
                for t in reversed(range(SB_GROUP)):
                    lm_t = lm[:, t * SB_T:(t + 1) * SB_T]
                    rest[t] = cr + _split_dot(lm_t, after, 2, True)
                    cr = cr + jnp.sum(lm_t, axis=1, keepdims=True)
                a = jnp.exp(lb + jnp.concatenate(rest, axis=1))
                if masked:
                    a = jnp.where(strict, a, 0.0)
                out.append((o_acc + _dot(a, v), cr))
            return tuple(out)

        init = tuple((jnp.zeros((SB_T, SB_HEAD_DIM), F32), jnp.zeros((SB_T, 1), F32)) for _ in range(SB_HB))
        carries = step(top, True, init)
        carries = lax.fori_loop(0, top, lambda i, c: step(top - 1 - i, False, c), carries)
        for hh in range(SB_HB):
            o, tot = carries[hh]
            g = g_ref[:, _head_lanes(hh)]
            o_ref[:, _head_lanes(hh)] = o
            og_ref[:, _head_lanes(hh)] = (o * (g * _sigmoid(g))).astype(og_ref.dtype)
            t_ref[hh] = jnp.broadcast_to(tot, (SB_T, LANES))

    wide = SB_HB * SB_HEAD_DIM
    qb = pl.BlockSpec((SB_T, wide), lambda h, i: (i, h))
    kv = pl.BlockSpec((s_dim, wide), lambda h, i: (0, h))
    outs, rode = _pcall(
        body, grid=(SB_N_HEADS // SB_HB, nq),
        in_specs=[qb, kv, kv, pl.BlockSpec((SB_T, wide), lambda h, i: (i, 3 * SB_N_HEADS // SB_HB + h))],
        out_specs=[qb, qb, pl.BlockSpec((SB_HB, SB_T, LANES), lambda h, i: (h, i, 0))],
        out_shape=[jax.ShapeDtypeStruct((s_dim, SB_WIDTH), BF16), jax.ShapeDtypeStruct((s_dim, SB_WIDTH), F32),
                   jax.ShapeDtypeStruct((SB_N_HEADS, s_dim, LANES), F32)],
        args=[qn, kn, vb, proj], sem=("parallel", "arbitrary"), name=name, rider=rider)
    return (outs, rode) if rider is not None else outs


def _sb_bwd(qn, kn, vb, proj, o, tot, dog, name, rider=None):
    s_dim = qn.shape[0]
    nq = s_dim // SB_T
    assert nq % SB_GROUP == 0

    def body(q_ref, k_ref, v_ref, g_ref, o_ref, t_ref, dog_ref, dq_ref, dk_ref, dv_ref, dvb_ref, dg_ref):
        qi = pl.program_id(1)
        top = qi // SB_GROUP

        @pl.when(qi == 0)
        def _():
            dk_ref[...] = jnp.zeros_like(dk_ref)
            dv_ref[...] = jnp.zeros_like(dv_ref)

        after = _tri(SB_T, True, strict=True)
        before = _tri(SB_T, False, strict=True)
        qs, dos, totals = [], [], []
        for hh in range(SB_HB):
            g = g_ref[:, _head_lanes(hh)]
            sg = _sigmoid(g)
            dog_v = dog_ref[:, _head_lanes(hh)]
            dg_ref[:, _head_lanes(hh)] = (dog_v * o_ref[:, _head_lanes(hh)] * (sg * (1.0 + g * (1.0 - sg)))).astype(dg_ref.dtype)
            dos.append((dog_v * (g * sg)).astype(BF16))
            qs.append(q_ref[:, _head_lanes(hh)])
            totals.append(t_ref[hh][:, 0:1])

        def step(grp, masked, carries):
            start = pl.multiple_of(grp * SB_WIDE, SB_WIDE)
            strict = _sb_strict(qi, grp) if masked else None
            out = []
            for hh in range(SB_HB):
                dq_acc, cp, ce = carries[hh]
                q, do = qs[hh], dos[hh]
                k = k_ref[pl.ds(start, SB_WIDE), _head_lanes(hh)]
                v = v_ref[pl.ds(start, SB_WIDE), _head_lanes(hh)]
                lb, lm = _sb_logits(q, k, strict)
                rest = []
                for t in range(SB_GROUP):
                    lm_t = lm[:, t * SB_T:(t + 1) * SB_T]
                    cp = cp + jnp.sum(lm_t, axis=1, keepdims=True)
                    rest.append((totals[hh] - cp) + _split_dot(lm_t, after, 2, True))
                a = jnp.exp(lb + jnp.concatenate(rest, axis=1))
                if masked:
                    a = jnp.where(strict, a, 0.0)
                e = a * _dot(do, v, "nt")
                excl = []
                for t in range(SB_GROUP):
                    e_t = e[:, t * SB_T:(t + 1) * SB_T]
                    excl.append(ce + _split_dot(e_t, before, 1, True))
                    ce = ce + jnp.sum(e_t, axis=1, keepdims=True)
                eex = jnp.concatenate(excl, axis=1)
                if masked:
                    eex = jnp.where(strict, eex, 0.0)
                sig = jnp.exp(lb)
                dz = (e * (1.0 - sig) - eex * sig) * SB_SCALE
                dv_ref[pl.ds(start, SB_WIDE), _head_lanes(hh)] += _dot(a, do, "tn")
                dk_ref[pl.ds(start, SB_WIDE), _head_lanes(hh)] += _dot(dz, q, "tn")
                out.append((dq_acc + _dot(dz, k), cp, ce))
            return tuple(out)

        zero = jnp.zeros((SB_T, 1), F32)
        init = tuple((jnp.zeros((SB_T, SB_HEAD_DIM), F32), zero, zero) for _ in range(SB_HB))
        carries = lax.fori_loop(0, top, lambda i, c: step(i, False, c), init)
        carries = step(top, True, carries)
        for hh in range(SB_HB):
            dq_ref[:, _head_lanes(hh)] = carries[hh][0]

        @pl.when(qi == nq - 1)
        def _():
            dvb_ref[...] = dv_ref[...].astype(BF16)

    wide = SB_HB * SB_HEAD_DIM
    qb = pl.BlockSpec((SB_T, wide), lambda h, i: (i, h))
    kv = pl.BlockSpec((s_dim, wide), lambda h, i: (0, h))
    outs, rode = _pcall(
        body, grid=(SB_N_HEADS // SB_HB, nq),
        in_specs=[qb, kv, kv, pl.BlockSpec((SB_T, wide), lambda h, i: (i, 3 * SB_N_HEADS // SB_HB + h)), qb,
                  pl.BlockSpec((SB_HB, SB_T, LANES), lambda h, i: (h, i, 0)), qb],
        out_specs=[qb, kv, kv, kv, qb],
        out_shape=[jax.ShapeDtypeStruct((s_dim, SB_WIDTH), F32), jax.ShapeDtypeStruct((s_dim, SB_WIDTH), F32),
                   jax.ShapeDtypeStruct((s_dim, SB_WIDTH), F32), jax.ShapeDtypeStruct((s_dim, SB_WIDTH), BF16),
                   jax.ShapeDtypeStruct((s_dim, SB_WIDTH), BF16)],
        args=[qn, kn, vb, proj, o, tot, dog], sem=("parallel", "arbitrary"), name=name, rider=rider)
    return (outs, rode) if rider is not None else outs


def _adamw_math(w, g, m, v):
    m = ADAM_B1 * m + (1.0 - ADAM_B1) * g
    v = ADAM_B2 * v + (1.0 - ADAM_B2) * (g * g)
    m_hat = m / (1.0 - ADAM_B1 ** ADAM_STEP)
    v_hat = v / (1.0 - ADAM_B2 ** ADAM_STEP)
    delta = -ADAM_LR * (m_hat / (jnp.sqrt(v_hat) + ADAM_EPS) + ADAM_WD * w)
    return delta, m, v


def _row_block(rows, cols, itemsize=4, limit=1 << 20):
    tr = rows
    while tr * cols * itemsize > limit and tr % (2 * BF16_ROWS) == 0:
        tr //= 2
    return tr


def _divisor_block(rows, cols, itemsize=4, limit=2 << 20):
    best = BF16_ROWS
    for t in range(BF16_ROWS, rows + 1, BF16_ROWS):
        if rows % t == 0 and t * cols * itemsize <= limit:
            best = t
    return best


def _adamw(w, g, m, v, name, rider=None):
    n, rows, cols = w.shape
    tr = rows if rows * cols * 4 <= (2 << 20) else _divisor_block(rows, cols)

    def body(w_ref, g_ref, m_ref, v_ref, d_out, m_out, v_out):
        d, m_new, v_new = _adamw_math(w_ref[...], g_ref[...], m_ref[...], v_ref[...])
        d_out[...] = d
        m_out[...] = m_new
        v_out[...] = v_new

    blk = pl.BlockSpec((None, tr, cols), lambda i, j: (i, j, 0))
    outs, rode = _pcall(
        body, grid=(n, rows // tr), in_specs=[blk] * 4, out_specs=[blk] * 3,
        out_shape=[jax.ShapeDtypeStruct(w.shape, F32)] * 3,
        args=[w, g, m, v], sem=("parallel", "parallel"), name=name, rider=rider)
    return (outs, rode) if rider is not None else outs


_FLIPS = ((1, 0), (0, 1), (1, 1))


def _place():
    return lax.axis_index("x"), lax.axis_index("y"), lax.axis_index("c")


def _flip(v, f):
    return 1 - v if f else v


def _half_rows(ref, lead, hc, hr, sub=(0, 1)):
    part = hr // sub[1]
    return ref.at[(*lead, pl.ds(pl.multiple_of(hc * hr + sub[0] * part, BF16_ROWS), part), slice(None))]


def _half_cols(ref, lead, hc, hw):
    return ref.at[(*lead, pl.ds(pl.multiple_of(hc * hw, LANES), hw))]


def _rows_of_chip(chip, r):
    return pl.ds(pl.multiple_of(chip * r, BF16_ROWS), r)


def _slot_half(gathered, shard_shape, chip, l, hc, sub=(0, 1)):
    r, c = shard_shape[1:]
    if len(gathered.shape) == 3:
        assert sub == (0, 1)
        return _half_cols(gathered, (l, _rows_of_chip(chip, r)), hc, c // 2)
    return _half_rows(gathered, (chip, l), hc, r // 2, sub)


def _shard_half(shard, stacked, l, hc, sub=(0, 1)):
    r, c = shard.shape[1:]
    if stacked:
        assert sub == (0, 1)
        return _half_cols(shard, (l, slice(None)), hc, c // 2)
    return _half_rows(shard, (l,), hc, r // 2, sub)


def _piece(piece):
    return piece[0], piece[1], tuple(piece[2:]) or (0, 1)


def _remote(src, dst, send, recv, k, to):
    return pltpu.make_async_remote_copy(src_ref=src, dst_ref=dst, send_sem=send.at[k], recv_sem=recv.at[k], device_id=to,
                                        device_id_type=MESH)


def _comm_call(reads, writes, n_sems, phases, name):
    passed = [k for k, w in enumerate(writes) if not isinstance(w, jax.ShapeDtypeStruct)]
    n_rd = len(reads)

    def body(*refs):
        rd = refs[:n_rd]
        wr = refs[n_rd + len(passed):n_rd + len(passed) + len(writes)]
        send, recv = refs[-2:]
        for phase in phases:
            sends, arrivals = phase(rd, wr, send, recv)
            for cp in sends:
                cp.start()
            for cp in arrivals:
                cp.wait_recv()
            for cp in sends:
                cp.wait_send()

    return pl.pallas_call(
        body, in_specs=[_ANY] * (n_rd + len(passed)), out_specs=[_ANY] * len(writes),
        out_shape=[jax.ShapeDtypeStruct(w.shape, w.dtype) for w in writes],
        input_output_aliases={n_rd + pos: k for pos, k in enumerate(passed)},
        scratch_shapes=[pltpu.SemaphoreType.DMA((n_sems,)), pltpu.SemaphoreType.DMA((n_sems,))], name=name,
    )(*reads, *[writes[k] for k in passed])


def _ag_ici(pieces, names, base=0):
    def phase(shards, gathered, send, recv):
        x, y, c = _place()
        me = 2 * x + y
        sends, arrivals = [], []
        for k, piece in enumerate(pieces):
            n, l, sub = _piece(piece)
            a = names.index(n)
            shape = shards[a].shape
            src = _shard_half(shards[a], len(gathered[a].shape) == 3, l, c, sub)
            for j, (fx, fy) in enumerate(_FLIPS):
                tx, ty = _flip(x, fx), _flip(y, fy)
                sends.append(_remote(src, _slot_half(gathered[a], shape, me, l, c, sub), send, recv, base + 3 * k + j, (tx, ty, c)))
                arrivals.append(_remote(src, _slot_half(gathered[a], shape, 2 * tx + ty, l, c, sub), send, recv, base + 3 * k + j,
                                        (tx, ty, c)))
        return sends, arrivals

    return phase


def _ag_pass_on(pieces, names, shapes, base=0):
    def phase(_, gathered, send, recv):
        x, y, c = _place()
        sibling = (x, y, 1 - c)
        sends, arrivals = [], []
        for k, piece in enumerate(pieces):
            n, l, sub = _piece(piece)
            a = names.index(n)
            for j, (fx, fy) in enumerate(_FLIPS):
                chip = 2 * _flip(x, fx) + _flip(y, fy)
                landed = _slot_half(gathered[a], shapes[a], chip, l, c, sub)
                sends.append(_remote(landed, landed, send, recv, base + 3 * k + j, sibling))
                arrivals.append(_remote(landed, _slot_half(gathered[a], shapes[a], chip, l, 1 - c, sub), send, recv, base + 3 * k + j, sibling))
        return sends, arrivals

    return phase


def _other_half(ref, hc):
    if len(ref.shape) == 3:
        return _half_cols(ref, (slice(None), slice(None)), hc, ref.shape[2] // 2)
    return _half_rows(ref, (slice(None), slice(None)), hc, ref.shape[2] // 2)


def _half_shape(shape):
    return shape[:2] + (shape[2] // 2,) if len(shape) == 3 else shape[:2] + (shape[2] // 2, shape[3])


def _exchange_phase(n_arr):
    def phase(ins, outs, send, recv):
        x, y, c = _place()
        cps = [_remote(_other_half(ins[a], 1 - c), outs[a], send, recv, a, (x, y, 1 - c)) for a in range(n_arr)]
        return cps, cps

    return phase


def _exchange_outs(grads):
    return [jax.ShapeDtypeStruct(_half_shape(g.shape), g.dtype) for g in grads]


def _pair_exchange(grads, name):
    return _comm_call(grads, _exchange_outs(grads), len(grads), [_exchange_phase(len(grads))], name)


def _exchange_rider(grads):
    return _Rider(grads, _exchange_outs(grads), len(grads), _exchange_phase(len(grads)))


def _pair_sum_stacked(g, got, place, name):
    _, rows, hw = got.shape
    tr = _divisor_block(rows, hw)

    def body(place_ref, g_ref, r_ref, o_ref):
        o_ref[...] = (g_ref[...].astype(F32) + r_ref[...].astype(F32)).astype(o_ref.dtype)

    blk = pl.BlockSpec((None, tr, hw), lambda i, pr: (0, i, 0))
    return pl.pallas_call(
        body,
        grid_spec=pltpu.PrefetchScalarGridSpec(
            num_scalar_prefetch=1, grid=(rows // tr,),
            in_specs=[pl.BlockSpec((None, tr, hw), lambda i, pr: (0, i, pr[1])), blk], out_specs=blk),
        out_shape=jax.ShapeDtypeStruct(got.shape, BF16),
        compiler_params=_params("parallel"), name=name,
    )(place, g, got)


def _pair_sum(g, got, place, name):
    if len(g.shape) == 3:
        return _pair_sum_stacked(g, got, place, name)
    _, layers, hr, cols = got.shape
    tr = _row_block(hr, cols, limit=2 << 20)
    per = hr // tr

    def body(place_ref, g_ref, r_ref, o_ref):
        o_ref[...] = (g_ref[...].astype(F32) + r_ref[...].astype(F32)).astype(o_ref.dtype)

    blk = pl.BlockSpec((None, None, tr, cols), lambda k, l, i, pr: (k, l, i, 0))
    return pl.pallas_call(
        body,
        grid_spec=pltpu.PrefetchScalarGridSpec(
            num_scalar_prefetch=1, grid=(4, layers, per),
            in_specs=[pl.BlockSpec((None, None, tr, cols), lambda k, l, i, pr: (k, l, pr[1] * per + i, 0)), blk],
            out_specs=blk),
        out_shape=jax.ShapeDtypeStruct(got.shape, BF16),
        compiler_params=_params("parallel", "parallel", "parallel"), name=name,
    )(place, g, got)


def _scatter_phase(n_arr):
    def phase(ins, outs, send, recv):
        x, y, c = _place()
        cps = []
        for a in range(n_arr):
            for j, (fx, fy) in enumerate(_FLIPS):
                tx, ty = _flip(x, fx), _flip(y, fy)
                if len(ins[a].shape) == 3:
                    src = ins[a].at[:, _rows_of_chip(2 * tx + ty, ins[a].shape[1] // 4), :]
                else:
                    src = ins[a].at[2 * tx + ty]
                cps.append(_remote(src, outs[a].at[j], send, recv, 3 * a + j, (tx, ty, c)))
        return cps, cps

    return phase


def _scatter_outs(pairs):
    return [jax.ShapeDtypeStruct((3, 1, p.shape[1] // 4, p.shape[2]) if len(p.shape) == 3 else (3,) + p.shape[1:], p.dtype) for p in pairs]


def _chip_scatter(pairs, name):
    return _comm_call(pairs, _scatter_outs(pairs), 3 * len(pairs), [_scatter_phase(len(pairs))], name)


def _scatter_rider(pairs):
    return _Rider(pairs, _scatter_outs(pairs), 3 * len(pairs), _scatter_phase(len(pairs)))


def _chip_sum_stacked(p, got, place, layer, layers, o_buf, name, row0=0, rows=None):
    _, r, hw = got.shape[1:]
    rows = rows or r
    tr = _divisor_block(math.gcd(r, row0) if row0 else r, hw)
    per = r // tr
    first = row0 // tr

    def body(place_ref, p_ref, r_ref, *rest):
        o_ref = rest[-1]
        acc = p_ref[...].astype(F32)
        for j in range(3):
            acc = acc + r_ref[j].astype(F32)
        o_ref[...] = acc

    has_buf = o_buf is not None
    return pl.pallas_call(
        body,
        grid_spec=pltpu.PrefetchScalarGridSpec(
            num_scalar_prefetch=1, grid=(per,),
            in_specs=[pl.BlockSpec((None, tr, hw), lambda i, pr: (0, pr[0] * per + i, 0)),
                      pl.BlockSpec((3, None, tr, hw), lambda i, pr: (0, 0, i, 0))] + ([_ANY] if has_buf else []),
            out_specs=pl.BlockSpec((None, tr, hw), lambda i, pr: (layer, first + i, pr[1]))),
        out_shape=jax.ShapeDtypeStruct((layers, rows, 2 * hw), F32),
        input_output_aliases={3: 0} if has_buf else {},
        compiler_params=_params("parallel"), name=name,
    )(*((place, p, got) + ((o_buf,) if has_buf else ())))


def _chip_sum(p, got, place, layer, layers, o_buf, name):
    if len(p.shape) == 3:
        return _chip_sum_stacked(p, got, place, layer, layers, o_buf, name)
    _, _, hr, cols = p.shape
    tr = _row_block(hr, cols, limit=2 << 20)
    per = hr // tr

    def body(place_ref, p_ref, r_ref, *rest):
        o_ref = rest[-1]
        acc = p_ref[...].astype(F32)
        for j in range(3):
            acc = acc + r_ref[j].astype(F32)
        o_ref[...] = acc

    has_buf = o_buf is not None
    return pl.pallas_call(
        body,
        grid_spec=pltpu.PrefetchScalarGridSpec(
            num_scalar_prefetch=1, grid=(per,),
            in_specs=[pl.BlockSpec((None, None, tr, cols), lambda i, pr: (pr[0], 0, i, 0)),
                      pl.BlockSpec((3, None, tr, cols), lambda i, pr: (0, 0, i, 0))] + ([_ANY] if has_buf else []),
            out_specs=pl.BlockSpec((None, tr, cols), lambda i, pr: (layer, pr[1] * per + i, 0))),
        out_shape=jax.ShapeDtypeStruct((layers, 2 * hr, cols), F32),
        input_output_aliases={3: 0} if has_buf else {},
        compiler_params=_params("parallel"), name=name,
    )(*((place, p, got) + ((o_buf,) if has_buf else ())))


def _pair_gather(halves, by_cols, name):
    def phase(_, bufs, send, recv):
        x, y, c = _place()
        sends, arrivals = [], []
        for a, h in enumerate(halves):
            cut = (lambda hc, a=a, h=h: _half_cols(bufs[a], (slice(None), slice(None)), hc, h.shape[2] // 2)) if by_cols[a] else (
                lambda hc, a=a, h=h: _half_rows(bufs[a], (slice(None),), hc, h.shape[1] // 2))
            sends.append(_remote(cut(c), cut(c), send, recv, a, (x, y, 1 - c)))
            arrivals.append(_remote(cut(c), cut(1 - c), send, recv, a, (x, y, 1 - c)))
        return sends, arrivals

    return _comm_call([], halves, len(halves), [phase], name)


def _allreduce_small(v, name):
    rows, cols = v.shape

    def body(v_ref, o_ref, buf, send_sems, recv_sems):
        x, y, c = _place()
        me = 4 * x + 2 * y + c
        buf[0] = v_ref[...]
        cps = []
        for k in range(1, 8):
            kx, ky, kc = (k >> 2) & 1, (k >> 1) & 1, k & 1
            cp = pltpu.make_async_remote_copy(src_ref=v_ref, dst_ref=buf.at[k], send_sem=send_sems.at[k - 1], recv_sem=recv_sems.at[k - 1],
                                              device_id=(_flip(x, kx), _flip(y, ky), _flip(c, kc)), device_id_type=MESH)
            cp.start()
            cps.append(cp)
        for cp in cps:
            cp.wait()
        acc = buf[me]
        for d in range(1, 8):
            acc = acc + buf[jnp.bitwise_xor(d, me)]
        o_ref[...] = acc

    vm = pl.BlockSpec(memory_space=pltpu.VMEM)
    return pl.pallas_call(
        body, in_specs=[vm], out_specs=vm, out_shape=jax.ShapeDtypeStruct((rows, cols), F32),
        scratch_shapes=[pltpu.VMEM((8, rows, cols), F32), pltpu.SemaphoreType.DMA((7,)), pltpu.SemaphoreType.DMA((7,))],
        name=name,
    )(v)


def _pad_lanes(a):
    return jnp.pad(a, ((0, 0), (0, LANES - a.shape[1])))


def _group_lanes(v):
    return jnp.pad(v.reshape(SSD_N_GROUPS, 1, 8), ((0, 0), (0, 0), (0, LANES - 8)))


def kernel(x, p, norm_w, ssd_in_w, ssd_conv_w, ssd_conv_b, ssd_dt_bias, ssd_a_log, ssd_d, ssd_gnorm_w, ssd_out_w, sb_in_w, sb_qn_w, sb_kn_w, sb_out_w, ple_norm_w, ple_gate_w, ple_proj_w, loss_target, m_norm_w, m_ssd_in_w, m_ssd_conv_w, m_ssd_conv_b, m_ssd_dt_bias, m_ssd_a_log, m_ssd_d, m_ssd_gnorm_w, m_ssd_out_w, m_sb_in_w, m_sb_qn_w, m_sb_kn_w, m_sb_out_w, m_ple_norm_w, m_ple_gate_w, m_ple_proj_w, v_norm_w, v_ssd_in_w, v_ssd_conv_w, v_ssd_conv_b, v_ssd_dt_bias, v_ssd_a_log, v_ssd_d, v_ssd_gnorm_w, v_ssd_out_w, v_sb_in_w, v_sb_qn_w, v_sb_kn_w, v_sb_out_w, v_ple_norm_w, v_ple_gate_w, v_ple_proj_w):
    w_in = dict(norm_w=norm_w, ssd_in_w=ssd_in_w, ssd_conv_w=ssd_conv_w, ssd_conv_b=ssd_conv_b, ssd_dt_bias=ssd_dt_bias,
                ssd_a_log=ssd_a_log, ssd_d=ssd_d, ssd_gnorm_w=ssd_gnorm_w, ssd_out_w=ssd_out_w, sb_in_w=sb_in_w, sb_qn_w=sb_qn_w,
                sb_kn_w=sb_kn_w, sb_out_w=sb_out_w, ple_norm_w=ple_norm_w, ple_gate_w=ple_gate_w, ple_proj_w=ple_proj_w)
    m_in = dict(norm_w=m_norm_w, ssd_in_w=m_ssd_in_w, ssd_conv_w=m_ssd_conv_w, ssd_conv_b=m_ssd_conv_b, ssd_dt_bias=m_ssd_dt_bias,
                ssd_a_log=m_ssd_a_log, ssd_d=m_ssd_d, ssd_gnorm_w=m_ssd_gnorm_w, ssd_out_w=m_ssd_out_w, sb_in_w=m_sb_in_w,
                sb_qn_w=m_sb_qn_w, sb_kn_w=m_sb_kn_w, sb_out_w=m_sb_out_w, ple_norm_w=m_ple_norm_w, ple_gate_w=m_ple_gate_w,
                ple_proj_w=m_ple_proj_w)
    v_in = dict(norm_w=v_norm_w, ssd_in_w=v_ssd_in_w, ssd_conv_w=v_ssd_conv_w, ssd_conv_b=v_ssd_conv_b, ssd_dt_bias=v_ssd_dt_bias,
                ssd_a_log=v_ssd_a_log, ssd_d=v_ssd_d, ssd_gnorm_w=v_ssd_gnorm_w, ssd_out_w=v_ssd_out_w, sb_in_w=v_sb_in_w,
                sb_qn_w=v_sb_qn_w, sb_kn_w=v_sb_kn_w, sb_out_w=v_sb_out_w, ple_norm_w=v_ple_norm_w, ple_gate_w=v_ple_gate_w,
                ple_proj_w=v_ple_proj_w)
    ix, iy, ic = lax.axis_index("x"), lax.axis_index("y"), lax.axis_index("c")
    chip = (2 * ix + iy).astype(jnp.int32)
    place = jnp.stack([chip, ic.astype(jnp.int32)])
    zero = jnp.zeros((), jnp.int32)
    big_names = [n for n, _, _ in _BIG]
    layers_of = {n: s[0] for n, s, _ in _BIG}
    cut_of = {n: cut for n, _, cut in _BIG}

    def layer_pieces(i):
        mixer = ("ssd_in_w", "ssd_out_w") if i % 2 == 0 else ("sb_in_w", "sb_out_w")
        return [(mixer[0], i // 2), (mixer[1], i // 2), ("ple_gate_w", i), ("ple_proj_w", i)]

    def names_of(pieces):
        return [n for n in big_names if any(n == q[0] for q in pieces)]

    held = lambda n, a: a.transpose(0, 2, 1) if cut_of[n] == "stack" else a
    mine = {n: held(n, w_in[n]).astype(BF16) for n in big_names}
    shard_shapes = [mine[n].shape for n in big_names]
    room = [jax.ShapeDtypeStruct((s[0], 4 * s[1], s[2]) if cut_of[n] == "stack" else (4,) + s, BF16) for n, s in zip(big_names, shard_shapes)]
    first = layer_pieces(0)[:1]
    gathered = _comm_call([mine[n] for n in big_names], room, 6 * len(first),
                          [_ag_ici(first, big_names), _ag_pass_on(first, big_names, shard_shapes, base=3 * len(first))], "allgather_layer0")
    gw = {}
    for n, g in zip(big_names, gathered):
        if cut_of[n] == "stack":
            layers, r, c = mine[n].shape
            gw[n] = lax.dynamic_update_slice(g.reshape(layers, 4, r, c), mine[n][:, None], (zero, chip, zero, zero)).reshape(g.shape)
        else:
            gw[n] = lax.dynamic_update_slice(g, mine[n][None], (chip, zero, zero, zero))

    lp = [layer_pieces(i) for i in range(DEPTH)]
    in3 = [lp[3][0] + (k, 2) for k in range(2)]
    carries = {
        "ssd_in_0": (lp[0][1:2], []), "conv_0": (lp[0][2:], lp[0][1:2]), "ssd_0": (lp[1][:1], lp[0][2:]),
        "ssd_out_0": (lp[1][1:2], lp[1][:1]), "sb_in_1": (lp[1][2:], lp[1][1:2]), "sb_1": (lp[2][:2], lp[1][2:]),
        "sb_out_1": (lp[2][2:], lp[2][:2]), "ssd_in_2": ([in3[0]], lp[2][2:]), "conv_2": (lp[3][1:2], [in3[0]]),
        "ssd_2": ([in3[1]] + lp[3][2:], lp[3][1:2]), "ssd_out_2": ([], [in3[1]]), "sb_in_3": ([], lp[3][2:]),
    }

    def gather_rider(call):
        if call not in carries:
            return None, lambda outs: outs
        ici, passing = carries[call]
        names = names_of(ici + passing)
        phases = ([_ag_ici(ici, names)] if ici else []) + (
            [_ag_pass_on(passing, names, [mine[n].shape for n in names], base=3 * len(ici))] if passing else [])

        def issue(rd, wr, send, recv):
            both = [ph(rd, wr, send, recv) for ph in phases]
            return sum((b[0] for b in both), []), sum((b[1] for b in both), [])

        def land(outs):
            outs, bufs = outs
            for n, g in zip(names, bufs):
                gw[n] = g
            return outs

        return _Rider([mine[n] for n in names], [gw[n] for n in names], 3 * (len(ici) + len(passing)), issue), land

    onehot = (jnp.arange(4) == chip).astype(F32) * (ic == 0).astype(F32)
    cw_mine = onehot[:, None, None, None] * ssd_conv_w[None]
    cw_full = _allreduce_small(cw_mine.transpose(1, 2, 0, 3).reshape(-1, LANES), "gather_conv_w").reshape(2, SSD_D_CONV, SSD_CONV_DIM)

    def wmm(a, name, layer, *, dn="nn", res=None, call, rider=None):
        return _matmul(a, gw[name], dn=dn, res=res, b_lay=(cut_of[name], layer), name=call, rider=rider)

    h = x[0]
    target = loss_target[0]
    saved = []
    for i in range(DEPTH):
        j = i // 2
        nw = norm_w[i:i + 1]
        pw = ple_norm_w[i:i + 1]
        s = dict(h=h)
        u = _rms_fwd(h, nw, f"rms_{i}")
        s["u"] = u
        if i % 2 == 0:
            w_dt = jnp.pad(gw["ssd_in_w"][j, SSD_ZX:], ((0, LANES - SSD_N_HEADS), (0, 0)))
            rider, land = gather_rider(f"ssd_in_{i}")
            pzx = land(_matmul(u, gw["ssd_in_w"], dn="nt", b_lay=("stack", j, SSD_ZX), name=f"ssd_in_{i}", rider=rider))
            pdt = _matmul(u, w_dt, dn="nt", name=f"ssd_indt_{i}")
            rider, land = gather_rider(f"conv_{i}")
            act = land(_conv_fwd(pzx, cw_full[j], ssd_conv_b[j:j + 1], f"conv_{i}", rider=rider))
            dtg = jnp.pad(pdt[:, :SSD_N_HEADS].reshape(-1, SSD_N_GROUPS, 8).transpose(1, 0, 2), ((0, 0), (0, 0), (0, LANES - 8)))
            vecs = (_group_lanes(ssd_dt_bias[j]), _group_lanes(ssd_a_log[j]), _group_lanes(ssd_d[j]))
            rider, land = gather_rider(f"ssd_{i}")
            yn, states = land(_ssd_fwd(act, dtg, *vecs, pzx, ssd_gnorm_w[j:j + 1], f"ssd_{i}", rider=rider))
            s.update(w_dt=w_dt, pzx=pzx, act=act, dtg=dtg, vecs=vecs, yn=yn, states=states)
            rider, land = gather_rider(f"ssd_out_{i}")
            h1 = land(wmm(yn, "ssd_out_w", j, res=h, call=f"ssd_out_{i}", rider=rider))
        else:
            rider, land = gather_rider(f"sb_in_{i}")
            proj = land(wmm(u, "sb_in_w", j, call=f"sb_in_{i}", rider=rider))
            qn, kn, vb = _qknorm_fwd(proj, sb_qn_w[j:j + 1], sb_kn_w[j:j + 1], f"qknorm_{i}")
            rider, land = gather_rider(f"sb_{i}")
            og, o, tot = land(_sb_fwd(qn, kn, vb, proj, f"sb_{i}", rider=rider))
            s.update(proj=proj, qn=qn, kn=kn, vb=vb, og=og, o=o, tot=tot)
            rider, land = gather_rider(f"sb_out_{i}")
            h1 = land(wmm(og, "sb_out_w", j, res=h, call=f"sb_out_{i}", rider=rider))
        n2 = _rms_fwd(h1, pw, f"ple_rms_{i}")
        gl = wmm(n2, "ple_gate_w", i, call=f"ple_gate_{i}")
        pp = wmm(p[i, 0], "ple_proj_w", i, call=f"ple_proj_{i}")
        h = _ple_fwd(h1, pp, gl, f"ple_{i}")
        s.update(h1=h1, n2=n2, gl=gl, pp=pp)
        saved.append(s)

    dh, loss_lanes = _loss_bwd(h, target, "loss")

    wg = {}
    gsmall = {n: [None] * s[0] for n, s in _SMALL}
    g_conv_w = [None, None]
    scat = {}
    pending = late = None

    def wgrad(a, b, name, layer, call, rider=None):
        out = _matmul(a, b, dn="tn", out_dtype=BF16, o_lay=(cut_of[name], 0, 1), name=call, rider=rider)
        wg[(name, layer)], rode = out if rider is not None else (out, None)
        return rode

    def pair_sums(pieces, got, tag):
        return pieces, [_pair_sum(wg[q], r, place, f"rs_pair_sum_{tag}_{k}") for k, (q, r) in enumerate(zip(pieces, got))]

    def sibling_rider(pieces):
        return _exchange_rider([wg[q] for q in pieces])

    def riding_with(own):
        return (pending[0] + own[0], pending[1] + own[1]) if pending else own

    def arrived(sent, got):
        for q, pair, g in zip(sent[0], sent[1], got):
            scat[q] = (pair, g)

    for i in reversed(range(DEPTH)):
        j = i // 2
        s = saved[i]
        nw = norm_w[i:i + 1]
        pw = ple_norm_w[i:i + 1]
        dpp, dgl = _ple_bwd(dh, s["pp"], s["gl"], f"ple_bwd_{i}")
        wgrad(p[i, 0], dpp, "ple_proj_w", i, f"d_ple_proj_{i}")
        if late is None:
            wgrad(s["n2"], dgl, "ple_gate_w", i, f"d_ple_gate_{i}")
        else:
            pending = pair_sums(late, wgrad(s["n2"], dgl, "ple_gate_w", i, f"d_ple_gate_{i}", rider=sibling_rider(late)), f"{i + 1}_in")
        dn2 = wmm(dgl, "ple_gate_w", i, dn="nt", call=f"ple_gate_bwd_{i}")
        dh1, dpw = _rms_bwd(s["h1"], pw, dn2, dh, f"ple_rms_bwd_{i}")
        gsmall["ple_norm_w"][i] = dpw
        if i % 2 == 0:
            wgrad(s["yn"], dh1, "ssd_out_w", j, f"d_ssd_out_{i}")
            early = layer_pieces(i)[1:]
            dyn, got = wmm(dh1, "ssd_out_w", j, dn="nt", call=f"ssd_out_bwd_{i}", rider=sibling_rider(early))
            riding = riding_with(pair_sums(early, got, f"{i}_out"))
            outs, got = _ssd_bwd(s["act"], s["dtg"], *s["vecs"], s["pzx"], ssd_gnorm_w[j:j + 1], s["states"], dyn, f"ssd_bwd_{i}",
                                 rider=_scatter_rider(riding[1]))
            arrived(riding, got)
            dxs, dbm, dcm, ddtg, dbias, dalog, ddsk, dz, dgw = outs
            dzx, dcw, dcb = _conv_bwd(s["pzx"], cw_full[j], ssd_conv_b[j:j + 1], dxs, dbm, dcm, dz, f"conv_bwd_{i}")
            ddt = _pad_lanes(ddtg[:, :, :8].transpose(1, 0, 2).reshape(-1, SSD_N_HEADS)).astype(BF16)
            dwt = _matmul(dzx, s["u"], dn="tn", out_dtype=BF16, out_rows=SSD_IN_DIM, name=f"d_ssd_in_{i}")
            dwt_dt = _matmul(ddt, s["u"], dn="tn", out_dtype=BF16, name=f"d_ssd_indt_{i}")
            wg[("ssd_in_w", j)] = lax.dynamic_update_slice(dwt, dwt_dt[:SSD_N_HEADS], (SSD_ZX, 0))[None]
            if i == 0:
                by_shard = wg[("ssd_in_w", 0)].reshape(4, -1, D_MODEL)
                parts = [("ssd_in_w", 0, 0), ("ssd_in_w", 0, 1)]
                wg[parts[0]] = by_shard[:, :LAST_SPLIT].reshape(1, -1, D_MODEL)
                wg[parts[1]] = by_shard[:, LAST_SPLIT:].reshape(1, -1, D_MODEL)
                last = pair_sums(parts, _pair_exchange([wg[q] for q in parts], "rs_pair_exchange_last"), "0_in")
                du, got = _matmul(dzx, gw["ssd_in_w"], b_lay=("stack", j, SSD_ZX), name=f"ssd_in_bwd_{i}",
                                  rider=_scatter_rider(last[1][1:]))
                arrived((parts[1:], last[1][1:]), got)
            else:
                du = _matmul(dzx, gw["ssd_in_w"], b_lay=("stack", j, SSD_ZX), name=f"ssd_in_bwd_{i}")
            du = _matmul(ddt, s["w_dt"], res=du, name=f"ssd_indt_bwd_{i}")
            g_conv_w[j] = dcw
            gsmall["ssd_conv_b"][j] = dcb
            gsmall["ssd_dt_bias"][j] = dbias[:, 0, :8].reshape(1, SSD_N_HEADS)
            gsmall["ssd_a_log"][j] = dalog[:, 0, :8].reshape(1, SSD_N_HEADS)
            gsmall["ssd_d"][j] = ddsk[:, 0, :8].reshape(1, SSD_N_HEADS)
            gsmall["ssd_gnorm_w"][j] = dgw
        else:
            wgrad(s["og"], dh1, "sb_out_w", j, f"d_sb_out_{i}")
            early = layer_pieces(i)[1:]
            dog, got = wmm(dh1, "sb_out_w", j, dn="nt", call=f"sb_out_bwd_{i}", rider=sibling_rider(early))
            riding = riding_with(pair_sums(early, got, f"{i}_out"))
            outs, got = _sb_bwd(s["qn"], s["kn"], s["vb"], s["proj"], s["o"], s["tot"], dog, f"sb_bwd_{i}", rider=_scatter_rider(riding[1]))
            arrived(riding, got)
            dqn, dkn, _, dvb, dg = outs
            dq, dk, dqw, dkw = _qknorm_bwd(s["proj"], sb_qn_w[j:j + 1], sb_kn_w[j:j + 1], dqn, dkn, f"qknorm_bwd_{i}")
            dproj = jnp.concatenate([dq, dk, dvb, dg], axis=1)
            du = wmm(dproj, "sb_in_w", j, dn="nt", call=f"sb_in_bwd_{i}")
            wgrad(s["u"], dproj, "sb_in_w", j, f"d_sb_in_{i}")
            gsmall["sb_qn_w"][j] = dqw
            gsmall["sb_kn_w"][j] = dkw
        dh, dnw = _rms_bwd(s["h"], nw, du, dh1, f"rms_bwd_{i}")
        gsmall["norm_w"][i] = dnw
        late = layer_pieces(i)[:1]
    grad_x = dh[None]

    def reduced(names, call):
        halves = []
        for n in names:
            buf = None
            for l in range(layers_of[n]):
                if (n, l, 0) in scat:
                    r = shard_shapes[big_names.index(n)][1]
                    for part, row0 in ((0, 0), (1, LAST_SPLIT)):
                        buf = _chip_sum_stacked(*scat[(n, l, part)], place, l, layers_of[n], buf, f"rs_chip_sum_{n}_{l}_{part}", row0, r)
                else:
                    buf = _chip_sum(*scat[(n, l)], place, l, layers_of[n], buf, f"rs_chip_sum_{n}_{l}")
            halves.append(buf)
        return dict(zip(names, _pair_gather(halves, [cut_of[n] == "stack" for n in names], call)))

    def updated(n, rider=None):
        return _adamw(held(n, w_in[n]), g_big[n], held(n, m_in[n]), held(n, v_in[n]), f"adamw_{n}", rider=rider)

    done_early = ["sb_in_w", "sb_out_w"]
    g_big = reduced(done_early, "rs_pair_gather_sb")
    step = {}
    step["sb_in_w"], got = updated("sb_in_w", rider=_scatter_rider(last[1][:1]))
    arrived((last[0][:1], last[1][:1]), got)
    g_big.update(reduced([n for n in big_names if n not in done_early], "rs_pair_gather"))

    small_parts = [jnp.concatenate(gsmall[n], axis=0).reshape(-1) for n, _ in _SMALL]
    small_parts.append(jnp.stack(g_conv_w).reshape(-1))
    small_parts.append(loss_lanes.reshape(-1))
    small_sum = _allreduce_small(jnp.concatenate(small_parts).reshape(-1, LANES), "allreduce_small").reshape(-1)
    g_small, off = {}, 0
    for n, shape in _SMALL:
        size = math.prod(shape)
        g_small[n] = small_sum[off:off + size].reshape(shape)
        off += size
    cw_size = 2 * SSD_D_CONV * SSD_CONV_DIM
    g_cw_full = small_sum[off:off + cw_size].reshape(2, SSD_D_CONV, 4, SSD_CONV_DIM // 4)
    g_small["ssd_conv_w"] = jnp.sum(g_cw_full * (jnp.arange(4) == chip).astype(F32)[None, None, :, None], axis=2)
    loss = 0.5 * jnp.sum(small_sum[off + cw_size:]) / D_MODEL

    grads, delta, new_m, new_v = {}, {}, {}, {}
    for n in big_names:
        grads[n], delta[n], new_m[n], new_v[n] = (held(n, a) for a in (g_big[n], *(step[n] if n in step else updated(n))))
    small_names = [n for n, _ in _SMALL] + ["ssd_conv_w"]
    pack = lambda d: jnp.concatenate([d[n].reshape(-1) for n in small_names]).reshape(1, -1, LANES)
    ds, ms, vs = _adamw(pack(w_in), pack(g_small), pack(m_in), pack(v_in), "adamw_small")
    off = 0
    for n in small_names:
        shape = w_in[n].shape
        size = math.prod(shape)
        grads[n] = g_small[n]
        delta[n] = ds.reshape(-1)[off:off + size].reshape(shape)
        new_m[n] = ms.reshape(-1)[off:off + size].reshape(shape)
        new_v[n] = vs.reshape(-1)[off:off + size].reshape(shape)
        off += size

    order = ["norm_w", "ssd_in_w", "ssd_conv_w", "ssd_conv_b", "ssd_dt_bias", "ssd_a_log", "ssd_d", "ssd_gnorm_w", "ssd_out_w",
             "sb_in_w", "sb_qn_w", "sb_kn_w", "sb_out_w", "ple_norm_w", "ple_gate_w", "ple_proj_w"]
    return (loss, grad_x, *[grads[n] for n in order], *[delta[n] for n in order], *[new_m[n] for n in order],
            *[new_v[n] for n in order])
```

```python
import functools
import math

import jax
import jax.numpy as jnp
from jax import lax
from jax.experimental import pallas as pl
from jax.experimental.pallas import tpu as pltpu

F32 = jnp.float32
BF16 = jnp.bfloat16
MESH = pl.DeviceIdType.MESH

D_MODEL = 2048
DEPTH = 4
SSD_D_INNER = 4096
SSD_N_GROUPS = 8
SSD_GROUP_W = SSD_D_INNER // SSD_N_GROUPS
SSD_D_STATE = 128
SSD_CHUNK = 128
SSD_CONV_DIM = 6144
SSD_D_CONV = 4
SSD_N_HEADS = 64
SB_HEAD_DIM = 128
SB_N_HEADS = 16
SB_WIDTH = 2048
NORM_EPS = 1e-6
GATED_NORM_EPS = 1e-5
ADAM_LR = 0.001
ADAM_B1 = 0.9
ADAM_B2 = 0.999
ADAM_EPS = 1e-08
ADAM_WD = 0.01
ADAM_STEP = 10

SSD_ZX = SSD_D_INNER + SSD_CONV_DIM
SSD_IN_DIM = SSD_ZX + SSD_N_HEADS
LAST_SPLIT = 1104
LANES = 128
BF16_ROWS = 16

_BIG = (
    ("ssd_in_w", (2, 2576, 2048), "stack"),
    ("ssd_out_w", (2, 1024, 2048), "row"),
    ("sb_in_w", (2, 2048, 2048), "col"),
    ("sb_out_w", (2, 512, 2048), "row"),
    ("ple_gate_w", (4, 512, 2048), "row"),
    ("ple_proj_w", (4, 256, 512), "col"),
)
_SMALL = (
    ("norm_w", (4, 2048)),
    ("ssd_conv_b", (2, 6144)),
    ("ssd_dt_bias", (2, 64)),
    ("ssd_a_log", (2, 64)),
    ("ssd_d", (2, 64)),
    ("ssd_gnorm_w", (2, 4096)),
    ("sb_qn_w", (2, 128)),
    ("sb_kn_w", (2, 128)),
    ("ple_norm_w", (4, 2048)),
)

_DN = {
    "nn": (((1,), (0,)), ((), ())),
    "nt": (((1,), (1,)), ((), ())),
    "tn": (((0,), (0,)), ((), ())),
}


def _dot(a, b, dn="nn"):
    return lax.dot_general(a.astype(BF16), b.astype(BF16), _DN[dn], preferred_element_type=F32)


@functools.partial(jax.custom_vjp, nondiff_argnums=(2,))
def _gdot(a, b, dn):
    return _dot(a, b, dn)


def _gdot_fwd(a, b, dn):
    return _dot(a, b, dn), (a, b)


def _gdot_bwd(dn, res, g):
    a, b = res
    if dn == "nn":
        return _dot(g, b, "nt"), _dot(a, g, "tn")
    if dn == "nt":
        return _dot(g, b, "nn"), _dot(g, a, "tn")
    return _dot(b, g, "nt"), _dot(a, g, "nn")


_gdot.defvjp(_gdot_fwd, _gdot_bwd)


def _split_dot(x, t, parts, x_left):
    acc = None
    r = x
    for i in range(parts):
        p = r.astype(BF16)
        d = lax.dot_general(p, t, _DN["nn"], preferred_element_type=F32) if x_left else lax.dot_general(
            t, p, _DN["nn"], preferred_element_type=F32)
        acc = d if acc is None else acc + d
        if i + 1 < parts:
            r = r - p.astype(F32)
    return acc


def _tri(n, lower, strict=False):
    r = lax.broadcasted_iota(jnp.int32, (n, n), 0)
    c = lax.broadcasted_iota(jnp.int32, (n, n), 1)
    keep = (r > c if strict else r >= c) if lower else (r < c if strict else r <= c)
    return jnp.where(keep, 1.0, 0.0).astype(BF16)


def _cumsum_rows_raw(x):
    return _split_dot(x, _tri(x.shape[0], True), 3, False)


@jax.custom_vjp
def _cumsum_rows(x):
    return _cumsum_rows_raw(x)


def _cumsum_rows_fwd(x):
    return _cumsum_rows_raw(x), None


def _cumsum_rows_bwd(_, g):
    return (_split_dot(g, _tri(g.shape[0], False), 3, False),)


_cumsum_rows.defvjp(_cumsum_rows_fwd, _cumsum_rows_bwd)


def _sigmoid(x):
    return 1.0 / (1.0 + jnp.exp(-x))


def _softplus(x):
    return jnp.maximum(x, 0.0) + jnp.log(1.0 + jnp.exp(-jnp.abs(x)))


def _rms(x, w, eps):
    return x * lax.rsqrt(jnp.mean(x * x, axis=-1, keepdims=True) + eps) * w


_ANY = pl.BlockSpec(memory_space=pl.ANY)


def _params(*sem):
    return pltpu.CompilerParams(dimension_semantics=sem)


class _Rider:
    def __init__(self, reads, writes, n_sems, issue):
        self.reads, self.writes, self.n_sems, self.issue = list(reads), list(writes), n_sems, issue


def _pcall(body, *, grid, in_specs, out_specs, out_shape, args, sem, name, scratch_shapes=(), aliases=None, rider=None):
    aliases = dict(aliases or {})
    if rider is None:
        outs = pl.pallas_call(body, grid=grid, in_specs=in_specs, out_specs=out_specs, out_shape=out_shape,
                              scratch_shapes=list(scratch_shapes), input_output_aliases=aliases,
                              compiler_params=_params(*sem), name=name)(*args)
        return list(outs), []
    n_in, n_out, n_scr, n_rd, n_wr = len(args), len(out_shape), len(scratch_shapes), len(rider.reads), len(rider.writes)
    passed = [k for k, w in enumerate(rider.writes) if not isinstance(w, jax.ShapeDtypeStruct)]
    for pos, k in enumerate(passed):
        aliases[n_in + n_rd + pos] = n_out + k

    def wrapped(*refs):
        ins = refs[:n_in]
        reads = refs[n_in:n_in + n_rd]
        base = n_in + n_rd + len(passed)
        outs = refs[base:base + n_out]
        writes = refs[base + n_out:base + n_out + n_wr]
        scr = refs[base + n_out + n_wr:base + n_out + n_wr + n_scr]
        send, recv = refs[-2:]
        first = last = None
        for d, n in enumerate(grid):
            i = pl.program_id(d)
            first = (i == 0) if first is None else first & (i == 0)
            last = (i == n - 1) if last is None else last & (i == n - 1)

        @pl.when(first)
        def _():
            for cp in rider.issue(reads, writes, send, recv)[0]:
                cp.start()

        body(*ins, *outs, *scr)

        @pl.when(last)
        def _():
            sends, arrivals = rider.issue(reads, writes, send, recv)
            for cp in arrivals:
                cp.wait_recv()
            for cp in sends:
                cp.wait_send()

    outs = pl.pallas_call(
        wrapped, grid=grid,
        in_specs=list(in_specs) + [_ANY] * (n_rd + len(passed)),
        out_specs=list(out_specs) + [_ANY] * n_wr,
        out_shape=list(out_shape) + [jax.ShapeDtypeStruct(w.shape, w.dtype) for w in rider.writes],
        scratch_shapes=list(scratch_shapes) + [pltpu.SemaphoreType.DMA((rider.n_sems,)), pltpu.SemaphoreType.DMA((rider.n_sems,))],
        input_output_aliases=aliases, compiler_params=_params(*(["arbitrary"] * len(grid))), name=name,
    )(*args, *rider.reads, *[rider.writes[k] for k in passed])
    return list(outs[:n_out]), list(outs[n_out:])


MM_TK = 2048


def _pick(dim, pref, unit=None):
    t = pref
    while t >= LANES:
        if dim % t == 0 and (unit is None or unit % t == 0):
            return t
        t //= 2
    return dim


def _matmul(a, b, *, dn="nn", res=None, out_dtype=F32, name, b_lay=None, o_lay=None, o_buf=None, out_rows=None, rider=None):
    if dn == "tn":
        k_dim, m_dim = a.shape
    else:
        m_dim, k_dim = a.shape
    unit_m = unit_n = unit_k = None
    if b_lay is None:
        n_dim = b.shape[0] if dn == "nt" else b.shape[1]
    elif b_lay[0] == "stack":
        cut, layer, rows = b_lay
        cols = b.shape[2]
        n_dim = cols if dn == "nn" else rows
        assert k_dim == (rows if dn == "nn" else cols) and dn != "tn"
    else:
        cut, layer = b_lay
        r, c = b.shape[2:]
        rows, cols = (4 * r, c) if cut == "row" else (r, 4 * c)
        n_dim = cols if dn == "nn" else rows
        assert k_dim == (rows if dn == "nn" else cols) and dn != "tn"
        if (cut == "row") == (dn == "nn"):
            unit_k = r if cut == "row" else c
        else:
            unit_n = r if cut == "row" else c
    if o_lay is not None:
        o_cut, o_layer, o_layers = o_lay
        if o_cut == "row":
            unit_m = m_dim // 4
        else:
            unit_n = n_dim // 4
    tm, tn, tk = _pick(m_dim, 1024, unit_m), _pick(n_dim, 1024, unit_n), _pick(k_dim, MM_TK, unit_k)
    nk = k_dim // tk
    a_spec = pl.BlockSpec((tk, tm), lambda i, j, k: (k, i)) if dn == "tn" else pl.BlockSpec((tm, tk), lambda i, j, k: (i, k))
    if b_lay is None:
        b_spec = pl.BlockSpec((tn, tk), lambda i, j, k: (j, k)) if dn == "nt" else pl.BlockSpec((tk, tn), lambda i, j, k: (k, j))
    elif cut == "stack":
        b_spec = (pl.BlockSpec((None, tk, tn), lambda i, j, k: (layer, k, j)) if dn == "nn" else
                  pl.BlockSpec((None, tn, tk), lambda i, j, k: (layer, j, k)))
    elif dn == "nn" and cut == "row":
        per = r // tk
        b_spec = pl.BlockSpec((None, None, tk, tn), lambda i, j, k: (k // per, layer, k % per, j))
    elif dn == "nn":
        per = c // tn
        b_spec = pl.BlockSpec((None, None, tk, tn), lambda i, j, k: (j // per, layer, k, j % per))
    elif cut == "row":
        per = r // tn
        b_spec = pl.BlockSpec((None, None, tn, tk), lambda i, j, k: (j // per, layer, j % per, k))
    else:
        per = c // tk
        b_spec = pl.BlockSpec((None, None, tn, tk), lambda i, j, k: (k // per, layer, j, k % per))
    r_spec = pl.BlockSpec((tm, tn), lambda i, j, k: (i, j))
    if o_lay is None:
        o_spec = r_spec
        out_shape = jax.ShapeDtypeStruct((out_rows or m_dim, n_dim), out_dtype)
    elif o_cut == "row":
        per_o = unit_m // tm
        o_spec = pl.BlockSpec((None, None, tm, tn), lambda i, j, k: (i // per_o, o_layer, i % per_o, j))
        out_shape = jax.ShapeDtypeStruct((4, o_layers, unit_m, n_dim), out_dtype)
    else:
        per_o = unit_n // tn
        o_spec = pl.BlockSpec((None, None, tm, tn), lambda i, j, k: (j // per_o, o_layer, i, j % per_o))
        out_shape = jax.ShapeDtypeStruct((4, o_layers, m_dim, unit_n), out_dtype)
    has_res = res is not None
    has_buf = o_buf is not None

    def body(*refs):
        a_ref, b_ref = refs[:2]
        r_ref = refs[2] if has_res else None
        o_ref = refs[-1] if nk == 1 else refs[-2]

        def finish(v):
            if has_res:
                v = v + r_ref[...]
            o_ref[...] = v.astype(o_ref.dtype)

        if nk == 1:
            finish(_dot(a_ref[...], b_ref[...], dn))
            return
        acc_ref = refs[-1]
        k = pl.program_id(2)

        @pl.when(k == 0)
        def _():
            acc_ref[...] = jnp.zeros_like(acc_ref)

        acc_ref[...] += _dot(a_ref[...], b_ref[...], dn)

        @pl.when(k == nk - 1)
        def _():
            finish(acc_ref[...])

    args = [a, b] + ([res] if has_res else []) + ([o_buf] if has_buf else [])
    outs, rode = _pcall(
        body, grid=(m_dim // tm, n_dim // tn, nk),
        in_specs=[a_spec, b_spec] + ([r_spec] if has_res else []) + ([_ANY] if has_buf else []),
        out_specs=[o_spec], out_shape=[out_shape],
        scratch_shapes=[] if nk == 1 else [pltpu.VMEM((tm, tn), F32)],
        aliases={len(args) - 1: 0} if has_buf else {},
        args=args, sem=("parallel", "parallel", "arbitrary"), name=name, rider=rider)
    return (outs[0], rode) if rider is not None else outs[0]


def _rowcall(fn, rows, consts, outs, accs, *, name, tm=512):
    args = list(rows) + list(consts)
    in_specs = [pl.BlockSpec((tm, r.shape[1]), lambda i: (i, 0)) for r in rows]
    in_specs += [pl.BlockSpec(c.shape, lambda i: (0, 0)) for c in consts]
    s_dim = args[0].shape[0]
    n_in, n_out = len(args), len(outs)
    out_shape = [jax.ShapeDtypeStruct((s_dim, w), dt) for w, dt in outs] + [jax.ShapeDtypeStruct(s, F32) for s in accs]
    out_specs = [pl.BlockSpec((tm, w), lambda i: (i, 0)) for w, _ in outs] + [pl.BlockSpec(s, lambda i: (0, 0)) for s in accs]

    def body(*refs):
        vals = fn(*[r[...] for r in refs[:n_in]])
        o_refs = refs[n_in:n_in + n_out]
        a_refs = refs[n_in + n_out:]
        for o, v in zip(o_refs, vals[:n_out]):
            o[...] = v.astype(o.dtype)
        if a_refs:
            @pl.when(pl.program_id(0) == 0)
            def _():
                for a_ref in a_refs:
                    a_ref[...] = jnp.zeros_like(a_ref)

            for a_ref, v in zip(a_refs, vals[n_out:]):
                a_ref[...] += v

    return pl.pallas_call(
        body, grid=(s_dim // tm,), in_specs=in_specs, out_specs=out_specs, out_shape=out_shape,
        compiler_params=_params("arbitrary"), name=name,
    )(*args)


def _rms_fwd(h, w, name):
    return _rowcall(lambda x, w_: (_rms(x, w_, NORM_EPS),), [h], [w], [(h.shape[1], BF16)], [], name=name)[0]


def _rms_bwd(h, w, dy, dres, name):
    def fn(x, dy_, dres_, w_):
        _, vjp = jax.vjp(lambda a, b: _rms(a, b, NORM_EPS), x, w_)
        dx, dw = vjp(dy_)
        return dx + dres_, dw

    return _rowcall(fn, [h, dy, dres], [w], [(h.shape[1], F32)], [w.shape], name=name)


def _ple_fwd(h1, pp, gl, name):
    return _rowcall(lambda a, b, c: (a + b * _sigmoid(c),), [h1, pp, gl], [], [(h1.shape[1], F32)], [], name=name)[0]


def _ple_bwd(dh2, pp, gl, name):
    def fn(d, b, c):
        gate = _sigmoid(c)
        return d * gate, d * b * gate * (1.0 - gate)

    return _rowcall(fn, [dh2, pp, gl], [], [(dh2.shape[1], BF16), (dh2.shape[1], BF16)], [], name=name)


def _loss_bwd(y, target, name):
    width = y.shape[1]

    def fn(a, t):
        d = a - t
        col = jnp.sum(d * d, axis=0, keepdims=True)
        part = col[:, 0:LANES]
        for j in range(1, width // LANES):
            part = part + col[:, j * LANES:(j + 1) * LANES]
        return d * (1.0 / width), part

    return _rowcall(fn, [y, target], [], [(width, F32)], [(1, LANES)], name=name)


CONV_TC = 256


def _shift_down(x, j):
    if j == 0:
        return x
    row = lax.broadcasted_iota(jnp.int32, x.shape, 0)
    return jnp.where(row >= j, pltpu.roll(x, j, 0), 0.0)


def _shift_up(x, j):
    if j == 0:
        return x
    n = x.shape[0]
    row = lax.broadcasted_iota(jnp.int32, x.shape, 0)
    return jnp.where(row < n - j, pltpu.roll(x, n - j, 0), 0.0)


def _conv_fwd(pzx, cw, cb, name, rider=None):
    s_dim = pzx.shape[0]
    off = SSD_D_INNER // CONV_TC

    def body(x_ref, w_ref, b_ref, o_ref):
        x = x_ref[...]
        w = w_ref[...]
        y = b_ref[...] + w[3:4, :] * x
        for k in range(SSD_D_CONV - 1):
            y = y + w[k:k + 1, :] * _shift_down(x, SSD_D_CONV - 1 - k)
        o_ref[...] = y * _sigmoid(y)

    outs, rode = _pcall(
        body, grid=(SSD_CONV_DIM // CONV_TC,),
        in_specs=[pl.BlockSpec((s_dim, CONV_TC), lambda j: (0, off + j)), pl.BlockSpec((SSD_D_CONV, CONV_TC), lambda j: (0, j)),
                  pl.BlockSpec((1, CONV_TC), lambda j: (0, j))],
        out_specs=[pl.BlockSpec((s_dim, CONV_TC), lambda j: (0, j))],
        out_shape=[jax.ShapeDtypeStruct((s_dim, SSD_CONV_DIM), F32)],
        args=[pzx, cw, cb], sem=("parallel",), name=name, rider=rider)
    return (outs[0], rode) if rider is not None else outs[0]


def _conv_bwd(pzx, cw, cb, dxs, dbm, dcm, dzx, name):
    s_dim = pzx.shape[0]
    off = SSD_D_INNER // CONV_TC
    n_x, n_b = dxs.shape[1] // CONV_TC, dbm.shape[1] // CONV_TC

    def body(x_ref, w_ref, b_ref, dxs_ref, dbm_ref, dcm_ref, _, dx_ref, dw_ref, db_ref):
        j = pl.program_id(0)
        d = jnp.where(j < n_x, dxs_ref[...], jnp.where(j < n_x + n_b, dbm_ref[...], dcm_ref[...]))
        x = x_ref[...]
        w = w_ref[...]
        xs = [_shift_down(x, SSD_D_CONV - 1 - k) for k in range(SSD_D_CONV)]
        y = b_ref[...]
        for k in range(SSD_D_CONV):
            y = y + w[k:k + 1, :] * xs[k]
        sg = _sigmoid(y)
        dy = d * (sg * (1.0 + y * (1.0 - sg)))
        dx = w[3:4, :] * dy
        for k in range(SSD_D_CONV - 1):
            dx = dx + w[k:k + 1, :] * _shift_up(dy, SSD_D_CONV - 1 - k)
        dx_ref[...] = dx.astype(dx_ref.dtype)
        for k in range(SSD_D_CONV):
            dw_ref[k:k + 1, :] = jnp.sum(dy * xs[k], axis=0, keepdims=True)
        db_ref[...] = jnp.sum(dy, axis=0, keepdims=True)

    part = lambda lo, n: pl.BlockSpec((s_dim, CONV_TC), lambda j: (0, jnp.clip(j - lo, 0, n - 1)))
    return pl.pallas_call(
        body, grid=(SSD_CONV_DIM // CONV_TC,),
        in_specs=[pl.BlockSpec((s_dim, CONV_TC), lambda j: (0, off + j)), pl.BlockSpec((SSD_D_CONV, CONV_TC), lambda j: (0, j)),
                  pl.BlockSpec((1, CONV_TC), lambda j: (0, j)), part(0, n_x), part(n_x, n_b), part(n_x + n_b, n_b), _ANY],
        out_specs=[pl.BlockSpec((s_dim, CONV_TC), lambda j: (0, off + j)), pl.BlockSpec((SSD_D_CONV, CONV_TC), lambda j: (0, j)),
                   pl.BlockSpec((1, CONV_TC), lambda j: (0, j))],
        out_shape=[jax.ShapeDtypeStruct(dzx.shape, dzx.dtype), jax.ShapeDtypeStruct((SSD_D_CONV, SSD_CONV_DIM), F32),
                   jax.ShapeDtypeStruct((1, SSD_CONV_DIM), F32)],
        input_output_aliases={6: 0}, compiler_params=_params("arbitrary"), name=name,
    )(pzx, cw, cb, dxs, dbm, dcm, dzx)


def _ssd_step(xs, bm, cm, dtraw, bias, alog, dskip, st_in, z, gw, dot, cumsum):
    n = xs.shape[0]
    lane = lax.broadcasted_iota(jnp.int32, (1, LANES), 1)
    sub = lax.broadcasted_iota(jnp.int32, (LANES, 1), 0)
    left = (lane < 64).astype(F32)
    right = 1.0 - left
    top = (sub < 64).astype(F32)
    bot = 1.0 - top
    row = lax.broadcasted_iota(jnp.int32, (n, n), 0)
    colm = lax.broadcasted_iota(jnp.int32, (n, n), 1)
    causal = row >= colm

    dt = _softplus(dtraw + bias)
    adt = dt * (-jnp.exp(alog))
    acum = cumsum(adt)
    acum_t = acum.T
    last = jnp.sum(adt, axis=0, keepdims=True)
    scores = dot(cm, bm, "nt")

    def lane_of(v, h):
        return jnp.sum(v * (lane == h).astype(F32), axis=1, keepdims=True)

    ys, sts = [], []
    for pr in range(4):
        heads = (2 * pr, 2 * pr + 1)
        ac = [lane_of(acum, h) for h in heads]
        ar = [jnp.sum(acum_t * (sub == h).astype(F32), axis=0, keepdims=True) for h in heads]
        dth = [lane_of(dt, h) for h in heads]
        la = [lane_of(last, h) for h in heads]
        dk = [lane_of(dskip, h) for h in heads]
        x2 = xs[:, pr * LANES:(pr + 1) * LANES]
        xdt = x2 * (dth[0] * left + dth[1] * right)
        yd = None
        for i, side in enumerate((left, right)):
            decay = jnp.where(causal, jnp.exp(jnp.minimum(ac[i] - ar[i], 0.0)), 0.0)
            t = dot(scores * decay, xdt * side, "nn")
            yd = t if yd is None else yd + t
        st2 = st_in[pr * LANES:(pr + 1) * LANES, :]
        yo = dot(cm, st2, "nt") * (jnp.exp(ac[0]) * left + jnp.exp(ac[1]) * right)
        dte = jnp.exp(la[0] - ac[0]) * left + jnp.exp(la[1] - ac[1]) * right
        cs = dot(xdt * dte, bm, "tn")
        sts.append(st2 * (jnp.exp(la[0]) * top + jnp.exp(la[1]) * bot) + cs)
        ys.append(yd + yo + (dk[0] * left + dk[1] * right) * x2)
    y = jnp.concatenate(ys, axis=1)
    yg = y * (z * _sigmoid(z))
    yn = yg * lax.rsqrt(jnp.mean(yg * yg, axis=-1, keepdims=True) + GATED_NORM_EPS) * gw
    return yn, jnp.concatenate(sts, axis=0)


SSD_GB_FWD, SSD_GB_BWD = 4, 4


def _ssd_specs(n_chunks, rev, gb):
    ci = (lambda c: n_chunks - 1 - c) if rev else (lambda c: c)
    n_x = SSD_D_INNER // (gb * LANES)
    n_g = SSD_N_GROUPS // gb
    return dict(
        xs=pl.BlockSpec((SSD_CHUNK, gb * SSD_GROUP_W), lambda g, c: (ci(c), g)),
        bm=pl.BlockSpec((SSD_CHUNK, gb * LANES), lambda g, c: (ci(c), n_x + g)),
        cm=pl.BlockSpec((SSD_CHUNK, gb * LANES), lambda g, c: (ci(c), n_x + n_g + g)),
        dt=pl.BlockSpec((gb, SSD_CHUNK, LANES), lambda g, c: (g, ci(c), 0)),
        vec=pl.BlockSpec((gb, 1, LANES), lambda g, c: (g, 0, 0)),
        z=pl.BlockSpec((SSD_CHUNK, gb * SSD_GROUP_W), lambda g, c: (ci(c), g)),
        gw=pl.BlockSpec((1, gb * SSD_GROUP_W), lambda g, c: (0, g)),
        st=pl.BlockSpec((gb, None, SSD_GROUP_W, SSD_D_STATE), lambda g, c: (g, ci(c), 0, 0)),
    )


def _group_cols(k, width):
    return slice(k * width, (k + 1) * width)


def _ssd_fwd(act, dtg, bias, alog, dskip, pzx, gw, name, rider=None):
    s_dim = act.shape[0]
    n_chunks = s_dim // SSD_CHUNK
    gb = SSD_GB_FWD
    sp = _ssd_specs(n_chunks, False, gb)

    def body(xs, bm, cm, dt, b_ref, a_ref, d_ref, z, gw_ref, yn_ref, st_ref, state):
        @pl.when(pl.program_id(1) == 0)
        def _():
            state[...] = jnp.zeros_like(state)

        for k in range(gb):
            wide, lanes = _group_cols(k, SSD_GROUP_W), _group_cols(k, LANES)
            st_in = state[k]
            st_ref[k] = st_in
            yn, st_out = _ssd_step(xs[:, wide], bm[:, lanes], cm[:, lanes], dt[k], b_ref[k], a_ref[k], d_ref[k], st_in, z[:, wide],
                                   gw_ref[:, wide], _dot, _cumsum_rows_raw)
            yn_ref[:, wide] = yn.astype(yn_ref.dtype)
            state[k] = st_out

    outs, rode = _pcall(
        body, grid=(SSD_N_GROUPS // gb, n_chunks),
        in_specs=[sp["xs"], sp["bm"], sp["cm"], sp["dt"], sp["vec"], sp["vec"], sp["vec"], sp["z"], sp["gw"]],
        out_specs=[sp["xs"], sp["st"]],
        out_shape=[jax.ShapeDtypeStruct((s_dim, SSD_D_INNER), BF16),
                   jax.ShapeDtypeStruct((SSD_N_GROUPS, n_chunks, SSD_GROUP_W, SSD_D_STATE), F32)],
        scratch_shapes=[pltpu.VMEM((gb, SSD_GROUP_W, SSD_D_STATE), F32)],
        args=[act, act, act, dtg, bias, alog, dskip, pzx, gw], sem=("parallel", "arbitrary"), name=name, rider=rider)
    return (outs, rode) if rider is not None else outs


def _ssd_bwd(act, dtg, bias, alog, dskip, pzx, gw, states, dyn, name, rider=None):
    s_dim = act.shape[0]
    n_chunks = s_dim // SSD_CHUNK
    gb = SSD_GB_BWD
    sp = _ssd_specs(n_chunks, True, gb)
    rc = lambda c: n_chunks - 1 - c

    def body(xs, bm, cm, dt, b_ref, a_ref, d_ref, z, gw_ref, st_ref, dyn_ref,
             dxs_ref, dbm_ref, dcm_ref, ddt_ref, db_ref, da_ref, dd_ref, dz_ref, dgw_ref, dstate):
        first = pl.program_id(1) == 0

        @pl.when(first)
        def _():
            dstate[...] = jnp.zeros_like(dstate)
            db_ref[...] = jnp.zeros_like(db_ref)
            da_ref[...] = jnp.zeros_like(da_ref)
            dd_ref[...] = jnp.zeros_like(dd_ref)
            dgw_ref[...] = jnp.zeros_like(dgw_ref)

        fn = functools.partial(_ssd_step, dot=_gdot, cumsum=_cumsum_rows)
        for k in range(gb):
            wide, lanes = _group_cols(k, SSD_GROUP_W), _group_cols(k, LANES)
            _, vjp = jax.vjp(fn, xs[:, wide], bm[:, lanes], cm[:, lanes], dt[k], b_ref[k], a_ref[k], d_ref[k], st_ref[k], z[:, wide],
                             gw_ref[:, wide])
            dxs, dbm, dcm, ddt, db, da, dd, dst, dz, dgw = vjp((dyn_ref[:, wide], dstate[k]))
            dxs_ref[:, wide] = dxs
            dbm_ref[:, lanes] = dbm
            dcm_ref[:, lanes] = dcm
            ddt_ref[k] = ddt
            dz_ref[:, wide] = dz.astype(dz_ref.dtype)
            db_ref[k] += db
            da_ref[k] += da
            dd_ref[k] += dd
            dgw_ref[:, wide] += dgw
            dstate[k] = dst

    bc = pl.BlockSpec((SSD_CHUNK, gb * LANES), lambda g, c: (rc(c), g))
    outs, rode = _pcall(
        body, grid=(SSD_N_GROUPS // gb, n_chunks),
        in_specs=[sp["xs"], sp["bm"], sp["cm"], sp["dt"], sp["vec"], sp["vec"], sp["vec"], sp["z"], sp["gw"], sp["st"], sp["xs"]],
        out_specs=[sp["xs"], bc, bc, sp["dt"], sp["vec"], sp["vec"], sp["vec"], sp["xs"], sp["gw"]],
        out_shape=[jax.ShapeDtypeStruct((s_dim, SSD_D_INNER), F32),
                   jax.ShapeDtypeStruct((s_dim, SSD_N_GROUPS * SSD_D_STATE), F32),
                   jax.ShapeDtypeStruct((s_dim, SSD_N_GROUPS * SSD_D_STATE), F32),
                   jax.ShapeDtypeStruct((SSD_N_GROUPS, s_dim, LANES), F32),
                   jax.ShapeDtypeStruct((SSD_N_GROUPS, 1, LANES), F32),
                   jax.ShapeDtypeStruct((SSD_N_GROUPS, 1, LANES), F32),
                   jax.ShapeDtypeStruct((SSD_N_GROUPS, 1, LANES), F32),
                   jax.ShapeDtypeStruct((s_dim, SSD_ZX), BF16),
                   jax.ShapeDtypeStruct((1, SSD_D_INNER), F32)],
        scratch_shapes=[pltpu.VMEM((gb, SSD_GROUP_W, SSD_D_STATE), F32)],
        args=[act, act, act, dtg, bias, alog, dskip, pzx, gw, states, dyn], sem=("arbitrary", "arbitrary"), name=name, rider=rider)
    return (outs, rode) if rider is not None else outs


SB_T = 128
SB_GROUP = 8
SB_WIDE = SB_GROUP * SB_T
SB_HB = 4
SB_SCALE = 1.0 / math.sqrt(SB_HEAD_DIM)


def _qknorm_fwd(proj, qw, kw, name, tm=512):
    s_dim = proj.shape[0]

    def body(q_ref, k_ref, v_ref, qw_ref, kw_ref, qo, ko, vo):
        for hh in range(SB_HB):
            qo[:, _head_lanes(hh)] = _rms(q_ref[:, _head_lanes(hh)], qw_ref[...], NORM_EPS).astype(BF16)
            ko[:, _head_lanes(hh)] = _rms(k_ref[:, _head_lanes(hh)], kw_ref[...], NORM_EPS).astype(BF16)
        vo[...] = v_ref[...].astype(BF16)

    groups = SB_N_HEADS // SB_HB
    blk = lambda o: pl.BlockSpec((tm, SB_HB * SB_HEAD_DIM), lambda i, h: (i, o + h))
    vec = pl.BlockSpec((1, SB_HEAD_DIM), lambda i, h: (0, 0))
    return pl.pallas_call(
        body, grid=(s_dim // tm, groups),
        in_specs=[blk(0), blk(groups), blk(2 * groups), vec, vec],
        out_specs=[blk(0)] * 3,
        out_shape=[jax.ShapeDtypeStruct((s_dim, SB_WIDTH), BF16)] * 3,
        compiler_params=_params("parallel", "parallel"), name=name,
    )(proj, proj, proj, qw, kw)


def _qknorm_bwd(proj, qw, kw, dqn, dkn, name, tm=512):
    s_dim = proj.shape[0]

    def body(q_ref, k_ref, dq_ref, dk_ref, qw_ref, kw_ref, dqo, dko, dqw, dkw):
        @pl.when((pl.program_id(0) == 0) & (pl.program_id(1) == 0))
        def _():
            dqw[...] = jnp.zeros_like(dqw)
            dkw[...] = jnp.zeros_like(dkw)

        fn = lambda a, b: _rms(a, b, NORM_EPS)
        for hh in range(SB_HB):
            lanes = _head_lanes(hh)
            for x_ref, w_ref, d_ref, dx_out, dw_out in ((q_ref, qw_ref, dq_ref, dqo, dqw), (k_ref, kw_ref, dk_ref, dko, dkw)):
                _, vjp = jax.vjp(fn, x_ref[:, lanes], w_ref[...])
                dx, dw = vjp(d_ref[:, lanes])
                dx_out[:, lanes] = dx.astype(BF16)
                dw_out[...] += dw

    groups = SB_N_HEADS // SB_HB
    blk = lambda o: pl.BlockSpec((tm, SB_HB * SB_HEAD_DIM), lambda i, h: (i, o + h))
    vec = pl.BlockSpec((1, SB_HEAD_DIM), lambda i, h: (0, 0))
    return pl.pallas_call(
        body, grid=(s_dim // tm, groups),
        in_specs=[blk(0), blk(groups), blk(0), blk(0), vec, vec],
        out_specs=[blk(0), blk(0), vec, vec],
        out_shape=[jax.ShapeDtypeStruct((s_dim, SB_WIDTH), BF16)] * 2 + [jax.ShapeDtypeStruct((1, SB_HEAD_DIM), F32)] * 2,
        compiler_params=_params("arbitrary", "arbitrary"), name=name,
    )(proj, proj, dqn, dkn, qw, kw)


def _sb_logits(q, k, strict):
    z = _dot(q, k, "nt") * SB_SCALE
    lb = jnp.minimum(z, 0.0) - jnp.log(1.0 + jnp.exp(-jnp.abs(z)))
    lm = lb - z
    if strict is not None:
        lm = jnp.where(strict, lm, 0.0)
    return lb, lm


def _sb_strict(qi, grp):
    r = lax.broadcasted_iota(jnp.int32, (SB_T, SB_WIDE), 0) + qi * SB_T
    c = lax.broadcasted_iota(jnp.int32, (SB_T, SB_WIDE), 1) + grp * SB_WIDE
    return c < r


def _head_lanes(hh):
    return slice(hh * SB_HEAD_DIM, (hh + 1) * SB_HEAD_DIM)


def _sb_fwd(qn, kn, vb, proj, name, rider=None):
    s_dim = qn.shape[0]
    nq = s_dim // SB_T
    assert nq % SB_GROUP == 0

    def body(q_ref, k_ref, v_ref, g_ref, og_ref, o_ref, t_ref):
        qi = pl.program_id(1)
        top = qi // SB_GROUP
        after = _tri(SB_T, True, strict=True)
        qs = [q_ref[:, _head_lanes(hh)] for hh in range(SB_HB)]

        def step(grp, masked, carries):
            start = pl.multiple_of(grp * SB_WIDE, SB_WIDE)
            strict = _sb_strict(qi, grp) if masked else None
            out = []
            for hh in range(SB_HB):
                o_acc, cr = carries[hh]
                k = k_ref[pl.ds(start, SB_WIDE), _head_lanes(hh)]
                v = v_ref[pl.ds(start, SB_WIDE), _head_lanes(hh)]
                lb, lm = _sb_logits(qs[hh], k, strict)
                rest = [None] * SB_GROUP
                for t in reversed(range(SB_GROUP)):
                    lm_t = lm[:, t * SB_T:(t + 1) * SB_T]
                    rest[t] = cr + _split_dot(lm_t, after, 2, True)
                    cr = cr + jnp.sum(lm_t, axis=1, keepdims=True)
                a = jnp.exp(lb + jnp.concatenate(rest, axis=1))
                if masked:
                    a = jnp.where(strict, a, 0.0)
                out.append((o_acc + _dot(a, v), cr))
            return tuple(out)

        init = tuple((jnp.zeros((SB_T, SB_HEAD_DIM), F32), jnp.zeros((SB_T, 1), F32)) for _ in range(SB_HB))
        carries = step(top, True, init)
        carries = lax.fori_loop(0, top, lambda i, c: step(top - 1 - i, False, c), carries)
        for hh in range(SB_HB):
            o, tot = carries[hh]
            g = g_ref[:, _head_lanes(hh)]
            o_ref[:, _head_lanes(hh)] = o
            og_ref[:, _head_lanes(hh)] = (o * (g * _sigmoid(g))).astype(og_ref.dtype)
            t_ref[hh] = jnp.broadcast_to(tot, (SB_T, LANES))

    wide = SB_HB * SB_HEAD_DIM
    qb = pl.BlockSpec((SB_T, wide), lambda h, i: (i, h))
    kv = pl.BlockSpec((s_dim, wide), lambda h, i: (0, h))
    outs, rode = _pcall(
        body, grid=(SB_N_HEADS // SB_HB, nq),
        in_specs=[qb, kv, kv, pl.BlockSpec((SB_T, wide), lambda h, i: (i, 3 * SB_N_HEADS // SB_HB + h))],
        out_specs=[qb, qb, pl.BlockSpec((SB_HB, SB_T, LANES), lambda h, i: (h, i, 0))],
        out_shape=[jax.ShapeDtypeStruct((s_dim, SB_WIDTH), BF16), jax.ShapeDtypeStruct((s_dim, SB_WIDTH), F32),
                   jax.ShapeDtypeStruct((SB_N_HEADS, s_dim, LANES), F32)],
        args=[qn, kn, vb, proj], sem=("parallel", "arbitrary"), name=name, rider=rider)
    return (outs, rode) if rider is not None else outs


def _sb_bwd(qn, kn, vb, proj, o, tot, dog, name, rider=None):
    s_dim = qn.shape[0]
    nq = s_dim // SB_T
    assert nq % SB_GROUP == 0

    def body(q_ref, k_ref, v_ref, g_ref, o_ref, t_ref, dog_ref, dq_ref, dk_ref, dv_ref, dvb_ref, dg_ref):
        qi = pl.program_id(1)
        top = qi // SB_GROUP

        @pl.when(qi == 0)
        def _():
            dk_ref[...] = jnp.zeros_like(dk_ref)
            dv_ref[...] = jnp.zeros_like(dv_ref)

        after = _tri(SB_T, True, strict=True)
        before = _tri(SB_T, False, strict=True)
        qs, dos, totals = [], [], []
        for hh in range(SB_HB):
            g = g_ref[:, _head_lanes(hh)]
            sg = _sigmoid(g)
            dog_v = dog_ref[:, _head_lanes(hh)]
            dg_ref[:, _head_lanes(hh)] = (dog_v * o_ref[:, _head_lanes(hh)] * (sg * (1.0 + g * (1.0 - sg)))).astype(dg_ref.dtype)
            dos.append((dog_v * (g * sg)).astype(BF16))
            qs.append(q_ref[:, _head_lanes(hh)])
            totals.append(t_ref[hh][:, 0:1])

        def step(grp, masked, carries):
            start = pl.multiple_of(grp * SB_WIDE, SB_WIDE)
            strict = _sb_strict(qi, grp) if masked else None
            out = []
            for hh in range(SB_HB):
                dq_acc, cp, ce = carries[hh]
                q, do = qs[hh], dos[hh]
                k = k_ref[pl.ds(start, SB_WIDE), _head_lanes(hh)]
                v = v_ref[pl.ds(start, SB_WIDE), _head_lanes(hh)]
                lb, lm = _sb_logits(q, k, strict)
                rest = []
                for t in range(SB_GROUP):
                    lm_t = lm[:, t * SB_T:(t + 1) * SB_T]
                    cp = cp + jnp.sum(lm_t, axis=1, keepdims=True)
                    rest.append((totals[hh] - cp) + _split_dot(lm_t, after, 2, True))
                a = jnp.exp(lb + jnp.concatenate(rest, axis=1))
                if masked:
                    a = jnp.where(strict, a, 0.0)
                e = a * _dot(do, v, "nt")
                excl = []
                for t in range(SB_GROUP):
                    e_t = e[:, t * SB_T:(t + 1) * SB_T]
                    excl.append(ce + _split_dot(e_t, before, 1, True))
                    ce = ce + jnp.sum(e_t, axis=1, keepdims=True)
                eex = jnp.concatenate(excl, axis=1)
                if masked:
                    eex = jnp.where(strict, eex, 0.0)
                sig = jnp.exp(lb)
                dz = (e * (1.0 - sig) - eex * sig) * SB_SCALE
                dv_ref[pl.ds(start, SB_WIDE), _head_lanes(hh)] += _dot(a, do, "tn")
                dk_ref[pl.ds(start, SB_WIDE), _head_lanes(hh)] += _dot(dz, q, "tn")
                out.append((dq_acc + _dot(dz, k), cp, ce))
            return tuple(out)

        zero = jnp.zeros((SB_T, 1), F32)
        init = tuple((jnp.zeros((SB_T, SB_HEAD_DIM), F32), zero, zero) for _ in range(SB_HB))
        carries = lax.fori_loop(0, top, lambda i, c: step(i, False, c), init)
        carries = step(top, True, carries)
        for hh in range(SB_HB):
            dq_ref[:, _head_lanes(hh)] = carries[hh][0]

        @pl.when(qi == nq - 1)
        def _():
            dvb_ref[...] = dv_ref[...].astype(BF16)

    wide = SB_HB * SB_HEAD_DIM
    qb = pl.BlockSpec((SB_T, wide), lambda h, i: (i, h))
    kv = pl.BlockSpec((s_dim, wide), lambda h, i: (0, h))
    outs, rode = _pcall(
        body, grid=(SB_N_HEADS // SB_HB, nq),
        in_specs=[qb, kv, kv, pl.BlockSpec((SB_T, wide), lambda h, i: (i, 3 * SB_N_HEADS // SB_HB + h)), qb,
                  pl.BlockSpec((SB_HB, SB_T, LANES), lambda h, i: (h, i, 0)), qb],
        out_specs=[qb, kv, kv, kv, qb],
        out_shape=[jax.ShapeDtypeStruct((s_dim, SB_WIDTH), F32), jax.ShapeDtypeStruct((s_dim, SB_WIDTH), F32),
                   jax.ShapeDtypeStruct((s_dim, SB_WIDTH), F32), jax.ShapeDtypeStruct((s_dim, SB_WIDTH), BF16),
                   jax.ShapeDtypeStruct((s_dim, SB_WIDTH), BF16)],
        args=[qn, kn, vb, proj, o, tot, dog], sem=("parallel", "arbitrary"), name=name, rider=rider)
    return (outs, rode) if rider is not None else outs


def _adamw_math(w, g, m, v):
    m = ADAM_B1 * m + (1.0 - ADAM_B1) * g
    v = ADAM_B2 * v + (1.0 - ADAM_B2) * (g * g)
    m_hat = m / (1.0 - ADAM_B1 ** ADAM_STEP)
    v_hat = v / (1.0 - ADAM_B2 ** ADAM_STEP)
    delta = -ADAM_LR * (m_hat / (jnp.sqrt(v_hat) + ADAM_EPS) + ADAM_WD * w)
    return delta, m, v


def _row_block(rows, cols, itemsize=4, limit=1 << 20):
    tr = rows
    while tr * cols * itemsize > limit and tr % (2 * BF16_ROWS) == 0:
        tr //= 2
    return tr


def _divisor_block(rows, cols, itemsize=4, limit=2 << 20):
    best = BF16_ROWS
    for t in range(BF16_ROWS, rows + 1, BF16_ROWS):
        if rows % t == 0 and t * cols * itemsize <= limit:
            best = t
    return best


def _adamw(w, g, m, v, name, rider=None):
    n, rows, cols = w.shape
    tr = rows if rows * cols * 4 <= (2 << 20) else _divisor_block(rows, cols)

    def body(w_ref, g_ref, m_ref, v_ref, d_out, m_out, v_out):
        d, m_new, v_new = _adamw_math(w_ref[...], g_ref[...], m_ref[...], v_ref[...])
        d_out[...] = d
        m_out[...] = m_new
        v_out[...] = v_new

    blk = pl.BlockSpec((None, tr, cols), lambda i, j: (i, j, 0))
    outs, rode = _pcall(
        body, grid=(n, rows // tr), in_specs=[blk] * 4, out_specs=[blk] * 3,
        out_shape=[jax.ShapeDtypeStruct(w.shape, F32)] * 3,
        args=[w, g, m, v], sem=("parallel", "parallel"), name=name, rider=rider)
    return (outs, rode) if rider is not None else outs


_FLIPS = ((1, 0), (0, 1), (1, 1))


def _place():
    return lax.axis_index("x"), lax.axis_index("y"), lax.axis_index("c")


def _flip(v, f):
    return 1 - v if f else v


def _half_rows(ref, lead, hc, hr, sub=(0, 1)):
    part = hr // sub[1]
    return ref.at[(*lead, pl.ds(pl.multiple_of(hc * hr + sub[0] * part, BF16_ROWS), part), slice(None))]


def _half_cols(ref, lead, hc, hw):
    return ref.at[(*lead, pl.ds(pl.multiple_of(hc * hw, LANES), hw))]


def _rows_of_chip(chip, r):
    return pl.ds(pl.multiple_of(chip * r, BF16_ROWS), r)


def _slot_half(gathered, shard_shape, chip, l, hc, sub=(0, 1)):
    r, c = shard_shape[1:]
    if len(gathered.shape) == 3:
        assert sub == (0, 1)
        return _half_cols(gathered, (l, _rows_of_chip(chip, r)), hc, c // 2)
    return _half_rows(gathered, (chip, l), hc, r // 2, sub)


def _shard_half(shard, stacked, l, hc, sub=(0, 1)):
    r, c = shard.shape[1:]
    if stacked:
        assert sub == (0, 1)
        return _half_cols(shard, (l, slice(None)), hc, c // 2)
    return _half_rows(shard, (l,), hc, r // 2, sub)


def _piece(piece):
    return piece[0], piece[1], tuple(piece[2:]) or (0, 1)


def _remote(src, dst, send, recv, k, to):
    return pltpu.make_async_remote_copy(src_ref=src, dst_ref=dst, send_sem=send.at[k], recv_sem=recv.at[k], device_id=to,
                                        device_id_type=MESH)


def _comm_call(reads, writes, n_sems, phases, name):
    passed = [k for k, w in enumerate(writes) if not isinstance(w, jax.ShapeDtypeStruct)]
    n_rd = len(reads)

    def body(*refs):
        rd = refs[:n_rd]
        wr = refs[n_rd + len(passed):n_rd + len(passed) + len(writes)]
        send, recv = refs[-2:]
        for phase in phases:
            sends, arrivals = phase(rd, wr, send, recv)
            for cp in sends:
                cp.start()
            for cp in arrivals:
                cp.wait_recv()
            for cp in sends:
                cp.wait_send()

    return pl.pallas_call(
        body, in_specs=[_ANY] * (n_rd + len(passed)), out_specs=[_ANY] * len(writes),
        out_shape=[jax.ShapeDtypeStruct(w.shape, w.dtype) for w in writes],
        input_output_aliases={n_rd + pos: k for pos, k in enumerate(passed)},
        scratch_shapes=[pltpu.SemaphoreType.DMA((n_sems,)), pltpu.SemaphoreType.DMA((n_sems,))], name=name,
    )(*reads, *[writes[k] for k in passed])


def _ag_ici(pieces, names, base=0):
    def phase(shards, gathered, send, recv):
        x, y, c = _place()
        me = 2 * x + y
        sends, arrivals = [], []
        for k, piece in enumerate(pieces):
            n, l, sub = _piece(piece)
            a = names.index(n)
            shape = shards[a].shape
            src = _shard_half(shards[a], len(gathered[a].shape) == 3, l, c, sub)
            for j, (fx, fy) in enumerate(_FLIPS):
                tx, ty = _flip(x, fx), _flip(y, fy)
                sends.append(_remote(src, _slot_half(gathered[a], shape, me, l, c, sub), send, recv, base + 3 * k + j, (tx, ty, c)))
                arrivals.append(_remote(src, _slot_half(gathered[a], shape, 2 * tx + ty, l, c, sub), send, recv, base + 3 * k + j,
                                        (tx, ty, c)))
        return sends, arrivals

    return phase


def _ag_pass_on(pieces, names, shapes, base=0):
    def phase(_, gathered, send, recv):
        x, y, c = _place()
        sibling = (x, y, 1 - c)
        sends, arrivals = [], []
        for k, piece in enumerate(pieces):
            n, l, sub = _piece(piece)
            a = names.index(n)
            for j, (fx, fy) in enumerate(_FLIPS):
                chip = 2 * _flip(x, fx) + _flip(y, fy)
                landed = _slot_half(gathered[a], shapes[a], chip, l, c, sub)
                sends.append(_remote(landed, landed, send, recv, base + 3 * k + j, sibling))
                arrivals.append(_remote(landed, _slot_half(gathered[a], shapes[a], chip, l, 1 - c, sub), send, recv, base + 3 * k + j, sibling))
        return sends, arrivals

    return phase


def _other_half(ref, hc):
    if len(ref.shape) == 3:
        return _half_cols(ref, (slice(None), slice(None)), hc, ref.shape[2] // 2)
    return _half_rows(ref, (slice(None), slice(None)), hc, ref.shape[2] // 2)


def _half_shape(shape):
    return shape[:2] + (shape[2] // 2,) if len(shape) == 3 else shape[:2] + (shape[2] // 2, shape[3])


def _exchange_phase(n_arr):
    def phase(ins, outs, send, recv):
        x, y, c = _place()
        cps = [_remote(_other_half(ins[a], 1 - c), outs[a], send, recv, a, (x, y, 1 - c)) for a in range(n_arr)]
        return cps, cps

    return phase


def _exchange_outs(grads):
    return [jax.ShapeDtypeStruct(_half_shape(g.shape), g.dtype) for g in grads]


def _pair_exchange(grads, name):
    return _comm_call(grads, _exchange_outs(grads), len(grads), [_exchange_phase(len(grads))], name)


def _exchange_rider(grads):
    return _Rider(grads, _exchange_outs(grads), len(grads), _exchange_phase(len(grads)))


def _pair_sum_stacked(g, got, place, name):
    _, rows, hw = got.shape
    tr = _divisor_block(rows, hw)

    def body(place_ref, g_ref, r_ref, o_ref):
        o_ref[...] = (g_ref[...].astype(F32) + r_ref[...].astype(F32)).astype(o_ref.dtype)

    blk = pl.BlockSpec((None, tr, hw), lambda i, pr: (0, i, 0))
    return pl.pallas_call(
        body,
        grid_spec=pltpu.PrefetchScalarGridSpec(
            num_scalar_prefetch=1, grid=(rows // tr,),
            in_specs=[pl.BlockSpec((None, tr, hw), lambda i, pr: (0, i, pr[1])), blk], out_specs=blk),
        out_shape=jax.ShapeDtypeStruct(got.shape, BF16),
        compiler_params=_params("parallel"), name=name,
    )(place, g, got)


def _pair_sum(g, got, place, name):
    if len(g.shape) == 3:
        return _pair_sum_stacked(g, got, place, name)
    _, layers, hr, cols = got.shape
    tr = _row_block(hr, cols, limit=2 << 20)
    per = hr // tr

    def body(place_ref, g_ref, r_ref, o_ref):
        o_ref[...] = (g_ref[...].astype(F32) + r_ref[...].astype(F32)).astype(o_ref.dtype)

    blk = pl.BlockSpec((None, None, tr, cols), lambda k, l, i, pr: (k, l, i, 0))
    return pl.pallas_call(
        body,
        grid_spec=pltpu.PrefetchScalarGridSpec(
            num_scalar_prefetch=1, grid=(4, layers, per),
            in_specs=[pl.BlockSpec((None, None, tr, cols), lambda k, l, i, pr: (k, l, pr[1] * per + i, 0)), blk],
            out_specs=blk),
        out_shape=jax.ShapeDtypeStruct(got.shape, BF16),
        compiler_params=_params("parallel", "parallel", "parallel"), name=name,
    )(place, g, got)


def _scatter_phase(n_arr):
    def phase(ins, outs, send, recv):
        x, y, c = _place()
        cps = []
        for a in range(n_arr):
            for j, (fx, fy) in enumerate(_FLIPS):
                tx, ty = _flip(x, fx), _flip(y, fy)
                if len(ins[a].shape) == 3:
                    src = ins[a].at[:, _rows_of_chip(2 * tx + ty, ins[a].shape[1] // 4), :]
                else:
                    src = ins[a].at[2 * tx + ty]
                cps.append(_remote(src, outs[a].at[j], send, recv, 3 * a + j, (tx, ty, c)))
        return cps, cps

    return phase


def _scatter_outs(pairs):
    return [jax.ShapeDtypeStruct((3, 1, p.shape[1] // 4, p.shape[2]) if len(p.shape) == 3 else (3,) + p.shape[1:], p.dtype) for p in pairs]


def _chip_scatter(pairs, name):
    return _comm_call(pairs, _scatter_outs(pairs), 3 * len(pairs), [_scatter_phase(len(pairs))], name)


def _scatter_rider(pairs):
    return _Rider(pairs, _scatter_outs(pairs), 3 * len(pairs), _scatter_phase(len(pairs)))


def _chip_sum_stacked(p, got, place, layer, layers, o_buf, name, row0=0, rows=None):
    _, r, hw = got.shape[1:]
    rows = rows or r
    tr = _divisor_block(math.gcd(r, row0) if row0 else r, hw)
    per = r // tr
    first = row0 // tr

    def body(place_ref, p_ref, r_ref, *rest):
        o_ref = rest[-1]
        acc = p_ref[...].astype(F32)
        for j in range(3):
            acc = acc + r_ref[j].astype(F32)
        o_ref[...] = acc

    has_buf = o_buf is not None
    return pl.pallas_call(
        body,
        grid_spec=pltpu.PrefetchScalarGridSpec(
            num_scalar_prefetch=1, grid=(per,),
            in_specs=[pl.BlockSpec((None, tr, hw), lambda i, pr: (0, pr[0] * per + i, 0)),
                      pl.BlockSpec((3, None, tr, hw), lambda i, pr: (0, 0, i, 0))] + ([_ANY] if has_buf else []),
            out_specs=pl.BlockSpec((None, tr, hw), lambda i, pr: (layer, first + i, pr[1]))),
        out_shape=jax.ShapeDtypeStruct((layers, rows, 2 * hw), F32),
        input_output_aliases={3: 0} if has_buf else {},
        compiler_params=_params("parallel"), name=name,
    )(*((place, p, got) + ((o_buf,) if has_buf else ())))


def _chip_sum(p, got, place, layer, layers, o_buf, name):
    if len(p.shape) == 3:
        return _chip_sum_stacked(p, got, place, layer, layers, o_buf, name)
    _, _, hr, cols = p.shape
    tr = _row_block(hr, cols, limit=2 << 20)
    per = hr // tr

    def body(place_ref, p_ref, r_ref, *rest):
        o_ref = rest[-1]
        acc = p_ref[...].astype(F32)
        for j in range(3):
            acc = acc + r_ref[j].astype(F32)
        o_ref[...] = acc

    has_buf = o_buf is not None
    return pl.pallas_call(
        body,
        grid_spec=pltpu.PrefetchScalarGridSpec(
            num_scalar_prefetch=1, grid=(per,),
            in_specs=[pl.BlockSpec((None, None, tr, cols), lambda i, pr: (pr[0], 0, i, 0)),
                      pl.BlockSpec((3, None, tr, cols), lambda i, pr: (0, 0, i, 0))] + ([_ANY] if has_buf else []),
            out_specs=pl.BlockSpec((None, tr, cols), lambda i, pr: (layer, pr[1] * per + i, 0))),
        out_shape=jax.ShapeDtypeStruct((layers, 2 * hr, cols), F32),
        input_output_aliases={3: 0} if has_buf else {},
        compiler_params=_params("parallel"), name=name,
    )(*((place, p, got) + ((o_buf,) if has_buf else ())))


def _pair_gather(halves, by_cols, name):
    def phase(_, bufs, send, recv):
        x, y, c = _place()
        sends, arrivals = [], []
        for a, h in enumerate(halves):
            cut = (lambda hc, a=a, h=h: _half_cols(bufs[a], (slice(None), slice(None)), hc, h.shape[2] // 2)) if by_cols[a] else (
                lambda hc, a=a, h=h: _half_rows(bufs[a], (slice(None),), hc, h.shape[1] // 2))
            sends.append(_remote(cut(c), cut(c), send, recv, a, (x, y, 1 - c)))
            arrivals.append(_remote(cut(c), cut(1 - c), send, recv, a, (x, y, 1 - c)))
        return sends, arrivals

    return _comm_call([], halves, len(halves), [phase], name)


def _allreduce_small(v, name):
    rows, cols = v.shape

    def body(v_ref, o_ref, buf, send_sems, recv_sems):
        x, y, c = _place()
        me = 4 * x + 2 * y + c
        buf[0] = v_ref[...]
        cps = []
        for k in range(1, 8):
            kx, ky, kc = (k >> 2) & 1, (k >> 1) & 1, k & 1
            cp = pltpu.make_async_remote_copy(src_ref=v_ref, dst_ref=buf.at[k], send_sem=send_sems.at[k - 1], recv_sem=recv_sems.at[k - 1],
                                              device_id=(_flip(x, kx), _flip(y, ky), _flip(c, kc)), device_id_type=MESH)
            cp.start()
            cps.append(cp)
        for cp in cps:
            cp.wait()
        acc = buf[me]
        for d in range(1, 8):
            acc = acc + buf[jnp.bitwise_xor(d, me)]
        o_ref[...] = acc

    vm = pl.BlockSpec(memory_space=pltpu.VMEM)
    return pl.pallas_call(
        body, in_specs=[vm], out_specs=vm, out_shape=jax.ShapeDtypeStruct((rows, cols), F32),
        scratch_shapes=[pltpu.VMEM((8, rows, cols), F32), pltpu.SemaphoreType.DMA((7,)), pltpu.SemaphoreType.DMA((7,))],
        name=name,
    )(v)


def _pad_lanes(a):
    return jnp.pad(a, ((0, 0), (0, LANES - a.shape[1])))


def _group_lanes(v):
    return jnp.pad(v.reshape(SSD_N_GROUPS, 1, 8), ((0, 0), (0, 0), (0, LANES - 8)))


def kernel(x, p, norm_w, ssd_in_w, ssd_conv_w, ssd_conv_b, ssd_dt_bias, ssd_a_log, ssd_d, ssd_gnorm_w, ssd_out_w, sb_in_w, sb_qn_w, sb_kn_w, sb_out_w, ple_norm_w, ple_gate_w, ple_proj_w, loss_target, m_norm_w, m_ssd_in_w, m_ssd_conv_w, m_ssd_conv_b, m_ssd_dt_bias, m_ssd_a_log, m_ssd_d, m_ssd_gnorm_w, m_ssd_out_w, m_sb_in_w, m_sb_qn_w, m_sb_kn_w, m_sb_out_w, m_ple_norm_w, m_ple_gate_w, m_ple_proj_w, v_norm_w, v_ssd_in_w, v_ssd_conv_w, v_ssd_conv_b, v_ssd_dt_bias, v_ssd_a_log, v_ssd_d, v_ssd_gnorm_w, v_ssd_out_w, v_sb_in_w, v_sb_qn_w, v_sb_kn_w, v_sb_out_w, v_ple_norm_w, v_ple_gate_w, v_ple_proj_w):
    w_in = dict(norm_w=norm_w, ssd_in_w=ssd_in_w, ssd_conv_w=ssd_conv_w, ssd_conv_b=ssd_conv_b, ssd_dt_bias=ssd_dt_bias,
                ssd_a_log=ssd_a_log, ssd_d=ssd_d, ssd_gnorm_w=ssd_gnorm_w, ssd_out_w=ssd_out_w, sb_in_w=sb_in_w, sb_qn_w=sb_qn_w,
                sb_kn_w=sb_kn_w, sb_out_w=sb_out_w, ple_norm_w=ple_norm_w, ple_gate_w=ple_gate_w, ple_proj_w=ple_proj_w)
    m_in = dict(norm_w=m_norm_w, ssd_in_w=m_ssd_in_w, ssd_conv_w=m_ssd_conv_w, ssd_conv_b=m_ssd_conv_b, ssd_dt_bias=m_ssd_dt_bias,
                ssd_a_log=m_ssd_a_log, ssd_d=m_ssd_d, ssd_gnorm_w=m_ssd_gnorm_w, ssd_out_w=m_ssd_out_w, sb_in_w=m_sb_in_w,
                sb_qn_w=m_sb_qn_w, sb_kn_w=m_sb_kn_w, sb_out_w=m_sb_out_w, ple_norm_w=m_ple_norm_w, ple_gate_w=m_ple_gate_w,
                ple_proj_w=m_ple_proj_w)
    v_in = dict(norm_w=v_norm_w, ssd_in_w=v_ssd_in_w, ssd_conv_w=v_ssd_conv_w, ssd_conv_b=v_ssd_conv_b, ssd_dt_bias=v_ssd_dt_bias,
                ssd_a_log=v_ssd_a_log, ssd_d=v_ssd_d, ssd_gnorm_w=v_ssd_gnorm_w, ssd_out_w=v_ssd_out_w, sb_in_w=v_sb_in_w,
                sb_qn_w=v_sb_qn_w, sb_kn_w=v_sb_kn_w, sb_out_w=v_sb_out_w, ple_norm_w=v_ple_norm_w, ple_gate_w=v_ple_gate_w,
                ple_proj_w=v_ple_proj_w)
    ix, iy, ic = lax.axis_index("x"), lax.axis_index("y"), lax.axis_index("c")
    chip = (2 * ix + iy).astype(jnp.int32)
    place = jnp.stack([chip, ic.astype(jnp.int32)])
    zero = jnp.zeros((), jnp.int32)
    big_names = [n for n, _, _ in _BIG]
    layers_of = {n: s[0] for n, s, _ in _BIG}
    cut_of = {n: cut for n, _, cut in _BIG}

    def layer_pieces(i):
        mixer = ("ssd_in_w", "ssd_out_w") if i % 2 == 0 else ("sb_in_w", "sb_out_w")
        return [(mixer[0], i // 2), (mixer[1], i // 2), ("ple_gate_w", i), ("ple_proj_w", i)]

    def names_of(pieces):
        return [n for n in big_names if any(n == q[0] for q in pieces)]

    held = lambda n, a: a.transpose(0, 2, 1) if cut_of[n] == "stack" else a
    mine = {n: held(n, w_in[n]).astype(BF16) for n in big_names}
    shard_shapes = [mine[n].shape for n in big_names]
    room = [jax.ShapeDtypeStruct((s[0], 4 * s[1], s[2]) if cut_of[n] == "stack" else (4,) + s, BF16) for n, s in zip(big_names, shard_shapes)]
    first = layer_pieces(0)[:1]
    gathered = _comm_call([mine[n] for n in big_names], room, 6 * len(first),
                          [_ag_ici(first, big_names), _ag_pass_on(first, big_names, shard_shapes, base=3 * len(first))], "allgather_layer0")
    gw = {}
    for n, g in zip(big_names, gathered):
        if cut_of[n] == "stack":
            layers, r, c = mine[n].shape
            gw[n] = lax.dynamic_update_slice(g.reshape(layers, 4, r, c), mine[n][:, None], (zero, chip, zero, zero)).reshape(g.shape)
        else:
            gw[n] = lax.dynamic_update_slice(g, mine[n][None], (chip, zero, zero, zero))

    lp = [layer_pieces(i) for i in range(DEPTH)]
    in3 = [lp[3][0] + (k, 2) for k in range(2)]
    carries = {
        "ssd_in_0": (lp[0][1:2], []), "conv_0": (lp[0][2:], lp[0][1:2]), "ssd_0": (lp[1][:1], lp[0][2:]),
        "ssd_out_0": (lp[1][1:2], lp[1][:1]), "sb_in_1": (lp[1][2:], lp[1][1:2]), "sb_1": (lp[2][:2], lp[1][2:]),
        "sb_out_1": (lp[2][2:], lp[2][:2]), "ssd_in_2": ([in3[0]], lp[2][2:]), "conv_2": (lp[3][1:2], [in3[0]]),
        "ssd_2": ([in3[1]] + lp[3][2:], lp[3][1:2]), "ssd_out_2": ([], [in3[1]]), "sb_in_3": ([], lp[3][2:]),
    }

    def gather_rider(call):
        if call not in carries:
            return None, lambda outs: outs
        ici, passing = carries[call]
        names = names_of(ici + passing)
        phases = ([_ag_ici(ici, names)] if ici else []) + (
            [_ag_pass_on(passing, names, [mine[n].shape for n in names], base=3 * len(ici))] if passing else [])

        def issue(rd, wr, send, recv):
            both = [ph(rd, wr, send, recv) for ph in phases]
            return sum((b[0] for b in both), []), sum((b[1] for b in both), [])

        def land(outs):
            outs, bufs = outs
            for n, g in zip(names, bufs):
                gw[n] = g
            return outs

        return _Rider([mine[n] for n in names], [gw[n] for n in names], 3 * (len(ici) + len(passing)), issue), land

    onehot = (jnp.arange(4) == chip).astype(F32) * (ic == 0).astype(F32)
    cw_mine = onehot[:, None, None, None] * ssd_conv_w[None]
    cw_full = _allreduce_small(cw_mine.transpose(1, 2, 0, 3).reshape(-1, LANES), "gather_conv_w").reshape(2, SSD_D_CONV, SSD_CONV_DIM)

    def wmm(a, name, layer, *, dn="nn", res=None, call, rider=None):
        return _matmul(a, gw[name], dn=dn, res=res, b_lay=(cut_of[name], layer), name=call, rider=rider)

    h = x[0]
    target = loss_target[0]
    saved = []
    for i in range(DEPTH):
        j = i // 2
        nw = norm_w[i:i + 1]
        pw = ple_norm_w[i:i + 1]
        s = dict(h=h)
        u = _rms_fwd(h, nw, f"rms_{i}")
        s["u"] = u
        if i % 2 == 0:
            w_dt = jnp.pad(gw["ssd_in_w"][j, SSD_ZX:], ((0, LANES - SSD_N_HEADS), (0, 0)))
            rider, land = gather_rider(f"ssd_in_{i}")
            pzx = land(_matmul(u, gw["ssd_in_w"], dn="nt", b_lay=("stack", j, SSD_ZX), name=f"ssd_in_{i}", rider=rider))
            pdt = _matmul(u, w_dt, dn="nt", name=f"ssd_indt_{i}")
            rider, land = gather_rider(f"conv_{i}")
            act = land(_conv_fwd(pzx, cw_full[j], ssd_conv_b[j:j + 1], f"conv_{i}", rider=rider))
            dtg = jnp.pad(pdt[:, :SSD_N_HEADS].reshape(-1, SSD_N_GROUPS, 8).transpose(1, 0, 2), ((0, 0), (0, 0), (0, LANES - 8)))
            vecs = (_group_lanes(ssd_dt_bias[j]), _group_lanes(ssd_a_log[j]), _group_lanes(ssd_d[j]))
            rider, land = gather_rider(f"ssd_{i}")
            yn, states = land(_ssd_fwd(act, dtg, *vecs, pzx, ssd_gnorm_w[j:j + 1], f"ssd_{i}", rider=rider))
            s.update(w_dt=w_dt, pzx=pzx, act=act, dtg=dtg, vecs=vecs, yn=yn, states=states)
            rider, land = gather_rider(f"ssd_out_{i}")
            h1 = land(wmm(yn, "ssd_out_w", j, res=h, call=f"ssd_out_{i}", rider=rider))
        else:
            rider, land = gather_rider(f"sb_in_{i}")
            proj = land(wmm(u, "sb_in_w", j, call=f"sb_in_{i}", rider=rider))
            qn, kn, vb = _qknorm_fwd(proj, sb_qn_w[j:j + 1], sb_kn_w[j:j + 1], f"qknorm_{i}")
            rider, land = gather_rider(f"sb_{i}")
            og, o, tot = land(_sb_fwd(qn, kn, vb, proj, f"sb_{i}", rider=rider))
            s.update(proj=proj, qn=qn, kn=kn, vb=vb, og=og, o=o, tot=tot)
            rider, land = gather_rider(f"sb_out_{i}")
            h1 = land(wmm(og, "sb_out_w", j, res=h, call=f"sb_out_{i}", rider=rider))
        n2 = _rms_fwd(h1, pw, f"ple_rms_{i}")
        gl = wmm(n2, "ple_gate_w", i, call=f"ple_gate_{i}")
        pp = wmm(p[i, 0], "ple_proj_w", i, call=f"ple_proj_{i}")
        h = _ple_fwd(h1, pp, gl, f"ple_{i}")
        s.update(h1=h1, n2=n2, gl=gl, pp=pp)
        saved.append(s)

    dh, loss_lanes = _loss_bwd(h, target, "loss")

    wg = {}
    gsmall = {n: [None] * s[0] for n, s in _SMALL}
    g_conv_w = [None, None]
    scat = {}
    pending = late = None

    def wgrad(a, b, name, layer, call, rider=None):
        out = _matmul(a, b, dn="tn", out_dtype=BF16, o_lay=(cut_of[name], 0, 1), name=call, rider=rider)
        wg[(name, layer)], rode = out if rider is not None else (out, None)
        return rode

    def pair_sums(pieces, got, tag):
        return pieces, [_pair_sum(wg[q], r, place, f"rs_pair_sum_{tag}_{k}") for k, (q, r) in enumerate(zip(pieces, got))]

    def sibling_rider(pieces):
        return _exchange_rider([wg[q] for q in pieces])

    def riding_with(own):
        return (pending[0] + own[0], pending[1] + own[1]) if pending else own

    def arrived(sent, got):
        for q, pair, g in zip(sent[0], sent[1], got):
            scat[q] = (pair, g)

    for i in reversed(range(DEPTH)):
        j = i // 2
        s = saved[i]
        nw = norm_w[i:i + 1]
        pw = ple_norm_w[i:i + 1]
        dpp, dgl = _ple_bwd(dh, s["pp"], s["gl"], f"ple_bwd_{i}")
        wgrad(p[i, 0], dpp, "ple_proj_w", i, f"d_ple_proj_{i}")
        if late is None:
            wgrad(s["n2"], dgl, "ple_gate_w", i, f"d_ple_gate_{i}")
        else:
            pending = pair_sums(late, wgrad(s["n2"], dgl, "ple_gate_w", i, f"d_ple_gate_{i}", rider=sibling_rider(late)), f"{i + 1}_in")
        dn2 = wmm(dgl, "ple_gate_w", i, dn="nt", call=f"ple_gate_bwd_{i}")
        dh1, dpw = _rms_bwd(s["h1"], pw, dn2, dh, f"ple_rms_bwd_{i}")
        gsmall["ple_norm_w"][i] = dpw
        if i % 2 == 0:
            wgrad(s["yn"], dh1, "ssd_out_w", j, f"d_ssd_out_{i}")
            early = layer_pieces(i)[1:]
            dyn, got = wmm(dh1, "ssd_out_w", j, dn="nt", call=f"ssd_out_bwd_{i}", rider=sibling_rider(early))
            riding = riding_with(pair_sums(early, got, f"{i}_out"))
            outs, got = _ssd_bwd(s["act"], s["dtg"], *s["vecs"], s["pzx"], ssd_gnorm_w[j:j + 1], s["states"], dyn, f"ssd_bwd_{i}",
                                 rider=_scatter_rider(riding[1]))
            arrived(riding, got)
            dxs, dbm, dcm, ddtg, dbias, dalog, ddsk, dz, dgw = outs
            dzx, dcw, dcb = _conv_bwd(s["pzx"], cw_full[j], ssd_conv_b[j:j + 1], dxs, dbm, dcm, dz, f"conv_bwd_{i}")
            ddt = _pad_lanes(ddtg[:, :, :8].transpose(1, 0, 2).reshape(-1, SSD_N_HEADS)).astype(BF16)
            dwt = _matmul(dzx, s["u"], dn="tn", out_dtype=BF16, out_rows=SSD_IN_DIM, name=f"d_ssd_in_{i}")
            dwt_dt = _matmul(ddt, s["u"], dn="tn", out_dtype=BF16, name=f"d_ssd_indt_{i}")
            wg[("ssd_in_w", j)] = lax.dynamic_update_slice(dwt, dwt_dt[:SSD_N_HEADS], (SSD_ZX, 0))[None]
            if i == 0:
                by_shard = wg[("ssd_in_w", 0)].reshape(4, -1, D_MODEL)
                parts = [("ssd_in_w", 0, 0), ("ssd_in_w", 0, 1)]
                wg[parts[0]] = by_shard[:, :LAST_SPLIT].reshape(1, -1, D_MODEL)
                wg[parts[1]] = by_shard[:, LAST_SPLIT:].reshape(1, -1, D_MODEL)
                last = pair_sums(parts, _pair_exchange([wg[q] for q in parts], "rs_pair_exchange_last"), "0_in")
                du, got = _matmul(dzx, gw["ssd_in_w"], b_lay=("stack", j, SSD_ZX), name=f"ssd_in_bwd_{i}",
                                  rider=_scatter_rider(last[1][1:]))
                arrived((parts[1:], last[1][1:]), got)
            else:
                du = _matmul(dzx, gw["ssd_in_w"], b_lay=("stack", j, SSD_ZX), name=f"ssd_in_bwd_{i}")
            du = _matmul(ddt, s["w_dt"], res=du, name=f"ssd_indt_bwd_{i}")
            g_conv_w[j] = dcw
            gsmall["ssd_conv_b"][j] = dcb
            gsmall["ssd_dt_bias"][j] = dbias[:, 0, :8].reshape(1, SSD_N_HEADS)
            gsmall["ssd_a_log"][j] = dalog[:, 0, :8].reshape(1, SSD_N_HEADS)
            gsmall["ssd_d"][j] = ddsk[:, 0, :8].reshape(1, SSD_N_HEADS)
            gsmall["ssd_gnorm_w"][j] = dgw
        else:
            wgrad(s["og"], dh1, "sb_out_w", j, f"d_sb_out_{i}")
            early = layer_pieces(i)[1:]
            dog, got = wmm(dh1, "sb_out_w", j, dn="nt", call=f"sb_out_bwd_{i}", rider=sibling_rider(early))
            riding = riding_with(pair_sums(early, got, f"{i}_out"))
            outs, got = _sb_bwd(s["qn"], s["kn"], s["vb"], s["proj"], s["o"], s["tot"], dog, f"sb_bwd_{i}", rider=_scatter_rider(riding[1]))
            arrived(riding, got)
            dqn, dkn, _, dvb, dg = outs
            dq, dk, dqw, dkw = _qknorm_bwd(s["proj"], sb_qn_w[j:j + 1], sb_kn_w[j:j + 1], dqn, dkn, f"qknorm_bwd_{i}")
            dproj = jnp.concatenate([dq, dk, dvb, dg], axis=1)
            du = wmm(dproj, "sb_in_w", j, dn="nt", call=f"sb_in_bwd_{i}")
            wgrad(s["u"], dproj, "sb_in_w", j, f"d_sb_in_{i}")
            gsmall["sb_qn_w"][j] = dqw
            gsmall["sb_kn_w"][j] = dkw
        dh, dnw = _rms_bwd(s["h"], nw, du, dh1, f"rms_bwd_{i}")
        gsmall["norm_w"][i] = dnw
        late = layer_pieces(i)[:1]
    grad_x = dh[None]

    def reduced(names, call):
        halves = []
        for n in names:
            buf = None
            for l in range(layers_of[n]):
                if (n, l, 0) in scat:
                    r = shard_shapes[big_names.index(n)][1]
                    for part, row0 in ((0, 0), (1, LAST_SPLIT)):
                        buf = _chip_sum_stacked(*scat[(n, l, part)], place, l, layers_of[n], buf, f"rs_chip_sum_{n}_{l}_{part}", row0, r)
                else:
                    buf = _chip_sum(*scat[(n, l)], place, l, layers_of[n], buf, f"rs_chip_sum_{n}_{l}")
            halves.append(buf)
        return dict(zip(names, _pair_gather(halves, [cut_of[n] == "stack" for n in names], call)))

    def updated(n, rider=None):
        return _adamw(held(n, w_in[n]), g_big[n], held(n, m_in[n]), held(n, v_in[n]), f"adamw_{n}", rider=rider)

    done_early = ["sb_in_w", "sb_out_w"]
    g_big = reduced(done_early, "rs_pair_gather_sb")
    step = {}
    step["sb_in_w"], got = updated("sb_in_w", rider=_scatter_rider(last[1][:1]))
    arrived((last[0][:1], last[1][:1]), got)
    g_big.update(reduced([n for n in big_names if n not in done_early], "rs_pair_gather"))

    small_parts = [jnp.concatenate(gsmall[n], axis=0).reshape(-1) for n, _ in _SMALL]
    small_parts.append(jnp.stack(g_conv_w).reshape(-1))
    small_parts.append(loss_lanes.reshape(-1))
    small_sum = _allreduce_small(jnp.concatenate(small_parts).reshape(-1, LANES), "allreduce_small").reshape(-1)
    g_small, off = {}, 0
    for n, shape in _SMALL:
        size = math.prod(shape)
        g_small[n] = small_sum[off:off + size].reshape(shape)
        off += size
    cw_size = 2 * SSD_D_CONV * SSD_CONV_DIM
    g_cw_full = small_sum[off:off + cw_size].reshape(2, SSD_D_CONV, 4, SSD_CONV_DIM // 4)
    g_small["ssd_conv_w"] = jnp.sum(g_cw_full * (jnp.arange(4) == chip).astype(F32)[None, None, :, None], axis=2)
    loss = 0.5 * jnp.sum(small_sum[off + cw_size:]) / D_MODEL

    grads, delta, new_m, new_v = {}, {}, {}, {}
    for n in big_names:
        grads[n], delta[n], new_m[n], new_v[n] = (held(n, a) for a in (g_big[n], *(step[n] if n in step else updated(n))))
    small_names = [n for n, _ in _SMALL] + ["ssd_conv_w"]
    pack = lambda d: jnp.concatenate([d[n].reshape(-1) for n in small_names]).reshape(1, -1, LANES)
    ds, ms, vs = _adamw(pack(w_in), pack(g_small), pack(m_in), pack(v_in), "adamw_small")
    off = 0
    for n in small_names:
        shape = w_in[n].shape
        size = math.prod(shape)
        grads[n] = g_small[n]
        delta[n] = ds.reshape(-1)[off:off + size].reshape(shape)
        new_m[n] = ms.reshape(-1)[off:off + size].reshape(shape)
        new_v[n] = vs.reshape(-1)[off:off + size].reshape(shape)
        off += size

    order = ["norm_w", "ssd_in_w", "ssd_conv_w", "ssd_conv_b", "ssd_dt_bias", "ssd_a_log", "ssd_d", "ssd_gnorm_w", "ssd_out_w",
             "sb_in_w", "sb_qn_w", "sb_kn_w", "sb_out_w", "ple_norm_w", "ple_gate_w", "ple_proj_w"]
    return (loss, grad_x, *[grads[n] for n in order], *[delta[n] for n in order], *[new_m[n] for n in order],
            *[new_v[n] for n in order])
```
